```python
import jax
import jax.numpy as jnp
from jax import lax
import numpy as np

D_MODEL = 2048
BATCH = 8
SEQ = 8192
DEPTH = 1

CHUNK = 64
PLE_DIM = 256
D_FF = 5632
GDN_HEADS = 8
GDN_DK = 128
GDN_DV = 128
GDN_QKV = 2 * GDN_HEADS * GDN_DK + GDN_HEADS * GDN_DV
GDN_W = GDN_HEADS * GDN_DV
CONV_K = 4
ATT_HEADS = 8
ATT_DH = 128
ATT_W = ATT_HEADS * ATT_DH
LEFT_CHUNKS = 8
BAND = (LEFT_CHUNKS + 1) * CHUNK
MAX_REL = 128
N_REL = (CHUNK - 1) + MAX_REL + 1
EPS = 1e-6
NEG_INF = -1e30
IN_SPLITS = (GDN_QKV, GDN_W, GDN_HEADS, GDN_HEADS, 3 * ATT_W, D_MODEL, D_MODEL)
IN_COLS = sum(IN_SPLITS)

kernel_name = "hybrid_gdn_bandattn_macaron_block"


def rmsnorm(x, w, eps=EPS):
    xf = x.astype(jnp.float32)
    y = xf * lax.rsqrt(jnp.mean(xf * xf, axis=-1, keepdims=True) + eps)
    return (y * w.astype(jnp.float32)).astype(x.dtype)


def l2norm(x, eps=EPS):
    xf = x.astype(jnp.float32)
    return xf * lax.rsqrt(jnp.sum(xf * xf, axis=-1, keepdims=True) + eps)


def swiglu_ffn(h, w_gu, w_down):
    g, u = jnp.split(h @ w_gu, 2, axis=-1)
    return (jax.nn.silu(g) * u) @ w_down


def causal_short_conv(x, w):
    ksz = w.shape[0]
    seqlen = x.shape[1]
    xp = jnp.pad(x, ((0, 0), (ksz - 1, 0), (0, 0)))
    y = xp[:, 0:seqlen] * w[0]
    for j in range(1, ksz):
        y = y + xp[:, j:j + seqlen] * w[j]
    return y


def to_chunks(t):
    b, s, h, d = t.shape
    return t.reshape(b, s // CHUNK, CHUNK, h, d).transpose(0, 3, 1, 2, 4)


def to_chunks_h(t):
    b, s, h = t.shape
    return t.reshape(b, s // CHUNK, CHUNK, h).transpose(0, 3, 1, 2)


def gated_delta_rule_chunked(q, k, v, g, beta):
    c = q.shape[-2]
    dv = v.shape[-1]
    gc = jnp.cumsum(g, axis=-1)
    idx = jnp.arange(c)
    incl = idx[:, None] >= idx[None, :]
    strict = idx[:, None] > idx[None, :]
    diff = gc[..., :, None] - gc[..., None, :]
    decay = jnp.where(incl, jnp.exp(jnp.where(incl, diff, 0.0)), 0.0)
    kb = k * beta[..., None]
    lmat = jnp.where(strict, jnp.einsum('bhnid,bhnjd->bhnij', kb, k) * decay, 0.0)
    rhs = jnp.concatenate([v * beta[..., None], kb * jnp.exp(gc)[..., None]], axis=-1)
    sol = lax.linalg.triangular_solve(lmat + jnp.eye(c, dtype=lmat.dtype), rhs,
                                      left_side=True, lower=True, unit_diagonal=True)
    u, w = sol[..., :dv], sol[..., dv:]
    aqk = jnp.einsum('bhnid,bhnjd->bhnij', q, k) * decay
    q_dec = q * jnp.exp(gc)[..., None]
    k_tail = k * jnp.exp(gc[..., -1:] - gc)[..., None]
    tail = jnp.exp(gc[..., -1])

    def step(state, xs):
        u_c, w_c, aqk_c, qd_c, kt_c, tl_c = xs
        v_new = u_c - jnp.einsum('bhcd,bhde->bhce', w_c, state)
        o_c = (jnp.einsum('bhcd,bhde->bhce', qd_c, state)
               + jnp.einsum('bhcj,bhje->bhce', aqk_c, v_new))
        state = state * tl_c[..., None, None] + jnp.einsum('bhcd,bhce->bhde', kt_c, v_new)
        return state, o_c

    xs = tuple(jnp.moveaxis(t, 2, 0) for t in (u, w, aqk, q_dec, k_tail, tail))
    s0 = jnp.zeros(q.shape[:2] + (q.shape[-1], dv), jnp.float32)
    _, o = lax.scan(step, s0, xs)
    return jnp.moveaxis(o, 0, 2)


def gated_deltanet_branch(qkv, z, a_raw, b_raw, conv_w, a_log, dt_bias, norm_w):
    bsz, seqlen, _ = qkv.shape
    qkv = jax.nn.silu(causal_short_conv(qkv, conv_w))
    q, k, v = jnp.split(qkv, [GDN_HEADS * GDN_DK, 2 * GDN_HEADS * GDN_DK], axis=-1)
    q = l2norm(q.reshape(bsz, seqlen, GDN_HEADS, GDN_DK)) * (GDN_DK ** -0.5)
    k = l2norm(k.reshape(bsz, seqlen, GDN_HEADS, GDN_DK))
    v = v.reshape(bsz, seqlen, GDN_HEADS, GDN_DV).astype(jnp.float32)
    g = -jnp.exp(a_log.astype(jnp.float32)) * jax.nn.softplus(
        a_raw.astype(jnp.float32) + dt_bias.astype(jnp.float32))
    beta = jax.nn.sigmoid(b_raw.astype(jnp.float32))
    o = gated_delta_rule_chunked(to_chunks(q), to_chunks(k), to_chunks(v),
                                 to_chunks_h(g), to_chunks_h(beta))
    o = o.transpose(0, 2, 3, 1, 4).reshape(bsz, seqlen, GDN_HEADS, GDN_DV)
    zf = z.reshape(bsz, seqlen, GDN_HEADS, GDN_DV).astype(jnp.float32)
    o = rmsnorm(o, norm_w) * jax.nn.silu(zf)
    return o.reshape(bsz, seqlen, GDN_W).astype(z.dtype)


def gather_band(t):
    nc = t.shape[1]
    tp = jnp.pad(t, ((0, 0), (LEFT_CHUNKS, 0), (0, 0), (0, 0), (0, 0)))
    return jnp.concatenate([tp[:, j:j + nc] for j in range(LEFT_CHUNKS + 1)], axis=2)


def band_attention_branch(qkv, q_norm_w, k_norm_w, rel_bias):
    bsz, seqlen, _ = qkv.shape
    nc = seqlen // CHUNK
    q, k, v = jnp.split(qkv, 3, axis=-1)
    shp = (bsz, nc, CHUNK, ATT_HEADS, ATT_DH)
    q = rmsnorm(q.reshape(shp), q_norm_w)
    k_band = gather_band(rmsnorm(k.reshape(shp), k_norm_w))
    v_band = gather_band(v.reshape(shp))
    qpos = np.arange(CHUNK)[:, None]
    kpos = np.arange(BAND)[None, :] - LEFT_CHUNKS * CHUNK
    rel_idx = np.clip(qpos - kpos, -(CHUNK - 1), MAX_REL) + (CHUNK - 1)
    bias = rel_bias[:, rel_idx].astype(jnp.float32)
    valid = jnp.asarray((np.arange(nc)[:, None] - LEFT_CHUNKS
                         + np.arange(BAND)[None, :] // CHUNK) >= 0)
    s = jnp.einsum('bnqhd,bnkhd->bhnqk', q, k_band).astype(jnp.float32) * (ATT_DH ** -0.5)
    s = jnp.where(valid[:, None, :], s + bias[:, None], NEG_INF)
    pr = jax.nn.softmax(s, axis=-1).astype(v_band.dtype)
    o = jnp.einsum('bhnqk,bnkhd->bnqhd', pr, v_band)
    return o.reshape(bsz, seqlen, ATT_W)


def _fwd_setup_inputs(seed: int = 0) -> dict:
    key = jax.random.key(seed)
    ks = jax.random.split(key, 24)
    f32 = jnp.float32

    def dense(k, shape):
        return jax.random.normal(k, shape, f32) * (shape[-2] ** -0.5)

    def gain(k, n):
        return 1.0 + 0.05 * jax.random.normal(k, (DEPTH, n), f32)

    dt = jnp.exp(jax.random.uniform(ks[9], (DEPTH, GDN_HEADS), f32,
                                    float(np.log(1e-3)), float(np.log(1e-1))))
    dt_bias = dt + jnp.log(-jnp.expm1(-dt))
    return {
        'x': jax.random.normal(ks[0], (BATCH, SEQ, D_MODEL), f32),
        'p': jax.random.normal(ks[1], (DEPTH, BATCH, SEQ, PLE_DIM), f32),
        'ffn1_norm': gain(ks[2], D_MODEL),
        'ffn1_w_gu': dense(ks[3], (DEPTH, D_MODEL, 2 * D_FF)),
        'ffn1_w_down': dense(ks[4], (DEPTH, D_FF, D_MODEL)),
        'mix_norm': gain(ks[5], D_MODEL),
        'w_in': dense(ks[6], (DEPTH, D_MODEL, IN_COLS)),
        'conv_w': 0.5 * jax.random.normal(ks[7], (DEPTH, CONV_K, GDN_QKV), f32),
        'a_log': jnp.log(jax.random.uniform(ks[8], (DEPTH, GDN_HEADS), f32, 1.0, 16.0)),
        'dt_bias': dt_bias,
        'gdn_norm': gain(ks[10], GDN_DV),
        'q_norm': gain(ks[11], ATT_DH),
        'k_norm': gain(ks[12], ATT_DH),
        'rel_bias': 0.5 * jax.random.normal(ks[13], (DEPTH, ATT_HEADS, N_REL), f32),
        'w_branch_a': dense(ks[14], (DEPTH, GDN_W, D_MODEL)),
        'w_branch_b': dense(ks[15], (DEPTH, ATT_W, D_MODEL)),
        'w_out': dense(ks[16], (DEPTH, D_MODEL, D_MODEL)),
        'ffn2_norm': gain(ks[17], D_MODEL),
        'ffn2_w_gu': dense(ks[18], (DEPTH, D_MODEL, 2 * D_FF)),
        'ffn2_w_down': dense(ks[19], (DEPTH, D_FF, D_MODEL)),
        'ple_norm': gain(ks[20], D_MODEL),
        'ple_gate': dense(ks[21], (DEPTH, D_MODEL, D_MODEL)),
        'ple_proj': dense(ks[22], (DEPTH, PLE_DIM, D_MODEL)),
    }


def _fwd_reference(x, p, ffn1_norm, ffn1_w_gu, ffn1_w_down, mix_norm, w_in, conv_w, a_log,
              dt_bias, gdn_norm, q_norm, k_norm, rel_bias, w_branch_a, w_branch_b, w_out,
              ffn2_norm, ffn2_w_gu, ffn2_w_down, ple_norm, ple_gate, ple_proj):
    split_at = [int(s) for s in np.cumsum(IN_SPLITS)[:-1]]
    for i in range(DEPTH):
        x = x + 0.5 * swiglu_ffn(rmsnorm(x, ffn1_norm[i]), ffn1_w_gu[i], ffn1_w_down[i])
        h = rmsnorm(x, mix_norm[i])
        qkv_a, z_a, a_raw, b_raw, qkv_b, gate_a, gate_b = jnp.split(h @ w_in[i], split_at, axis=-1)
        ya = gated_deltanet_branch(qkv_a, z_a, a_raw, b_raw, conv_w[i], a_log[i],
                                   dt_bias[i], gdn_norm[i])
        yb = band_attention_branch(qkv_b, q_norm[i], k_norm[i], rel_bias[i])
        mixed = (jax.nn.sigmoid(gate_a) * (ya @ w_branch_a[i])
                 + jax.nn.sigmoid(gate_b) * (yb @ w_branch_b[i]))
        x = x + mixed @ w_out[i]
        x = x + 0.5 * swiglu_ffn(rmsnorm(x, ffn2_norm[i]), ffn2_w_gu[i], ffn2_w_down[i])
        x = x + jax.nn.sigmoid(rmsnorm(x, ple_norm[i]) @ ple_gate[i]) * (p[i] @ ple_proj[i])
    return x


import jax as _jax
import jax.numpy as _jnp

TWIN_FORMAT = 'train_step'
FWD_PARAMS = ['x', 'p', 'ffn1_norm', 'ffn1_w_gu', 'ffn1_w_down', 'mix_norm', 'w_in', 'conv_w', 'a_log', 'dt_bias', 'gdn_norm', 'q_norm', 'k_norm', 'rel_bias', 'w_branch_a', 'w_branch_b', 'w_out', 'ffn2_norm', 'ffn2_w_gu', 'ffn2_w_down', 'ple_norm', 'ple_gate', 'ple_proj']
TWIN_WEIGHTS = ['ffn1_norm', 'ffn1_w_gu', 'ffn1_w_down', 'mix_norm', 'w_in', 'conv_w', 'a_log', 'dt_bias', 'gdn_norm', 'q_norm', 'k_norm', 'rel_bias', 'w_branch_a', 'w_branch_b', 'w_out', 'ffn2_norm', 'ffn2_w_gu', 'ffn2_w_down', 'ple_norm', 'ple_gate', 'ple_proj']
TWIN_DIFF_INPUT = 'x'
TWIN_INPUTS = ['x', 'p', 'ffn1_norm', 'ffn1_w_gu', 'ffn1_w_down', 'mix_norm', 'w_in', 'conv_w', 'a_log', 'dt_bias', 'gdn_norm', 'q_norm', 'k_norm', 'rel_bias', 'w_branch_a', 'w_branch_b', 'w_out', 'ffn2_norm', 'ffn2_w_gu', 'ffn2_w_down', 'ple_norm', 'ple_gate', 'ple_proj', 'loss_target', 'm_ffn1_norm', 'm_ffn1_w_gu', 'm_ffn1_w_down', 'm_mix_norm', 'm_w_in', 'm_conv_w', 'm_a_log', 'm_dt_bias', 'm_gdn_norm', 'm_q_norm', 'm_k_norm', 'm_rel_bias', 'm_w_branch_a', 'm_w_branch_b', 'm_w_out', 'm_ffn2_norm', 'm_ffn2_w_gu', 'm_ffn2_w_down', 'm_ple_norm', 'm_ple_gate', 'm_ple_proj', 'v_ffn1_norm', 'v_ffn1_w_gu', 'v_ffn1_w_down', 'v_mix_norm', 'v_w_in', 'v_conv_w', 'v_a_log', 'v_dt_bias', 'v_gdn_norm', 'v_q_norm', 'v_k_norm', 'v_rel_bias', 'v_w_branch_a', 'v_w_branch_b', 'v_w_out', 'v_ffn2_norm', 'v_ffn2_w_gu', 'v_ffn2_w_down', 'v_ple_norm', 'v_ple_gate', 'v_ple_proj']
TWIN_OUTPUTS = ['loss', 'grad_x', 'grad_ffn1_norm', 'grad_ffn1_w_gu', 'grad_ffn1_w_down', 'grad_mix_norm', 'grad_w_in', 'grad_conv_w', 'grad_a_log', 'grad_dt_bias', 'grad_gdn_norm', 'grad_q_norm', 'grad_k_norm', 'grad_rel_bias', 'grad_w_branch_a', 'grad_w_branch_b', 'grad_w_out', 'grad_ffn2_norm', 'grad_ffn2_w_gu', 'grad_ffn2_w_down', 'grad_ple_norm', 'grad_ple_gate', 'grad_ple_proj', 'delta_ffn1_norm', 'delta_ffn1_w_gu', 'delta_ffn1_w_down', 'delta_mix_norm', 'delta_w_in', 'delta_conv_w', 'delta_a_log', 'delta_dt_bias', 'delta_gdn_norm', 'delta_q_norm', 'delta_k_norm', 'delta_rel_bias', 'delta_w_branch_a', 'delta_w_branch_b', 'delta_w_out', 'delta_ffn2_norm', 'delta_ffn2_w_gu', 'delta_ffn2_w_down', 'delta_ple_norm', 'delta_ple_gate', 'delta_ple_proj', 'new_m_ffn1_norm', 'new_m_ffn1_w_gu', 'new_m_ffn1_w_down', 'new_m_mix_norm', 'new_m_w_in', 'new_m_conv_w', 'new_m_a_log', 'new_m_dt_bias', 'new_m_gdn_norm', 'new_m_q_norm', 'new_m_k_norm', 'new_m_rel_bias', 'new_m_w_branch_a', 'new_m_w_branch_b', 'new_m_w_out', 'new_m_ffn2_norm', 'new_m_ffn2_w_gu', 'new_m_ffn2_w_down', 'new_m_ple_norm', 'new_m_ple_gate', 'new_m_ple_proj', 'new_v_ffn1_norm', 'new_v_ffn1_w_gu', 'new_v_ffn1_w_down', 'new_v_mix_norm', 'new_v_w_in', 'new_v_conv_w', 'new_v_a_log', 'new_v_dt_bias', 'new_v_gdn_norm', 'new_v_q_norm', 'new_v_k_norm', 'new_v_rel_bias', 'new_v_w_branch_a', 'new_v_w_branch_b', 'new_v_w_out', 'new_v_ffn2_norm', 'new_v_ffn2_w_gu', 'new_v_ffn2_w_down', 'new_v_ple_norm', 'new_v_ple_gate', 'new_v_ple_proj']
TWIN_LEAF_KINDS = {'loss': 'loss', 'grad_x': 'grad_x', 'grad_ffn1_norm': 'grad_w', 'grad_ffn1_w_gu': 'grad_w', 'grad_ffn1_w_down': 'grad_w', 'grad_mix_norm': 'grad_w', 'grad_w_in': 'grad_w', 'grad_conv_w': 'grad_w', 'grad_a_log': 'grad_w', 'grad_dt_bias': 'grad_w', 'grad_gdn_norm': 'grad_w', 'grad_q_norm': 'grad_w', 'grad_k_norm': 'grad_w', 'grad_rel_bias': 'grad_w', 'grad_w_branch_a': 'grad_w', 'grad_w_branch_b': 'grad_w', 'grad_w_out': 'grad_w', 'grad_ffn2_norm': 'grad_w', 'grad_ffn2_w_gu': 'grad_w', 'grad_ffn2_w_down': 'grad_w', 'grad_ple_norm': 'grad_w', 'grad_ple_gate': 'grad_w', 'grad_ple_proj': 'grad_w', 'delta_ffn1_norm': 'delta_w', 'delta_ffn1_w_gu': 'delta_w', 'delta_ffn1_w_down': 'delta_w', 'delta_mix_norm': 'delta_w', 'delta_w_in': 'delta_w', 'delta_conv_w': 'delta_w', 'delta_a_log': 'delta_w', 'delta_dt_bias': 'delta_w', 'delta_gdn_norm': 'delta_w', 'delta_q_norm': 'delta_w', 'delta_k_norm': 'delta_w', 'delta_rel_bias': 'delta_w', 'delta_w_branch_a': 'delta_w', 'delta_w_branch_b': 'delta_w', 'delta_w_out': 'delta_w', 'delta_ffn2_norm': 'delta_w', 'delta_ffn2_w_gu': 'delta_w', 'delta_ffn2_w_down': 'delta_w', 'delta_ple_norm': 'delta_w', 'delta_ple_gate': 'delta_w', 'delta_ple_proj': 'delta_w', 'new_m_ffn1_norm': 'new_m', 'new_m_ffn1_w_gu': 'new_m', 'new_m_ffn1_w_down': 'new_m', 'new_m_mix_norm': 'new_m', 'new_m_w_in': 'new_m', 'new_m_conv_w': 'new_m', 'new_m_a_log': 'new_m', 'new_m_dt_bias': 'new_m', 'new_m_gdn_norm': 'new_m', 'new_m_q_norm': 'new_m', 'new_m_k_norm': 'new_m', 'new_m_rel_bias': 'new_m', 'new_m_w_branch_a': 'new_m', 'new_m_w_branch_b': 'new_m', 'new_m_w_out': 'new_m', 'new_m_ffn2_norm': 'new_m', 'new_m_ffn2_w_gu': 'new_m', 'new_m_ffn2_w_down': 'new_m', 'new_m_ple_norm': 'new_m', 'new_m_ple_gate': 'new_m', 'new_m_ple_proj': 'new_m', 'new_v_ffn1_norm': 'new_v', 'new_v_ffn1_w_gu': 'new_v', 'new_v_ffn1_w_down': 'new_v', 'new_v_mix_norm': 'new_v', 'new_v_w_in': 'new_v', 'new_v_conv_w': 'new_v', 'new_v_a_log': 'new_v', 'new_v_dt_bias': 'new_v', 'new_v_gdn_norm': 'new_v', 'new_v_q_norm': 'new_v', 'new_v_k_norm': 'new_v', 'new_v_rel_bias': 'new_v', 'new_v_w_branch_a': 'new_v', 'new_v_w_branch_b': 'new_v', 'new_v_w_out': 'new_v', 'new_v_ffn2_norm': 'new_v', 'new_v_ffn2_w_gu': 'new_v', 'new_v_ffn2_w_down': 'new_v', 'new_v_ple_norm': 'new_v', 'new_v_ple_gate': 'new_v', 'new_v_ple_proj': 'new_v'}


def _forward(args):
    return _fwd_reference(*[args[k] for k in FWD_PARAMS])


def _output_shape():
    def fwd():
        inp = _fwd_setup_inputs(0)
        return _fwd_reference(*[inp[k] for k in FWD_PARAMS])
    out = _jax.eval_shape(fwd)
    return out.shape, out.dtype

N_MICROBATCH = 1
ADAM_LR = 0.001
ADAM_B1 = 0.9
ADAM_B2 = 0.999
ADAM_EPS = 1e-08
ADAM_WD = 0.01
ADAM_STEP = 10
PER_EXAMPLE_BATCH_AXIS = {'x': 0, 'p': 1, 'loss_target': 0}
SHARED_INPUTS = []
_WEIGHT_DTYPES = {'ffn1_norm': _jnp.float32, 'ffn1_w_gu': _jnp.float32, 'ffn1_w_down': _jnp.float32, 'mix_norm': _jnp.float32, 'w_in': _jnp.float32, 'conv_w': _jnp.float32, 'a_log': _jnp.float32, 'dt_bias': _jnp.float32, 'gdn_norm': _jnp.float32, 'q_norm': _jnp.float32, 'k_norm': _jnp.float32, 'rel_bias': _jnp.float32, 'w_branch_a': _jnp.float32, 'w_branch_b': _jnp.float32, 'w_out': _jnp.float32, 'ffn2_norm': _jnp.float32, 'ffn2_w_gu': _jnp.float32, 'ffn2_w_down': _jnp.float32, 'ple_norm': _jnp.float32, 'ple_gate': _jnp.float32, 'ple_proj': _jnp.float32}
MOMENT_SCALE = {'ffn1_norm': 6.217399e+00, 'ffn1_w_gu': 5.809703e-02, 'ffn1_w_down': 9.770469e-02, 'mix_norm': 4.131252e+00, 'w_in': 8.915678e-02, 'conv_w': 2.848217e-01, 'a_log': 2.174735e+01, 'dt_bias': 2.042750e+01, 'gdn_norm': 5.394995e+01, 'q_norm': 5.654100e-01, 'k_norm': 5.633387e-01, 'rel_bias': 1.002919e-01, 'w_branch_a': 2.609546e-01, 'w_branch_b': 2.293734e-02, 'w_out': 2.095727e-01, 'ffn2_norm': 6.224718e+00, 'ffn2_w_gu': 5.075548e-02, 'ffn2_w_down': 8.190175e-02, 'ple_norm': 9.893068e-01, 'ple_gate': 1.336441e-01, 'ple_proj': 4.530060e-01}


def _to_microbatches(a, axis):
    t = _jnp.moveaxis(a, axis, 0)
    t = t.reshape((N_MICROBATCH, t.shape[0] // N_MICROBATCH) + t.shape[1:])
    return _jnp.moveaxis(t, 1, axis + 1)


def setup_inputs(seed: int = 0) -> dict:
    inp = _fwd_setup_inputs(seed)
    key = _jax.random.fold_in(_jax.random.key(seed), 7919)
    shape, _ = _output_shape()
    out = dict(inp)
    out["loss_target"] = _jax.random.normal(_jax.random.fold_in(key, 0), shape, _jnp.float32)
    for i, name in enumerate(TWIN_WEIGHTS):
        w = inp[name].astype(_jnp.float32)
        if MOMENT_SCALE is None:
            s = _jnp.sqrt(_jnp.mean(_jnp.square(w)) + 1e-30)
        else:
            s = MOMENT_SCALE[name]
        km, kv = _jax.random.split(_jax.random.fold_in(key, i + 1))
        out[name] = w
        out["m_" + name] = s * _jax.random.normal(km, w.shape, _jnp.float32)
        out["v_" + name] = (s * s) * _jax.random.uniform(kv, w.shape, _jnp.float32, 0.5, 1.5)
    if N_MICROBATCH > 1:
        for name, axis in PER_EXAMPLE_BATCH_AXIS.items():
            out[name] = _to_microbatches(out[name], axis)
    return {'x': out['x'], 'p': out['p'], 'ffn1_norm': out['ffn1_norm'], 'ffn1_w_gu': out['ffn1_w_gu'], 'ffn1_w_down': out['ffn1_w_down'], 'mix_norm': out['mix_norm'], 'w_in': out['w_in'], 'conv_w': out['conv_w'], 'a_log': out['a_log'], 'dt_bias': out['dt_bias'], 'gdn_norm': out['gdn_norm'], 'q_norm': out['q_norm'], 'k_norm': out['k_norm'], 'rel_bias': out['rel_bias'], 'w_branch_a': out['w_branch_a'], 'w_branch_b': out['w_branch_b'], 'w_out': out['w_out'], 'ffn2_norm': out['ffn2_norm'], 'ffn2_w_gu': out['ffn2_w_gu'], 'ffn2_w_down': out['ffn2_w_down'], 'ple_norm': out['ple_norm'], 'ple_gate': out['ple_gate'], 'ple_proj': out['ple_proj'], 'loss_target': out['loss_target'], 'm_ffn1_norm': out['m_ffn1_norm'], 'm_ffn1_w_gu': out['m_ffn1_w_gu'], 'm_ffn1_w_down': out['m_ffn1_w_down'], 'm_mix_norm': out['m_mix_norm'], 'm_w_in': out['m_w_in'], 'm_conv_w': out['m_conv_w'], 'm_a_log': out['m_a_log'], 'm_dt_bias': out['m_dt_bias'], 'm_gdn_norm': out['m_gdn_norm'], 'm_q_norm': out['m_q_norm'], 'm_k_norm': out['m_k_norm'], 'm_rel_bias': out['m_rel_bias'], 'm_w_branch_a': out['m_w_branch_a'], 'm_w_branch_b': out['m_w_branch_b'], 'm_w_out': out['m_w_out'], 'm_ffn2_norm': out['m_ffn2_norm'], 'm_ffn2_w_gu': out['m_ffn2_w_gu'], 'm_ffn2_w_down': out['m_ffn2_w_down'], 'm_ple_norm': out['m_ple_norm'], 'm_ple_gate': out['m_ple_gate'], 'm_ple_proj': out['m_ple_proj'], 'v_ffn1_norm': out['v_ffn1_norm'], 'v_ffn1_w_gu': out['v_ffn1_w_gu'], 'v_ffn1_w_down': out['v_ffn1_w_down'], 'v_mix_norm': out['v_mix_norm'], 'v_w_in': out['v_w_in'], 'v_conv_w': out['v_conv_w'], 'v_a_log': out['v_a_log'], 'v_dt_bias': out['v_dt_bias'], 'v_gdn_norm': out['v_gdn_norm'], 'v_q_norm': out['v_q_norm'], 'v_k_norm': out['v_k_norm'], 'v_rel_bias': out['v_rel_bias'], 'v_w_branch_a': out['v_w_branch_a'], 'v_w_branch_b': out['v_w_branch_b'], 'v_w_out': out['v_w_out'], 'v_ffn2_norm': out['v_ffn2_norm'], 'v_ffn2_w_gu': out['v_ffn2_w_gu'], 'v_ffn2_w_down': out['v_ffn2_w_down'], 'v_ple_norm': out['v_ple_norm'], 'v_ple_gate': out['v_ple_gate'], 'v_ple_proj': out['v_ple_proj']}


def _loss(weights, diff, rest, loss_target):
    with _jax.named_scope("forward"):
        args = {**rest, TWIN_DIFF_INPUT: diff, **{k: w.astype(_WEIGHT_DTYPES[k]) for k, w in weights.items()}}
        y = _forward(args)
    with _jax.named_scope("loss_head"):
        err = _jnp.square(y.astype(_jnp.float32) - loss_target)
        return 0.5 * _jnp.sum(_jnp.mean(err, axis=-1)) if err.ndim else 0.5 * err


def _adamw(w, g, m, v):
    m = ADAM_B1 * m + (1.0 - ADAM_B1) * g
    v = ADAM_B2 * v + (1.0 - ADAM_B2) * _jnp.square(g)
    m_hat = m / (1.0 - ADAM_B1 ** ADAM_STEP)
    v_hat = v / (1.0 - ADAM_B2 ** ADAM_STEP)
    delta = -ADAM_LR * (m_hat / (_jnp.sqrt(v_hat) + ADAM_EPS) + ADAM_WD * w)
    return delta, m, v


def reference(x, p, ffn1_norm, ffn1_w_gu, ffn1_w_down, mix_norm, w_in, conv_w, a_log, dt_bias, gdn_norm, q_norm, k_norm, rel_bias, w_branch_a, w_branch_b, w_out, ffn2_norm, ffn2_w_gu, ffn2_w_down, ple_norm, ple_gate, ple_proj, loss_target, m_ffn1_norm, m_ffn1_w_gu, m_ffn1_w_down, m_mix_norm, m_w_in, m_conv_w, m_a_log, m_dt_bias, m_gdn_norm, m_q_norm, m_k_norm, m_rel_bias, m_w_branch_a, m_w_branch_b, m_w_out, m_ffn2_norm, m_ffn2_w_gu, m_ffn2_w_down, m_ple_norm, m_ple_gate, m_ple_proj, v_ffn1_norm, v_ffn1_w_gu, v_ffn1_w_down, v_mix_norm, v_w_in, v_conv_w, v_a_log, v_dt_bias, v_gdn_norm, v_q_norm, v_k_norm, v_rel_bias, v_w_branch_a, v_w_branch_b, v_w_out, v_ffn2_norm, v_ffn2_w_gu, v_ffn2_w_down, v_ple_norm, v_ple_gate, v_ple_proj):
    given = dict(x=x, p=p, ffn1_norm=ffn1_norm, ffn1_w_gu=ffn1_w_gu, ffn1_w_down=ffn1_w_down, mix_norm=mix_norm, w_in=w_in, conv_w=conv_w, a_log=a_log, dt_bias=dt_bias, gdn_norm=gdn_norm, q_norm=q_norm, k_norm=k_norm, rel_bias=rel_bias, w_branch_a=w_branch_a, w_branch_b=w_branch_b, w_out=w_out, ffn2_norm=ffn2_norm, ffn2_w_gu=ffn2_w_gu, ffn2_w_down=ffn2_w_down, ple_norm=ple_norm, ple_gate=ple_gate, ple_proj=ple_proj, loss_target=loss_target, m_ffn1_norm=m_ffn1_norm, m_ffn1_w_gu=m_ffn1_w_gu, m_ffn1_w_down=m_ffn1_w_down, m_mix_norm=m_mix_norm, m_w_in=m_w_in, m_conv_w=m_conv_w, m_a_log=m_a_log, m_dt_bias=m_dt_bias, m_gdn_norm=m_gdn_norm, m_q_norm=m_q_norm, m_k_norm=m_k_norm, m_rel_bias=m_rel_bias, m_w_branch_a=m_w_branch_a, m_w_branch_b=m_w_branch_b, m_w_out=m_w_out, m_ffn2_norm=m_ffn2_norm, m_ffn2_w_gu=m_ffn2_w_gu, m_ffn2_w_down=m_ffn2_w_down, m_ple_norm=m_ple_norm, m_ple_gate=m_ple_gate, m_ple_proj=m_ple_proj, v_ffn1_norm=v_ffn1_norm, v_ffn1_w_gu=v_ffn1_w_gu, v_ffn1_w_down=v_ffn1_w_down, v_mix_norm=v_mix_norm, v_w_in=v_w_in, v_conv_w=v_conv_w, v_a_log=v_a_log, v_dt_bias=v_dt_bias, v_gdn_norm=v_gdn_norm, v_q_norm=v_q_norm, v_k_norm=v_k_norm, v_rel_bias=v_rel_bias, v_w_branch_a=v_w_branch_a, v_w_branch_b=v_w_branch_b, v_w_out=v_w_out, v_ffn2_norm=v_ffn2_norm, v_ffn2_w_gu=v_ffn2_w_gu, v_ffn2_w_down=v_ffn2_w_down, v_ple_norm=v_ple_norm, v_ple_gate=v_ple_gate, v_ple_proj=v_ple_proj)
    weights = {n: given[n] for n in TWIN_WEIGHTS}
    shared = {n: given[n] for n in SHARED_INPUTS}
    per_example = {n: given[n] for n in ['x', 'p']}
    grad_fn = _jax.value_and_grad(_loss, argnums=(0, 1))

    def one_microbatch(ex, loss_target):
        ex = dict(ex)
        diff = ex.pop(TWIN_DIFF_INPUT)
        return grad_fn(weights, diff, {**shared, **ex}, loss_target)

    if N_MICROBATCH == 1:
        loss, (grad_w, grad_x) = one_microbatch(per_example, given["loss_target"])
    else:
        def body(carry, xs):
            loss_sum, grad_sum = carry
            l_k, (gw_k, gx_k) = one_microbatch(xs[0], xs[1])
            with _jax.named_scope("update"):
                return (loss_sum + l_k, _jax.tree.map(_jnp.add, grad_sum, gw_k)), gx_k

        init = (_jnp.zeros((), _jnp.float32), _jax.tree.map(_jnp.zeros_like, weights))
        (loss, grad_w), grad_x = _jax.lax.scan(body, init, (per_example, given["loss_target"]))
    with _jax.named_scope("update"):
        delta_w, new_m, new_v = {}, {}, {}
        for n in TWIN_WEIGHTS:
            delta_w[n], new_m[n], new_v[n] = _adamw(weights[n], grad_w[n], given["m_" + n], given["v_" + n])
    return (loss, grad_x, *[grad_w[n] for n in TWIN_WEIGHTS], *[delta_w[n] for n in TWIN_WEIGHTS],
            *[new_m[n] for n in TWIN_WEIGHTS], *[new_v[n] for n in TWIN_WEIGHTS])
```

```python
import functools

import numpy as np
import jax
import jax.numpy as jnp
from jax import lax
from jax.experimental import pallas as pl
from jax.experimental.pallas import tpu as pltpu

F32 = jnp.float32
BF16 = jnp.bfloat16
HIGHEST = lax.Precision.HIGHEST
MESH = pl.DeviceIdType.MESH

D_MODEL = 2048
D_FF = 5632
HEADS = 8
HEAD_DIM = 128
HW = HEADS * HEAD_DIM
CHUNK = 64
LEFT_CHUNKS = 8
MAX_REL = 128
N_REL = (CHUNK - 1) + MAX_REL + 1
CONV_K = 4
EPS = 1e-6
NEG_INF = -1e30
N_DEV = 8
LANES = 128
SUBLANES = 8
VMEM_LIMIT = 56 * 1024 * 1024

ATT_QB = 256
ATT_KW = ATT_QB + LEFT_CHUNKS * CHUNK
ATT_PAD = LEFT_CHUNKS * CHUNK
GDN_CB = 8

ADAM_LR = 0.001
ADAM_B1 = 0.9
ADAM_B2 = 0.999
ADAM_EPS = 1e-08
ADAM_WD = 0.01
ADAM_STEP = 10

IN_QZ = 3 * HW + HW
IN_AB0 = IN_QZ
IN_QKVB0 = IN_AB0 + 2 * HEADS
IN_GG0 = IN_QKVB0 + 3 * HW
IN_COLS = IN_GG0 + 2 * D_MODEL

FLAT_W = 1024
FLAT_TILE = 1024

BIG = ("ffn1_w_gu", "ffn1_w_down", "w_in", "w_branch_a", "w_branch_b", "w_out",
       "ffn2_w_gu", "ffn2_w_down", "ple_gate", "ple_proj")
COL_SHARDED = ("ffn1_w_gu", "w_in", "w_branch_a", "w_branch_b", "ffn2_w_gu", "ple_proj")


def _params(semantics=None, **kw):
    return pltpu.CompilerParams(dimension_semantics=semantics, vmem_limit_bytes=VMEM_LIMIT, **kw)


def _pick(n, cands):
    for c in cands:
        if n % c == 0:
            return c
    return n


def _matmul(a, b, mode, out_dtype, name):
    if mode == "nn":
        (m, k), (k2, n) = a.shape, b.shape
    elif mode == "nt":
        (m, k), (n, k2) = a.shape, b.shape
    else:
        (k, m), (k2, n) = a.shape, b.shape
    assert k == k2, (a.shape, b.shape, mode)
    tm = _pick(m, (1024, 512, 256, 128))
    tn = _pick(n, (1024, 512, 256, 128))
    tk = _pick(k, (1024, 512, 256, 128))
    nk = k // tk
    if mode == "nn":
        a_spec = pl.BlockSpec((tm, tk), lambda i, j, kk: (i, kk))
        b_spec = pl.BlockSpec((tk, tn), lambda i, j, kk: (kk, j))
        dims = (((1,), (0,)), ((), ()))
    elif mode == "nt":
        a_spec = pl.BlockSpec((tm, tk), lambda i, j, kk: (i, kk))
        b_spec = pl.BlockSpec((tn, tk), lambda i, j, kk: (j, kk))
        dims = (((1,), (1,)), ((), ()))
    else:
        a_spec = pl.BlockSpec((tk, tm), lambda i, j, kk: (kk, i))
        b_spec = pl.BlockSpec((tk, tn), lambda i, j, kk: (kk, j))
        dims = (((0,), (0,)), ((), ()))

    def body(a_ref, b_ref, o_ref, acc_ref):
        kk = pl.program_id(2)

        @pl.when(kk == 0)
        def _():
            acc_ref[...] = jnp.zeros_like(acc_ref)

        acc_ref[...] += lax.dot_general(a_ref[...].astype(BF16), b_ref[...].astype(BF16), dims,
                                        preferred_element_type=F32)

        @pl.when(kk == nk - 1)
        def _():
            o_ref[...] = acc_ref[...].astype(o_ref.dtype)

    return pl.pallas_call(
        body, name=name,
        out_shape=jax.ShapeDtypeStruct((m, n), out_dtype),
        grid=(m // tm, n // tn, nk),
        in_specs=[a_spec, b_spec],
        out_specs=pl.BlockSpec((tm, tn), lambda i, j, kk: (i, j)),
        scratch_shapes=[pltpu.VMEM((tm, tn), F32)],
        compiler_params=_params(("parallel", "parallel", "arbitrary")),
    )(a, b)


def _rows(fn, row_ins, consts, row_outs, acc_outs, tile, name):
    t_rows = row_ins[0][0].shape[0]
    tile = min(tile, t_rows)
    assert t_rows % tile == 0 and tile % SUBLANES == 0
    n = t_rows // tile
    per = tile // SUBLANES
    last8 = t_rows // SUBLANES - 1
    in_specs = []
    for arr, kind in row_ins:
        c = arr.shape[1]
        if kind == "t":
            in_specs.append(pl.BlockSpec((tile, c), lambda i: (i, 0)))
        elif kind == "p":
            in_specs.append(pl.BlockSpec((SUBLANES, c), lambda i: (jnp.maximum(i * per - 1, 0), 0)))
        else:
            in_specs.append(pl.BlockSpec((SUBLANES, c), lambda i: (jnp.minimum((i + 1) * per, last8), 0)))
    for arr in consts:
        in_specs.append(pl.BlockSpec(arr.shape, lambda i, nd=arr.ndim: (0,) * nd))
    out_shape = [jax.ShapeDtypeStruct((t_rows, c), dt) for c, dt in row_outs]
    out_specs = [pl.BlockSpec((tile, c), lambda i: (i, 0)) for c, _ in row_outs]
    for shp in acc_outs:
        out_shape.append(jax.ShapeDtypeStruct(shp, F32))
        out_specs.append(pl.BlockSpec(shp, lambda i, nd=len(shp): (0,) * nd))
    n_in = len(row_ins) + len(consts)
    n_row_out = len(row_outs)

    def body(*refs):
        i = pl.program_id(0)
        vals = [r[...] for r in refs[:len(row_ins)]]
        res = fn(i, n, *vals, *refs[len(row_ins):n_in])
        outs = refs[n_in:]
        for r, v in zip(outs[:n_row_out], res[:n_row_out]):
            r[...] = v.astype(r.dtype)
        if acc_outs:
            @pl.when(i == 0)
            def _():
                for r in outs[n_row_out:]:
                    r[...] = jnp.zeros_like(r)

            for r, v in zip(outs[n_row_out:], res[n_row_out:]):
                r[...] += v

    res = pl.pallas_call(
        body, name=name, out_shape=out_shape, grid=(n,), in_specs=in_specs, out_specs=out_specs,
        compiler_params=_params(("arbitrary",) if acc_outs else ("parallel",)),
    )(*[a for a, _ in row_ins], *consts)
    return res


def _rms(x, w):
    return x * lax.rsqrt(jnp.mean(x * x, axis=-1, keepdims=True) + EPS) * w


def _l2n(x):
    return x * lax.rsqrt(jnp.sum(x * x, axis=-1, keepdims=True) + EPS)


def _sigmoid(x):
    return 1.0 / (1.0 + jnp.exp(-x))


def _silu(x):
    return x * _sigmoid(x)


def _softplus(x):
    return jnp.maximum(x, 0.0) + jnp.log(1.0 + jnp.exp(-jnp.abs(x)))


def _heads(fn, *xs):
    nh = xs[0].shape[1] // HEAD_DIM
    return jnp.concatenate(
        [fn(*[x[:, h * HEAD_DIM:(h + 1) * HEAD_DIM] for x in xs]) for h in range(nh)], axis=1)


def _colsum(x):
    return jnp.sum(x, axis=0, keepdims=True)


def _swiglu(gu):
    return _silu(gu[:, :D_FF]) * gu[:, D_FF:]


def _gated_norm(o, z, w):
    return _heads(lambda oh, zh: _rms(oh, w) * _silu(zh), o, z)


def _mix(gg, ta, tb):
    return _sigmoid(gg[:, :D_MODEL]) * ta + _sigmoid(gg[:, D_MODEL:]) * tb


def _gdn_post(y):
    a = _silu(y)
    q = _heads(lambda v: _l2n(v) * (HEAD_DIM ** -0.5), a[:, :HW])
    k = _heads(_l2n, a[:, HW:2 * HW])
    return q, k, a[:, 2 * HW:]


def _gdn_gates(ab, alog, dtb, e_g, e_b):
    g = -jnp.exp(alog) * _softplus(ab + dtb)
    beta = _sigmoid(ab)
    g_b = jnp.dot(g, e_g, precision=HIGHEST, preferred_element_type=F32)
    beta_b = jnp.dot(beta, e_b, precision=HIGHEST, preferred_element_type=F32)
    return g_b, beta_b


def _shift_down(x, halo, s, i):
    if s == 0:
        return x
    halo = jnp.where(i == 0, 0.0, halo)
    xr = pltpu.roll(x, s, 0)
    hr = pltpu.roll(halo, s, 0)
    row = lax.broadcasted_iota(jnp.int32, (SUBLANES, x.shape[1]), 0)
    top = jnp.where(row < s, hr, xr[:SUBLANES])
    return jnp.concatenate([top, xr[SUBLANES:]], axis=0)


def _shift_up(x, halo, s, i, n):
    if s == 0:
        return x
    t = x.shape[0]
    halo = jnp.where(i == n - 1, 0.0, halo)
    xr = pltpu.roll(x, t - s, 0)
    hr = pltpu.roll(halo, SUBLANES - s, 0)
    row = lax.broadcasted_iota(jnp.int32, (SUBLANES, x.shape[1]), 0)
    bot = jnp.where(row >= SUBLANES - s, hr, xr[t - SUBLANES:])
    return jnp.concatenate([xr[:t - SUBLANES], bot], axis=0)


def _conv(pa, prev, cw_ref, i):
    y = pa * cw_ref[CONV_K - 1:CONV_K, :]
    for j in range(CONV_K - 1):
        y = y + _shift_down(pa, prev, CONV_K - 1 - j, i) * cw_ref[j:j + 1, :]
    return y


def _dot_nt(a, b, precision=None):
    return lax.dot_general(a, b, (((1,), (1,)), ((), ())), precision=precision, preferred_element_type=F32)


def _dot_tn(a, b, precision=None):
    return lax.dot_general(a, b, (((0,), (0,)), ((), ())), precision=precision, preferred_element_type=F32)


def _dot(a, b, precision=None):
    return jnp.dot(a, b, precision=precision, preferred_element_type=F32)


def _bf(x):
    return x.astype(BF16)


def _gdn_chunk(q, k, v, gb, bb):
    c = q.shape[0]
    ri = lax.broadcasted_iota(jnp.int32, (c, c), 0)
    ci = lax.broadcasted_iota(jnp.int32, (c, c), 1)
    incl = ri >= ci
    strict = ri > ci
    tril = jnp.where(incl, 1.0, 0.0).astype(F32)
    gc = _dot(tril, gb, HIGHEST)
    avg = jnp.full((c, LANES), 1.0 / LANES, F32)
    g_row = _dot_nt(gc, avg, HIGHEST)
    g_col = _dot_nt(avg, gc, HIGHEST)
    decay = jnp.where(incl, jnp.exp(jnp.where(incl, g_row - g_col, 0.0)), 0.0)
    kb = k * bb
    lmat = jnp.where(strict, _dot_nt(_bf(kb), _bf(k)) * decay, 0.0)
    eye = jnp.where(ri == ci, 1.0, 0.0).astype(F32)
    pw = -lmat
    inv = eye + pw
    for _ in range(5):
        pw = _dot(pw, pw, HIGHEST)
        inv = inv + _dot(inv, pw, HIGHEST)
    egc = jnp.exp(gc)
    u = _dot(inv, v * bb, HIGHEST)
    w = _dot(inv, kb * egc, HIGHEST)
    aqk = _dot_nt(_bf(q), _bf(k)) * decay
    tot = _colsum(gb)
    k_tail = k * jnp.exp(tot - gc)
    tail = jnp.broadcast_to(jnp.exp(tot), (SUBLANES, LANES))
    return u, w, aqk, q * egc, k_tail, tail


def _gdn_intra(qn, kn, vv, g_b, beta_b):
    t_rows = qn.shape[0]
    nc = t_rows // CHUNK
    cb = min(GDN_CB, nc)
    rows = cb * CHUNK
    col = pl.BlockSpec((rows, HEAD_DIM), lambda h, b: (b, h))

    def body(q_ref, k_ref, v_ref, g_ref, b_ref, u_ref, w_ref, a_ref, qd_ref, kt_ref, tl_ref):
        def chunk(ci, carry):
            r = pl.ds(pl.multiple_of(ci * CHUNK, CHUNK), CHUNK)
            u, w, aqk, qd, kt, tl = _gdn_chunk(q_ref[r, :], k_ref[r, :], v_ref[r, :], g_ref[r, :], b_ref[r, :])
            u_ref[r, :] = u
            w_ref[r, :] = w
            a_ref[0, r, :] = aqk
            qd_ref[r, :] = qd
            kt_ref[r, :] = kt
            tl_ref[0, ci] = tl
            return carry

        lax.fori_loop(0, cb, chunk, 0)

    full = jax.ShapeDtypeStruct((t_rows, HW), F32)
    return pl.pallas_call(
        body, name="gdn_intra_fwd",
        out_shape=[full, full, jax.ShapeDtypeStruct((HEADS, t_rows, CHUNK), F32), full, full,
                   jax.ShapeDtypeStruct((HEADS, nc, SUBLANES, LANES), F32)],
        grid=(HEADS, nc // cb),
        in_specs=[col] * 5,
        out_specs=[col, col, pl.BlockSpec((1, rows, CHUNK), lambda h, b: (h, b, 0)), col, col,
                   pl.BlockSpec((1, cb, SUBLANES, LANES), lambda h, b: (h, b, 0, 0))],
        compiler_params=_params(("parallel", "parallel")),
    )(qn, kn, vv, g_b, beta_b)


def _gdn_intra_bwd(qn, kn, vv, g_b, beta_b, du, dw, da, dqd, dkt, dtl):
    t_rows = qn.shape[0]
    nc = t_rows // CHUNK
    cb = min(GDN_CB, nc)
    rows = cb * CHUNK
    col = pl.BlockSpec((rows, HEAD_DIM), lambda h, b: (b, h))
    a_spec = pl.BlockSpec((1, rows, CHUNK), lambda h, b: (h, b, 0))
    tl_spec = pl.BlockSpec((1, cb, SUBLANES, LANES), lambda h, b: (h, b, 0, 0))

    def body(q_ref, k_ref, v_ref, g_ref, b_ref, du_ref, dw_ref, da_ref, dqd_ref, dkt_ref, dtl_ref,
             dq_ref, dk_ref, dv_ref, dg_ref, db_ref):
        def chunk(ci, carry):
            r = pl.ds(pl.multiple_of(ci * CHUNK, CHUNK), CHUNK)
            _, vjp = jax.vjp(_gdn_chunk, q_ref[r, :], k_ref[r, :], v_ref[r, :], g_ref[r, :], b_ref[r, :])
            dq, dk, dv, dg, db = vjp((du_ref[r, :], dw_ref[r, :], da_ref[0, r, :], dqd_ref[r, :], dkt_ref[r, :],
                                      dtl_ref[0, ci]))
            dq_ref[r, :] = dq
            dk_ref[r, :] = dk
            dv_ref[r, :] = dv
            dg_ref[r, :] = dg
            db_ref[r, :] = db
            return carry

        lax.fori_loop(0, cb, chunk, 0)

    full = jax.ShapeDtypeStruct((t_rows, HW), F32)
    return pl.pallas_call(
        body, name="gdn_intra_bwd",
        out_shape=[full] * 5,
        grid=(HEADS, nc // cb),
        in_specs=[col] * 7 + [a_spec, col, col, tl_spec],
        out_specs=[col] * 5,
        compiler_params=_params(("parallel", "parallel")),
    )(qn, kn, vv, g_b, beta_b, du, dw, da, dqd, dkt, dtl)


def _head_cols(h):
    return slice(h * HEAD_DIM, (h + 1) * HEAD_DIM)


def _gdn_scan(u, w, aqk, qd, kt, tl):
    t_rows = u.shape[0]
    nc = t_rows // CHUNK
    cb = min(GDN_CB, nc)
    rows = cb * CHUNK
    wide = pl.BlockSpec((rows, HW), lambda b: (b, 0))

    def body(u_ref, w_ref, a_ref, qd_ref, kt_ref, tl_ref, o_ref, s_out_ref, s_ref):
        @pl.when(pl.program_id(0) == 0)
        def _():
            s_ref[...] = jnp.zeros_like(s_ref)

        def chunk(ci, carry):
            r = pl.ds(pl.multiple_of(ci * CHUNK, CHUNK), CHUNK)
            for h in range(HEADS):
                hc = _head_cols(h)
                s = s_ref[h]
                s_out_ref[ci, h] = s
                sb = _bf(s)
                vn = u_ref[r, hc] - _dot(_bf(w_ref[r, hc]), sb)
                vnb = _bf(vn)
                o_ref[r, hc] = _dot(_bf(qd_ref[r, hc]), sb) + _dot(_bf(a_ref[h, r, :]), vnb)
                s_ref[h] = s * tl_ref[h, ci, 0:1, :] + _dot_tn(_bf(kt_ref[r, hc]), vnb)
            return carry

        lax.fori_loop(0, cb, chunk, 0)

    return pl.pallas_call(
        body, name="gdn_scan_fwd",
        out_shape=[jax.ShapeDtypeStruct((t_rows, HW), F32),
                   jax.ShapeDtypeStruct((nc, HEADS, HEAD_DIM, HEAD_DIM), F32)],
        grid=(nc // cb,),
        in_specs=[wide, wide, pl.BlockSpec((HEADS, rows, CHUNK), lambda b: (0, b, 0)), wide, wide,
                  pl.BlockSpec((HEADS, cb, SUBLANES, LANES), lambda b: (0, b, 0, 0))],
        out_specs=[wide, pl.BlockSpec((cb, HEADS, HEAD_DIM, HEAD_DIM), lambda b: (b, 0, 0, 0))],
        scratch_shapes=[pltpu.VMEM((HEADS, HEAD_DIM, HEAD_DIM), F32)],
        compiler_params=_params(("arbitrary",)),
    )(u, w, aqk, qd, kt, tl)


def _gdn_scan_bwd(do, u, w, aqk, qd, kt, tl, states):
    t_rows = u.shape[0]
    nc = t_rows // CHUNK
    cb = min(GDN_CB, nc)
    rows = cb * CHUNK
    nb = nc // cb
    wide = pl.BlockSpec((rows, HW), lambda b: (nb - 1 - b, 0))
    a_spec = pl.BlockSpec((HEADS, rows, CHUNK), lambda b: (0, nb - 1 - b, 0))
    tl_spec = pl.BlockSpec((HEADS, cb, SUBLANES, LANES), lambda b: (0, nb - 1 - b, 0, 0))

    def body(do_ref, u_ref, w_ref, a_ref, qd_ref, kt_ref, tl_ref, s_in_ref,
             du_ref, dw_ref, da_ref, dqd_ref, dkt_ref, dtl_ref, ds_ref):
        @pl.when(pl.program_id(0) == 0)
        def _():
            ds_ref[...] = jnp.zeros_like(ds_ref)

        row0 = lax.broadcasted_iota(jnp.int32, (SUBLANES, LANES), 0) == 0

        def chunk(step, carry):
            ci = cb - 1 - step
            r = pl.ds(pl.multiple_of(ci * CHUNK, CHUNK), CHUNK)
            for h in range(HEADS):
                hc = _head_cols(h)
                s = s_in_ref[ci, h]
                ds_next = ds_ref[h]
                sb, dsb = _bf(s), _bf(ds_next)
                wb, ab, ktb, qdb = _bf(w_ref[r, hc]), _bf(a_ref[h, r, :]), _bf(kt_ref[r, hc]), _bf(qd_ref[r, hc])
                dob = _bf(do_ref[r, hc])
                vn = u_ref[r, hc] - _dot(wb, sb)
                vnb = _bf(vn)
                dvn = _dot_tn(ab, dob) + _dot(ktb, dsb)
                dvnb = _bf(dvn)
                du_ref[r, hc] = dvn
                dw_ref[r, hc] = -_dot_nt(dvnb, sb)
                da_ref[h, r, :] = _dot_nt(dob, vnb)
                dqd_ref[r, hc] = _dot_nt(dob, sb)
                dkt_ref[r, hc] = _dot_nt(vnb, dsb)
                dtl_ref[h, ci] = jnp.where(row0, _colsum(s * ds_next), 0.0)
                ds_ref[h] = _dot_tn(qdb, dob) + ds_next * tl_ref[h, ci, 0:1, :] - _dot_tn(wb, dvnb)
            return carry

        lax.fori_loop(0, cb, chunk, 0)

    full = jax.ShapeDtypeStruct((t_rows, HW), F32)
    return pl.pallas_call(
        body, name="gdn_scan_bwd",
        out_shape=[full, full, jax.ShapeDtypeStruct((HEADS, t_rows, CHUNK), F32), full, full,
                   jax.ShapeDtypeStruct((HEADS, nc, SUBLANES, LANES), F32)],
        grid=(nb,),
        in_specs=[wide, wide, wide, a_spec, wide, wide, tl_spec,
                  pl.BlockSpec((cb, HEADS, HEAD_DIM, HEAD_DIM), lambda b: (nb - 1 - b, 0, 0, 0))],
        out_specs=[wide, wide, a_spec, wide, wide, tl_spec],
        scratch_shapes=[pltpu.VMEM((HEADS, HEAD_DIM, HEAD_DIM), F32)],
        compiler_params=_params(("arbitrary",)),
    )(do, u, w, aqk, qd, kt, tl, states)


def _att_rel_index():
    qi = lax.broadcasted_iota(jnp.int32, (ATT_QB, ATT_KW), 0)
    kj = lax.broadcasted_iota(jnp.int32, (ATT_QB, ATT_KW), 1)
    return jnp.clip(qi - kj + ATT_PAD, -(CHUNK - 1), MAX_REL) + (CHUNK - 1)


def _att_valid(b):
    qi = lax.broadcasted_iota(jnp.int32, (ATT_QB, ATT_KW), 0)
    kj = lax.broadcasted_iota(jnp.int32, (ATT_QB, ATT_KW), 1)
    shift = CHUNK.bit_length() - 1
    qc = jnp.right_shift(qi, shift)
    kc = jnp.right_shift(kj, shift) - LEFT_CHUNKS
    return (kc <= qc) & (kc >= qc - LEFT_CHUNKS) & (kj + b * ATT_QB >= ATT_PAD)


def _att_block(q_raw, k_raw, v, qw, kw, bias, valid):
    q = _rms(q_raw, qw)
    k = _rms(k_raw, kw)
    s = _dot_nt(_bf(q), _bf(k)) * (HEAD_DIM ** -0.5) + bias
    s = jnp.where(valid, s, NEG_INF)
    p = jnp.exp(s - jnp.max(s, axis=-1, keepdims=True))
    p = p / jnp.sum(p, axis=-1, keepdims=True)
    return _dot(_bf(p), _bf(v))


def _att_specs():
    q_spec = pl.BlockSpec((ATT_QB, HEAD_DIM), lambda h, b: (b, h))
    k_specs = [pl.BlockSpec((ATT_QB, HEAD_DIM), lambda h, b, j=j: (b + j, HEADS + h)) for j in range(3)]
    v_specs = [pl.BlockSpec((ATT_QB, HEAD_DIM), lambda h, b, j=j: (b + j, 2 * HEADS + h)) for j in range(3)]
    w_spec = pl.BlockSpec((1, HEAD_DIM), lambda h, b: (0, 0))
    smem = pl.BlockSpec(memory_space=pltpu.SMEM)
    return q_spec, k_specs, v_specs, w_spec, smem


def _att_fill_bias(bias_ref, rel_ref, h):
    idx = _att_rel_index()

    def fill(r, acc):
        return jnp.where(idx == r, rel_ref[h, r], acc)

    bias_ref[...] = lax.fori_loop(0, N_REL, fill, jnp.zeros((ATT_QB, ATT_KW), F32))


def _attention(pb, pbp, qw, kw, rel):
    t_rows = pb.shape[0]
    q_spec, k_specs, v_specs, w_spec, smem = _att_specs()

    def body(q_ref, k0, k1, k2, v0, v1, v2, qw_ref, kw_ref, rel_ref, o_ref, bias_ref):
        h, b = pl.program_id(0), pl.program_id(1)

        @pl.when(b == 0)
        def _():
            _att_fill_bias(bias_ref, rel_ref, h)

        kwin = jnp.concatenate([k0[...], k1[...], k2[...]], axis=0)
        vwin = jnp.concatenate([v0[...], v1[...], v2[...]], axis=0)
        o = _att_block(q_ref[...], kwin, vwin, qw_ref[...], kw_ref[...], bias_ref[...], _att_valid(b))
        o_ref[...] = o.astype(o_ref.dtype)

    return pl.pallas_call(
        body, name="band_attention_fwd",
        out_shape=jax.ShapeDtypeStruct((t_rows, HW), BF16),
        grid=(HEADS, t_rows // ATT_QB),
        in_specs=[q_spec] + k_specs + v_specs + [w_spec, w_spec, smem],
        out_specs=pl.BlockSpec((ATT_QB, HEAD_DIM), lambda h, b: (b, h)),
        scratch_shapes=[pltpu.VMEM((ATT_QB, ATT_KW), F32)],
        compiler_params=_params(("arbitrary", "arbitrary")),
    )(pb, pbp, pbp, pbp, pbp, pbp, pbp, qw, kw, rel)


def _attention_bwd(pb, pbp, qw, kw, rel, dyb):
    t_rows = pb.shape[0]
    nb = t_rows // ATT_QB
    q_spec, k_specs, v_specs, w_spec, smem = _att_specs()
    pad_rows = t_rows + ATT_PAD
    acc_spec = pl.BlockSpec((pad_rows, HEAD_DIM), lambda h, b: (0, h))

    def body(q_ref, k0, k1, k2, v0, v1, v2, qw_ref, kw_ref, rel_ref, do_ref,
             dq_ref, dk_ref, dv_ref, dqw_ref, dkw_ref, drel_ref, bias_ref, dbias_ref):
        h, b = pl.program_id(0), pl.program_id(1)

        @pl.when(b == 0)
        def _():
            _att_fill_bias(bias_ref, rel_ref, h)
            dbias_ref[...] = jnp.zeros_like(dbias_ref)
            dk_ref[...] = jnp.zeros_like(dk_ref)
            dv_ref[...] = jnp.zeros_like(dv_ref)

        @pl.when((b == 0) & (h == 0))
        def _():
            dqw_ref[...] = jnp.zeros_like(dqw_ref)
            dkw_ref[...] = jnp.zeros_like(dkw_ref)

        kwin = jnp.concatenate([k0[...], k1[...], k2[...]], axis=0)
        vwin = jnp.concatenate([v0[...], v1[...], v2[...]], axis=0)
        valid = _att_valid(b)
        _, vjp = jax.vjp(lambda q, k, v, a, c, bias: _att_block(q, k, v, a, c, bias, valid),
                         q_ref[...], kwin, vwin, qw_ref[...], kw_ref[...], bias_ref[...])
        dq, dk, dv, dqw, dkw, dbias = vjp(do_ref[...])
        dq_ref[...] = dq.astype(dq_ref.dtype)
        win = pl.ds(pl.multiple_of(b * ATT_QB, ATT_QB), ATT_KW)
        dk_ref[win, :] += dk
        dv_ref[win, :] += dv
        dqw_ref[...] += dqw
        dkw_ref[...] += dkw
        dbias_ref[...] += dbias

        @pl.when(b == nb - 1)
        def _():
            idx = _att_rel_index()
            tot = dbias_ref[...]

            def reduce(r, carry):
                drel_ref[h, r] = jnp.sum(jnp.where(idx == r, tot, 0.0))
                return carry

            lax.fori_loop(0, N_REL, reduce, 0)

    return pl.pallas_call(
        body, name="band_attention_bwd",
        out_shape=[jax.ShapeDtypeStruct((t_rows, HW), BF16),
                   jax.ShapeDtypeStruct((pad_rows, HW), F32), jax.ShapeDtypeStruct((pad_rows, HW), F32),
                   jax.ShapeDtypeStruct((1, HEAD_DIM), F32), jax.ShapeDtypeStruct((1, HEAD_DIM), F32),
                   jax.ShapeDtypeStruct((HEADS, N_REL), F32)],
        grid=(HEADS, nb),
        in_specs=[q_spec] + k_specs + v_specs + [w_spec, w_spec, smem, q_spec],
        out_specs=[q_spec, acc_spec, acc_spec, w_spec, w_spec, smem],
        scratch_shapes=[pltpu.VMEM((ATT_QB, ATT_KW), F32), pltpu.VMEM((ATT_QB, ATT_KW), F32)],
        compiler_params=_params(("arbitrary", "arbitrary")),
    )(pb, pbp, pbp, pbp, pbp, pbp, pbp, qw, kw, rel, dyb)


def _me():
    return lax.axis_index("x"), lax.axis_index("y"), lax.axis_index("c")


def _index(x, y, c):
    return 4 * x + 2 * y + c


HBM_SPEC = pl.BlockSpec(memory_space=pl.ANY)


def _all_gather(shard):
    rows, width = shard.shape

    def body(x_ref, out_ref, send_sems, recv_sems, local_sem):
        x, y, c = _me()
        me, sibling = (x, y, c), (x, y, 1 - c)
        chips = [(1 - x, y), (x, 1 - y), (1 - x, 1 - y)]

        def block(px, py, pc):
            return out_ref.at[_index(px, py, pc)]

        def copy(k, blk, to, src=None):
            return pltpu.make_async_remote_copy(
                src_ref=block(*blk) if src is None else src, dst_ref=block(*blk),
                send_sem=send_sems.at[k], recv_sem=recv_sems.at[k], device_id=to, device_id_type=MESH)

        mine = pltpu.make_async_copy(x_ref, block(*me), local_sem)
        mine.start()
        first = [copy(0, me, sibling, src=x_ref)]
        first += [copy(1 + j, me, (*chip, c), src=x_ref) for j, chip in enumerate(chips)]
        for cp in first:
            cp.start()
        passed = [copy(4 + j, (*chip, c), sibling) for j, chip in enumerate(chips)]
        for j, chip in enumerate(chips):
            copy(1 + j, (*chip, c), me).wait_recv()
            passed[j].start()
        copy(0, sibling, me).wait_recv()
        for j, chip in enumerate(chips):
            copy(4 + j, (*chip, 1 - c), me).wait_recv()
        for cp in first + passed:
            cp.wait_send()
        mine.wait()

    return pl.pallas_call(
        body, name="weights_all_gather",
        out_shape=jax.ShapeDtypeStruct((N_DEV, rows, width), shard.dtype),
        in_specs=[HBM_SPEC], out_specs=HBM_SPEC,
        scratch_shapes=[pltpu.SemaphoreType.DMA((7,)), pltpu.SemaphoreType.DMA((7,)), pltpu.SemaphoreType.DMA],
        compiler_params=pltpu.CompilerParams(has_side_effects=True),
    )(shard)


def _exchange(blocks):
    _, rows, width = blocks.shape

    def body(g_ref, out_ref, send_sems, recv_sems, local_sem):
        x, y, c = _me()
        mine = _index(x, y, c)
        local = pltpu.make_async_copy(g_ref.at[mine], out_ref.at[mine], local_sem)
        local.start()
        copies = []
        for k in range(1, N_DEV):
            px, py, pc = x ^ (k >> 2), y ^ ((k >> 1) & 1), c ^ (k & 1)
            copies.append(pltpu.make_async_remote_copy(
                src_ref=g_ref.at[_index(px, py, pc)], dst_ref=out_ref.at[mine],
                send_sem=send_sems.at[k - 1], recv_sem=recv_sems.at[k - 1],
                device_id=(px, py, pc), device_id_type=MESH))
        for cp in copies:
            cp.start()
        for cp in copies:
            cp.wait()
        local.wait()

    return pl.pallas_call(
        body, name="grads_exchange",
        out_shape=jax.ShapeDtypeStruct(blocks.shape, blocks.dtype),
        in_specs=[HBM_SPEC], out_specs=HBM_SPEC,
        scratch_shapes=[pltpu.SemaphoreType.DMA((7,)), pltpu.SemaphoreType.DMA((7,)), pltpu.SemaphoreType.DMA],
        compiler_params=pltpu.CompilerParams(has_side_effects=True),
    )(blocks)


def _all_reduce_small(vals, name):
    rows, width = vals.shape

    def body(x_ref, out_ref, buf_ref, send_sems, recv_sems):
        x, y, c = _me()
        mine = _index(x, y, c)
        buf_ref[mine] = x_ref[...]
        copies = []
        for k in range(1, N_DEV):
            px, py, pc = x ^ (k >> 2), y ^ ((k >> 1) & 1), c ^ (k & 1)
            copies.append(pltpu.make_async_remote_copy(
                src_ref=x_ref, dst_ref=buf_ref.at[mine],
                send_sem=send_sems.at[k - 1], recv_sem=recv_sems.at[k - 1],
                device_id=(px, py, pc), device_id_type=MESH))
        for cp in copies:
            cp.start()
        for cp in copies:
            cp.wait()
        acc = buf_ref[0]
        for j in range(1, N_DEV):
            acc = acc + buf_ref[j]
        out_ref[...] = acc

    vmem = pl.BlockSpec(memory_space=pltpu.VMEM)
    return pl.pallas_call(
        body, name=name,
        out_shape=jax.ShapeDtypeStruct(vals.shape, F32),
        in_specs=[vmem], out_specs=vmem,
        scratch_shapes=[pltpu.VMEM((N_DEV, rows, width), F32),
                        pltpu.SemaphoreType.DMA((7,)), pltpu.SemaphoreType.DMA((7,))],
        compiler_params=pltpu.CompilerParams(has_side_effects=True),
    )(vals)


def _adamw_math(w, g, m, v):
    m = ADAM_B1 * m + (1.0 - ADAM_B1) * g
    v = ADAM_B2 * v + (1.0 - ADAM_B2) * (g * g)
    m_hat = m / (1.0 - ADAM_B1 ** ADAM_STEP)
    v_hat = v / (1.0 - ADAM_B2 ** ADAM_STEP)
    delta = -ADAM_LR * (m_hat / (jnp.sqrt(v_hat) + ADAM_EPS) + ADAM_WD * w)
    return delta, m, v


def _adamw_big(recv, w, m, v):
    _, rows, width = recv.shape
    tile = _pick(rows, (256, 128, 64, 32, 16))

    def body(r_ref, w_ref, m_ref, v_ref, g_out, d_out, m_out, v_out):
        g = r_ref[0].astype(F32)
        for j in range(1, N_DEV):
            g = g + r_ref[j].astype(F32)
        d, mn, vn = _adamw_math(w_ref[...], g, m_ref[...], v_ref[...])
        g_out[...] = g
        d_out[...] = d
        m_out[...] = mn
        v_out[...] = vn

    flat = pl.BlockSpec((tile, width), lambda i: (i, 0))
    shape = jax.ShapeDtypeStruct((rows, width), F32)
    return pl.pallas_call(
        body, name="adamw_big",
        out_shape=[shape] * 4, grid=(rows // tile,),
        in_specs=[pl.BlockSpec((N_DEV, tile, width), lambda i: (0, i, 0)), flat, flat, flat],
        out_specs=[flat] * 4,
        compiler_params=_params(("parallel",)),
    )(recv, w, m, v)


def _adamw_small(w, g, m, v, name):
    def fn(i, n, w_, g_, m_, v_):
        return _adamw_math(w_, g_, m_, v_)

    c = w.shape[1]
    return _rows(fn, [(w, "t"), (g, "t"), (m, "t"), (v, "t")], [], [(c, F32)] * 3, [], w.shape[0], name)


def _norm_fwd(x, w, name):
    return _rows(lambda i, n, x_, w_: (_rms(x_, w_[...]),), [(x, "t")], [w], [(D_MODEL, BF16)], [], 512, name)[0]


def _residual_norm_fwd(x, y, scale, w, name):
    def fn(i, n, x_, y_, w_):
        xn = x_ + scale * y_
        return xn, _rms(xn, w_[...])

    return _rows(fn, [(x, "t"), (y, "t")], [w], [(D_MODEL, F32), (D_MODEL, BF16)], [], 512, name)


def _residual_norm_bwd(x, w, dhs, dres, scale, name):
    nh = len(dhs)

    def fn(i, n, x_, dres_, *rest):
        dh = rest[0]
        for extra in rest[1:nh]:
            dh = dh + extra
        _, vjp = jax.vjp(_rms, x_, rest[nh][...])
        dx, dw = vjp(dh)
        dx = dx + dres_
        return dx, scale * dx, dw

    return _rows(fn, [(x, "t"), (dres, "t")] + [(d, "t") for d in dhs], [w],
                 [(D_MODEL, F32), (D_MODEL, BF16)], [(1, D_MODEL)], 256, name)


def _ffn_fwd(h, w_gu, w_down, tag):
    gu = _matmul(h, w_gu, "nn", F32, tag + "_gu")
    act = _rows(lambda i, n, gu_: (_swiglu(gu_),), [(gu, "t")], [], [(D_FF, BF16)], [], 128, tag + "_swiglu")[0]
    y = _matmul(act, w_down, "nn", F32, tag + "_down")
    return gu, act, y


def _ffn_bwd(h, gu, act, dy, w_gu, w_down, tag):
    d_w_down = _matmul(act, dy, "tn", BF16, tag + "_d_w_down")
    dact = _matmul(dy, w_down, "nt", F32, tag + "_dact")

    def fn(i, n, gu_, dact_):
        _, vjp = jax.vjp(_swiglu, gu_)
        return vjp(dact_)

    dgu = _rows(fn, [(gu, "t"), (dact, "t")], [], [(2 * D_FF, BF16)], [], 128, tag + "_swiglu_bwd")[0]
    d_w_gu = _matmul(h, dgu, "tn", BF16, tag + "_d_w_gu")
    dh = _matmul(dgu, w_gu, "nt", F32, tag + "_dh")
    return dh, d_w_gu, d_w_down


def _expanders():
    e_g = np.zeros((LANES, HW), np.float32)
    e_b = np.zeros((LANES, HW), np.float32)
    for h in range(HEADS):
        e_g[h, h * HEAD_DIM:(h + 1) * HEAD_DIM] = 1.0
        e_b[HEADS + h, h * HEAD_DIM:(h + 1) * HEAD_DIM] = 1.0
    return jnp.asarray(e_g), jnp.asarray(e_b)


def _pad_lanes(v):
    return jnp.pad(v, ((0, 0), (0, LANES - v.shape[1])))


def _local_step(x, p, tgt, small, big):
    w_in = big["w_in"]
    w_qz = w_in[:, :IN_QZ]
    w_ab = jnp.pad(w_in[:, IN_AB0:IN_QKVB0], ((0, 0), (0, LANES - 2 * HEADS)))
    w_qkvb = w_in[:, IN_QKVB0:IN_GG0]
    w_gg = w_in[:, IN_GG0:]
    e_g, e_b = _expanders()
    alog, dtb = _pad_lanes(small["a_log"]), _pad_lanes(small["dt_bias"])
    conv_w = jnp.pad(small["conv_w"], ((0, SUBLANES - CONV_K), (0, 0)))
    rel = small["rel_bias"]

    h1 = _norm_fwd(x, small["ffn1_norm"], "ffn1_norm")
    gu1, act1, y1 = _ffn_fwd(h1, big["ffn1_w_gu"], big["ffn1_w_down"], "ffn1")
    x1, h2 = _residual_norm_fwd(x, y1, 0.5, small["mix_norm"], "mix_norm")

    qz = _matmul(h2, w_qz, "nn", F32, "in_qz")
    ab = _matmul(h2, w_ab, "nn", F32, "in_ab")
    pb = _matmul(h2, w_qkvb, "nn", F32, "in_qkvb")
    gg = _matmul(h2, w_gg, "nn", F32, "in_gates")
    pa, z = qz[:, :3 * HW], qz[:, 3 * HW:]

    def prep(i, n, pa_, prev_, ab_, cw_, alog_, dtb_, eg_, eb_):
        q, k, v = _gdn_post(_conv(pa_, prev_, cw_, i))
        g_b, beta_b = _gdn_gates(ab_, alog_[...], dtb_[...], eg_[...], eb_[...])
        return q, k, v, g_b, beta_b

    qn, kn, vv, g_b, beta_b = _rows(prep, [(pa, "t"), (pa, "p"), (ab, "t")], [conv_w, alog, dtb, e_g, e_b],
                                    [(HW, F32)] * 5, [], 256, "gdn_prep")
    u, w, aqk, qd, kt, tl = _gdn_intra(qn, kn, vv, g_b, beta_b)
    o, states = _gdn_scan(u, w, aqk, qd, kt, tl)
    ya = _rows(lambda i, n, o_, z_, w_: (_gated_norm(o_, z_, w_[...]),), [(o, "t"), (z, "t")], [small["gdn_norm"]],
               [(HW, BF16)], [], 512, "gdn_gated_norm")[0]

    pbp = jnp.pad(pb, ((ATT_PAD, 0), (0, 0)))
    yb = _attention(pb, pbp, small["q_norm"], small["k_norm"], rel)

    ta = _matmul(ya, big["w_branch_a"], "nn", F32, "branch_a")
    tb = _matmul(yb, big["w_branch_b"], "nn", F32, "branch_b")
    mixed = _rows(lambda i, n, gg_, ta_, tb_: (_mix(gg_, ta_, tb_),), [(gg, "t"), (ta, "t"), (tb, "t")], [],
                  [(D_MODEL, BF16)], [], 256, "mix")[0]
    m_out = _matmul(mixed, big["w_out"], "nn", F32, "w_out")
    x2, h3 = _residual_norm_fwd(x1, m_out, 1.0, small["ffn2_norm"], "ffn2_norm")
    gu2, act2, y2 = _ffn_fwd(h3, big["ffn2_w_gu"], big["ffn2_w_down"], "ffn2")
    x3, h4 = _residual_norm_fwd(x2, y2, 0.5, small["ple_norm"], "ple_norm")
    gp = _matmul(h4, big["ple_gate"], "nn", F32, "ple_gate")
    pp = _matmul(p, big["ple_proj"], "nn", F32, "ple_proj")

    def head(i, n, x3_, gp_, pp_, tgt_):
        sg = _sigmoid(gp_)
        err = x3_ + sg * pp_ - tgt_
        dx4 = err * (1.0 / D_MODEL)
        sq = _colsum(err * err)
        part = sq[:, :LANES]
        for j in range(1, D_MODEL // LANES):
            part = part + sq[:, j * LANES:(j + 1) * LANES]
        return dx4, dx4 * pp_ * sg * (1.0 - sg), dx4 * sg, (0.5 / D_MODEL) * part

    dx4, dgp, dpp, loss_lanes = _rows(head, [(x3, "t"), (gp, "t"), (pp, "t"), (tgt, "t")], [],
                                      [(D_MODEL, F32), (D_MODEL, BF16), (D_MODEL, BF16)], [(1, LANES)], 256,
                                      "ple_loss_head")
    loss = jnp.sum(loss_lanes)

    gbig, gsmall = {}, {}
    gbig["ple_proj"] = _matmul(p, dpp, "tn", BF16, "d_ple_proj")
    gbig["ple_gate"] = _matmul(h4, dgp, "tn", BF16, "d_ple_gate")
    dh4 = _matmul(dgp, big["ple_gate"], "nt", F32, "ple_gate_dh")
    dx3, dy2, gsmall["ple_norm"] = _residual_norm_bwd(x3, small["ple_norm"], [dh4], dx4, 0.5, "ple_norm_bwd")

    dh3, gbig["ffn2_w_gu"], gbig["ffn2_w_down"] = _ffn_bwd(h3, gu2, act2, dy2, big["ffn2_w_gu"],
                                                           big["ffn2_w_down"], "ffn2")
    dx2, dx2b, gsmall["ffn2_norm"] = _residual_norm_bwd(x2, small["ffn2_norm"], [dh3], dx3, 1.0, "ffn2_norm_bwd")

    gbig["w_out"] = _matmul(mixed, dx2b, "tn", BF16, "d_w_out")
    dmixed = _matmul(dx2b, big["w_out"], "nt", F32, "w_out_dx")

    def mix_bwd(i, n, gg_, ta_, tb_, dm_):
        _, vjp = jax.vjp(_mix, gg_, ta_, tb_)
        return vjp(dm_)

    dgg, dta, dtb_ = _rows(mix_bwd, [(gg, "t"), (ta, "t"), (tb, "t"), (dmixed, "t")], [],
                           [(2 * D_MODEL, BF16), (D_MODEL, BF16), (D_MODEL, BF16)], [], 256, "mix_bwd")
    gbig["w_branch_a"] = _matmul(ya, dta, "tn", BF16, "d_branch_a")
    gbig["w_branch_b"] = _matmul(yb, dtb_, "tn", BF16, "d_branch_b")
    dya = _matmul(dta, big["w_branch_a"], "nt", F32, "branch_a_dx")
    dyb = _matmul(dtb_, big["w_branch_b"], "nt", F32, "branch_b_dx")

    dq_b, dk_b, dv_b, gsmall["q_norm"], gsmall["k_norm"], gsmall["rel_bias"] = _attention_bwd(
        pb, pbp, small["q_norm"], small["k_norm"], rel, dyb)
    dpb = jnp.concatenate([dq_b, dk_b[ATT_PAD:].astype(BF16), dv_b[ATT_PAD:].astype(BF16)], axis=1)

    def gated_bwd(i, n, o_, z_, dya_, w_):
        _, vjp = jax.vjp(_gated_norm, o_, z_, w_[...])
        return vjp(dya_)

    do, dz, gsmall["gdn_norm"] = _rows(gated_bwd, [(o, "t"), (z, "t"), (dya, "t")], [small["gdn_norm"]],
                                       [(HW, F32), (HW, BF16)], [(1, HEAD_DIM)], 256, "gdn_gated_norm_bwd")
    du, dw, da, dqd, dkt, dtl = _gdn_scan_bwd(do, u, w, aqk, qd, kt, tl, states)
    dqn, dkn, dvv, dg_b, dbeta_b = _gdn_intra_bwd(qn, kn, vv, g_b, beta_b, du, dw, da, dqd, dkt, dtl)

    def prep_bwd(i, n, pa_, prev_, ab_, dq_, dk_, dv_, dg_, db_, cw_, alog_, dtb_, eg_, eb_):
        _, vjp = jax.vjp(_gdn_post, _conv(pa_, prev_, cw_, i))
        (dy,) = vjp((dq_, dk_, dv_))
        e_g_, e_b_ = eg_[...], eb_[...]
        _, vjp_g = jax.vjp(lambda a, b, c: _gdn_gates(a, b, c, e_g_, e_b_), ab_, alog_[...], dtb_[...])
        dab, dalog, ddtb = vjp_g((dg_, db_))
        return dy, dab, dalog, ddtb

    dy_conv, dab, dalog, ddtb = _rows(
        prep_bwd, [(pa, "t"), (pa, "p"), (ab, "t"), (dqn, "t"), (dkn, "t"), (dvv, "t"), (dg_b, "t"), (dbeta_b, "t")],
        [conv_w, alog, dtb, e_g, e_b], [(3 * HW, F32), (LANES, BF16)], [(1, LANES), (1, LANES)], 256,
        "gdn_prep_bwd")
    gsmall["a_log"] = dalog[:, :HEADS]
    gsmall["dt_bias"] = ddtb[:, :HEADS]

    def conv_bwd(i, n, dy_, nxt_, pa_, prev_, cw_):
        dpa = dy_ * cw_[CONV_K - 1:CONV_K, :]
        row = lax.broadcasted_iota(jnp.int32, (SUBLANES, dy_.shape[1]), 0)
        dcw = jnp.where(row == CONV_K - 1, _colsum(dy_ * pa_), 0.0)
        for j in range(CONV_K - 1):
            s = CONV_K - 1 - j
            dpa = dpa + _shift_up(dy_, nxt_, s, i, n) * cw_[j:j + 1, :]
            dcw = dcw + jnp.where(row == j, _colsum(dy_ * _shift_down(pa_, prev_, s, i)), 0.0)
        return dpa, dcw

    dpa, dcw = _rows(conv_bwd, [(dy_conv, "t"), (dy_conv, "n"), (pa, "t"), (pa, "p")], [conv_w],
                     [(3 * HW, BF16)], [(SUBLANES, 3 * HW)], 256, "gdn_conv_bwd")
    gsmall["conv_w"] = dcw[:CONV_K]

    dqz = jnp.concatenate([dpa, dz], axis=1)
    d_w_qz = _matmul(h2, dqz, "tn", BF16, "d_in_qz")
    d_w_ab = _matmul(h2, dab, "tn", BF16, "d_in_ab")
    d_w_qkvb = _matmul(h2, dpb, "tn", BF16, "d_in_qkvb")
    d_w_gg = _matmul(h2, dgg, "tn", BF16, "d_in_gates")
    gbig["w_in"] = jnp.concatenate([d_w_qz, d_w_ab[:, :2 * HEADS], d_w_qkvb, d_w_gg], axis=1)
    dh2 = [_matmul(dqz, w_qz, "nt", F32, "in_qz_dh"), _matmul(dab, w_ab, "nt", F32, "in_ab_dh"),
           _matmul(dpb, w_qkvb, "nt", F32, "in_qkvb_dh"), _matmul(dgg, w_gg, "nt", F32, "in_gates_dh")]
    dx1, dy1, gsmall["mix_norm"] = _residual_norm_bwd(x1, small["mix_norm"], dh2, dx2, 0.5, "mix_norm_bwd")

    dh1, gbig["ffn1_w_gu"], gbig["ffn1_w_down"] = _ffn_bwd(h1, gu1, act1, dy1, big["ffn1_w_gu"],
                                                           big["ffn1_w_down"], "ffn1")
    grad_x, _, gsmall["ffn1_norm"] = _residual_norm_bwd(x, small["ffn1_norm"], [dh1], dx1, 1.0, "ffn1_norm_bwd")
    return loss, grad_x, gsmall, gbig


def _flat_rows(shards):
    total = sum(int(np.prod(shards[n].shape)) for n in BIG)
    unit = FLAT_W * FLAT_TILE
    return -(-total // unit) * FLAT_TILE


def _flatten(parts, rows):
    flat = jnp.concatenate([parts[n].reshape(-1) for n in BIG])
    return jnp.pad(flat, (0, rows * FLAT_W - flat.shape[0])).reshape(rows, FLAT_W)


def _unflatten(flat, shapes):
    lead = flat.shape[:-2]
    flat = flat.reshape(lead + (-1,))
    out, off = {}, 0
    for n in BIG:
        size = int(np.prod(shapes[n]))
        out[n] = flat[..., off:off + size].reshape(lead + tuple(shapes[n]))
        off += size
    return out


def _full_weight(name, gathered):
    _, r, c = gathered.shape
    if name in COL_SHARDED:
        return gathered.transpose(1, 0, 2).reshape(r, N_DEV * c)
    return gathered.reshape(N_DEV * r, c)


def _split_grad(name, grad):
    if name in COL_SHARDED:
        r, cols = grad.shape
        return grad.reshape(r, N_DEV, cols // N_DEV).transpose(1, 0, 2)
    rows, c = grad.shape
    return grad.reshape(N_DEV, rows // N_DEV, c)


SMALL_ROWS = ("ffn1_norm", "mix_norm", "ffn2_norm", "ple_norm", "gdn_norm", "q_norm", "k_norm", "a_log", "dt_bias",
              "rel_bias", "conv_w")


def _pack_small(vals):
    rows = []
    for n in SMALL_ROWS:
        v = vals[n]
        if n == "rel_bias":
            v = jnp.pad(v, ((0, 0), (0, 2 * LANES - N_REL)))
        elif n in ("a_log", "dt_bias"):
            v = _pad_lanes(v)
        rows.append(v.reshape(-1, LANES))
    packed = jnp.concatenate(rows, axis=0)
    return jnp.pad(packed, ((0, -packed.shape[0] % SUBLANES), (0, 0)))


def _unpack_small(packed, shapes):
    out, off = {}, 0
    for n in SMALL_ROWS:
        shp = shapes[n]
        if n == "rel_bias":
            out[n] = packed[off:off + 2 * HEADS].reshape(HEADS, 2 * LANES)[:, :N_REL]
            off += 2 * HEADS
        elif n in ("a_log", "dt_bias"):
            out[n] = packed[off:off + 1, :HEADS]
            off += 1
        else:
            r = int(np.prod(shp)) // LANES
            out[n] = packed[off:off + r].reshape(shp)
            off += r
    return out


WEIGHTS = ("ffn1_norm", "ffn1_w_gu", "ffn1_w_down", "mix_norm", "w_in", "conv_w", "a_log", "dt_bias", "gdn_norm",
           "q_norm", "k_norm", "rel_bias", "w_branch_a", "w_branch_b", "w_out", "ffn2_norm", "ffn2_w_gu",
           "ffn2_w_down", "ple_norm", "ple_gate", "ple_proj")


def kernel(x, p, ffn1_norm, ffn1_w_gu, ffn1_w_down, mix_norm, w_in, conv_w, a_log, dt_bias, gdn_norm, q_norm, k_norm, rel_bias, w_branch_a, w_branch_b, w_out, ffn2_norm, ffn2_w_gu, ffn2_w_down, ple_norm, ple_gate, ple_proj, loss_target, m_ffn1_norm, m_ffn1_w_gu, m_ffn1_w_down, m_mix_norm, m_w_in, m_conv_w, m_a_log, m_dt_bias, m_gdn_norm, m_q_norm, m_k_norm, m_rel_bias, m_w_branch_a, m_w_branch_b, m_w_out, m_ffn2_norm, m_ffn2_w_gu, m_ffn2_w_down, m_ple_norm, m_ple_gate, m_ple_proj, v_ffn1_norm, v_ffn1_w_gu, v_ffn1_w_down, v_mix_norm, v_w_in, v_conv_w, v_a_log, v_dt_bias, v_gdn_norm, v_q_norm, v_k_norm, v_rel_bias, v_w_branch_a, v_w_branch_b, v_w_out, v_ffn2_norm, v_ffn2_w_gu, v_ffn2_w_down, v_ple_norm, v_ple_gate, v_ple_proj):
    args = dict(locals())
    def layer0(v):
        return v[0] if v.ndim == 3 else v

    wts = {n: layer0(args[n]) for n in WEIGHTS}
    mom = {n: layer0(args["m_" + n]) for n in WEIGHTS}
    var = {n: layer0(args["v_" + n]) for n in WEIGHTS}
    x2d, p2d, tgt = x[0], p[0, 0], loss_target[0]
    my_index = _index(*_me())

    shard_shapes = {n: wts[n].shape for n in BIG}
    rows = _flat_rows(wts)
    gathered = _all_gather(_flatten({n: wts[n].astype(BF16) for n in BIG}, rows))
    big = {n: _full_weight(n, g) for n, g in _unflatten(gathered, shard_shapes).items()}

    small = {n: wts[n] for n in SMALL_ROWS if n != "conv_w"}
    conv_shard = wts["conv_w"]
    conv_cols = conv_shard.shape[1]
    conv_packed = jnp.zeros((SUBLANES, N_DEV * conv_cols), F32)
    conv_packed = lax.dynamic_update_slice(conv_packed, jnp.pad(conv_shard, ((0, SUBLANES - CONV_K), (0, 0))),
                                           (0, my_index * conv_cols))
    small["conv_w"] = _all_reduce_small(conv_packed.reshape(-1, LANES), "conv_w_gather").reshape(SUBLANES, -1)[:CONV_K]

    loss, grad_x, gsmall, gbig = _local_step(x2d, p2d, tgt, small, big)
    loss = lax.psum(loss, ("x", "y", "c"))

    send = jnp.stack([_flatten({n: _split_grad(n, gbig[n])[d] for n in BIG}, rows) for d in range(N_DEV)])
    recv = _exchange(send)
    w_flat, m_flat, v_flat = (_flatten({n: src[n] for n in BIG}, rows) for src in (wts, mom, var))
    outs_big = [_unflatten(o, shard_shapes) for o in _adamw_big(recv, w_flat, m_flat, v_flat)]

    small_shapes = {n: (small[n].shape if n != "conv_w" else (CONV_K, N_DEV * conv_cols)) for n in SMALL_ROWS}
    gsum = _unpack_small(_all_reduce_small(_pack_small(gsmall), "small_grads_all_reduce"), small_shapes)
    gsum["conv_w"] = lax.dynamic_slice(gsum["conv_w"], (0, my_index * conv_cols), (CONV_K, conv_cols))
    rep = [n for n in SMALL_ROWS if n != "conv_w"]
    rep_shapes = {n: small_shapes[n] for n in rep}

    def pack_rep(vals):
        return _pack_small({**{n: vals[n] for n in rep}, "conv_w": jnp.zeros((CONV_K, LANES), F32)})

    def unpack_rep(packed):
        return _unpack_small(packed, {**rep_shapes, "conv_w": (CONV_K, LANES)})

    outs_small = [unpack_rep(o) for o in _adamw_small(pack_rep(wts), pack_rep(gsum), pack_rep(mom), pack_rep(var),
                                                      "adamw_replicated")]
    pad8 = functools.partial(jnp.pad, pad_width=((0, SUBLANES - CONV_K), (0, 0)))
    outs_conv = [o[:CONV_K] for o in _adamw_small(pad8(conv_shard), pad8(gsum["conv_w"]), pad8(mom["conv_w"]),
                                                   pad8(var["conv_w"]), "adamw_conv")]

    def leaf(kind, n):
        if n in BIG:
            return outs_big[kind][n][None]
        if n == "conv_w":
            return (gsum["conv_w"] if kind == 0 else outs_conv[kind - 1])[None]
        return (gsum[n] if kind == 0 else outs_small[kind - 1][n]).reshape(args[n].shape)

    result = [loss, grad_x[None]]
    for kind in range(4):
        result += [leaf(kind, n) for n in WEIGHTS]
    return tuple(result)
```

```python
import functools

import numpy as np
import jax
import jax.numpy as jnp
from jax import lax
from jax.experimental import pallas as pl
from jax.experimental.pallas import tpu as pltpu

F32 = jnp.float32
BF16 = jnp.bfloat16
HIGHEST = lax.Precision.HIGHEST
MESH = pl.DeviceIdType.MESH

D_MODEL = 2048
D_FF = 5632
HEADS = 8
HEAD_DIM = 128
HW = HEADS * HEAD_DIM
CHUNK = 64
LEFT_CHUNKS = 8
MAX_REL = 128
N_REL = (CHUNK - 1) + MAX_REL + 1
CONV_K = 4
EPS = 1e-6
NEG_INF = -1e30
N_DEV = 8
LANES = 128
SUBLANES = 8
VMEM_LIMIT = 56 * 1024 * 1024

ATT_QB = 256
ATT_KW = ATT_QB + LEFT_CHUNKS * CHUNK
ATT_PAD = LEFT_CHUNKS * CHUNK
GDN_CB = 8
GDN_GROUP = 4

ADAM_LR = 0.001
ADAM_B1 = 0.9
ADAM_B2 = 0.999
ADAM_EPS = 1e-08
ADAM_WD = 0.01
ADAM_STEP = 10

IN_QZ = 3 * HW + HW
IN_AB0 = IN_QZ
IN_QKVB0 = IN_AB0 + 2 * HEADS
IN_GG0 = IN_QKVB0 + 3 * HW
IN_COLS = IN_GG0 + 2 * D_MODEL

BIG = ("ffn1_w_gu", "ffn1_w_down", "w_in", "w_branch_a", "w_branch_b", "w_out",
       "ffn2_w_gu", "ffn2_w_down", "ple_gate", "ple_proj")
COL_SHARDED = ("ffn1_w_gu", "w_in", "w_branch_a", "w_branch_b", "ffn2_w_gu", "ple_proj")


def _params(semantics=None, **kw):
    return pltpu.CompilerParams(dimension_semantics=semantics, vmem_limit_bytes=VMEM_LIMIT, **kw)


def _pick(n, cands):
    for c in cands:
        if n % c == 0:
            return c
    return n


def _matmul(a, b, mode, out_dtype, name):
    if mode == "nn":
        (m, k), (k2, n) = a.shape, b.shape
    elif mode == "nt":
        (m, k), (n, k2) = a.shape, b.shape
    else:
        (k, m), (k2, n) = a.shape, b.shape
    assert k == k2, (a.shape, b.shape, mode)
    tm = _pick(m, (1024, 512, 256, 128))
    tn = _pick(n, (1024, 512, 256, 128))
    tk = _pick(k, (1024, 512, 256, 128))
    nk = k // tk
    if mode == "nn":
        a_spec = pl.BlockSpec((tm, tk), lambda i, j, kk: (i, kk))
        b_spec = pl.BlockSpec((tk, tn), lambda i, j, kk: (kk, j))
        dims = (((1,), (0,)), ((), ()))
    elif mode == "nt":
        a_spec = pl.BlockSpec((tm, tk), lambda i, j, kk: (i, kk))
        b_spec = pl.BlockSpec((tn, tk), lambda i, j, kk: (j, kk))
        dims = (((1,), (1,)), ((), ()))
    else:
        a_spec = pl.BlockSpec((tk, tm), lambda i, j, kk: (kk, i))
        b_spec = pl.BlockSpec((tk, tn), lambda i, j, kk: (kk, j))
        dims = (((0,), (0,)), ((), ()))

    def body(a_ref, b_ref, o_ref, acc_ref):
        kk = pl.program_id(2)

        @pl.when(kk == 0)
        def _():
            acc_ref[...] = jnp.zeros_like(acc_ref)

        acc_ref[...] += lax.dot_general(a_ref[...].astype(BF16), b_ref[...].astype(BF16), dims,
                                        preferred_element_type=F32)

        @pl.when(kk == nk - 1)
        def _():
            o_ref[...] = acc_ref[...].astype(o_ref.dtype)

    return pl.pallas_call(
        body, name=name,
        out_shape=jax.ShapeDtypeStruct((m, n), out_dtype),
        grid=(m // tm, n // tn, nk),
        in_specs=[a_spec, b_spec],
        out_specs=pl.BlockSpec((tm, tn), lambda i, j, kk: (i, j)),
        scratch_shapes=[pltpu.VMEM((tm, tn), F32)],
        compiler_params=_params(("parallel", "parallel", "arbitrary")),
    )(a, b)


def _rows(fn, row_ins, consts, row_outs, acc_outs, tile, name):
    t_rows = row_ins[0][0].shape[0]
    tile = min(tile, t_rows)
    assert t_rows % tile == 0 and tile % SUBLANES == 0
    n = t_rows // tile
    per = tile // SUBLANES
    last8 = t_rows // SUBLANES - 1
    in_specs = []
    for arr, kind in row_ins:
        c = arr.shape[1]
        if kind == "t":
            in_specs.append(pl.BlockSpec((tile, c), lambda i: (i, 0)))
        elif kind == "p":
            in_specs.append(pl.BlockSpec((SUBLANES, c), lambda i: (jnp.maximum(i * per - 1, 0), 0)))
        else:
            in_specs.append(pl.BlockSpec((SUBLANES, c), lambda i: (jnp.minimum((i + 1) * per, last8), 0)))
    for arr in consts:
        in_specs.append(pl.BlockSpec(arr.shape, lambda i, nd=arr.ndim: (0,) * nd))
    out_shape = [jax.ShapeDtypeStruct((t_rows, c), dt) for c, dt in row_outs]
    out_specs = [pl.BlockSpec((tile, c), lambda i: (i, 0)) for c, _ in row_outs]
    for shp in acc_outs:
        out_shape.append(jax.ShapeDtypeStruct(shp, F32))
        out_specs.append(pl.BlockSpec(shp, lambda i, nd=len(shp): (0,) * nd))
    n_in = len(row_ins) + len(consts)
    n_row_out = len(row_outs)

    def body(*refs):
        i = pl.program_id(0)
        vals = [r[...] for r in refs[:len(row_ins)]]
        res = fn(i, n, *vals, *refs[len(row_ins):n_in])
        outs = refs[n_in:]
        for r, v in zip(outs[:n_row_out], res[:n_row_out]):
            r[...] = v.astype(r.dtype)
        if acc_outs:
            @pl.when(i == 0)
            def _():
                for r in outs[n_row_out:]:
                    r[...] = jnp.zeros_like(r)

            for r, v in zip(outs[n_row_out:], res[n_row_out:]):
                r[...] += v

    res = pl.pallas_call(
        body, name=name, out_shape=out_shape, grid=(n,), in_specs=in_specs, out_specs=out_specs,
        compiler_params=_params(("arbitrary",) if acc_outs else ("parallel",)),
    )(*[a for a, _ in row_ins], *consts)
    return res


def _rms(x, w):
    return x * lax.rsqrt(jnp.mean(x * x, axis=-1, keepdims=True) + EPS) * w


def _l2n(x):
    return x * lax.rsqrt(jnp.sum(x * x, axis=-1, keepdims=True) + EPS)


def _sigmoid(x):
    return 1.0 / (1.0 + jnp.exp(-x))


def _silu(x):
    return x * _sigmoid(x)


def _softplus(x):
    return jnp.maximum(x, 0.0) + jnp.log(1.0 + jnp.exp(-jnp.abs(x)))


def _heads(fn, *xs):
    nh = xs[0].shape[1] // HEAD_DIM
    return jnp.concatenate(
        [fn(*[x[:, h * HEAD_DIM:(h + 1) * HEAD_DIM] for x in xs]) for h in range(nh)], axis=1)


def _colsum(x):
    return jnp.sum(x, axis=0, keepdims=True)


def _swiglu(gu):
    return _silu(gu[:, :D_FF]) * gu[:, D_FF:]


def _gated_norm(o, z, w):
    return _heads(lambda oh, zh: _rms(oh, w) * _silu(zh), o, z)


def _mix(gg, ta, tb):
    return _sigmoid(gg[:, :D_MODEL]) * ta + _sigmoid(gg[:, D_MODEL:]) * tb


def _gdn_post(y):
    a = _silu(y)
    q = _heads(lambda v: _l2n(v) * (HEAD_DIM ** -0.5), a[:, :HW])
    k = _heads(_l2n, a[:, HW:2 * HW])
    return q, k, a[:, 2 * HW:]


NN = (((1,), (0,)), ((), ()))
NT = (((1,), (1,)), ((), ()))
TN = (((0,), (0,)), ((), ()))


def _dg(a, b, dims):
    return lax.dot_general(a, b, dims, preferred_element_type=F32)


def _split2(x):
    hi = x.astype(BF16)
    return hi, (x - hi.astype(F32)).astype(BF16)


def _split3(x):
    hi = x.astype(BF16)
    r = x - hi.astype(F32)
    mid = r.astype(BF16)
    return hi, mid, (r - mid.astype(F32)).astype(BF16)


def _dg3(a, b, dims):
    ah, al = _split2(a)
    bh, bl = _split2(b)
    return _dg(ah, bh, dims) + (_dg(ah, bl, dims) + _dg(al, bh, dims))


@jax.custom_vjp
def _mm3(a, b):
    return _dg3(a, b, NN)


_mm3.defvjp(lambda a, b: (_dg3(a, b, NN), (a, b)),
            lambda res, g: (_dg3(g, res[1], NT), _dg3(res[0], g, TN)))


def _xm(x, m, dims):
    mb = m.astype(BF16)
    parts = _split3(x)
    return _dg(parts[0], mb, dims) + (_dg(parts[1], mb, dims) + _dg(parts[2], mb, dims))


def _mx(m, x, dims):
    mb = m.astype(BF16)
    parts = _split3(x)
    return _dg(mb, parts[0], dims) + (_dg(mb, parts[1], dims) + _dg(mb, parts[2], dims))


@jax.custom_vjp
def _times_const(x, m):
    return _xm(x, m, NN)


_times_const.defvjp(lambda x, m: (_xm(x, m, NN), m),
                    lambda m, g: (_xm(g, m, NT), jnp.zeros_like(m)))


@jax.custom_vjp
def _const_times(m, x):
    return _mx(m, x, NN)


_const_times.defvjp(lambda m, x: (_mx(m, x, NN), m),
                    lambda m, g: (jnp.zeros_like(m), _mx(m, g, TN)))


@jax.custom_vjp
def _lane_mean_cols(x, avg):
    return _mx(avg, x, NT)


_lane_mean_cols.defvjp(lambda x, avg: (_mx(avg, x, NT), avg),
                       lambda avg, g: (_xm(g, avg, TN), jnp.zeros_like(avg)))


def _gdn_gates(ab, alog, dtb, e_g, e_b):
    t = ab.shape[0]
    g = -jnp.exp(alog) * _softplus(ab + dtb)
    beta = _sigmoid(ab)
    ri = lax.broadcasted_iota(jnp.int32, (t, t), 0)
    ci = lax.broadcasted_iota(jnp.int32, (t, t), 1)
    shift = CHUNK.bit_length() - 1
    same = jnp.right_shift(ri, shift) == jnp.right_shift(ci, shift)
    tril = jnp.where(same & (ri >= ci), 1.0, 0.0).astype(F32)
    gc = _const_times(tril, g)
    return _times_const(gc, e_g), _times_const(beta, e_b)


def _shift_down(x, halo, s, i):
    if s == 0:
        return x
    halo = jnp.where(i == 0, 0.0, halo)
    xr = pltpu.roll(x, s, 0)
    hr = pltpu.roll(halo, s, 0)
    row = lax.broadcasted_iota(jnp.int32, (SUBLANES, x.shape[1]), 0)
    top = jnp.where(row < s, hr, xr[:SUBLANES])
    return jnp.concatenate([top, xr[SUBLANES:]], axis=0)


def _shift_up(x, halo, s, i, n):
    if s == 0:
        return x
    t = x.shape[0]
    halo = jnp.where(i == n - 1, 0.0, halo)
    xr = pltpu.roll(x, t - s, 0)
    hr = pltpu.roll(halo, SUBLANES - s, 0)
    row = lax.broadcasted_iota(jnp.int32, (SUBLANES, x.shape[1]), 0)
    bot = jnp.where(row >= SUBLANES - s, hr, xr[t - SUBLANES:])
    return jnp.concatenate([xr[:t - SUBLANES], bot], axis=0)


def _conv(pa, prev, cw_ref, i):
    y = pa * cw_ref[CONV_K - 1:CONV_K, :]
    for j in range(CONV_K - 1):
        y = y + _shift_down(pa, prev, CONV_K - 1 - j, i) * cw_ref[j:j + 1, :]
    return y


def _dot_nt(a, b, precision=None):
    return lax.dot_general(a, b, (((1,), (1,)), ((), ())), precision=precision, preferred_element_type=F32)


def _dot_tn(a, b, precision=None):
    return lax.dot_general(a, b, (((0,), (0,)), ((), ())), precision=precision, preferred_element_type=F32)


def _dot(a, b, precision=None):
    return jnp.dot(a, b, precision=precision, preferred_element_type=F32)


def _bf(x):
    return x.astype(BF16)


def _gdn_chunk(q, k, v, gc, bb):
    c = q.shape[0]
    ri = lax.broadcasted_iota(jnp.int32, (c, c), 0)
    ci = lax.broadcasted_iota(jnp.int32, (c, c), 1)
    incl = ri >= ci
    strict = ri > ci
    g_row = gc[:, :c]
    g_col = _lane_mean_cols(gc, jnp.full((c, LANES), 1.0 / LANES, F32))
    decay = jnp.where(incl, jnp.exp(jnp.where(incl, g_row - g_col, 0.0)), 0.0)
    kb = k * bb
    lmat = jnp.where(strict, _dot_nt(_bf(kb), _bf(k)) * decay, 0.0)
    eye = jnp.where(ri == ci, 1.0, 0.0).astype(F32)
    pw = -lmat
    inv = eye + pw
    for _ in range(5):
        pw = _mm3(pw, pw)
        inv = inv + _mm3(inv, pw)
    egc = jnp.exp(gc)
    u = _mm3(inv, v * bb)
    w = _mm3(inv, kb * egc)
    aqk = _dot_nt(_bf(q), _bf(k)) * decay
    last = lax.broadcasted_iota(jnp.int32, (c, LANES), 0) == c - 1
    tot = _colsum(jnp.where(last, gc, 0.0))
    k_tail = k * jnp.exp(tot - gc)
    tail = jnp.broadcast_to(jnp.exp(tot), (SUBLANES, LANES))
    return u, w, aqk, q * egc, k_tail, tail


def _gdn_intra(qn, kn, vv, g_b, beta_b):
    t_rows = qn.shape[0]
    nc = t_rows // CHUNK
    cb = min(GDN_CB, nc)
    rows = cb * CHUNK
    col = pl.BlockSpec((rows, HEAD_DIM), lambda h, b: (b, h))

    def body(q_ref, k_ref, v_ref, g_ref, b_ref, u_ref, w_ref, a_ref, qd_ref, kt_ref, tl_ref):
        def group(gi, carry):
            cis = [gi * grp + j for j in range(grp)]
            rs = [pl.ds(pl.multiple_of(ci * CHUNK, CHUNK), CHUNK) for ci in cis]
            ins = [(q_ref[r, :], k_ref[r, :], v_ref[r, :], g_ref[r, :], b_ref[r, :]) for r in rs]
            outs = [_gdn_chunk(*x) for x in ins]
            for ci, r, (u, w, aqk, qd, kt, tl) in zip(cis, rs, outs):
                u_ref[r, :] = u
                w_ref[r, :] = w
                a_ref[0, r, :] = aqk
                qd_ref[r, :] = qd
                kt_ref[r, :] = kt
                tl_ref[0, ci] = tl
            return carry

        grp = min(GDN_GROUP, cb)
        lax.fori_loop(0, cb // grp, group, 0)

    full = jax.ShapeDtypeStruct((t_rows, HW), F32)
    return pl.pallas_call(
        body, name="gdn_intra_fwd",
        out_shape=[full, full, jax.ShapeDtypeStruct((HEADS, t_rows, CHUNK), F32), full, full,
                   jax.ShapeDtypeStruct((HEADS, nc, SUBLANES, LANES), F32)],
        grid=(HEADS, nc // cb),
        in_specs=[col] * 5,
        out_specs=[col, col, pl.BlockSpec((1, rows, CHUNK), lambda h, b: (h, b, 0)), col, col,
                   pl.BlockSpec((1, cb, SUBLANES, LANES), lambda h, b: (h, b, 0, 0))],
        compiler_params=_params(("parallel", "parallel")),
    )(qn, kn, vv, g_b, beta_b)


def _gdn_intra_bwd(qn, kn, vv, g_b, beta_b, du, dw, da, dqd, dkt, dtl):
    t_rows = qn.shape[0]
    nc = t_rows // CHUNK
    cb = min(GDN_CB, nc)
    rows = cb * CHUNK
    col = pl.BlockSpec((rows, HEAD_DIM), lambda h, b: (b, h))
    a_spec = pl.BlockSpec((1, rows, CHUNK), lambda h, b: (h, b, 0))
    tl_spec = pl.BlockSpec((1, cb, SUBLANES, LANES), lambda h, b: (h, b, 0, 0))

    def body(q_ref, k_ref, v_ref, g_ref, b_ref, du_ref, dw_ref, da_ref, dqd_ref, dkt_ref, dtl_ref,
             dq_ref, dk_ref, dv_ref, dg_ref, db_ref):
        def group(gi, carry):
            cis = [gi * grp + j for j in range(grp)]
            rs = [pl.ds(pl.multiple_of(ci * CHUNK, CHUNK), CHUNK) for ci in cis]
            ins = [(q_ref[r, :], k_ref[r, :], v_ref[r, :], g_ref[r, :], b_ref[r, :]) for r in rs]
            cts = [(du_ref[r, :], dw_ref[r, :], da_ref[0, r, :], dqd_ref[r, :], dkt_ref[r, :], dtl_ref[0, ci])
                   for ci, r in zip(cis, rs)]
            grads = [jax.vjp(_gdn_chunk, *x)[1](ct) for x, ct in zip(ins, cts)]
            for r, (dq, dk, dv, dg, db) in zip(rs, grads):
                dq_ref[r, :] = dq
                dk_ref[r, :] = dk
                dv_ref[r, :] = dv
                dg_ref[r, :] = dg
                db_ref[r, :] = db
            return carry

        grp = min(GDN_GROUP, cb)
        lax.fori_loop(0, cb // grp, group, 0)

    full = jax.ShapeDtypeStruct((t_rows, HW), F32)
    return pl.pallas_call(
        body, name="gdn_intra_bwd",
        out_shape=[full] * 5,
        grid=(HEADS, nc // cb),
        in_specs=[col] * 7 + [a_spec, col, col, tl_spec],
        out_specs=[col] * 5,
        compiler_params=_params(("parallel", "parallel")),
    )(qn, kn, vv, g_b, beta_b, du, dw, da, dqd, dkt, dtl)


def _head_cols(h):
    return slice(h * HEAD_DIM, (h + 1) * HEAD_DIM)


def _gdn_scan(u, w, aqk, qd, kt, tl):
    t_rows = u.shape[0]
    nc = t_rows // CHUNK
    cb = min(GDN_CB, nc)
    rows = cb * CHUNK
    wide = pl.BlockSpec((rows, HW), lambda b: (b, 0))

    def body(u_ref, w_ref, a_ref, qd_ref, kt_ref, tl_ref, o_ref, s_out_ref, s_ref):
        @pl.when(pl.program_id(0) == 0)
        def _():
            s_ref[...] = jnp.zeros_like(s_ref)

        def chunk(ci, carry):
            r = pl.ds(pl.multiple_of(ci * CHUNK, CHUNK), CHUNK)
            for h in range(HEADS):
                hc = _head_cols(h)
                s = s_ref[h]
                s_out_ref[ci, h] = s
                sb = _bf(s)
                vn = u_ref[r, hc] - _dot(_bf(w_ref[r, hc]), sb)
                vnb = _bf(vn)
                o_ref[r, hc] = _dot(_bf(qd_ref[r, hc]), sb) + _dot(_bf(a_ref[h, r, :]), vnb)
                s_ref[h] = s * tl_ref[h, ci, 0:1, :] + _dot_tn(_bf(kt_ref[r, hc]), vnb)
            return carry

        lax.fori_loop(0, cb, chunk, 0)

    return pl.pallas_call(
        body, name="gdn_scan_fwd",
        out_shape=[jax.ShapeDtypeStruct((t_rows, HW), F32),
                   jax.ShapeDtypeStruct((nc, HEADS, HEAD_DIM, HEAD_DIM), F32)],
        grid=(nc // cb,),
        in_specs=[wide, wide, pl.BlockSpec((HEADS, rows, CHUNK), lambda b: (0, b, 0)), wide, wide,
                  pl.BlockSpec((HEADS, cb, SUBLANES, LANES), lambda b: (0, b, 0, 0))],
        out_specs=[wide, pl.BlockSpec((cb, HEADS, HEAD_DIM, HEAD_DIM), lambda b: (b, 0, 0, 0))],
        scratch_shapes=[pltpu.VMEM((HEADS, HEAD_DIM, HEAD_DIM), F32)],
        compiler_params=_params(("arbitrary",)),
    )(u, w, aqk, qd, kt, tl)


def _gdn_scan_bwd(do, u, w, aqk, qd, kt, tl, states):
    t_rows = u.shape[0]
    nc = t_rows // CHUNK
    cb = min(GDN_CB, nc)
    rows = cb * CHUNK
    nb = nc // cb
    wide = pl.BlockSpec((rows, HW), lambda b: (nb - 1 - b, 0))
    a_spec = pl.BlockSpec((HEADS, rows, CHUNK), lambda b: (0, nb - 1 - b, 0))
    tl_spec = pl.BlockSpec((HEADS, cb, SUBLANES, LANES), lambda b: (0, nb - 1 - b, 0, 0))

    def body(do_ref, u_ref, w_ref, a_ref, qd_ref, kt_ref, tl_ref, s_in_ref,
             du_ref, dw_ref, da_ref, dqd_ref, dkt_ref, dtl_ref, ds_ref):
        @pl.when(pl.program_id(0) == 0)
        def _():
            ds_ref[...] = jnp.zeros_like(ds_ref)

        row0 = lax.broadcasted_iota(jnp.int32, (SUBLANES, LANES), 0) == 0

        def chunk(step, carry):
            ci = cb - 1 - step
            r = pl.ds(pl.multiple_of(ci * CHUNK, CHUNK), CHUNK)
            for h in range(HEADS):
                hc = _head_cols(h)
                s = s_in_ref[ci, h]
                ds_next = ds_ref[h]
                sb, dsb = _bf(s), _bf(ds_next)
                wb, ab, ktb, qdb = _bf(w_ref[r, hc]), _bf(a_ref[h, r, :]), _bf(kt_ref[r, hc]), _bf(qd_ref[r, hc])
                dob = _bf(do_ref[r, hc])
                vn = u_ref[r, hc] - _dot(wb, sb)
                vnb = _bf(vn)
                dvn = _dot_tn(ab, dob) + _dot(ktb, dsb)
                dvnb = _bf(dvn)
                du_ref[r, hc] = dvn
                dw_ref[r, hc] = -_dot_nt(dvnb, sb)
                da_ref[h, r, :] = _dot_nt(dob, vnb)
                dqd_ref[r, hc] = _dot_nt(dob, sb)
                dkt_ref[r, hc] = _dot_nt(vnb, dsb)
                dtl_ref[h, ci] = jnp.where(row0, _colsum(s * ds_next), 0.0)
                ds_ref[h] = _dot_tn(qdb, dob) + ds_next * tl_ref[h, ci, 0:1, :] - _dot_tn(wb, dvnb)
            return carry

        lax.fori_loop(0, cb, chunk, 0)

    full = jax.ShapeDtypeStruct((t_rows, HW), F32)
    return pl.pallas_call(
        body, name="gdn_scan_bwd",
        out_shape=[full, full, jax.ShapeDtypeStruct((HEADS, t_rows, CHUNK), F32), full, full,
                   jax.ShapeDtypeStruct((HEADS, nc, SUBLANES, LANES), F32)],
        grid=(nb,),
        in_specs=[wide, wide, wide, a_spec, wide, wide, tl_spec,
                  pl.BlockSpec((cb, HEADS, HEAD_DIM, HEAD_DIM), lambda b: (nb - 1 - b, 0, 0, 0))],
        out_specs=[wide, wide, a_spec, wide, wide, tl_spec],
        scratch_shapes=[pltpu.VMEM((HEADS, HEAD_DIM, HEAD_DIM), F32)],
        compiler_params=_params(("arbitrary",)),
    )(do, u, w, aqk, qd, kt, tl, states)


def _att_rel_index():
    qi = lax.broadcasted_iota(jnp.int32, (ATT_QB, ATT_KW), 0)
    kj = lax.broadcasted_iota(jnp.int32, (ATT_QB, ATT_KW), 1)
    return jnp.clip(qi - kj + ATT_PAD, -(CHUNK - 1), MAX_REL) + (CHUNK - 1)


def _att_valid(b):
    qi = lax.broadcasted_iota(jnp.int32, (ATT_QB, ATT_KW), 0)
    kj = lax.broadcasted_iota(jnp.int32, (ATT_QB, ATT_KW), 1)
    shift = CHUNK.bit_length() - 1
    qc = jnp.right_shift(qi, shift)
    kc = jnp.right_shift(kj, shift) - LEFT_CHUNKS
    return (kc <= qc) & (kc >= qc - LEFT_CHUNKS) & (kj + b * ATT_QB >= ATT_PAD)


def _att_block(q_raw, k_raw, v, qw, kw, bias, valid):
    q = _rms(q_raw, qw)
    k = _rms(k_raw, kw)
    s = _dot_nt(_bf(q), _bf(k)) * (HEAD_DIM ** -0.5) + bias
    s = jnp.where(valid, s, NEG_INF)
    p = jnp.exp(s - jnp.max(s, axis=-1, keepdims=True))
    p = p / jnp.sum(p, axis=-1, keepdims=True)
    return _dot(_bf(p), _bf(v))


def _att_specs():
    q_spec = pl.BlockSpec((ATT_QB, HEAD_DIM), lambda h, b: (b, h))
    k_specs = [pl.BlockSpec((ATT_QB, HEAD_DIM), lambda h, b, j=j: (b + j, HEADS + h)) for j in range(3)]
    v_specs = [pl.BlockSpec((ATT_QB, HEAD_DIM), lambda h, b, j=j: (b + j, 2 * HEADS + h)) for j in range(3)]
    w_spec = pl.BlockSpec((1, HEAD_DIM), lambda h, b: (0, 0))
    smem = pl.BlockSpec(memory_space=pltpu.SMEM)
    return q_spec, k_specs, v_specs, w_spec, smem


def _att_fill_bias(bias_ref, rel_ref, h):
    idx = _att_rel_index()

    def fill(r, acc):
        return jnp.where(idx == r, rel_ref[h, r], acc)

    bias_ref[...] = lax.fori_loop(0, N_REL, fill, jnp.zeros((ATT_QB, ATT_KW), F32))


def _attention(pb, pbp, qw, kw, rel):
    t_rows = pb.shape[0]
    q_spec, k_specs, v_specs, w_spec, smem = _att_specs()

    def body(q_ref, k0, k1, k2, v0, v1, v2, qw_ref, kw_ref, rel_ref, o_ref, bias_ref):
        h, b = pl.program_id(0), pl.program_id(1)

        @pl.when(b == 0)
        def _():
            _att_fill_bias(bias_ref, rel_ref, h)

        kwin = jnp.concatenate([k0[...], k1[...], k2[...]], axis=0)
        vwin = jnp.concatenate([v0[...], v1[...], v2[...]], axis=0)
        o = _att_block(q_ref[...], kwin, vwin, qw_ref[...], kw_ref[...], bias_ref[...], _att_valid(b))
        o_ref[...] = o.astype(o_ref.dtype)

    return pl.pallas_call(
        body, name="band_attention_fwd",
        out_shape=jax.ShapeDtypeStruct((t_rows, HW), BF16),
        grid=(HEADS, t_rows // ATT_QB),
        in_specs=[q_spec] + k_specs + v_specs + [w_spec, w_spec, smem],
        out_specs=pl.BlockSpec((ATT_QB, HEAD_DIM), lambda h, b: (b, h)),
        scratch_shapes=[pltpu.VMEM((ATT_QB, ATT_KW), F32)],
        compiler_params=_params(("arbitrary", "arbitrary")),
    )(pb, pbp, pbp, pbp, pbp, pbp, pbp, qw, kw, rel)


def _attention_bwd(pb, pbp, qw, kw, rel, dyb):
    t_rows = pb.shape[0]
    nb = t_rows // ATT_QB
    q_spec, k_specs, v_specs, w_spec, smem = _att_specs()
    pad_rows = t_rows + ATT_PAD
    acc_spec = pl.BlockSpec((pad_rows, HEAD_DIM), lambda h, b: (0, h))

    def body(q_ref, k0, k1, k2, v0, v1, v2, qw_ref, kw_ref, rel_ref, do_ref,
             dq_ref, dk_ref, dv_ref, dqw_ref, dkw_ref, drel_ref, bias_ref, dbias_ref):
        h, b = pl.program_id(0), pl.program_id(1)

        @pl.when(b == 0)
        def _():
            _att_fill_bias(bias_ref, rel_ref, h)
            dbias_ref[...] = jnp.zeros_like(dbias_ref)
            dk_ref[...] = jnp.zeros_like(dk_ref)
            dv_ref[...] = jnp.zeros_like(dv_ref)

        @pl.when((b == 0) & (h == 0))
        def _():
            dqw_ref[...] = jnp.zeros_like(dqw_ref)
            dkw_ref[...] = jnp.zeros_like(dkw_ref)

        kwin = jnp.concatenate([k0[...], k1[...], k2[...]], axis=0)
        vwin = jnp.concatenate([v0[...], v1[...], v2[...]], axis=0)
        valid = _att_valid(b)
        _, vjp = jax.vjp(lambda q, k, v, a, c, bias: _att_block(q, k, v, a, c, bias, valid),
                         q_ref[...], kwin, vwin, qw_ref[...], kw_ref[...], bias_ref[...])
        dq, dk, dv, dqw, dkw, dbias = vjp(do_ref[...])
        dq_ref[...] = dq.astype(dq_ref.dtype)
        win = pl.ds(pl.multiple_of(b * ATT_QB, ATT_QB), ATT_KW)
        dk_ref[win, :] += dk
        dv_ref[win, :] += dv
        dqw_ref[...] += dqw
        dkw_ref[...] += dkw
        dbias_ref[...] += dbias

        @pl.when(b == nb - 1)
        def _():
            idx = _att_rel_index()
            tot = dbias_ref[...]

            def reduce(r, carry):
                drel_ref[h, r] = jnp.sum(jnp.where(idx == r, tot, 0.0))
                return carry

            lax.fori_loop(0, N_REL, reduce, 0)

    return pl.pallas_call(
        body, name="band_attention_bwd",
        out_shape=[jax.ShapeDtypeStruct((t_rows, HW), BF16),
                   jax.ShapeDtypeStruct((pad_rows, HW), F32), jax.ShapeDtypeStruct((pad_rows, HW), F32),
                   jax.ShapeDtypeStruct((1, HEAD_DIM), F32), jax.ShapeDtypeStruct((1, HEAD_DIM), F32),
                   jax.ShapeDtypeStruct((HEADS, N_REL), F32)],
        grid=(HEADS, nb),
        in_specs=[q_spec] + k_specs + v_specs + [w_spec, w_spec, smem, q_spec],
        out_specs=[q_spec, acc_spec, acc_spec, w_spec, w_spec, smem],
        scratch_shapes=[pltpu.VMEM((ATT_QB, ATT_KW), F32), pltpu.VMEM((ATT_QB, ATT_KW), F32)],
        compiler_params=_params(("arbitrary", "arbitrary")),
    )(pb, pbp, pbp, pbp, pbp, pbp, pbp, qw, kw, rel, dyb)


def _me():
    return lax.axis_index("x"), lax.axis_index("y"), lax.axis_index("c")


def _index(x, y, c):
    return 4 * x + 2 * y + c


HBM_SPEC = pl.BlockSpec(memory_space=pl.ANY)


def _block(ref, kind, d, r, c):
    if kind == "rows":
        return ref.at[pl.ds(d * r, r), :]
    if kind == "win":
        return ref.at[:, pl.ds(d * WIN_STEP, c)]
    return ref.at[:, pl.ds(d * c, c)]


def _all_gather(shards, kinds):
    n = len(shards)

    def body(*refs):
        x_refs, out_refs = refs[:n], refs[n:2 * n]
        send_sems, recv_sems, local_sems = refs[2 * n:]
        x, y, c = _me()
        me, sibling = (x, y, c), (x, y, 1 - c)
        chips = [(1 - x, y), (x, 1 - y), (1 - x, 1 - y)]

        def copy(i, k, blk, to, src=None):
            r_, c_ = shards[i].shape
            dst = _block(out_refs[i], kinds[i], _index(*blk), r_, c_)
            return pltpu.make_async_remote_copy(
                src_ref=dst if src is None else src, dst_ref=dst,
                send_sem=send_sems.at[i, k], recv_sem=recv_sems.at[i, k], device_id=to, device_id_type=MESH)

        sends, local = [], []
        for i in range(n):
            r_, c_ = shards[i].shape
            mine = pltpu.make_async_copy(x_refs[i], _block(out_refs[i], kinds[i], _index(*me), r_, c_),
                                         local_sems.at[i])
            mine.start()
            local.append(mine)
            first = [copy(i, 0, me, sibling, src=x_refs[i])]
            first += [copy(i, 1 + j, me, (*chip, c), src=x_refs[i]) for j, chip in enumerate(chips)]
            for cp in first:
                cp.start()
            sends += first
        for i in range(n):
            for j, chip in enumerate(chips):
                copy(i, 1 + j, (*chip, c), me).wait_recv()
                passed = copy(i, 4 + j, (*chip, c), sibling)
                passed.start()
                sends.append(passed)
        for i in range(n):
            copy(i, 0, sibling, me).wait_recv()
            for j, chip in enumerate(chips):
                copy(i, 4 + j, (*chip, 1 - c), me).wait_recv()
        for cp in sends:
            cp.wait_send()
        for cp in local:
            cp.wait()

    def full_shape(s, kind):
        r_, c_ = s.shape
        return (N_DEV * r_, c_) if kind == "rows" else (r_, N_DEV * c_)

    return pl.pallas_call(
        body, name="weights_all_gather",
        out_shape=[jax.ShapeDtypeStruct(full_shape(s, k), s.dtype) for s, k in zip(shards, kinds)],
        in_specs=[HBM_SPEC] * n, out_specs=[HBM_SPEC] * n,
        scratch_shapes=[pltpu.SemaphoreType.DMA((n, 7)), pltpu.SemaphoreType.DMA((n, 7)),
                        pltpu.SemaphoreType.DMA((n,))],
        compiler_params=pltpu.CompilerParams(has_side_effects=True),
    )(*shards)


def _exchange(grads, kinds, block_shapes):
    n = len(grads)

    def body(*refs):
        g_refs, out_refs = refs[:n], refs[n:2 * n]
        send_sems, recv_sems, local_sems = refs[2 * n:]
        x, y, c = _me()
        mine = _index(x, y, c)
        copies, local = [], []
        for i in range(n):
            r_, c_ = block_shapes[i]
            cp = pltpu.make_async_copy(_block(g_refs[i], kinds[i], mine, r_, c_), out_refs[i].at[mine],
                                       local_sems.at[i])
            cp.start()
            local.append(cp)
            for k in range(1, N_DEV):
                px, py, pc = x ^ (k >> 2), y ^ ((k >> 1) & 1), c ^ (k & 1)
                copies.append(pltpu.make_async_remote_copy(
                    src_ref=_block(g_refs[i], kinds[i], _index(px, py, pc), r_, c_), dst_ref=out_refs[i].at[mine],
                    send_sem=send_sems.at[i, k - 1], recv_sem=recv_sems.at[i, k - 1],
                    device_id=(px, py, pc), device_id_type=MESH))
        for cp in copies:
            cp.start()
        for cp in copies:
            cp.wait()
        for cp in local:
            cp.wait()

    return pl.pallas_call(
        body, name="grads_exchange",
        out_shape=[jax.ShapeDtypeStruct((N_DEV,) + tuple(shp), g.dtype) for g, shp in zip(grads, block_shapes)],
        in_specs=[HBM_SPEC] * n, out_specs=[HBM_SPEC] * n,
        scratch_shapes=[pltpu.SemaphoreType.DMA((n, 7)), pltpu.SemaphoreType.DMA((n, 7)),
                        pltpu.SemaphoreType.DMA((n,))],
        compiler_params=pltpu.CompilerParams(has_side_effects=True),
    )(*grads)


def _all_reduce_small(vals, name):
    rows, width = vals.shape

    def body(x_ref, out_ref, buf_ref, send_sems, recv_sems):
        x, y, c = _me()
        mine = _index(x, y, c)
        buf_ref[mine] = x_ref[...]
        copies = []
        for k in range(1, N_DEV):
            px, py, pc = x ^ (k >> 2), y ^ ((k >> 1) & 1), c ^ (k & 1)
            copies.append(pltpu.make_async_remote_copy(
                src_ref=x_ref, dst_ref=buf_ref.at[mine],
                send_sem=send_sems.at[k - 1], recv_sem=recv_sems.at[k - 1],
                device_id=(px, py, pc), device_id_type=MESH))
        for cp in copies:
            cp.start()
        for cp in copies:
            cp.wait()
        acc = buf_ref[0]
        for j in range(1, N_DEV):
            acc = acc + buf_ref[j]
        out_ref[...] = acc

    vmem = pl.BlockSpec(memory_space=pltpu.VMEM)
    return pl.pallas_call(
        body, name=name,
        out_shape=jax.ShapeDtypeStruct(vals.shape, F32),
        in_specs=[vmem], out_specs=vmem,
        scratch_shapes=[pltpu.VMEM((N_DEV, rows, width), F32),
                        pltpu.SemaphoreType.DMA((7,)), pltpu.SemaphoreType.DMA((7,))],
        compiler_params=pltpu.CompilerParams(has_side_effects=True),
    )(vals)


def _adamw_math(w, g, m, v):
    m = ADAM_B1 * m + (1.0 - ADAM_B1) * g
    v = ADAM_B2 * v + (1.0 - ADAM_B2) * (g * g)
    m_hat = m / (1.0 - ADAM_B1 ** ADAM_STEP)
    v_hat = v / (1.0 - ADAM_B2 ** ADAM_STEP)
    delta = -ADAM_LR * (m_hat / (jnp.sqrt(v_hat) + ADAM_EPS) + ADAM_WD * w)
    return delta, m, v


ROW_TILE_ELEMS = 384 * 1024


def _row_tile(rows, width):
    best = SUBLANES
    for t in range(SUBLANES, rows + 1, SUBLANES):
        if rows % t == 0 and t * width <= ROW_TILE_ELEMS:
            best = t
    return best


def _sum_received(r_ref):
    g = r_ref[0].astype(F32)
    for j in range(1, N_DEV):
        g = g + r_ref[j].astype(F32)
    return g


def _adamw_recv(recv, w, m, v, name):
    _, rows, width = recv.shape
    tile = _row_tile(rows, width)

    def body(r_ref, w_ref, m_ref, v_ref, g_out, d_out, m_out, v_out):
        g = _sum_received(r_ref)
        d, mn, vn = _adamw_math(w_ref[...], g, m_ref[...], v_ref[...])
        g_out[...] = g
        d_out[...] = d
        m_out[...] = mn
        v_out[...] = vn

    spec = pl.BlockSpec((tile, width), lambda i: (i, 0))
    shape = jax.ShapeDtypeStruct((rows, width), F32)
    return pl.pallas_call(
        body, name=name,
        out_shape=[shape] * 4, grid=(rows // tile,),
        in_specs=[pl.BlockSpec((N_DEV, tile, width), lambda i: (0, i, 0)), spec, spec, spec],
        out_specs=[spec] * 4,
        compiler_params=_params(("parallel",)),
    )(recv, w, m, v)


WIN_STEP = 1408
WIN_W = 1536
IN_SHARD = IN_COLS // N_DEV
IN_PADDED = WIN_STEP * (N_DEV - 1) + WIN_W


def _roll_w_in(shard_padded):
    rows = shard_padded.shape[0]
    tile = _row_tile(rows, WIN_W)

    def body(x_ref, main_ref, edge_ref):
        win = pltpu.roll(x_ref[...], 2 * _index(*_me()), 1).astype(BF16)
        main_ref[...] = win[:, :WIN_STEP]
        edge_ref[...] = win[:, WIN_STEP:]

    return pl.pallas_call(
        body, name="w_in_window",
        out_shape=[jax.ShapeDtypeStruct((rows, WIN_STEP), BF16), jax.ShapeDtypeStruct((rows, WIN_W - WIN_STEP), BF16)],
        grid=(rows // tile,),
        in_specs=[pl.BlockSpec((tile, WIN_W), lambda i: (i, 0))],
        out_specs=[pl.BlockSpec((tile, WIN_STEP), lambda i: (i, 0)),
                   pl.BlockSpec((tile, WIN_W - WIN_STEP), lambda i: (i, 0))],
        compiler_params=_params(("parallel",)),
    )(shard_padded)


def _sum_w_in_windows(recv):
    _, rows, width = recv.shape
    tile = _row_tile(rows, width)

    def body(r_ref, g_out):
        g_out[...] = pltpu.roll(_sum_received(r_ref), width - 2 * _index(*_me()), 1)

    return pl.pallas_call(
        body, name="w_in_grad_sum",
        out_shape=jax.ShapeDtypeStruct((rows, width), F32), grid=(rows // tile,),
        in_specs=[pl.BlockSpec((N_DEV, tile, width), lambda i: (0, i, 0))],
        out_specs=pl.BlockSpec((tile, width), lambda i: (i, 0)),
        compiler_params=_params(("parallel",)),
    )(recv)


def _adamw_small(w, g, m, v, name):
    def fn(i, n, w_, g_, m_, v_):
        return _adamw_math(w_, g_, m_, v_)

    r, c = w.shape
    return _rows(fn, [(w, "t"), (g, "t"), (m, "t"), (v, "t")], [], [(c, F32)] * 3, [], _row_tile(r, c), name)


def _norm_fwd(x, w, name):
    return _rows(lambda i, n, x_, w_: (_rms(x_, w_[...]),), [(x, "t")], [w], [(D_MODEL, BF16)], [], 512, name)[0]


def _residual_norm_fwd(x, y, scale, w, name):
    def fn(i, n, x_, y_, w_):
        xn = x_ + scale * y_
        return xn, _rms(xn, w_[...])

    return _rows(fn, [(x, "t"), (y, "t")], [w], [(D_MODEL, F32), (D_MODEL, BF16)], [], 512, name)


def _residual_norm_bwd(x, w, dhs, dres, scale, name):
    nh = len(dhs)

    def fn(i, n, x_, dres_, *rest):
        dh = rest[0]
        for extra in rest[1:nh]:
            dh = dh + extra
        _, vjp = jax.vjp(_rms, x_, rest[nh][...])
        dx, dw = vjp(dh)
        dx = dx + dres_
        return dx, scale * dx, dw

    return _rows(fn, [(x, "t"), (dres, "t")] + [(d, "t") for d in dhs], [w],
                 [(D_MODEL, F32), (D_MODEL, BF16)], [(1, D_MODEL)], 256, name)


def _ffn_fwd(h, w_gu, w_down, tag):
    gu = _matmul(h, w_gu, "nn", F32, tag + "_gu")
    act = _rows(lambda i, n, gu_: (_swiglu(gu_),), [(gu, "t")], [], [(D_FF, BF16)], [], 128, tag + "_swiglu")[0]
    y = _matmul(act, w_down, "nn", F32, tag + "_down")
    return gu, act, y


def _ffn_bwd(h, gu, act, dy, w_gu, w_down, tag):
    d_w_down = _matmul(act, dy, "tn", BF16, tag + "_d_w_down")
    dact = _matmul(dy, w_down, "nt", F32, tag + "_dact")

    def fn(i, n, gu_, dact_):
        _, vjp = jax.vjp(_swiglu, gu_)
        return vjp(dact_)

    dgu = _rows(fn, [(gu, "t"), (dact, "t")], [], [(2 * D_FF, BF16)], [], 128, tag + "_swiglu_bwd")[0]
    d_w_gu = _matmul(h, dgu, "tn", BF16, tag + "_d_w_gu")
    dh = _matmul(dgu, w_gu, "nt", F32, tag + "_dh")
    return dh, d_w_gu, d_w_down


def _expanders():
    e_g = np.zeros((LANES, HW), np.float32)
    e_b = np.zeros((LANES, HW), np.float32)
    for h in range(HEADS):
        e_g[h, h * HEAD_DIM:(h + 1) * HEAD_DIM] = 1.0
        e_b[HEADS + h, h * HEAD_DIM:(h + 1) * HEAD_DIM] = 1.0
    return jnp.asarray(e_g), jnp.asarray(e_b)


def _pad_lanes(v):
    return jnp.pad(v, ((0, 0), (0, LANES - v.shape[1])))


def _local_step(x, p, tgt, small, big):
    w_in = big["w_in"]
    w_qz = w_in[:, :IN_QZ]
    w_ab = jnp.pad(w_in[:, IN_AB0:IN_QKVB0], ((0, 0), (0, LANES - 2 * HEADS)))
    w_qkvb = w_in[:, IN_QKVB0:IN_GG0]
    w_gg = w_in[:, IN_GG0:IN_COLS]
    e_g, e_b = _expanders()
    alog, dtb = _pad_lanes(small["a_log"]), _pad_lanes(small["dt_bias"])
    conv_w = jnp.pad(small["conv_w"], ((0, SUBLANES - CONV_K), (0, 0)))
    rel = small["rel_bias"]

    h1 = _norm_fwd(x, small["ffn1_norm"], "ffn1_norm")
    gu1, act1, y1 = _ffn_fwd(h1, big["ffn1_w_gu"], big["ffn1_w_down"], "ffn1")
    x1, h2 = _residual_norm_fwd(x, y1, 0.5, small["mix_norm"], "mix_norm")

    qz = _matmul(h2, w_qz, "nn", F32, "in_qz")
    ab = _matmul(h2, w_ab, "nn", F32, "in_ab")
    pb = _matmul(h2, w_qkvb, "nn", F32, "in_qkvb")
    gg = _matmul(h2, w_gg, "nn", F32, "in_gates")
    pa, z = qz[:, :3 * HW], qz[:, 3 * HW:]

    def prep(i, n, pa_, prev_, ab_, cw_, alog_, dtb_, eg_, eb_):
        q, k, v = _gdn_post(_conv(pa_, prev_, cw_, i))
        g_b, beta_b = _gdn_gates(ab_, alog_[...], dtb_[...], eg_[...], eb_[...])
        return q, k, v, g_b, beta_b

    qn, kn, vv, g_b, beta_b = _rows(prep, [(pa, "t"), (pa, "p"), (ab, "t")], [conv_w, alog, dtb, e_g, e_b],
                                    [(HW, F32)] * 5, [], 256, "gdn_prep")
    u, w, aqk, qd, kt, tl = _gdn_intra(qn, kn, vv, g_b, beta_b)
    o, states = _gdn_scan(u, w, aqk, qd, kt, tl)
    ya = _rows(lambda i, n, o_, z_, w_: (_gated_norm(o_, z_, w_[...]),), [(o, "t"), (z, "t")], [small["gdn_norm"]],
               [(HW, BF16)], [], 512, "gdn_gated_norm")[0]

    pbp = jnp.pad(pb, ((ATT_PAD, 0), (0, 0)))
    yb = _attention(pb, pbp, small["q_norm"], small["k_norm"], rel)

    ta = _matmul(ya, big["w_branch_a"], "nn", F32, "branch_a")
    tb = _matmul(yb, big["w_branch_b"], "nn", F32, "branch_b")
    mixed = _rows(lambda i, n, gg_, ta_, tb_: (_mix(gg_, ta_, tb_),), [(gg, "t"), (ta, "t"), (tb, "t")], [],
                  [(D_MODEL, BF16)], [], 256, "mix")[0]
    m_out = _matmul(mixed, big["w_out"], "nn", F32, "w_out")
    x2, h3 = _residual_norm_fwd(x1, m_out, 1.0, small["ffn2_norm"], "ffn2_norm")
    gu2, act2, y2 = _ffn_fwd(h3, big["ffn2_w_gu"], big["ffn2_w_down"], "ffn2")
    x3, h4 = _residual_norm_fwd(x2, y2, 0.5, small["ple_norm"], "ple_norm")
    gp = _matmul(h4, big["ple_gate"], "nn", F32, "ple_gate")
    pp = _matmul(p, big["ple_proj"], "nn", F32, "ple_proj")

    def head(i, n, x3_, gp_, pp_, tgt_):
        sg = _sigmoid(gp_)
        err = x3_ + sg * pp_ - tgt_
        dx4 = err * (1.0 / D_MODEL)
        sq = _colsum(err * err)
        part = sq[:, :LANES]
        for j in range(1, D_MODEL // LANES):
            part = part + sq[:, j * LANES:(j + 1) * LANES]
        return dx4, dx4 * pp_ * sg * (1.0 - sg), dx4 * sg, (0.5 / D_MODEL) * part

    dx4, dgp, dpp, loss_lanes = _rows(head, [(x3, "t"), (gp, "t"), (pp, "t"), (tgt, "t")], [],
                                      [(D_MODEL, F32), (D_MODEL, BF16), (D_MODEL, BF16)], [(1, LANES)], 256,
                                      "ple_loss_head")
    loss = jnp.sum(loss_lanes)

    gbig, gsmall = {}, {}
    gbig["ple_proj"] = _matmul(p, dpp, "tn", BF16, "d_ple_proj")
    gbig["ple_gate"] = _matmul(h4, dgp, "tn", BF16, "d_ple_gate")
    dh4 = _matmul(dgp, big["ple_gate"], "nt", F32, "ple_gate_dh")
    dx3, dy2, gsmall["ple_norm"] = _residual_norm_bwd(x3, small["ple_norm"], [dh4], dx4, 0.5, "ple_norm_bwd")

    dh3, gbig["ffn2_w_gu"], gbig["ffn2_w_down"] = _ffn_bwd(h3, gu2, act2, dy2, big["ffn2_w_gu"],
                                                           big["ffn2_w_down"], "ffn2")
    dx2, dx2b, gsmall["ffn2_norm"] = _residual_norm_bwd(x2, small["ffn2_norm"], [dh3], dx3, 1.0, "ffn2_norm_bwd")

    gbig["w_out"] = _matmul(mixed, dx2b, "tn", BF16, "d_w_out")
    dmixed = _matmul(dx2b, big["w_out"], "nt", F32, "w_out_dx")

    def mix_bwd(i, n, gg_, ta_, tb_, dm_):
        _, vjp = jax.vjp(_mix, gg_, ta_, tb_)
        return vjp(dm_)

    dgg, dta, dtb_ = _rows(mix_bwd, [(gg, "t"), (ta, "t"), (tb, "t"), (dmixed, "t")], [],
                           [(2 * D_MODEL, BF16), (D_MODEL, BF16), (D_MODEL, BF16)], [], 256, "mix_bwd")
    gbig["w_branch_a"] = _matmul(ya, dta, "tn", BF16, "d_branch_a")
    gbig["w_branch_b"] = _matmul(yb, dtb_, "tn", BF16, "d_branch_b")
    dya = _matmul(dta, big["w_branch_a"], "nt", F32, "branch_a_dx")
    dyb = _matmul(dtb_, big["w_branch_b"], "nt", F32, "branch_b_dx")

    dq_b, dk_b, dv_b, gsmall["q_norm"], gsmall["k_norm"], gsmall["rel_bias"] = _attention_bwd(
        pb, pbp, small["q_norm"], small["k_norm"], rel, dyb)
    dpb = jnp.concatenate([dq_b, dk_b[ATT_PAD:].astype(BF16), dv_b[ATT_PAD:].astype(BF16)], axis=1)

    def gated_bwd(i, n, o_, z_, dya_, w_):
        _, vjp = jax.vjp(_gated_norm, o_, z_, w_[...])
        return vjp(dya_)

    do, dz, gsmall["gdn_norm"] = _rows(gated_bwd, [(o, "t"), (z, "t"), (dya, "t")], [small["gdn_norm"]],
                                       [(HW, F32), (HW, BF16)], [(1, HEAD_DIM)], 256, "gdn_gated_norm_bwd")
    du, dw, da, dqd, dkt, dtl = _gdn_scan_bwd(do, u, w, aqk, qd, kt, tl, states)
    dqn, dkn, dvv, dg_b, dbeta_b = _gdn_intra_bwd(qn, kn, vv, g_b, beta_b, du, dw, da, dqd, dkt, dtl)

    def prep_bwd(i, n, pa_, prev_, ab_, dq_, dk_, dv_, dg_, db_, cw_, alog_, dtb_, eg_, eb_):
        _, vjp = jax.vjp(_gdn_post, _conv(pa_, prev_, cw_, i))
        (dy,) = vjp((dq_, dk_, dv_))
        e_g_, e_b_ = eg_[...], eb_[...]
        _, vjp_g = jax.vjp(lambda a, b, c: _gdn_gates(a, b, c, e_g_, e_b_), ab_, alog_[...], dtb_[...])
        dab, dalog, ddtb = vjp_g((dg_, db_))
        return dy, dab, dalog, ddtb

    dy_conv, dab, dalog, ddtb = _rows(
        prep_bwd, [(pa, "t"), (pa, "p"), (ab, "t"), (dqn, "t"), (dkn, "t"), (dvv, "t"), (dg_b, "t"), (dbeta_b, "t")],
        [conv_w, alog, dtb, e_g, e_b], [(3 * HW, F32), (LANES, BF16)], [(1, LANES), (1, LANES)], 256,
        "gdn_prep_bwd")
    gsmall["a_log"] = dalog[:, :HEADS]
    gsmall["dt_bias"] = ddtb[:, :HEADS]

    def conv_bwd(i, n, dy_, nxt_, pa_, prev_, cw_):
        dpa = dy_ * cw_[CONV_K - 1:CONV_K, :]
        row = lax.broadcasted_iota(jnp.int32, (SUBLANES, dy_.shape[1]), 0)
        dcw = jnp.where(row == CONV_K - 1, _colsum(dy_ * pa_), 0.0)
        for j in range(CONV_K - 1):
            s = CONV_K - 1 - j
            dpa = dpa + _shift_up(dy_, nxt_, s, i, n) * cw_[j:j + 1, :]
            dcw = dcw + jnp.where(row == j, _colsum(dy_ * _shift_down(pa_, prev_, s, i)), 0.0)
        return dpa, dcw

    dpa, dcw = _rows(conv_bwd, [(dy_conv, "t"), (dy_conv, "n"), (pa, "t"), (pa, "p")], [conv_w],
                     [(3 * HW, BF16)], [(SUBLANES, 3 * HW)], 256, "gdn_conv_bwd")
    gsmall["conv_w"] = dcw[:CONV_K]

    dqz = jnp.concatenate([dpa, dz], axis=1)
    d_w_qz = _matmul(h2, dqz, "tn", BF16, "d_in_qz")
    d_w_ab = _matmul(h2, dab, "tn", BF16, "d_in_ab")
    d_w_qkvb = _matmul(h2, dpb, "tn", BF16, "d_in_qkvb")
    d_w_gg = _matmul(h2, dgg, "tn", BF16, "d_in_gates")
    gbig["w_in"] = jnp.concatenate([d_w_qz, d_w_ab[:, :2 * HEADS], d_w_qkvb, d_w_gg,
                                    jnp.zeros((D_MODEL, IN_PADDED - IN_COLS), BF16)], axis=1)
    dh2 = [_matmul(dqz, w_qz, "nt", F32, "in_qz_dh"), _matmul(dab, w_ab, "nt", F32, "in_ab_dh"),
           _matmul(dpb, w_qkvb, "nt", F32, "in_qkvb_dh"), _matmul(dgg, w_gg, "nt", F32, "in_gates_dh")]
    dx1, dy1, gsmall["mix_norm"] = _residual_norm_bwd(x1, small["mix_norm"], dh2, dx2, 0.5, "mix_norm_bwd")

    dh1, gbig["ffn1_w_gu"], gbig["ffn1_w_down"] = _ffn_bwd(h1, gu1, act1, dy1, big["ffn1_w_gu"],
                                                           big["ffn1_w_down"], "ffn1")
    grad_x, _, gsmall["ffn1_norm"] = _residual_norm_bwd(x, small["ffn1_norm"], [dh1], dx1, 1.0, "ffn1_norm_bwd")
    return loss, grad_x, gsmall, gbig


def _gather_weights(wts):
    shards, kinds, names = [], [], []
    for n in BIG:
        if n == "w_in":
            continue
        shards.append(wts[n].astype(BF16))
        kinds.append("cols" if n in COL_SHARDED else "rows")
        names.append(n)
    main, edge = _roll_w_in(jnp.pad(wts["w_in"], ((0, 0), (0, WIN_W - IN_SHARD))))
    shards += [main, edge]
    kinds += ["cols", "cols"]
    full = dict(zip(names + ["w_in_main", "w_in_edge"], _all_gather(shards, kinds)))
    edge_w = WIN_W - WIN_STEP
    w_in = jnp.pad(full.pop("w_in_main"), ((0, 0), (0, edge_w)))
    edges = full.pop("w_in_edge")
    for d in range(N_DEV):
        at = WIN_STEP * (d + 1)
        w_in = w_in + jnp.pad(edges[:, d * edge_w:(d + 1) * edge_w], ((0, 0), (at, IN_PADDED - at - edge_w)))
    full["w_in"] = w_in
    return full


SMALL_ROWS = ("ffn1_norm", "mix_norm", "ffn2_norm", "ple_norm", "gdn_norm", "q_norm", "k_norm", "a_log", "dt_bias",
              "rel_bias", "conv_w")


def _pack_small(vals):
    rows = []
    for n in SMALL_ROWS:
        v = vals[n]
        if n == "rel_bias":
            v = jnp.pad(v, ((0, 0), (0, 2 * LANES - N_REL)))
        elif n in ("a_log", "dt_bias"):
            v = _pad_lanes(v)
        rows.append(v.reshape(-1, LANES))
    packed = jnp.concatenate(rows, axis=0)
    return jnp.pad(packed, ((0, -packed.shape[0] % SUBLANES), (0, 0)))


def _unpack_small(packed, shapes):
    out, off = {}, 0
    for n in SMALL_ROWS:
        shp = shapes[n]
        if n == "rel_bias":
            out[n] = packed[off:off + 2 * HEADS].reshape(HEADS, 2 * LANES)[:, :N_REL]
            off += 2 * HEADS
        elif n in ("a_log", "dt_bias"):
            out[n] = packed[off:off + 1, :HEADS]
            off += 1
        else:
            r = int(np.prod(shp)) // LANES
            out[n] = packed[off:off + r].reshape(shp)
            off += r
    return out


WEIGHTS = ("ffn1_norm", "ffn1_w_gu", "ffn1_w_down", "mix_norm", "w_in", "conv_w", "a_log", "dt_bias", "gdn_norm",
           "q_norm", "k_norm", "rel_bias", "w_branch_a", "w_branch_b", "w_out", "ffn2_norm", "ffn2_w_gu",
           "ffn2_w_down", "ple_norm", "ple_gate", "ple_proj")


def kernel(x, p, ffn1_norm, ffn1_w_gu, ffn1_w_down, mix_norm, w_in, conv_w, a_log, dt_bias, gdn_norm, q_norm, k_norm, rel_bias, w_branch_a, w_branch_b, w_out, ffn2_norm, ffn2_w_gu, ffn2_w_down, ple_norm, ple_gate, ple_proj, loss_target, m_ffn1_norm, m_ffn1_w_gu, m_ffn1_w_down, m_mix_norm, m_w_in, m_conv_w, m_a_log, m_dt_bias, m_gdn_norm, m_q_norm, m_k_norm, m_rel_bias, m_w_branch_a, m_w_branch_b, m_w_out, m_ffn2_norm, m_ffn2_w_gu, m_ffn2_w_down, m_ple_norm, m_ple_gate, m_ple_proj, v_ffn1_norm, v_ffn1_w_gu, v_ffn1_w_down, v_mix_norm, v_w_in, v_conv_w, v_a_log, v_dt_bias, v_gdn_norm, v_q_norm, v_k_norm, v_rel_bias, v_w_branch_a, v_w_branch_b, v_w_out, v_ffn2_norm, v_ffn2_w_gu, v_ffn2_w_down, v_ple_norm, v_ple_gate, v_ple_proj):
    args = dict(locals())
    def layer0(v):
        return v[0] if v.ndim == 3 else v

    wts = {n: layer0(args[n]) for n in WEIGHTS}
    mom = {n: layer0(args["m_" + n]) for n in WEIGHTS}
    var = {n: layer0(args["v_" + n]) for n in WEIGHTS}
    x2d, p2d, tgt = x[0], p[0, 0], loss_target[0]
    my_index = _index(*_me())

    big = _gather_weights(wts)

    small = {n: wts[n] for n in SMALL_ROWS if n != "conv_w"}
    conv_shard = wts["conv_w"]
    conv_cols = conv_shard.shape[1]
    conv_packed = jnp.zeros((SUBLANES, N_DEV * conv_cols), F32)
    conv_packed = lax.dynamic_update_slice(conv_packed, jnp.pad(conv_shard, ((0, SUBLANES - CONV_K), (0, 0))),
                                           (0, my_index * conv_cols))
    small["conv_w"] = _all_reduce_small(conv_packed.reshape(-1, LANES), "conv_w_gather").reshape(SUBLANES, -1)[:CONV_K]

    loss, grad_x, gsmall, gbig = _local_step(x2d, p2d, tgt, small, big)
    loss = lax.psum(loss, ("x", "y", "c"))

    kinds = ["win" if n == "w_in" else "cols" if n in COL_SHARDED else "rows" for n in BIG]
    blocks = [(wts[n].shape[0], WIN_W) if n == "w_in" else wts[n].shape for n in BIG]
    recv = dict(zip(BIG, _exchange([gbig[n] for n in BIG], kinds, blocks)))
    outs_big = {}
    for n in BIG:
        if n == "w_in":
            g_in = _sum_w_in_windows(recv[n])[:, :IN_SHARD]
            outs_big[n] = [g_in] + list(_adamw_small(wts[n], g_in, mom[n], var[n], "adamw_w_in"))
        else:
            outs_big[n] = _adamw_recv(recv[n], wts[n], mom[n], var[n], "adamw_" + n)

    small_shapes = {n: (small[n].shape if n != "conv_w" else (CONV_K, N_DEV * conv_cols)) for n in SMALL_ROWS}
    gsum = _unpack_small(_all_reduce_small(_pack_small(gsmall), "small_grads_all_reduce"), small_shapes)
    gsum["conv_w"] = lax.dynamic_slice(gsum["conv_w"], (0, my_index * conv_cols), (CONV_K, conv_cols))
    rep = [n for n in SMALL_ROWS if n != "conv_w"]
    rep_shapes = {n: small_shapes[n] for n in rep}

    def pack_rep(vals):
        return _pack_small({**{n: vals[n] for n in rep}, "conv_w": jnp.zeros((CONV_K, LANES), F32)})

    def unpack_rep(packed):
        return _unpack_small(packed, {**rep_shapes, "conv_w": (CONV_K, LANES)})

    outs_small = [unpack_rep(o) for o in _adamw_small(pack_rep(wts), pack_rep(gsum), pack_rep(mom), pack_rep(var),
                                                      "adamw_replicated")]
    pad8 = functools.partial(jnp.pad, pad_width=((0, SUBLANES - CONV_K), (0, 0)))
    outs_conv = [o[:CONV_K] for o in _adamw_small(pad8(conv_shard), pad8(gsum["conv_w"]), pad8(mom["conv_w"]),
                                                   pad8(var["conv_w"]), "adamw_conv")]

    def leaf(kind, n):
        if n in BIG:
            return outs_big[n][kind][None]
        if n == "conv_w":
            return (gsum["conv_w"] if kind == 0 else outs_conv[kind - 1])[None]
        return (gsum[n] if kind == 0 else outs_small[kind - 1][n]).reshape(args[n].shape)

    result = [loss, grad_x[None]]
    for kind in range(4):
        result += [leaf(kind, n) for n in WEIGHTS]
    return tuple(result)
```

```python
import functools

import numpy as np
import jax
import jax.numpy as jnp
from jax import lax
from jax.experimental import pallas as pl
from jax.experimental.pallas import tpu as pltpu

F32 = jnp.float32
BF16 = jnp.bfloat16
HIGHEST = lax.Precision.HIGHEST
MESH = pl.DeviceIdType.MESH

D_MODEL = 2048
D_FF = 5632
HEADS = 8
HEAD_DIM = 128
HW = HEADS * HEAD_DIM
CHUNK = 64
LEFT_CHUNKS = 8
MAX_REL = 128
N_REL = (CHUNK - 1) + MAX_REL + 1
CONV_K = 4
EPS = 1e-6
NEG_INF = -1e30
N_DEV = 8
LANES = 128
SUBLANES = 8
VMEM_LIMIT = 56 * 1024 * 1024

ATT_QB = 256
ATT_KW = ATT_QB + LEFT_CHUNKS * CHUNK
ATT_PAD = LEFT_CHUNKS * CHUNK
GDN_CB = 8
GDN_GROUP = 4

ADAM_LR = 0.001
ADAM_B1 = 0.9
ADAM_B2 = 0.999
ADAM_EPS = 1e-08
ADAM_WD = 0.01
ADAM_STEP = 10

IN_QZ = 3 * HW + HW
IN_AB0 = IN_QZ
IN_QKVB0 = IN_AB0 + 2 * HEADS
IN_GG0 = IN_QKVB0 + 3 * HW
IN_COLS = IN_GG0 + 2 * D_MODEL

BIG = ("ffn1_w_gu", "ffn1_w_down", "w_in", "w_branch_a", "w_branch_b", "w_out",
       "ffn2_w_gu", "ffn2_w_down", "ple_gate", "ple_proj")
COL_SHARDED = ("ffn1_w_gu", "w_in", "w_branch_a", "w_branch_b", "ffn2_w_gu", "ple_proj")


def _params(semantics=None, **kw):
    return pltpu.CompilerParams(dimension_semantics=semantics, vmem_limit_bytes=VMEM_LIMIT, **kw)


def _pick(n, cands):
    for c in cands:
        if n % c == 0:
            return c
    return n


def _matmul(a, b, mode, out_dtype, name):
    if mode == "nn":
        (m, k), (k2, n) = a.shape, b.shape
    elif mode == "nt":
        (m, k), (n, k2) = a.shape, b.shape
    else:
        (k, m), (k2, n) = a.shape, b.shape
    assert k == k2, (a.shape, b.shape, mode)
    tm = _pick(m, (1024, 512, 256, 128))
    tn = _pick(n, (1024, 512, 256, 128))
    tk = _pick(k, (1024, 512, 256, 128))
    nk = k // tk
    if mode == "nn":
        a_spec = pl.BlockSpec((tm, tk), lambda i, j, kk: (i, kk))
        b_spec = pl.BlockSpec((tk, tn), lambda i, j, kk: (kk, j))
        dims = (((1,), (0,)), ((), ()))
    elif mode == "nt":
        a_spec = pl.BlockSpec((tm, tk), lambda i, j, kk: (i, kk))
        b_spec = pl.BlockSpec((tn, tk), lambda i, j, kk: (j, kk))
        dims = (((1,), (1,)), ((), ()))
    else:
        a_spec = pl.BlockSpec((tk, tm), lambda i, j, kk: (kk, i))
        b_spec = pl.BlockSpec((tk, tn), lambda i, j, kk: (kk, j))
        dims = (((0,), (0,)), ((), ()))

    def body(a_ref, b_ref, o_ref, acc_ref):
        kk = pl.program_id(2)

        @pl.when(kk == 0)
        def _():
            acc_ref[...] = jnp.zeros_like(acc_ref)

        acc_ref[...] += lax.dot_general(a_ref[...].astype(BF16), b_ref[...].astype(BF16), dims,
                                        preferred_element_type=F32)

        @pl.when(kk == nk - 1)
        def _():
            o_ref[...] = acc_ref[...].astype(o_ref.dtype)

    return pl.pallas_call(
        body, name=name,
        out_shape=jax.ShapeDtypeStruct((m, n), out_dtype),
        grid=(m // tm, n // tn, nk),
        in_specs=[a_spec, b_spec],
        out_specs=pl.BlockSpec((tm, tn), lambda i, j, kk: (i, j)),
        scratch_shapes=[pltpu.VMEM((tm, tn), F32)],
        compiler_params=_params(("parallel", "parallel", "arbitrary")),
    )(a, b)


def _rows(fn, row_ins, consts, row_outs, acc_outs, tile, name):
    t_rows = row_ins[0][0].shape[0]
    tile = min(tile, t_rows)
    assert t_rows % tile == 0 and tile % SUBLANES == 0
    n = t_rows // tile
    per = tile // SUBLANES
    last8 = t_rows // SUBLANES - 1
    in_specs = []
    for arr, kind in row_ins:
        c = arr.shape[1]
        if kind == "t":
            in_specs.append(pl.BlockSpec((tile, c), lambda i: (i, 0)))
        elif kind == "p":
            in_specs.append(pl.BlockSpec((SUBLANES, c), lambda i: (jnp.maximum(i * per - 1, 0), 0)))
        else:
            in_specs.append(pl.BlockSpec((SUBLANES, c), lambda i: (jnp.minimum((i + 1) * per, last8), 0)))
    for arr in consts:
        in_specs.append(pl.BlockSpec(arr.shape, lambda i, nd=arr.ndim: (0,) * nd))
    out_shape = [jax.ShapeDtypeStruct((t_rows, c), dt) for c, dt in row_outs]
    out_specs = [pl.BlockSpec((tile, c), lambda i: (i, 0)) for c, _ in row_outs]
    for shp in acc_outs:
        out_shape.append(jax.ShapeDtypeStruct(shp, F32))
        out_specs.append(pl.BlockSpec(shp, lambda i, nd=len(shp): (0,) * nd))
    n_in = len(row_ins) + len(consts)
    n_row_out = len(row_outs)

    def body(*refs):
        i = pl.program_id(0)
        vals = [r[...] for r in refs[:len(row_ins)]]
        res = fn(i, n, *vals, *refs[len(row_ins):n_in])
        outs = refs[n_in:]
        for r, v in zip(outs[:n_row_out], res[:n_row_out]):
            r[...] = v.astype(r.dtype)
        if acc_outs:
            @pl.when(i == 0)
            def _():
                for r in outs[n_row_out:]:
                    r[...] = jnp.zeros_like(r)

            for r, v in zip(outs[n_row_out:], res[n_row_out:]):
                r[...] += v

    res = pl.pallas_call(
        body, name=name, out_shape=out_shape, grid=(n,), in_specs=in_specs, out_specs=out_specs,
        compiler_params=_params(("arbitrary",) if acc_outs else ("parallel",)),
    )(*[a for a, _ in row_ins], *consts)
    return res


def _rms(x, w):
    return x * lax.rsqrt(jnp.mean(x * x, axis=-1, keepdims=True) + EPS) * w


def _l2n(x):
    return x * lax.rsqrt(jnp.sum(x * x, axis=-1, keepdims=True) + EPS)


def _sigmoid(x):
    return 1.0 / (1.0 + jnp.exp(-x))


def _silu(x):
    return x * _sigmoid(x)


def _softplus(x):
    return jnp.maximum(x, 0.0) + jnp.log(1.0 + jnp.exp(-jnp.abs(x)))


def _heads(fn, *xs):
    nh = xs[0].shape[1] // HEAD_DIM
    return jnp.concatenate(
        [fn(*[x[:, h * HEAD_DIM:(h + 1) * HEAD_DIM] for x in xs]) for h in range(nh)], axis=1)


def _colsum(x):
    return jnp.sum(x, axis=0, keepdims=True)


def _swiglu(gu):
    return _silu(gu[:, :D_FF]) * gu[:, D_FF:]


def _gated_norm(o, z, w):
    return _heads(lambda oh, zh: _rms(oh, w) * _silu(zh), o, z)


def _mix(gg, ta, tb):
    return _sigmoid(gg[:, :D_MODEL]) * ta + _sigmoid(gg[:, D_MODEL:]) * tb


def _gdn_post(y):
    a = _silu(y)
    q = _heads(lambda v: _l2n(v) * (HEAD_DIM ** -0.5), a[:, :HW])
    k = _heads(_l2n, a[:, HW:2 * HW])
    return q, k, a[:, 2 * HW:]


NN = (((1,), (0,)), ((), ()))
NT = (((1,), (1,)), ((), ()))
TN = (((0,), (0,)), ((), ()))


def _dg(a, b, dims):
    return lax.dot_general(a, b, dims, preferred_element_type=F32)


def _split2(x):
    hi = x.astype(BF16)
    return hi, (x - hi.astype(F32)).astype(BF16)


def _split3(x):
    hi = x.astype(BF16)
    r = x - hi.astype(F32)
    mid = r.astype(BF16)
    return hi, mid, (r - mid.astype(F32)).astype(BF16)


def _dg3(a, b, dims):
    ah, al = _split2(a)
    bh, bl = _split2(b)
    return _dg(ah, bh, dims) + (_dg(ah, bl, dims) + _dg(al, bh, dims))


@jax.custom_vjp
def _mm3(a, b):
    return _dg3(a, b, NN)


_mm3.defvjp(lambda a, b: (_dg3(a, b, NN), (a, b)),
            lambda res, g: (_dg3(g, res[1], NT), _dg3(res[0], g, TN)))


def _xm(x, m, dims):
    mb = m.astype(BF16)
    parts = _split3(x)
    return _dg(parts[0], mb, dims) + (_dg(parts[1], mb, dims) + _dg(parts[2], mb, dims))


def _mx(m, x, dims):
    mb = m.astype(BF16)
    parts = _split3(x)
    return _dg(mb, parts[0], dims) + (_dg(mb, parts[1], dims) + _dg(mb, parts[2], dims))


@jax.custom_vjp
def _times_const(x, m):
    return _xm(x, m, NN)


_times_const.defvjp(lambda x, m: (_xm(x, m, NN), m),
                    lambda m, g: (_xm(g, m, NT), jnp.zeros_like(m)))


@jax.custom_vjp
def _const_times(m, x):
    return _mx(m, x, NN)


_const_times.defvjp(lambda m, x: (_mx(m, x, NN), m),
                    lambda m, g: (jnp.zeros_like(m), _mx(m, g, TN)))


@jax.custom_vjp
def _lane_mean_cols(x, avg):
    return _mx(avg, x, NT)


_lane_mean_cols.defvjp(lambda x, avg: (_mx(avg, x, NT), avg),
                       lambda avg, g: (_xm(g, avg, TN), jnp.zeros_like(avg)))


def _gdn_gates(ab, alog, dtb, e_g, e_b):
    t = ab.shape[0]
    g = -jnp.exp(alog) * _softplus(ab + dtb)
    beta = _sigmoid(ab)
    ri = lax.broadcasted_iota(jnp.int32, (t, t), 0)
    ci = lax.broadcasted_iota(jnp.int32, (t, t), 1)
    shift = CHUNK.bit_length() - 1
    same = jnp.right_shift(ri, shift) == jnp.right_shift(ci, shift)
    tril = jnp.where(same & (ri >= ci), 1.0, 0.0).astype(F32)
    gc = _const_times(tril, g)
    return _times_const(gc, e_g), _times_const(beta, e_b)


def _shift_down(x, halo, s, i):
    if s == 0:
        return x
    halo = jnp.where(i == 0, 0.0, halo)
    xr = pltpu.roll(x, s, 0)
    hr = pltpu.roll(halo, s, 0)
    row = lax.broadcasted_iota(jnp.int32, (SUBLANES, x.shape[1]), 0)
    top = jnp.where(row < s, hr, xr[:SUBLANES])
    return jnp.concatenate([top, xr[SUBLANES:]], axis=0)


def _shift_up(x, halo, s, i, n):
    if s == 0:
        return x
    t = x.shape[0]
    halo = jnp.where(i == n - 1, 0.0, halo)
    xr = pltpu.roll(x, t - s, 0)
    hr = pltpu.roll(halo, SUBLANES - s, 0)
    row = lax.broadcasted_iota(jnp.int32, (SUBLANES, x.shape[1]), 0)
    bot = jnp.where(row >= SUBLANES - s, hr, xr[t - SUBLANES:])
    return jnp.concatenate([xr[:t - SUBLANES], bot], axis=0)


def _conv(pa, prev, cw_ref, i):
    y = pa * cw_ref[CONV_K - 1:CONV_K, :]
    for j in range(CONV_K - 1):
        y = y + _shift_down(pa, prev, CONV_K - 1 - j, i) * cw_ref[j:j + 1, :]
    return y


def _dot_nt(a, b, precision=None):
    return lax.dot_general(a, b, (((1,), (1,)), ((), ())), precision=precision, preferred_element_type=F32)


def _dot_tn(a, b, precision=None):
    return lax.dot_general(a, b, (((0,), (0,)), ((), ())), precision=precision, preferred_element_type=F32)


def _dot(a, b, precision=None):
    return jnp.dot(a, b, precision=precision, preferred_element_type=F32)


def _bf(x):
    return x.astype(BF16)


def _gdn_chunk(q, k, v, gc, bb):
    c = q.shape[0]
    ri = lax.broadcasted_iota(jnp.int32, (c, c), 0)
    ci = lax.broadcasted_iota(jnp.int32, (c, c), 1)
    incl = ri >= ci
    strict = ri > ci
    g_row = gc[:, :c]
    g_col = _lane_mean_cols(gc, jnp.full((c, LANES), 1.0 / LANES, F32))
    decay = jnp.where(incl, jnp.exp(jnp.where(incl, g_row - g_col, 0.0)), 0.0)
    kb = k * bb
    lmat = jnp.where(strict, _dot_nt(_bf(kb), _bf(k)) * decay, 0.0)
    eye = jnp.where(ri == ci, 1.0, 0.0).astype(F32)
    pw = -lmat
    inv = eye + pw
    for _ in range(5):
        pw = _mm3(pw, pw)
        inv = inv + _mm3(inv, pw)
    egc = jnp.exp(gc)
    u = _mm3(inv, v * bb)
    w = _mm3(inv, kb * egc)
    aqk = _dot_nt(_bf(q), _bf(k)) * decay
    last = lax.broadcasted_iota(jnp.int32, (c, LANES), 0) == c - 1
    tot = _colsum(jnp.where(last, gc, 0.0))
    k_tail = k * jnp.exp(tot - gc)
    tail = jnp.broadcast_to(jnp.exp(tot), (SUBLANES, LANES))
    return u, w, aqk, q * egc, k_tail, tail


def _gdn_intra(qn, kn, vv, g_b, beta_b):
    t_rows = qn.shape[0]
    nc = t_rows // CHUNK
    cb = min(GDN_CB, nc)
    rows = cb * CHUNK
    col = pl.BlockSpec((rows, HEAD_DIM), lambda h, b: (b, h))

    def body(q_ref, k_ref, v_ref, g_ref, b_ref, u_ref, w_ref, a_ref, qd_ref, kt_ref, tl_ref):
        def group(gi, carry):
            cis = [gi * grp + j for j in range(grp)]
            rs = [pl.ds(pl.multiple_of(ci * CHUNK, CHUNK), CHUNK) for ci in cis]
            ins = [(q_ref[r, :], k_ref[r, :], v_ref[r, :], g_ref[r, :], b_ref[r, :]) for r in rs]
            outs = [_gdn_chunk(*x) for x in ins]
            for ci, r, (u, w, aqk, qd, kt, tl) in zip(cis, rs, outs):
                u_ref[r, :] = u
                w_ref[r, :] = w
                a_ref[0, r, :] = aqk
                qd_ref[r, :] = qd
                kt_ref[r, :] = kt
                tl_ref[0, ci] = tl
            return carry

        grp = min(GDN_GROUP, cb)
        lax.fori_loop(0, cb // grp, group, 0)

    full = jax.ShapeDtypeStruct((t_rows, HW), F32)
    return pl.pallas_call(
        body, name="gdn_intra_fwd",
        out_shape=[full, full, jax.ShapeDtypeStruct((HEADS, t_rows, CHUNK), F32), full, full,
                   jax.ShapeDtypeStruct((HEADS, nc, SUBLANES, LANES), F32)],
        grid=(HEADS, nc // cb),
        in_specs=[col] * 5,
        out_specs=[col, col, pl.BlockSpec((1, rows, CHUNK), lambda h, b: (h, b, 0)), col, col,
                   pl.BlockSpec((1, cb, SUBLANES, LANES), lambda h, b: (h, b, 0, 0))],
        compiler_params=_params(("parallel", "parallel")),
    )(qn, kn, vv, g_b, beta_b)


def _gdn_intra_bwd(qn, kn, vv, g_b, beta_b, du, dw, da, dqd, dkt, dtl):
    t_rows = qn.shape[0]
    nc = t_rows // CHUNK
    cb = min(GDN_CB, nc)
    rows = cb * CHUNK
    col = pl.BlockSpec((rows, HEAD_DIM), lambda h, b: (b, h))
    a_spec = pl.BlockSpec((1, rows, CHUNK), lambda h, b: (h, b, 0))
    tl_spec = pl.BlockSpec((1, cb, SUBLANES, LANES), lambda h, b: (h, b, 0, 0))

    def body(q_ref, k_ref, v_ref, g_ref, b_ref, du_ref, dw_ref, da_ref, dqd_ref, dkt_ref, dtl_ref,
             dq_ref, dk_ref, dv_ref, dg_ref, db_ref):
        def group(gi, carry):
            cis = [gi * grp + j for j in range(grp)]
            rs = [pl.ds(pl.multiple_of(ci * CHUNK, CHUNK), CHUNK) for ci in cis]
            ins = [(q_ref[r, :], k_ref[r, :], v_ref[r, :], g_ref[r, :], b_ref[r, :]) for r in rs]
            cts = [(du_ref[r, :], dw_ref[r, :], da_ref[0, r, :], dqd_ref[r, :], dkt_ref[r, :], dtl_ref[0, ci])
                   for ci, r in zip(cis, rs)]
            grads = [jax.vjp(_gdn_chunk, *x)[1](ct) for x, ct in zip(ins, cts)]
            for r, (dq, dk, dv, dg, db) in zip(rs, grads):
                dq_ref[r, :] = dq
                dk_ref[r, :] = dk
                dv_ref[r, :] = dv
                dg_ref[r, :] = dg
                db_ref[r, :] = db
            return carry

        grp = min(GDN_GROUP, cb)
        lax.fori_loop(0, cb // grp, group, 0)

    full = jax.ShapeDtypeStruct((t_rows, HW), F32)
    return pl.pallas_call(
        body, name="gdn_intra_bwd",
        out_shape=[full] * 5,
        grid=(HEADS, nc // cb),
        in_specs=[col] * 7 + [a_spec, col, col, tl_spec],
        out_specs=[col] * 5,
        compiler_params=_params(("parallel", "parallel")),
    )(qn, kn, vv, g_b, beta_b, du, dw, da, dqd, dkt, dtl)


def _head_cols(h):
    return slice(h * HEAD_DIM, (h + 1) * HEAD_DIM)


def _gdn_scan(u, w, aqk, qd, kt, tl):
    t_rows = u.shape[0]
    nc = t_rows // CHUNK
    cb = min(GDN_CB, nc)
    rows = cb * CHUNK
    wide = pl.BlockSpec((rows, HW), lambda b: (b, 0))

    def body(u_ref, w_ref, a_ref, qd_ref, kt_ref, tl_ref, o_ref, s_out_ref, s_ref):
        @pl.when(pl.program_id(0) == 0)
        def _():
            s_ref[...] = jnp.zeros_like(s_ref)

        def chunk(ci, carry):
            r = pl.ds(pl.multiple_of(ci * CHUNK, CHUNK), CHUNK)
            for h in range(HEADS):
                hc = _head_cols(h)
                s = s_ref[h]
                s_out_ref[ci, h] = s
                sb = _bf(s)
                vn = u_ref[r, hc] - _dot(_bf(w_ref[r, hc]), sb)
                vnb = _bf(vn)
                o_ref[r, hc] = _dot(_bf(qd_ref[r, hc]), sb) + _dot(_bf(a_ref[h, r, :]), vnb)
                s_ref[h] = s * tl_ref[h, ci, 0:1, :] + _dot_tn(_bf(kt_ref[r, hc]), vnb)
            return carry

        lax.fori_loop(0, cb, chunk, 0)

    return pl.pallas_call(
        body, name="gdn_scan_fwd",
        out_shape=[jax.ShapeDtypeStruct((t_rows, HW), F32),
                   jax.ShapeDtypeStruct((nc, HEADS, HEAD_DIM, HEAD_DIM), F32)],
        grid=(nc // cb,),
        in_specs=[wide, wide, pl.BlockSpec((HEADS, rows, CHUNK), lambda b: (0, b, 0)), wide, wide,
                  pl.BlockSpec((HEADS, cb, SUBLANES, LANES), lambda b: (0, b, 0, 0))],
        out_specs=[wide, pl.BlockSpec((cb, HEADS, HEAD_DIM, HEAD_DIM), lambda b: (b, 0, 0, 0))],
        scratch_shapes=[pltpu.VMEM((HEADS, HEAD_DIM, HEAD_DIM), F32)],
        compiler_params=_params(("arbitrary",)),
    )(u, w, aqk, qd, kt, tl)


def _gdn_scan_bwd(do, u, w, aqk, qd, kt, tl, states):
    t_rows = u.shape[0]
    nc = t_rows // CHUNK
    cb = min(GDN_CB, nc)
    rows = cb * CHUNK
    nb = nc // cb
    wide = pl.BlockSpec((rows, HW), lambda b: (nb - 1 - b, 0))
    a_spec = pl.BlockSpec((HEADS, rows, CHUNK), lambda b: (0, nb - 1 - b, 0))
    tl_spec = pl.BlockSpec((HEADS, cb, SUBLANES, LANES), lambda b: (0, nb - 1 - b, 0, 0))

    def body(do_ref, u_ref, w_ref, a_ref, qd_ref, kt_ref, tl_ref, s_in_ref,
             du_ref, dw_ref, da_ref, dqd_ref, dkt_ref, dtl_ref, ds_ref):
        @pl.when(pl.program_id(0) == 0)
        def _():
            ds_ref[...] = jnp.zeros_like(ds_ref)

        row0 = lax.broadcasted_iota(jnp.int32, (SUBLANES, LANES), 0) == 0

        def chunk(step, carry):
            ci = cb - 1 - step
            r = pl.ds(pl.multiple_of(ci * CHUNK, CHUNK), CHUNK)
            for h in range(HEADS):
                hc = _head_cols(h)
                s = s_in_ref[ci, h]
                ds_next = ds_ref[h]
                sb, dsb = _bf(s), _bf(ds_next)
                wb, ab, ktb, qdb = _bf(w_ref[r, hc]), _bf(a_ref[h, r, :]), _bf(kt_ref[r, hc]), _bf(qd_ref[r, hc])
                dob = _bf(do_ref[r, hc])
                vn = u_ref[r, hc] - _dot(wb, sb)
                vnb = _bf(vn)
                dvn = _dot_tn(ab, dob) + _dot(ktb, dsb)
                dvnb = _bf(dvn)
                du_ref[r, hc] = dvn
                dw_ref[r, hc] = -_dot_nt(dvnb, sb)
                da_ref[h, r, :] = _dot_nt(dob, vnb)
                dqd_ref[r, hc] = _dot_nt(dob, sb)
                dkt_ref[r, hc] = _dot_nt(vnb, dsb)
                dtl_ref[h, ci] = jnp.where(row0, _colsum(s * ds_next), 0.0)
                ds_ref[h] = _dot_tn(qdb, dob) + ds_next * tl_ref[h, ci, 0:1, :] - _dot_tn(wb, dvnb)
            return carry

        lax.fori_loop(0, cb, chunk, 0)

    full = jax.ShapeDtypeStruct((t_rows, HW), F32)
    return pl.pallas_call(
        body, name="gdn_scan_bwd",
        out_shape=[full, full, jax.ShapeDtypeStruct((HEADS, t_rows, CHUNK), F32), full, full,
                   jax.ShapeDtypeStruct((HEADS, nc, SUBLANES, LANES), F32)],
        grid=(nb,),
        in_specs=[wide, wide, wide, a_spec, wide, wide, tl_spec,
                  pl.BlockSpec((cb, HEADS, HEAD_DIM, HEAD_DIM), lambda b: (nb - 1 - b, 0, 0, 0))],
        out_specs=[wide, wide, a_spec, wide, wide, tl_spec],
        scratch_shapes=[pltpu.VMEM((HEADS, HEAD_DIM, HEAD_DIM), F32)],
        compiler_params=_params(("arbitrary",)),
    )(do, u, w, aqk, qd, kt, tl, states)


def _att_rel_index():
    qi = lax.broadcasted_iota(jnp.int32, (ATT_QB, ATT_KW), 0)
    kj = lax.broadcasted_iota(jnp.int32, (ATT_QB, ATT_KW), 1)
    return jnp.clip(qi - kj + ATT_PAD, -(CHUNK - 1), MAX_REL) + (CHUNK - 1)


def _att_valid(b):
    qi = lax.broadcasted_iota(jnp.int32, (ATT_QB, ATT_KW), 0)
    kj = lax.broadcasted_iota(jnp.int32, (ATT_QB, ATT_KW), 1)
    shift = CHUNK.bit_length() - 1
    qc = jnp.right_shift(qi, shift)
    kc = jnp.right_shift(kj, shift) - LEFT_CHUNKS
    return (kc <= qc) & (kc >= qc - LEFT_CHUNKS) & (kj + b * ATT_QB >= ATT_PAD)


def _att_block(q_raw, k_raw, v, qw, kw, bias, valid):
    q = _rms(q_raw, qw)
    k = _rms(k_raw, kw)
    s = _dot_nt(_bf(q), _bf(k)) * (HEAD_DIM ** -0.5) + bias
    s = jnp.where(valid, s, NEG_INF)
    p = jnp.exp(s - jnp.max(s, axis=-1, keepdims=True))
    p = p / jnp.sum(p, axis=-1, keepdims=True)
    return _dot(_bf(p), _bf(v))


def _att_specs():
    q_spec = pl.BlockSpec((ATT_QB, HEAD_DIM), lambda h, b: (b, h))
    k_specs = [pl.BlockSpec((ATT_QB, HEAD_DIM), lambda h, b, j=j: (b + j, HEADS + h)) for j in range(3)]
    v_specs = [pl.BlockSpec((ATT_QB, HEAD_DIM), lambda h, b, j=j: (b + j, 2 * HEADS + h)) for j in range(3)]
    w_spec = pl.BlockSpec((1, HEAD_DIM), lambda h, b: (0, 0))
    smem = pl.BlockSpec(memory_space=pltpu.SMEM)
    return q_spec, k_specs, v_specs, w_spec, smem


def _att_fill_bias(bias_ref, rel_ref, h):
    idx = _att_rel_index()

    def fill(r, acc):
        return jnp.where(idx == r, rel_ref[h, r], acc)

    bias_ref[...] = lax.fori_loop(0, N_REL, fill, jnp.zeros((ATT_QB, ATT_KW), F32))


def _attention(pb, pbp, qw, kw, rel):
    t_rows = pb.shape[0]
    q_spec, k_specs, v_specs, w_spec, smem = _att_specs()

    def body(q_ref, k0, k1, k2, v0, v1, v2, qw_ref, kw_ref, rel_ref, o_ref, bias_ref):
        h, b = pl.program_id(0), pl.program_id(1)

        @pl.when(b == 0)
        def _():
            _att_fill_bias(bias_ref, rel_ref, h)

        kwin = jnp.concatenate([k0[...], k1[...], k2[...]], axis=0)
        vwin = jnp.concatenate([v0[...], v1[...], v2[...]], axis=0)
        o = _att_block(q_ref[...], kwin, vwin, qw_ref[...], kw_ref[...], bias_ref[...], _att_valid(b))
        o_ref[...] = o.astype(o_ref.dtype)

    return pl.pallas_call(
        body, name="band_attention_fwd",
        out_shape=jax.ShapeDtypeStruct((t_rows, HW), BF16),
        grid=(HEADS, t_rows // ATT_QB),
        in_specs=[q_spec] + k_specs + v_specs + [w_spec, w_spec, smem],
        out_specs=pl.BlockSpec((ATT_QB, HEAD_DIM), lambda h, b: (b, h)),
        scratch_shapes=[pltpu.VMEM((ATT_QB, ATT_KW), F32)],
        compiler_params=_params(("arbitrary", "arbitrary")),
    )(pb, pbp, pbp, pbp, pbp, pbp, pbp, qw, kw, rel)


def _attention_bwd(pb, pbp, qw, kw, rel, dyb):
    t_rows = pb.shape[0]
    nb = t_rows // ATT_QB
    q_spec, k_specs, v_specs, w_spec, smem = _att_specs()
    pad_rows = t_rows + ATT_PAD
    acc_spec = pl.BlockSpec((pad_rows, HEAD_DIM), lambda h, b: (0, h))

    def body(q_ref, k0, k1, k2, v0, v1, v2, qw_ref, kw_ref, rel_ref, do_ref,
             dq_ref, dk_ref, dv_ref, dqw_ref, dkw_ref, drel_ref, bias_ref, dbias_ref):
        h, b = pl.program_id(0), pl.program_id(1)

        @pl.when(b == 0)
        def _():
            _att_fill_bias(bias_ref, rel_ref, h)
            dbias_ref[...] = jnp.zeros_like(dbias_ref)
            dk_ref[...] = jnp.zeros_like(dk_ref)
            dv_ref[...] = jnp.zeros_like(dv_ref)

        @pl.when((b == 0) & (h == 0))
        def _():
            dqw_ref[...] = jnp.zeros_like(dqw_ref)
            dkw_ref[...] = jnp.zeros_like(dkw_ref)

        kwin = jnp.concatenate([k0[...], k1[...], k2[...]], axis=0)
        vwin = jnp.concatenate([v0[...], v1[...], v2[...]], axis=0)
        valid = _att_valid(b)
        _, vjp = jax.vjp(lambda q, k, v, a, c, bias: _att_block(q, k, v, a, c, bias, valid),
                         q_ref[...], kwin, vwin, qw_ref[...], kw_ref[...], bias_ref[...])
        dq, dk, dv, dqw, dkw, dbias = vjp(do_ref[...])
        dq_ref[...] = dq.astype(dq_ref.dtype)
        win = pl.ds(pl.multiple_of(b * ATT_QB, ATT_QB), ATT_KW)
        dk_ref[win, :] += dk
        dv_ref[win, :] += dv
        dqw_ref[...] += dqw
        dkw_ref[...] += dkw
        dbias_ref[...] += dbias

        @pl.when(b == nb - 1)
        def _():
            idx = _att_rel_index()
            tot = dbias_ref[...]

            def reduce(r, carry):
                drel_ref[h, r] = jnp.sum(jnp.where(idx == r, tot, 0.0))
                return carry

            lax.fori_loop(0, N_REL, reduce, 0)

    return pl.pallas_call(
        body, name="band_attention_bwd",
        out_shape=[jax.ShapeDtypeStruct((t_rows, HW), BF16),
                   jax.ShapeDtypeStruct((pad_rows, HW), F32), jax.ShapeDtypeStruct((pad_rows, HW), F32),
                   jax.ShapeDtypeStruct((1, HEAD_DIM), F32), jax.ShapeDtypeStruct((1, HEAD_DIM), F32),
                   jax.ShapeDtypeStruct((HEADS, N_REL), F32)],
        grid=(HEADS, nb),
        in_specs=[q_spec] + k_specs + v_specs + [w_spec, w_spec, smem, q_spec],
        out_specs=[q_spec, acc_spec, acc_spec, w_spec, w_spec, smem],
        scratch_shapes=[pltpu.VMEM((ATT_QB, ATT_KW), F32), pltpu.VMEM((ATT_QB, ATT_KW), F32)],
        compiler_params=_params(("arbitrary", "arbitrary")),
    )(pb, pbp, pbp, pbp, pbp, pbp, pbp, qw, kw, rel, dyb)


def _me():
    return lax.axis_index("x"), lax.axis_index("y"), lax.axis_index("c")


def _index(x, y, c):
    return 4 * x + 2 * y + c


HBM_SPEC = pl.BlockSpec(memory_space=pl.ANY)


def _block(ref, kind, d, r, c):
    if kind == "rows":
        return ref.at[pl.ds(d * r, r), :]
    if kind == "win":
        return ref.at[:, pl.ds(d * WIN_STEP, c)]
    return ref.at[:, pl.ds(d * c, c)]


def _all_gather(shards, kinds):
    n = len(shards)

    def body(*refs):
        x_refs, out_refs = refs[:n], refs[n:2 * n]
        send_sems, recv_sems, local_sems = refs[2 * n:]
        x, y, c = _me()
        me, sibling = (x, y, c), (x, y, 1 - c)
        chips = [(1 - x, y), (x, 1 - y), (1 - x, 1 - y)]

        def copy(i, k, blk, to, src=None):
            r_, c_ = shards[i].shape
            dst = _block(out_refs[i], kinds[i], _index(*blk), r_, c_)
            return pltpu.make_async_remote_copy(
                src_ref=dst if src is None else src, dst_ref=dst,
                send_sem=send_sems.at[i, k], recv_sem=recv_sems.at[i, k], device_id=to, device_id_type=MESH)

        sends, local = [], []
        for i in range(n):
            r_, c_ = shards[i].shape
            mine = pltpu.make_async_copy(x_refs[i], _block(out_refs[i], kinds[i], _index(*me), r_, c_),
                                         local_sems.at[i])
            mine.start()
            local.append(mine)
            first = [copy(i, 0, me, sibling, src=x_refs[i])]
            first += [copy(i, 1 + j, me, (*chip, c), src=x_refs[i]) for j, chip in enumerate(chips)]
            for cp in first:
                cp.start()
            sends += first
        for i in range(n):
            for j, chip in enumerate(chips):
                copy(i, 1 + j, (*chip, c), me).wait_recv()
                passed = copy(i, 4 + j, (*chip, c), sibling)
                passed.start()
                sends.append(passed)
        for i in range(n):
            copy(i, 0, sibling, me).wait_recv()
            for j, chip in enumerate(chips):
                copy(i, 4 + j, (*chip, 1 - c), me).wait_recv()
        for cp in sends:
            cp.wait_send()
        for cp in local:
            cp.wait()

    def full_shape(s, kind):
        r_, c_ = s.shape
        return (N_DEV * r_, c_) if kind == "rows" else (r_, N_DEV * c_)

    return pl.pallas_call(
        body, name="weights_all_gather",
        out_shape=[jax.ShapeDtypeStruct(full_shape(s, k), s.dtype) for s, k in zip(shards, kinds)],
        in_specs=[HBM_SPEC] * n, out_specs=[HBM_SPEC] * n,
        scratch_shapes=[pltpu.SemaphoreType.DMA((n, 7)), pltpu.SemaphoreType.DMA((n, 7)),
                        pltpu.SemaphoreType.DMA((n,))],
        compiler_params=pltpu.CompilerParams(has_side_effects=True),
    )(*shards)


def _exchange(grads, kinds, block_shapes):
    n = len(grads)

    def body(*refs):
        g_refs, out_refs = refs[:n], refs[n:2 * n]
        send_sems, recv_sems, local_sems = refs[2 * n:]
        x, y, c = _me()
        mine = _index(x, y, c)
        copies, local = [], []
        for i in range(n):
            r_, c_ = block_shapes[i]
            cp = pltpu.make_async_copy(_block(g_refs[i], kinds[i], mine, r_, c_), out_refs[i].at[mine],
                                       local_sems.at[i])
            cp.start()
            local.append(cp)
            for k in range(1, N_DEV):
                px, py, pc = x ^ (k >> 2), y ^ ((k >> 1) & 1), c ^ (k & 1)
                copies.append(pltpu.make_async_remote_copy(
                    src_ref=_block(g_refs[i], kinds[i], _index(px, py, pc), r_, c_), dst_ref=out_refs[i].at[mine],
                    send_sem=send_sems.at[i, k - 1], recv_sem=recv_sems.at[i, k - 1],
                    device_id=(px, py, pc), device_id_type=MESH))
        for cp in copies:
            cp.start()
        for cp in copies:
            cp.wait()
        for cp in local:
            cp.wait()

    return pl.pallas_call(
        body, name="grads_exchange",
        out_shape=[jax.ShapeDtypeStruct((N_DEV,) + tuple(shp), g.dtype) for g, shp in zip(grads, block_shapes)],
        in_specs=[HBM_SPEC] * n, out_specs=[HBM_SPEC] * n,
        scratch_shapes=[pltpu.SemaphoreType.DMA((n, 7)), pltpu.SemaphoreType.DMA((n, 7)),
                        pltpu.SemaphoreType.DMA((n,))],
        compiler_params=pltpu.CompilerParams(has_side_effects=True),
    )(*grads)


SEM_SPEC = pl.BlockSpec(memory_space=pltpu.SEMAPHORE)
HBM_ONLY = pl.BlockSpec(memory_space=pltpu.HBM)
DATAFLOW = pltpu.SideEffectType.DATAFLOW_SIDE_EFFECTING


def _peers():
    x, y, c = _me()
    return [(x ^ (k >> 2), y ^ ((k >> 1) & 1), c ^ (k & 1)) for k in range(1, N_DEV)]


def _gather_copies(shapes, kinds):
    def make(src_refs, land_refs, send_sems, recv_sems):
        mine = _index(*_me())
        return [pltpu.make_async_remote_copy(
            src_ref=src_refs[i], dst_ref=_block(land_refs[i], kind, mine, r, c),
            send_sem=send_sems.at[7 * i + k], recv_sem=recv_sems.at[7 * i + k], device_id=peer, device_id_type=MESH)
            for i, ((r, c), kind) in enumerate(zip(shapes, kinds)) for k, peer in enumerate(_peers())]

    return make


def _exchange_copies(shapes, kinds):
    def make(src_refs, land_refs, send_sems, recv_sems):
        mine = _index(*_me())
        return [pltpu.make_async_remote_copy(
            src_ref=_block(src_refs[i], kind, _index(*peer), r, c), dst_ref=land_refs[i].at[mine],
            send_sem=send_sems.at[7 * i + k], recv_sem=recv_sems.at[7 * i + k], device_id=peer, device_id_type=MESH)
            for i, ((r, c), kind) in enumerate(zip(shapes, kinds)) for k, peer in enumerate(_peers())]

    return make


def _split_start(srcs, lands, make, name):
    n = len(srcs)

    def body(*refs):
        send_sems, recv_sems = refs[2 * n], refs[2 * n + 1]
        for cp in make(refs[:n], refs[n:2 * n], send_sems, recv_sems):
            cp.start()
        refs[-1][...] = jnp.zeros_like(refs[-1])

    arrays = list(srcs) + list(lands)
    out = pl.pallas_call(
        body, name=name,
        out_shape=(pltpu.SemaphoreType.DMA((7 * n,)), pltpu.SemaphoreType.DMA((7 * n,)),
                   *[pltpu.HBM(a.shape, a.dtype) for a in arrays], jax.ShapeDtypeStruct((SUBLANES, LANES), F32)),
        in_specs=[HBM_ONLY] * (2 * n),
        out_specs=(SEM_SPEC, SEM_SPEC, *[HBM_ONLY] * (2 * n), pl.BlockSpec(memory_space=pltpu.VMEM)),
        input_output_aliases={i: 2 + i for i in range(2 * n)},
        compiler_params=pltpu.CompilerParams(has_side_effects=DATAFLOW),
    )(*[pltpu.with_memory_space_constraint(a, pltpu.HBM) for a in arrays])
    return out[0], out[1], list(out[2:2 + n]), list(out[2 + n:2 + 2 * n]), out[-1]


def _split_wait(send_sems, recv_sems, srcs, lands, after, make, name):
    n = len(srcs)

    def body(*refs):
        for cp in make(refs[:n], refs[n:2 * n], refs[2 * n], refs[2 * n + 1]):
            cp.wait_send()
            cp.wait_recv()

    arrays = list(srcs) + list(lands)
    out = pl.pallas_call(
        body, name=name,
        out_shape=tuple(pltpu.HBM(a.shape, a.dtype) for a in arrays),
        in_specs=[HBM_ONLY] * (2 * n) + [SEM_SPEC, SEM_SPEC, pl.BlockSpec(memory_space=pl.ANY)],
        out_specs=tuple([HBM_ONLY] * (2 * n)),
        input_output_aliases={i: i for i in range(2 * n)},
        compiler_params=pltpu.CompilerParams(has_side_effects=DATAFLOW),
    )(*arrays, send_sems, recv_sems, after)
    return list(out[n:])


def _all_reduce_small(vals, name):
    rows, width = vals.shape

    def body(x_ref, out_ref, buf_ref, send_sems, recv_sems):
        x, y, c = _me()
        mine = _index(x, y, c)
        buf_ref[mine] = x_ref[...]
        copies = []
        for k in range(1, N_DEV):
            px, py, pc = x ^ (k >> 2), y ^ ((k >> 1) & 1), c ^ (k & 1)
            copies.append(pltpu.make_async_remote_copy(
                src_ref=x_ref, dst_ref=buf_ref.at[mine],
                send_sem=send_sems.at[k - 1], recv_sem=recv_sems.at[k - 1],
                device_id=(px, py, pc), device_id_type=MESH))
        for cp in copies:
            cp.start()
        for cp in copies:
            cp.wait()
        acc = buf_ref[0]
        for j in range(1, N_DEV):
            acc = acc + buf_ref[j]
        out_ref[...] = acc

    vmem = pl.BlockSpec(memory_space=pltpu.VMEM)
    return pl.pallas_call(
        body, name=name,
        out_shape=jax.ShapeDtypeStruct(vals.shape, F32),
        in_specs=[vmem], out_specs=vmem,
        scratch_shapes=[pltpu.VMEM((N_DEV, rows, width), F32),
                        pltpu.SemaphoreType.DMA((7,)), pltpu.SemaphoreType.DMA((7,))],
        compiler_params=pltpu.CompilerParams(has_side_effects=True),
    )(vals)


def _adamw_math(w, g, m, v):
    m = ADAM_B1 * m + (1.0 - ADAM_B1) * g
    v = ADAM_B2 * v + (1.0 - ADAM_B2) * (g * g)
    m_hat = m / (1.0 - ADAM_B1 ** ADAM_STEP)
    v_hat = v / (1.0 - ADAM_B2 ** ADAM_STEP)
    delta = -ADAM_LR * (m_hat / (jnp.sqrt(v_hat) + ADAM_EPS) + ADAM_WD * w)
    return delta, m, v


ROW_TILE_ELEMS = 384 * 1024


def _row_tile(rows, width):
    best = SUBLANES
    for t in range(SUBLANES, rows + 1, SUBLANES):
        if rows % t == 0 and t * width <= ROW_TILE_ELEMS:
            best = t
    return best


def _sum_received(r_ref):
    g = r_ref[0].astype(F32)
    for j in range(1, N_DEV):
        g = g + r_ref[j].astype(F32)
    return g


def _adamw_recv(recv, w, m, v, name):
    _, rows, width = recv.shape
    tile = _row_tile(rows, width)

    def body(r_ref, w_ref, m_ref, v_ref, g_out, d_out, m_out, v_out):
        g = _sum_received(r_ref)
        d, mn, vn = _adamw_math(w_ref[...], g, m_ref[...], v_ref[...])
        g_out[...] = g
        d_out[...] = d
        m_out[...] = mn
        v_out[...] = vn

    spec = pl.BlockSpec((tile, width), lambda i: (i, 0))
    shape = jax.ShapeDtypeStruct((rows, width), F32)
    return pl.pallas_call(
        body, name=name,
        out_shape=[shape] * 4, grid=(rows // tile,),
        in_specs=[pl.BlockSpec((N_DEV, tile, width), lambda i: (0, i, 0)), spec, spec, spec],
        out_specs=[spec] * 4,
        compiler_params=_params(("parallel",)),
    )(recv, w, m, v)


WIN_STEP = 1408
WIN_W = 1536
IN_SHARD = IN_COLS // N_DEV
IN_PADDED = WIN_STEP * (N_DEV - 1) + WIN_W


def _roll_w_in(shard_padded):
    rows = shard_padded.shape[0]
    tile = _row_tile(rows, WIN_W)

    def body(x_ref, main_ref, edge_ref):
        win = pltpu.roll(x_ref[...], 2 * _index(*_me()), 1).astype(BF16)
        main_ref[...] = win[:, :WIN_STEP]
        edge_ref[...] = win[:, WIN_STEP:]

    return pl.pallas_call(
        body, name="w_in_window",
        out_shape=[jax.ShapeDtypeStruct((rows, WIN_STEP), BF16), jax.ShapeDtypeStruct((rows, WIN_W - WIN_STEP), BF16)],
        grid=(rows // tile,),
        in_specs=[pl.BlockSpec((tile, WIN_W), lambda i: (i, 0))],
        out_specs=[pl.BlockSpec((tile, WIN_STEP), lambda i: (i, 0)),
                   pl.BlockSpec((tile, WIN_W - WIN_STEP), lambda i: (i, 0))],
        compiler_params=_params(("parallel",)),
    )(shard_padded)


def _sum_w_in_windows(recv):
    _, rows, width = recv.shape
    tile = _row_tile(rows, width)

    def body(r_ref, g_out):
        g_out[...] = pltpu.roll(_sum_received(r_ref), width - 2 * _index(*_me()), 1)

    return pl.pallas_call(
        body, name="w_in_grad_sum",
        out_shape=jax.ShapeDtypeStruct((rows, width), F32), grid=(rows // tile,),
        in_specs=[pl.BlockSpec((N_DEV, tile, width), lambda i: (0, i, 0))],
        out_specs=pl.BlockSpec((tile, width), lambda i: (i, 0)),
        compiler_params=_params(("parallel",)),
    )(recv)


def _adamw_small(w, g, m, v, name):
    def fn(i, n, w_, g_, m_, v_):
        return _adamw_math(w_, g_, m_, v_)

    r, c = w.shape
    return _rows(fn, [(w, "t"), (g, "t"), (m, "t"), (v, "t")], [], [(c, F32)] * 3, [], _row_tile(r, c), name)


def _norm_fwd(x, w, name):
    return _rows(lambda i, n, x_, w_: (_rms(x_, w_[...]),), [(x, "t")], [w], [(D_MODEL, BF16)], [], 512, name)[0]


def _residual_norm_fwd(x, y, scale, w, name):
    def fn(i, n, x_, y_, w_):
        xn = x_ + scale * y_
        return xn, _rms(xn, w_[...])

    return _rows(fn, [(x, "t"), (y, "t")], [w], [(D_MODEL, F32), (D_MODEL, BF16)], [], 512, name)


def _residual_norm_bwd(x, w, dhs, dres, scale, name):
    nh = len(dhs)

    def fn(i, n, x_, dres_, *rest):
        dh = rest[0]
        for extra in rest[1:nh]:
            dh = dh + extra
        _, vjp = jax.vjp(_rms, x_, rest[nh][...])
        dx, dw = vjp(dh)
        dx = dx + dres_
        return dx, scale * dx, dw

    return _rows(fn, [(x, "t"), (dres, "t")] + [(d, "t") for d in dhs], [w],
                 [(D_MODEL, F32), (D_MODEL, BF16)], [(1, D_MODEL)], 256, name)


def _ffn_fwd(h, w_gu, w_down, tag):
    gu = _matmul(h, w_gu, "nn", F32, tag + "_gu")
    act = _rows(lambda i, n, gu_: (_swiglu(gu_),), [(gu, "t")], [], [(D_FF, BF16)], [], 128, tag + "_swiglu")[0]
    y = _matmul(act, w_down, "nn", F32, tag + "_down")
    return gu, act, y


def _ffn_bwd(h, gu, act, dy, w_gu, w_down, tag):
    d_w_down = _matmul(act, dy, "tn", BF16, tag + "_d_w_down")
    dact = _matmul(dy, w_down, "nt", F32, tag + "_dact")

    def fn(i, n, gu_, dact_):
        _, vjp = jax.vjp(_swiglu, gu_)
        return vjp(dact_)

    dgu = _rows(fn, [(gu, "t"), (dact, "t")], [], [(2 * D_FF, BF16)], [], 128, tag + "_swiglu_bwd")[0]
    d_w_gu = _matmul(h, dgu, "tn", BF16, tag + "_d_w_gu")
    dh = _matmul(dgu, w_gu, "nt", F32, tag + "_dh")
    return dh, d_w_gu, d_w_down


def _expanders():
    e_g = np.zeros((LANES, HW), np.float32)
    e_b = np.zeros((LANES, HW), np.float32)
    for h in range(HEADS):
        e_g[h, h * HEAD_DIM:(h + 1) * HEAD_DIM] = 1.0
        e_b[HEADS + h, h * HEAD_DIM:(h + 1) * HEAD_DIM] = 1.0
    return jnp.asarray(e_g), jnp.asarray(e_b)


def _pad_lanes(v):
    return jnp.pad(v, ((0, 0), (0, LANES - v.shape[1])))


class _LocalWeights:
    def __init__(self, big):
        self.big, self.sent = big, {}

    def arrive(self, group, after):
        return self.big

    def send(self, group, grads):
        self.sent.update(grads)
        return jnp.zeros((), F32)


def _local_step(x, p, tgt, small, comm):
    big = comm.arrive("ffn1", None)
    e_g, e_b = _expanders()
    alog, dtb = _pad_lanes(small["a_log"]), _pad_lanes(small["dt_bias"])
    conv_w = jnp.pad(small["conv_w"], ((0, SUBLANES - CONV_K), (0, 0)))
    rel = small["rel_bias"]

    h1 = _norm_fwd(x, small["ffn1_norm"], "ffn1_norm")
    gu1, act1, y1 = _ffn_fwd(h1, big["ffn1_w_gu"], big["ffn1_w_down"], "ffn1")
    x1, h2 = _residual_norm_fwd(x, y1, 0.5, small["mix_norm"], "mix_norm")

    big = {**big, **comm.arrive("mixer", h2)}
    w_in = big["w_in"]
    w_qz = w_in[:, :IN_QZ]
    w_ab = jnp.pad(w_in[:, IN_AB0:IN_QKVB0], ((0, 0), (0, LANES - 2 * HEADS)))
    w_qkvb = w_in[:, IN_QKVB0:IN_GG0]
    w_gg = w_in[:, IN_GG0:IN_COLS]
    qz = _matmul(h2, w_qz, "nn", F32, "in_qz")
    ab = _matmul(h2, w_ab, "nn", F32, "in_ab")
    pb = _matmul(h2, w_qkvb, "nn", F32, "in_qkvb")
    gg = _matmul(h2, w_gg, "nn", F32, "in_gates")
    pa, z = qz[:, :3 * HW], qz[:, 3 * HW:]

    def prep(i, n, pa_, prev_, ab_, cw_, alog_, dtb_, eg_, eb_):
        q, k, v = _gdn_post(_conv(pa_, prev_, cw_, i))
        g_b, beta_b = _gdn_gates(ab_, alog_[...], dtb_[...], eg_[...], eb_[...])
        return q, k, v, g_b, beta_b

    qn, kn, vv, g_b, beta_b = _rows(prep, [(pa, "t"), (pa, "p"), (ab, "t")], [conv_w, alog, dtb, e_g, e_b],
                                    [(HW, F32)] * 5, [], 256, "gdn_prep")
    u, w, aqk, qd, kt, tl = _gdn_intra(qn, kn, vv, g_b, beta_b)
    o, states = _gdn_scan(u, w, aqk, qd, kt, tl)
    ya = _rows(lambda i, n, o_, z_, w_: (_gated_norm(o_, z_, w_[...]),), [(o, "t"), (z, "t")], [small["gdn_norm"]],
               [(HW, BF16)], [], 512, "gdn_gated_norm")[0]

    pbp = jnp.pad(pb, ((ATT_PAD, 0), (0, 0)))
    yb = _attention(pb, pbp, small["q_norm"], small["k_norm"], rel)

    ta = _matmul(ya, big["w_branch_a"], "nn", F32, "branch_a")
    tb = _matmul(yb, big["w_branch_b"], "nn", F32, "branch_b")
    mixed = _rows(lambda i, n, gg_, ta_, tb_: (_mix(gg_, ta_, tb_),), [(gg, "t"), (ta, "t"), (tb, "t")], [],
                  [(D_MODEL, BF16)], [], 256, "mix")[0]
    m_out = _matmul(mixed, big["w_out"], "nn", F32, "w_out")
    x2, h3 = _residual_norm_fwd(x1, m_out, 1.0, small["ffn2_norm"], "ffn2_norm")
    big = {**big, **comm.arrive("tail", h3)}
    gu2, act2, y2 = _ffn_fwd(h3, big["ffn2_w_gu"], big["ffn2_w_down"], "ffn2")
    x3, h4 = _residual_norm_fwd(x2, y2, 0.5, small["ple_norm"], "ple_norm")
    gp = _matmul(h4, big["ple_gate"], "nn", F32, "ple_gate")
    pp = _matmul(p, big["ple_proj"], "nn", F32, "ple_proj")

    def head(i, n, x3_, gp_, pp_, tgt_):
        sg = _sigmoid(gp_)
        err = x3_ + sg * pp_ - tgt_
        dx4 = err * (1.0 / D_MODEL)
        sq = _colsum(err * err)
        part = sq[:, :LANES]
        for j in range(1, D_MODEL // LANES):
            part = part + sq[:, j * LANES:(j + 1) * LANES]
        return dx4, dx4 * pp_ * sg * (1.0 - sg), dx4 * sg, (0.5 / D_MODEL) * part

    dx4, dgp, dpp, loss_lanes = _rows(head, [(x3, "t"), (gp, "t"), (pp, "t"), (tgt, "t")], [],
                                      [(D_MODEL, F32), (D_MODEL, BF16), (D_MODEL, BF16)], [(1, LANES)], 256,
                                      "ple_loss_head")
    loss = jnp.sum(loss_lanes)

    gbig, gsmall = {}, {}
    gbig["ple_proj"] = _matmul(p, dpp, "tn", BF16, "d_ple_proj")
    gbig["ple_gate"] = _matmul(h4, dgp, "tn", BF16, "d_ple_gate")
    dh4 = _matmul(dgp, big["ple_gate"], "nt", F32, "ple_gate_dh")
    dx3, dy2, gsmall["ple_norm"] = _residual_norm_bwd(x3, small["ple_norm"], [dh4], dx4, 0.5, "ple_norm_bwd")

    dh3, gbig["ffn2_w_gu"], gbig["ffn2_w_down"] = _ffn_bwd(h3, gu2, act2, dy2, big["ffn2_w_gu"],
                                                           big["ffn2_w_down"], "ffn2")
    sent = comm.send("tail", {n: gbig[n] for n in ("ple_proj", "ple_gate", "ffn2_w_down", "ffn2_w_gu")})
    dx2, dx2b, gsmall["ffn2_norm"] = _residual_norm_bwd(x2, small["ffn2_norm"] + sent, [dh3], dx3, 1.0,
                                                        "ffn2_norm_bwd")

    gbig["w_out"] = _matmul(mixed, dx2b, "tn", BF16, "d_w_out")
    dmixed = _matmul(dx2b, big["w_out"], "nt", F32, "w_out_dx")

    def mix_bwd(i, n, gg_, ta_, tb_, dm_):
        _, vjp = jax.vjp(_mix, gg_, ta_, tb_)
        return vjp(dm_)

    dgg, dta, dtb_ = _rows(mix_bwd, [(gg, "t"), (ta, "t"), (tb, "t"), (dmixed, "t")], [],
                           [(2 * D_MODEL, BF16), (D_MODEL, BF16), (D_MODEL, BF16)], [], 256, "mix_bwd")
    gbig["w_branch_a"] = _matmul(ya, dta, "tn", BF16, "d_branch_a")
    gbig["w_branch_b"] = _matmul(yb, dtb_, "tn", BF16, "d_branch_b")
    dya = _matmul(dta, big["w_branch_a"], "nt", F32, "branch_a_dx")
    dyb = _matmul(dtb_, big["w_branch_b"], "nt", F32, "branch_b_dx")

    dq_b, dk_b, dv_b, gsmall["q_norm"], gsmall["k_norm"], gsmall["rel_bias"] = _attention_bwd(
        pb, pbp, small["q_norm"], small["k_norm"], rel, dyb)
    dpb = jnp.concatenate([dq_b, dk_b[ATT_PAD:].astype(BF16), dv_b[ATT_PAD:].astype(BF16)], axis=1)

    def gated_bwd(i, n, o_, z_, dya_, w_):
        _, vjp = jax.vjp(_gated_norm, o_, z_, w_[...])
        return vjp(dya_)

    do, dz, gsmall["gdn_norm"] = _rows(gated_bwd, [(o, "t"), (z, "t"), (dya, "t")], [small["gdn_norm"]],
                                       [(HW, F32), (HW, BF16)], [(1, HEAD_DIM)], 256, "gdn_gated_norm_bwd")
    du, dw, da, dqd, dkt, dtl = _gdn_scan_bwd(do, u, w, aqk, qd, kt, tl, states)
    dqn, dkn, dvv, dg_b, dbeta_b = _gdn_intra_bwd(qn, kn, vv, g_b, beta_b, du, dw, da, dqd, dkt, dtl)

    def prep_bwd(i, n, pa_, prev_, ab_, dq_, dk_, dv_, dg_, db_, cw_, alog_, dtb_, eg_, eb_):
        _, vjp = jax.vjp(_gdn_post, _conv(pa_, prev_, cw_, i))
        (dy,) = vjp((dq_, dk_, dv_))
        e_g_, e_b_ = eg_[...], eb_[...]
        _, vjp_g = jax.vjp(lambda a, b, c: _gdn_gates(a, b, c, e_g_, e_b_), ab_, alog_[...], dtb_[...])
        dab, dalog, ddtb = vjp_g((dg_, db_))
        return dy, dab, dalog, ddtb

    dy_conv, dab, dalog, ddtb = _rows(
        prep_bwd, [(pa, "t"), (pa, "p"), (ab, "t"), (dqn, "t"), (dkn, "t"), (dvv, "t"), (dg_b, "t"), (dbeta_b, "t")],
        [conv_w, alog, dtb, e_g, e_b], [(3 * HW, F32), (LANES, BF16)], [(1, LANES), (1, LANES)], 256,
        "gdn_prep_bwd")
    gsmall["a_log"] = dalog[:, :HEADS]
    gsmall["dt_bias"] = ddtb[:, :HEADS]

    def conv_bwd(i, n, dy_, nxt_, pa_, prev_, cw_):
        dpa = dy_ * cw_[CONV_K - 1:CONV_K, :]
        row = lax.broadcasted_iota(jnp.int32, (SUBLANES, dy_.shape[1]), 0)
        dcw = jnp.where(row == CONV_K - 1, _colsum(dy_ * pa_), 0.0)
        for j in range(CONV_K - 1):
            s = CONV_K - 1 - j
            dpa = dpa + _shift_up(dy_, nxt_, s, i, n) * cw_[j:j + 1, :]
            dcw = dcw + jnp.where(row == j, _colsum(dy_ * _shift_down(pa_, prev_, s, i)), 0.0)
        return dpa, dcw

    dpa, dcw = _rows(conv_bwd, [(dy_conv, "t"), (dy_conv, "n"), (pa, "t"), (pa, "p")], [conv_w],
                     [(3 * HW, BF16)], [(SUBLANES, 3 * HW)], 256, "gdn_conv_bwd")
    gsmall["conv_w"] = dcw[:CONV_K]

    dqz = jnp.concatenate([dpa, dz], axis=1)
    d_w_qz = _matmul(h2, dqz, "tn", BF16, "d_in_qz")
    d_w_ab = _matmul(h2, dab, "tn", BF16, "d_in_ab")
    d_w_qkvb = _matmul(h2, dpb, "tn", BF16, "d_in_qkvb")
    d_w_gg = _matmul(h2, dgg, "tn", BF16, "d_in_gates")
    gbig["w_in"] = jnp.concatenate([d_w_qz, d_w_ab[:, :2 * HEADS], d_w_qkvb, d_w_gg,
                                    jnp.zeros((D_MODEL, IN_PADDED - IN_COLS), BF16)], axis=1)
    dh2 = [_matmul(dqz, w_qz, "nt", F32, "in_qz_dh"), _matmul(dab, w_ab, "nt", F32, "in_ab_dh"),
           _matmul(dpb, w_qkvb, "nt", F32, "in_qkvb_dh"), _matmul(dgg, w_gg, "nt", F32, "in_gates_dh")]
    sent = comm.send("mixer", {n: gbig[n] for n in ("w_out", "w_branch_b", "w_branch_a", "w_in")})
    dx1, dy1, gsmall["mix_norm"] = _residual_norm_bwd(x1, small["mix_norm"] + sent, dh2, dx2, 0.5, "mix_norm_bwd")

    dh1, gbig["ffn1_w_gu"], gbig["ffn1_w_down"] = _ffn_bwd(h1, gu1, act1, dy1, big["ffn1_w_gu"],
                                                           big["ffn1_w_down"], "ffn1")
    grad_x, _, gsmall["ffn1_norm"] = _residual_norm_bwd(x, small["ffn1_norm"], [dh1], dx1, 1.0, "ffn1_norm_bwd")
    comm.send("ffn1", {n: gbig[n] for n in ("ffn1_w_down", "ffn1_w_gu")})
    return loss, grad_x, gsmall


GATHER_GROUPS = {"ffn1": ("ffn1_w_gu", "ffn1_w_down"),
                 "mixer": ("w_in_main", "w_in_edge", "w_branch_a", "w_branch_b", "w_out"),
                 "tail": ("ffn2_w_gu", "ffn2_w_down", "ple_gate", "ple_proj")}


def _kind(name):
    return "cols" if name in COL_SHARDED or name.startswith("w_in_") else "rows"


def _own_offset(kind, shape, index):
    r, c = shape
    if kind == "rows":
        return (index * r, 0)
    return (0, index * (WIN_STEP if kind == "win" else c))


def _merge_w_in(main, edges):
    edge_w = WIN_W - WIN_STEP
    w_in = jnp.pad(main, ((0, 0), (0, edge_w)))
    for d in range(N_DEV):
        at = WIN_STEP * (d + 1)
        w_in = w_in + jnp.pad(edges[:, d * edge_w:(d + 1) * edge_w], ((0, 0), (at, IN_PADDED - at - edge_w)))
    return w_in


class _Fsdp:
    def __init__(self, wts):
        self.index = _index(*_me())
        main, edge = _roll_w_in(jnp.pad(wts["w_in"], ((0, 0), (0, WIN_W - IN_SHARD))))
        self.shards = {n: wts[n].astype(BF16) for n in BIG if n != "w_in"}
        self.shards.update(w_in_main=main, w_in_edge=edge)
        names = GATHER_GROUPS["ffn1"]
        self.first = dict(zip(names, _all_gather([self.shards[n] for n in names], [_kind(n) for n in names])))
        self.flight, token = {}, self.first["ffn1_w_down"][0, 0].astype(F32) * 0.0
        for group in ("mixer", "tail"):
            names = GATHER_GROUPS[group]
            srcs = [self.shards[n] for n in names]
            kinds = [_kind(n) for n in names]
            lands = [lax.empty((s.shape[0] * (N_DEV if k == "rows" else 1), s.shape[1] * (N_DEV if k == "cols" else 1)),
                               BF16) for s, k in zip(srcs, kinds)]
            make = _gather_copies([s.shape for s in srcs], kinds)
            srcs[0] = srcs[0] + token.astype(BF16)
            send_sems, recv_sems, srcs, lands, tok = _split_start(srcs, lands, make, "gather_start_" + group)
            token = token + tok[0, 0]
            self.flight[group] = (send_sems, recv_sems, srcs, lands, make, kinds)
        self.token = token
        self.sent, self.recv = {}, {}

    def arrive(self, group, after):
        if group == "ffn1":
            return self.first
        names = GATHER_GROUPS[group]
        send_sems, recv_sems, srcs, lands, make, kinds = self.flight[group]
        lands = _split_wait(send_sems, recv_sems, srcs, lands, after, make, "gather_wait_" + group)
        full = {}
        for n, land, kind in zip(names, lands, kinds):
            shard = self.shards[n]
            full[n] = lax.dynamic_update_slice(land, shard, _own_offset(kind, shard.shape, self.index))
        if group == "mixer":
            full["w_in"] = _merge_w_in(full.pop("w_in_main"), full.pop("w_in_edge"))
        return full

    def send(self, group, grads):
        names = list(grads)
        kinds = ["win" if n == "w_in" else _kind(n) for n in names]
        shapes = [(D_MODEL, WIN_W) if n == "w_in" else self.shards[n].shape for n in names]
        srcs = [grads[n] for n in names]
        if group == "ffn1":
            self.recv.update(zip(names, _exchange(srcs, kinds, shapes)))
            return None
        own = [lax.dynamic_slice(g, _own_offset(k, s, self.index), s) for g, k, s in zip(srcs, kinds, shapes)]
        lands = [lax.empty((N_DEV,) + tuple(s), BF16) for s in shapes]
        make = _exchange_copies(shapes, kinds)
        send_sems, recv_sems, srcs, lands, tok = _split_start(srcs, lands, make, "grads_start_" + group)
        self.sent[group] = (names, own, send_sems, recv_sems, srcs, lands, make)
        return tok[0, 0]

    def received(self):
        after = self.recv["ffn1_w_gu"]
        for group, (names, own, send_sems, recv_sems, srcs, lands, make) in self.sent.items():
            lands = _split_wait(send_sems, recv_sems, srcs, lands, after, make, "grads_wait_" + group)
            for n, mine, land in zip(names, own, lands):
                self.recv[n] = lax.dynamic_update_slice(land, mine[None], (self.index, 0, 0))
        return self.recv


SMALL_ROWS = ("ffn1_norm", "mix_norm", "ffn2_norm", "ple_norm", "gdn_norm", "q_norm", "k_norm", "a_log", "dt_bias",
              "rel_bias", "conv_w")


def _pack_small(vals):
    rows = []
    for n in SMALL_ROWS:
        v = vals[n]
        if n == "rel_bias":
            v = jnp.pad(v, ((0, 0), (0, 2 * LANES - N_REL)))
        elif n in ("a_log", "dt_bias"):
            v = _pad_lanes(v)
        rows.append(v.reshape(-1, LANES))
    packed = jnp.concatenate(rows, axis=0)
    return jnp.pad(packed, ((0, -packed.shape[0] % SUBLANES), (0, 0)))


def _unpack_small(packed, shapes):
    out, off = {}, 0
    for n in SMALL_ROWS:
        shp = shapes[n]
        if n == "rel_bias":
            out[n] = packed[off:off + 2 * HEADS].reshape(HEADS, 2 * LANES)[:, :N_REL]
            off += 2 * HEADS
        elif n in ("a_log", "dt_bias"):
            out[n] = packed[off:off + 1, :HEADS]
            off += 1
        else:
            r = int(np.prod(shp)) // LANES
            out[n] = packed[off:off + r].reshape(shp)
            off += r
    return out


WEIGHTS = ("ffn1_norm", "ffn1_w_gu", "ffn1_w_down", "mix_norm", "w_in", "conv_w", "a_log", "dt_bias", "gdn_norm",
           "q_norm", "k_norm", "rel_bias", "w_branch_a", "w_branch_b", "w_out", "ffn2_norm", "ffn2_w_gu",
           "ffn2_w_down", "ple_norm", "ple_gate", "ple_proj")


def kernel(x, p, ffn1_norm, ffn1_w_gu, ffn1_w_down, mix_norm, w_in, conv_w, a_log, dt_bias, gdn_norm, q_norm, k_norm, rel_bias, w_branch_a, w_branch_b, w_out, ffn2_norm, ffn2_w_gu, ffn2_w_down, ple_norm, ple_gate, ple_proj, loss_target, m_ffn1_norm, m_ffn1_w_gu, m_ffn1_w_down, m_mix_norm, m_w_in, m_conv_w, m_a_log, m_dt_bias, m_gdn_norm, m_q_norm, m_k_norm, m_rel_bias, m_w_branch_a, m_w_branch_b, m_w_out, m_ffn2_norm, m_ffn2_w_gu, m_ffn2_w_down, m_ple_norm, m_ple_gate, m_ple_proj, v_ffn1_norm, v_ffn1_w_gu, v_ffn1_w_down, v_mix_norm, v_w_in, v_conv_w, v_a_log, v_dt_bias, v_gdn_norm, v_q_norm, v_k_norm, v_rel_bias, v_w_branch_a, v_w_branch_b, v_w_out, v_ffn2_norm, v_ffn2_w_gu, v_ffn2_w_down, v_ple_norm, v_ple_gate, v_ple_proj):
    args = dict(locals())
    def layer0(v):
        return v[0] if v.ndim == 3 else v

    wts = {n: layer0(args[n]) for n in WEIGHTS}
    mom = {n: layer0(args["m_" + n]) for n in WEIGHTS}
    var = {n: layer0(args["v_" + n]) for n in WEIGHTS}
    x2d, p2d, tgt = x[0], p[0, 0], loss_target[0]
    my_index = _index(*_me())

    fsdp = _Fsdp(wts)

    small = {n: wts[n] for n in SMALL_ROWS if n != "conv_w"}
    small["ffn1_norm"] = small["ffn1_norm"] + fsdp.token
    conv_shard = wts["conv_w"]
    conv_cols = conv_shard.shape[1]
    conv_packed = jnp.zeros((SUBLANES, N_DEV * conv_cols), F32)
    conv_packed = lax.dynamic_update_slice(conv_packed, jnp.pad(conv_shard, ((0, SUBLANES - CONV_K), (0, 0))),
                                           (0, my_index * conv_cols))
    small["conv_w"] = _all_reduce_small(conv_packed.reshape(-1, LANES), "conv_w_gather").reshape(SUBLANES, -1)[:CONV_K]

    loss, grad_x, gsmall = _local_step(x2d, p2d, tgt, small, fsdp)
    loss = lax.psum(loss, ("x", "y", "c"))

    recv = fsdp.received()
    outs_big = {}
    for n in BIG:
        if n == "w_in":
            g_in = _sum_w_in_windows(recv[n])[:, :IN_SHARD]
            outs_big[n] = [g_in] + list(_adamw_small(wts[n], g_in, mom[n], var[n], "adamw_w_in"))
        else:
            outs_big[n] = _adamw_recv(recv[n], wts[n], mom[n], var[n], "adamw_" + n)

    small_shapes = {n: (small[n].shape if n != "conv_w" else (CONV_K, N_DEV * conv_cols)) for n in SMALL_ROWS}
    gsum = _unpack_small(_all_reduce_small(_pack_small(gsmall), "small_grads_all_reduce"), small_shapes)
    gsum["conv_w"] = lax.dynamic_slice(gsum["conv_w"], (0, my_index * conv_cols), (CONV_K, conv_cols))
    rep = [n for n in SMALL_ROWS if n != "conv_w"]
    rep_shapes = {n: small_shapes[n] for n in rep}

    def pack_rep(vals):
        return _pack_small({**{n: vals[n] for n in rep}, "conv_w": jnp.zeros((CONV_K, LANES), F32)})

    def unpack_rep(packed):
        return _unpack_small(packed, {**rep_shapes, "conv_w": (CONV_K, LANES)})

    outs_small = [unpack_rep(o) for o in _adamw_small(pack_rep(wts), pack_rep(gsum), pack_rep(mom), pack_rep(var),
                                                      "adamw_replicated")]
    pad8 = functools.partial(jnp.pad, pad_width=((0, SUBLANES - CONV_K), (0, 0)))
    outs_conv = [o[:CONV_K] for o in _adamw_small(pad8(conv_shard), pad8(gsum["conv_w"]), pad8(mom["conv_w"]),
                                                   pad8(var["conv_w"]), "adamw_conv")]

    def leaf(kind, n):
        if n in BIG:
            return outs_big[n][kind][None]
        if n == "conv_w":
            return (gsum["conv_w"] if kind == 0 else outs_conv[kind - 1])[None]
        return (gsum[n] if kind == 0 else outs_small[kind - 1][n]).reshape(args[n].shape)

    result = [loss, grad_x[None]]
    for kind in range(4):
        result += [leaf(kind, n) for n in WEIGHTS]
    return tuple(result)
```

```python
import functools

import numpy as np
import jax
import jax.numpy as jnp
from jax import lax
from jax.experimental import pallas as pl
from jax.experimental.pallas import tpu as pltpu

F32 = jnp.float32
BF16 = jnp.bfloat16
HIGHEST = lax.Precision.HIGHEST
MESH = pl.DeviceIdType.MESH

D_MODEL = 2048
D_FF = 5632
HEADS = 8
HEAD_DIM = 128
HW = HEADS * HEAD_DIM
CHUNK = 64
LEFT_CHUNKS = 8
MAX_REL = 128
N_REL = (CHUNK - 1) + MAX_REL + 1
CONV_K = 4
EPS = 1e-6
NEG_INF = -1e30
N_DEV = 8
LANES = 128
SUBLANES = 8
VMEM_LIMIT = 56 * 1024 * 1024

MATMUL_WHOLE_K = 2048

ATT_QB = 256
ATT_KW = ATT_QB + LEFT_CHUNKS * CHUNK
ATT_PAD = LEFT_CHUNKS * CHUNK
GDN_CB = 8
GDN_GROUP = 8

ADAM_LR = 0.001
ADAM_B1 = 0.9
ADAM_B2 = 0.999
ADAM_EPS = 1e-08
ADAM_WD = 0.01
ADAM_STEP = 10

IN_QZ = 3 * HW + HW
IN_AB0 = IN_QZ
IN_QKVB0 = IN_AB0 + 2 * HEADS
IN_GG0 = IN_QKVB0 + 3 * HW
IN_COLS = IN_GG0 + 2 * D_MODEL

BIG = ("ffn1_w_gu", "ffn1_w_down", "w_in", "w_branch_a", "w_branch_b", "w_out",
       "ffn2_w_gu", "ffn2_w_down", "ple_gate", "ple_proj")
COL_SHARDED = ("ffn1_w_gu", "w_in", "w_branch_a", "w_branch_b", "ffn2_w_gu", "ple_proj")


def _params(semantics=None, **kw):
    return pltpu.CompilerParams(dimension_semantics=semantics, vmem_limit_bytes=VMEM_LIMIT, **kw)


def _pick(n, cands):
    for c in cands:
        if n % c == 0:
            return c
    return n


def _matmul(a, b, mode, out_dtype, name):
    if mode == "nn":
        (m, k), (k2, n) = a.shape, b.shape
    elif mode == "nt":
        (m, k), (n, k2) = a.shape, b.shape
    else:
        (k, m), (k2, n) = a.shape, b.shape
    assert k == k2, (a.shape, b.shape, mode)
    tm = _pick(m, (1024, 512, 256, 128))
    tn = _pick(n, (1024, 512, 256, 128))
    tk = k if k <= MATMUL_WHOLE_K else _pick(k, (1024, 512, 256, 128))
    nk = k // tk
    if mode == "nn":
        a_spec = pl.BlockSpec((tm, tk), lambda i, j, kk: (i, kk))
        b_spec = pl.BlockSpec((tk, tn), lambda i, j, kk: (kk, j))
        dims = (((1,), (0,)), ((), ()))
    elif mode == "nt":
        a_spec = pl.BlockSpec((tm, tk), lambda i, j, kk: (i, kk))
        b_spec = pl.BlockSpec((tn, tk), lambda i, j, kk: (j, kk))
        dims = (((1,), (1,)), ((), ()))
    else:
        a_spec = pl.BlockSpec((tk, tm), lambda i, j, kk: (kk, i))
        b_spec = pl.BlockSpec((tk, tn), lambda i, j, kk: (kk, j))
        dims = (((0,), (0,)), ((), ()))

    def body(a_ref, b_ref, o_ref, *acc):
        prod = lax.dot_general(a_ref[...].astype(BF16), b_ref[...].astype(BF16), dims, preferred_element_type=F32)
        if nk == 1:
            o_ref[...] = prod.astype(o_ref.dtype)
            return
        acc_ref, kk = acc[0], pl.program_id(2)

        @pl.when(kk == 0)
        def _():
            acc_ref[...] = prod

        @pl.when((kk > 0) & (kk < nk - 1))
        def _():
            acc_ref[...] += prod

        @pl.when(kk == nk - 1)
        def _():
            o_ref[...] = (acc_ref[...] + prod).astype(o_ref.dtype)

    return pl.pallas_call(
        body, name=name,
        out_shape=jax.ShapeDtypeStruct((m, n), out_dtype),
        grid=(m // tm, n // tn, nk),
        in_specs=[a_spec, b_spec],
        out_specs=pl.BlockSpec((tm, tn), lambda i, j, kk: (i, j)),
        scratch_shapes=[pltpu.VMEM((tm, tn), F32)] if nk > 1 else [],
        compiler_params=_params(("parallel", "parallel", "arbitrary")),
    )(a, b)


def _rows(fn, row_ins, consts, row_outs, acc_outs, tile, name):
    t_rows = row_ins[0][0].shape[0]
    tile = min(tile, t_rows)
    assert t_rows % tile == 0 and tile % SUBLANES == 0
    n = t_rows // tile
    per = tile // SUBLANES
    last8 = t_rows // SUBLANES - 1
    in_specs = []
    for arr, kind in row_ins:
        c = arr.shape[1]
        if kind == "t":
            in_specs.append(pl.BlockSpec((tile, c), lambda i: (i, 0)))
        elif kind == "p":
            in_specs.append(pl.BlockSpec((SUBLANES, c), lambda i: (jnp.maximum(i * per - 1, 0), 0)))
        else:
            in_specs.append(pl.BlockSpec((SUBLANES, c), lambda i: (jnp.minimum((i + 1) * per, last8), 0)))
    for arr in consts:
        in_specs.append(pl.BlockSpec(arr.shape, lambda i, nd=arr.ndim: (0,) * nd))
    out_shape = [jax.ShapeDtypeStruct((t_rows, c), dt) for c, dt in row_outs]
    out_specs = [pl.BlockSpec((tile, c), lambda i: (i, 0)) for c, _ in row_outs]
    for shp in acc_outs:
        out_shape.append(jax.ShapeDtypeStruct(shp, F32))
        out_specs.append(pl.BlockSpec(shp, lambda i, nd=len(shp): (0,) * nd))
    n_in = len(row_ins) + len(consts)
    n_row_out = len(row_outs)

    def body(*refs):
        i = pl.program_id(0)
        vals = [r[...] for r in refs[:len(row_ins)]]
        res = fn(i, n, *vals, *refs[len(row_ins):n_in])
        outs = refs[n_in:]
        for r, v in zip(outs[:n_row_out], res[:n_row_out]):
            r[...] = v.astype(r.dtype)
        if acc_outs:
            @pl.when(i == 0)
            def _():
                for r in outs[n_row_out:]:
                    r[...] = jnp.zeros_like(r)

            for r, v in zip(outs[n_row_out:], res[n_row_out:]):
                r[...] += v

    res = pl.pallas_call(
        body, name=name, out_shape=out_shape, grid=(n,), in_specs=in_specs, out_specs=out_specs,
        compiler_params=_params(("arbitrary",) if acc_outs else ("parallel",)),
    )(*[a for a, _ in row_ins], *consts)
    return res


def _rms(x, w):
    return x * lax.rsqrt(jnp.mean(x * x, axis=-1, keepdims=True) + EPS) * w


def _l2n(x):
    return x * lax.rsqrt(jnp.sum(x * x, axis=-1, keepdims=True) + EPS)


def _sigmoid(x):
    return 1.0 / (1.0 + jnp.exp(-x))


def _silu(x):
    return x * _sigmoid(x)


def _softplus(x):
    return jnp.maximum(x, 0.0) + jnp.log(1.0 + jnp.exp(-jnp.abs(x)))


def _heads(fn, *xs):
    nh = xs[0].shape[1] // HEAD_DIM
    return jnp.concatenate(
        [fn(*[x[:, h * HEAD_DIM:(h + 1) * HEAD_DIM] for x in xs]) for h in range(nh)], axis=1)


def _colsum(x):
    return jnp.sum(x, axis=0, keepdims=True)


def _swiglu(gu):
    return _silu(gu[:, :D_FF]) * gu[:, D_FF:]


def _gated_norm(o, z, w):
    return _heads(lambda oh, zh: _rms(oh, w) * _silu(zh), o, z)


def _mix(gg, ta, tb):
    return _sigmoid(gg[:, :D_MODEL]) * ta + _sigmoid(gg[:, D_MODEL:]) * tb


def _gdn_post(y):
    a = _silu(y)
    q = _heads(lambda v: _l2n(v) * (HEAD_DIM ** -0.5), a[:, :HW])
    k = _heads(_l2n, a[:, HW:2 * HW])
    return q, k, a[:, 2 * HW:]


NN = (((1,), (0,)), ((), ()))
NT = (((1,), (1,)), ((), ()))
TN = (((0,), (0,)), ((), ()))


def _dg(a, b, dims):
    return lax.dot_general(a, b, dims, preferred_element_type=F32)


def _split2(x):
    hi = x.astype(BF16)
    return hi, (x - hi.astype(F32)).astype(BF16)


def _split3(x):
    hi = x.astype(BF16)
    r = x - hi.astype(F32)
    mid = r.astype(BF16)
    return hi, mid, (r - mid.astype(F32)).astype(BF16)


def _dg3(a, b, dims):
    ah, al = _split2(a)
    bh, bl = _split2(b)
    return _dg(ah, bh, dims) + (_dg(ah, bl, dims) + _dg(al, bh, dims))


BNN = (((2,), (1,)), ((0,), (0,)))
BNT = (((2,), (2,)), ((0,), (0,)))
BTN = (((1,), (1,)), ((0,), (0,)))


@jax.custom_vjp
def _mm3(a, b):
    return _dg3(a, b, BNN)


_mm3.defvjp(lambda a, b: (_dg3(a, b, BNN), (a, b)),
            lambda res, g: (_dg3(g, res[1], BNT), _dg3(res[0], g, BTN)))


def _xm(x, m, dims):
    mb = m.astype(BF16)
    parts = _split3(x)
    return _dg(parts[0], mb, dims) + (_dg(parts[1], mb, dims) + _dg(parts[2], mb, dims))


def _mx(m, x, dims):
    mb = m.astype(BF16)
    parts = _split3(x)
    return _dg(mb, parts[0], dims) + (_dg(mb, parts[1], dims) + _dg(mb, parts[2], dims))


@jax.custom_vjp
def _times_const(x, m):
    return _xm(x, m, NN)


_times_const.defvjp(lambda x, m: (_xm(x, m, NN), m),
                    lambda m, g: (_xm(g, m, NT), jnp.zeros_like(m)))


@jax.custom_vjp
def _const_times(m, x):
    return _mx(m, x, NN)


_const_times.defvjp(lambda m, x: (_mx(m, x, NN), m),
                    lambda m, g: (jnp.zeros_like(m), _mx(m, g, TN)))


@jax.custom_vjp
def _lane_mean_cols(x, avg):
    return _mx(avg, x, BNT)


_lane_mean_cols.defvjp(lambda x, avg: (_mx(avg, x, BNT), avg),
                       lambda avg, g: (_xm(g, avg, BTN), jnp.zeros_like(avg)))


def _gdn_gates(ab, alog, dtb, e_g, e_b):
    t = ab.shape[0]
    g = -jnp.exp(alog) * _softplus(ab + dtb)
    beta = _sigmoid(ab)
    ri = lax.broadcasted_iota(jnp.int32, (t, t), 0)
    ci = lax.broadcasted_iota(jnp.int32, (t, t), 1)
    shift = CHUNK.bit_length() - 1
    same = jnp.right_shift(ri, shift) == jnp.right_shift(ci, shift)
    tril = jnp.where(same & (ri >= ci), 1.0, 0.0).astype(F32)
    gc = _const_times(tril, g)
    return _times_const(gc, e_g), _times_const(beta, e_b)


def _shift_down(x, halo, s, i):
    if s == 0:
        return x
    halo = jnp.where(i == 0, 0.0, halo)
    xr = pltpu.roll(x, s, 0)
    hr = pltpu.roll(halo, s, 0)
    row = lax.broadcasted_iota(jnp.int32, (SUBLANES, x.shape[1]), 0)
    top = jnp.where(row < s, hr, xr[:SUBLANES])
    return jnp.concatenate([top, xr[SUBLANES:]], axis=0)


def _shift_up(x, halo, s, i, n):
    if s == 0:
        return x
    t = x.shape[0]
    halo = jnp.where(i == n - 1, 0.0, halo)
    xr = pltpu.roll(x, t - s, 0)
    hr = pltpu.roll(halo, SUBLANES - s, 0)
    row = lax.broadcasted_iota(jnp.int32, (SUBLANES, x.shape[1]), 0)
    bot = jnp.where(row >= SUBLANES - s, hr, xr[t - SUBLANES:])
    return jnp.concatenate([xr[:t - SUBLANES], bot], axis=0)


def _conv(pa, prev, cw_ref, i):
    y = pa * cw_ref[CONV_K - 1:CONV_K, :]
    for j in range(CONV_K - 1):
        y = y + _shift_down(pa, prev, CONV_K - 1 - j, i) * cw_ref[j:j + 1, :]
    return y


def _dot_nt(a, b, precision=None):
    return lax.dot_general(a, b, (((1,), (1,)), ((), ())), precision=precision, preferred_element_type=F32)


def _dot_tn(a, b, precision=None):
    return lax.dot_general(a, b, (((0,), (0,)), ((), ())), precision=precision, preferred_element_type=F32)


def _dot(a, b, precision=None):
    return jnp.dot(a, b, precision=precision, preferred_element_type=F32)


def _bf(x):
    return x.astype(BF16)


def _gdn_chunk(q, k, v, gc, bb):
    nb, c, _ = q.shape
    ri = lax.broadcasted_iota(jnp.int32, (nb, c, c), 1)
    ci = lax.broadcasted_iota(jnp.int32, (nb, c, c), 2)
    incl = ri >= ci
    strict = ri > ci
    g_row = gc[:, :, :c]
    g_col = _lane_mean_cols(gc, jnp.full((nb, c, LANES), 1.0 / LANES, F32))
    decay = jnp.where(incl, jnp.exp(jnp.where(incl, g_row - g_col, 0.0)), 0.0)
    kb = k * bb
    lmat = jnp.where(strict, _dg(_bf(kb), _bf(k), BNT) * decay, 0.0)
    eye = jnp.where(ri == ci, 1.0, 0.0).astype(F32)
    pw = -lmat
    inv = eye + pw
    for _ in range(5):
        pw = _mm3(pw, pw)
        inv = inv + _mm3(inv, pw)
    egc = jnp.exp(gc)
    u = _mm3(inv, v * bb)
    w = _mm3(inv, kb * egc)
    aqk = _dg(_bf(q), _bf(k), BNT) * decay
    last = lax.broadcasted_iota(jnp.int32, (nb, c, LANES), 1) == c - 1
    tot = jnp.sum(jnp.where(last, gc, 0.0), axis=1, keepdims=True)
    k_tail = k * jnp.exp(tot - gc)
    tail = jnp.broadcast_to(jnp.exp(tot), (nb, SUBLANES, LANES))
    return u, w, aqk, q * egc, k_tail, tail


def _gdn_intra(qn, kn, vv, g_b, beta_b):
    t_rows = qn.shape[0]
    nc = t_rows // CHUNK
    cb = min(GDN_CB, nc)
    rows = cb * CHUNK
    col = pl.BlockSpec((rows, HEAD_DIM), lambda h, b: (b, h))

    def body(q_ref, k_ref, v_ref, g_ref, b_ref, u_ref, w_ref, a_ref, qd_ref, kt_ref, tl_ref):
        def group(gi, carry):
            r = pl.ds(pl.multiple_of(gi * (grp * CHUNK), grp * CHUNK), grp * CHUNK)
            ins = [ref[r, :].reshape(grp, CHUNK, HEAD_DIM) for ref in (q_ref, k_ref, v_ref, g_ref, b_ref)]
            u, w, aqk, qd, kt, tl = _gdn_chunk(*ins)
            for ref, val in ((u_ref, u), (w_ref, w), (qd_ref, qd), (kt_ref, kt)):
                ref[r, :] = val.reshape(grp * CHUNK, HEAD_DIM)
            a_ref[0, r, :] = aqk.reshape(grp * CHUNK, CHUNK)
            tl_ref[0, pl.ds(gi * grp, grp)] = tl
            return carry

        grp = min(GDN_GROUP, cb)
        lax.fori_loop(0, cb // grp, group, 0)

    full = jax.ShapeDtypeStruct((t_rows, HW), F32)
    return pl.pallas_call(
        body, name="gdn_intra_fwd",
        out_shape=[full, full, jax.ShapeDtypeStruct((HEADS, t_rows, CHUNK), F32), full, full,
                   jax.ShapeDtypeStruct((HEADS, nc, SUBLANES, LANES), F32)],
        grid=(HEADS, nc // cb),
        in_specs=[col] * 5,
        out_specs=[col, col, pl.BlockSpec((1, rows, CHUNK), lambda h, b: (h, b, 0)), col, col,
                   pl.BlockSpec((1, cb, SUBLANES, LANES), lambda h, b: (h, b, 0, 0))],
        compiler_params=_params(("parallel", "parallel")),
    )(qn, kn, vv, g_b, beta_b)


def _gdn_intra_bwd(qn, kn, vv, g_b, beta_b, du, dw, da, dqd, dkt, dtl):
    t_rows = qn.shape[0]
    nc = t_rows // CHUNK
    cb = min(GDN_CB, nc)
    rows = cb * CHUNK
    col = pl.BlockSpec((rows, HEAD_DIM), lambda h, b: (b, h))
    a_spec = pl.BlockSpec((1, rows, CHUNK), lambda h, b: (h, b, 0))
    tl_spec = pl.BlockSpec((1, cb, SUBLANES, LANES), lambda h, b: (h, b, 0, 0))

    def body(q_ref, k_ref, v_ref, g_ref, b_ref, du_ref, dw_ref, da_ref, dqd_ref, dkt_ref, dtl_ref,
             dq_ref, dk_ref, dv_ref, dg_ref, db_ref):
        def group(gi, carry):
            r = pl.ds(pl.multiple_of(gi * (grp * CHUNK), grp * CHUNK), grp * CHUNK)
            wide = (grp, CHUNK, HEAD_DIM)
            ins = [ref[r, :].reshape(wide) for ref in (q_ref, k_ref, v_ref, g_ref, b_ref)]
            cts = (du_ref[r, :].reshape(wide), dw_ref[r, :].reshape(wide),
                   da_ref[0, r, :].reshape(grp, CHUNK, CHUNK), dqd_ref[r, :].reshape(wide),
                   dkt_ref[r, :].reshape(wide), dtl_ref[0, pl.ds(gi * grp, grp)])
            grads = jax.vjp(_gdn_chunk, *ins)[1](cts)
            for ref, val in zip((dq_ref, dk_ref, dv_ref, dg_ref, db_ref), grads):
                ref[r, :] = val.reshape(grp * CHUNK, HEAD_DIM)
            return carry

        grp = min(GDN_GROUP, cb)
        lax.fori_loop(0, cb // grp, group, 0)

    full = jax.ShapeDtypeStruct((t_rows, HW), F32)
    return pl.pallas_call(
        body, name="gdn_intra_bwd",
        out_shape=[full] * 5,
        grid=(HEADS, nc // cb),
        in_specs=[col] * 7 + [a_spec, col, col, tl_spec],
        out_specs=[col] * 5,
        compiler_params=_params(("parallel", "parallel")),
    )(qn, kn, vv, g_b, beta_b, du, dw, da, dqd, dkt, dtl)


def _head_cols(h):
    return slice(h * HEAD_DIM, (h + 1) * HEAD_DIM)


def _gdn_scan(u, w, aqk, qd, kt, tl):
    t_rows = u.shape[0]
    nc = t_rows // CHUNK
    cb = min(GDN_CB, nc)
    rows = cb * CHUNK
    wide = pl.BlockSpec((rows, HW), lambda b: (b, 0))

    def body(u_ref, w_ref, a_ref, qd_ref, kt_ref, tl_ref, o_ref, s_out_ref, s_ref):
        @pl.when(pl.program_id(0) == 0)
        def _():
            s_ref[...] = jnp.zeros_like(s_ref)

        def chunk(ci, carry):
            r = pl.ds(pl.multiple_of(ci * CHUNK, CHUNK), CHUNK)
            for h in range(HEADS):
                hc = _head_cols(h)
                s = s_ref[h]
                s_out_ref[ci, h] = s
                sb = _bf(s)
                vn = u_ref[r, hc] - _dot(_bf(w_ref[r, hc]), sb)
                vnb = _bf(vn)
                o_ref[r, hc] = _dot(_bf(qd_ref[r, hc]), sb) + _dot(_bf(a_ref[h, r, :]), vnb)
                s_ref[h] = s * tl_ref[h, ci, 0:1, :] + _dot_tn(_bf(kt_ref[r, hc]), vnb)
            return carry

        lax.fori_loop(0, cb, chunk, 0)

    return pl.pallas_call(
        body, name="gdn_scan_fwd",
        out_shape=[jax.ShapeDtypeStruct((t_rows, HW), F32),
                   jax.ShapeDtypeStruct((nc, HEADS, HEAD_DIM, HEAD_DIM), F32)],
        grid=(nc // cb,),
        in_specs=[wide, wide, pl.BlockSpec((HEADS, rows, CHUNK), lambda b: (0, b, 0)), wide, wide,
                  pl.BlockSpec((HEADS, cb, SUBLANES, LANES), lambda b: (0, b, 0, 0))],
        out_specs=[wide, pl.BlockSpec((cb, HEADS, HEAD_DIM, HEAD_DIM), lambda b: (b, 0, 0, 0))],
        scratch_shapes=[pltpu.VMEM((HEADS, HEAD_DIM, HEAD_DIM), F32)],
        compiler_params=_params(("arbitrary",)),
    )(u, w, aqk, qd, kt, tl)


def _gdn_scan_bwd(do, u, w, aqk, qd, kt, tl, states):
    t_rows = u.shape[0]
    nc = t_rows // CHUNK
    cb = min(GDN_CB, nc)
    rows = cb * CHUNK
    nb = nc // cb
    wide = pl.BlockSpec((rows, HW), lambda b: (nb - 1 - b, 0))
    a_spec = pl.BlockSpec((HEADS, rows, CHUNK), lambda b: (0, nb - 1 - b, 0))
    tl_spec = pl.BlockSpec((HEADS, cb, SUBLANES, LANES), lambda b: (0, nb - 1 - b, 0, 0))

    def body(do_ref, u_ref, w_ref, a_ref, qd_ref, kt_ref, tl_ref, s_in_ref,
             du_ref, dw_ref, da_ref, dqd_ref, dkt_ref, dtl_ref, ds_ref):
        @pl.when(pl.program_id(0) == 0)
        def _():
            ds_ref[...] = jnp.zeros_like(ds_ref)

        row0 = lax.broadcasted_iota(jnp.int32, (SUBLANES, LANES), 0) == 0

        def chunk(step, carry):
            ci = cb - 1 - step
            r = pl.ds(pl.multiple_of(ci * CHUNK, CHUNK), CHUNK)
            for h in range(HEADS):
                hc = _head_cols(h)
                s = s_in_ref[ci, h]
                ds_next = ds_ref[h]
                sb, dsb = _bf(s), _bf(ds_next)
                wb, ab, ktb, qdb = _bf(w_ref[r, hc]), _bf(a_ref[h, r, :]), _bf(kt_ref[r, hc]), _bf(qd_ref[r, hc])
                dob = _bf(do_ref[r, hc])
                vn = u_ref[r, hc] - _dot(wb, sb)
                vnb = _bf(vn)
                dvn = _dot_tn(ab, dob) + _dot(ktb, dsb)
                dvnb = _bf(dvn)
                du_ref[r, hc] = dvn
                dw_ref[r, hc] = -_dot_nt(dvnb, sb)
                da_ref[h, r, :] = _dot_nt(dob, vnb)
                dqd_ref[r, hc] = _dot_nt(dob, sb)
                dkt_ref[r, hc] = _dot_nt(vnb, dsb)
                dtl_ref[h, ci] = jnp.where(row0, _colsum(s * ds_next), 0.0)
                ds_ref[h] = _dot_tn(qdb, dob) + ds_next * tl_ref[h, ci, 0:1, :] - _dot_tn(wb, dvnb)
            return carry

        lax.fori_loop(0, cb, chunk, 0)

    full = jax.ShapeDtypeStruct((t_rows, HW), F32)
    return pl.pallas_call(
        body, name="gdn_scan_bwd",
        out_shape=[full, full, jax.ShapeDtypeStruct((HEADS, t_rows, CHUNK), F32), full, full,
                   jax.ShapeDtypeStruct((HEADS, nc, SUBLANES, LANES), F32)],
        grid=(nb,),
        in_specs=[wide, wide, wide, a_spec, wide, wide, tl_spec,
                  pl.BlockSpec((cb, HEADS, HEAD_DIM, HEAD_DIM), lambda b: (nb - 1 - b, 0, 0, 0))],
        out_specs=[wide, wide, a_spec, wide, wide, tl_spec],
        scratch_shapes=[pltpu.VMEM((HEADS, HEAD_DIM, HEAD_DIM), F32)],
        compiler_params=_params(("arbitrary",)),
    )(do, u, w, aqk, qd, kt, tl, states)


def _att_rel_index():
    qi = lax.broadcasted_iota(jnp.int32, (ATT_QB, ATT_KW), 0)
    kj = lax.broadcasted_iota(jnp.int32, (ATT_QB, ATT_KW), 1)
    return jnp.clip(qi - kj + ATT_PAD, -(CHUNK - 1), MAX_REL) + (CHUNK - 1)


def _att_in_band():
    qi = lax.broadcasted_iota(jnp.int32, (ATT_QB, ATT_KW), 0)
    kj = lax.broadcasted_iota(jnp.int32, (ATT_QB, ATT_KW), 1)
    shift = CHUNK.bit_length() - 1
    qc = jnp.right_shift(qi, shift)
    kc = jnp.right_shift(kj, shift) - LEFT_CHUNKS
    return (kc <= qc) & (kc >= qc - LEFT_CHUNKS)


def _att_valid(b):
    kj = lax.broadcasted_iota(jnp.int32, (ATT_QB, ATT_KW), 1)
    return kj + b * ATT_QB >= ATT_PAD


def _att_block(q_raw, k_raw, v, qw, kw, bias, valid):
    q = _rms(q_raw, qw)
    k = _rms(k_raw, kw)
    s = _dot_nt(_bf(q), _bf(k)) * (HEAD_DIM ** -0.5) + bias
    s = jnp.where(valid, s, NEG_INF)
    p = jnp.exp(s - jnp.max(s, axis=-1, keepdims=True))
    p = p * (1.0 / jnp.sum(p, axis=-1, keepdims=True))
    return _dot(_bf(p), _bf(v))


def _att_specs():
    q_spec = pl.BlockSpec((ATT_QB, HEAD_DIM), lambda h, b: (b, h))
    k_specs = [pl.BlockSpec((ATT_QB, HEAD_DIM), lambda h, b, j=j: (b + j, HEADS + h)) for j in range(3)]
    v_specs = [pl.BlockSpec((ATT_QB, HEAD_DIM), lambda h, b, j=j: (b + j, 2 * HEADS + h)) for j in range(3)]
    w_spec = pl.BlockSpec((1, HEAD_DIM), lambda h, b: (0, 0))
    smem = pl.BlockSpec(memory_space=pltpu.SMEM)
    return q_spec, k_specs, v_specs, w_spec, smem


def _att_fill_bias(bias_ref, rel_ref, h):
    idx = _att_rel_index()

    def fill(r, acc):
        return jnp.where(idx == r, rel_ref[h, r], acc)

    table = lax.fori_loop(0, N_REL, fill, jnp.zeros((ATT_QB, ATT_KW), F32))
    bias_ref[...] = jnp.where(_att_in_band(), table, NEG_INF)


def _attention(pb, pbp, qw, kw, rel):
    t_rows = pb.shape[0]
    q_spec, k_specs, v_specs, w_spec, smem = _att_specs()

    def body(q_ref, k0, k1, k2, v0, v1, v2, qw_ref, kw_ref, rel_ref, o_ref, bias_ref):
        h, b = pl.program_id(0), pl.program_id(1)

        @pl.when(b == 0)
        def _():
            _att_fill_bias(bias_ref, rel_ref, h)

        kwin = jnp.concatenate([k0[...], k1[...], k2[...]], axis=0)
        vwin = jnp.concatenate([v0[...], v1[...], v2[...]], axis=0)
        o = _att_block(q_ref[...], kwin, vwin, qw_ref[...], kw_ref[...], bias_ref[...], _att_valid(b))
        o_ref[...] = o.astype(o_ref.dtype)

    return pl.pallas_call(
        body, name="band_attention_fwd",
        out_shape=jax.ShapeDtypeStruct((t_rows, HW), BF16),
        grid=(HEADS, t_rows // ATT_QB),
        in_specs=[q_spec] + k_specs + v_specs + [w_spec, w_spec, smem],
        out_specs=pl.BlockSpec((ATT_QB, HEAD_DIM), lambda h, b: (b, h)),
        scratch_shapes=[pltpu.VMEM((ATT_QB, ATT_KW), F32)],
        compiler_params=_params(("arbitrary", "arbitrary")),
    )(pb, pbp, pbp, pbp, pbp, pbp, pbp, qw, kw, rel)


def _attention_bwd(pb, pbp, qw, kw, rel, dyb):
    t_rows = pb.shape[0]
    nb = t_rows // ATT_QB
    q_spec, k_specs, v_specs, w_spec, smem = _att_specs()
    pad_rows = t_rows + ATT_PAD
    acc_spec = pl.BlockSpec((pad_rows, HEAD_DIM), lambda h, b: (0, h))

    def body(q_ref, k0, k1, k2, v0, v1, v2, qw_ref, kw_ref, rel_ref, do_ref,
             dq_ref, dk_ref, dv_ref, dqw_ref, dkw_ref, drel_ref, bias_ref, dbias_ref):
        h, b = pl.program_id(0), pl.program_id(1)

        @pl.when(b == 0)
        def _():
            _att_fill_bias(bias_ref, rel_ref, h)
            dbias_ref[...] = jnp.zeros_like(dbias_ref)
            dk_ref[...] = jnp.zeros_like(dk_ref)
            dv_ref[...] = jnp.zeros_like(dv_ref)

        @pl.when((b == 0) & (h == 0))
        def _():
            dqw_ref[...] = jnp.zeros_like(dqw_ref)
            dkw_ref[...] = jnp.zeros_like(dkw_ref)

        kwin = jnp.concatenate([k0[...], k1[...], k2[...]], axis=0)
        vwin = jnp.concatenate([v0[...], v1[...], v2[...]], axis=0)
        valid = _att_valid(b)
        _, vjp = jax.vjp(lambda q, k, v, a, c, bias: _att_block(q, k, v, a, c, bias, valid),
                         q_ref[...], kwin, vwin, qw_ref[...], kw_ref[...], bias_ref[...])
        dq, dk, dv, dqw, dkw, dbias = vjp(do_ref[...])
        dq_ref[...] = dq.astype(dq_ref.dtype)
        win = pl.ds(pl.multiple_of(b * ATT_QB, ATT_QB), ATT_KW)
        dk_ref[win, :] += dk
        dv_ref[win, :] += dv
        dqw_ref[...] += dqw
        dkw_ref[...] += dkw
        dbias_ref[...] += dbias

        @pl.when(b == nb - 1)
        def _():
            idx = _att_rel_index()
            tot = dbias_ref[...]

            def reduce(r, carry):
                drel_ref[h, r] = jnp.sum(jnp.where(idx == r, tot, 0.0))
                return carry

            lax.fori_loop(0, N_REL, reduce, 0)

    return pl.pallas_call(
        body, name="band_attention_bwd",
        out_shape=[jax.ShapeDtypeStruct((t_rows, HW), BF16),
                   jax.ShapeDtypeStruct((pad_rows, HW), F32), jax.ShapeDtypeStruct((pad_rows, HW), F32),
                   jax.ShapeDtypeStruct((1, HEAD_DIM), F32), jax.ShapeDtypeStruct((1, HEAD_DIM), F32),
                   jax.ShapeDtypeStruct((HEADS, N_REL), F32)],
        grid=(HEADS, nb),
        in_specs=[q_spec] + k_specs + v_specs + [w_spec, w_spec, smem, q_spec],
        out_specs=[q_spec, acc_spec, acc_spec, w_spec, w_spec, smem],
        scratch_shapes=[pltpu.VMEM((ATT_QB, ATT_KW), F32), pltpu.VMEM((ATT_QB, ATT_KW), F32)],
        compiler_params=_params(("arbitrary", "arbitrary")),
    )(pb, pbp, pbp, pbp, pbp, pbp, pbp, qw, kw, rel, dyb)


def _me():
    return lax.axis_index("x"), lax.axis_index("y"), lax.axis_index("c")


def _index(x, y, c):
    return 4 * x + 2 * y + c


HBM_SPEC = pl.BlockSpec(memory_space=pl.ANY)


def _block(ref, kind, d, r, c):
    if kind == "rows":
        return ref.at[pl.ds(d * r, r), :]
    if kind == "win":
        return ref.at[:, pl.ds(d * WIN_STEP, c)]
    return ref.at[:, pl.ds(d * c, c)]


def _all_gather(shards, kinds):
    n = len(shards)

    def body(*refs):
        x_refs, out_refs = refs[:n], refs[n:2 * n]
        send_sems, recv_sems, local_sems = refs[2 * n:]
        x, y, c = _me()
        me, sibling = (x, y, c), (x, y, 1 - c)
        chips = [(1 - x, y), (x, 1 - y), (1 - x, 1 - y)]

        def copy(i, k, blk, to, src=None):
            r_, c_ = shards[i].shape
            dst = _block(out_refs[i], kinds[i], _index(*blk), r_, c_)
            return pltpu.make_async_remote_copy(
                src_ref=dst if src is None else src, dst_ref=dst,
                send_sem=send_sems.at[i, k], recv_sem=recv_sems.at[i, k], device_id=to, device_id_type=MESH)

        sends, local = [], []
        for i in range(n):
            r_, c_ = shards[i].shape
            mine = pltpu.make_async_copy(x_refs[i], _block(out_refs[i], kinds[i], _index(*me), r_, c_),
                                         local_sems.at[i])
            mine.start()
            local.append(mine)
            first = [copy(i, 0, me, sibling, src=x_refs[i])]
            first += [copy(i, 1 + j, me, (*chip, c), src=x_refs[i]) for j, chip in enumerate(chips)]
            for cp in first:
                cp.start()
            sends += first
        for i in range(n):
            for j, chip in enumerate(chips):
                copy(i, 1 + j, (*chip, c), me).wait_recv()
                passed = copy(i, 4 + j, (*chip, c), sibling)
                passed.start()
                sends.append(passed)
        for i in range(n):
            copy(i, 0, sibling, me).wait_recv()
            for j, chip in enumerate(chips):
                copy(i, 4 + j, (*chip, 1 - c), me).wait_recv()
        for cp in sends:
            cp.wait_send()
        for cp in local:
            cp.wait()

    def full_shape(s, kind):
        r_, c_ = s.shape
        return (N_DEV * r_, c_) if kind == "rows" else (r_, N_DEV * c_)

    return pl.pallas_call(
        body, name="weights_all_gather",
        out_shape=[jax.ShapeDtypeStruct(full_shape(s, k), s.dtype) for s, k in zip(shards, kinds)],
        in_specs=[HBM_SPEC] * n, out_specs=[HBM_SPEC] * n,
        scratch_shapes=[pltpu.SemaphoreType.DMA((n, 7)), pltpu.SemaphoreType.DMA((n, 7)),
                        pltpu.SemaphoreType.DMA((n,))],
        compiler_params=pltpu.CompilerParams(has_side_effects=True),
    )(*shards)


SEM_SPEC = pl.BlockSpec(memory_space=pltpu.SEMAPHORE)
HBM_ONLY = pl.BlockSpec(memory_space=pltpu.HBM)
DATAFLOW = pltpu.SideEffectType.DATAFLOW_SIDE_EFFECTING


def _peers():
    x, y, c = _me()
    return [(x ^ (k >> 2), y ^ ((k >> 1) & 1), c ^ (k & 1)) for k in range(1, N_DEV)]


def _gather_copies(shapes, kinds):
    def make(src_refs, land_refs, send_sems, recv_sems):
        mine = _index(*_me())
        return [pltpu.make_async_remote_copy(
            src_ref=src_refs[i], dst_ref=_block(land_refs[i], kind, mine, r, c),
            send_sem=send_sems.at[7 * i + k], recv_sem=recv_sems.at[7 * i + k], device_id=peer, device_id_type=MESH)
            for i, ((r, c), kind) in enumerate(zip(shapes, kinds)) for k, peer in enumerate(_peers())]

    return make


def _exchange_copies(shapes, kinds):
    def make(src_refs, land_refs, send_sems, recv_sems):
        mine = _index(*_me())
        return [pltpu.make_async_remote_copy(
            src_ref=_block(src_refs[i], kind, _index(*peer), r, c), dst_ref=land_refs[i].at[mine],
            send_sem=send_sems.at[7 * i + k], recv_sem=recv_sems.at[7 * i + k], device_id=peer, device_id_type=MESH)
            for i, ((r, c), kind) in enumerate(zip(shapes, kinds)) for k, peer in enumerate(_peers())]

    return make


def _place_own(srcs, lands, shapes, kinds, gather, name):
    n = len(srcs)

    def body(*refs):
        src_refs, out_refs, sems = refs[:n], refs[2 * n:3 * n], refs[3 * n]
        mine = _index(*_me())
        copies = []
        for i, ((r, c), kind) in enumerate(zip(shapes, kinds)):
            if gather:
                cp = pltpu.make_async_copy(src_refs[i], _block(out_refs[i], kind, mine, r, c), sems.at[i])
            else:
                cp = pltpu.make_async_copy(_block(src_refs[i], kind, mine, r, c), out_refs[i].at[mine], sems.at[i])
            cp.start()
            copies.append(cp)
        for cp in copies:
            cp.wait()

    return pl.pallas_call(
        body, name=name,
        out_shape=[jax.ShapeDtypeStruct(a.shape, a.dtype) for a in lands],
        in_specs=[HBM_SPEC] * (2 * n), out_specs=[HBM_SPEC] * n,
        input_output_aliases={n + i: i for i in range(n)},
        scratch_shapes=[pltpu.SemaphoreType.DMA((n,))],
        compiler_params=pltpu.CompilerParams(has_side_effects=True),
    )(*srcs, *lands)


def _split_start(srcs, lands, make, name):
    n = len(srcs)

    def body(*refs):
        send_sems, recv_sems = refs[2 * n], refs[2 * n + 1]
        for cp in make(refs[:n], refs[n:2 * n], send_sems, recv_sems):
            cp.start()
        refs[-1][...] = jnp.zeros_like(refs[-1])

    arrays = list(srcs) + list(lands)
    out = pl.pallas_call(
        body, name=name,
        out_shape=(pltpu.SemaphoreType.DMA((7 * n,)), pltpu.SemaphoreType.DMA((7 * n,)),
                   *[pltpu.HBM(a.shape, a.dtype) for a in arrays], jax.ShapeDtypeStruct((SUBLANES, LANES), F32)),
        in_specs=[HBM_ONLY] * (2 * n),
        out_specs=(SEM_SPEC, SEM_SPEC, *[HBM_ONLY] * (2 * n), pl.BlockSpec(memory_space=pltpu.VMEM)),
        input_output_aliases={i: 2 + i for i in range(2 * n)},
        compiler_params=pltpu.CompilerParams(has_side_effects=DATAFLOW),
    )(*[pltpu.with_memory_space_constraint(a, pltpu.HBM) for a in arrays])
    return out[0], out[1], list(out[2:2 + n]), list(out[2 + n:2 + 2 * n]), out[-1]


def _split_wait(send_sems, recv_sems, srcs, lands, after, make, name):
    n = len(srcs)

    def body(*refs):
        for cp in make(refs[:n], refs[n:2 * n], refs[2 * n], refs[2 * n + 1]):
            cp.wait_send()
            cp.wait_recv()

    arrays = list(srcs) + list(lands)
    out = pl.pallas_call(
        body, name=name,
        out_shape=tuple(pltpu.HBM(a.shape, a.dtype) for a in arrays),
        in_specs=[HBM_ONLY] * (2 * n) + [SEM_SPEC, SEM_SPEC, pl.BlockSpec(memory_space=pl.ANY)],
        out_specs=tuple([HBM_ONLY] * (2 * n)),
        input_output_aliases={i: i for i in range(2 * n)},
        compiler_params=pltpu.CompilerParams(has_side_effects=DATAFLOW),
    )(*arrays, send_sems, recv_sems, after)
    return list(out[n:])


def _all_reduce_small(vals, name):
    rows, width = vals.shape

    def body(x_ref, out_ref, buf_ref, send_sems, recv_sems):
        x, y, c = _me()
        mine = _index(x, y, c)
        buf_ref[mine] = x_ref[...]
        copies = []
        for k in range(1, N_DEV):
            px, py, pc = x ^ (k >> 2), y ^ ((k >> 1) & 1), c ^ (k & 1)
            copies.append(pltpu.make_async_remote_copy(
                src_ref=x_ref, dst_ref=buf_ref.at[mine],
                send_sem=send_sems.at[k - 1], recv_sem=recv_sems.at[k - 1],
                device_id=(px, py, pc), device_id_type=MESH))
        for cp in copies:
            cp.start()
        for cp in copies:
            cp.wait()
        acc = buf_ref[0]
        for j in range(1, N_DEV):
            acc = acc + buf_ref[j]
        out_ref[...] = acc

    vmem = pl.BlockSpec(memory_space=pltpu.VMEM)
    return pl.pallas_call(
        body, name=name,
        out_shape=jax.ShapeDtypeStruct(vals.shape, F32),
        in_specs=[vmem], out_specs=vmem,
        scratch_shapes=[pltpu.VMEM((N_DEV, rows, width), F32),
                        pltpu.SemaphoreType.DMA((7,)), pltpu.SemaphoreType.DMA((7,))],
        compiler_params=pltpu.CompilerParams(has_side_effects=True),
    )(vals)


def _adamw_math(w, g, m, v):
    m = ADAM_B1 * m + (1.0 - ADAM_B1) * g
    v = ADAM_B2 * v + (1.0 - ADAM_B2) * (g * g)
    m_hat = m / (1.0 - ADAM_B1 ** ADAM_STEP)
    v_hat = v / (1.0 - ADAM_B2 ** ADAM_STEP)
    delta = -ADAM_LR * (m_hat / (jnp.sqrt(v_hat) + ADAM_EPS) + ADAM_WD * w)
    return delta, m, v


ROW_TILE_ELEMS = 384 * 1024


def _row_tile(rows, width):
    best = SUBLANES
    for t in range(SUBLANES, rows + 1, SUBLANES):
        if rows % t == 0 and t * width <= ROW_TILE_ELEMS:
            best = t
    return best


def _sum_received(r_ref):
    g = r_ref[0].astype(F32)
    for j in range(1, N_DEV):
        g = g + r_ref[j].astype(F32)
    return g


def _adamw_recv(recv, w, m, v, name):
    _, rows, width = recv.shape
    tile = _row_tile(rows, width)

    def body(r_ref, w_ref, m_ref, v_ref, g_out, d_out, m_out, v_out):
        g = _sum_received(r_ref)
        d, mn, vn = _adamw_math(w_ref[...], g, m_ref[...], v_ref[...])
        g_out[...] = g
        d_out[...] = d
        m_out[...] = mn
        v_out[...] = vn

    spec = pl.BlockSpec((tile, width), lambda i: (i, 0))
    shape = jax.ShapeDtypeStruct((rows, width), F32)
    return pl.pallas_call(
        body, name=name,
        out_shape=[shape] * 4, grid=(rows // tile,),
        in_specs=[pl.BlockSpec((N_DEV, tile, width), lambda i: (0, i, 0)), spec, spec, spec],
        out_specs=[spec] * 4,
        compiler_params=_params(("parallel",)),
    )(recv, w, m, v)


WIN_STEP = 1408
WIN_W = 1536
IN_SHARD = IN_COLS // N_DEV
IN_PADDED = WIN_STEP * (N_DEV - 1) + WIN_W


def _roll_w_in(shard_padded):
    rows = shard_padded.shape[0]
    tile = _row_tile(rows, WIN_W)

    def body(x_ref, main_ref, edge_ref):
        win = pltpu.roll(x_ref[...], 2 * _index(*_me()), 1).astype(BF16)
        main_ref[...] = win[:, :WIN_STEP]
        edge_ref[...] = win[:, WIN_STEP:]

    return pl.pallas_call(
        body, name="w_in_window",
        out_shape=[jax.ShapeDtypeStruct((rows, WIN_STEP), BF16), jax.ShapeDtypeStruct((rows, WIN_W - WIN_STEP), BF16)],
        grid=(rows // tile,),
        in_specs=[pl.BlockSpec((tile, WIN_W), lambda i: (i, 0))],
        out_specs=[pl.BlockSpec((tile, WIN_STEP), lambda i: (i, 0)),
                   pl.BlockSpec((tile, WIN_W - WIN_STEP), lambda i: (i, 0))],
        compiler_params=_params(("parallel",)),
    )(shard_padded)


def _sum_w_in_windows(recv):
    _, rows, width = recv.shape
    tile = _row_tile(rows, width)

    def body(r_ref, g_out):
        g_out[...] = pltpu.roll(_sum_received(r_ref), width - 2 * _index(*_me()), 1)

    return pl.pallas_call(
        body, name="w_in_grad_sum",
        out_shape=jax.ShapeDtypeStruct((rows, width), F32), grid=(rows // tile,),
        in_specs=[pl.BlockSpec((N_DEV, tile, width), lambda i: (0, i, 0))],
        out_specs=pl.BlockSpec((tile, width), lambda i: (i, 0)),
        compiler_params=_params(("parallel",)),
    )(recv)


def _adamw_small(w, g, m, v, name):
    def fn(i, n, w_, g_, m_, v_):
        return _adamw_math(w_, g_, m_, v_)

    r, c = w.shape
    return _rows(fn, [(w, "t"), (g, "t"), (m, "t"), (v, "t")], [], [(c, F32)] * 3, [], _row_tile(r, c), name)


def _norm_fwd(x, w, name):
    return _rows(lambda i, n, x_, w_: (_rms(x_, w_[...]),), [(x, "t")], [w], [(D_MODEL, BF16)], [], 512, name)[0]


def _residual_norm_fwd(x, y, scale, w, name):
    def fn(i, n, x_, y_, w_):
        xn = x_ + scale * y_
        return xn, _rms(xn, w_[...])

    return _rows(fn, [(x, "t"), (y, "t")], [w], [(D_MODEL, F32), (D_MODEL, BF16)], [], 512, name)


def _residual_norm_bwd(x, w, dhs, dres, scale, name):
    nh = len(dhs)

    def fn(i, n, x_, dres_, *rest):
        dh = rest[0]
        for extra in rest[1:nh]:
            dh = dh + extra
        _, vjp = jax.vjp(_rms, x_, rest[nh][...])
        dx, dw = vjp(dh)
        dx = dx + dres_
        return dx, scale * dx, dw

    return _rows(fn, [(x, "t"), (dres, "t")] + [(d, "t") for d in dhs], [w],
                 [(D_MODEL, F32), (D_MODEL, BF16)], [(1, D_MODEL)], 256, name)


def _ffn_fwd(h, w_gu, w_down, tag):
    gu = _matmul(h, w_gu, "nn", F32, tag + "_gu")
    act = _rows(lambda i, n, gu_: (_swiglu(gu_),), [(gu, "t")], [], [(D_FF, BF16)], [], 128, tag + "_swiglu")[0]
    y = _matmul(act, w_down, "nn", F32, tag + "_down")
    return gu, act, y


def _ffn_bwd(h, gu, act, dy, w_gu, w_down, tag):
    d_w_down = _matmul(act, dy, "tn", BF16, tag + "_d_w_down")
    dact = _matmul(dy, w_down, "nt", F32, tag + "_dact")

    def fn(i, n, gu_, dact_):
        _, vjp = jax.vjp(_swiglu, gu_)
        return vjp(dact_)

    dgu = _rows(fn, [(gu, "t"), (dact, "t")], [], [(2 * D_FF, BF16)], [], 128, tag + "_swiglu_bwd")[0]
    d_w_gu = _matmul(h, dgu, "tn", BF16, tag + "_d_w_gu")
    dh = _matmul(dgu, w_gu, "nt", F32, tag + "_dh")
    return dh, d_w_gu, d_w_down


def _expanders():
    e_g = np.zeros((LANES, HW), np.float32)
    e_b = np.zeros((LANES, HW), np.float32)
    for h in range(HEADS):
        e_g[h, h * HEAD_DIM:(h + 1) * HEAD_DIM] = 1.0
        e_b[HEADS + h, h * HEAD_DIM:(h + 1) * HEAD_DIM] = 1.0
    return jnp.asarray(e_g), jnp.asarray(e_b)


def _pad_lanes(v):
    return jnp.pad(v, ((0, 0), (0, LANES - v.shape[1])))


class _LocalWeights:
    def __init__(self, big):
        self.big, self.sent = big, {}

    def arrive(self, group, after):
        return self.big

    def send(self, group, grads):
        self.sent.update(grads)
        return jnp.zeros((), F32)


def _local_step(x, p, tgt, small, comm):
    big = comm.arrive("ffn1", None)
    e_g, e_b = _expanders()
    alog, dtb = _pad_lanes(small["a_log"]), _pad_lanes(small["dt_bias"])
    conv_w = jnp.pad(small["conv_w"], ((0, SUBLANES - CONV_K), (0, 0)))
    rel = small["rel_bias"]

    h1 = _norm_fwd(x, small["ffn1_norm"], "ffn1_norm")
    gu1, act1, y1 = _ffn_fwd(h1, big["ffn1_w_gu"], big["ffn1_w_down"], "ffn1")
    x1, h2 = _residual_norm_fwd(x, y1, 0.5, small["mix_norm"], "mix_norm")

    big = {**big, **comm.arrive("mixer", h2)}
    w_in = big["w_in"]
    w_qz = w_in[:, :IN_QZ]
    w_ab = jnp.pad(w_in[:, IN_AB0:IN_QKVB0], ((0, 0), (0, LANES - 2 * HEADS)))
    w_qkvb = w_in[:, IN_QKVB0:IN_GG0]
    w_gg = w_in[:, IN_GG0:IN_COLS]
    qz = _matmul(h2, w_qz, "nn", F32, "in_qz")
    ab = _matmul(h2, w_ab, "nn", F32, "in_ab")
    pb = _matmul(h2, w_qkvb, "nn", F32, "in_qkvb")
    gg = _matmul(h2, w_gg, "nn", F32, "in_gates")
    pa, z = qz[:, :3 * HW], qz[:, 3 * HW:]

    def prep(i, n, pa_, prev_, ab_, cw_, alog_, dtb_, eg_, eb_):
        q, k, v = _gdn_post(_conv(pa_, prev_, cw_, i))
        g_b, beta_b = _gdn_gates(ab_, alog_[...], dtb_[...], eg_[...], eb_[...])
        return q, k, v, g_b, beta_b

    qn, kn, vv, g_b, beta_b = _rows(prep, [(pa, "t"), (pa, "p"), (ab, "t")], [conv_w, alog, dtb, e_g, e_b],
                                    [(HW, F32)] * 5, [], 256, "gdn_prep")
    u, w, aqk, qd, kt, tl = _gdn_intra(qn, kn, vv, g_b, beta_b)
    o, states = _gdn_scan(u, w, aqk, qd, kt, tl)
    ya = _rows(lambda i, n, o_, z_, w_: (_gated_norm(o_, z_, w_[...]),), [(o, "t"), (z, "t")], [small["gdn_norm"]],
               [(HW, BF16)], [], 512, "gdn_gated_norm")[0]

    pbp = jnp.pad(pb, ((ATT_PAD, 0), (0, 0)))
    yb = _attention(pb, pbp, small["q_norm"], small["k_norm"], rel)

    ta = _matmul(ya, big["w_branch_a"], "nn", F32, "branch_a")
    tb = _matmul(yb, big["w_branch_b"], "nn", F32, "branch_b")
    mixed = _rows(lambda i, n, gg_, ta_, tb_: (_mix(gg_, ta_, tb_),), [(gg, "t"), (ta, "t"), (tb, "t")], [],
                  [(D_MODEL, BF16)], [], 256, "mix")[0]
    m_out = _matmul(mixed, big["w_out"], "nn", F32, "w_out")
    x2, h3 = _residual_norm_fwd(x1, m_out, 1.0, small["ffn2_norm"], "ffn2_norm")
    big = {**big, **comm.arrive("tail", h3)}
    gu2, act2, y2 = _ffn_fwd(h3, big["ffn2_w_gu"], big["ffn2_w_down"], "ffn2")
    x3, h4 = _residual_norm_fwd(x2, y2, 0.5, small["ple_norm"], "ple_norm")
    gp = _matmul(h4, big["ple_gate"], "nn", F32, "ple_gate")
    pp = _matmul(p, big["ple_proj"], "nn", F32, "ple_proj")

    def head(i, n, x3_, gp_, pp_, tgt_):
        sg = _sigmoid(gp_)
        err = x3_ + sg * pp_ - tgt_
        dx4 = err * (1.0 / D_MODEL)
        sq = _colsum(err * err)
        part = sq[:, :LANES]
        for j in range(1, D_MODEL // LANES):
            part = part + sq[:, j * LANES:(j + 1) * LANES]
        return dx4, dx4 * pp_ * sg * (1.0 - sg), dx4 * sg, (0.5 / D_MODEL) * part

    dx4, dgp, dpp, loss_lanes = _rows(head, [(x3, "t"), (gp, "t"), (pp, "t"), (tgt, "t")], [],
                                      [(D_MODEL, F32), (D_MODEL, BF16), (D_MODEL, BF16)], [(1, LANES)], 256,
                                      "ple_loss_head")
    loss = jnp.sum(loss_lanes)

    gbig, gsmall = {}, {}
    gbig["ple_proj"] = _matmul(p, dpp, "tn", BF16, "d_ple_proj")
    gbig["ple_gate"] = _matmul(h4, dgp, "tn", BF16, "d_ple_gate")
    dh4 = _matmul(dgp, big["ple_gate"], "nt", F32, "ple_gate_dh")
    dx3, dy2, gsmall["ple_norm"] = _residual_norm_bwd(x3, small["ple_norm"], [dh4], dx4, 0.5, "ple_norm_bwd")

    dh3, gbig["ffn2_w_gu"], gbig["ffn2_w_down"] = _ffn_bwd(h3, gu2, act2, dy2, big["ffn2_w_gu"],
                                                           big["ffn2_w_down"], "ffn2")
    sent = comm.send("tail", {n: gbig[n] for n in ("ple_proj", "ple_gate", "ffn2_w_down", "ffn2_w_gu")})
    dx2, dx2b, gsmall["ffn2_norm"] = _residual_norm_bwd(x2, small["ffn2_norm"] + sent, [dh3], dx3, 1.0,
                                                        "ffn2_norm_bwd")

    gbig["w_out"] = _matmul(mixed, dx2b, "tn", BF16, "d_w_out")
    dmixed = _matmul(dx2b, big["w_out"], "nt", F32, "w_out_dx")

    def mix_bwd(i, n, gg_, ta_, tb_, dm_):
        _, vjp = jax.vjp(_mix, gg_, ta_, tb_)
        return vjp(dm_)

    dgg, dta, dtb_ = _rows(mix_bwd, [(gg, "t"), (ta, "t"), (tb, "t"), (dmixed, "t")], [],
                           [(2 * D_MODEL, BF16), (D_MODEL, BF16), (D_MODEL, BF16)], [], 256, "mix_bwd")
    gbig["w_branch_a"] = _matmul(ya, dta, "tn", BF16, "d_branch_a")
    gbig["w_branch_b"] = _matmul(yb, dtb_, "tn", BF16, "d_branch_b")
    dya = _matmul(dta, big["w_branch_a"], "nt", F32, "branch_a_dx")
    dyb = _matmul(dtb_, big["w_branch_b"], "nt", F32, "branch_b_dx")

    dq_b, dk_b, dv_b, gsmall["q_norm"], gsmall["k_norm"], gsmall["rel_bias"] = _attention_bwd(
        pb, pbp, small["q_norm"], small["k_norm"], rel, dyb)
    dpb = jnp.concatenate([dq_b, dk_b[ATT_PAD:].astype(BF16), dv_b[ATT_PAD:].astype(BF16)], axis=1)

    def gated_bwd(i, n, o_, z_, dya_, w_):
        _, vjp = jax.vjp(_gated_norm, o_, z_, w_[...])
        return vjp(dya_)

    do, dz, gsmall["gdn_norm"] = _rows(gated_bwd, [(o, "t"), (z, "t"), (dya, "t")], [small["gdn_norm"]],
                                       [(HW, F32), (HW, BF16)], [(1, HEAD_DIM)], 256, "gdn_gated_norm_bwd")
    du, dw, da, dqd, dkt, dtl = _gdn_scan_bwd(do, u, w, aqk, qd, kt, tl, states)
    dqn, dkn, dvv, dg_b, dbeta_b = _gdn_intra_bwd(qn, kn, vv, g_b, beta_b, du, dw, da, dqd, dkt, dtl)

    def prep_bwd(i, n, pa_, prev_, ab_, dq_, dk_, dv_, dg_, db_, cw_, alog_, dtb_, eg_, eb_):
        _, vjp = jax.vjp(_gdn_post, _conv(pa_, prev_, cw_, i))
        (dy,) = vjp((dq_, dk_, dv_))
        e_g_, e_b_ = eg_[...], eb_[...]
        _, vjp_g = jax.vjp(lambda a, b, c: _gdn_gates(a, b, c, e_g_, e_b_), ab_, alog_[...], dtb_[...])
        dab, dalog, ddtb = vjp_g((dg_, db_))
        return dy, dab, dalog, ddtb

    dy_conv, dab, dalog, ddtb = _rows(
        prep_bwd, [(pa, "t"), (pa, "p"), (ab, "t"), (dqn, "t"), (dkn, "t"), (dvv, "t"), (dg_b, "t"), (dbeta_b, "t")],
        [conv_w, alog, dtb, e_g, e_b], [(3 * HW, F32), (LANES, BF16)], [(1, LANES), (1, LANES)], 256,
        "gdn_prep_bwd")
    gsmall["a_log"] = dalog[:, :HEADS]
    gsmall["dt_bias"] = ddtb[:, :HEADS]

    def conv_bwd(i, n, dy_, nxt_, pa_, prev_, cw_):
        dpa = dy_ * cw_[CONV_K - 1:CONV_K, :]
        row = lax.broadcasted_iota(jnp.int32, (SUBLANES, dy_.shape[1]), 0)
        dcw = jnp.where(row == CONV_K - 1, _colsum(dy_ * pa_), 0.0)
        for j in range(CONV_K - 1):
            s = CONV_K - 1 - j
            dpa = dpa + _shift_up(dy_, nxt_, s, i, n) * cw_[j:j + 1, :]
            dcw = dcw + jnp.where(row == j, _colsum(dy_ * _shift_down(pa_, prev_, s, i)), 0.0)
        return dpa, dcw

    dpa, dcw = _rows(conv_bwd, [(dy_conv, "t"), (dy_conv, "n"), (pa, "t"), (pa, "p")], [conv_w],
                     [(3 * HW, BF16)], [(SUBLANES, 3 * HW)], 256, "gdn_conv_bwd")
    gsmall["conv_w"] = dcw[:CONV_K]

    dqz = jnp.concatenate([dpa, dz], axis=1)
    d_w_qz = _matmul(h2, dqz, "tn", BF16, "d_in_qz")
    d_w_ab = _matmul(h2, dab, "tn", BF16, "d_in_ab")
    d_w_qkvb = _matmul(h2, dpb, "tn", BF16, "d_in_qkvb")
    d_w_gg = _matmul(h2, dgg, "tn", BF16, "d_in_gates")
    gbig["w_in"] = jnp.concatenate([d_w_qz, d_w_ab[:, :2 * HEADS], d_w_qkvb, d_w_gg,
                                    jnp.zeros((D_MODEL, IN_PADDED - IN_COLS), BF16)], axis=1)
    dh2 = [_matmul(dqz, w_qz, "nt", F32, "in_qz_dh"), _matmul(dab, w_ab, "nt", F32, "in_ab_dh"),
           _matmul(dpb, w_qkvb, "nt", F32, "in_qkvb_dh"), _matmul(dgg, w_gg, "nt", F32, "in_gates_dh")]
    sent = comm.send("mixer", {n: gbig[n] for n in ("w_out", "w_branch_b", "w_branch_a", "w_in")})
    dx1, dy1, gsmall["mix_norm"] = _residual_norm_bwd(x1, small["mix_norm"] + sent, dh2, dx2, 0.5, "mix_norm_bwd")

    dh1, gbig["ffn1_w_gu"], gbig["ffn1_w_down"] = _ffn_bwd(h1, gu1, act1, dy1, big["ffn1_w_gu"],
                                                           big["ffn1_w_down"], "ffn1")
    sent = comm.send("ffn1", {n: gbig[n] for n in ("ffn1_w_down", "ffn1_w_gu")})
    grad_x, _, gsmall["ffn1_norm"] = _residual_norm_bwd(x, small["ffn1_norm"] + sent, [dh1], dx1, 1.0,
                                                        "ffn1_norm_bwd")
    return loss, grad_x, gsmall


GATHER_GROUPS = {"ffn1": ("ffn1_w_gu", "ffn1_w_down"),
                 "mixer": ("w_in_main", "w_in_edge", "w_branch_a", "w_branch_b", "w_out"),
                 "tail": ("ffn2_w_gu", "ffn2_w_down", "ple_gate", "ple_proj")}


def _kind(name):
    return "cols" if name in COL_SHARDED or name.startswith("w_in_") else "rows"


def _merge_w_in(main, edges):
    edge_w = WIN_W - WIN_STEP
    w_in = jnp.pad(main, ((0, 0), (0, edge_w)))
    for d in range(N_DEV):
        at = WIN_STEP * (d + 1)
        w_in = w_in + jnp.pad(edges[:, d * edge_w:(d + 1) * edge_w], ((0, 0), (at, IN_PADDED - at - edge_w)))
    return w_in


class _Fsdp:
    def __init__(self, wts):
        main, edge = _roll_w_in(jnp.pad(wts["w_in"], ((0, 0), (0, WIN_W - IN_SHARD))))
        self.shards = {n: wts[n].astype(BF16) for n in BIG if n != "w_in"}
        self.shards.update(w_in_main=main, w_in_edge=edge)
        names = GATHER_GROUPS["ffn1"]
        self.first = dict(zip(names, _all_gather([self.shards[n] for n in names], [_kind(n) for n in names])))
        self.flight, token = {}, self.first["ffn1_w_down"][0, 0].astype(F32) * 0.0
        for group in ("mixer", "tail"):
            names = GATHER_GROUPS[group]
            srcs = [self.shards[n] for n in names]
            kinds = [_kind(n) for n in names]
            shapes = [s.shape for s in srcs]
            lands = [lax.empty((s[0] * (N_DEV if k == "rows" else 1), s[1] * (N_DEV if k == "cols" else 1)), BF16)
                     for s, k in zip(shapes, kinds)]
            lands = _place_own(srcs, lands, shapes, kinds, True, "gather_own_" + group)
            make = _gather_copies(shapes, kinds)
            srcs[0] = srcs[0] + token.astype(BF16)
            send_sems, recv_sems, srcs, lands, tok = _split_start(srcs, lands, make, "gather_start_" + group)
            token = token + tok[0, 0]
            self.flight[group] = (send_sems, recv_sems, srcs, lands, make)
        self.token = token
        self.sent = {}

    def arrive(self, group, after):
        if group == "ffn1":
            return self.first
        send_sems, recv_sems, srcs, lands, make = self.flight[group]
        lands = _split_wait(send_sems, recv_sems, srcs, lands, after, make, "gather_wait_" + group)
        full = dict(zip(GATHER_GROUPS[group], lands))
        if group == "mixer":
            full["w_in"] = _merge_w_in(full.pop("w_in_main"), full.pop("w_in_edge"))
        return full

    def send(self, group, grads):
        names = list(grads)
        kinds = ["win" if n == "w_in" else _kind(n) for n in names]
        shapes = [(D_MODEL, WIN_W) if n == "w_in" else self.shards[n].shape for n in names]
        srcs = [grads[n] for n in names]
        lands = [lax.empty((N_DEV,) + tuple(s), BF16) for s in shapes]
        lands = _place_own(srcs, lands, shapes, kinds, False, "grads_own_" + group)
        make = _exchange_copies(shapes, kinds)
        send_sems, recv_sems, srcs, lands, tok = _split_start(srcs, lands, make, "grads_start_" + group)
        self.sent[group] = (names, send_sems, recv_sems, srcs, lands, make)
        return tok[0, 0]

    def received(self, group, after):
        names, send_sems, recv_sems, srcs, lands, make = self.sent[group]
        return dict(zip(names, _split_wait(send_sems, recv_sems, srcs, lands, after, make, "grads_wait_" + group)))


SMALL_ROWS = ("ffn1_norm", "mix_norm", "ffn2_norm", "ple_norm", "gdn_norm", "q_norm", "k_norm", "a_log", "dt_bias",
              "rel_bias", "conv_w")


def _pack_small(vals):
    rows = []
    for n in SMALL_ROWS:
        v = vals[n]
        if n == "rel_bias":
            v = jnp.pad(v, ((0, 0), (0, 2 * LANES - N_REL)))
        elif n in ("a_log", "dt_bias"):
            v = _pad_lanes(v)
        rows.append(v.reshape(-1, LANES))
    packed = jnp.concatenate(rows, axis=0)
    return jnp.pad(packed, ((0, -packed.shape[0] % SUBLANES), (0, 0)))


def _unpack_small(packed, shapes):
    out, off = {}, 0
    for n in SMALL_ROWS:
        shp = shapes[n]
        if n == "rel_bias":
            out[n] = packed[off:off + 2 * HEADS].reshape(HEADS, 2 * LANES)[:, :N_REL]
            off += 2 * HEADS
        elif n in ("a_log", "dt_bias"):
            out[n] = packed[off:off + 1, :HEADS]
            off += 1
        else:
            r = int(np.prod(shp)) // LANES
            out[n] = packed[off:off + r].reshape(shp)
            off += r
    return out


WEIGHTS = ("ffn1_norm", "ffn1_w_gu", "ffn1_w_down", "mix_norm", "w_in", "conv_w", "a_log", "dt_bias", "gdn_norm",
           "q_norm", "k_norm", "rel_bias", "w_branch_a", "w_branch_b", "w_out", "ffn2_norm", "ffn2_w_gu",
           "ffn2_w_down", "ple_norm", "ple_gate", "ple_proj")


def kernel(x, p, ffn1_norm, ffn1_w_gu, ffn1_w_down, mix_norm, w_in, conv_w, a_log, dt_bias, gdn_norm, q_norm, k_norm, rel_bias, w_branch_a, w_branch_b, w_out, ffn2_norm, ffn2_w_gu, ffn2_w_down, ple_norm, ple_gate, ple_proj, loss_target, m_ffn1_norm, m_ffn1_w_gu, m_ffn1_w_down, m_mix_norm, m_w_in, m_conv_w, m_a_log, m_dt_bias, m_gdn_norm, m_q_norm, m_k_norm, m_rel_bias, m_w_branch_a, m_w_branch_b, m_w_out, m_ffn2_norm, m_ffn2_w_gu, m_ffn2_w_down, m_ple_norm, m_ple_gate, m_ple_proj, v_ffn1_norm, v_ffn1_w_gu, v_ffn1_w_down, v_mix_norm, v_w_in, v_conv_w, v_a_log, v_dt_bias, v_gdn_norm, v_q_norm, v_k_norm, v_rel_bias, v_w_branch_a, v_w_branch_b, v_w_out, v_ffn2_norm, v_ffn2_w_gu, v_ffn2_w_down, v_ple_norm, v_ple_gate, v_ple_proj):
    args = dict(locals())
    def layer0(v):
        return v[0] if v.ndim == 3 else v

    wts = {n: layer0(args[n]) for n in WEIGHTS}
    mom = {n: layer0(args["m_" + n]) for n in WEIGHTS}
    var = {n: layer0(args["v_" + n]) for n in WEIGHTS}
    x2d, p2d, tgt = x[0], p[0, 0], loss_target[0]
    my_index = _index(*_me())

    fsdp = _Fsdp(wts)

    small = {n: wts[n] for n in SMALL_ROWS if n != "conv_w"}
    small["ffn1_norm"] = small["ffn1_norm"] + fsdp.token
    conv_shard = wts["conv_w"]
    conv_cols = conv_shard.shape[1]
    conv_packed = jnp.zeros((SUBLANES, N_DEV * conv_cols), F32)
    conv_packed = lax.dynamic_update_slice(conv_packed, jnp.pad(conv_shard, ((0, SUBLANES - CONV_K), (0, 0))),
                                           (0, my_index * conv_cols))
    small["conv_w"] = _all_reduce_small(conv_packed.reshape(-1, LANES), "conv_w_gather").reshape(SUBLANES, -1)[:CONV_K]

    loss, grad_x, gsmall = _local_step(x2d, p2d, tgt, small, fsdp)
    loss = lax.psum(loss, ("x", "y", "c"))

    outs_big, after = {}, grad_x
    for group in ("tail", "mixer", "ffn1"):
        for n, recv in fsdp.received(group, after).items():
            if n == "w_in":
                g_in = _sum_w_in_windows(recv)[:, :IN_SHARD]
                outs_big[n] = [g_in] + list(_adamw_small(wts[n], g_in, mom[n], var[n], "adamw_w_in"))
            else:
                outs_big[n] = _adamw_recv(recv, wts[n], mom[n], var[n], "adamw_" + n)
            after = outs_big[n][1]

    small_shapes = {n: (small[n].shape if n != "conv_w" else (CONV_K, N_DEV * conv_cols)) for n in SMALL_ROWS}
    gsum = _unpack_small(_all_reduce_small(_pack_small(gsmall), "small_grads_all_reduce"), small_shapes)
    gsum["conv_w"] = lax.dynamic_slice(gsum["conv_w"], (0, my_index * conv_cols), (CONV_K, conv_cols))
    rep = [n for n in SMALL_ROWS if n != "conv_w"]
    rep_shapes = {n: small_shapes[n] for n in rep}

    def pack_rep(vals):
        return _pack_small({**{n: vals[n] for n in rep}, "conv_w": jnp.zeros((CONV_K, LANES), F32)})

    def unpack_rep(packed):
        return _unpack_small(packed, {**rep_shapes, "conv_w": (CONV_K, LANES)})

    outs_small = [unpack_rep(o) for o in _adamw_small(pack_rep(wts), pack_rep(gsum), pack_rep(mom), pack_rep(var),
                                                      "adamw_replicated")]
    pad8 = functools.partial(jnp.pad, pad_width=((0, SUBLANES - CONV_K), (0, 0)))
    outs_conv = [o[:CONV_K] for o in _adamw_small(pad8(conv_shard), pad8(gsum["conv_w"]), pad8(mom["conv_w"]),
                                                   pad8(var["conv_w"]), "adamw_conv")]

    def leaf(kind, n):
        if n in BIG:
            return outs_big[n][kind][None]
        if n == "conv_w":
            return (gsum["conv_w"] if kind == 0 else outs_conv[kind - 1])[None]
        return (gsum[n] if kind == 0 else outs_small[kind - 1][n]).reshape(args[n].shape)

    result = [loss, grad_x[None]]
    for kind in range(4):
        result += [leaf(kind, n) for n in WEIGHTS]
    return tuple(result)
```

```python
import functools

import numpy as np
import jax
import jax.numpy as jnp
from jax import lax
from jax.experimental import pallas as pl
from jax.experimental.pallas import tpu as pltpu

F32 = jnp.float32
BF16 = jnp.bfloat16
HIGHEST = lax.Precision.HIGHEST
MESH = pl.DeviceIdType.MESH

D_MODEL = 2048
D_FF = 5632
HEADS = 8
HEAD_DIM = 128
HW = HEADS * HEAD_DIM
CHUNK = 64
LEFT_CHUNKS = 8
MAX_REL = 128
N_REL = (CHUNK - 1) + MAX_REL + 1
CONV_K = 4
EPS = 1e-6
NEG_INF = -1e30
N_DEV = 8
LANES = 128
SUBLANES = 8
VMEM_LIMIT = 56 * 1024 * 1024

MATMUL_WHOLE_K = 2048

ATT_QB = 256
ATT_KW = ATT_QB + LEFT_CHUNKS * CHUNK
ATT_PAD = LEFT_CHUNKS * CHUNK
GDN_CB = 8
GDN_GROUP = 8

ADAM_LR = 0.001
ADAM_B1 = 0.9
ADAM_B2 = 0.999
ADAM_EPS = 1e-08
ADAM_WD = 0.01
ADAM_STEP = 10

IN_QZ = 3 * HW + HW
IN_AB0 = IN_QZ
IN_QKVB0 = IN_AB0 + 2 * HEADS
IN_GG0 = IN_QKVB0 + 3 * HW
IN_COLS = IN_GG0 + 2 * D_MODEL

BIG = ("ffn1_w_gu", "ffn1_w_down", "w_in", "w_branch_a", "w_branch_b", "w_out",
       "ffn2_w_gu", "ffn2_w_down", "ple_gate", "ple_proj")
COL_SHARDED = ("ffn1_w_gu", "w_in", "w_branch_a", "w_branch_b", "ffn2_w_gu", "ple_proj")


def _params(semantics=None, **kw):
    return pltpu.CompilerParams(dimension_semantics=semantics, vmem_limit_bytes=VMEM_LIMIT, **kw)


def _pick(n, cands):
    for c in cands:
        if n % c == 0:
            return c
    return n


def _matmul(a, b, mode, out_dtype, name):
    if mode == "nn":
        (m, k), (k2, n) = a.shape, b.shape
    elif mode == "nt":
        (m, k), (n, k2) = a.shape, b.shape
    else:
        (k, m), (k2, n) = a.shape, b.shape
    assert k == k2, (a.shape, b.shape, mode)
    tm = _pick(m, (1024, 512, 256, 128))
    tn = _pick(n, (1024, 512, 256, 128))
    tk = k if k <= MATMUL_WHOLE_K else _pick(k, (1024, 512, 256, 128))
    nk = k // tk
    if mode == "nn":
        a_spec = pl.BlockSpec((tm, tk), lambda i, j, kk: (i, kk))
        b_spec = pl.BlockSpec((tk, tn), lambda i, j, kk: (kk, j))
        dims = (((1,), (0,)), ((), ()))
    elif mode == "nt":
        a_spec = pl.BlockSpec((tm, tk), lambda i, j, kk: (i, kk))
        b_spec = pl.BlockSpec((tn, tk), lambda i, j, kk: (j, kk))
        dims = (((1,), (1,)), ((), ()))
    else:
        a_spec = pl.BlockSpec((tk, tm), lambda i, j, kk: (kk, i))
        b_spec = pl.BlockSpec((tk, tn), lambda i, j, kk: (kk, j))
        dims = (((0,), (0,)), ((), ()))

    def body(a_ref, b_ref, o_ref, *acc):
        prod = lax.dot_general(a_ref[...].astype(BF16), b_ref[...].astype(BF16), dims, preferred_element_type=F32)
        if nk == 1:
            o_ref[...] = prod.astype(o_ref.dtype)
            return
        acc_ref, kk = acc[0], pl.program_id(2)

        @pl.when(kk == 0)
        def _():
            acc_ref[...] = prod

        @pl.when((kk > 0) & (kk < nk - 1))
        def _():
            acc_ref[...] += prod

        @pl.when(kk == nk - 1)
        def _():
            o_ref[...] = (acc_ref[...] + prod).astype(o_ref.dtype)

    return pl.pallas_call(
        body, name=name,
        out_shape=jax.ShapeDtypeStruct((m, n), out_dtype),
        grid=(m // tm, n // tn, nk),
        in_specs=[a_spec, b_spec],
        out_specs=pl.BlockSpec((tm, tn), lambda i, j, kk: (i, j)),
        scratch_shapes=[pltpu.VMEM((tm, tn), F32)] if nk > 1 else [],
        compiler_params=_params(("parallel", "parallel", "arbitrary")),
    )(a, b)


def _rows(fn, row_ins, consts, row_outs, acc_outs, tile, name):
    t_rows = row_ins[0][0].shape[0]
    tile = min(tile, t_rows)
    assert t_rows % tile == 0 and tile % SUBLANES == 0
    n = t_rows // tile
    per = tile // SUBLANES
    last8 = t_rows // SUBLANES - 1
    in_specs = []
    for arr, kind in row_ins:
        c = arr.shape[1]
        if kind == "t":
            in_specs.append(pl.BlockSpec((tile, c), lambda i: (i, 0)))
        elif kind == "p":
            in_specs.append(pl.BlockSpec((SUBLANES, c), lambda i: (jnp.maximum(i * per - 1, 0), 0)))
        else:
            in_specs.append(pl.BlockSpec((SUBLANES, c), lambda i: (jnp.minimum((i + 1) * per, last8), 0)))
    for arr in consts:
        in_specs.append(pl.BlockSpec(arr.shape, lambda i, nd=arr.ndim: (0,) * nd))
    out_shape = [jax.ShapeDtypeStruct((t_rows, c), dt) for c, dt in row_outs]
    out_specs = [pl.BlockSpec((tile, c), lambda i: (i, 0)) for c, _ in row_outs]
    for shp in acc_outs:
        out_shape.append(jax.ShapeDtypeStruct(shp, F32))
        out_specs.append(pl.BlockSpec(shp, lambda i, nd=len(shp): (0,) * nd))
    n_in = len(row_ins) + len(consts)
    n_row_out = len(row_outs)

    def body(*refs):
        i = pl.program_id(0)
        vals = [r[...] for r in refs[:len(row_ins)]]
        res = fn(i, n, *vals, *refs[len(row_ins):n_in])
        outs = refs[n_in:]
        for r, v in zip(outs[:n_row_out], res[:n_row_out]):
            r[...] = v.astype(r.dtype)
        if acc_outs:
            @pl.when(i == 0)
            def _():
                for r in outs[n_row_out:]:
                    r[...] = jnp.zeros_like(r)

            for r, v in zip(outs[n_row_out:], res[n_row_out:]):
                r[...] += v

    res = pl.pallas_call(
        body, name=name, out_shape=out_shape, grid=(n,), in_specs=in_specs, out_specs=out_specs,
        compiler_params=_params(("arbitrary",) if acc_outs else ("parallel",)),
    )(*[a for a, _ in row_ins], *consts)
    return res


def _rms(x, w):
    return x * lax.rsqrt(jnp.mean(x * x, axis=-1, keepdims=True) + EPS) * w


def _l2n(x):
    return x * lax.rsqrt(jnp.sum(x * x, axis=-1, keepdims=True) + EPS)


def _sigmoid(x):
    return 1.0 / (1.0 + jnp.exp(-x))


def _silu(x):
    return x * _sigmoid(x)


def _softplus(x):
    return jnp.maximum(x, 0.0) + jnp.log(1.0 + jnp.exp(-jnp.abs(x)))


def _heads(fn, *xs):
    nh = xs[0].shape[1] // HEAD_DIM
    return jnp.concatenate(
        [fn(*[x[:, h * HEAD_DIM:(h + 1) * HEAD_DIM] for x in xs]) for h in range(nh)], axis=1)


def _colsum(x):
    return jnp.sum(x, axis=0, keepdims=True)


def _swiglu(gu):
    return _silu(gu[:, :D_FF]) * gu[:, D_FF:]


def _gated_norm(o, z, w):
    return _heads(lambda oh, zh: _rms(oh, w) * _silu(zh), o, z)


def _mix(gg, ta, tb):
    return _sigmoid(gg[:, :D_MODEL]) * ta + _sigmoid(gg[:, D_MODEL:]) * tb


def _gdn_post(y):
    a = _silu(y)
    q = _heads(lambda v: _l2n(v) * (HEAD_DIM ** -0.5), a[:, :HW])
    k = _heads(_l2n, a[:, HW:2 * HW])
    return q, k, a[:, 2 * HW:]


NN = (((1,), (0,)), ((), ()))
NT = (((1,), (1,)), ((), ()))
TN = (((0,), (0,)), ((), ()))


def _dg(a, b, dims):
    return lax.dot_general(a, b, dims, preferred_element_type=F32)


def _split2(x):
    hi = x.astype(BF16)
    return hi, (x - hi.astype(F32)).astype(BF16)


def _split3(x):
    hi = x.astype(BF16)
    r = x - hi.astype(F32)
    mid = r.astype(BF16)
    return hi, mid, (r - mid.astype(F32)).astype(BF16)


def _dg3(a, b, dims):
    ah, al = _split2(a)
    bh, bl = _split2(b)
    return _dg(ah, bh, dims) + (_dg(ah, bl, dims) + _dg(al, bh, dims))


BNN = (((2,), (1,)), ((0,), (0,)))
BNT = (((2,), (2,)), ((0,), (0,)))
BTN = (((1,), (1,)), ((0,), (0,)))


@jax.custom_vjp
def _mm3(a, b):
    return _dg3(a, b, BNN)


_mm3.defvjp(lambda a, b: (_dg3(a, b, BNN), (a, b)),
            lambda res, g: (_dg3(g, res[1], BNT), _dg3(res[0], g, BTN)))


def _xm(x, m, dims):
    mb = m.astype(BF16)
    parts = _split3(x)
    return _dg(parts[0], mb, dims) + (_dg(parts[1], mb, dims) + _dg(parts[2], mb, dims))


def _mx(m, x, dims):
    mb = m.astype(BF16)
    parts = _split3(x)
    return _dg(mb, parts[0], dims) + (_dg(mb, parts[1], dims) + _dg(mb, parts[2], dims))


@jax.custom_vjp
def _times_const(x, m):
    return _xm(x, m, NN)


_times_const.defvjp(lambda x, m: (_xm(x, m, NN), m),
                    lambda m, g: (_xm(g, m, NT), jnp.zeros_like(m)))


@jax.custom_vjp
def _const_times(m, x):
    return _mx(m, x, NN)


_const_times.defvjp(lambda m, x: (_mx(m, x, NN), m),
                    lambda m, g: (jnp.zeros_like(m), _mx(m, g, TN)))


@jax.custom_vjp
def _lane_mean_cols(x, avg):
    return _mx(avg, x, BNT)


_lane_mean_cols.defvjp(lambda x, avg: (_mx(avg, x, BNT), avg),
                       lambda avg, g: (_xm(g, avg, BTN), jnp.zeros_like(avg)))


def _gdn_gates(ab, alog, dtb, e_g, e_b):
    t = ab.shape[0]
    g = -jnp.exp(alog) * _softplus(ab + dtb)
    beta = _sigmoid(ab)
    ri = lax.broadcasted_iota(jnp.int32, (t, t), 0)
    ci = lax.broadcasted_iota(jnp.int32, (t, t), 1)
    shift = CHUNK.bit_length() - 1
    same = jnp.right_shift(ri, shift) == jnp.right_shift(ci, shift)
    tril = jnp.where(same & (ri >= ci), 1.0, 0.0).astype(F32)
    gc = _const_times(tril, g)
    return _times_const(gc, e_g), _times_const(beta, e_b)


def _shift_down(x, halo, s, i):
    if s == 0:
        return x
    halo = jnp.where(i == 0, 0.0, halo)
    xr = pltpu.roll(x, s, 0)
    hr = pltpu.roll(halo, s, 0)
    row = lax.broadcasted_iota(jnp.int32, (SUBLANES, x.shape[1]), 0)
    top = jnp.where(row < s, hr, xr[:SUBLANES])
    return jnp.concatenate([top, xr[SUBLANES:]], axis=0)


def _shift_up(x, halo, s, i, n):
    if s == 0:
        return x
    t = x.shape[0]
    halo = jnp.where(i == n - 1, 0.0, halo)
    xr = pltpu.roll(x, t - s, 0)
    hr = pltpu.roll(halo, SUBLANES - s, 0)
    row = lax.broadcasted_iota(jnp.int32, (SUBLANES, x.shape[1]), 0)
    bot = jnp.where(row >= SUBLANES - s, hr, xr[t - SUBLANES:])
    return jnp.concatenate([xr[:t - SUBLANES], bot], axis=0)


def _conv(pa, prev, cw_ref, i):
    y = pa * cw_ref[CONV_K - 1:CONV_K, :]
    for j in range(CONV_K - 1):
        y = y + _shift_down(pa, prev, CONV_K - 1 - j, i) * cw_ref[j:j + 1, :]
    return y


def _dot_nt(a, b, precision=None):
    return lax.dot_general(a, b, (((1,), (1,)), ((), ())), precision=precision, preferred_element_type=F32)


def _dot_tn(a, b, precision=None):
    return lax.dot_general(a, b, (((0,), (0,)), ((), ())), precision=precision, preferred_element_type=F32)


def _dot(a, b, precision=None):
    return jnp.dot(a, b, precision=precision, preferred_element_type=F32)


def _bf(x):
    return x.astype(BF16)


def _gdn_chunk(q, k, v, gc, bb):
    nb, c, _ = q.shape
    ri = lax.broadcasted_iota(jnp.int32, (nb, c, c), 1)
    ci = lax.broadcasted_iota(jnp.int32, (nb, c, c), 2)
    incl = ri >= ci
    strict = ri > ci
    g_row = gc[:, :, :c]
    g_col = _lane_mean_cols(gc, jnp.full((nb, c, LANES), 1.0 / LANES, F32))
    decay = jnp.where(incl, jnp.exp(jnp.where(incl, g_row - g_col, 0.0)), 0.0)
    kb = k * bb
    lmat = jnp.where(strict, _dg(_bf(kb), _bf(k), BNT) * decay, 0.0)
    eye = jnp.where(ri == ci, 1.0, 0.0).astype(F32)
    pw = -lmat
    inv = eye + pw
    for _ in range(5):
        pw = _mm3(pw, pw)
        inv = inv + _mm3(inv, pw)
    egc = jnp.exp(gc)
    u = _mm3(inv, v * bb)
    w = _mm3(inv, kb * egc)
    aqk = _dg(_bf(q), _bf(k), BNT) * decay
    last = lax.broadcasted_iota(jnp.int32, (nb, c, LANES), 1) == c - 1
    tot = jnp.sum(jnp.where(last, gc, 0.0), axis=1, keepdims=True)
    k_tail = k * jnp.exp(tot - gc)
    tail = jnp.broadcast_to(jnp.exp(tot), (nb, SUBLANES, LANES))
    return u, w, aqk, q * egc, k_tail, tail


def _gdn_intra(qn, kn, vv, g_b, beta_b):
    t_rows = qn.shape[0]
    nc = t_rows // CHUNK
    cb = min(GDN_CB, nc)
    rows = cb * CHUNK
    col = pl.BlockSpec((rows, HEAD_DIM), lambda h, b: (b, h))

    def body(q_ref, k_ref, v_ref, g_ref, b_ref, u_ref, w_ref, a_ref, qd_ref, kt_ref, tl_ref):
        def group(gi, carry):
            r = pl.ds(pl.multiple_of(gi * (grp * CHUNK), grp * CHUNK), grp * CHUNK)
            ins = [ref[r, :].reshape(grp, CHUNK, HEAD_DIM) for ref in (q_ref, k_ref, v_ref, g_ref, b_ref)]
            u, w, aqk, qd, kt, tl = _gdn_chunk(*ins)
            for ref, val in ((u_ref, u), (w_ref, w), (qd_ref, qd), (kt_ref, kt)):
                ref[r, :] = val.reshape(grp * CHUNK, HEAD_DIM)
            a_ref[0, r, :] = aqk.reshape(grp * CHUNK, CHUNK)
            tl_ref[0, pl.ds(gi * grp, grp)] = tl
            return carry

        grp = min(GDN_GROUP, cb)
        lax.fori_loop(0, cb // grp, group, 0)

    full = jax.ShapeDtypeStruct((t_rows, HW), F32)
    return pl.pallas_call(
        body, name="gdn_intra_fwd",
        out_shape=[full, full, jax.ShapeDtypeStruct((HEADS, t_rows, CHUNK), F32), full, full,
                   jax.ShapeDtypeStruct((HEADS, nc, SUBLANES, LANES), F32)],
        grid=(HEADS, nc // cb),
        in_specs=[col] * 5,
        out_specs=[col, col, pl.BlockSpec((1, rows, CHUNK), lambda h, b: (h, b, 0)), col, col,
                   pl.BlockSpec((1, cb, SUBLANES, LANES), lambda h, b: (h, b, 0, 0))],
        compiler_params=_params(("parallel", "parallel")),
    )(qn, kn, vv, g_b, beta_b)


def _gdn_intra_bwd(qn, kn, vv, g_b, beta_b, du, dw, da, dqd, dkt, dtl):
    t_rows = qn.shape[0]
    nc = t_rows // CHUNK
    cb = min(GDN_CB, nc)
    rows = cb * CHUNK
    col = pl.BlockSpec((rows, HEAD_DIM), lambda h, b: (b, h))
    a_spec = pl.BlockSpec((1, rows, CHUNK), lambda h, b: (h, b, 0))
    tl_spec = pl.BlockSpec((1, cb, SUBLANES, LANES), lambda h, b: (h, b, 0, 0))

    def body(q_ref, k_ref, v_ref, g_ref, b_ref, du_ref, dw_ref, da_ref, dqd_ref, dkt_ref, dtl_ref,
             dq_ref, dk_ref, dv_ref, dg_ref, db_ref):
        def group(gi, carry):
            r = pl.ds(pl.multiple_of(gi * (grp * CHUNK), grp * CHUNK), grp * CHUNK)
            wide = (grp, CHUNK, HEAD_DIM)
            ins = [ref[r, :].reshape(wide) for ref in (q_ref, k_ref, v_ref, g_ref, b_ref)]
            cts = (du_ref[r, :].reshape(wide), dw_ref[r, :].reshape(wide),
                   da_ref[0, r, :].reshape(grp, CHUNK, CHUNK), dqd_ref[r, :].reshape(wide),
                   dkt_ref[r, :].reshape(wide), dtl_ref[0, pl.ds(gi * grp, grp)])
            grads = jax.vjp(_gdn_chunk, *ins)[1](cts)
            for ref, val in zip((dq_ref, dk_ref, dv_ref, dg_ref, db_ref), grads):
                ref[r, :] = val.reshape(grp * CHUNK, HEAD_DIM)
            return carry

        grp = min(GDN_GROUP, cb)
        lax.fori_loop(0, cb // grp, group, 0)

    full = jax.ShapeDtypeStruct((t_rows, HW), F32)
    return pl.pallas_call(
        body, name="gdn_intra_bwd",
        out_shape=[full] * 5,
        grid=(HEADS, nc // cb),
        in_specs=[col] * 7 + [a_spec, col, col, tl_spec],
        out_specs=[col] * 5,
        compiler_params=_params(("parallel", "parallel")),
    )(qn, kn, vv, g_b, beta_b, du, dw, da, dqd, dkt, dtl)


def _head_cols(h):
    return slice(h * HEAD_DIM, (h + 1) * HEAD_DIM)


def _gdn_scan(u, w, aqk, qd, kt, tl):
    t_rows = u.shape[0]
    nc = t_rows // CHUNK
    cb = min(GDN_CB, nc)
    rows = cb * CHUNK
    wide = pl.BlockSpec((rows, HW), lambda b: (b, 0))

    def body(u_ref, w_ref, a_ref, qd_ref, kt_ref, tl_ref, o_ref, s_out_ref, s_ref):
        @pl.when(pl.program_id(0) == 0)
        def _():
            s_ref[...] = jnp.zeros_like(s_ref)

        def chunk(ci, carry):
            r = pl.ds(pl.multiple_of(ci * CHUNK, CHUNK), CHUNK)
            for h in range(HEADS):
                hc = _head_cols(h)
                s = s_ref[h]
                s_out_ref[ci, h] = s
                sb = _bf(s)
                vn = u_ref[r, hc] - _dot(_bf(w_ref[r, hc]), sb)
                vnb = _bf(vn)
                o_ref[r, hc] = _dot(_bf(qd_ref[r, hc]), sb) + _dot(_bf(a_ref[h, r, :]), vnb)
                s_ref[h] = s * tl_ref[h, ci, 0:1, :] + _dot_tn(_bf(kt_ref[r, hc]), vnb)
            return carry

        lax.fori_loop(0, cb, chunk, 0)

    return pl.pallas_call(
        body, name="gdn_scan_fwd",
        out_shape=[jax.ShapeDtypeStruct((t_rows, HW), F32),
                   jax.ShapeDtypeStruct((nc, HEADS, HEAD_DIM, HEAD_DIM), F32)],
        grid=(nc // cb,),
        in_specs=[wide, wide, pl.BlockSpec((HEADS, rows, CHUNK), lambda b: (0, b, 0)), wide, wide,
                  pl.BlockSpec((HEADS, cb, SUBLANES, LANES), lambda b: (0, b, 0, 0))],
        out_specs=[wide, pl.BlockSpec((cb, HEADS, HEAD_DIM, HEAD_DIM), lambda b: (b, 0, 0, 0))],
        scratch_shapes=[pltpu.VMEM((HEADS, HEAD_DIM, HEAD_DIM), F32)],
        compiler_params=_params(("arbitrary",)),
    )(u, w, aqk, qd, kt, tl)


def _gdn_scan_bwd(do, u, w, aqk, qd, kt, tl, states):
    t_rows = u.shape[0]
    nc = t_rows // CHUNK
    cb = min(GDN_CB, nc)
    rows = cb * CHUNK
    nb = nc // cb
    wide = pl.BlockSpec((rows, HW), lambda b: (nb - 1 - b, 0))
    a_spec = pl.BlockSpec((HEADS, rows, CHUNK), lambda b: (0, nb - 1 - b, 0))
    tl_spec = pl.BlockSpec((HEADS, cb, SUBLANES, LANES), lambda b: (0, nb - 1 - b, 0, 0))

    def body(do_ref, u_ref, w_ref, a_ref, qd_ref, kt_ref, tl_ref, s_in_ref,
             du_ref, dw_ref, da_ref, dqd_ref, dkt_ref, dtl_ref, ds_ref):
        @pl.when(pl.program_id(0) == 0)
        def _():
            ds_ref[...] = jnp.zeros_like(ds_ref)

        row0 = lax.broadcasted_iota(jnp.int32, (SUBLANES, LANES), 0) == 0

        def chunk(step, carry):
            ci = cb - 1 - step
            r = pl.ds(pl.multiple_of(ci * CHUNK, CHUNK), CHUNK)
            for h in range(HEADS):
                hc = _head_cols(h)
                s = s_in_ref[ci, h]
                ds_next = ds_ref[h]
                sb, dsb = _bf(s), _bf(ds_next)
                wb, ab, ktb, qdb = _bf(w_ref[r, hc]), _bf(a_ref[h, r, :]), _bf(kt_ref[r, hc]), _bf(qd_ref[r, hc])
                dob = _bf(do_ref[r, hc])
                vn = u_ref[r, hc] - _dot(wb, sb)
                vnb = _bf(vn)
                dvn = _dot_tn(ab, dob) + _dot(ktb, dsb)
                dvnb = _bf(dvn)
                du_ref[r, hc] = dvn
                dw_ref[r, hc] = -_dot_nt(dvnb, sb)
                da_ref[h, r, :] = _dot_nt(dob, vnb)
                dqd_ref[r, hc] = _dot_nt(dob, sb)
                dkt_ref[r, hc] = _dot_nt(vnb, dsb)
                dtl_ref[h, ci] = jnp.where(row0, _colsum(s * ds_next), 0.0)
                ds_ref[h] = _dot_tn(qdb, dob) + ds_next * tl_ref[h, ci, 0:1, :] - _dot_tn(wb, dvnb)
            return carry

        lax.fori_loop(0, cb, chunk, 0)

    full = jax.ShapeDtypeStruct((t_rows, HW), F32)
    return pl.pallas_call(
        body, name="gdn_scan_bwd",
        out_shape=[full, full, jax.ShapeDtypeStruct((HEADS, t_rows, CHUNK), F32), full, full,
                   jax.ShapeDtypeStruct((HEADS, nc, SUBLANES, LANES), F32)],
        grid=(nb,),
        in_specs=[wide, wide, wide, a_spec, wide, wide, tl_spec,
                  pl.BlockSpec((cb, HEADS, HEAD_DIM, HEAD_DIM), lambda b: (nb - 1 - b, 0, 0, 0))],
        out_specs=[wide, wide, a_spec, wide, wide, tl_spec],
        scratch_shapes=[pltpu.VMEM((HEADS, HEAD_DIM, HEAD_DIM), F32)],
        compiler_params=_params(("arbitrary",)),
    )(do, u, w, aqk, qd, kt, tl, states)


def _att_rel_index():
    qi = lax.broadcasted_iota(jnp.int32, (ATT_QB, ATT_KW), 0)
    kj = lax.broadcasted_iota(jnp.int32, (ATT_QB, ATT_KW), 1)
    return jnp.clip(qi - kj + ATT_PAD, -(CHUNK - 1), MAX_REL) + (CHUNK - 1)


def _att_in_band():
    qi = lax.broadcasted_iota(jnp.int32, (ATT_QB, ATT_KW), 0)
    kj = lax.broadcasted_iota(jnp.int32, (ATT_QB, ATT_KW), 1)
    shift = CHUNK.bit_length() - 1
    qc = jnp.right_shift(qi, shift)
    kc = jnp.right_shift(kj, shift) - LEFT_CHUNKS
    return (kc <= qc) & (kc >= qc - LEFT_CHUNKS)


def _att_valid(b):
    kj = lax.broadcasted_iota(jnp.int32, (ATT_QB, ATT_KW), 1)
    return kj + b * ATT_QB >= ATT_PAD


def _att_block(q_raw, k_raw, v, qw, kw, bias, valid):
    q = _rms(q_raw, qw)
    k = _rms(k_raw, kw)
    s = _dot_nt(_bf(q), _bf(k)) * (HEAD_DIM ** -0.5) + bias
    s = jnp.where(valid, s, NEG_INF)
    p = jnp.exp(s - jnp.max(s, axis=-1, keepdims=True))
    p = p * (1.0 / jnp.sum(p, axis=-1, keepdims=True))
    return _dot(_bf(p), _bf(v))


def _att_specs():
    q_spec = pl.BlockSpec((ATT_QB, HEAD_DIM), lambda h, b: (b, h))
    k_specs = [pl.BlockSpec((ATT_QB, HEAD_DIM), lambda h, b, j=j: (b + j, HEADS + h)) for j in range(3)]
    v_specs = [pl.BlockSpec((ATT_QB, HEAD_DIM), lambda h, b, j=j: (b + j, 2 * HEADS + h)) for j in range(3)]
    w_spec = pl.BlockSpec((1, HEAD_DIM), lambda h, b: (0, 0))
    smem = pl.BlockSpec(memory_space=pltpu.SMEM)
    return q_spec, k_specs, v_specs, w_spec, smem


def _att_fill_bias(bias_ref, rel_ref, h):
    idx = _att_rel_index()

    def fill(r, acc):
        return jnp.where(idx == r, rel_ref[h, r], acc)

    table = lax.fori_loop(0, N_REL, fill, jnp.zeros((ATT_QB, ATT_KW), F32))
    bias_ref[...] = jnp.where(_att_in_band(), table, NEG_INF)


def _attention(pb, pbp, qw, kw, rel):
    t_rows = pb.shape[0]
    q_spec, k_specs, v_specs, w_spec, smem = _att_specs()

    def body(q_ref, k0, k1, k2, v0, v1, v2, qw_ref, kw_ref, rel_ref, o_ref, bias_ref):
        h, b = pl.program_id(0), pl.program_id(1)

        @pl.when(b == 0)
        def _():
            _att_fill_bias(bias_ref, rel_ref, h)

        kwin = jnp.concatenate([k0[...], k1[...], k2[...]], axis=0)
        vwin = jnp.concatenate([v0[...], v1[...], v2[...]], axis=0)
        o = _att_block(q_ref[...], kwin, vwin, qw_ref[...], kw_ref[...], bias_ref[...], _att_valid(b))
        o_ref[...] = o.astype(o_ref.dtype)

    return pl.pallas_call(
        body, name="band_attention_fwd",
        out_shape=jax.ShapeDtypeStruct((t_rows, HW), BF16),
        grid=(HEADS, t_rows // ATT_QB),
        in_specs=[q_spec] + k_specs + v_specs + [w_spec, w_spec, smem],
        out_specs=pl.BlockSpec((ATT_QB, HEAD_DIM), lambda h, b: (b, h)),
        scratch_shapes=[pltpu.VMEM((ATT_QB, ATT_KW), F32)],
        compiler_params=_params(("arbitrary", "arbitrary")),
    )(pb, pbp, pbp, pbp, pbp, pbp, pbp, qw, kw, rel)


def _attention_bwd(pb, pbp, qw, kw, rel, dyb):
    t_rows = pb.shape[0]
    nb = t_rows // ATT_QB
    q_spec, k_specs, v_specs, w_spec, smem = _att_specs()
    pad_rows = t_rows + ATT_PAD
    acc_spec = pl.BlockSpec((pad_rows, HEAD_DIM), lambda h, b: (0, h))

    def body(q_ref, k0, k1, k2, v0, v1, v2, qw_ref, kw_ref, rel_ref, do_ref,
             dq_ref, dk_ref, dv_ref, dqw_ref, dkw_ref, drel_ref, bias_ref, dbias_ref):
        h, b = pl.program_id(0), pl.program_id(1)

        @pl.when(b == 0)
        def _():
            _att_fill_bias(bias_ref, rel_ref, h)
            dbias_ref[...] = jnp.zeros_like(dbias_ref)
            dk_ref[...] = jnp.zeros_like(dk_ref)
            dv_ref[...] = jnp.zeros_like(dv_ref)

        @pl.when((b == 0) & (h == 0))
        def _():
            dqw_ref[...] = jnp.zeros_like(dqw_ref)
            dkw_ref[...] = jnp.zeros_like(dkw_ref)

        kwin = jnp.concatenate([k0[...], k1[...], k2[...]], axis=0)
        vwin = jnp.concatenate([v0[...], v1[...], v2[...]], axis=0)
        valid = _att_valid(b)
        _, vjp = jax.vjp(lambda q, k, v, a, c, bias: _att_block(q, k, v, a, c, bias, valid),
                         q_ref[...], kwin, vwin, qw_ref[...], kw_ref[...], bias_ref[...])
        dq, dk, dv, dqw, dkw, dbias = vjp(do_ref[...])
        dq_ref[...] = dq.astype(dq_ref.dtype)
        win = pl.ds(pl.multiple_of(b * ATT_QB, ATT_QB), ATT_KW)
        dk_ref[win, :] += dk
        dv_ref[win, :] += dv
        dqw_ref[...] += dqw
        dkw_ref[...] += dkw
        dbias_ref[...] += dbias

        @pl.when(b == nb - 1)
        def _():
            idx = _att_rel_index()
            tot = dbias_ref[...]

            def reduce(r, carry):
                drel_ref[h, r] = jnp.sum(jnp.where(idx == r, tot, 0.0))
                return carry

            lax.fori_loop(0, N_REL, reduce, 0)

    return pl.pallas_call(
        body, name="band_attention_bwd",
        out_shape=[jax.ShapeDtypeStruct((t_rows, HW), BF16),
                   jax.ShapeDtypeStruct((pad_rows, HW), F32), jax.ShapeDtypeStruct((pad_rows, HW), F32),
                   jax.ShapeDtypeStruct((1, HEAD_DIM), F32), jax.ShapeDtypeStruct((1, HEAD_DIM), F32),
                   jax.ShapeDtypeStruct((HEADS, N_REL), F32)],
        grid=(HEADS, nb),
        in_specs=[q_spec] + k_specs + v_specs + [w_spec, w_spec, smem, q_spec],
        out_specs=[q_spec, acc_spec, acc_spec, w_spec, w_spec, smem],
        scratch_shapes=[pltpu.VMEM((ATT_QB, ATT_KW), F32), pltpu.VMEM((ATT_QB, ATT_KW), F32)],
        compiler_params=_params(("arbitrary", "arbitrary")),
    )(pb, pbp, pbp, pbp, pbp, pbp, pbp, qw, kw, rel, dyb)


def _me():
    return lax.axis_index("x"), lax.axis_index("y"), lax.axis_index("c")


def _index(x, y, c):
    return 4 * x + 2 * y + c


HBM_SPEC = pl.BlockSpec(memory_space=pl.ANY)


def _block(ref, kind, d, r, c):
    if kind == "rows":
        return ref.at[pl.ds(d * r, r), :]
    if kind == "win":
        return ref.at[:, pl.ds(d * WIN_STEP, c)]
    return ref.at[:, pl.ds(d * c, c)]


def _all_gather(shards, kinds, n_gather):
    n = len(shards)

    def body(*refs):
        x_refs, out_refs = refs[:n], refs[n:2 * n]
        send_sems, recv_sems, local_sems = refs[2 * n:]
        x, y, c = _me()
        me, sibling = (x, y, c), (x, y, 1 - c)
        chips = [(1 - x, y), (x, 1 - y), (1 - x, 1 - y)]

        def copy(i, k, blk, to, src=None):
            r_, c_ = shards[i].shape
            dst = _block(out_refs[i], kinds[i], _index(*blk), r_, c_)
            return pltpu.make_async_remote_copy(
                src_ref=dst if src is None else src, dst_ref=dst,
                send_sem=send_sems.at[i, k], recv_sem=recv_sems.at[i, k], device_id=to, device_id_type=MESH)

        sends, local = [], []
        for i in range(n):
            r_, c_ = shards[i].shape
            mine = pltpu.make_async_copy(x_refs[i], _block(out_refs[i], kinds[i], _index(*me), r_, c_),
                                         local_sems.at[i])
            mine.start()
            local.append(mine)
            if i >= n_gather:
                continue
            first = [copy(i, 0, me, sibling, src=x_refs[i])]
            first += [copy(i, 1 + j, me, (*chip, c), src=x_refs[i]) for j, chip in enumerate(chips)]
            for cp in first:
                cp.start()
            sends += first
        for i in range(n_gather):
            for j, chip in enumerate(chips):
                copy(i, 1 + j, (*chip, c), me).wait_recv()
                passed = copy(i, 4 + j, (*chip, c), sibling)
                passed.start()
                sends.append(passed)
        for i in range(n_gather):
            copy(i, 0, sibling, me).wait_recv()
            for j, chip in enumerate(chips):
                copy(i, 4 + j, (*chip, 1 - c), me).wait_recv()
        for cp in sends:
            cp.wait_send()
        for cp in local:
            cp.wait()

    def full_shape(s, kind):
        r_, c_ = s.shape
        return (N_DEV * r_, c_) if kind == "rows" else (r_, N_DEV * c_)

    return pl.pallas_call(
        body, name="weights_all_gather",
        out_shape=[jax.ShapeDtypeStruct(full_shape(s, k), s.dtype) for s, k in zip(shards, kinds)],
        in_specs=[HBM_SPEC] * n, out_specs=[HBM_SPEC] * n,
        scratch_shapes=[pltpu.SemaphoreType.DMA((n_gather, 7)), pltpu.SemaphoreType.DMA((n_gather, 7)),
                        pltpu.SemaphoreType.DMA((n,))],
        compiler_params=pltpu.CompilerParams(has_side_effects=True),
    )(*shards)


SEM_SPEC = pl.BlockSpec(memory_space=pltpu.SEMAPHORE)
HBM_ONLY = pl.BlockSpec(memory_space=pltpu.HBM)
DATAFLOW = pltpu.SideEffectType.DATAFLOW_SIDE_EFFECTING


def _peers():
    x, y, c = _me()
    return [(x ^ (k >> 2), y ^ ((k >> 1) & 1), c ^ (k & 1)) for k in range(1, N_DEV)]


def _gather_copies(shapes, kinds):
    def make(src_refs, land_refs, send_sems, recv_sems):
        mine = _index(*_me())
        return [pltpu.make_async_remote_copy(
            src_ref=src_refs[i], dst_ref=_block(land_refs[i], kind, mine, r, c),
            send_sem=send_sems.at[7 * i + k], recv_sem=recv_sems.at[7 * i + k], device_id=peer, device_id_type=MESH)
            for i, ((r, c), kind) in enumerate(zip(shapes, kinds)) for k, peer in enumerate(_peers())]

    return make


def _exchange_copies(shapes, kinds):
    def make(src_refs, land_refs, send_sems, recv_sems):
        mine = _index(*_me())
        return [pltpu.make_async_remote_copy(
            src_ref=_block(src_refs[i], kind, _index(*peer), r, c), dst_ref=land_refs[i].at[mine],
            send_sem=send_sems.at[7 * i + k], recv_sem=recv_sems.at[7 * i + k], device_id=peer, device_id_type=MESH)
            for i, ((r, c), kind) in enumerate(zip(shapes, kinds)) for k, peer in enumerate(_peers())]

    return make


def _split_start(srcs, lands, make, name):
    n = len(srcs)

    def body(*refs):
        send_sems, recv_sems = refs[2 * n], refs[2 * n + 1]
        for cp in make(refs[:n], refs[n:2 * n], send_sems, recv_sems):
            cp.start()
        refs[-1][...] = jnp.zeros_like(refs[-1])

    arrays = list(srcs) + list(lands)
    out = pl.pallas_call(
        body, name=name,
        out_shape=(pltpu.SemaphoreType.DMA((7 * n,)), pltpu.SemaphoreType.DMA((7 * n,)),
                   *[pltpu.HBM(a.shape, a.dtype) for a in arrays], jax.ShapeDtypeStruct((SUBLANES, LANES), F32)),
        in_specs=[HBM_ONLY] * (2 * n),
        out_specs=(SEM_SPEC, SEM_SPEC, *[HBM_ONLY] * (2 * n), pl.BlockSpec(memory_space=pltpu.VMEM)),
        input_output_aliases={i: 2 + i for i in range(2 * n)},
        compiler_params=pltpu.CompilerParams(has_side_effects=DATAFLOW),
    )(*[pltpu.with_memory_space_constraint(a, pltpu.HBM) for a in arrays])
    return out[0], out[1], list(out[2:2 + n]), list(out[2 + n:2 + 2 * n]), out[-1]


def _split_wait(send_sems, recv_sems, srcs, lands, after, make, name):
    n = len(srcs)

    def body(*refs):
        for cp in make(refs[:n], refs[n:2 * n], refs[2 * n], refs[2 * n + 1]):
            cp.wait_send()
            cp.wait_recv()

    arrays = list(srcs) + list(lands)
    out = pl.pallas_call(
        body, name=name,
        out_shape=tuple(pltpu.HBM(a.shape, a.dtype) for a in arrays),
        in_specs=[HBM_ONLY] * (2 * n) + [SEM_SPEC, SEM_SPEC, pl.BlockSpec(memory_space=pl.ANY)],
        out_specs=tuple([HBM_ONLY] * (2 * n)),
        input_output_aliases={i: i for i in range(2 * n)},
        compiler_params=pltpu.CompilerParams(has_side_effects=DATAFLOW),
    )(*arrays, send_sems, recv_sems, after)
    return list(out[:n]), list(out[n:])


def _all_reduce_small(vals, name):
    rows, width = vals.shape

    def body(x_ref, out_ref, buf_ref, send_sems, recv_sems):
        x, y, c = _me()
        mine = _index(x, y, c)
        buf_ref[mine] = x_ref[...]
        copies = []
        for k in range(1, N_DEV):
            px, py, pc = x ^ (k >> 2), y ^ ((k >> 1) & 1), c ^ (k & 1)
            copies.append(pltpu.make_async_remote_copy(
                src_ref=x_ref, dst_ref=buf_ref.at[mine],
                send_sem=send_sems.at[k - 1], recv_sem=recv_sems.at[k - 1],
                device_id=(px, py, pc), device_id_type=MESH))
        for cp in copies:
            cp.start()
        for cp in copies:
            cp.wait()
        acc = buf_ref[0]
        for j in range(1, N_DEV):
            acc = acc + buf_ref[j]
        out_ref[...] = acc

    vmem = pl.BlockSpec(memory_space=pltpu.VMEM)
    return pl.pallas_call(
        body, name=name,
        out_shape=jax.ShapeDtypeStruct(vals.shape, F32),
        in_specs=[vmem], out_specs=vmem,
        scratch_shapes=[pltpu.VMEM((N_DEV, rows, width), F32),
                        pltpu.SemaphoreType.DMA((7,)), pltpu.SemaphoreType.DMA((7,))],
        compiler_params=pltpu.CompilerParams(has_side_effects=True),
    )(vals)


def _adamw_math(w, g, m, v):
    m = ADAM_B1 * m + (1.0 - ADAM_B1) * g
    v = ADAM_B2 * v + (1.0 - ADAM_B2) * (g * g)
    m_hat = m / (1.0 - ADAM_B1 ** ADAM_STEP)
    v_hat = v / (1.0 - ADAM_B2 ** ADAM_STEP)
    delta = -ADAM_LR * (m_hat / (jnp.sqrt(v_hat) + ADAM_EPS) + ADAM_WD * w)
    return delta, m, v


ROW_TILE_ELEMS = 384 * 1024


def _row_tile(rows, width):
    best = SUBLANES
    for t in range(SUBLANES, rows + 1, SUBLANES):
        if rows % t == 0 and t * width <= ROW_TILE_ELEMS:
            best = t
    return best


def _sum_received(r_ref, own, me):
    g = None
    for j in range(N_DEV):
        term = jnp.where(me == j, own, r_ref[j].astype(F32))
        g = term if g is None else g + term
    return g


def _my_index_operand():
    return _index(*_me()).astype(jnp.int32).reshape(1)


def _adamw_recv(recv, grad, kind, w, m, v, name):
    _, rows, width = recv.shape
    tile = _row_tile(rows, width)
    nt = rows // tile

    def body(me_ref, r_ref, own_ref, w_ref, m_ref, v_ref, g_out, d_out, m_out, v_out):
        g = _sum_received(r_ref, own_ref[...].astype(F32), me_ref[0])
        d, mn, vn = _adamw_math(w_ref[...], g, m_ref[...], v_ref[...])
        g_out[...] = g
        d_out[...] = d
        m_out[...] = mn
        v_out[...] = vn

    if kind == "rows":
        own_spec = pl.BlockSpec((tile, width), lambda i, me: (me[0] * nt + i, 0))
    else:
        own_spec = pl.BlockSpec((tile, width), lambda i, me: (i, me[0]))
    spec = pl.BlockSpec((tile, width), lambda i, me: (i, 0))
    shape = jax.ShapeDtypeStruct((rows, width), F32)
    return pl.pallas_call(
        body, name=name, out_shape=[shape] * 4,
        grid_spec=pltpu.PrefetchScalarGridSpec(
            num_scalar_prefetch=1, grid=(nt,),
            in_specs=[pl.BlockSpec((N_DEV, tile, width), lambda i, me: (0, i, 0)), own_spec, spec, spec, spec],
            out_specs=[spec] * 4),
        compiler_params=_params(("parallel",)),
    )(_my_index_operand(), recv, grad, w, m, v)


WIN_STEP = 1408
WIN_W = 1536
IN_SHARD = IN_COLS // N_DEV
IN_PADDED = WIN_STEP * (N_DEV - 1) + WIN_W


def _roll_w_in(shard_padded):
    rows = shard_padded.shape[0]
    tile = _row_tile(rows, WIN_W)

    def body(x_ref, main_ref, edge_ref):
        win = pltpu.roll(x_ref[...], 2 * _index(*_me()), 1).astype(BF16)
        main_ref[...] = win[:, :WIN_STEP]
        edge_ref[...] = win[:, WIN_STEP:]

    return pl.pallas_call(
        body, name="w_in_window",
        out_shape=[jax.ShapeDtypeStruct((rows, WIN_STEP), BF16), jax.ShapeDtypeStruct((rows, WIN_W - WIN_STEP), BF16)],
        grid=(rows // tile,),
        in_specs=[pl.BlockSpec((tile, WIN_W), lambda i: (i, 0))],
        out_specs=[pl.BlockSpec((tile, WIN_STEP), lambda i: (i, 0)),
                   pl.BlockSpec((tile, WIN_W - WIN_STEP), lambda i: (i, 0))],
        compiler_params=_params(("parallel",)),
    )(shard_padded)


def _sum_w_in_windows(recv, grad):
    _, rows, width = recv.shape
    tile = _row_tile(rows, width)

    def body(me_ref, r_ref, g_ref, g_out, own_ref, sem):
        me = me_ref[0]
        rows_i = pl.ds(pl.multiple_of(pl.program_id(0) * tile, tile), tile)
        own = pltpu.make_async_copy(g_ref.at[rows_i, pl.ds(pl.multiple_of(me * WIN_STEP, LANES), width)], own_ref, sem)
        own.start()
        own.wait()
        g_out[...] = pltpu.roll(_sum_received(r_ref, own_ref[...].astype(F32), me), width - 2 * me, 1)

    return pl.pallas_call(
        body, name="w_in_grad_sum", out_shape=jax.ShapeDtypeStruct((rows, width), F32),
        grid_spec=pltpu.PrefetchScalarGridSpec(
            num_scalar_prefetch=1, grid=(rows // tile,),
            in_specs=[pl.BlockSpec((N_DEV, tile, width), lambda i, me: (0, i, 0)), HBM_SPEC],
            out_specs=pl.BlockSpec((tile, width), lambda i, me: (i, 0)),
            scratch_shapes=[pltpu.VMEM((tile, width), BF16), pltpu.SemaphoreType.DMA]),
        compiler_params=_params(("arbitrary",)),
    )(_my_index_operand(), recv, grad)


def _adamw_small(w, g, m, v, name):
    def fn(i, n, w_, g_, m_, v_):
        return _adamw_math(w_, g_, m_, v_)

    r, c = w.shape
    return _rows(fn, [(w, "t"), (g, "t"), (m, "t"), (v, "t")], [], [(c, F32)] * 3, [], _row_tile(r, c), name)


def _norm_fwd(x, w, name):
    return _rows(lambda i, n, x_, w_: (_rms(x_, w_[...]),), [(x, "t")], [w], [(D_MODEL, BF16)], [], 512, name)[0]


def _residual_norm_fwd(x, y, scale, w, name):
    def fn(i, n, x_, y_, w_):
        xn = x_ + scale * y_
        return xn, _rms(xn, w_[...])

    return _rows(fn, [(x, "t"), (y, "t")], [w], [(D_MODEL, F32), (D_MODEL, BF16)], [], 512, name)


def _residual_norm_bwd(x, w, dhs, dres, scale, name):
    nh = len(dhs)

    def fn(i, n, x_, dres_, *rest):
        dh = rest[0]
        for extra in rest[1:nh]:
            dh = dh + extra
        _, vjp = jax.vjp(_rms, x_, rest[nh][...])
        dx, dw = vjp(dh)
        dx = dx + dres_
        return dx, scale * dx, dw

    return _rows(fn, [(x, "t"), (dres, "t")] + [(d, "t") for d in dhs], [w],
                 [(D_MODEL, F32), (D_MODEL, BF16)], [(1, D_MODEL)], 256, name)


def _ffn_fwd(h, w_gu, w_down, tag):
    gu = _matmul(h, w_gu, "nn", F32, tag + "_gu")
    act = _rows(lambda i, n, gu_: (_swiglu(gu_),), [(gu, "t")], [], [(D_FF, BF16)], [], 128, tag + "_swiglu")[0]
    y = _matmul(act, w_down, "nn", F32, tag + "_down")
    return gu, act, y


def _ffn_bwd(h, gu, act, dy, w_gu, w_down, tag, comm, more=None):
    dact = _matmul(dy, w_down, "nt", F32, tag + "_dact")

    def fn(i, n, gu_, dact_):
        _, vjp = jax.vjp(_swiglu, gu_)
        return vjp(dact_)

    dgu = _rows(fn, [(gu, "t"), (dact, "t")], [], [(2 * D_FF, BF16)], [], 128, tag + "_swiglu_bwd")[0]
    sent = comm.send(tag + "_gu", {tag + "_w_gu": _matmul(h, dgu, "tn", BF16, tag + "_d_w_gu")})
    sent = sent + comm.send(tag + "_down", {tag + "_w_down": _matmul(act, dy + sent.astype(BF16), "tn", BF16,
                                                                    tag + "_d_w_down"), **(more or {})})
    dh = _matmul(dgu, w_gu, "nt", F32, tag + "_dh")
    return dh, sent


def _expanders():
    e_g = np.zeros((LANES, HW), np.float32)
    e_b = np.zeros((LANES, HW), np.float32)
    for h in range(HEADS):
        e_g[h, h * HEAD_DIM:(h + 1) * HEAD_DIM] = 1.0
        e_b[HEADS + h, h * HEAD_DIM:(h + 1) * HEAD_DIM] = 1.0
    return jnp.asarray(e_g), jnp.asarray(e_b)


def _pad_lanes(v):
    return jnp.pad(v, ((0, 0), (0, LANES - v.shape[1])))


class _LocalWeights:
    def __init__(self, big):
        self.big, self.sent = big, {}

    def arrive(self, group, after):
        return self.big

    def send(self, group, grads):
        self.sent.update(grads)
        return jnp.zeros((), F32)


def _local_step(x, p, tgt, small, comm):
    big = comm.arrive("ffn1", None)
    e_g, e_b = _expanders()
    alog, dtb = _pad_lanes(small["a_log"]), _pad_lanes(small["dt_bias"])
    conv_w = jnp.pad(small["conv_w"], ((0, SUBLANES - CONV_K), (0, 0)))
    rel = small["rel_bias"]

    h1 = _norm_fwd(x, small["ffn1_norm"], "ffn1_norm")
    gu1, act1, y1 = _ffn_fwd(h1, big["ffn1_w_gu"], big["ffn1_w_down"], "ffn1")
    x1, h2 = _residual_norm_fwd(x, y1, 0.5, small["mix_norm"], "mix_norm")

    big = {**big, **comm.arrive("mixer", h2)}
    w_in = big["w_in"]
    w_qz = w_in[:, :IN_QZ]
    w_ab = jnp.pad(w_in[:, IN_AB0:IN_QKVB0], ((0, 0), (0, LANES - 2 * HEADS)))
    w_qkvb = w_in[:, IN_QKVB0:IN_GG0]
    w_gg = w_in[:, IN_GG0:IN_COLS]
    qz = _matmul(h2, w_qz, "nn", F32, "in_qz")
    ab = _matmul(h2, w_ab, "nn", F32, "in_ab")
    pb = _matmul(h2, w_qkvb, "nn", F32, "in_qkvb")
    gg = _matmul(h2, w_gg, "nn", F32, "in_gates")
    pa, z = qz[:, :3 * HW], qz[:, 3 * HW:]

    def prep(i, n, pa_, prev_, ab_, cw_, alog_, dtb_, eg_, eb_):
        q, k, v = _gdn_post(_conv(pa_, prev_, cw_, i))
        g_b, beta_b = _gdn_gates(ab_, alog_[...], dtb_[...], eg_[...], eb_[...])
        return q, k, v, g_b, beta_b

    qn, kn, vv, g_b, beta_b = _rows(prep, [(pa, "t"), (pa, "p"), (ab, "t")], [conv_w, alog, dtb, e_g, e_b],
                                    [(HW, F32)] * 5, [], 256, "gdn_prep")
    u, w, aqk, qd, kt, tl = _gdn_intra(qn, kn, vv, g_b, beta_b)
    o, states = _gdn_scan(u, w, aqk, qd, kt, tl)
    ya = _rows(lambda i, n, o_, z_, w_: (_gated_norm(o_, z_, w_[...]),), [(o, "t"), (z, "t")], [small["gdn_norm"]],
               [(HW, BF16)], [], 512, "gdn_gated_norm")[0]

    pbp = jnp.pad(pb, ((ATT_PAD, 0), (0, 0)))
    yb = _attention(pb, pbp, small["q_norm"], small["k_norm"], rel)

    ta = _matmul(ya, big["w_branch_a"], "nn", F32, "branch_a")
    tb = _matmul(yb, big["w_branch_b"], "nn", F32, "branch_b")
    mixed = _rows(lambda i, n, gg_, ta_, tb_: (_mix(gg_, ta_, tb_),), [(gg, "t"), (ta, "t"), (tb, "t")], [],
                  [(D_MODEL, BF16)], [], 256, "mix")[0]
    m_out = _matmul(mixed, big["w_out"], "nn", F32, "w_out")
    x2, h3 = _residual_norm_fwd(x1, m_out, 1.0, small["ffn2_norm"], "ffn2_norm")
    big = {**big, **comm.arrive("tail", h3)}
    gu2, act2, y2 = _ffn_fwd(h3, big["ffn2_w_gu"], big["ffn2_w_down"], "ffn2")
    x3, h4 = _residual_norm_fwd(x2, y2, 0.5, small["ple_norm"], "ple_norm")
    gp = _matmul(h4, big["ple_gate"], "nn", F32, "ple_gate")
    pp = _matmul(p, big["ple_proj"], "nn", F32, "ple_proj")

    def head(i, n, x3_, gp_, pp_, tgt_):
        sg = _sigmoid(gp_)
        err = x3_ + sg * pp_ - tgt_
        dx4 = err * (1.0 / D_MODEL)
        sq = _colsum(err * err)
        part = sq[:, :LANES]
        for j in range(1, D_MODEL // LANES):
            part = part + sq[:, j * LANES:(j + 1) * LANES]
        return dx4, dx4 * pp_ * sg * (1.0 - sg), dx4 * sg, (0.5 / D_MODEL) * part

    dx4, dgp, dpp, loss_lanes = _rows(head, [(x3, "t"), (gp, "t"), (pp, "t"), (tgt, "t")], [],
                                      [(D_MODEL, F32), (D_MODEL, BF16), (D_MODEL, BF16)], [(1, LANES)], 256,
                                      "ple_loss_head")
    loss = jnp.sum(loss_lanes)

    gbig, gsmall = {}, {}
    gbig["ple_proj"] = _matmul(p, dpp, "tn", BF16, "d_ple_proj")
    gbig["ple_gate"] = _matmul(h4, dgp, "tn", BF16, "d_ple_gate")
    dh4 = _matmul(dgp, big["ple_gate"], "nt", F32, "ple_gate_dh")
    dx3, dy2, gsmall["ple_norm"] = _residual_norm_bwd(x3, small["ple_norm"], [dh4], dx4, 0.5, "ple_norm_bwd")

    dh3, sent = _ffn_bwd(h3, gu2, act2, dy2, big["ffn2_w_gu"], big["ffn2_w_down"], "ffn2", comm,
                         {n: gbig[n] for n in ("ple_proj", "ple_gate")})
    dx2, dx2b, gsmall["ffn2_norm"] = _residual_norm_bwd(x2, small["ffn2_norm"] + sent, [dh3], dx3, 1.0,
                                                        "ffn2_norm_bwd")

    gbig["w_out"] = _matmul(mixed, dx2b, "tn", BF16, "d_w_out")
    dmixed = _matmul(dx2b, big["w_out"], "nt", F32, "w_out_dx")

    def mix_bwd(i, n, gg_, ta_, tb_, dm_):
        _, vjp = jax.vjp(_mix, gg_, ta_, tb_)
        return vjp(dm_)

    dgg, dta, dtb_ = _rows(mix_bwd, [(gg, "t"), (ta, "t"), (tb, "t"), (dmixed, "t")], [],
                           [(2 * D_MODEL, BF16), (D_MODEL, BF16), (D_MODEL, BF16)], [], 256, "mix_bwd")
    gbig["w_branch_a"] = _matmul(ya, dta, "tn", BF16, "d_branch_a")
    gbig["w_branch_b"] = _matmul(yb, dtb_, "tn", BF16, "d_branch_b")
    dya = _matmul(dta, big["w_branch_a"], "nt", F32, "branch_a_dx")
    dyb = _matmul(dtb_, big["w_branch_b"], "nt", F32, "branch_b_dx")

    dq_b, dk_b, dv_b, gsmall["q_norm"], gsmall["k_norm"], gsmall["rel_bias"] = _attention_bwd(
        pb, pbp, small["q_norm"], small["k_norm"], rel, dyb)
    dpb = jnp.concatenate([dq_b, dk_b[ATT_PAD:].astype(BF16), dv_b[ATT_PAD:].astype(BF16)], axis=1)

    def gated_bwd(i, n, o_, z_, dya_, w_):
        _, vjp = jax.vjp(_gated_norm, o_, z_, w_[...])
        return vjp(dya_)

    do, dz, gsmall["gdn_norm"] = _rows(gated_bwd, [(o, "t"), (z, "t"), (dya, "t")], [small["gdn_norm"]],
                                       [(HW, F32), (HW, BF16)], [(1, HEAD_DIM)], 256, "gdn_gated_norm_bwd")
    du, dw, da, dqd, dkt, dtl = _gdn_scan_bwd(do, u, w, aqk, qd, kt, tl, states)
    dqn, dkn, dvv, dg_b, dbeta_b = _gdn_intra_bwd(qn, kn, vv, g_b, beta_b, du, dw, da, dqd, dkt, dtl)

    def prep_bwd(i, n, pa_, prev_, ab_, dq_, dk_, dv_, dg_, db_, cw_, alog_, dtb_, eg_, eb_):
        _, vjp = jax.vjp(_gdn_post, _conv(pa_, prev_, cw_, i))
        (dy,) = vjp((dq_, dk_, dv_))
        e_g_, e_b_ = eg_[...], eb_[...]
        _, vjp_g = jax.vjp(lambda a, b, c: _gdn_gates(a, b, c, e_g_, e_b_), ab_, alog_[...], dtb_[...])
        dab, dalog, ddtb = vjp_g((dg_, db_))
        return dy, dab, dalog, ddtb

    dy_conv, dab, dalog, ddtb = _rows(
        prep_bwd, [(pa, "t"), (pa, "p"), (ab, "t"), (dqn, "t"), (dkn, "t"), (dvv, "t"), (dg_b, "t"), (dbeta_b, "t")],
        [conv_w, alog, dtb, e_g, e_b], [(3 * HW, F32), (LANES, BF16)], [(1, LANES), (1, LANES)], 256,
        "gdn_prep_bwd")
    gsmall["a_log"] = dalog[:, :HEADS]
    gsmall["dt_bias"] = ddtb[:, :HEADS]

    def conv_bwd(i, n, dy_, nxt_, pa_, prev_, cw_):
        dpa = dy_ * cw_[CONV_K - 1:CONV_K, :]
        row = lax.broadcasted_iota(jnp.int32, (SUBLANES, dy_.shape[1]), 0)
        dcw = jnp.where(row == CONV_K - 1, _colsum(dy_ * pa_), 0.0)
        for j in range(CONV_K - 1):
            s = CONV_K - 1 - j
            dpa = dpa + _shift_up(dy_, nxt_, s, i, n) * cw_[j:j + 1, :]
            dcw = dcw + jnp.where(row == j, _colsum(dy_ * _shift_down(pa_, prev_, s, i)), 0.0)
        return dpa, dcw

    dpa, dcw = _rows(conv_bwd, [(dy_conv, "t"), (dy_conv, "n"), (pa, "t"), (pa, "p")], [conv_w],
                     [(3 * HW, BF16)], [(SUBLANES, 3 * HW)], 256, "gdn_conv_bwd")
    gsmall["conv_w"] = dcw[:CONV_K]

    dqz = jnp.concatenate([dpa, dz], axis=1)
    d_w_qz = _matmul(h2, dqz, "tn", BF16, "d_in_qz")
    d_w_ab = _matmul(h2, dab, "tn", BF16, "d_in_ab")
    d_w_qkvb = _matmul(h2, dpb, "tn", BF16, "d_in_qkvb")
    d_w_gg = _matmul(h2, dgg, "tn", BF16, "d_in_gates")
    gbig["w_in"] = jnp.concatenate([d_w_qz, d_w_ab[:, :2 * HEADS], d_w_qkvb, d_w_gg,
                                    jnp.zeros((D_MODEL, IN_PADDED - IN_COLS), BF16)], axis=1)
    dh2 = [_matmul(dqz, w_qz, "nt", F32, "in_qz_dh"), _matmul(dab, w_ab, "nt", F32, "in_ab_dh"),
           _matmul(dpb, w_qkvb, "nt", F32, "in_qkvb_dh"), _matmul(dgg, w_gg, "nt", F32, "in_gates_dh")]
    sent = comm.send("mixer", {n: gbig[n] for n in ("w_out", "w_branch_b", "w_branch_a", "w_in")})
    dx1, dy1, gsmall["mix_norm"] = _residual_norm_bwd(x1, small["mix_norm"] + sent, dh2, dx2, 0.5, "mix_norm_bwd")

    dh1, sent = _ffn_bwd(h1, gu1, act1, dy1, big["ffn1_w_gu"], big["ffn1_w_down"], "ffn1", comm)
    grad_x, _, gsmall["ffn1_norm"] = _residual_norm_bwd(x, small["ffn1_norm"] + sent, [dh1], dx1, 1.0,
                                                        "ffn1_norm_bwd")
    return loss, grad_x, gsmall


GATHER_GROUPS = {"ffn1": ("ffn1_w_gu", "ffn1_w_down"),
                 "mixer": ("w_in_main", "w_in_edge", "w_branch_a", "w_branch_b", "w_out"),
                 "tail": ("ffn2_w_gu", "ffn2_w_down", "ple_gate", "ple_proj")}


def _kind(name):
    return "cols" if name in COL_SHARDED or name.startswith("w_in_") else "rows"


def _merge_w_in(main, edges):
    edge_w = WIN_W - WIN_STEP
    w_in = jnp.pad(main, ((0, 0), (0, edge_w)))
    for d in range(N_DEV):
        at = WIN_STEP * (d + 1)
        w_in = w_in + jnp.pad(edges[:, d * edge_w:(d + 1) * edge_w], ((0, 0), (at, IN_PADDED - at - edge_w)))
    return w_in


class _Fsdp:
    def __init__(self, wts):
        main, edge = _roll_w_in(jnp.pad(wts["w_in"], ((0, 0), (0, WIN_W - IN_SHARD))))
        self.shards = {n: wts[n].astype(BF16) for n in BIG if n != "w_in"}
        self.shards.update(w_in_main=main, w_in_edge=edge)
        names = [n for group in ("ffn1", "mixer", "tail") for n in GATHER_GROUPS[group]]
        full = dict(zip(names, _all_gather([self.shards[n] for n in names], [_kind(n) for n in names],
                                           len(GATHER_GROUPS["ffn1"]))))
        self.first = {n: full[n] for n in GATHER_GROUPS["ffn1"]}
        self.flight, token = {}, self.first["ffn1_w_down"][0, 0].astype(F32) * 0.0
        for group in ("mixer", "tail"):
            names = GATHER_GROUPS[group]
            srcs = [self.shards[n] for n in names]
            lands = [full[n] for n in names]
            make = _gather_copies([s.shape for s in srcs], [_kind(n) for n in names])
            srcs[0] = srcs[0] + token.astype(BF16)
            send_sems, recv_sems, srcs, lands, tok = _split_start(srcs, lands, make, "gather_start_" + group)
            token = token + tok[0, 0]
            self.flight[group] = (send_sems, recv_sems, srcs, lands, make)
        self.token = token
        self.sent = {}

    def arrive(self, group, after):
        if group == "ffn1":
            return self.first
        send_sems, recv_sems, srcs, lands, make = self.flight[group]
        _, lands = _split_wait(send_sems, recv_sems, srcs, lands, after, make, "gather_wait_" + group)
        full = dict(zip(GATHER_GROUPS[group], lands))
        if group == "mixer":
            full["w_in"] = _merge_w_in(full.pop("w_in_main"), full.pop("w_in_edge"))
        return full

    def send(self, group, grads):
        names = list(grads)
        kinds = ["win" if n == "w_in" else _kind(n) for n in names]
        shapes = [(D_MODEL, WIN_W) if n == "w_in" else self.shards[n].shape for n in names]
        srcs = [grads[n] for n in names]
        lands = [lax.empty((N_DEV,) + tuple(s), BF16) for s in shapes]
        make = _exchange_copies(shapes, kinds)
        send_sems, recv_sems, srcs, lands, tok = _split_start(srcs, lands, make, "grads_start_" + group)
        self.sent[group] = (names, kinds, send_sems, recv_sems, srcs, lands, make)
        return tok[0, 0]

    def received(self, group, after):
        names, kinds, send_sems, recv_sems, srcs, lands, make = self.sent[group]
        srcs, lands = _split_wait(send_sems, recv_sems, srcs, lands, after, make, "grads_wait_" + group)
        return {n: (k, g, r) for n, k, g, r in zip(names, kinds, srcs, lands)}


SMALL_ROWS = ("ffn1_norm", "mix_norm", "ffn2_norm", "ple_norm", "gdn_norm", "q_norm", "k_norm", "a_log", "dt_bias",
              "rel_bias", "conv_w")


def _pack_small(vals):
    rows = []
    for n in SMALL_ROWS:
        v = vals[n]
        if n == "rel_bias":
            v = jnp.pad(v, ((0, 0), (0, 2 * LANES - N_REL)))
        elif n in ("a_log", "dt_bias"):
            v = _pad_lanes(v)
        rows.append(v.reshape(-1, LANES))
    packed = jnp.concatenate(rows, axis=0)
    return jnp.pad(packed, ((0, -packed.shape[0] % SUBLANES), (0, 0)))


def _unpack_small(packed, shapes):
    out, off = {}, 0
    for n in SMALL_ROWS:
        shp = shapes[n]
        if n == "rel_bias":
            out[n] = packed[off:off + 2 * HEADS].reshape(HEADS, 2 * LANES)[:, :N_REL]
            off += 2 * HEADS
        elif n in ("a_log", "dt_bias"):
            out[n] = packed[off:off + 1, :HEADS]
            off += 1
        else:
            r = int(np.prod(shp)) // LANES
            out[n] = packed[off:off + r].reshape(shp)
            off += r
    return out


WEIGHTS = ("ffn1_norm", "ffn1_w_gu", "ffn1_w_down", "mix_norm", "w_in", "conv_w", "a_log", "dt_bias", "gdn_norm",
           "q_norm", "k_norm", "rel_bias", "w_branch_a", "w_branch_b", "w_out", "ffn2_norm", "ffn2_w_gu",
           "ffn2_w_down", "ple_norm", "ple_gate", "ple_proj")


def kernel(x, p, ffn1_norm, ffn1_w_gu, ffn1_w_down, mix_norm, w_in, conv_w, a_log, dt_bias, gdn_norm, q_norm, k_norm, rel_bias, w_branch_a, w_branch_b, w_out, ffn2_norm, ffn2_w_gu, ffn2_w_down, ple_norm, ple_gate, ple_proj, loss_target, m_ffn1_norm, m_ffn1_w_gu, m_ffn1_w_down, m_mix_norm, m_w_in, m_conv_w, m_a_log, m_dt_bias, m_gdn_norm, m_q_norm, m_k_norm, m_rel_bias, m_w_branch_a, m_w_branch_b, m_w_out, m_ffn2_norm, m_ffn2_w_gu, m_ffn2_w_down, m_ple_norm, m_ple_gate, m_ple_proj, v_ffn1_norm, v_ffn1_w_gu, v_ffn1_w_down, v_mix_norm, v_w_in, v_conv_w, v_a_log, v_dt_bias, v_gdn_norm, v_q_norm, v_k_norm, v_rel_bias, v_w_branch_a, v_w_branch_b, v_w_out, v_ffn2_norm, v_ffn2_w_gu, v_ffn2_w_down, v_ple_norm, v_ple_gate, v_ple_proj):
    args = dict(locals())
    def layer0(v):
        return v[0] if v.ndim == 3 else v

    wts = {n: layer0(args[n]) for n in WEIGHTS}
    mom = {n: layer0(args["m_" + n]) for n in WEIGHTS}
    var = {n: layer0(args["v_" + n]) for n in WEIGHTS}
    x2d, p2d, tgt = x[0], p[0, 0], loss_target[0]
    my_index = _index(*_me())

    fsdp = _Fsdp(wts)

    small = {n: wts[n] for n in SMALL_ROWS if n != "conv_w"}
    small["ffn1_norm"] = small["ffn1_norm"] + fsdp.token
    conv_shard = wts["conv_w"]
    conv_cols = conv_shard.shape[1]
    conv_packed = jnp.zeros((SUBLANES, N_DEV * conv_cols), F32)
    conv_packed = lax.dynamic_update_slice(conv_packed, jnp.pad(conv_shard, ((0, SUBLANES - CONV_K), (0, 0))),
                                           (0, my_index * conv_cols))
    small["conv_w"] = _all_reduce_small(conv_packed.reshape(-1, LANES), "conv_w_gather").reshape(SUBLANES, -1)[:CONV_K]

    loss, grad_x, gsmall = _local_step(x2d, p2d, tgt, small, fsdp)
    loss = lax.psum(loss, ("x", "y", "c"))

    outs_big, after = {}, grad_x
    for group in list(fsdp.sent):
        for n, (kind, grad, recv) in fsdp.received(group, after).items():
            if n == "w_in":
                g_in = _sum_w_in_windows(recv, grad)[:, :IN_SHARD]
                outs_big[n] = [g_in] + list(_adamw_small(wts[n], g_in, mom[n], var[n], "adamw_w_in"))
            else:
                outs_big[n] = _adamw_recv(recv, grad, kind, wts[n], mom[n], var[n], "adamw_" + n)
            after = outs_big[n][1]

    small_shapes = {n: (small[n].shape if n != "conv_w" else (CONV_K, N_DEV * conv_cols)) for n in SMALL_ROWS}
    gsum = _unpack_small(_all_reduce_small(_pack_small(gsmall), "small_grads_all_reduce"), small_shapes)
    gsum["conv_w"] = lax.dynamic_slice(gsum["conv_w"], (0, my_index * conv_cols), (CONV_K, conv_cols))
    rep = [n for n in SMALL_ROWS if n != "conv_w"]
    rep_shapes = {n: small_shapes[n] for n in rep}

    def pack_rep(vals):
        return _pack_small({**{n: vals[n] for n in rep}, "conv_w": jnp.zeros((CONV_K, LANES), F32)})

    def unpack_rep(packed):
        return _unpack_small(packed, {**rep_shapes, "conv_w": (CONV_K, LANES)})

    outs_small = [unpack_rep(o) for o in _adamw_small(pack_rep(wts), pack_rep(gsum), pack_rep(mom), pack_rep(var),
                                                      "adamw_replicated")]
    pad8 = functools.partial(jnp.pad, pad_width=((0, SUBLANES - CONV_K), (0, 0)))
    outs_conv = [o[:CONV_K] for o in _adamw_small(pad8(conv_shard), pad8(gsum["conv_w"]), pad8(mom["conv_w"]),
                                                   pad8(var["conv_w"]), "adamw_conv")]

    def leaf(kind, n):
        if n in BIG:
            return outs_big[n][kind][None]
        if n == "conv_w":
            return (gsum["conv_w"] if kind == 0 else outs_conv[kind - 1])[None]
        return (gsum[n] if kind == 0 else outs_small[kind - 1][n]).reshape(args[n].shape)

    result = [loss, grad_x[None]]
    for kind in range(4):
        result += [leaf(kind, n) for n in WEIGHTS]
    return tuple(result)
```

```python
import functools

import numpy as np
import jax
import jax.numpy as jnp
from jax import lax
from jax.experimental import pallas as pl
from jax.experimental.pallas import tpu as pltpu

F32 = jnp.float32
BF16 = jnp.bfloat16
HIGHEST = lax.Precision.HIGHEST
MESH = pl.DeviceIdType.MESH

D_MODEL = 2048
D_FF = 5632
HEADS = 8
HEAD_DIM = 128
HW = HEADS * HEAD_DIM
CHUNK = 64
LEFT_CHUNKS = 8
MAX_REL = 128
N_REL = (CHUNK - 1) + MAX_REL + 1
CONV_K = 4
EPS = 1e-6
NEG_INF = -1e30
N_DEV = 8
LANES = 128
SUBLANES = 8
VMEM_LIMIT = 56 * 1024 * 1024

MATMUL_WHOLE_K = 2048

ATT_QB = 256
ATT_KW = ATT_QB + LEFT_CHUNKS * CHUNK
ATT_PAD = LEFT_CHUNKS * CHUNK
GDN_CB = 8
GDN_GROUP = 8

ADAM_LR = 0.001
ADAM_B1 = 0.9
ADAM_B2 = 0.999
ADAM_EPS = 1e-08
ADAM_WD = 0.01
ADAM_STEP = 10

IN_QZ = 3 * HW + HW
IN_AB0 = IN_QZ
IN_QKVB0 = IN_AB0 + 2 * HEADS
IN_GG0 = IN_QKVB0 + 3 * HW
IN_COLS = IN_GG0 + 2 * D_MODEL

BIG = ("ffn1_w_gu", "ffn1_w_down", "w_in", "w_branch_a", "w_branch_b", "w_out",
       "ffn2_w_gu", "ffn2_w_down", "ple_gate", "ple_proj")
COL_SHARDED = ("ffn1_w_gu", "w_in", "w_branch_a", "w_branch_b", "ffn2_w_gu", "ple_proj")


def _params(semantics=None, **kw):
    return pltpu.CompilerParams(dimension_semantics=semantics, vmem_limit_bytes=VMEM_LIMIT, **kw)


def _pick(n, cands):
    for c in cands:
        if n % c == 0:
            return c
    return n


def _matmul(a, b, mode, out_dtype, name):
    if mode == "nn":
        (m, k), (k2, n) = a.shape, b.shape
    elif mode == "nt":
        (m, k), (n, k2) = a.shape, b.shape
    else:
        (k, m), (k2, n) = a.shape, b.shape
    assert k == k2, (a.shape, b.shape, mode)
    tm = _pick(m, (1024, 512, 256, 128))
    tn = _pick(n, (1024, 512, 256, 128))
    tk = k if k <= MATMUL_WHOLE_K else _pick(k, (2816, 2048, 1536, 1024, 512, 256, 128))
    nk = k // tk
    if mode == "nn":
        a_spec = pl.BlockSpec((tm, tk), lambda i, j, kk: (i, kk))
        b_spec = pl.BlockSpec((tk, tn), lambda i, j, kk: (kk, j))
        dims = (((1,), (0,)), ((), ()))
    elif mode == "nt":
        a_spec = pl.BlockSpec((tm, tk), lambda i, j, kk: (i, kk))
        b_spec = pl.BlockSpec((tn, tk), lambda i, j, kk: (j, kk))
        dims = (((1,), (1,)), ((), ()))
    else:
        a_spec = pl.BlockSpec((tk, tm), lambda i, j, kk: (kk, i))
        b_spec = pl.BlockSpec((tk, tn), lambda i, j, kk: (kk, j))
        dims = (((0,), (0,)), ((), ()))

    def body(a_ref, b_ref, o_ref, *acc):
        prod = lax.dot_general(a_ref[...].astype(BF16), b_ref[...].astype(BF16), dims, preferred_element_type=F32)
        if nk == 1:
            o_ref[...] = prod.astype(o_ref.dtype)
            return
        acc_ref, kk = acc[0], pl.program_id(2)

        @pl.when(kk == 0)
        def _():
            acc_ref[...] = prod

        @pl.when((kk > 0) & (kk < nk - 1))
        def _():
            acc_ref[...] += prod

        @pl.when(kk == nk - 1)
        def _():
            o_ref[...] = (acc_ref[...] + prod).astype(o_ref.dtype)

    return pl.pallas_call(
        body, name=name,
        out_shape=jax.ShapeDtypeStruct((m, n), out_dtype),
        grid=(m // tm, n // tn, nk),
        in_specs=[a_spec, b_spec],
        out_specs=pl.BlockSpec((tm, tn), lambda i, j, kk: (i, j)),
        scratch_shapes=[pltpu.VMEM((tm, tn), F32)] if nk > 1 else [],
        compiler_params=_params(("parallel", "parallel", "arbitrary")),
    )(a, b)


def _rows(fn, row_ins, consts, row_outs, acc_outs, tile, name):
    t_rows = row_ins[0][0].shape[0]
    tile = min(tile, t_rows)
    assert t_rows % tile == 0 and tile % SUBLANES == 0
    n = t_rows // tile
    per = tile // SUBLANES
    last8 = t_rows // SUBLANES - 1
    in_specs = []
    for arr, kind in row_ins:
        c = arr.shape[1]
        if kind == "t":
            in_specs.append(pl.BlockSpec((tile, c), lambda i: (i, 0)))
        elif kind == "p":
            in_specs.append(pl.BlockSpec((SUBLANES, c), lambda i: (jnp.maximum(i * per - 1, 0), 0)))
        else:
            in_specs.append(pl.BlockSpec((SUBLANES, c), lambda i: (jnp.minimum((i + 1) * per, last8), 0)))
    for arr in consts:
        in_specs.append(pl.BlockSpec(arr.shape, lambda i, nd=arr.ndim: (0,) * nd))
    out_shape = [jax.ShapeDtypeStruct((t_rows, c), dt) for c, dt in row_outs]
    out_specs = [pl.BlockSpec((tile, c), lambda i: (i, 0)) for c, _ in row_outs]
    for shp in acc_outs:
        out_shape.append(jax.ShapeDtypeStruct(shp, F32))
        out_specs.append(pl.BlockSpec(shp, lambda i, nd=len(shp): (0,) * nd))
    n_in = len(row_ins) + len(consts)
    n_row_out = len(row_outs)

    def body(*refs):
        i = pl.program_id(0)
        vals = [r[...].astype(F32) for r in refs[:len(row_ins)]]
        res = fn(i, n, *vals, *refs[len(row_ins):n_in])
        outs = refs[n_in:]
        for r, v in zip(outs[:n_row_out], res[:n_row_out]):
            r[...] = v.astype(r.dtype)
        if acc_outs:
            @pl.when(i == 0)
            def _():
                for r in outs[n_row_out:]:
                    r[...] = jnp.zeros_like(r)

            for r, v in zip(outs[n_row_out:], res[n_row_out:]):
                r[...] += v

    res = pl.pallas_call(
        body, name=name, out_shape=out_shape, grid=(n,), in_specs=in_specs, out_specs=out_specs,
        compiler_params=_params(("arbitrary",) if acc_outs else ("parallel",)),
    )(*[a for a, _ in row_ins], *consts)
    return res


def _rms(x, w):
    return x * lax.rsqrt(jnp.mean(x * x, axis=-1, keepdims=True) + EPS) * w


def _l2n(x):
    return x * lax.rsqrt(jnp.sum(x * x, axis=-1, keepdims=True) + EPS)


def _sigmoid(x):
    return 1.0 / (1.0 + jnp.exp(-x))


def _silu(x):
    return x * _sigmoid(x)


def _softplus(x):
    return jnp.maximum(x, 0.0) + jnp.log(1.0 + jnp.exp(-jnp.abs(x)))


def _heads(fn, *xs):
    nh = xs[0].shape[1] // HEAD_DIM
    return jnp.concatenate(
        [fn(*[x[:, h * HEAD_DIM:(h + 1) * HEAD_DIM] for x in xs]) for h in range(nh)], axis=1)


def _colsum(x):
    return jnp.sum(x, axis=0, keepdims=True)


def _swiglu(gu):
    return _silu(gu[:, :D_FF]) * gu[:, D_FF:]


def _gated_norm(o, z, w):
    return _heads(lambda oh, zh: _rms(oh, w) * _silu(zh), o, z)


def _mix(gg, ta, tb):
    return _sigmoid(gg[:, :D_MODEL]) * ta + _sigmoid(gg[:, D_MODEL:]) * tb


def _gdn_post(y):
    a = _silu(y)
    q = _heads(lambda v: _l2n(v) * (HEAD_DIM ** -0.5), a[:, :HW])
    k = _heads(_l2n, a[:, HW:2 * HW])
    return q, k, a[:, 2 * HW:]


NN = (((1,), (0,)), ((), ()))
NT = (((1,), (1,)), ((), ()))
TN = (((0,), (0,)), ((), ()))


def _dg(a, b, dims):
    return lax.dot_general(a, b, dims, preferred_element_type=F32)


def _split2(x):
    hi = x.astype(BF16)
    return hi, (x - hi.astype(F32)).astype(BF16)


def _split3(x):
    hi = x.astype(BF16)
    r = x - hi.astype(F32)
    mid = r.astype(BF16)
    return hi, mid, (r - mid.astype(F32)).astype(BF16)


def _dg3(a, b, dims):
    ah, al = _split2(a)
    bh, bl = _split2(b)
    return _dg(ah, bh, dims) + (_dg(ah, bl, dims) + _dg(al, bh, dims))


BNN = (((2,), (1,)), ((0,), (0,)))
BNT = (((2,), (2,)), ((0,), (0,)))
BTN = (((1,), (1,)), ((0,), (0,)))


@jax.custom_vjp
def _mm3(a, b):
    return _dg3(a, b, BNN)


_mm3.defvjp(lambda a, b: (_dg3(a, b, BNN), (a, b)),
            lambda res, g: (_dg3(g, res[1], BNT), _dg3(res[0], g, BTN)))


def _xm(x, m, dims):
    mb = m.astype(BF16)
    parts = _split3(x)
    return _dg(parts[0], mb, dims) + (_dg(parts[1], mb, dims) + _dg(parts[2], mb, dims))


def _mx(m, x, dims):
    mb = m.astype(BF16)
    parts = _split3(x)
    return _dg(mb, parts[0], dims) + (_dg(mb, parts[1], dims) + _dg(mb, parts[2], dims))


@jax.custom_vjp
def _times_const(x, m):
    return _xm(x, m, NN)


_times_const.defvjp(lambda x, m: (_xm(x, m, NN), m),
                    lambda m, g: (_xm(g, m, NT), jnp.zeros_like(m)))


@jax.custom_vjp
def _const_times(m, x):
    return _mx(m, x, NN)


_const_times.defvjp(lambda m, x: (_mx(m, x, NN), m),
                    lambda m, g: (jnp.zeros_like(m), _mx(m, g, TN)))


@jax.custom_vjp
def _lane_mean_cols(x, avg):
    return _mx(avg, x, BNT)


_lane_mean_cols.defvjp(lambda x, avg: (_mx(avg, x, BNT), avg),
                       lambda avg, g: (_xm(g, avg, BTN), jnp.zeros_like(avg)))


def _gdn_gates(ab, alog, dtb, e_g, e_b):
    t = ab.shape[0]
    g = -jnp.exp(alog) * _softplus(ab + dtb)
    beta = _sigmoid(ab)
    ri = lax.broadcasted_iota(jnp.int32, (t, t), 0)
    ci = lax.broadcasted_iota(jnp.int32, (t, t), 1)
    shift = CHUNK.bit_length() - 1
    same = jnp.right_shift(ri, shift) == jnp.right_shift(ci, shift)
    tril = jnp.where(same & (ri >= ci), 1.0, 0.0).astype(F32)
    gc = _const_times(tril, g)
    return _times_const(gc, e_g), _times_const(beta, e_b)


def _shift_down(x, halo, s, i):
    if s == 0:
        return x
    halo = jnp.where(i == 0, 0.0, halo)
    xr = pltpu.roll(x, s, 0)
    hr = pltpu.roll(halo, s, 0)
    row = lax.broadcasted_iota(jnp.int32, (SUBLANES, x.shape[1]), 0)
    top = jnp.where(row < s, hr, xr[:SUBLANES])
    return jnp.concatenate([top, xr[SUBLANES:]], axis=0)


def _shift_up(x, halo, s, i, n):
    if s == 0:
        return x
    t = x.shape[0]
    halo = jnp.where(i == n - 1, 0.0, halo)
    xr = pltpu.roll(x, t - s, 0)
    hr = pltpu.roll(halo, SUBLANES - s, 0)
    row = lax.broadcasted_iota(jnp.int32, (SUBLANES, x.shape[1]), 0)
    bot = jnp.where(row >= SUBLANES - s, hr, xr[t - SUBLANES:])
    return jnp.concatenate([xr[:t - SUBLANES], bot], axis=0)


def _conv(pa, prev, cw_ref, i):
    y = pa * cw_ref[CONV_K - 1:CONV_K, :]
    for j in range(CONV_K - 1):
        y = y + _shift_down(pa, prev, CONV_K - 1 - j, i) * cw_ref[j:j + 1, :]
    return y


def _dot_nt(a, b, precision=None):
    return lax.dot_general(a, b, (((1,), (1,)), ((), ())), precision=precision, preferred_element_type=F32)


def _dot_tn(a, b, precision=None):
    return lax.dot_general(a, b, (((0,), (0,)), ((), ())), precision=precision, preferred_element_type=F32)


def _dot(a, b, precision=None):
    return jnp.dot(a, b, precision=precision, preferred_element_type=F32)


def _bf(x):
    return x.astype(BF16)


def _gdn_chunk(q, k, v, gc, bb):
    nb, c, _ = q.shape
    ri = lax.broadcasted_iota(jnp.int32, (nb, c, c), 1)
    ci = lax.broadcasted_iota(jnp.int32, (nb, c, c), 2)
    incl = ri >= ci
    strict = ri > ci
    g_row = gc[:, :, :c]
    g_col = _lane_mean_cols(gc, jnp.full((nb, c, LANES), 1.0 / LANES, F32))
    decay = jnp.where(incl, jnp.exp(jnp.where(incl, g_row - g_col, 0.0)), 0.0)
    kb = k * bb
    lmat = jnp.where(strict, _dg(_bf(kb), _bf(k), BNT) * decay, 0.0)
    eye = jnp.where(ri == ci, 1.0, 0.0).astype(F32)
    pw = -lmat
    inv = eye + pw
    for _ in range(5):
        pw = _mm3(pw, pw)
        inv = inv + _mm3(inv, pw)
    egc = jnp.exp(gc)
    u = _mm3(inv, v * bb)
    w = _mm3(inv, kb * egc)
    aqk = _dg(_bf(q), _bf(k), BNT) * decay
    last = lax.broadcasted_iota(jnp.int32, (nb, c, LANES), 1) == c - 1
    tot = jnp.sum(jnp.where(last, gc, 0.0), axis=1, keepdims=True)
    k_tail = k * jnp.exp(tot - gc)
    tail = jnp.broadcast_to(jnp.exp(tot), (nb, SUBLANES, LANES))
    return u, w, aqk, q * egc, k_tail, tail


def _gdn_intra(qn, kn, vv, g_b, beta_b):
    t_rows = qn.shape[0]
    nc = t_rows // CHUNK
    cb = min(GDN_CB, nc)
    rows = cb * CHUNK
    col = pl.BlockSpec((rows, HEAD_DIM), lambda h, b: (b, h))

    def body(q_ref, k_ref, v_ref, g_ref, b_ref, u_ref, w_ref, a_ref, qd_ref, kt_ref, tl_ref):
        def group(gi, carry):
            r = pl.ds(pl.multiple_of(gi * (grp * CHUNK), grp * CHUNK), grp * CHUNK)
            ins = [ref[r, :].reshape(grp, CHUNK, HEAD_DIM) for ref in (q_ref, k_ref, v_ref, g_ref, b_ref)]
            u, w, aqk, qd, kt, tl = _gdn_chunk(*ins)
            for ref, val in ((u_ref, u), (w_ref, w), (qd_ref, qd), (kt_ref, kt)):
                ref[r, :] = val.reshape(grp * CHUNK, HEAD_DIM)
            a_ref[0, r, :] = aqk.reshape(grp * CHUNK, CHUNK)
            tl_ref[0, pl.ds(gi * grp, grp)] = tl
            return carry

        grp = min(GDN_GROUP, cb)
        lax.fori_loop(0, cb // grp, group, 0)

    full = jax.ShapeDtypeStruct((t_rows, HW), F32)
    return pl.pallas_call(
        body, name="gdn_intra_fwd",
        out_shape=[full, full, jax.ShapeDtypeStruct((HEADS, t_rows, CHUNK), F32), full, full,
                   jax.ShapeDtypeStruct((HEADS, nc, SUBLANES, LANES), F32)],
        grid=(HEADS, nc // cb),
        in_specs=[col] * 5,
        out_specs=[col, col, pl.BlockSpec((1, rows, CHUNK), lambda h, b: (h, b, 0)), col, col,
                   pl.BlockSpec((1, cb, SUBLANES, LANES), lambda h, b: (h, b, 0, 0))],
        compiler_params=_params(("parallel", "parallel")),
    )(qn, kn, vv, g_b, beta_b)


def _gdn_intra_bwd(qn, kn, vv, g_b, beta_b, du, dw, da, dqd, dkt, dtl):
    t_rows = qn.shape[0]
    nc = t_rows // CHUNK
    cb = min(GDN_CB, nc)
    rows = cb * CHUNK
    col = pl.BlockSpec((rows, HEAD_DIM), lambda h, b: (b, h))
    a_spec = pl.BlockSpec((1, rows, CHUNK), lambda h, b: (h, b, 0))
    tl_spec = pl.BlockSpec((1, cb, SUBLANES, LANES), lambda h, b: (h, b, 0, 0))

    def body(q_ref, k_ref, v_ref, g_ref, b_ref, du_ref, dw_ref, da_ref, dqd_ref, dkt_ref, dtl_ref,
             dq_ref, dk_ref, dv_ref, dg_ref, db_ref):
        def group(gi, carry):
            r = pl.ds(pl.multiple_of(gi * (grp * CHUNK), grp * CHUNK), grp * CHUNK)
            wide = (grp, CHUNK, HEAD_DIM)
            ins = [ref[r, :].reshape(wide) for ref in (q_ref, k_ref, v_ref, g_ref, b_ref)]
            cts = (du_ref[r, :].reshape(wide), dw_ref[r, :].reshape(wide),
                   da_ref[0, r, :].reshape(grp, CHUNK, CHUNK), dqd_ref[r, :].reshape(wide),
                   dkt_ref[r, :].reshape(wide), dtl_ref[0, pl.ds(gi * grp, grp)])
            grads = jax.vjp(_gdn_chunk, *ins)[1](cts)
            for ref, val in zip((dq_ref, dk_ref, dv_ref, dg_ref, db_ref), grads):
                ref[r, :] = val.reshape(grp * CHUNK, HEAD_DIM)
            return carry

        grp = min(GDN_GROUP, cb)
        lax.fori_loop(0, cb // grp, group, 0)

    full = jax.ShapeDtypeStruct((t_rows, HW), F32)
    return pl.pallas_call(
        body, name="gdn_intra_bwd",
        out_shape=[full] * 5,
        grid=(HEADS, nc // cb),
        in_specs=[col] * 7 + [a_spec, col, col, tl_spec],
        out_specs=[col] * 5,
        compiler_params=_params(("parallel", "parallel")),
    )(qn, kn, vv, g_b, beta_b, du, dw, da, dqd, dkt, dtl)


def _head_cols(h):
    return slice(h * HEAD_DIM, (h + 1) * HEAD_DIM)


def _gdn_scan(u, w, aqk, qd, kt, tl):
    t_rows = u.shape[0]
    nc = t_rows // CHUNK
    cb = min(GDN_CB, nc)
    rows = cb * CHUNK
    wide = pl.BlockSpec((rows, HW), lambda b: (b, 0))

    def body(u_ref, w_ref, a_ref, qd_ref, kt_ref, tl_ref, o_ref, s_out_ref, s_ref):
        @pl.when(pl.program_id(0) == 0)
        def _():
            s_ref[...] = jnp.zeros_like(s_ref)

        def chunk(ci, carry):
            r = pl.ds(pl.multiple_of(ci * CHUNK, CHUNK), CHUNK)
            for h in range(HEADS):
                hc = _head_cols(h)
                s = s_ref[h]
                s_out_ref[ci, h] = s
                sb = _bf(s)
                vn = u_ref[r, hc] - _dot(_bf(w_ref[r, hc]), sb)
                vnb = _bf(vn)
                o_ref[r, hc] = _dot(_bf(qd_ref[r, hc]), sb) + _dot(_bf(a_ref[h, r, :]), vnb)
                s_ref[h] = s * tl_ref[h, ci, 0:1, :] + _dot_tn(_bf(kt_ref[r, hc]), vnb)
            return carry

        lax.fori_loop(0, cb, chunk, 0)

    return pl.pallas_call(
        body, name="gdn_scan_fwd",
        out_shape=[jax.ShapeDtypeStruct((t_rows, HW), F32),
                   jax.ShapeDtypeStruct((nc, HEADS, HEAD_DIM, HEAD_DIM), F32)],
        grid=(nc // cb,),
        in_specs=[wide, wide, pl.BlockSpec((HEADS, rows, CHUNK), lambda b: (0, b, 0)), wide, wide,
                  pl.BlockSpec((HEADS, cb, SUBLANES, LANES), lambda b: (0, b, 0, 0))],
        out_specs=[wide, pl.BlockSpec((cb, HEADS, HEAD_DIM, HEAD_DIM), lambda b: (b, 0, 0, 0))],
        scratch_shapes=[pltpu.VMEM((HEADS, HEAD_DIM, HEAD_DIM), F32)],
        compiler_params=_params(("arbitrary",)),
    )(u, w, aqk, qd, kt, tl)


def _gdn_scan_bwd(do, u, w, aqk, qd, kt, tl, states):
    t_rows = u.shape[0]
    nc = t_rows // CHUNK
    cb = min(GDN_CB, nc)
    rows = cb * CHUNK
    nb = nc // cb
    wide = pl.BlockSpec((rows, HW), lambda b: (nb - 1 - b, 0))
    a_spec = pl.BlockSpec((HEADS, rows, CHUNK), lambda b: (0, nb - 1 - b, 0))
    tl_spec = pl.BlockSpec((HEADS, cb, SUBLANES, LANES), lambda b: (0, nb - 1 - b, 0, 0))

    def body(do_ref, u_ref, w_ref, a_ref, qd_ref, kt_ref, tl_ref, s_in_ref,
             du_ref, dw_ref, da_ref, dqd_ref, dkt_ref, dtl_ref, ds_ref):
        @pl.when(pl.program_id(0) == 0)
        def _():
            ds_ref[...] = jnp.zeros_like(ds_ref)

        row0 = lax.broadcasted_iota(jnp.int32, (SUBLANES, LANES), 0) == 0

        def chunk(step, carry):
            ci = cb - 1 - step
            r = pl.ds(pl.multiple_of(ci * CHUNK, CHUNK), CHUNK)
            for h in range(HEADS):
                hc = _head_cols(h)
                s = s_in_ref[ci, h]
                ds_next = ds_ref[h]
                sb, dsb = _bf(s), _bf(ds_next)
                wb, ab, ktb, qdb = _bf(w_ref[r, hc]), _bf(a_ref[h, r, :]), _bf(kt_ref[r, hc]), _bf(qd_ref[r, hc])
                dob = _bf(do_ref[r, hc])
                vn = u_ref[r, hc] - _dot(wb, sb)
                vnb = _bf(vn)
                dvn = _dot_tn(ab, dob) + _dot(ktb, dsb)
                dvnb = _bf(dvn)
                du_ref[r, hc] = dvn
                dw_ref[r, hc] = -_dot_nt(dvnb, sb)
                da_ref[h, r, :] = _dot_nt(dob, vnb)
                dqd_ref[r, hc] = _dot_nt(dob, sb)
                dkt_ref[r, hc] = _dot_nt(vnb, dsb)
                dtl_ref[h, ci] = jnp.where(row0, _colsum(s * ds_next), 0.0)
                ds_ref[h] = _dot_tn(qdb, dob) + ds_next * tl_ref[h, ci, 0:1, :] - _dot_tn(wb, dvnb)
            return carry

        lax.fori_loop(0, cb, chunk, 0)

    full = jax.ShapeDtypeStruct((t_rows, HW), F32)
    return pl.pallas_call(
        body, name="gdn_scan_bwd",
        out_shape=[full, full, jax.ShapeDtypeStruct((HEADS, t_rows, CHUNK), F32), full, full,
                   jax.ShapeDtypeStruct((HEADS, nc, SUBLANES, LANES), F32)],
        grid=(nb,),
        in_specs=[wide, wide, wide, a_spec, wide, wide, tl_spec,
                  pl.BlockSpec((cb, HEADS, HEAD_DIM, HEAD_DIM), lambda b: (nb - 1 - b, 0, 0, 0))],
        out_specs=[wide, wide, a_spec, wide, wide, tl_spec],
        scratch_shapes=[pltpu.VMEM((HEADS, HEAD_DIM, HEAD_DIM), F32)],
        compiler_params=_params(("arbitrary",)),
    )(do, u, w, aqk, qd, kt, tl, states)


def _att_rel_index():
    qi = lax.broadcasted_iota(jnp.int32, (ATT_QB, ATT_KW), 0)
    kj = lax.broadcasted_iota(jnp.int32, (ATT_QB, ATT_KW), 1)
    return jnp.clip(qi - kj + ATT_PAD, -(CHUNK - 1), MAX_REL) + (CHUNK - 1)


def _att_in_band():
    qi = lax.broadcasted_iota(jnp.int32, (ATT_QB, ATT_KW), 0)
    kj = lax.broadcasted_iota(jnp.int32, (ATT_QB, ATT_KW), 1)
    shift = CHUNK.bit_length() - 1
    qc = jnp.right_shift(qi, shift)
    kc = jnp.right_shift(kj, shift) - LEFT_CHUNKS
    return (kc <= qc) & (kc >= qc - LEFT_CHUNKS)


def _att_valid(b):
    kj = lax.broadcasted_iota(jnp.int32, (ATT_QB, ATT_KW), 1)
    return kj + b * ATT_QB >= ATT_PAD


def _att_block(q_raw, k_raw, v, qw, kw, bias, valid):
    q = _rms(q_raw, qw)
    k = _rms(k_raw, kw)
    s = _dot_nt(_bf(q), _bf(k)) * (HEAD_DIM ** -0.5) + bias
    s = jnp.where(valid, s, NEG_INF)
    p = jnp.exp(s - jnp.max(s, axis=-1, keepdims=True))
    p = p * (1.0 / jnp.sum(p, axis=-1, keepdims=True))
    return _dot(_bf(p), _bf(v))


def _att_specs():
    q_spec = pl.BlockSpec((ATT_QB, HEAD_DIM), lambda h, b: (b, h))
    k_specs = [pl.BlockSpec((ATT_QB, HEAD_DIM), lambda h, b, j=j: (b + j, HEADS + h)) for j in range(3)]
    v_specs = [pl.BlockSpec((ATT_QB, HEAD_DIM), lambda h, b, j=j: (b + j, 2 * HEADS + h)) for j in range(3)]
    w_spec = pl.BlockSpec((1, HEAD_DIM), lambda h, b: (0, 0))
    smem = pl.BlockSpec(memory_space=pltpu.SMEM)
    return q_spec, k_specs, v_specs, w_spec, smem


def _att_fill_bias(bias_ref, rel_ref, h):
    idx = _att_rel_index()

    def fill(r, acc):
        return jnp.where(idx == r, rel_ref[h, r], acc)

    table = lax.fori_loop(0, N_REL, fill, jnp.zeros((ATT_QB, ATT_KW), F32))
    bias_ref[...] = jnp.where(_att_in_band(), table, NEG_INF)


def _attention(pb, pbp, qw, kw, rel):
    t_rows = pb.shape[0]
    q_spec, k_specs, v_specs, w_spec, smem = _att_specs()

    def body(q_ref, k0, k1, k2, v0, v1, v2, qw_ref, kw_ref, rel_ref, o_ref, bias_ref):
        h, b = pl.program_id(0), pl.program_id(1)

        @pl.when(b == 0)
        def _():
            _att_fill_bias(bias_ref, rel_ref, h)

        kwin = jnp.concatenate([k0[...], k1[...], k2[...]], axis=0)
        vwin = jnp.concatenate([v0[...], v1[...], v2[...]], axis=0)
        o = _att_block(q_ref[...], kwin, vwin, qw_ref[...], kw_ref[...], bias_ref[...], _att_valid(b))
        o_ref[...] = o.astype(o_ref.dtype)

    return pl.pallas_call(
        body, name="band_attention_fwd",
        out_shape=jax.ShapeDtypeStruct((t_rows, HW), BF16),
        grid=(HEADS, t_rows // ATT_QB),
        in_specs=[q_spec] + k_specs + v_specs + [w_spec, w_spec, smem],
        out_specs=pl.BlockSpec((ATT_QB, HEAD_DIM), lambda h, b: (b, h)),
        scratch_shapes=[pltpu.VMEM((ATT_QB, ATT_KW), F32)],
        compiler_params=_params(("arbitrary", "arbitrary")),
    )(pb, pbp, pbp, pbp, pbp, pbp, pbp, qw, kw, rel)


def _attention_bwd(pb, pbp, qw, kw, rel, dyb):
    t_rows = pb.shape[0]
    nb = t_rows // ATT_QB
    q_spec, k_specs, v_specs, w_spec, smem = _att_specs()
    pad_rows = t_rows + ATT_PAD
    acc_spec = pl.BlockSpec((pad_rows, HEAD_DIM), lambda h, b: (0, h))

    def body(q_ref, k0, k1, k2, v0, v1, v2, qw_ref, kw_ref, rel_ref, do_ref,
             dq_ref, dk_ref, dv_ref, dqw_ref, dkw_ref, drel_ref, bias_ref, dbias_ref):
        h, b = pl.program_id(0), pl.program_id(1)

        @pl.when(b == 0)
        def _():
            _att_fill_bias(bias_ref, rel_ref, h)
            dbias_ref[...] = jnp.zeros_like(dbias_ref)
            dk_ref[...] = jnp.zeros_like(dk_ref)
            dv_ref[...] = jnp.zeros_like(dv_ref)

        @pl.when((b == 0) & (h == 0))
        def _():
            dqw_ref[...] = jnp.zeros_like(dqw_ref)
            dkw_ref[...] = jnp.zeros_like(dkw_ref)

        kwin = jnp.concatenate([k0[...], k1[...], k2[...]], axis=0)
        vwin = jnp.concatenate([v0[...], v1[...], v2[...]], axis=0)
        valid = _att_valid(b)
        _, vjp = jax.vjp(lambda q, k, v, a, c, bias: _att_block(q, k, v, a, c, bias, valid),
                         q_ref[...], kwin, vwin, qw_ref[...], kw_ref[...], bias_ref[...])
        dq, dk, dv, dqw, dkw, dbias = vjp(do_ref[...])
        dq_ref[...] = dq.astype(dq_ref.dtype)
        win = pl.ds(pl.multiple_of(b * ATT_QB, ATT_QB), ATT_KW)
        dk_ref[win, :] += dk
        dv_ref[win, :] += dv
        dqw_ref[...] += dqw
        dkw_ref[...] += dkw
        dbias_ref[...] += dbias

        @pl.when(b == nb - 1)
        def _():
            idx = _att_rel_index()
            tot = dbias_ref[...]

            def reduce(r, carry):
                drel_ref[h, r] = jnp.sum(jnp.where(idx == r, tot, 0.0))
                return carry

            lax.fori_loop(0, N_REL, reduce, 0)

    return pl.pallas_call(
        body, name="band_attention_bwd",
        out_shape=[jax.ShapeDtypeStruct((t_rows, HW), BF16),
                   jax.ShapeDtypeStruct((pad_rows, HW), F32), jax.ShapeDtypeStruct((pad_rows, HW), F32),
                   jax.ShapeDtypeStruct((1, HEAD_DIM), F32), jax.ShapeDtypeStruct((1, HEAD_DIM), F32),
                   jax.ShapeDtypeStruct((HEADS, N_REL), F32)],
        grid=(HEADS, nb),
        in_specs=[q_spec] + k_specs + v_specs + [w_spec, w_spec, smem, q_spec],
        out_specs=[q_spec, acc_spec, acc_spec, w_spec, w_spec, smem],
        scratch_shapes=[pltpu.VMEM((ATT_QB, ATT_KW), F32), pltpu.VMEM((ATT_QB, ATT_KW), F32)],
        compiler_params=_params(("arbitrary", "arbitrary")),
    )(pb, pbp, pbp, pbp, pbp, pbp, pbp, qw, kw, rel, dyb)


def _me():
    return lax.axis_index("x"), lax.axis_index("y"), lax.axis_index("c")


def _index(x, y, c):
    return 4 * x + 2 * y + c


HBM_SPEC = pl.BlockSpec(memory_space=pl.ANY)


def _block(ref, kind, d, r, c):
    if kind == "rows":
        return ref.at[pl.ds(d * r, r), :]
    if kind == "win":
        return ref.at[:, pl.ds(d * WIN_STEP, c)]
    return ref.at[:, pl.ds(d * c, c)]


def _all_gather(shards, kinds, n_gather):
    n = len(shards)

    def body(*refs):
        x_refs, out_refs = refs[:n], refs[n:2 * n]
        send_sems, recv_sems, local_sems = refs[2 * n:]
        x, y, c = _me()
        me, sibling = (x, y, c), (x, y, 1 - c)
        chips = [(1 - x, y), (x, 1 - y), (1 - x, 1 - y)]

        def copy(i, k, blk, to, src=None):
            r_, c_ = shards[i].shape
            dst = _block(out_refs[i], kinds[i], _index(*blk), r_, c_)
            return pltpu.make_async_remote_copy(
                src_ref=dst if src is None else src, dst_ref=dst,
                send_sem=send_sems.at[i, k], recv_sem=recv_sems.at[i, k], device_id=to, device_id_type=MESH)

        sends, local = [], []
        for i in range(n):
            r_, c_ = shards[i].shape
            mine = pltpu.make_async_copy(x_refs[i], _block(out_refs[i], kinds[i], _index(*me), r_, c_),
                                         local_sems.at[i])
            mine.start()
            local.append(mine)
            if i >= n_gather:
                continue
            first = [copy(i, 0, me, sibling, src=x_refs[i])]
            first += [copy(i, 1 + j, me, (*chip, c), src=x_refs[i]) for j, chip in enumerate(chips)]
            for cp in first:
                cp.start()
            sends += first
        for i in range(n_gather):
            for j, chip in enumerate(chips):
                copy(i, 1 + j, (*chip, c), me).wait_recv()
                passed = copy(i, 4 + j, (*chip, c), sibling)
                passed.start()
                sends.append(passed)
        for i in range(n_gather):
            copy(i, 0, sibling, me).wait_recv()
            for j, chip in enumerate(chips):
                copy(i, 4 + j, (*chip, 1 - c), me).wait_recv()
        for cp in sends:
            cp.wait_send()
        for cp in local:
            cp.wait()

    def full_shape(s, kind):
        r_, c_ = s.shape
        return (N_DEV * r_, c_) if kind == "rows" else (r_, N_DEV * c_)

    return pl.pallas_call(
        body, name="weights_all_gather",
        out_shape=[jax.ShapeDtypeStruct(full_shape(s, k), s.dtype) for s, k in zip(shards, kinds)],
        in_specs=[HBM_SPEC] * n, out_specs=[HBM_SPEC] * n,
        scratch_shapes=[pltpu.SemaphoreType.DMA((n_gather, 7)), pltpu.SemaphoreType.DMA((n_gather, 7)),
                        pltpu.SemaphoreType.DMA((n,))],
        compiler_params=pltpu.CompilerParams(has_side_effects=True),
    )(*shards)


SEM_SPEC = pl.BlockSpec(memory_space=pltpu.SEMAPHORE)
HBM_ONLY = pl.BlockSpec(memory_space=pltpu.HBM)
DATAFLOW = pltpu.SideEffectType.DATAFLOW_SIDE_EFFECTING


def _peers():
    x, y, c = _me()
    return [(x ^ (k >> 2), y ^ ((k >> 1) & 1), c ^ (k & 1)) for k in range(1, N_DEV)]


def _gather_copies(shapes, kinds):
    def make(src_refs, land_refs, send_sems, recv_sems):
        mine = _index(*_me())
        return [pltpu.make_async_remote_copy(
            src_ref=src_refs[i], dst_ref=_block(land_refs[i], kind, mine, r, c),
            send_sem=send_sems.at[7 * i + k], recv_sem=recv_sems.at[7 * i + k], device_id=peer, device_id_type=MESH)
            for i, ((r, c), kind) in enumerate(zip(shapes, kinds)) for k, peer in enumerate(_peers())]

    return make


def _exchange_copies(shapes, kinds):
    def make(src_refs, land_refs, send_sems, recv_sems):
        mine = _index(*_me())
        return [pltpu.make_async_remote_copy(
            src_ref=_block(src_refs[i], kind, _index(*peer), r, c), dst_ref=land_refs[i].at[mine],
            send_sem=send_sems.at[7 * i + k], recv_sem=recv_sems.at[7 * i + k], device_id=peer, device_id_type=MESH)
            for i, ((r, c), kind) in enumerate(zip(shapes, kinds)) for k, peer in enumerate(_peers())]

    return make


def _place_block(shard, kind, name):
    r, c = shard.shape
    tile = _row_tile(r, c)
    nt = r // tile
    full = (N_DEV * r, c) if kind == "rows" else (r, N_DEV * c)

    def body(me_ref, x_ref, out_ref):
        out_ref[...] = x_ref[...]

    if kind == "rows":
        out_spec = pl.BlockSpec((tile, c), lambda i, me: (me[0] * nt + i, 0))
    else:
        out_spec = pl.BlockSpec((tile, c), lambda i, me: (i, me[0]))
    return pl.pallas_call(
        body, name=name, out_shape=jax.ShapeDtypeStruct(full, shard.dtype),
        grid_spec=pltpu.PrefetchScalarGridSpec(
            num_scalar_prefetch=1, grid=(nt,),
            in_specs=[pl.BlockSpec((tile, c), lambda i, me: (i, 0))], out_specs=out_spec),
        compiler_params=_params(("arbitrary",)),
    )(_my_index_operand(), shard)


def _split_start(srcs, lands, make, name):
    n = len(srcs)

    def body(*refs):
        send_sems, recv_sems = refs[2 * n], refs[2 * n + 1]
        for cp in make(refs[:n], refs[n:2 * n], send_sems, recv_sems):
            cp.start()
        refs[-1][...] = jnp.zeros_like(refs[-1])

    arrays = list(srcs) + list(lands)
    out = pl.pallas_call(
        body, name=name,
        out_shape=(pltpu.SemaphoreType.DMA((7 * n,)), pltpu.SemaphoreType.DMA((7 * n,)),
                   *[pltpu.HBM(a.shape, a.dtype) for a in arrays], jax.ShapeDtypeStruct((SUBLANES, LANES), F32)),
        in_specs=[HBM_ONLY] * (2 * n),
        out_specs=(SEM_SPEC, SEM_SPEC, *[HBM_ONLY] * (2 * n), pl.BlockSpec(memory_space=pltpu.VMEM)),
        input_output_aliases={i: 2 + i for i in range(2 * n)},
        compiler_params=pltpu.CompilerParams(has_side_effects=DATAFLOW),
    )(*[pltpu.with_memory_space_constraint(a, pltpu.HBM) for a in arrays])
    return out[0], out[1], list(out[2:2 + n]), list(out[2 + n:2 + 2 * n]), out[-1]


def _split_wait(send_sems, recv_sems, srcs, lands, after, make, name):
    n = len(srcs)

    def body(*refs):
        for cp in make(refs[:n], refs[n:2 * n], refs[2 * n], refs[2 * n + 1]):
            cp.wait_send()
            cp.wait_recv()

    arrays = list(srcs) + list(lands)
    out = pl.pallas_call(
        body, name=name,
        out_shape=tuple(pltpu.HBM(a.shape, a.dtype) for a in arrays),
        in_specs=[HBM_ONLY] * (2 * n) + [SEM_SPEC, SEM_SPEC, pl.BlockSpec(memory_space=pl.ANY)],
        out_specs=tuple([HBM_ONLY] * (2 * n)),
        input_output_aliases={i: i for i in range(2 * n)},
        compiler_params=pltpu.CompilerParams(has_side_effects=DATAFLOW),
    )(*arrays, send_sems, recv_sems, after)
    return list(out[:n]), list(out[n:])


def _all_reduce_small(vals, name):
    rows, width = vals.shape

    def body(x_ref, out_ref, buf_ref, send_sems, recv_sems):
        x, y, c = _me()
        mine = _index(x, y, c)
        buf_ref[mine] = x_ref[...]
        copies = []
        for k in range(1, N_DEV):
            px, py, pc = x ^ (k >> 2), y ^ ((k >> 1) & 1), c ^ (k & 1)
            copies.append(pltpu.make_async_remote_copy(
                src_ref=x_ref, dst_ref=buf_ref.at[mine],
                send_sem=send_sems.at[k - 1], recv_sem=recv_sems.at[k - 1],
                device_id=(px, py, pc), device_id_type=MESH))
        for cp in copies:
            cp.start()
        for cp in copies:
            cp.wait()
        acc = buf_ref[0]
        for j in range(1, N_DEV):
            acc = acc + buf_ref[j]
        out_ref[...] = acc

    vmem = pl.BlockSpec(memory_space=pltpu.VMEM)
    return pl.pallas_call(
        body, name=name,
        out_shape=jax.ShapeDtypeStruct(vals.shape, F32),
        in_specs=[vmem], out_specs=vmem,
        scratch_shapes=[pltpu.VMEM((N_DEV, rows, width), F32),
                        pltpu.SemaphoreType.DMA((7,)), pltpu.SemaphoreType.DMA((7,))],
        compiler_params=pltpu.CompilerParams(has_side_effects=True),
    )(vals)


def _adamw_math(w, g, m, v):
    m = ADAM_B1 * m + (1.0 - ADAM_B1) * g
    v = ADAM_B2 * v + (1.0 - ADAM_B2) * (g * g)
    m_hat = m / (1.0 - ADAM_B1 ** ADAM_STEP)
    v_hat = v / (1.0 - ADAM_B2 ** ADAM_STEP)
    delta = -ADAM_LR * (m_hat / (jnp.sqrt(v_hat) + ADAM_EPS) + ADAM_WD * w)
    return delta, m, v


ROW_TILE_ELEMS = 384 * 1024


def _row_tile(rows, width):
    best = SUBLANES
    for t in range(SUBLANES, rows + 1, SUBLANES):
        if rows % t == 0 and t * width <= ROW_TILE_ELEMS:
            best = t
    return best


def _sum_received(r_ref, own, me):
    g = None
    for j in range(N_DEV):
        term = jnp.where(me == j, own, r_ref[j].astype(F32))
        g = term if g is None else g + term
    return g


def _my_index_operand():
    return _index(*_me()).astype(jnp.int32).reshape(1)


def _adamw_recv(recv, grad, kind, w, m, v, name):
    _, rows, width = recv.shape
    tile = _row_tile(rows, width)
    nt = rows // tile

    def body(me_ref, r_ref, own_ref, w_ref, m_ref, v_ref, g_out, d_out, m_out, v_out):
        g = _sum_received(r_ref, own_ref[...].astype(F32), me_ref[0])
        d, mn, vn = _adamw_math(w_ref[...], g, m_ref[...], v_ref[...])
        g_out[...] = g
        d_out[...] = d
        m_out[...] = mn
        v_out[...] = vn

    if kind == "rows":
        own_spec = pl.BlockSpec((tile, width), lambda i, me: (me[0] * nt + i, 0))
    else:
        own_spec = pl.BlockSpec((tile, width), lambda i, me: (i, me[0]))
    spec = pl.BlockSpec((tile, width), lambda i, me: (i, 0))
    shape = jax.ShapeDtypeStruct((rows, width), F32)
    return pl.pallas_call(
        body, name=name, out_shape=[shape] * 4,
        grid_spec=pltpu.PrefetchScalarGridSpec(
            num_scalar_prefetch=1, grid=(nt,),
            in_specs=[pl.BlockSpec((N_DEV, tile, width), lambda i, me: (0, i, 0)), own_spec, spec, spec, spec],
            out_specs=[spec] * 4),
        compiler_params=_params(("parallel",)),
    )(_my_index_operand(), recv, grad, w, m, v)


WIN_STEP = 1408
WIN_W = 1536
IN_SHARD = IN_COLS // N_DEV
IN_PADDED = WIN_STEP * (N_DEV - 1) + WIN_W


def _roll_w_in(shard_padded):
    rows = shard_padded.shape[0]
    tile = _row_tile(rows, WIN_W)

    def body(x_ref, main_ref, edge_ref):
        win = pltpu.roll(x_ref[...], 2 * _index(*_me()), 1).astype(BF16)
        main_ref[...] = win[:, :WIN_STEP]
        edge_ref[...] = win[:, WIN_STEP:]

    return pl.pallas_call(
        body, name="w_in_window",
        out_shape=[jax.ShapeDtypeStruct((rows, WIN_STEP), BF16), jax.ShapeDtypeStruct((rows, WIN_W - WIN_STEP), BF16)],
        grid=(rows // tile,),
        in_specs=[pl.BlockSpec((tile, WIN_W), lambda i: (i, 0))],
        out_specs=[pl.BlockSpec((tile, WIN_STEP), lambda i: (i, 0)),
                   pl.BlockSpec((tile, WIN_W - WIN_STEP), lambda i: (i, 0))],
        compiler_params=_params(("parallel",)),
    )(shard_padded)


def _sum_w_in_windows(recv, grad):
    _, rows, width = recv.shape
    tile = _row_tile(rows, width)

    def body(me_ref, r_ref, g_ref, g_out, own_ref, sem):
        me = me_ref[0]
        rows_i = pl.ds(pl.multiple_of(pl.program_id(0) * tile, tile), tile)
        own = pltpu.make_async_copy(g_ref.at[rows_i, pl.ds(pl.multiple_of(me * WIN_STEP, LANES), width)], own_ref, sem)
        own.start()
        own.wait()
        g_out[...] = pltpu.roll(_sum_received(r_ref, own_ref[...].astype(F32), me), width - 2 * me, 1)

    return pl.pallas_call(
        body, name="w_in_grad_sum", out_shape=jax.ShapeDtypeStruct((rows, width), F32),
        grid_spec=pltpu.PrefetchScalarGridSpec(
            num_scalar_prefetch=1, grid=(rows // tile,),
            in_specs=[pl.BlockSpec((N_DEV, tile, width), lambda i, me: (0, i, 0)), HBM_SPEC],
            out_specs=pl.BlockSpec((tile, width), lambda i, me: (i, 0)),
            scratch_shapes=[pltpu.VMEM((tile, width), BF16), pltpu.SemaphoreType.DMA]),
        compiler_params=_params(("arbitrary",)),
    )(_my_index_operand(), recv, grad)


def _adamw_small(w, g, m, v, name):
    def fn(i, n, w_, g_, m_, v_):
        return _adamw_math(w_, g_, m_, v_)

    r, c = w.shape
    return _rows(fn, [(w, "t"), (g, "t"), (m, "t"), (v, "t")], [], [(c, F32)] * 3, [], _row_tile(r, c), name)


def _norm_fwd(x, w, name):
    return _rows(lambda i, n, x_, w_: (_rms(x_, w_[...]),), [(x, "t")], [w], [(D_MODEL, BF16)], [], 512, name)[0]


def _residual_norm_fwd(x, y, scale, w, name):
    def fn(i, n, x_, y_, w_):
        xn = x_ + scale * y_
        return xn, _rms(xn, w_[...])

    return _rows(fn, [(x, "t"), (y, "t")], [w], [(D_MODEL, F32), (D_MODEL, BF16)], [], 512, name)


def _residual_norm_bwd(x, w, dhs, dres, scale, name):
    nh = len(dhs)

    def fn(i, n, x_, dres_, *rest):
        dh = rest[0]
        for extra in rest[1:nh]:
            dh = dh + extra
        _, vjp = jax.vjp(_rms, x_, rest[nh][...])
        dx, dw = vjp(dh)
        dx = dx + dres_
        return dx, scale * dx, dw

    return _rows(fn, [(x, "t"), (dres, "t")] + [(d, "t") for d in dhs], [w],
                 [(D_MODEL, F32), (D_MODEL, BF16)], [(1, D_MODEL)], 256, name)


def _ffn_fwd(h, w_gu, w_down, tag):
    gu = _matmul(h, w_gu, "nn", BF16, tag + "_gu")
    act = _rows(lambda i, n, gu_: (_swiglu(gu_),), [(gu, "t")], [], [(D_FF, BF16)], [], 128, tag + "_swiglu")[0]
    y = _matmul(act, w_down, "nn", F32, tag + "_down")
    return gu, act, y


def _ffn_bwd(h, gu, act, dy, w_gu, w_down, tag, comm, more=None):
    dact = _matmul(dy, w_down, "nt", BF16, tag + "_dact")

    def fn(i, n, gu_, dact_):
        _, vjp = jax.vjp(_swiglu, gu_)
        return vjp(dact_)

    dgu = _rows(fn, [(gu, "t"), (dact, "t")], [], [(2 * D_FF, BF16)], [], 128, tag + "_swiglu_bwd")[0]
    sent = comm.send(tag + "_gu", {tag + "_w_gu": _matmul(h, dgu, "tn", BF16, tag + "_d_w_gu")})
    sent = sent + comm.send(tag + "_down", {tag + "_w_down": _matmul(act, dy + sent.astype(BF16), "tn", BF16,
                                                                    tag + "_d_w_down"), **(more or {})})
    dh = _matmul(dgu, w_gu, "nt", F32, tag + "_dh")
    return dh, sent


def _expanders():
    e_g = np.zeros((LANES, HW), np.float32)
    e_b = np.zeros((LANES, HW), np.float32)
    for h in range(HEADS):
        e_g[h, h * HEAD_DIM:(h + 1) * HEAD_DIM] = 1.0
        e_b[HEADS + h, h * HEAD_DIM:(h + 1) * HEAD_DIM] = 1.0
    return jnp.asarray(e_g), jnp.asarray(e_b)


def _pad_lanes(v):
    return jnp.pad(v, ((0, 0), (0, LANES - v.shape[1])))


class _LocalWeights:
    def __init__(self, big):
        self.big, self.sent = big, {}

    def arrive(self, group, after):
        return self.big

    def send(self, group, grads):
        self.sent.update(grads)
        return jnp.zeros((), F32)


def _local_step(x, p, tgt, small, comm):
    big = comm.arrive("ffn1", None)
    e_g, e_b = _expanders()
    alog, dtb = _pad_lanes(small["a_log"]), _pad_lanes(small["dt_bias"])
    conv_w = jnp.pad(small["conv_w"], ((0, SUBLANES - CONV_K), (0, 0)))
    rel = small["rel_bias"]

    h1 = _norm_fwd(x, small["ffn1_norm"], "ffn1_norm")
    gu1, act1, y1 = _ffn_fwd(h1, big["ffn1_w_gu"], big["ffn1_w_down"], "ffn1")
    x1, h2 = _residual_norm_fwd(x, y1, 0.5, small["mix_norm"], "mix_norm")

    big = {**big, **comm.arrive("mixer", h2)}
    w_in = big["w_in"]
    w_qz = w_in[:, :IN_QZ]
    w_ab = jnp.pad(w_in[:, IN_AB0:IN_QKVB0], ((0, 0), (0, LANES - 2 * HEADS)))
    w_qkvb = w_in[:, IN_QKVB0:IN_GG0]
    w_gg = w_in[:, IN_GG0:IN_COLS]
    qz = _matmul(h2, w_qz, "nn", F32, "in_qz")
    ab = _matmul(h2, w_ab, "nn", F32, "in_ab")
    pb = _matmul(h2, w_qkvb, "nn", F32, "in_qkvb")
    gg = _matmul(h2, w_gg, "nn", BF16, "in_gates")
    pa, z = qz[:, :3 * HW], qz[:, 3 * HW:]

    def prep(i, n, pa_, prev_, ab_, cw_, alog_, dtb_, eg_, eb_):
        q, k, v = _gdn_post(_conv(pa_, prev_, cw_, i))
        g_b, beta_b = _gdn_gates(ab_, alog_[...], dtb_[...], eg_[...], eb_[...])
        return q, k, v, g_b, beta_b

    qn, kn, vv, g_b, beta_b = _rows(prep, [(pa, "t"), (pa, "p"), (ab, "t")], [conv_w, alog, dtb, e_g, e_b],
                                    [(HW, F32)] * 5, [], 256, "gdn_prep")
    u, w, aqk, qd, kt, tl = _gdn_intra(qn, kn, vv, g_b, beta_b)
    o, states = _gdn_scan(u, w, aqk, qd, kt, tl)
    ya = _rows(lambda i, n, o_, z_, w_: (_gated_norm(o_, z_, w_[...]),), [(o, "t"), (z, "t")], [small["gdn_norm"]],
               [(HW, BF16)], [], 512, "gdn_gated_norm")[0]

    pbp = jnp.pad(pb, ((ATT_PAD, 0), (0, 0)))
    yb = _attention(pb, pbp, small["q_norm"], small["k_norm"], rel)

    ta = _matmul(ya, big["w_branch_a"], "nn", BF16, "branch_a")
    tb = _matmul(yb, big["w_branch_b"], "nn", BF16, "branch_b")
    mixed = _rows(lambda i, n, gg_, ta_, tb_: (_mix(gg_, ta_, tb_),), [(gg, "t"), (ta, "t"), (tb, "t")], [],
                  [(D_MODEL, BF16)], [], 256, "mix")[0]
    m_out = _matmul(mixed, big["w_out"], "nn", F32, "w_out")
    x2, h3 = _residual_norm_fwd(x1, m_out, 1.0, small["ffn2_norm"], "ffn2_norm")
    big = {**big, **comm.arrive("tail", h3)}
    gu2, act2, y2 = _ffn_fwd(h3, big["ffn2_w_gu"], big["ffn2_w_down"], "ffn2")
    x3, h4 = _residual_norm_fwd(x2, y2, 0.5, small["ple_norm"], "ple_norm")
    gp = _matmul(h4, big["ple_gate"], "nn", BF16, "ple_gate")
    pp = _matmul(p, big["ple_proj"], "nn", BF16, "ple_proj")

    def head(i, n, x3_, gp_, pp_, tgt_):
        sg = _sigmoid(gp_)
        err = x3_ + sg * pp_ - tgt_
        dx4 = err * (1.0 / D_MODEL)
        sq = _colsum(err * err)
        part = sq[:, :LANES]
        for j in range(1, D_MODEL // LANES):
            part = part + sq[:, j * LANES:(j + 1) * LANES]
        return dx4, dx4 * pp_ * sg * (1.0 - sg), dx4 * sg, (0.5 / D_MODEL) * part

    dx4, dgp, dpp, loss_lanes = _rows(head, [(x3, "t"), (gp, "t"), (pp, "t"), (tgt, "t")], [],
                                      [(D_MODEL, F32), (D_MODEL, BF16), (D_MODEL, BF16)], [(1, LANES)], 256,
                                      "ple_loss_head")
    loss = jnp.sum(loss_lanes)

    gbig, gsmall = {}, {}
    gbig["ple_proj"] = _matmul(p, dpp, "tn", BF16, "d_ple_proj")
    gbig["ple_gate"] = _matmul(h4, dgp, "tn", BF16, "d_ple_gate")
    dh4 = _matmul(dgp, big["ple_gate"], "nt", F32, "ple_gate_dh")
    dx3, dy2, gsmall["ple_norm"] = _residual_norm_bwd(x3, small["ple_norm"], [dh4], dx4, 0.5, "ple_norm_bwd")

    dh3, sent = _ffn_bwd(h3, gu2, act2, dy2, big["ffn2_w_gu"], big["ffn2_w_down"], "ffn2", comm,
                         {n: gbig[n] for n in ("ple_proj", "ple_gate")})
    dx2, dx2b, gsmall["ffn2_norm"] = _residual_norm_bwd(x2, small["ffn2_norm"] + sent, [dh3], dx3, 1.0,
                                                        "ffn2_norm_bwd")

    gbig["w_out"] = _matmul(mixed, dx2b, "tn", BF16, "d_w_out")
    dmixed = _matmul(dx2b, big["w_out"], "nt", BF16, "w_out_dx")

    def mix_bwd(i, n, gg_, ta_, tb_, dm_):
        _, vjp = jax.vjp(_mix, gg_, ta_, tb_)
        return vjp(dm_)

    dgg, dta, dtb_ = _rows(mix_bwd, [(gg, "t"), (ta, "t"), (tb, "t"), (dmixed, "t")], [],
                           [(2 * D_MODEL, BF16), (D_MODEL, BF16), (D_MODEL, BF16)], [], 256, "mix_bwd")
    gbig["w_branch_a"] = _matmul(ya, dta, "tn", BF16, "d_branch_a")
    gbig["w_branch_b"] = _matmul(yb, dtb_, "tn", BF16, "d_branch_b")
    dya = _matmul(dta, big["w_branch_a"], "nt", F32, "branch_a_dx")
    dyb = _matmul(dtb_, big["w_branch_b"], "nt", F32, "branch_b_dx")

    dq_b, dk_b, dv_b, gsmall["q_norm"], gsmall["k_norm"], gsmall["rel_bias"] = _attention_bwd(
        pb, pbp, small["q_norm"], small["k_norm"], rel, dyb)
    dpb = jnp.concatenate([dq_b, dk_b[ATT_PAD:].astype(BF16), dv_b[ATT_PAD:].astype(BF16)], axis=1)

    def gated_bwd(i, n, o_, z_, dya_, w_):
        _, vjp = jax.vjp(_gated_norm, o_, z_, w_[...])
        return vjp(dya_)

    do, dz, gsmall["gdn_norm"] = _rows(gated_bwd, [(o, "t"), (z, "t"), (dya, "t")], [small["gdn_norm"]],
                                       [(HW, F32), (HW, BF16)], [(1, HEAD_DIM)], 256, "gdn_gated_norm_bwd")
    du, dw, da, dqd, dkt, dtl = _gdn_scan_bwd(do, u, w, aqk, qd, kt, tl, states)
    dqn, dkn, dvv, dg_b, dbeta_b = _gdn_intra_bwd(qn, kn, vv, g_b, beta_b, du, dw, da, dqd, dkt, dtl)

    def prep_bwd(i, n, pa_, prev_, ab_, dq_, dk_, dv_, dg_, db_, cw_, alog_, dtb_, eg_, eb_):
        _, vjp = jax.vjp(_gdn_post, _conv(pa_, prev_, cw_, i))
        (dy,) = vjp((dq_, dk_, dv_))
        e_g_, e_b_ = eg_[...], eb_[...]
        _, vjp_g = jax.vjp(lambda a, b, c: _gdn_gates(a, b, c, e_g_, e_b_), ab_, alog_[...], dtb_[...])
        dab, dalog, ddtb = vjp_g((dg_, db_))
        return dy, dab, dalog, ddtb

    dy_conv, dab, dalog, ddtb = _rows(
        prep_bwd, [(pa, "t"), (pa, "p"), (ab, "t"), (dqn, "t"), (dkn, "t"), (dvv, "t"), (dg_b, "t"), (dbeta_b, "t")],
        [conv_w, alog, dtb, e_g, e_b], [(3 * HW, F32), (LANES, BF16)], [(1, LANES), (1, LANES)], 256,
        "gdn_prep_bwd")
    gsmall["a_log"] = dalog[:, :HEADS]
    gsmall["dt_bias"] = ddtb[:, :HEADS]

    def conv_bwd(i, n, dy_, nxt_, pa_, prev_, cw_):
        dpa = dy_ * cw_[CONV_K - 1:CONV_K, :]
        row = lax.broadcasted_iota(jnp.int32, (SUBLANES, dy_.shape[1]), 0)
        dcw = jnp.where(row == CONV_K - 1, _colsum(dy_ * pa_), 0.0)
        for j in range(CONV_K - 1):
            s = CONV_K - 1 - j
            dpa = dpa + _shift_up(dy_, nxt_, s, i, n) * cw_[j:j + 1, :]
            dcw = dcw + jnp.where(row == j, _colsum(dy_ * _shift_down(pa_, prev_, s, i)), 0.0)
        return dpa, dcw

    dpa, dcw = _rows(conv_bwd, [(dy_conv, "t"), (dy_conv, "n"), (pa, "t"), (pa, "p")], [conv_w],
                     [(3 * HW, BF16)], [(SUBLANES, 3 * HW)], 256, "gdn_conv_bwd")
    gsmall["conv_w"] = dcw[:CONV_K]

    dqz = jnp.concatenate([dpa, dz], axis=1)
    d_w_qz = _matmul(h2, dqz, "tn", BF16, "d_in_qz")
    d_w_ab = _matmul(h2, dab, "tn", BF16, "d_in_ab")
    d_w_qkvb = _matmul(h2, dpb, "tn", BF16, "d_in_qkvb")
    d_w_gg = _matmul(h2, dgg, "tn", BF16, "d_in_gates")
    gbig["w_in"] = jnp.concatenate([d_w_qz, d_w_ab[:, :2 * HEADS], d_w_qkvb, d_w_gg,
                                    jnp.zeros((D_MODEL, IN_PADDED - IN_COLS), BF16)], axis=1)
    dh2 = [_matmul(dqz, w_qz, "nt", F32, "in_qz_dh"), _matmul(dab, w_ab, "nt", F32, "in_ab_dh"),
           _matmul(dpb, w_qkvb, "nt", F32, "in_qkvb_dh"), _matmul(dgg, w_gg, "nt", F32, "in_gates_dh")]
    sent = comm.send("mixer", {n: gbig[n] for n in ("w_out", "w_branch_b", "w_branch_a", "w_in")})
    dx1, dy1, gsmall["mix_norm"] = _residual_norm_bwd(x1, small["mix_norm"] + sent, dh2, dx2, 0.5, "mix_norm_bwd")

    dh1, sent = _ffn_bwd(h1, gu1, act1, dy1, big["ffn1_w_gu"], big["ffn1_w_down"], "ffn1", comm)
    grad_x, _, gsmall["ffn1_norm"] = _residual_norm_bwd(x, small["ffn1_norm"] + sent, [dh1], dx1, 1.0,
                                                        "ffn1_norm_bwd")
    return loss, grad_x, gsmall


GATHER_GROUPS = {"ffn1": ("ffn1_w_gu", "ffn1_w_down"),
                 "mixer": ("w_in_main", "w_in_edge", "w_branch_a", "w_branch_b", "w_out"),
                 "tail": ("ffn2_w_gu", "ffn2_w_down", "ple_gate", "ple_proj")}


def _kind(name):
    return "cols" if name in COL_SHARDED or name.startswith("w_in_") else "rows"


def _merge_w_in(main, edges):
    edge_w = WIN_W - WIN_STEP
    w_in = jnp.pad(main, ((0, 0), (0, edge_w)))
    for d in range(N_DEV):
        at = WIN_STEP * (d + 1)
        w_in = w_in + jnp.pad(edges[:, d * edge_w:(d + 1) * edge_w], ((0, 0), (at, IN_PADDED - at - edge_w)))
    return w_in


class _Fsdp:
    def __init__(self, wts):
        main, edge = _roll_w_in(jnp.pad(wts["w_in"], ((0, 0), (0, WIN_W - IN_SHARD))))
        self.shards = {n: wts[n].astype(BF16) for n in BIG if n != "w_in"}
        self.shards.update(w_in_main=main, w_in_edge=edge)
        names = GATHER_GROUPS["ffn1"]
        self.first = dict(zip(names, _all_gather([self.shards[n] for n in names], [_kind(n) for n in names],
                                                 len(names))))
        self.flight, token = {}, self.first["ffn1_w_down"][0, 0].astype(F32) * 0.0
        for group in ("mixer", "tail"):
            names = GATHER_GROUPS[group]
            srcs = [self.shards[n] for n in names]
            lands = [_place_block(self.shards[n], _kind(n), "own_" + n) for n in names]
            make = _gather_copies([s.shape for s in srcs], [_kind(n) for n in names])
            srcs[0] = srcs[0] + token.astype(BF16)
            send_sems, recv_sems, srcs, lands, tok = _split_start(srcs, lands, make, "gather_start_" + group)
            token = token + tok[0, 0]
            self.flight[group] = (send_sems, recv_sems, srcs, lands, make)
        self.token = token
        self.sent = {}

    def arrive(self, group, after):
        if group == "ffn1":
            return self.first
        send_sems, recv_sems, srcs, lands, make = self.flight[group]
        _, lands = _split_wait(send_sems, recv_sems, srcs, lands, after, make, "gather_wait_" + group)
        full = dict(zip(GATHER_GROUPS[group], lands))
        if group == "mixer":
            full["w_in"] = _merge_w_in(full.pop("w_in_main"), full.pop("w_in_edge"))
        return full

    def send(self, group, grads):
        names = list(grads)
        kinds = ["win" if n == "w_in" else _kind(n) for n in names]
        shapes = [(D_MODEL, WIN_W) if n == "w_in" else self.shards[n].shape for n in names]
        srcs = [grads[n] for n in names]
        lands = [lax.empty((N_DEV,) + tuple(s), BF16) for s in shapes]
        make = _exchange_copies(shapes, kinds)
        send_sems, recv_sems, srcs, lands, tok = _split_start(srcs, lands, make, "grads_start_" + group)
        self.sent[group] = (names, kinds, send_sems, recv_sems, srcs, lands, make)
        return tok[0, 0]

    def received(self, group, after):
        names, kinds, send_sems, recv_sems, srcs, lands, make = self.sent[group]
        srcs, lands = _split_wait(send_sems, recv_sems, srcs, lands, after, make, "grads_wait_" + group)
        return {n: (k, g, r) for n, k, g, r in zip(names, kinds, srcs, lands)}


SMALL_ROWS = ("ffn1_norm", "mix_norm", "ffn2_norm", "ple_norm", "gdn_norm", "q_norm", "k_norm", "a_log", "dt_bias",
              "rel_bias", "conv_w")


def _pack_small(vals):
    rows = []
    for n in SMALL_ROWS:
        v = vals[n]
        if n == "rel_bias":
            v = jnp.pad(v, ((0, 0), (0, 2 * LANES - N_REL)))
        elif n in ("a_log", "dt_bias"):
            v = _pad_lanes(v)
        rows.append(v.reshape(-1, LANES))
    packed = jnp.concatenate(rows, axis=0)
    return jnp.pad(packed, ((0, -packed.shape[0] % SUBLANES), (0, 0)))


def _unpack_small(packed, shapes):
    out, off = {}, 0
    for n in SMALL_ROWS:
        shp = shapes[n]
        if n == "rel_bias":
            out[n] = packed[off:off + 2 * HEADS].reshape(HEADS, 2 * LANES)[:, :N_REL]
            off += 2 * HEADS
        elif n in ("a_log", "dt_bias"):
            out[n] = packed[off:off + 1, :HEADS]
            off += 1
        else:
            r = int(np.prod(shp)) // LANES
            out[n] = packed[off:off + r].reshape(shp)
            off += r
    return out


WEIGHTS = ("ffn1_norm", "ffn1_w_gu", "ffn1_w_down", "mix_norm", "w_in", "conv_w", "a_log", "dt_bias", "gdn_norm",
           "q_norm", "k_norm", "rel_bias", "w_branch_a", "w_branch_b", "w_out", "ffn2_norm", "ffn2_w_gu",
           "ffn2_w_down", "ple_norm", "ple_gate", "ple_proj")


def kernel(x, p, ffn1_norm, ffn1_w_gu, ffn1_w_down, mix_norm, w_in, conv_w, a_log, dt_bias, gdn_norm, q_norm, k_norm, rel_bias, w_branch_a, w_branch_b, w_out, ffn2_norm, ffn2_w_gu, ffn2_w_down, ple_norm, ple_gate, ple_proj, loss_target, m_ffn1_norm, m_ffn1_w_gu, m_ffn1_w_down, m_mix_norm, m_w_in, m_conv_w, m_a_log, m_dt_bias, m_gdn_norm, m_q_norm, m_k_norm, m_rel_bias, m_w_branch_a, m_w_branch_b, m_w_out, m_ffn2_norm, m_ffn2_w_gu, m_ffn2_w_down, m_ple_norm, m_ple_gate, m_ple_proj, v_ffn1_norm, v_ffn1_w_gu, v_ffn1_w_down, v_mix_norm, v_w_in, v_conv_w, v_a_log, v_dt_bias, v_gdn_norm, v_q_norm, v_k_norm, v_rel_bias, v_w_branch_a, v_w_branch_b, v_w_out, v_ffn2_norm, v_ffn2_w_gu, v_ffn2_w_down, v_ple_norm, v_ple_gate, v_ple_proj):
    args = dict(locals())
    def layer0(v):
        return v[0] if v.ndim == 3 else v

    wts = {n: layer0(args[n]) for n in WEIGHTS}
    mom = {n: layer0(args["m_" + n]) for n in WEIGHTS}
    var = {n: layer0(args["v_" + n]) for n in WEIGHTS}
    x2d, p2d, tgt = x[0], p[0, 0], loss_target[0]
    my_index = _index(*_me())

    fsdp = _Fsdp(wts)

    small = {n: wts[n] for n in SMALL_ROWS if n != "conv_w"}
    small["ffn1_norm"] = small["ffn1_norm"] + fsdp.token
    conv_shard = wts["conv_w"]
    conv_cols = conv_shard.shape[1]
    conv_packed = jnp.zeros((SUBLANES, N_DEV * conv_cols), F32)
    conv_packed = lax.dynamic_update_slice(conv_packed, jnp.pad(conv_shard, ((0, SUBLANES - CONV_K), (0, 0))),
                                           (0, my_index * conv_cols))
    small["conv_w"] = _all_reduce_small(conv_packed.reshape(-1, LANES), "conv_w_gather").reshape(SUBLANES, -1)[:CONV_K]

    loss, grad_x, gsmall = _local_step(x2d, p2d, tgt, small, fsdp)
    loss = lax.psum(loss, ("x", "y", "c"))

    outs_big, after = {}, grad_x
    for group in list(fsdp.sent):
        for n, (kind, grad, recv) in fsdp.received(group, after).items():
            if n == "w_in":
                g_in = _sum_w_in_windows(recv, grad)[:, :IN_SHARD]
                outs_big[n] = [g_in] + list(_adamw_small(wts[n], g_in, mom[n], var[n], "adamw_w_in"))
            else:
                outs_big[n] = _adamw_recv(recv, grad, kind, wts[n], mom[n], var[n], "adamw_" + n)
            after = outs_big[n][1]

    small_shapes = {n: (small[n].shape if n != "conv_w" else (CONV_K, N_DEV * conv_cols)) for n in SMALL_ROWS}
    gsum = _unpack_small(_all_reduce_small(_pack_small(gsmall), "small_grads_all_reduce"), small_shapes)
    gsum["conv_w"] = lax.dynamic_slice(gsum["conv_w"], (0, my_index * conv_cols), (CONV_K, conv_cols))
    rep = [n for n in SMALL_ROWS if n != "conv_w"]
    rep_shapes = {n: small_shapes[n] for n in rep}

    def pack_rep(vals):
        return _pack_small({**{n: vals[n] for n in rep}, "conv_w": jnp.zeros((CONV_K, LANES), F32)})

    def unpack_rep(packed):
        return _unpack_small(packed, {**rep_shapes, "conv_w": (CONV_K, LANES)})

    outs_small = [unpack_rep(o) for o in _adamw_small(pack_rep(wts), pack_rep(gsum), pack_rep(mom), pack_rep(var),
                                                      "adamw_replicated")]
    pad8 = functools.partial(jnp.pad, pad_width=((0, SUBLANES - CONV_K), (0, 0)))
    outs_conv = [o[:CONV_K] for o in _adamw_small(pad8(conv_shard), pad8(gsum["conv_w"]), pad8(mom["conv_w"]),
                                                   pad8(var["conv_w"]), "adamw_conv")]

    def leaf(kind, n):
        if n in BIG:
            return outs_big[n][kind][None]
        if n == "conv_w":
            return (gsum["conv_w"] if kind == 0 else outs_conv[kind - 1])[None]
        return (gsum[n] if kind == 0 else outs_small[kind - 1][n]).reshape(args[n].shape)

    result = [loss, grad_x[None]]
    for kind in range(4):
        result += [leaf(kind, n) for n in WEIGHTS]
    return tuple(result)
```

```python
import functools

import numpy as np
import jax
import jax.numpy as jnp
from jax import lax
from jax.experimental import pallas as pl
from jax.experimental.pallas import tpu as pltpu

F32 = jnp.float32
BF16 = jnp.bfloat16
HIGHEST = lax.Precision.HIGHEST
MESH = pl.DeviceIdType.MESH

D_MODEL = 2048
D_FF = 5632
HEADS = 8
HEAD_DIM = 128
HW = HEADS * HEAD_DIM
CHUNK = 64
LEFT_CHUNKS = 8
MAX_REL = 128
N_REL = (CHUNK - 1) + MAX_REL + 1
CONV_K = 4
EPS = 1e-6
NEG_INF = -1e30
N_DEV = 8
LANES = 128
SUBLANES = 8
VMEM_LIMIT = 56 * 1024 * 1024

MATMUL_WHOLE_K = 2048

ATT_QB = 256
ATT_KW = ATT_QB + LEFT_CHUNKS * CHUNK
ATT_PAD = LEFT_CHUNKS * CHUNK
GDN_CB = 8
GDN_GROUP = 8

ADAM_LR = 0.001
ADAM_B1 = 0.9
ADAM_B2 = 0.999
ADAM_EPS = 1e-08
ADAM_WD = 0.01
ADAM_STEP = 10

IN_QZ = 3 * HW + HW
IN_AB0 = IN_QZ
IN_QKVB0 = IN_AB0 + 2 * HEADS
IN_GG0 = IN_QKVB0 + 3 * HW
IN_COLS = IN_GG0 + 2 * D_MODEL

BIG = ("ffn1_w_gu", "ffn1_w_down", "w_in", "w_branch_a", "w_branch_b", "w_out",
       "ffn2_w_gu", "ffn2_w_down", "ple_gate", "ple_proj")
COL_SHARDED = ("ffn1_w_gu", "w_in", "w_branch_a", "w_branch_b", "ffn2_w_gu", "ple_proj")


def _params(semantics=None, **kw):
    return pltpu.CompilerParams(dimension_semantics=semantics, vmem_limit_bytes=VMEM_LIMIT, **kw)


def _pick(n, cands):
    for c in cands:
        if n % c == 0:
            return c
    return n


def _matmul(a, b, mode, out_dtype, name):
    if mode == "nn":
        (m, k), (k2, n) = a.shape, b.shape
    elif mode == "nt":
        (m, k), (n, k2) = a.shape, b.shape
    else:
        (k, m), (k2, n) = a.shape, b.shape
    assert k == k2, (a.shape, b.shape, mode)
    tm = _pick(m, (1024, 512, 256, 128))
    tn = _pick(n, (1024, 512, 256, 128))
    tk = k if k <= MATMUL_WHOLE_K else _pick(k, (2816, 2048, 1536, 1024, 512, 256, 128))
    nk = k // tk
    if mode == "nn":
        a_spec = pl.BlockSpec((tm, tk), lambda i, j, kk: (i, kk))
        b_spec = pl.BlockSpec((tk, tn), lambda i, j, kk: (kk, j))
        dims = (((1,), (0,)), ((), ()))
    elif mode == "nt":
        a_spec = pl.BlockSpec((tm, tk), lambda i, j, kk: (i, kk))
        b_spec = pl.BlockSpec((tn, tk), lambda i, j, kk: (j, kk))
        dims = (((1,), (1,)), ((), ()))
    else:
        a_spec = pl.BlockSpec((tk, tm), lambda i, j, kk: (kk, i))
        b_spec = pl.BlockSpec((tk, tn), lambda i, j, kk: (kk, j))
        dims = (((0,), (0,)), ((), ()))

    def body(a_ref, b_ref, o_ref, *acc):
        prod = lax.dot_general(a_ref[...].astype(BF16), b_ref[...].astype(BF16), dims, preferred_element_type=F32)
        if nk == 1:
            o_ref[...] = prod.astype(o_ref.dtype)
            return
        acc_ref, kk = acc[0], pl.program_id(2)

        @pl.when(kk == 0)
        def _():
            acc_ref[...] = prod

        @pl.when((kk > 0) & (kk < nk - 1))
        def _():
            acc_ref[...] += prod

        @pl.when(kk == nk - 1)
        def _():
            o_ref[...] = (acc_ref[...] + prod).astype(o_ref.dtype)

    return pl.pallas_call(
        body, name=name,
        out_shape=jax.ShapeDtypeStruct((m, n), out_dtype),
        grid=(m // tm, n // tn, nk),
        in_specs=[a_spec, b_spec],
        out_specs=pl.BlockSpec((tm, tn), lambda i, j, kk: (i, j)),
        scratch_shapes=[pltpu.VMEM((tm, tn), F32)] if nk > 1 else [],
        compiler_params=_params(("parallel", "parallel", "arbitrary")),
    )(a, b)


def _rows(fn, row_ins, consts, row_outs, acc_outs, tile, name):
    t_rows = row_ins[0][0].shape[0]
    tile = min(tile, t_rows)
    assert t_rows % tile == 0 and tile % SUBLANES == 0
    n = t_rows // tile
    per = tile // SUBLANES
    last8 = t_rows // SUBLANES - 1
    in_specs = []
    for arr, kind in row_ins:
        c = arr.shape[1]
        if kind == "t":
            in_specs.append(pl.BlockSpec((tile, c), lambda i: (i, 0)))
        elif kind == "p":
            in_specs.append(pl.BlockSpec((SUBLANES, c), lambda i: (jnp.maximum(i * per - 1, 0), 0)))
        else:
            in_specs.append(pl.BlockSpec((SUBLANES, c), lambda i: (jnp.minimum((i + 1) * per, last8), 0)))
    for arr in consts:
        in_specs.append(pl.BlockSpec(arr.shape, lambda i, nd=arr.ndim: (0,) * nd))
    out_shape = [jax.ShapeDtypeStruct((t_rows, c), dt) for c, dt in row_outs]
    out_specs = [pl.BlockSpec((tile, c), lambda i: (i, 0)) for c, _ in row_outs]
    for shp in acc_outs:
        out_shape.append(jax.ShapeDtypeStruct(shp, F32))
        out_specs.append(pl.BlockSpec(shp, lambda i, nd=len(shp): (0,) * nd))
    n_in = len(row_ins) + len(consts)
    n_row_out = len(row_outs)

    def body(*refs):
        i = pl.program_id(0)
        vals = [r[...].astype(F32) for r in refs[:len(row_ins)]]
        res = fn(i, n, *vals, *refs[len(row_ins):n_in])
        outs = refs[n_in:]
        for r, v in zip(outs[:n_row_out], res[:n_row_out]):
            r[...] = v.astype(r.dtype)
        if acc_outs:
            @pl.when(i == 0)
            def _():
                for r in outs[n_row_out:]:
                    r[...] = jnp.zeros_like(r)

            for r, v in zip(outs[n_row_out:], res[n_row_out:]):
                r[...] += v

    res = pl.pallas_call(
        body, name=name, out_shape=out_shape, grid=(n,), in_specs=in_specs, out_specs=out_specs,
        compiler_params=_params(("arbitrary",) if acc_outs else ("parallel",)),
    )(*[a for a, _ in row_ins], *consts)
    return res


def _rms(x, w):
    return x * lax.rsqrt(jnp.mean(x * x, axis=-1, keepdims=True) + EPS) * w


def _l2n(x):
    return x * lax.rsqrt(jnp.sum(x * x, axis=-1, keepdims=True) + EPS)


def _sigmoid(x):
    return 1.0 / (1.0 + jnp.exp(-x))


def _silu(x):
    return x * _sigmoid(x)


def _softplus(x):
    return jnp.maximum(x, 0.0) + jnp.log(1.0 + jnp.exp(-jnp.abs(x)))


def _heads(fn, *xs):
    nh = xs[0].shape[1] // HEAD_DIM
    return jnp.concatenate(
        [fn(*[x[:, h * HEAD_DIM:(h + 1) * HEAD_DIM] for x in xs]) for h in range(nh)], axis=1)


def _colsum(x):
    return jnp.sum(x, axis=0, keepdims=True)


def _swiglu(gu):
    return _silu(gu[:, :D_FF]) * gu[:, D_FF:]


def _gated_norm(o, z, w):
    return _heads(lambda oh, zh: _rms(oh, w) * _silu(zh), o, z)


def _mix(gg, ta, tb):
    return _sigmoid(gg[:, :D_MODEL]) * ta + _sigmoid(gg[:, D_MODEL:]) * tb


def _gdn_post(y):
    a = _silu(y)
    q = _heads(lambda v: _l2n(v) * (HEAD_DIM ** -0.5), a[:, :HW])
    k = _heads(_l2n, a[:, HW:2 * HW])
    return q, k, a[:, 2 * HW:]


NN = (((1,), (0,)), ((), ()))
NT = (((1,), (1,)), ((), ()))
TN = (((0,), (0,)), ((), ()))


def _dg(a, b, dims):
    return lax.dot_general(a, b, dims, preferred_element_type=F32)


def _split2(x):
    hi = x.astype(BF16)
    return hi, (x - hi.astype(F32)).astype(BF16)


def _split3(x):
    hi = x.astype(BF16)
    r = x - hi.astype(F32)
    mid = r.astype(BF16)
    return hi, mid, (r - mid.astype(F32)).astype(BF16)


def _dg3(a, b, dims):
    ah, al = _split2(a)
    bh, bl = _split2(b)
    return _dg(ah, bh, dims) + (_dg(ah, bl, dims) + _dg(al, bh, dims))


BNN = (((2,), (1,)), ((0,), (0,)))
BNT = (((2,), (2,)), ((0,), (0,)))
BTN = (((1,), (1,)), ((0,), (0,)))


@jax.custom_vjp
def _mm3(a, b):
    return _dg3(a, b, BNN)


_mm3.defvjp(lambda a, b: (_dg3(a, b, BNN), (a, b)),
            lambda res, g: (_dg3(g, res[1], BNT), _dg3(res[0], g, BTN)))


def _xm(x, m, dims):
    mb = m.astype(BF16)
    parts = _split3(x)
    return _dg(parts[0], mb, dims) + (_dg(parts[1], mb, dims) + _dg(parts[2], mb, dims))


def _mx(m, x, dims):
    mb = m.astype(BF16)
    parts = _split3(x)
    return _dg(mb, parts[0], dims) + (_dg(mb, parts[1], dims) + _dg(mb, parts[2], dims))


@jax.custom_vjp
def _times_const(x, m):
    return _xm(x, m, NN)


_times_const.defvjp(lambda x, m: (_xm(x, m, NN), m),
                    lambda m, g: (_xm(g, m, NT), jnp.zeros_like(m)))


@jax.custom_vjp
def _const_times(m, x):
    return _mx(m, x, NN)


_const_times.defvjp(lambda m, x: (_mx(m, x, NN), m),
                    lambda m, g: (jnp.zeros_like(m), _mx(m, g, TN)))


@jax.custom_vjp
def _lane_mean_cols(x, avg):
    return _mx(avg, x, BNT)


_lane_mean_cols.defvjp(lambda x, avg: (_mx(avg, x, BNT), avg),
                       lambda avg, g: (_xm(g, avg, BTN), jnp.zeros_like(avg)))


def _gdn_gates(ab, alog, dtb, e_g, e_b):
    t = ab.shape[0]
    g = -jnp.exp(alog) * _softplus(ab + dtb)
    beta = _sigmoid(ab)
    ri = lax.broadcasted_iota(jnp.int32, (t, t), 0)
    ci = lax.broadcasted_iota(jnp.int32, (t, t), 1)
    shift = CHUNK.bit_length() - 1
    same = jnp.right_shift(ri, shift) == jnp.right_shift(ci, shift)
    tril = jnp.where(same & (ri >= ci), 1.0, 0.0).astype(F32)
    gc = _const_times(tril, g)
    return _times_const(gc, e_g), _times_const(beta, e_b)


def _shift_down(x, halo, s, i):
    if s == 0:
        return x
    halo = jnp.where(i == 0, 0.0, halo)
    xr = pltpu.roll(x, s, 0)
    hr = pltpu.roll(halo, s, 0)
    row = lax.broadcasted_iota(jnp.int32, (SUBLANES, x.shape[1]), 0)
    top = jnp.where(row < s, hr, xr[:SUBLANES])
    return jnp.concatenate([top, xr[SUBLANES:]], axis=0)


def _shift_up(x, halo, s, i, n):
    if s == 0:
        return x
    t = x.shape[0]
    halo = jnp.where(i == n - 1, 0.0, halo)
    xr = pltpu.roll(x, t - s, 0)
    hr = pltpu.roll(halo, SUBLANES - s, 0)
    row = lax.broadcasted_iota(jnp.int32, (SUBLANES, x.shape[1]), 0)
    bot = jnp.where(row >= SUBLANES - s, hr, xr[t - SUBLANES:])
    return jnp.concatenate([xr[:t - SUBLANES], bot], axis=0)


def _conv(pa, prev, cw_ref, i):
    y = pa * cw_ref[CONV_K - 1:CONV_K, :]
    for j in range(CONV_K - 1):
        y = y + _shift_down(pa, prev, CONV_K - 1 - j, i) * cw_ref[j:j + 1, :]
    return y


def _dot_nt(a, b, precision=None):
    return lax.dot_general(a, b, (((1,), (1,)), ((), ())), precision=precision, preferred_element_type=F32)


def _dot_tn(a, b, precision=None):
    return lax.dot_general(a, b, (((0,), (0,)), ((), ())), precision=precision, preferred_element_type=F32)


def _dot(a, b, precision=None):
    return jnp.dot(a, b, precision=precision, preferred_element_type=F32)


def _bf(x):
    return x.astype(BF16)


def _gdn_chunk(q, k, v, gc, bb):
    nb, c, _ = q.shape
    ri = lax.broadcasted_iota(jnp.int32, (nb, c, c), 1)
    ci = lax.broadcasted_iota(jnp.int32, (nb, c, c), 2)
    incl = ri >= ci
    strict = ri > ci
    g_row = gc[:, :, :c]
    g_col = _lane_mean_cols(gc, jnp.full((nb, c, LANES), 1.0 / LANES, F32))
    decay = jnp.where(incl, jnp.exp(jnp.where(incl, g_row - g_col, 0.0)), 0.0)
    kb = k * bb
    lmat = jnp.where(strict, _dg(_bf(kb), _bf(k), BNT) * decay, 0.0)
    eye = jnp.where(ri == ci, 1.0, 0.0).astype(F32)
    pw = -lmat
    inv = eye + pw
    for _ in range(5):
        pw = _mm3(pw, pw)
        inv = inv + _mm3(inv, pw)
    egc = jnp.exp(gc)
    u = _mm3(inv, v * bb)
    w = _mm3(inv, kb * egc)
    aqk = _dg(_bf(q), _bf(k), BNT) * decay
    last = lax.broadcasted_iota(jnp.int32, (nb, c, LANES), 1) == c - 1
    tot = jnp.sum(jnp.where(last, gc, 0.0), axis=1, keepdims=True)
    k_tail = k * jnp.exp(tot - gc)
    tail = jnp.broadcast_to(jnp.exp(tot), (nb, SUBLANES, LANES))
    return u, w, aqk, q * egc, k_tail, tail


def _gdn_intra(qn, kn, vv, g_b, beta_b):
    t_rows = qn.shape[0]
    nc = t_rows // CHUNK
    cb = min(GDN_CB, nc)
    rows = cb * CHUNK
    col = pl.BlockSpec((rows, HEAD_DIM), lambda h, b: (b, h))

    def body(q_ref, k_ref, v_ref, g_ref, b_ref, u_ref, w_ref, a_ref, qd_ref, kt_ref, tl_ref):
        def group(gi, carry):
            r = pl.ds(pl.multiple_of(gi * (grp * CHUNK), grp * CHUNK), grp * CHUNK)
            ins = [ref[r, :].reshape(grp, CHUNK, HEAD_DIM) for ref in (q_ref, k_ref, v_ref, g_ref, b_ref)]
            u, w, aqk, qd, kt, tl = _gdn_chunk(*ins)
            for ref, val in ((u_ref, u), (w_ref, w), (qd_ref, qd), (kt_ref, kt)):
                ref[r, :] = val.reshape(grp * CHUNK, HEAD_DIM)
            a_ref[0, r, :] = aqk.reshape(grp * CHUNK, CHUNK)
            tl_ref[0, pl.ds(gi * grp, grp)] = tl
            return carry

        grp = min(GDN_GROUP, cb)
        lax.fori_loop(0, cb // grp, group, 0)

    full = jax.ShapeDtypeStruct((t_rows, HW), F32)
    return pl.pallas_call(
        body, name="gdn_intra_fwd",
        out_shape=[full, full, jax.ShapeDtypeStruct((HEADS, t_rows, CHUNK), F32), full, full,
                   jax.ShapeDtypeStruct((HEADS, nc, SUBLANES, LANES), F32)],
        grid=(HEADS, nc // cb),
        in_specs=[col] * 5,
        out_specs=[col, col, pl.BlockSpec((1, rows, CHUNK), lambda h, b: (h, b, 0)), col, col,
                   pl.BlockSpec((1, cb, SUBLANES, LANES), lambda h, b: (h, b, 0, 0))],
        compiler_params=_params(("parallel", "parallel")),
    )(qn, kn, vv, g_b, beta_b)


def _gdn_intra_bwd(qn, kn, vv, g_b, beta_b, du, dw, da, dqd, dkt, dtl):
    t_rows = qn.shape[0]
    nc = t_rows // CHUNK
    cb = min(GDN_CB, nc)
    rows = cb * CHUNK
    col = pl.BlockSpec((rows, HEAD_DIM), lambda h, b: (b, h))
    a_spec = pl.BlockSpec((1, rows, CHUNK), lambda h, b: (h, b, 0))
    tl_spec = pl.BlockSpec((1, cb, SUBLANES, LANES), lambda h, b: (h, b, 0, 0))

    def body(q_ref, k_ref, v_ref, g_ref, b_ref, du_ref, dw_ref, da_ref, dqd_ref, dkt_ref, dtl_ref,
             dq_ref, dk_ref, dv_ref, dg_ref, db_ref):
        def group(gi, carry):
            r = pl.ds(pl.multiple_of(gi * (grp * CHUNK), grp * CHUNK), grp * CHUNK)
            wide = (grp, CHUNK, HEAD_DIM)
            ins = [ref[r, :].reshape(wide) for ref in (q_ref, k_ref, v_ref, g_ref, b_ref)]
            cts = (du_ref[r, :].reshape(wide), dw_ref[r, :].reshape(wide),
                   da_ref[0, r, :].reshape(grp, CHUNK, CHUNK), dqd_ref[r, :].reshape(wide),
                   dkt_ref[r, :].reshape(wide), dtl_ref[0, pl.ds(gi * grp, grp)])
            grads = jax.vjp(_gdn_chunk, *ins)[1](cts)
            for ref, val in zip((dq_ref, dk_ref, dv_ref, dg_ref, db_ref), grads):
                ref[r, :] = val.reshape(grp * CHUNK, HEAD_DIM)
            return carry

        grp = min(GDN_GROUP, cb)
        lax.fori_loop(0, cb // grp, group, 0)

    full = jax.ShapeDtypeStruct((t_rows, HW), F32)
    return pl.pallas_call(
        body, name="gdn_intra_bwd",
        out_shape=[full] * 5,
        grid=(HEADS, nc // cb),
        in_specs=[col] * 7 + [a_spec, col, col, tl_spec],
        out_specs=[col] * 5,
        compiler_params=_params(("parallel", "parallel")),
    )(qn, kn, vv, g_b, beta_b, du, dw, da, dqd, dkt, dtl)


def _head_cols(h):
    return slice(h * HEAD_DIM, (h + 1) * HEAD_DIM)


def _gdn_scan(u, w, aqk, qd, kt, tl):
    t_rows = u.shape[0]
    nc = t_rows // CHUNK
    cb = min(GDN_CB, nc)
    rows = cb * CHUNK
    wide = pl.BlockSpec((rows, HW), lambda b: (b, 0))

    def body(u_ref, w_ref, a_ref, qd_ref, kt_ref, tl_ref, o_ref, s_out_ref, s_ref):
        @pl.when(pl.program_id(0) == 0)
        def _():
            s_ref[...] = jnp.zeros_like(s_ref)

        def chunk(ci, carry):
            r = pl.ds(pl.multiple_of(ci * CHUNK, CHUNK), CHUNK)
            for h in range(HEADS):
                hc = _head_cols(h)
                s = s_ref[h]
                s_out_ref[ci, h] = s
                sb = _bf(s)
                vn = u_ref[r, hc] - _dot(_bf(w_ref[r, hc]), sb)
                vnb = _bf(vn)
                o_ref[r, hc] = _dot(_bf(qd_ref[r, hc]), sb) + _dot(_bf(a_ref[h, r, :]), vnb)
                s_ref[h] = s * tl_ref[h, ci, 0:1, :] + _dot_tn(_bf(kt_ref[r, hc]), vnb)
            return carry

        lax.fori_loop(0, cb, chunk, 0)

    return pl.pallas_call(
        body, name="gdn_scan_fwd",
        out_shape=[jax.ShapeDtypeStruct((t_rows, HW), F32),
                   jax.ShapeDtypeStruct((nc, HEADS, HEAD_DIM, HEAD_DIM), F32)],
        grid=(nc // cb,),
        in_specs=[wide, wide, pl.BlockSpec((HEADS, rows, CHUNK), lambda b: (0, b, 0)), wide, wide,
                  pl.BlockSpec((HEADS, cb, SUBLANES, LANES), lambda b: (0, b, 0, 0))],
        out_specs=[wide, pl.BlockSpec((cb, HEADS, HEAD_DIM, HEAD_DIM), lambda b: (b, 0, 0, 0))],
        scratch_shapes=[pltpu.VMEM((HEADS, HEAD_DIM, HEAD_DIM), F32)],
        compiler_params=_params(("arbitrary",)),
    )(u, w, aqk, qd, kt, tl)


def _gdn_scan_bwd(do, u, w, aqk, qd, kt, tl, states):
    t_rows = u.shape[0]
    nc = t_rows // CHUNK
    cb = min(GDN_CB, nc)
    rows = cb * CHUNK
    nb = nc // cb
    wide = pl.BlockSpec((rows, HW), lambda b: (nb - 1 - b, 0))
    a_spec = pl.BlockSpec((HEADS, rows, CHUNK), lambda b: (0, nb - 1 - b, 0))
    tl_spec = pl.BlockSpec((HEADS, cb, SUBLANES, LANES), lambda b: (0, nb - 1 - b, 0, 0))

    def body(do_ref, u_ref, w_ref, a_ref, qd_ref, kt_ref, tl_ref, s_in_ref,
             du_ref, dw_ref, da_ref, dqd_ref, dkt_ref, dtl_ref, ds_ref):
        @pl.when(pl.program_id(0) == 0)
        def _():
            ds_ref[...] = jnp.zeros_like(ds_ref)

        row0 = lax.broadcasted_iota(jnp.int32, (SUBLANES, LANES), 0) == 0

        def chunk(step, carry):
            ci = cb - 1 - step
            r = pl.ds(pl.multiple_of(ci * CHUNK, CHUNK), CHUNK)
            for h in range(HEADS):
                hc = _head_cols(h)
                s = s_in_ref[ci, h]
                ds_next = ds_ref[h]
                sb, dsb = _bf(s), _bf(ds_next)
                wb, ab, ktb, qdb = _bf(w_ref[r, hc]), _bf(a_ref[h, r, :]), _bf(kt_ref[r, hc]), _bf(qd_ref[r, hc])
                dob = _bf(do_ref[r, hc])
                vn = u_ref[r, hc] - _dot(wb, sb)
                vnb = _bf(vn)
                dvn = _dot_tn(ab, dob) + _dot(ktb, dsb)
                dvnb = _bf(dvn)
                du_ref[r, hc] = dvn
                dw_ref[r, hc] = -_dot_nt(dvnb, sb)
                da_ref[h, r, :] = _dot_nt(dob, vnb)
                dqd_ref[r, hc] = _dot_nt(dob, sb)
                dkt_ref[r, hc] = _dot_nt(vnb, dsb)
                dtl_ref[h, ci] = jnp.where(row0, _colsum(s * ds_next), 0.0)
                ds_ref[h] = _dot_tn(qdb, dob) + ds_next * tl_ref[h, ci, 0:1, :] - _dot_tn(wb, dvnb)
            return carry

        lax.fori_loop(0, cb, chunk, 0)

    full = jax.ShapeDtypeStruct((t_rows, HW), F32)
    return pl.pallas_call(
        body, name="gdn_scan_bwd",
        out_shape=[full, full, jax.ShapeDtypeStruct((HEADS, t_rows, CHUNK), F32), full, full,
                   jax.ShapeDtypeStruct((HEADS, nc, SUBLANES, LANES), F32)],
        grid=(nb,),
        in_specs=[wide, wide, wide, a_spec, wide, wide, tl_spec,
                  pl.BlockSpec((cb, HEADS, HEAD_DIM, HEAD_DIM), lambda b: (nb - 1 - b, 0, 0, 0))],
        out_specs=[wide, wide, a_spec, wide, wide, tl_spec],
        scratch_shapes=[pltpu.VMEM((HEADS, HEAD_DIM, HEAD_DIM), F32)],
        compiler_params=_params(("arbitrary",)),
    )(do, u, w, aqk, qd, kt, tl, states)


def _att_rel_index():
    qi = lax.broadcasted_iota(jnp.int32, (ATT_QB, ATT_KW), 0)
    kj = lax.broadcasted_iota(jnp.int32, (ATT_QB, ATT_KW), 1)
    return jnp.clip(qi - kj + ATT_PAD, -(CHUNK - 1), MAX_REL) + (CHUNK - 1)


def _att_in_band():
    qi = lax.broadcasted_iota(jnp.int32, (ATT_QB, ATT_KW), 0)
    kj = lax.broadcasted_iota(jnp.int32, (ATT_QB, ATT_KW), 1)
    shift = CHUNK.bit_length() - 1
    qc = jnp.right_shift(qi, shift)
    kc = jnp.right_shift(kj, shift) - LEFT_CHUNKS
    return (kc <= qc) & (kc >= qc - LEFT_CHUNKS)


def _att_valid(b):
    kj = lax.broadcasted_iota(jnp.int32, (1, ATT_KW), 1)
    return jnp.where(kj + b * ATT_QB >= ATT_PAD, 0.0, NEG_INF)


def _att_block(q_raw, k_raw, v, qw, kw, bias, before_start):
    q = _rms(q_raw, qw)
    k = _rms(k_raw, kw)
    s = _dot_nt(_bf(q), _bf(k)) * (HEAD_DIM ** -0.5) + (bias + before_start)
    p = jnp.exp(s - lax.stop_gradient(jnp.max(s, axis=-1, keepdims=True)))
    p = p * (1.0 / jnp.sum(p, axis=-1, keepdims=True))
    return _dot(_bf(p), _bf(v))


def _att_specs():
    q_spec = pl.BlockSpec((ATT_QB, HEAD_DIM), lambda h, b: (b, h))
    k_specs = [pl.BlockSpec((ATT_QB, HEAD_DIM), lambda h, b, j=j: (b + j, HEADS + h)) for j in range(3)]
    v_specs = [pl.BlockSpec((ATT_QB, HEAD_DIM), lambda h, b, j=j: (b + j, 2 * HEADS + h)) for j in range(3)]
    w_spec = pl.BlockSpec((1, HEAD_DIM), lambda h, b: (0, 0))
    smem = pl.BlockSpec(memory_space=pltpu.SMEM)
    return q_spec, k_specs, v_specs, w_spec, smem


def _att_fill_bias(bias_ref, rel_ref, h):
    idx = _att_rel_index()

    def fill(r, acc):
        return jnp.where(idx == r, rel_ref[h, r], acc)

    table = lax.fori_loop(0, N_REL, fill, jnp.zeros((ATT_QB, ATT_KW), F32))
    bias_ref[...] = jnp.where(_att_in_band(), table, NEG_INF)


def _attention(pb, pbp, qw, kw, rel):
    t_rows = pb.shape[0]
    q_spec, k_specs, v_specs, w_spec, smem = _att_specs()

    def body(q_ref, k0, k1, k2, v0, v1, v2, qw_ref, kw_ref, rel_ref, o_ref, bias_ref):
        h, b = pl.program_id(0), pl.program_id(1)

        @pl.when(b == 0)
        def _():
            _att_fill_bias(bias_ref, rel_ref, h)

        kwin = jnp.concatenate([k0[...], k1[...], k2[...]], axis=0)
        vwin = jnp.concatenate([v0[...], v1[...], v2[...]], axis=0)
        o = _att_block(q_ref[...], kwin, vwin, qw_ref[...], kw_ref[...], bias_ref[...], _att_valid(b))
        o_ref[...] = o.astype(o_ref.dtype)

    return pl.pallas_call(
        body, name="band_attention_fwd",
        out_shape=jax.ShapeDtypeStruct((t_rows, HW), BF16),
        grid=(HEADS, t_rows // ATT_QB),
        in_specs=[q_spec] + k_specs + v_specs + [w_spec, w_spec, smem],
        out_specs=pl.BlockSpec((ATT_QB, HEAD_DIM), lambda h, b: (b, h)),
        scratch_shapes=[pltpu.VMEM((ATT_QB, ATT_KW), F32)],
        compiler_params=_params(("arbitrary", "arbitrary")),
    )(pb, pbp, pbp, pbp, pbp, pbp, pbp, qw, kw, rel)


def _attention_bwd(pb, pbp, qw, kw, rel, dyb):
    t_rows = pb.shape[0]
    nb = t_rows // ATT_QB
    q_spec, k_specs, v_specs, w_spec, smem = _att_specs()
    pad_rows = t_rows + ATT_PAD
    acc_spec = pl.BlockSpec((pad_rows, HEAD_DIM), lambda h, b: (0, h))

    def body(q_ref, k0, k1, k2, v0, v1, v2, qw_ref, kw_ref, rel_ref, do_ref,
             dq_ref, dk_ref, dv_ref, dqw_ref, dkw_ref, drel_ref, bias_ref, dbias_ref):
        h, b = pl.program_id(0), pl.program_id(1)

        @pl.when(b == 0)
        def _():
            _att_fill_bias(bias_ref, rel_ref, h)
            dbias_ref[...] = jnp.zeros_like(dbias_ref)
            dk_ref[...] = jnp.zeros_like(dk_ref)
            dv_ref[...] = jnp.zeros_like(dv_ref)

        @pl.when((b == 0) & (h == 0))
        def _():
            dqw_ref[...] = jnp.zeros_like(dqw_ref)
            dkw_ref[...] = jnp.zeros_like(dkw_ref)

        kwin = jnp.concatenate([k0[...], k1[...], k2[...]], axis=0)
        vwin = jnp.concatenate([v0[...], v1[...], v2[...]], axis=0)
        valid = _att_valid(b)
        _, vjp = jax.vjp(lambda q, k, v, a, c, bias: _att_block(q, k, v, a, c, bias, valid),
                         q_ref[...], kwin, vwin, qw_ref[...], kw_ref[...], bias_ref[...])
        dq, dk, dv, dqw, dkw, dbias = vjp(do_ref[...])
        dq_ref[...] = dq.astype(dq_ref.dtype)
        win = pl.ds(pl.multiple_of(b * ATT_QB, ATT_QB), ATT_KW)
        dk_ref[win, :] += dk
        dv_ref[win, :] += dv
        dqw_ref[...] += dqw
        dkw_ref[...] += dkw
        dbias_ref[...] += dbias

        @pl.when(b == nb - 1)
        def _():
            idx = _att_rel_index()
            tot = dbias_ref[...]

            def reduce(r, carry):
                drel_ref[h, r] = jnp.sum(jnp.where(idx == r, tot, 0.0))
                return carry

            lax.fori_loop(0, N_REL, reduce, 0)

    return pl.pallas_call(
        body, name="band_attention_bwd",
        out_shape=[jax.ShapeDtypeStruct((t_rows, HW), BF16),
                   jax.ShapeDtypeStruct((pad_rows, HW), F32), jax.ShapeDtypeStruct((pad_rows, HW), F32),
                   jax.ShapeDtypeStruct((1, HEAD_DIM), F32), jax.ShapeDtypeStruct((1, HEAD_DIM), F32),
                   jax.ShapeDtypeStruct((HEADS, N_REL), F32)],
        grid=(HEADS, nb),
        in_specs=[q_spec] + k_specs + v_specs + [w_spec, w_spec, smem, q_spec],
        out_specs=[q_spec, acc_spec, acc_spec, w_spec, w_spec, smem],
        scratch_shapes=[pltpu.VMEM((ATT_QB, ATT_KW), F32), pltpu.VMEM((ATT_QB, ATT_KW), F32)],
        compiler_params=_params(("arbitrary", "arbitrary")),
    )(pb, pbp, pbp, pbp, pbp, pbp, pbp, qw, kw, rel, dyb)


def _me():
    return lax.axis_index("x"), lax.axis_index("y"), lax.axis_index("c")


def _index(x, y, c):
    return 4 * x + 2 * y + c


HBM_SPEC = pl.BlockSpec(memory_space=pl.ANY)


def _block(ref, kind, d, r, c):
    if kind == "rows":
        return ref.at[pl.ds(d * r, r), :]
    if kind == "win":
        return ref.at[:, pl.ds(d * WIN_STEP, c)]
    return ref.at[:, pl.ds(d * c, c)]


def _all_gather(shards, kinds, n_gather):
    n = len(shards)

    def body(*refs):
        x_refs, out_refs = refs[:n], refs[n:2 * n]
        send_sems, recv_sems, local_sems = refs[2 * n:]
        x, y, c = _me()
        me, sibling = (x, y, c), (x, y, 1 - c)
        chips = [(1 - x, y), (x, 1 - y), (1 - x, 1 - y)]

        def copy(i, k, blk, to, src=None):
            r_, c_ = shards[i].shape
            dst = _block(out_refs[i], kinds[i], _index(*blk), r_, c_)
            return pltpu.make_async_remote_copy(
                src_ref=dst if src is None else src, dst_ref=dst,
                send_sem=send_sems.at[i, k], recv_sem=recv_sems.at[i, k], device_id=to, device_id_type=MESH)

        sends, local = [], []
        for i in range(n):
            r_, c_ = shards[i].shape
            mine = pltpu.make_async_copy(x_refs[i], _block(out_refs[i], kinds[i], _index(*me), r_, c_),
                                         local_sems.at[i])
            mine.start()
            local.append(mine)
            if i >= n_gather:
                continue
            first = [copy(i, 0, me, sibling, src=x_refs[i])]
            first += [copy(i, 1 + j, me, (*chip, c), src=x_refs[i]) for j, chip in enumerate(chips)]
            for cp in first:
                cp.start()
            sends += first
        for i in range(n_gather):
            for j, chip in enumerate(chips):
                copy(i, 1 + j, (*chip, c), me).wait_recv()
                passed = copy(i, 4 + j, (*chip, c), sibling)
                passed.start()
                sends.append(passed)
        for i in range(n_gather):
            copy(i, 0, sibling, me).wait_recv()
            for j, chip in enumerate(chips):
                copy(i, 4 + j, (*chip, 1 - c), me).wait_recv()
        for cp in sends:
            cp.wait_send()
        for cp in local:
            cp.wait()

    def full_shape(s, kind):
        r_, c_ = s.shape
        return (N_DEV * r_, c_) if kind == "rows" else (r_, N_DEV * c_)

    return pl.pallas_call(
        body, name="weights_all_gather",
        out_shape=[jax.ShapeDtypeStruct(full_shape(s, k), s.dtype) for s, k in zip(shards, kinds)],
        in_specs=[HBM_SPEC] * n, out_specs=[HBM_SPEC] * n,
        scratch_shapes=[pltpu.SemaphoreType.DMA((n_gather, 7)), pltpu.SemaphoreType.DMA((n_gather, 7)),
                        pltpu.SemaphoreType.DMA((n,))],
        compiler_params=pltpu.CompilerParams(has_side_effects=True),
    )(*shards)


SEM_SPEC = pl.BlockSpec(memory_space=pltpu.SEMAPHORE)
HBM_ONLY = pl.BlockSpec(memory_space=pltpu.HBM)
DATAFLOW = pltpu.SideEffectType.DATAFLOW_SIDE_EFFECTING


def _peers():
    x, y, c = _me()
    return [(x ^ (k >> 2), y ^ ((k >> 1) & 1), c ^ (k & 1)) for k in range(1, N_DEV)]


def _gather_copies(shapes, kinds):
    def make(src_refs, land_refs, send_sems, recv_sems):
        mine = _index(*_me())
        return [pltpu.make_async_remote_copy(
            src_ref=src_refs[i], dst_ref=_block(land_refs[i], kind, mine, r, c),
            send_sem=send_sems.at[7 * i + k], recv_sem=recv_sems.at[7 * i + k], device_id=peer, device_id_type=MESH)
            for i, ((r, c), kind) in enumerate(zip(shapes, kinds)) for k, peer in enumerate(_peers())]

    return make


def _exchange_copies(shapes, kinds):
    def make(src_refs, land_refs, send_sems, recv_sems):
        mine = _index(*_me())
        return [pltpu.make_async_remote_copy(
            src_ref=_block(src_refs[i], kind, _index(*peer), r, c), dst_ref=land_refs[i].at[mine],
            send_sem=send_sems.at[7 * i + k], recv_sem=recv_sems.at[7 * i + k], device_id=peer, device_id_type=MESH)
            for i, ((r, c), kind) in enumerate(zip(shapes, kinds)) for k, peer in enumerate(_peers())]

    return make


def _place_block(shard, kind, name):
    r, c = shard.shape
    tile = _row_tile(r, c)
    nt = r // tile
    full = (N_DEV * r, c) if kind == "rows" else (r, N_DEV * c)

    def body(me_ref, x_ref, out_ref):
        out_ref[...] = x_ref[...]

    if kind == "rows":
        out_spec = pl.BlockSpec((tile, c), lambda i, me: (me[0] * nt + i, 0))
    else:
        out_spec = pl.BlockSpec((tile, c), lambda i, me: (i, me[0]))
    return pl.pallas_call(
        body, name=name, out_shape=jax.ShapeDtypeStruct(full, shard.dtype),
        grid_spec=pltpu.PrefetchScalarGridSpec(
            num_scalar_prefetch=1, grid=(nt,),
            in_specs=[pl.BlockSpec((tile, c), lambda i, me: (i, 0))], out_specs=out_spec),
        compiler_params=_params(("arbitrary",)),
    )(_my_index_operand(), shard)


def _split_start(srcs, lands, make, name):
    n = len(srcs)

    def body(*refs):
        send_sems, recv_sems = refs[2 * n], refs[2 * n + 1]
        for cp in make(refs[:n], refs[n:2 * n], send_sems, recv_sems):
            cp.start()
        refs[-1][...] = jnp.zeros_like(refs[-1])

    arrays = list(srcs) + list(lands)
    out = pl.pallas_call(
        body, name=name,
        out_shape=(pltpu.SemaphoreType.DMA((7 * n,)), pltpu.SemaphoreType.DMA((7 * n,)),
                   *[pltpu.HBM(a.shape, a.dtype) for a in arrays], jax.ShapeDtypeStruct((SUBLANES, LANES), F32)),
        in_specs=[HBM_ONLY] * (2 * n),
        out_specs=(SEM_SPEC, SEM_SPEC, *[HBM_ONLY] * (2 * n), pl.BlockSpec(memory_space=pltpu.VMEM)),
        input_output_aliases={i: 2 + i for i in range(2 * n)},
        compiler_params=pltpu.CompilerParams(has_side_effects=DATAFLOW),
    )(*[pltpu.with_memory_space_constraint(a, pltpu.HBM) for a in arrays])
    return out[0], out[1], list(out[2:2 + n]), list(out[2 + n:2 + 2 * n]), out[-1]


def _split_wait(send_sems, recv_sems, srcs, lands, after, make, name):
    n = len(srcs)

    def body(*refs):
        for cp in make(refs[:n], refs[n:2 * n], refs[2 * n], refs[2 * n + 1]):
            cp.wait_send()
            cp.wait_recv()

    arrays = list(srcs) + list(lands)
    out = pl.pallas_call(
        body, name=name,
        out_shape=tuple(pltpu.HBM(a.shape, a.dtype) for a in arrays),
        in_specs=[HBM_ONLY] * (2 * n) + [SEM_SPEC, SEM_SPEC, pl.BlockSpec(memory_space=pl.ANY)],
        out_specs=tuple([HBM_ONLY] * (2 * n)),
        input_output_aliases={i: i for i in range(2 * n)},
        compiler_params=pltpu.CompilerParams(has_side_effects=DATAFLOW),
    )(*arrays, send_sems, recv_sems, after)
    return list(out[:n]), list(out[n:])


def _all_reduce_small(vals, name):
    rows, width = vals.shape

    def body(x_ref, out_ref, buf_ref, send_sems, recv_sems):
        x, y, c = _me()
        mine = _index(x, y, c)
        buf_ref[mine] = x_ref[...]
        copies = []
        for k in range(1, N_DEV):
            px, py, pc = x ^ (k >> 2), y ^ ((k >> 1) & 1), c ^ (k & 1)
            copies.append(pltpu.make_async_remote_copy(
                src_ref=x_ref, dst_ref=buf_ref.at[mine],
                send_sem=send_sems.at[k - 1], recv_sem=recv_sems.at[k - 1],
                device_id=(px, py, pc), device_id_type=MESH))
        for cp in copies:
            cp.start()
        for cp in copies:
            cp.wait()
        acc = buf_ref[0]
        for j in range(1, N_DEV):
            acc = acc + buf_ref[j]
        out_ref[...] = acc

    vmem = pl.BlockSpec(memory_space=pltpu.VMEM)
    return pl.pallas_call(
        body, name=name,
        out_shape=jax.ShapeDtypeStruct(vals.shape, F32),
        in_specs=[vmem], out_specs=vmem,
        scratch_shapes=[pltpu.VMEM((N_DEV, rows, width), F32),
                        pltpu.SemaphoreType.DMA((7,)), pltpu.SemaphoreType.DMA((7,))],
        compiler_params=pltpu.CompilerParams(has_side_effects=True),
    )(vals)


def _adamw_math(w, g, m, v):
    m = ADAM_B1 * m + (1.0 - ADAM_B1) * g
    v = ADAM_B2 * v + (1.0 - ADAM_B2) * (g * g)
    m_hat = m / (1.0 - ADAM_B1 ** ADAM_STEP)
    v_hat = v / (1.0 - ADAM_B2 ** ADAM_STEP)
    delta = -ADAM_LR * (m_hat / (jnp.sqrt(v_hat) + ADAM_EPS) + ADAM_WD * w)
    return delta, m, v


ROW_TILE_ELEMS = 384 * 1024


def _row_tile(rows, width):
    best = SUBLANES
    for t in range(SUBLANES, rows + 1, SUBLANES):
        if rows % t == 0 and t * width <= ROW_TILE_ELEMS:
            best = t
    return best


def _sum_received(r_ref, own, me):
    g = None
    for j in range(N_DEV):
        term = jnp.where(me == j, own, r_ref[j].astype(F32))
        g = term if g is None else g + term
    return g


def _my_index_operand():
    return _index(*_me()).astype(jnp.int32).reshape(1)


def _adamw_recv(recv, grad, kind, w, m, v, name):
    _, rows, width = recv.shape
    tile = _row_tile(rows, width)
    nt = rows // tile

    def body(me_ref, r_ref, own_ref, w_ref, m_ref, v_ref, g_out, d_out, m_out, v_out):
        g = _sum_received(r_ref, own_ref[...].astype(F32), me_ref[0])
        d, mn, vn = _adamw_math(w_ref[...], g, m_ref[...], v_ref[...])
        g_out[...] = g
        d_out[...] = d
        m_out[...] = mn
        v_out[...] = vn

    if kind == "rows":
        own_spec = pl.BlockSpec((tile, width), lambda i, me: (me[0] * nt + i, 0))
    else:
        own_spec = pl.BlockSpec((tile, width), lambda i, me: (i, me[0]))
    spec = pl.BlockSpec((tile, width), lambda i, me: (i, 0))
    shape = jax.ShapeDtypeStruct((rows, width), F32)
    return pl.pallas_call(
        body, name=name, out_shape=[shape] * 4,
        grid_spec=pltpu.PrefetchScalarGridSpec(
            num_scalar_prefetch=1, grid=(nt,),
            in_specs=[pl.BlockSpec((N_DEV, tile, width), lambda i, me: (0, i, 0)), own_spec, spec, spec, spec],
            out_specs=[spec] * 4),
        compiler_params=_params(("parallel",)),
    )(_my_index_operand(), recv, grad, w, m, v)


WIN_STEP = 1408
WIN_W = 1536
IN_SHARD = IN_COLS // N_DEV
IN_PADDED = WIN_STEP * (N_DEV - 1) + WIN_W


def _roll_w_in(shard_padded):
    rows = shard_padded.shape[0]
    tile = _row_tile(rows, WIN_W)

    def body(x_ref, main_ref, edge_ref):
        win = pltpu.roll(x_ref[...], 2 * _index(*_me()), 1).astype(BF16)
        main_ref[...] = win[:, :WIN_STEP]
        edge_ref[...] = win[:, WIN_STEP:]

    return pl.pallas_call(
        body, name="w_in_window",
        out_shape=[jax.ShapeDtypeStruct((rows, WIN_STEP), BF16), jax.ShapeDtypeStruct((rows, WIN_W - WIN_STEP), BF16)],
        grid=(rows // tile,),
        in_specs=[pl.BlockSpec((tile, WIN_W), lambda i: (i, 0))],
        out_specs=[pl.BlockSpec((tile, WIN_STEP), lambda i: (i, 0)),
                   pl.BlockSpec((tile, WIN_W - WIN_STEP), lambda i: (i, 0))],
        compiler_params=_params(("parallel",)),
    )(shard_padded)


def _sum_w_in_windows(recv, grad):
    _, rows, width = recv.shape
    tile = _row_tile(rows, width)

    def body(me_ref, r_ref, g_ref, g_out, own_ref, sem):
        me = me_ref[0]
        rows_i = pl.ds(pl.multiple_of(pl.program_id(0) * tile, tile), tile)
        own = pltpu.make_async_copy(g_ref.at[rows_i, pl.ds(pl.multiple_of(me * WIN_STEP, LANES), width)], own_ref, sem)
        own.start()
        own.wait()
        g_out[...] = pltpu.roll(_sum_received(r_ref, own_ref[...].astype(F32), me), width - 2 * me, 1)

    return pl.pallas_call(
        body, name="w_in_grad_sum", out_shape=jax.ShapeDtypeStruct((rows, width), F32),
        grid_spec=pltpu.PrefetchScalarGridSpec(
            num_scalar_prefetch=1, grid=(rows // tile,),
            in_specs=[pl.BlockSpec((N_DEV, tile, width), lambda i, me: (0, i, 0)), HBM_SPEC],
            out_specs=pl.BlockSpec((tile, width), lambda i, me: (i, 0)),
            scratch_shapes=[pltpu.VMEM((tile, width), BF16), pltpu.SemaphoreType.DMA]),
        compiler_params=_params(("arbitrary",)),
    )(_my_index_operand(), recv, grad)


def _adamw_small(w, g, m, v, name):
    def fn(i, n, w_, g_, m_, v_):
        return _adamw_math(w_, g_, m_, v_)

    r, c = w.shape
    return _rows(fn, [(w, "t"), (g, "t"), (m, "t"), (v, "t")], [], [(c, F32)] * 3, [], _row_tile(r, c), name)


def _norm_fwd(x, w, name):
    return _rows(lambda i, n, x_, w_: (_rms(x_, w_[...]),), [(x, "t")], [w], [(D_MODEL, BF16)], [], 512, name)[0]


def _residual_norm_fwd(x, y, scale, w, name):
    def fn(i, n, x_, y_, w_):
        xn = x_ + scale * y_
        return xn, _rms(xn, w_[...])

    return _rows(fn, [(x, "t"), (y, "t")], [w], [(D_MODEL, F32), (D_MODEL, BF16)], [], 512, name)


def _residual_norm_bwd(x, w, dhs, dres, scale, name):
    nh = len(dhs)

    def fn(i, n, x_, dres_, *rest):
        dh = rest[0]
        for extra in rest[1:nh]:
            dh = dh + extra
        _, vjp = jax.vjp(_rms, x_, rest[nh][...])
        dx, dw = vjp(dh)
        dx = dx + dres_
        return dx, scale * dx, dw

    return _rows(fn, [(x, "t"), (dres, "t")] + [(d, "t") for d in dhs], [w],
                 [(D_MODEL, F32), (D_MODEL, BF16)], [(1, D_MODEL)], 256, name)


def _ffn_fwd(h, w_gu, get_w_down, tag):
    gu = _matmul(h, w_gu, "nn", BF16, tag + "_gu")
    act = _rows(lambda i, n, gu_: (_swiglu(gu_),), [(gu, "t")], [], [(D_FF, BF16)], [], 128, tag + "_swiglu")[0]
    y = _matmul(act, get_w_down(act), "nn", F32, tag + "_down")
    return gu, act, y


def _ffn_bwd(h, gu, act, dy, w_gu, w_down, tag, comm, more=None):
    dact = _matmul(dy, w_down, "nt", BF16, tag + "_dact")

    def fn(i, n, gu_, dact_):
        _, vjp = jax.vjp(_swiglu, gu_)
        return vjp(dact_)

    dgu = _rows(fn, [(gu, "t"), (dact, "t")], [], [(2 * D_FF, BF16)], [], 128, tag + "_swiglu_bwd")[0]
    sent = comm.send(tag + "_gu", {tag + "_w_gu": _matmul(h, dgu, "tn", BF16, tag + "_d_w_gu")})
    sent = sent + comm.send(tag + "_down", {tag + "_w_down": _matmul(act, dy + sent.astype(BF16), "tn", BF16,
                                                                    tag + "_d_w_down"), **(more or {})})
    dh = _matmul(dgu, w_gu, "nt", F32, tag + "_dh")
    return dh, sent


def _expanders():
    e_g = np.zeros((LANES, HW), np.float32)
    e_b = np.zeros((LANES, HW), np.float32)
    for h in range(HEADS):
        e_g[h, h * HEAD_DIM:(h + 1) * HEAD_DIM] = 1.0
        e_b[HEADS + h, h * HEAD_DIM:(h + 1) * HEAD_DIM] = 1.0
    return jnp.asarray(e_g), jnp.asarray(e_b)


def _pad_lanes(v):
    return jnp.pad(v, ((0, 0), (0, LANES - v.shape[1])))


class _LocalWeights:
    def __init__(self, big):
        self.big, self.sent = big, {}

    def arrive(self, group, after):
        return self.big

    def send(self, group, grads):
        self.sent.update(grads)
        return jnp.zeros((), F32)


def _local_step(x, p, tgt, small, comm):
    big = dict(comm.arrive("ffn1", None))
    e_g, e_b = _expanders()
    alog, dtb = _pad_lanes(small["a_log"]), _pad_lanes(small["dt_bias"])
    conv_w = jnp.pad(small["conv_w"], ((0, SUBLANES - CONV_K), (0, 0)))
    rel = small["rel_bias"]

    h1 = _norm_fwd(x, small["ffn1_norm"], "ffn1_norm")
    def ffn1_w_down(act):
        big.update(comm.arrive("ffn1_down", act))
        return big["ffn1_w_down"]

    gu1, act1, y1 = _ffn_fwd(h1, big["ffn1_w_gu"], ffn1_w_down, "ffn1")
    x1, h2 = _residual_norm_fwd(x, y1, 0.5, small["mix_norm"], "mix_norm")

    big = {**big, **comm.arrive("mixer", h2)}
    w_in = big["w_in"]
    w_qz = w_in[:, :IN_QZ]
    w_ab = jnp.pad(w_in[:, IN_AB0:IN_QKVB0], ((0, 0), (0, LANES - 2 * HEADS)))
    w_qkvb = w_in[:, IN_QKVB0:IN_GG0]
    w_gg = w_in[:, IN_GG0:IN_COLS]
    qz = _matmul(h2, w_qz, "nn", F32, "in_qz")
    ab = _matmul(h2, w_ab, "nn", F32, "in_ab")
    pb = _matmul(h2, w_qkvb, "nn", F32, "in_qkvb")
    gg = _matmul(h2, w_gg, "nn", BF16, "in_gates")
    pa, z = qz[:, :3 * HW], qz[:, 3 * HW:]

    def prep(i, n, pa_, prev_, ab_, cw_, alog_, dtb_, eg_, eb_):
        q, k, v = _gdn_post(_conv(pa_, prev_, cw_, i))
        g_b, beta_b = _gdn_gates(ab_, alog_[...], dtb_[...], eg_[...], eb_[...])
        return q, k, v, g_b, beta_b

    qn, kn, vv, g_b, beta_b = _rows(prep, [(pa, "t"), (pa, "p"), (ab, "t")], [conv_w, alog, dtb, e_g, e_b],
                                    [(HW, F32)] * 5, [], 256, "gdn_prep")
    u, w, aqk, qd, kt, tl = _gdn_intra(qn, kn, vv, g_b, beta_b)
    o, states = _gdn_scan(u, w, aqk, qd, kt, tl)
    ya = _rows(lambda i, n, o_, z_, w_: (_gated_norm(o_, z_, w_[...]),), [(o, "t"), (z, "t")], [small["gdn_norm"]],
               [(HW, BF16)], [], 512, "gdn_gated_norm")[0]

    pbp = jnp.pad(pb, ((ATT_PAD, 0), (0, 0)))
    yb = _attention(pb, pbp, small["q_norm"], small["k_norm"], rel)

    ta = _matmul(ya, big["w_branch_a"], "nn", BF16, "branch_a")
    tb = _matmul(yb, big["w_branch_b"], "nn", BF16, "branch_b")
    mixed = _rows(lambda i, n, gg_, ta_, tb_: (_mix(gg_, ta_, tb_),), [(gg, "t"), (ta, "t"), (tb, "t")], [],
                  [(D_MODEL, BF16)], [], 256, "mix")[0]
    m_out = _matmul(mixed, big["w_out"], "nn", F32, "w_out")
    x2, h3 = _residual_norm_fwd(x1, m_out, 1.0, small["ffn2_norm"], "ffn2_norm")
    big = {**big, **comm.arrive("tail", h3)}
    gu2, act2, y2 = _ffn_fwd(h3, big["ffn2_w_gu"], lambda act: big["ffn2_w_down"], "ffn2")
    x3, h4 = _residual_norm_fwd(x2, y2, 0.5, small["ple_norm"], "ple_norm")
    gp = _matmul(h4, big["ple_gate"], "nn", BF16, "ple_gate")
    pp = _matmul(p, big["ple_proj"], "nn", BF16, "ple_proj")

    def head(i, n, x3_, gp_, pp_, tgt_):
        sg = _sigmoid(gp_)
        err = x3_ + sg * pp_ - tgt_
        dx4 = err * (1.0 / D_MODEL)
        sq = _colsum(err * err)
        part = sq[:, :LANES]
        for j in range(1, D_MODEL // LANES):
            part = part + sq[:, j * LANES:(j + 1) * LANES]
        return dx4, dx4 * pp_ * sg * (1.0 - sg), dx4 * sg, (0.5 / D_MODEL) * part

    dx4, dgp, dpp, loss_lanes = _rows(head, [(x3, "t"), (gp, "t"), (pp, "t"), (tgt, "t")], [],
                                      [(D_MODEL, F32), (D_MODEL, BF16), (D_MODEL, BF16)], [(1, LANES)], 256,
                                      "ple_loss_head")
    loss = jnp.sum(loss_lanes)

    gbig, gsmall = {}, {}
    gbig["ple_proj"] = _matmul(p, dpp, "tn", BF16, "d_ple_proj")
    gbig["ple_gate"] = _matmul(h4, dgp, "tn", BF16, "d_ple_gate")
    dh4 = _matmul(dgp, big["ple_gate"], "nt", F32, "ple_gate_dh")
    dx3, dy2, gsmall["ple_norm"] = _residual_norm_bwd(x3, small["ple_norm"], [dh4], dx4, 0.5, "ple_norm_bwd")

    dh3, sent = _ffn_bwd(h3, gu2, act2, dy2, big["ffn2_w_gu"], big["ffn2_w_down"], "ffn2", comm,
                         {n: gbig[n] for n in ("ple_proj", "ple_gate")})
    dx2, dx2b, gsmall["ffn2_norm"] = _residual_norm_bwd(x2, small["ffn2_norm"] + sent, [dh3], dx3, 1.0,
                                                        "ffn2_norm_bwd")

    gbig["w_out"] = _matmul(mixed, dx2b, "tn", BF16, "d_w_out")
    dmixed = _matmul(dx2b, big["w_out"], "nt", BF16, "w_out_dx")

    def mix_bwd(i, n, gg_, ta_, tb_, dm_):
        _, vjp = jax.vjp(_mix, gg_, ta_, tb_)
        return vjp(dm_)

    dgg, dta, dtb_ = _rows(mix_bwd, [(gg, "t"), (ta, "t"), (tb, "t"), (dmixed, "t")], [],
                           [(2 * D_MODEL, BF16), (D_MODEL, BF16), (D_MODEL, BF16)], [], 256, "mix_bwd")
    gbig["w_branch_a"] = _matmul(ya, dta, "tn", BF16, "d_branch_a")
    gbig["w_branch_b"] = _matmul(yb, dtb_, "tn", BF16, "d_branch_b")
    dya = _matmul(dta, big["w_branch_a"], "nt", F32, "branch_a_dx")
    dyb = _matmul(dtb_, big["w_branch_b"], "nt", F32, "branch_b_dx")

    dq_b, dk_b, dv_b, gsmall["q_norm"], gsmall["k_norm"], gsmall["rel_bias"] = _attention_bwd(
        pb, pbp, small["q_norm"], small["k_norm"], rel, dyb)
    dpb = jnp.concatenate([dq_b, dk_b[ATT_PAD:].astype(BF16), dv_b[ATT_PAD:].astype(BF16)], axis=1)

    def gated_bwd(i, n, o_, z_, dya_, w_):
        _, vjp = jax.vjp(_gated_norm, o_, z_, w_[...])
        return vjp(dya_)

    do, dz, gsmall["gdn_norm"] = _rows(gated_bwd, [(o, "t"), (z, "t"), (dya, "t")], [small["gdn_norm"]],
                                       [(HW, F32), (HW, BF16)], [(1, HEAD_DIM)], 256, "gdn_gated_norm_bwd")
    du, dw, da, dqd, dkt, dtl = _gdn_scan_bwd(do, u, w, aqk, qd, kt, tl, states)
    dqn, dkn, dvv, dg_b, dbeta_b = _gdn_intra_bwd(qn, kn, vv, g_b, beta_b, du, dw, da, dqd, dkt, dtl)

    def prep_bwd(i, n, pa_, prev_, ab_, dq_, dk_, dv_, dg_, db_, cw_, alog_, dtb_, eg_, eb_):
        _, vjp = jax.vjp(_gdn_post, _conv(pa_, prev_, cw_, i))
        (dy,) = vjp((dq_, dk_, dv_))
        e_g_, e_b_ = eg_[...], eb_[...]
        _, vjp_g = jax.vjp(lambda a, b, c: _gdn_gates(a, b, c, e_g_, e_b_), ab_, alog_[...], dtb_[...])
        dab, dalog, ddtb = vjp_g((dg_, db_))
        return dy, dab, dalog, ddtb

    dy_conv, dab, dalog, ddtb = _rows(
        prep_bwd, [(pa, "t"), (pa, "p"), (ab, "t"), (dqn, "t"), (dkn, "t"), (dvv, "t"), (dg_b, "t"), (dbeta_b, "t")],
        [conv_w, alog, dtb, e_g, e_b], [(3 * HW, F32), (LANES, BF16)], [(1, LANES), (1, LANES)], 256,
        "gdn_prep_bwd")
    gsmall["a_log"] = dalog[:, :HEADS]
    gsmall["dt_bias"] = ddtb[:, :HEADS]

    def conv_bwd(i, n, dy_, nxt_, pa_, prev_, cw_):
        dpa = dy_ * cw_[CONV_K - 1:CONV_K, :]
        row = lax.broadcasted_iota(jnp.int32, (SUBLANES, dy_.shape[1]), 0)
        dcw = jnp.where(row == CONV_K - 1, _colsum(dy_ * pa_), 0.0)
        for j in range(CONV_K - 1):
            s = CONV_K - 1 - j
            dpa = dpa + _shift_up(dy_, nxt_, s, i, n) * cw_[j:j + 1, :]
            dcw = dcw + jnp.where(row == j, _colsum(dy_ * _shift_down(pa_, prev_, s, i)), 0.0)
        return dpa, dcw

    dpa, dcw = _rows(conv_bwd, [(dy_conv, "t"), (dy_conv, "n"), (pa, "t"), (pa, "p")], [conv_w],
                     [(3 * HW, BF16)], [(SUBLANES, 3 * HW)], 256, "gdn_conv_bwd")
    gsmall["conv_w"] = dcw[:CONV_K]

    dqz = jnp.concatenate([dpa, dz], axis=1)
    d_w_qz = _matmul(h2, dqz, "tn", BF16, "d_in_qz")
    d_w_ab = _matmul(h2, dab, "tn", BF16, "d_in_ab")
    d_w_qkvb = _matmul(h2, dpb, "tn", BF16, "d_in_qkvb")
    d_w_gg = _matmul(h2, dgg, "tn", BF16, "d_in_gates")
    gbig["w_in"] = jnp.concatenate([d_w_qz, d_w_ab[:, :2 * HEADS], d_w_qkvb, d_w_gg,
                                    jnp.zeros((D_MODEL, IN_PADDED - IN_COLS), BF16)], axis=1)
    dh2 = [_matmul(dqz, w_qz, "nt", F32, "in_qz_dh"), _matmul(dab, w_ab, "nt", F32, "in_ab_dh"),
           _matmul(dpb, w_qkvb, "nt", F32, "in_qkvb_dh"), _matmul(dgg, w_gg, "nt", F32, "in_gates_dh")]
    sent = comm.send("mixer", {n: gbig[n] for n in ("w_out", "w_branch_b", "w_branch_a", "w_in")})
    dx1, dy1, gsmall["mix_norm"] = _residual_norm_bwd(x1, small["mix_norm"] + sent, dh2, dx2, 0.5, "mix_norm_bwd")

    dh1, sent = _ffn_bwd(h1, gu1, act1, dy1, big["ffn1_w_gu"], big["ffn1_w_down"], "ffn1", comm)
    grad_x, _, gsmall["ffn1_norm"] = _residual_norm_bwd(x, small["ffn1_norm"] + sent, [dh1], dx1, 1.0,
                                                        "ffn1_norm_bwd")
    return loss, grad_x, gsmall


GATHER_GROUPS = {"ffn1": ("ffn1_w_gu",),
                 "ffn1_down": ("ffn1_w_down",),
                 "mixer": ("w_in_main", "w_in_edge", "w_branch_a", "w_branch_b", "w_out"),
                 "tail": ("ffn2_w_gu", "ffn2_w_down", "ple_gate", "ple_proj")}


def _kind(name):
    return "cols" if name in COL_SHARDED or name.startswith("w_in_") else "rows"


def _merge_w_in(main, edges):
    edge_w = WIN_W - WIN_STEP
    w_in = jnp.pad(main, ((0, 0), (0, edge_w)))
    for d in range(N_DEV):
        at = WIN_STEP * (d + 1)
        w_in = w_in + jnp.pad(edges[:, d * edge_w:(d + 1) * edge_w], ((0, 0), (at, IN_PADDED - at - edge_w)))
    return w_in


class _Fsdp:
    def __init__(self, wts, first):
        main, edge = _roll_w_in(jnp.pad(wts["w_in"], ((0, 0), (0, WIN_W - IN_SHARD))))
        self.shards = {n: wts[n].astype(BF16) for n in BIG if n not in ("w_in", "ffn1_w_gu")}
        self.shards.update(w_in_main=main, w_in_edge=edge, ffn1_w_gu=(wts["ffn1_w_gu"] + first).astype(BF16))
        names = GATHER_GROUPS["ffn1"]
        self.first = dict(zip(names, _all_gather([self.shards[n] for n in names], [_kind(n) for n in names],
                                                 len(names))))
        self.flight, token = {}, self.first["ffn1_w_gu"][0, 0].astype(F32) * 0.0
        for group in ("ffn1_down", "mixer", "tail"):
            names = GATHER_GROUPS[group]
            srcs = [self.shards[n] for n in names]
            lands = [_place_block(self.shards[n], _kind(n), "own_" + n) for n in names]
            make = _gather_copies([s.shape for s in srcs], [_kind(n) for n in names])
            srcs[0] = srcs[0] + token.astype(BF16)
            send_sems, recv_sems, srcs, lands, tok = _split_start(srcs, lands, make, "gather_start_" + group)
            token = token + tok[0, 0]
            self.flight[group] = (send_sems, recv_sems, srcs, lands, make)
        self.token = token
        self.sent = {}

    def arrive(self, group, after):
        if group == "ffn1":
            return self.first
        send_sems, recv_sems, srcs, lands, make = self.flight[group]
        _, lands = _split_wait(send_sems, recv_sems, srcs, lands, after, make, "gather_wait_" + group)
        full = dict(zip(GATHER_GROUPS[group], lands))
        if group == "mixer":
            full["w_in"] = _merge_w_in(full.pop("w_in_main"), full.pop("w_in_edge"))
        return full

    def send(self, group, grads):
        names = list(grads)
        kinds = ["win" if n == "w_in" else _kind(n) for n in names]
        shapes = [(D_MODEL, WIN_W) if n == "w_in" else self.shards[n].shape for n in names]
        srcs = [grads[n] for n in names]
        lands = [lax.empty((N_DEV,) + tuple(s), BF16) for s in shapes]
        make = _exchange_copies(shapes, kinds)
        send_sems, recv_sems, srcs, lands, tok = _split_start(srcs, lands, make, "grads_start_" + group)
        self.sent[group] = (names, kinds, send_sems, recv_sems, srcs, lands, make)
        return tok[0, 0]

    def received(self, group, after):
        names, kinds, send_sems, recv_sems, srcs, lands, make = self.sent[group]
        srcs, lands = _split_wait(send_sems, recv_sems, srcs, lands, after, make, "grads_wait_" + group)
        return {n: (k, g, r) for n, k, g, r in zip(names, kinds, srcs, lands)}


SMALL_ROWS = ("ffn1_norm", "mix_norm", "ffn2_norm", "ple_norm", "gdn_norm", "q_norm", "k_norm", "a_log", "dt_bias",
              "rel_bias", "conv_w")


def _pack_small(vals):
    rows = []
    for n in SMALL_ROWS:
        v = vals[n]
        if n == "rel_bias":
            v = jnp.pad(v, ((0, 0), (0, 2 * LANES - N_REL)))
        elif n in ("a_log", "dt_bias"):
            v = _pad_lanes(v)
        rows.append(v.reshape(-1, LANES))
    packed = jnp.concatenate(rows, axis=0)
    return jnp.pad(packed, ((0, -packed.shape[0] % SUBLANES), (0, 0)))


def _unpack_small(packed, shapes):
    out, off = {}, 0
    for n in SMALL_ROWS:
        shp = shapes[n]
        if n == "rel_bias":
            out[n] = packed[off:off + 2 * HEADS].reshape(HEADS, 2 * LANES)[:, :N_REL]
            off += 2 * HEADS
        elif n in ("a_log", "dt_bias"):
            out[n] = packed[off:off + 1, :HEADS]
            off += 1
        else:
            r = int(np.prod(shp)) // LANES
            out[n] = packed[off:off + r].reshape(shp)
            off += r
    return out


WEIGHTS = ("ffn1_norm", "ffn1_w_gu", "ffn1_w_down", "mix_norm", "w_in", "conv_w", "a_log", "dt_bias", "gdn_norm",
           "q_norm", "k_norm", "rel_bias", "w_branch_a", "w_branch_b", "w_out", "ffn2_norm", "ffn2_w_gu",
           "ffn2_w_down", "ple_norm", "ple_gate", "ple_proj")


def kernel(x, p, ffn1_norm, ffn1_w_gu, ffn1_w_down, mix_norm, w_in, conv_w, a_log, dt_bias, gdn_norm, q_norm, k_norm, rel_bias, w_branch_a, w_branch_b, w_out, ffn2_norm, ffn2_w_gu, ffn2_w_down, ple_norm, ple_gate, ple_proj, loss_target, m_ffn1_norm, m_ffn1_w_gu, m_ffn1_w_down, m_mix_norm, m_w_in, m_conv_w, m_a_log, m_dt_bias, m_gdn_norm, m_q_norm, m_k_norm, m_rel_bias, m_w_branch_a, m_w_branch_b, m_w_out, m_ffn2_norm, m_ffn2_w_gu, m_ffn2_w_down, m_ple_norm, m_ple_gate, m_ple_proj, v_ffn1_norm, v_ffn1_w_gu, v_ffn1_w_down, v_mix_norm, v_w_in, v_conv_w, v_a_log, v_dt_bias, v_gdn_norm, v_q_norm, v_k_norm, v_rel_bias, v_w_branch_a, v_w_branch_b, v_w_out, v_ffn2_norm, v_ffn2_w_gu, v_ffn2_w_down, v_ple_norm, v_ple_gate, v_ple_proj):
    args = dict(locals())
    def layer0(v):
        return v[0] if v.ndim == 3 else v

    wts = {n: layer0(args[n]) for n in WEIGHTS}
    mom = {n: layer0(args["m_" + n]) for n in WEIGHTS}
    var = {n: layer0(args["v_" + n]) for n in WEIGHTS}
    x2d, p2d, tgt = x[0], p[0, 0], loss_target[0]
    my_index = _index(*_me())

    small = {n: wts[n] for n in SMALL_ROWS if n != "conv_w"}
    conv_shard = wts["conv_w"]
    conv_cols = conv_shard.shape[1]
    conv_packed = jnp.zeros((SUBLANES, N_DEV * conv_cols), F32)
    conv_packed = lax.dynamic_update_slice(conv_packed, jnp.pad(conv_shard, ((0, SUBLANES - CONV_K), (0, 0))),
                                           (0, my_index * conv_cols))
    small["conv_w"] = _all_reduce_small(conv_packed.reshape(-1, LANES), "conv_w_gather").reshape(SUBLANES, -1)[:CONV_K]

    fsdp = _Fsdp(wts, small["conv_w"][0, 0] * 0.0)
    small["ffn1_norm"] = small["ffn1_norm"] + fsdp.token

    loss, grad_x, gsmall = _local_step(x2d, p2d, tgt, small, fsdp)
    loss = lax.psum(loss, ("x", "y", "c"))

    outs_big, after = {}, grad_x
    for group in list(fsdp.sent):
        for n, (kind, grad, recv) in fsdp.received(group, after).items():
            if n == "w_in":
                g_in = _sum_w_in_windows(recv, grad)[:, :IN_SHARD]
                outs_big[n] = [g_in] + list(_adamw_small(wts[n], g_in, mom[n], var[n], "adamw_w_in"))
            else:
                outs_big[n] = _adamw_recv(recv, grad, kind, wts[n], mom[n], var[n], "adamw_" + n)
            after = outs_big[n][1]

    small_shapes = {n: (small[n].shape if n != "conv_w" else (CONV_K, N_DEV * conv_cols)) for n in SMALL_ROWS}
    gsum = _unpack_small(_all_reduce_small(_pack_small(gsmall), "small_grads_all_reduce"), small_shapes)
    gsum["conv_w"] = lax.dynamic_slice(gsum["conv_w"], (0, my_index * conv_cols), (CONV_K, conv_cols))
    rep = [n for n in SMALL_ROWS if n != "conv_w"]
    rep_shapes = {n: small_shapes[n] for n in rep}

    def pack_rep(vals):
        return _pack_small({**{n: vals[n] for n in rep}, "conv_w": jnp.zeros((CONV_K, LANES), F32)})

    def unpack_rep(packed):
        return _unpack_small(packed, {**rep_shapes, "conv_w": (CONV_K, LANES)})

    outs_small = [unpack_rep(o) for o in _adamw_small(pack_rep(wts), pack_rep(gsum), pack_rep(mom), pack_rep(var),
                                                      "adamw_replicated")]
    pad8 = functools.partial(jnp.pad, pad_width=((0, SUBLANES - CONV_K), (0, 0)))
    outs_conv = [o[:CONV_K] for o in _adamw_small(pad8(conv_shard), pad8(gsum["conv_w"]), pad8(mom["conv_w"]),
                                                   pad8(var["conv_w"]), "adamw_conv")]

    def leaf(kind, n):
        if n in BIG:
            return outs_big[n][kind][None]
        if n == "conv_w":
            return (gsum["conv_w"] if kind == 0 else outs_conv[kind - 1])[None]
        return (gsum[n] if kind == 0 else outs_small[kind - 1][n]).reshape(args[n].shape)

    result = [loss, grad_x[None]]
    for kind in range(4):
        result += [leaf(kind, n) for n in WEIGHTS]
    return tuple(result)
```

```python
import functools

import numpy as np
import jax
import jax.numpy as jnp
from jax import lax
from jax.experimental import pallas as pl
from jax.experimental.pallas import tpu as pltpu

F32 = jnp.float32
BF16 = jnp.bfloat16
HIGHEST = lax.Precision.HIGHEST
MESH = pl.DeviceIdType.MESH

D_MODEL = 2048
D_FF = 5632
HEADS = 8
HEAD_DIM = 128
HW = HEADS * HEAD_DIM
CHUNK = 64
LEFT_CHUNKS = 8
MAX_REL = 128
N_REL = (CHUNK - 1) + MAX_REL + 1
CONV_K = 4
EPS = 1e-6
NEG_INF = -1e30
N_DEV = 8
LANES = 128
SUBLANES = 8
VMEM_LIMIT = 56 * 1024 * 1024

MATMUL_WHOLE_K = 2048

ATT_QB = 256
ATT_KW = ATT_QB + LEFT_CHUNKS * CHUNK
ATT_PAD = LEFT_CHUNKS * CHUNK
GDN_CB = 8
GDN_GROUP = 8

ADAM_LR = 0.001
ADAM_B1 = 0.9
ADAM_B2 = 0.999
ADAM_EPS = 1e-08
ADAM_WD = 0.01
ADAM_STEP = 10

IN_QZ = 3 * HW + HW
IN_AB0 = IN_QZ
IN_QKVB0 = IN_AB0 + 2 * HEADS
IN_GG0 = IN_QKVB0 + 3 * HW
IN_COLS = IN_GG0 + 2 * D_MODEL

BIG = ("ffn1_w_gu", "ffn1_w_down", "w_in", "w_branch_a", "w_branch_b", "w_out",
       "ffn2_w_gu", "ffn2_w_down", "ple_gate", "ple_proj")
COL_SHARDED = ("ffn1_w_gu", "w_in", "w_branch_a", "w_branch_b", "ffn2_w_gu", "ple_proj")


def _params(semantics=None, **kw):
    return pltpu.CompilerParams(dimension_semantics=semantics, vmem_limit_bytes=VMEM_LIMIT, **kw)


def _pick(n, cands):
    for c in cands:
        if n % c == 0:
            return c
    return n


def _matmul(a, b, mode, out_dtype, name):
    if mode == "nn":
        (m, k), (k2, n) = a.shape, b.shape
    elif mode == "nt":
        (m, k), (n, k2) = a.shape, b.shape
    else:
        (k, m), (k2, n) = a.shape, b.shape
    assert k == k2, (a.shape, b.shape, mode)
    tm = _pick(m, (1024, 512, 256, 128))
    tn = _pick(n, (1024, 512, 256, 128))
    tk = k if k <= MATMUL_WHOLE_K else _pick(k, (2816, 2048, 1536, 1024, 512, 256, 128))
    nk = k // tk
    if mode == "nn":
        a_spec = pl.BlockSpec((tm, tk), lambda i, j, kk: (i, kk))
        b_spec = pl.BlockSpec((tk, tn), lambda i, j, kk: (kk, j))
        dims = (((1,), (0,)), ((), ()))
    elif mode == "nt":
        a_spec = pl.BlockSpec((tm, tk), lambda i, j, kk: (i, kk))
        b_spec = pl.BlockSpec((tn, tk), lambda i, j, kk: (j, kk))
        dims = (((1,), (1,)), ((), ()))
    else:
        a_spec = pl.BlockSpec((tk, tm), lambda i, j, kk: (kk, i))
        b_spec = pl.BlockSpec((tk, tn), lambda i, j, kk: (kk, j))
        dims = (((0,), (0,)), ((), ()))

    def body(a_ref, b_ref, o_ref, *acc):
        prod = lax.dot_general(a_ref[...].astype(BF16), b_ref[...].astype(BF16), dims, preferred_element_type=F32)
        if nk == 1:
            o_ref[...] = prod.astype(o_ref.dtype)
            return
        acc_ref, kk = acc[0], pl.program_id(2)

        @pl.when(kk == 0)
        def _():
            acc_ref[...] = prod

        @pl.when((kk > 0) & (kk < nk - 1))
        def _():
            acc_ref[...] += prod

        @pl.when(kk == nk - 1)
        def _():
            o_ref[...] = (acc_ref[...] + prod).astype(o_ref.dtype)

    return pl.pallas_call(
        body, name=name,
        out_shape=jax.ShapeDtypeStruct((m, n), out_dtype),
        grid=(m // tm, n // tn, nk),
        in_specs=[a_spec, b_spec],
        out_specs=pl.BlockSpec((tm, tn), lambda i, j, kk: (i, j)),
        scratch_shapes=[pltpu.VMEM((tm, tn), F32)] if nk > 1 else [],
        compiler_params=_params(("parallel", "parallel", "arbitrary")),
    )(a, b)


def _rows(fn, row_ins, consts, row_outs, acc_outs, tile, name):
    t_rows = row_ins[0][0].shape[0]
    tile = min(tile, t_rows)
    assert t_rows % tile == 0 and tile % SUBLANES == 0
    n = t_rows // tile
    per = tile // SUBLANES
    last8 = t_rows // SUBLANES - 1
    in_specs = []
    for arr, kind in row_ins:
        c = arr.shape[1]
        if kind == "t":
            in_specs.append(pl.BlockSpec((tile, c), lambda i: (i, 0)))
        elif kind == "p":
            in_specs.append(pl.BlockSpec((SUBLANES, c), lambda i: (jnp.maximum(i * per - 1, 0), 0)))
        else:
            in_specs.append(pl.BlockSpec((SUBLANES, c), lambda i: (jnp.minimum((i + 1) * per, last8), 0)))
    for arr in consts:
        in_specs.append(pl.BlockSpec(arr.shape, lambda i, nd=arr.ndim: (0,) * nd))
    out_shape = [jax.ShapeDtypeStruct((t_rows, c), dt) for c, dt in row_outs]
    out_specs = [pl.BlockSpec((tile, c), lambda i: (i, 0)) for c, _ in row_outs]
    for shp in acc_outs:
        out_shape.append(jax.ShapeDtypeStruct(shp, F32))
        out_specs.append(pl.BlockSpec(shp, lambda i, nd=len(shp): (0,) * nd))
    n_in = len(row_ins) + len(consts)
    n_row_out = len(row_outs)

    def body(*refs):
        i = pl.program_id(0)
        vals = [r[...].astype(F32) for r in refs[:len(row_ins)]]
        res = fn(i, n, *vals, *refs[len(row_ins):n_in])
        outs = refs[n_in:]
        for r, v in zip(outs[:n_row_out], res[:n_row_out]):
            r[...] = v.astype(r.dtype)
        if acc_outs:
            @pl.when(i == 0)
            def _():
                for r in outs[n_row_out:]:
                    r[...] = jnp.zeros_like(r)

            for r, v in zip(outs[n_row_out:], res[n_row_out:]):
                r[...] += v

    res = pl.pallas_call(
        body, name=name, out_shape=out_shape, grid=(n,), in_specs=in_specs, out_specs=out_specs,
        compiler_params=_params(("arbitrary",) if acc_outs else ("parallel",)),
    )(*[a for a, _ in row_ins], *consts)
    return res


def _rms(x, w):
    return x * lax.rsqrt(jnp.mean(x * x, axis=-1, keepdims=True) + EPS) * w


def _l2n(x):
    return x * lax.rsqrt(jnp.sum(x * x, axis=-1, keepdims=True) + EPS)


def _sigmoid(x):
    return 1.0 / (1.0 + jnp.exp(-x))


def _silu(x):
    return x * _sigmoid(x)


def _softplus(x):
    return jnp.maximum(x, 0.0) + jnp.log(1.0 + jnp.exp(-jnp.abs(x)))


def _heads(fn, *xs):
    nh = xs[0].shape[1] // HEAD_DIM
    return jnp.concatenate(
        [fn(*[x[:, h * HEAD_DIM:(h + 1) * HEAD_DIM] for x in xs]) for h in range(nh)], axis=1)


def _colsum(x):
    return jnp.sum(x, axis=0, keepdims=True)


def _swiglu(gu):
    return _silu(gu[:, :D_FF]) * gu[:, D_FF:]


def _gated_norm(o, z, w):
    return _heads(lambda oh, zh: _rms(oh, w) * _silu(zh), o, z)


def _mix(gg, ta, tb):
    return _sigmoid(gg[:, :D_MODEL]) * ta + _sigmoid(gg[:, D_MODEL:]) * tb


def _gdn_post(y):
    a = _silu(y)
    q = _heads(lambda v: _l2n(v) * (HEAD_DIM ** -0.5), a[:, :HW])
    k = _heads(_l2n, a[:, HW:2 * HW])
    return q, k, a[:, 2 * HW:]


NN = (((1,), (0,)), ((), ()))
NT = (((1,), (1,)), ((), ()))
TN = (((0,), (0,)), ((), ()))


def _dg(a, b, dims):
    return lax.dot_general(a, b, dims, preferred_element_type=F32)


def _split2(x):
    hi = x.astype(BF16)
    return hi, (x - hi.astype(F32)).astype(BF16)


def _split3(x):
    hi = x.astype(BF16)
    r = x - hi.astype(F32)
    mid = r.astype(BF16)
    return hi, mid, (r - mid.astype(F32)).astype(BF16)


def _dg3(a, b, dims):
    ah, al = _split2(a)
    bh, bl = _split2(b)
    return _dg(ah, bh, dims) + (_dg(ah, bl, dims) + _dg(al, bh, dims))


BNN = (((2,), (1,)), ((0,), (0,)))
BNT = (((2,), (2,)), ((0,), (0,)))
BTN = (((1,), (1,)), ((0,), (0,)))


@jax.custom_vjp
def _mm3(a, b):
    return _dg3(a, b, BNN)


_mm3.defvjp(lambda a, b: (_dg3(a, b, BNN), (a, b)),
            lambda res, g: (_dg3(g, res[1], BNT), _dg3(res[0], g, BTN)))


def _xm(x, m, dims):
    mb = m.astype(BF16)
    parts = _split3(x)
    return _dg(parts[0], mb, dims) + (_dg(parts[1], mb, dims) + _dg(parts[2], mb, dims))


def _mx(m, x, dims):
    mb = m.astype(BF16)
    parts = _split3(x)
    return _dg(mb, parts[0], dims) + (_dg(mb, parts[1], dims) + _dg(mb, parts[2], dims))


@jax.custom_vjp
def _times_const(x, m):
    return _xm(x, m, NN)


_times_const.defvjp(lambda x, m: (_xm(x, m, NN), m),
                    lambda m, g: (_xm(g, m, NT), jnp.zeros_like(m)))


@jax.custom_vjp
def _const_times(m, x):
    return _mx(m, x, NN)


_const_times.defvjp(lambda m, x: (_mx(m, x, NN), m),
                    lambda m, g: (jnp.zeros_like(m), _mx(m, g, TN)))


@jax.custom_vjp
def _lane_mean_cols(x, avg):
    return _mx(avg, x, BNT)


_lane_mean_cols.defvjp(lambda x, avg: (_mx(avg, x, BNT), avg),
                       lambda avg, g: (_xm(g, avg, BTN), jnp.zeros_like(avg)))


def _gdn_gates(ab, alog, dtb, e_g, e_b):
    t = ab.shape[0]
    g = -jnp.exp(alog) * _softplus(ab + dtb)
    beta = _sigmoid(ab)
    ri = lax.broadcasted_iota(jnp.int32, (t, t), 0)
    ci = lax.broadcasted_iota(jnp.int32, (t, t), 1)
    shift = CHUNK.bit_length() - 1
    same = jnp.right_shift(ri, shift) == jnp.right_shift(ci, shift)
    tril = jnp.where(same & (ri >= ci), 1.0, 0.0).astype(F32)
    gc = _const_times(tril, g)
    return _times_const(gc, e_g), _times_const(beta, e_b)


def _shift_down(x, halo, s, i):
    if s == 0:
        return x
    halo = jnp.where(i == 0, 0.0, halo)
    xr = pltpu.roll(x, s, 0)
    hr = pltpu.roll(halo, s, 0)
    row = lax.broadcasted_iota(jnp.int32, (SUBLANES, x.shape[1]), 0)
    top = jnp.where(row < s, hr, xr[:SUBLANES])
    return jnp.concatenate([top, xr[SUBLANES:]], axis=0)


def _shift_up(x, halo, s, i, n):
    if s == 0:
        return x
    t = x.shape[0]
    halo = jnp.where(i == n - 1, 0.0, halo)
    xr = pltpu.roll(x, t - s, 0)
    hr = pltpu.roll(halo, SUBLANES - s, 0)
    row = lax.broadcasted_iota(jnp.int32, (SUBLANES, x.shape[1]), 0)
    bot = jnp.where(row >= SUBLANES - s, hr, xr[t - SUBLANES:])
    return jnp.concatenate([xr[:t - SUBLANES], bot], axis=0)


def _conv(pa, prev, cw_ref, i):
    y = pa * cw_ref[CONV_K - 1:CONV_K, :]
    for j in range(CONV_K - 1):
        y = y + _shift_down(pa, prev, CONV_K - 1 - j, i) * cw_ref[j:j + 1, :]
    return y


def _dot_nt(a, b, precision=None):
    return lax.dot_general(a, b, (((1,), (1,)), ((), ())), precision=precision, preferred_element_type=F32)


def _dot_tn(a, b, precision=None):
    return lax.dot_general(a, b, (((0,), (0,)), ((), ())), precision=precision, preferred_element_type=F32)


def _dot(a, b, precision=None):
    return jnp.dot(a, b, precision=precision, preferred_element_type=F32)


def _bf(x):
    return x.astype(BF16)


def _neumann_inverse(lmat):
    nb, c, _ = lmat.shape
    ri = lax.broadcasted_iota(jnp.int32, (nb, c, c), 1)
    ci = lax.broadcasted_iota(jnp.int32, (nb, c, c), 2)
    pw = -lmat
    inv = jnp.where(ri == ci, 1.0, 0.0).astype(F32) + pw
    for _ in range(5):
        pw = _mm3(pw, pw)
        inv = inv + _mm3(inv, pw)
    return inv


@jax.custom_vjp
def _unit_lower_inverse(lmat):
    return _neumann_inverse(lmat)


def _unit_lower_inverse_fwd(lmat):
    inv = _neumann_inverse(lmat)
    return inv, inv


def _unit_lower_inverse_bwd(inv, g):
    return (-_dg3(_dg3(inv, g, BTN), inv, BNT),)


_unit_lower_inverse.defvjp(_unit_lower_inverse_fwd, _unit_lower_inverse_bwd)


def _gdn_chunk(q, k, v, gc, bb):
    nb, c, _ = q.shape
    ri = lax.broadcasted_iota(jnp.int32, (nb, c, c), 1)
    ci = lax.broadcasted_iota(jnp.int32, (nb, c, c), 2)
    incl = ri >= ci
    strict = ri > ci
    g_row = gc[:, :, :c]
    g_col = _lane_mean_cols(gc, jnp.full((nb, c, LANES), 1.0 / LANES, F32))
    decay = jnp.where(incl, jnp.exp(jnp.where(incl, g_row - g_col, 0.0)), 0.0)
    kb = k * bb
    lmat = jnp.where(strict, _dg(_bf(kb), _bf(k), BNT) * decay, 0.0)
    inv = _unit_lower_inverse(lmat)
    egc = jnp.exp(gc)
    u = _mm3(inv, v * bb)
    w = _mm3(inv, kb * egc)
    aqk = _dg(_bf(q), _bf(k), BNT) * decay
    last = lax.broadcasted_iota(jnp.int32, (nb, c, LANES), 1) == c - 1
    tot = jnp.sum(jnp.where(last, gc, 0.0), axis=1, keepdims=True)
    k_tail = k * jnp.exp(tot - gc)
    tail = jnp.broadcast_to(jnp.exp(tot), (nb, SUBLANES, LANES))
    return u, w, aqk, q * egc, k_tail, tail


def _gdn_intra(qn, kn, vv, g_b, beta_b):
    t_rows = qn.shape[0]
    nc = t_rows // CHUNK
    cb = min(GDN_CB, nc)
    rows = cb * CHUNK
    col = pl.BlockSpec((rows, HEAD_DIM), lambda h, b: (b, h))

    def body(q_ref, k_ref, v_ref, g_ref, b_ref, u_ref, w_ref, a_ref, qd_ref, kt_ref, tl_ref):
        def group(gi, carry):
            r = pl.ds(pl.multiple_of(gi * (grp * CHUNK), grp * CHUNK), grp * CHUNK)
            ins = [ref[r, :].reshape(grp, CHUNK, HEAD_DIM) for ref in (q_ref, k_ref, v_ref, g_ref, b_ref)]
            u, w, aqk, qd, kt, tl = _gdn_chunk(*ins)
            for ref, val in ((u_ref, u), (w_ref, w), (qd_ref, qd), (kt_ref, kt)):
                ref[r, :] = val.reshape(grp * CHUNK, HEAD_DIM)
            a_ref[0, r, :] = aqk.reshape(grp * CHUNK, CHUNK)
            tl_ref[0, pl.ds(gi * grp, grp)] = tl
            return carry

        grp = min(GDN_GROUP, cb)
        lax.fori_loop(0, cb // grp, group, 0)

    full = jax.ShapeDtypeStruct((t_rows, HW), F32)
    return pl.pallas_call(
        body, name="gdn_intra_fwd",
        out_shape=[full, full, jax.ShapeDtypeStruct((HEADS, t_rows, CHUNK), F32), full, full,
                   jax.ShapeDtypeStruct((HEADS, nc, SUBLANES, LANES), F32)],
        grid=(HEADS, nc // cb),
        in_specs=[col] * 5,
        out_specs=[col, col, pl.BlockSpec((1, rows, CHUNK), lambda h, b: (h, b, 0)), col, col,
                   pl.BlockSpec((1, cb, SUBLANES, LANES), lambda h, b: (h, b, 0, 0))],
        compiler_params=_params(("parallel", "parallel")),
    )(qn, kn, vv, g_b, beta_b)


def _gdn_intra_bwd(qn, kn, vv, g_b, beta_b, du, dw, da, dqd, dkt, dtl):
    t_rows = qn.shape[0]
    nc = t_rows // CHUNK
    cb = min(GDN_CB, nc)
    rows = cb * CHUNK
    col = pl.BlockSpec((rows, HEAD_DIM), lambda h, b: (b, h))
    a_spec = pl.BlockSpec((1, rows, CHUNK), lambda h, b: (h, b, 0))
    tl_spec = pl.BlockSpec((1, cb, SUBLANES, LANES), lambda h, b: (h, b, 0, 0))

    def body(q_ref, k_ref, v_ref, g_ref, b_ref, du_ref, dw_ref, da_ref, dqd_ref, dkt_ref, dtl_ref,
             dq_ref, dk_ref, dv_ref, dg_ref, db_ref):
        def group(gi, carry):
            r = pl.ds(pl.multiple_of(gi * (grp * CHUNK), grp * CHUNK), grp * CHUNK)
            wide = (grp, CHUNK, HEAD_DIM)
            ins = [ref[r, :].reshape(wide) for ref in (q_ref, k_ref, v_ref, g_ref, b_ref)]
            cts = (du_ref[r, :].reshape(wide), dw_ref[r, :].reshape(wide),
                   da_ref[0, r, :].reshape(grp, CHUNK, CHUNK), dqd_ref[r, :].reshape(wide),
                   dkt_ref[r, :].reshape(wide), dtl_ref[0, pl.ds(gi * grp, grp)])
            grads = jax.vjp(_gdn_chunk, *ins)[1](cts)
            for ref, val in zip((dq_ref, dk_ref, dv_ref, dg_ref, db_ref), grads):
                ref[r, :] = val.reshape(grp * CHUNK, HEAD_DIM)
            return carry

        grp = min(GDN_GROUP, cb)
        lax.fori_loop(0, cb // grp, group, 0)

    full = jax.ShapeDtypeStruct((t_rows, HW), F32)
    return pl.pallas_call(
        body, name="gdn_intra_bwd",
        out_shape=[full] * 5,
        grid=(HEADS, nc // cb),
        in_specs=[col] * 7 + [a_spec, col, col, tl_spec],
        out_specs=[col] * 5,
        compiler_params=_params(("parallel", "parallel")),
    )(qn, kn, vv, g_b, beta_b, du, dw, da, dqd, dkt, dtl)


def _head_cols(h):
    return slice(h * HEAD_DIM, (h + 1) * HEAD_DIM)


def _gdn_scan(u, w, aqk, qd, kt, tl):
    t_rows = u.shape[0]
    nc = t_rows // CHUNK
    cb = min(GDN_CB, nc)
    rows = cb * CHUNK
    wide = pl.BlockSpec((rows, HW), lambda b: (b, 0))

    def body(u_ref, w_ref, a_ref, qd_ref, kt_ref, tl_ref, o_ref, s_out_ref, s_ref):
        @pl.when(pl.program_id(0) == 0)
        def _():
            s_ref[...] = jnp.zeros_like(s_ref)

        def chunk(ci, carry):
            r = pl.ds(pl.multiple_of(ci * CHUNK, CHUNK), CHUNK)
            for h in range(HEADS):
                hc = _head_cols(h)
                s = s_ref[h]
                s_out_ref[ci, h] = s
                sb = _bf(s)
                vn = u_ref[r, hc] - _dot(_bf(w_ref[r, hc]), sb)
                vnb = _bf(vn)
                o_ref[r, hc] = _dot(_bf(qd_ref[r, hc]), sb) + _dot(_bf(a_ref[h, r, :]), vnb)
                s_ref[h] = s * tl_ref[h, ci, 0:1, :] + _dot_tn(_bf(kt_ref[r, hc]), vnb)
            return carry

        lax.fori_loop(0, cb, chunk, 0)

    return pl.pallas_call(
        body, name="gdn_scan_fwd",
        out_shape=[jax.ShapeDtypeStruct((t_rows, HW), F32),
                   jax.ShapeDtypeStruct((nc, HEADS, HEAD_DIM, HEAD_DIM), F32)],
        grid=(nc // cb,),
        in_specs=[wide, wide, pl.BlockSpec((HEADS, rows, CHUNK), lambda b: (0, b, 0)), wide, wide,
                  pl.BlockSpec((HEADS, cb, SUBLANES, LANES), lambda b: (0, b, 0, 0))],
        out_specs=[wide, pl.BlockSpec((cb, HEADS, HEAD_DIM, HEAD_DIM), lambda b: (b, 0, 0, 0))],
        scratch_shapes=[pltpu.VMEM((HEADS, HEAD_DIM, HEAD_DIM), F32)],
        compiler_params=_params(("arbitrary",)),
    )(u, w, aqk, qd, kt, tl)


def _gdn_scan_bwd(do, u, w, aqk, qd, kt, tl, states):
    t_rows = u.shape[0]
    nc = t_rows // CHUNK
    cb = min(GDN_CB, nc)
    rows = cb * CHUNK
    nb = nc // cb
    wide = pl.BlockSpec((rows, HW), lambda b: (nb - 1 - b, 0))
    a_spec = pl.BlockSpec((HEADS, rows, CHUNK), lambda b: (0, nb - 1 - b, 0))
    tl_spec = pl.BlockSpec((HEADS, cb, SUBLANES, LANES), lambda b: (0, nb - 1 - b, 0, 0))

    def body(do_ref, u_ref, w_ref, a_ref, qd_ref, kt_ref, tl_ref, s_in_ref,
             du_ref, dw_ref, da_ref, dqd_ref, dkt_ref, dtl_ref, ds_ref):
        @pl.when(pl.program_id(0) == 0)
        def _():
            ds_ref[...] = jnp.zeros_like(ds_ref)

        row0 = lax.broadcasted_iota(jnp.int32, (SUBLANES, LANES), 0) == 0

        def chunk(step, carry):
            ci = cb - 1 - step
            r = pl.ds(pl.multiple_of(ci * CHUNK, CHUNK), CHUNK)
            for h in range(HEADS):
                hc = _head_cols(h)
                s = s_in_ref[ci, h]
                ds_next = ds_ref[h]
                sb, dsb = _bf(s), _bf(ds_next)
                wb, ab, ktb, qdb = _bf(w_ref[r, hc]), _bf(a_ref[h, r, :]), _bf(kt_ref[r, hc]), _bf(qd_ref[r, hc])
                dob = _bf(do_ref[r, hc])
                vn = u_ref[r, hc] - _dot(wb, sb)
                vnb = _bf(vn)
                dvn = _dot_tn(ab, dob) + _dot(ktb, dsb)
                dvnb = _bf(dvn)
                du_ref[r, hc] = dvn
                dw_ref[r, hc] = -_dot_nt(dvnb, sb)
                da_ref[h, r, :] = _dot_nt(dob, vnb)
                dqd_ref[r, hc] = _dot_nt(dob, sb)
                dkt_ref[r, hc] = _dot_nt(vnb, dsb)
                dtl_ref[h, ci] = jnp.where(row0, _colsum(s * ds_next), 0.0)
                ds_ref[h] = _dot_tn(qdb, dob) + ds_next * tl_ref[h, ci, 0:1, :] - _dot_tn(wb, dvnb)
            return carry

        lax.fori_loop(0, cb, chunk, 0)

    full = jax.ShapeDtypeStruct((t_rows, HW), F32)
    return pl.pallas_call(
        body, name="gdn_scan_bwd",
        out_shape=[full, full, jax.ShapeDtypeStruct((HEADS, t_rows, CHUNK), F32), full, full,
                   jax.ShapeDtypeStruct((HEADS, nc, SUBLANES, LANES), F32)],
        grid=(nb,),
        in_specs=[wide, wide, wide, a_spec, wide, wide, tl_spec,
                  pl.BlockSpec((cb, HEADS, HEAD_DIM, HEAD_DIM), lambda b: (nb - 1 - b, 0, 0, 0))],
        out_specs=[wide, wide, a_spec, wide, wide, tl_spec],
        scratch_shapes=[pltpu.VMEM((HEADS, HEAD_DIM, HEAD_DIM), F32)],
        compiler_params=_params(("arbitrary",)),
    )(do, u, w, aqk, qd, kt, tl, states)


def _att_rel_index():
    qi = lax.broadcasted_iota(jnp.int32, (ATT_QB, ATT_KW), 0)
    kj = lax.broadcasted_iota(jnp.int32, (ATT_QB, ATT_KW), 1)
    return jnp.clip(qi - kj + ATT_PAD, -(CHUNK - 1), MAX_REL) + (CHUNK - 1)


def _att_in_band():
    qi = lax.broadcasted_iota(jnp.int32, (ATT_QB, ATT_KW), 0)
    kj = lax.broadcasted_iota(jnp.int32, (ATT_QB, ATT_KW), 1)
    shift = CHUNK.bit_length() - 1
    qc = jnp.right_shift(qi, shift)
    kc = jnp.right_shift(kj, shift) - LEFT_CHUNKS
    return (kc <= qc) & (kc >= qc - LEFT_CHUNKS)


def _att_valid(b):
    kj = lax.broadcasted_iota(jnp.int32, (1, ATT_KW), 1)
    return jnp.where(kj + b * ATT_QB >= ATT_PAD, 0.0, NEG_INF)


def _att_block(q_raw, k_raw, v, qw, kw, bias, before_start):
    q = _rms(q_raw, qw)
    k = _rms(k_raw, kw)
    s = _dot_nt(_bf(q), _bf(k)) * (HEAD_DIM ** -0.5) + (bias + before_start)
    p = jnp.exp(s - lax.stop_gradient(jnp.max(s, axis=-1, keepdims=True)))
    p = p * (1.0 / jnp.sum(p, axis=-1, keepdims=True))
    return _dot(_bf(p), _bf(v))


def _att_specs():
    q_spec = pl.BlockSpec((ATT_QB, HEAD_DIM), lambda h, b: (b, h))
    k_specs = [pl.BlockSpec((ATT_QB, HEAD_DIM), lambda h, b, j=j: (b + j, HEADS + h)) for j in range(3)]
    v_specs = [pl.BlockSpec((ATT_QB, HEAD_DIM), lambda h, b, j=j: (b + j, 2 * HEADS + h)) for j in range(3)]
    w_spec = pl.BlockSpec((1, HEAD_DIM), lambda h, b: (0, 0))
    smem = pl.BlockSpec(memory_space=pltpu.SMEM)
    return q_spec, k_specs, v_specs, w_spec, smem


def _att_fill_bias(bias_ref, rel_ref, h):
    idx = _att_rel_index()

    def fill(r, acc):
        return jnp.where(idx == r, rel_ref[h, r], acc)

    table = lax.fori_loop(0, N_REL, fill, jnp.zeros((ATT_QB, ATT_KW), F32))
    bias_ref[...] = jnp.where(_att_in_band(), table, NEG_INF)


def _attention(pb, pbp, qw, kw, rel):
    t_rows = pb.shape[0]
    q_spec, k_specs, v_specs, w_spec, smem = _att_specs()

    def body(q_ref, k0, k1, k2, v0, v1, v2, qw_ref, kw_ref, rel_ref, o_ref, bias_ref):
        h, b = pl.program_id(0), pl.program_id(1)

        @pl.when(b == 0)
        def _():
            _att_fill_bias(bias_ref, rel_ref, h)

        kwin = jnp.concatenate([k0[...], k1[...], k2[...]], axis=0)
        vwin = jnp.concatenate([v0[...], v1[...], v2[...]], axis=0)
        o = _att_block(q_ref[...], kwin, vwin, qw_ref[...], kw_ref[...], bias_ref[...], _att_valid(b))
        o_ref[...] = o.astype(o_ref.dtype)

    return pl.pallas_call(
        body, name="band_attention_fwd",
        out_shape=jax.ShapeDtypeStruct((t_rows, HW), BF16),
        grid=(HEADS, t_rows // ATT_QB),
        in_specs=[q_spec] + k_specs + v_specs + [w_spec, w_spec, smem],
        out_specs=pl.BlockSpec((ATT_QB, HEAD_DIM), lambda h, b: (b, h)),
        scratch_shapes=[pltpu.VMEM((ATT_QB, ATT_KW), F32)],
        compiler_params=_params(("arbitrary", "arbitrary")),
    )(pb, pbp, pbp, pbp, pbp, pbp, pbp, qw, kw, rel)


def _attention_bwd(pb, pbp, qw, kw, rel, dyb):
    t_rows = pb.shape[0]
    nb = t_rows // ATT_QB
    q_spec, k_specs, v_specs, w_spec, smem = _att_specs()
    pad_rows = t_rows + ATT_PAD
    acc_spec = pl.BlockSpec((pad_rows, HEAD_DIM), lambda h, b: (0, h))

    def body(q_ref, k0, k1, k2, v0, v1, v2, qw_ref, kw_ref, rel_ref, do_ref,
             dq_ref, dk_ref, dv_ref, dqw_ref, dkw_ref, drel_ref, bias_ref, dbias_ref):
        h, b = pl.program_id(0), pl.program_id(1)

        @pl.when(b == 0)
        def _():
            _att_fill_bias(bias_ref, rel_ref, h)
            dbias_ref[...] = jnp.zeros_like(dbias_ref)
            dk_ref[...] = jnp.zeros_like(dk_ref)
            dv_ref[...] = jnp.zeros_like(dv_ref)

        @pl.when((b == 0) & (h == 0))
        def _():
            dqw_ref[...] = jnp.zeros_like(dqw_ref)
            dkw_ref[...] = jnp.zeros_like(dkw_ref)

        kwin = jnp.concatenate([k0[...], k1[...], k2[...]], axis=0)
        vwin = jnp.concatenate([v0[...], v1[...], v2[...]], axis=0)
        valid = _att_valid(b)
        _, vjp = jax.vjp(lambda q, k, v, a, c, bias: _att_block(q, k, v, a, c, bias, valid),
                         q_ref[...], kwin, vwin, qw_ref[...], kw_ref[...], bias_ref[...])
        dq, dk, dv, dqw, dkw, dbias = vjp(do_ref[...])
        dq_ref[...] = dq.astype(dq_ref.dtype)
        win = pl.ds(pl.multiple_of(b * ATT_QB, ATT_QB), ATT_KW)
        dk_ref[win, :] += dk
        dv_ref[win, :] += dv
        dqw_ref[...] += dqw
        dkw_ref[...] += dkw
        dbias_ref[...] += dbias

        @pl.when(b == nb - 1)
        def _():
            idx = _att_rel_index()
            tot = dbias_ref[...]

            def reduce(r, carry):
                drel_ref[h, r] = jnp.sum(jnp.where(idx == r, tot, 0.0))
                return carry

            lax.fori_loop(0, N_REL, reduce, 0)

    return pl.pallas_call(
        body, name="band_attention_bwd",
        out_shape=[jax.ShapeDtypeStruct((t_rows, HW), BF16),
                   jax.ShapeDtypeStruct((pad_rows, HW), F32), jax.ShapeDtypeStruct((pad_rows, HW), F32),
                   jax.ShapeDtypeStruct((1, HEAD_DIM), F32), jax.ShapeDtypeStruct((1, HEAD_DIM), F32),
                   jax.ShapeDtypeStruct((HEADS, N_REL), F32)],
        grid=(HEADS, nb),
        in_specs=[q_spec] + k_specs + v_specs + [w_spec, w_spec, smem, q_spec],
        out_specs=[q_spec, acc_spec, acc_spec, w_spec, w_spec, smem],
        scratch_shapes=[pltpu.VMEM((ATT_QB, ATT_KW), F32), pltpu.VMEM((ATT_QB, ATT_KW), F32)],
        compiler_params=_params(("arbitrary", "arbitrary")),
    )(pb, pbp, pbp, pbp, pbp, pbp, pbp, qw, kw, rel, dyb)


def _me():
    return lax.axis_index("x"), lax.axis_index("y"), lax.axis_index("c")


def _index(x, y, c):
    return 4 * x + 2 * y + c


HBM_SPEC = pl.BlockSpec(memory_space=pl.ANY)


def _block(ref, kind, d, r, c):
    if kind == "rows":
        return ref.at[pl.ds(d * r, r), :]
    if kind == "win":
        return ref.at[:, pl.ds(d * WIN_STEP, c)]
    return ref.at[:, pl.ds(d * c, c)]


def _all_gather(shards, kinds, n_gather):
    n = len(shards)

    def body(*refs):
        x_refs, out_refs = refs[:n], refs[n:2 * n]
        send_sems, recv_sems, local_sems = refs[2 * n:]
        x, y, c = _me()
        me, sibling = (x, y, c), (x, y, 1 - c)
        chips = [(1 - x, y), (x, 1 - y), (1 - x, 1 - y)]

        def copy(i, k, blk, to, src=None):
            r_, c_ = shards[i].shape
            dst = _block(out_refs[i], kinds[i], _index(*blk), r_, c_)
            return pltpu.make_async_remote_copy(
                src_ref=dst if src is None else src, dst_ref=dst,
                send_sem=send_sems.at[i, k], recv_sem=recv_sems.at[i, k], device_id=to, device_id_type=MESH)

        sends, local = [], []
        for i in range(n):
            r_, c_ = shards[i].shape
            mine = pltpu.make_async_copy(x_refs[i], _block(out_refs[i], kinds[i], _index(*me), r_, c_),
                                         local_sems.at[i])
            mine.start()
            local.append(mine)
            if i >= n_gather:
                continue
            first = [copy(i, 0, me, sibling, src=x_refs[i])]
            first += [copy(i, 1 + j, me, (*chip, c), src=x_refs[i]) for j, chip in enumerate(chips)]
            for cp in first:
                cp.start()
            sends += first
        for i in range(n_gather):
            for j, chip in enumerate(chips):
                copy(i, 1 + j, (*chip, c), me).wait_recv()
                passed = copy(i, 4 + j, (*chip, c), sibling)
                passed.start()
                sends.append(passed)
        for i in range(n_gather):
            copy(i, 0, sibling, me).wait_recv()
            for j, chip in enumerate(chips):
                copy(i, 4 + j, (*chip, 1 - c), me).wait_recv()
        for cp in sends:
            cp.wait_send()
        for cp in local:
            cp.wait()

    def full_shape(s, kind):
        r_, c_ = s.shape
        return (N_DEV * r_, c_) if kind == "rows" else (r_, N_DEV * c_)

    return pl.pallas_call(
        body, name="weights_all_gather",
        out_shape=[jax.ShapeDtypeStruct(full_shape(s, k), s.dtype) for s, k in zip(shards, kinds)],
        in_specs=[HBM_SPEC] * n, out_specs=[HBM_SPEC] * n,
        scratch_shapes=[pltpu.SemaphoreType.DMA((n_gather, 7)), pltpu.SemaphoreType.DMA((n_gather, 7)),
                        pltpu.SemaphoreType.DMA((n,))],
        compiler_params=pltpu.CompilerParams(has_side_effects=True),
    )(*shards)


SEM_SPEC = pl.BlockSpec(memory_space=pltpu.SEMAPHORE)
HBM_ONLY = pl.BlockSpec(memory_space=pltpu.HBM)
DATAFLOW = pltpu.SideEffectType.DATAFLOW_SIDE_EFFECTING


def _peers():
    x, y, c = _me()
    return [(x ^ (k >> 2), y ^ ((k >> 1) & 1), c ^ (k & 1)) for k in range(1, N_DEV)]


def _gather_copies(shapes, kinds):
    def make(src_refs, land_refs, send_sems, recv_sems):
        mine = _index(*_me())
        return [pltpu.make_async_remote_copy(
            src_ref=src_refs[i], dst_ref=_block(land_refs[i], kind, mine, r, c),
            send_sem=send_sems.at[7 * i + k], recv_sem=recv_sems.at[7 * i + k], device_id=peer, device_id_type=MESH)
            for i, ((r, c), kind) in enumerate(zip(shapes, kinds)) for k, peer in enumerate(_peers())]

    return make


def _exchange_copies(shapes, kinds):
    def make(src_refs, land_refs, send_sems, recv_sems):
        mine = _index(*_me())
        return [pltpu.make_async_remote_copy(
            src_ref=_block(src_refs[i], kind, _index(*peer), r, c), dst_ref=land_refs[i].at[mine],
            send_sem=send_sems.at[7 * i + k], recv_sem=recv_sems.at[7 * i + k], device_id=peer, device_id_type=MESH)
            for i, ((r, c), kind) in enumerate(zip(shapes, kinds)) for k, peer in enumerate(_peers())]

    return make


def _place_block(shard, kind, name):
    r, c = shard.shape
    tile = _row_tile(r, c)
    nt = r // tile
    full = (N_DEV * r, c) if kind == "rows" else (r, N_DEV * c)

    def body(me_ref, x_ref, out_ref):
        out_ref[...] = x_ref[...]

    if kind == "rows":
        out_spec = pl.BlockSpec((tile, c), lambda i, me: (me[0] * nt + i, 0))
    else:
        out_spec = pl.BlockSpec((tile, c), lambda i, me: (i, me[0]))
    return pl.pallas_call(
        body, name=name, out_shape=jax.ShapeDtypeStruct(full, shard.dtype),
        grid_spec=pltpu.PrefetchScalarGridSpec(
            num_scalar_prefetch=1, grid=(nt,),
            in_specs=[pl.BlockSpec((tile, c), lambda i, me: (i, 0))], out_specs=out_spec),
        compiler_params=_params(("arbitrary",)),
    )(_my_index_operand(), shard)


def _split_start(srcs, lands, make, name):
    n = len(srcs)

    def body(*refs):
        send_sems, recv_sems = refs[2 * n], refs[2 * n + 1]
        for cp in make(refs[:n], refs[n:2 * n], send_sems, recv_sems):
            cp.start()
        refs[-1][...] = jnp.zeros_like(refs[-1])

    arrays = list(srcs) + list(lands)
    out = pl.pallas_call(
        body, name=name,
        out_shape=(pltpu.SemaphoreType.DMA((7 * n,)), pltpu.SemaphoreType.DMA((7 * n,)),
                   *[pltpu.HBM(a.shape, a.dtype) for a in arrays], jax.ShapeDtypeStruct((SUBLANES, LANES), F32)),
        in_specs=[HBM_ONLY] * (2 * n),
        out_specs=(SEM_SPEC, SEM_SPEC, *[HBM_ONLY] * (2 * n), pl.BlockSpec(memory_space=pltpu.VMEM)),
        input_output_aliases={i: 2 + i for i in range(2 * n)},
        compiler_params=pltpu.CompilerParams(has_side_effects=DATAFLOW),
    )(*[pltpu.with_memory_space_constraint(a, pltpu.HBM) for a in arrays])
    return out[0], out[1], list(out[2:2 + n]), list(out[2 + n:2 + 2 * n]), out[-1]


def _split_wait(send_sems, recv_sems, srcs, lands, after, make, name):
    n = len(srcs)

    def body(*refs):
        for cp in make(refs[:n], refs[n:2 * n], refs[2 * n], refs[2 * n + 1]):
            cp.wait_send()
            cp.wait_recv()

    arrays = list(srcs) + list(lands)
    out = pl.pallas_call(
        body, name=name,
        out_shape=tuple(pltpu.HBM(a.shape, a.dtype) for a in arrays),
        in_specs=[HBM_ONLY] * (2 * n) + [SEM_SPEC, SEM_SPEC, pl.BlockSpec(memory_space=pl.ANY)],
        out_specs=tuple([HBM_ONLY] * (2 * n)),
        input_output_aliases={i: i for i in range(2 * n)},
        compiler_params=pltpu.CompilerParams(has_side_effects=DATAFLOW),
    )(*arrays, send_sems, recv_sems, after)
    return list(out[:n]), list(out[n:])


def _all_reduce_small(vals, name):
    rows, width = vals.shape

    def body(x_ref, out_ref, buf_ref, send_sems, recv_sems):
        x, y, c = _me()
        mine = _index(x, y, c)
        buf_ref[mine] = x_ref[...]
        copies = []
        for k in range(1, N_DEV):
            px, py, pc = x ^ (k >> 2), y ^ ((k >> 1) & 1), c ^ (k & 1)
            copies.append(pltpu.make_async_remote_copy(
                src_ref=x_ref, dst_ref=buf_ref.at[mine],
                send_sem=send_sems.at[k - 1], recv_sem=recv_sems.at[k - 1],
                device_id=(px, py, pc), device_id_type=MESH))
        for cp in copies:
            cp.start()
        for cp in copies:
            cp.wait()
        acc = buf_ref[0]
        for j in range(1, N_DEV):
            acc = acc + buf_ref[j]
        out_ref[...] = acc

    vmem = pl.BlockSpec(memory_space=pltpu.VMEM)
    return pl.pallas_call(
        body, name=name,
        out_shape=jax.ShapeDtypeStruct(vals.shape, F32),
        in_specs=[vmem], out_specs=vmem,
        scratch_shapes=[pltpu.VMEM((N_DEV, rows, width), F32),
                        pltpu.SemaphoreType.DMA((7,)), pltpu.SemaphoreType.DMA((7,))],
        compiler_params=pltpu.CompilerParams(has_side_effects=True),
    )(vals)


def _adamw_math(w, g, m, v):
    m = ADAM_B1 * m + (1.0 - ADAM_B1) * g
    v = ADAM_B2 * v + (1.0 - ADAM_B2) * (g * g)
    m_hat = m / (1.0 - ADAM_B1 ** ADAM_STEP)
    v_hat = v / (1.0 - ADAM_B2 ** ADAM_STEP)
    delta = -ADAM_LR * (m_hat / (jnp.sqrt(v_hat) + ADAM_EPS) + ADAM_WD * w)
    return delta, m, v


ROW_TILE_ELEMS = 384 * 1024


def _row_tile(rows, width):
    best = SUBLANES
    for t in range(SUBLANES, rows + 1, SUBLANES):
        if rows % t == 0 and t * width <= ROW_TILE_ELEMS:
            best = t
    return best


def _sum_received(r_ref, own, me):
    g = None
    for j in range(N_DEV):
        term = jnp.where(me == j, own, r_ref[j].astype(F32))
        g = term if g is None else g + term
    return g


def _my_index_operand():
    return _index(*_me()).astype(jnp.int32).reshape(1)


def _adamw_recv(recv, grad, kind, w, m, v, name):
    _, rows, width = recv.shape
    tile = _row_tile(rows, width)
    nt = rows // tile

    def body(me_ref, r_ref, own_ref, w_ref, m_ref, v_ref, g_out, d_out, m_out, v_out):
        g = _sum_received(r_ref, own_ref[...].astype(F32), me_ref[0])
        d, mn, vn = _adamw_math(w_ref[...], g, m_ref[...], v_ref[...])
        g_out[...] = g
        d_out[...] = d
        m_out[...] = mn
        v_out[...] = vn

    if kind == "rows":
        own_spec = pl.BlockSpec((tile, width), lambda i, me: (me[0] * nt + i, 0))
    else:
        own_spec = pl.BlockSpec((tile, width), lambda i, me: (i, me[0]))
    spec = pl.BlockSpec((tile, width), lambda i, me: (i, 0))
    shape = jax.ShapeDtypeStruct((rows, width), F32)
    return pl.pallas_call(
        body, name=name, out_shape=[shape] * 4,
        grid_spec=pltpu.PrefetchScalarGridSpec(
            num_scalar_prefetch=1, grid=(nt,),
            in_specs=[pl.BlockSpec((N_DEV, tile, width), lambda i, me: (0, i, 0)), own_spec, spec, spec, spec],
            out_specs=[spec] * 4),
        compiler_params=_params(("parallel",)),
    )(_my_index_operand(), recv, grad, w, m, v)


WIN_STEP = 1408
WIN_W = 1536
IN_SHARD = IN_COLS // N_DEV
IN_PADDED = WIN_STEP * (N_DEV - 1) + WIN_W


def _roll_w_in(shard_padded):
    rows = shard_padded.shape[0]
    tile = _row_tile(rows, WIN_W)

    def body(x_ref, main_ref, edge_ref):
        win = pltpu.roll(x_ref[...], 2 * _index(*_me()), 1).astype(BF16)
        main_ref[...] = win[:, :WIN_STEP]
        edge_ref[...] = win[:, WIN_STEP:]

    return pl.pallas_call(
        body, name="w_in_window",
        out_shape=[jax.ShapeDtypeStruct((rows, WIN_STEP), BF16), jax.ShapeDtypeStruct((rows, WIN_W - WIN_STEP), BF16)],
        grid=(rows // tile,),
        in_specs=[pl.BlockSpec((tile, WIN_W), lambda i: (i, 0))],
        out_specs=[pl.BlockSpec((tile, WIN_STEP), lambda i: (i, 0)),
                   pl.BlockSpec((tile, WIN_W - WIN_STEP), lambda i: (i, 0))],
        compiler_params=_params(("parallel",)),
    )(shard_padded)


def _sum_w_in_windows(recv, grad):
    _, rows, width = recv.shape
    tile = _row_tile(rows, width)

    def body(me_ref, r_ref, g_ref, g_out, own_ref, sem):
        me = me_ref[0]
        rows_i = pl.ds(pl.multiple_of(pl.program_id(0) * tile, tile), tile)
        own = pltpu.make_async_copy(g_ref.at[rows_i, pl.ds(pl.multiple_of(me * WIN_STEP, LANES), width)], own_ref, sem)
        own.start()
        own.wait()
        g_out[...] = pltpu.roll(_sum_received(r_ref, own_ref[...].astype(F32), me), width - 2 * me, 1)

    return pl.pallas_call(
        body, name="w_in_grad_sum", out_shape=jax.ShapeDtypeStruct((rows, width), F32),
        grid_spec=pltpu.PrefetchScalarGridSpec(
            num_scalar_prefetch=1, grid=(rows // tile,),
            in_specs=[pl.BlockSpec((N_DEV, tile, width), lambda i, me: (0, i, 0)), HBM_SPEC],
            out_specs=pl.BlockSpec((tile, width), lambda i, me: (i, 0)),
            scratch_shapes=[pltpu.VMEM((tile, width), BF16), pltpu.SemaphoreType.DMA]),
        compiler_params=_params(("arbitrary",)),
    )(_my_index_operand(), recv, grad)


def _adamw_small(w, g, m, v, name):
    def fn(i, n, w_, g_, m_, v_):
        return _adamw_math(w_, g_, m_, v_)

    r, c = w.shape
    return _rows(fn, [(w, "t"), (g, "t"), (m, "t"), (v, "t")], [], [(c, F32)] * 3, [], _row_tile(r, c), name)


def _norm_fwd(x, w, name):
    return _rows(lambda i, n, x_, w_: (_rms(x_, w_[...]),), [(x, "t")], [w], [(D_MODEL, BF16)], [], 512, name)[0]


def _residual_norm_fwd(x, y, scale, w, name):
    def fn(i, n, x_, y_, w_):
        xn = x_ + scale * y_
        return xn, _rms(xn, w_[...])

    return _rows(fn, [(x, "t"), (y, "t")], [w], [(D_MODEL, F32), (D_MODEL, BF16)], [], 512, name)


def _residual_norm_bwd(x, w, dhs, dres, scale, name):
    nh = len(dhs)

    def fn(i, n, x_, dres_, *rest):
        dh = rest[0]
        for extra in rest[1:nh]:
            dh = dh + extra
        _, vjp = jax.vjp(_rms, x_, rest[nh][...])
        dx, dw = vjp(dh)
        dx = dx + dres_
        return dx, scale * dx, dw

    return _rows(fn, [(x, "t"), (dres, "t")] + [(d, "t") for d in dhs], [w],
                 [(D_MODEL, F32), (D_MODEL, BF16)], [(1, D_MODEL)], 256, name)


def _ffn_fwd(h, w_gu, get_w_down, tag):
    gu = _matmul(h, w_gu, "nn", BF16, tag + "_gu")
    act = _rows(lambda i, n, gu_: (_swiglu(gu_),), [(gu, "t")], [], [(D_FF, BF16)], [], 128, tag + "_swiglu")[0]
    y = _matmul(act, get_w_down(act), "nn", F32, tag + "_down")
    return gu, act, y


def _ffn_bwd(h, gu, act, dy, w_gu, w_down, tag, comm, more=None):
    dact = _matmul(dy, w_down, "nt", BF16, tag + "_dact")

    def fn(i, n, gu_, dact_):
        _, vjp = jax.vjp(_swiglu, gu_)
        return vjp(dact_)

    dgu = _rows(fn, [(gu, "t"), (dact, "t")], [], [(2 * D_FF, BF16)], [], 128, tag + "_swiglu_bwd")[0]
    sent = comm.send(tag + "_gu", {tag + "_w_gu": _matmul(h, dgu, "tn", BF16, tag + "_d_w_gu")})
    sent = sent + comm.send(tag + "_down", {tag + "_w_down": _matmul(act, dy + sent.astype(BF16), "tn", BF16,
                                                                    tag + "_d_w_down"), **(more or {})})
    dh = _matmul(dgu, w_gu, "nt", F32, tag + "_dh")
    return dh, sent


def _expanders():
    e_g = np.zeros((LANES, HW), np.float32)
    e_b = np.zeros((LANES, HW), np.float32)
    for h in range(HEADS):
        e_g[h, h * HEAD_DIM:(h + 1) * HEAD_DIM] = 1.0
        e_b[HEADS + h, h * HEAD_DIM:(h + 1) * HEAD_DIM] = 1.0
    return jnp.asarray(e_g), jnp.asarray(e_b)


def _pad_lanes(v):
    return jnp.pad(v, ((0, 0), (0, LANES - v.shape[1])))


class _LocalWeights:
    def __init__(self, big):
        self.big, self.sent = big, {}

    def arrive(self, group, after):
        return self.big

    def send(self, group, grads):
        self.sent.update(grads)
        return jnp.zeros((), F32)


def _local_step(x, p, tgt, small, comm):
    e_g, e_b = _expanders()
    alog, dtb = _pad_lanes(small["a_log"]), _pad_lanes(small["dt_bias"])
    conv_w = jnp.pad(small["conv_w"], ((0, SUBLANES - CONV_K), (0, 0)))
    rel = small["rel_bias"]

    h1 = _norm_fwd(x, small["ffn1_norm"], "ffn1_norm")
    big = dict(comm.arrive("ffn1", h1))
    if "_token" in big:
        h1 = h1 + big.pop("_token").astype(BF16)

    def ffn1_w_down(act):
        big.update(comm.arrive("ffn1_down", act))
        return big["ffn1_w_down"]

    gu1, act1, y1 = _ffn_fwd(h1, big["ffn1_w_gu"], ffn1_w_down, "ffn1")
    x1, h2 = _residual_norm_fwd(x, y1, 0.5, small["mix_norm"], "mix_norm")

    big = {**big, **comm.arrive("mixer", h2)}
    w_in = big["w_in"]
    w_qz = w_in[:, :IN_QZ]
    w_ab = jnp.pad(w_in[:, IN_AB0:IN_QKVB0], ((0, 0), (0, LANES - 2 * HEADS)))
    w_qkvb = w_in[:, IN_QKVB0:IN_GG0]
    w_gg = w_in[:, IN_GG0:IN_COLS]
    qz = _matmul(h2, w_qz, "nn", F32, "in_qz")
    ab = _matmul(h2, w_ab, "nn", F32, "in_ab")
    pb = _matmul(h2, w_qkvb, "nn", F32, "in_qkvb")
    gg = _matmul(h2, w_gg, "nn", BF16, "in_gates")
    pa, z = qz[:, :3 * HW], qz[:, 3 * HW:]

    def prep(i, n, pa_, prev_, ab_, cw_, alog_, dtb_, eg_, eb_):
        q, k, v = _gdn_post(_conv(pa_, prev_, cw_, i))
        g_b, beta_b = _gdn_gates(ab_, alog_[...], dtb_[...], eg_[...], eb_[...])
        return q, k, v, g_b, beta_b

    qn, kn, vv, g_b, beta_b = _rows(prep, [(pa, "t"), (pa, "p"), (ab, "t")], [conv_w, alog, dtb, e_g, e_b],
                                    [(HW, F32)] * 5, [], 256, "gdn_prep")
    u, w, aqk, qd, kt, tl = _gdn_intra(qn, kn, vv, g_b, beta_b)
    o, states = _gdn_scan(u, w, aqk, qd, kt, tl)
    ya = _rows(lambda i, n, o_, z_, w_: (_gated_norm(o_, z_, w_[...]),), [(o, "t"), (z, "t")], [small["gdn_norm"]],
               [(HW, BF16)], [], 512, "gdn_gated_norm")[0]

    pbp = jnp.pad(pb, ((ATT_PAD, 0), (0, 0)))
    yb = _attention(pb, pbp, small["q_norm"], small["k_norm"], rel)

    ta = _matmul(ya, big["w_branch_a"], "nn", BF16, "branch_a")
    tb = _matmul(yb, big["w_branch_b"], "nn", BF16, "branch_b")
    mixed = _rows(lambda i, n, gg_, ta_, tb_: (_mix(gg_, ta_, tb_),), [(gg, "t"), (ta, "t"), (tb, "t")], [],
                  [(D_MODEL, BF16)], [], 256, "mix")[0]
    m_out = _matmul(mixed, big["w_out"], "nn", F32, "w_out")
    x2, h3 = _residual_norm_fwd(x1, m_out, 1.0, small["ffn2_norm"], "ffn2_norm")
    big = {**big, **comm.arrive("tail", h3)}
    gu2, act2, y2 = _ffn_fwd(h3, big["ffn2_w_gu"], lambda act: big["ffn2_w_down"], "ffn2")
    x3, h4 = _residual_norm_fwd(x2, y2, 0.5, small["ple_norm"], "ple_norm")
    gp = _matmul(h4, big["ple_gate"], "nn", BF16, "ple_gate")
    pp = _matmul(p, big["ple_proj"], "nn", BF16, "ple_proj")

    def head(i, n, x3_, gp_, pp_, tgt_):
        sg = _sigmoid(gp_)
        err = x3_ + sg * pp_ - tgt_
        dx4 = err * (1.0 / D_MODEL)
        sq = _colsum(err * err)
        part = sq[:, :LANES]
        for j in range(1, D_MODEL // LANES):
            part = part + sq[:, j * LANES:(j + 1) * LANES]
        return dx4, dx4 * pp_ * sg * (1.0 - sg), dx4 * sg, (0.5 / D_MODEL) * part

    dx4, dgp, dpp, loss_lanes = _rows(head, [(x3, "t"), (gp, "t"), (pp, "t"), (tgt, "t")], [],
                                      [(D_MODEL, F32), (D_MODEL, BF16), (D_MODEL, BF16)], [(1, LANES)], 256,
                                      "ple_loss_head")
    loss = jnp.sum(loss_lanes)

    gbig, gsmall = {}, {}
    gbig["ple_proj"] = _matmul(p, dpp, "tn", BF16, "d_ple_proj")
    gbig["ple_gate"] = _matmul(h4, dgp, "tn", BF16, "d_ple_gate")
    dh4 = _matmul(dgp, big["ple_gate"], "nt", F32, "ple_gate_dh")
    dx3, dy2, gsmall["ple_norm"] = _residual_norm_bwd(x3, small["ple_norm"], [dh4], dx4, 0.5, "ple_norm_bwd")

    dh3, sent = _ffn_bwd(h3, gu2, act2, dy2, big["ffn2_w_gu"], big["ffn2_w_down"], "ffn2", comm,
                         {n: gbig[n] for n in ("ple_proj", "ple_gate")})
    dx2, dx2b, gsmall["ffn2_norm"] = _residual_norm_bwd(x2, small["ffn2_norm"] + sent, [dh3], dx3, 1.0,
                                                        "ffn2_norm_bwd")

    gbig["w_out"] = _matmul(mixed, dx2b, "tn", BF16, "d_w_out")
    dmixed = _matmul(dx2b, big["w_out"], "nt", BF16, "w_out_dx")

    def mix_bwd(i, n, gg_, ta_, tb_, dm_):
        _, vjp = jax.vjp(_mix, gg_, ta_, tb_)
        return vjp(dm_)

    dgg, dta, dtb_ = _rows(mix_bwd, [(gg, "t"), (ta, "t"), (tb, "t"), (dmixed, "t")], [],
                           [(2 * D_MODEL, BF16), (D_MODEL, BF16), (D_MODEL, BF16)], [], 256, "mix_bwd")
    gbig["w_branch_a"] = _matmul(ya, dta, "tn", BF16, "d_branch_a")
    gbig["w_branch_b"] = _matmul(yb, dtb_, "tn", BF16, "d_branch_b")
    dya = _matmul(dta, big["w_branch_a"], "nt", F32, "branch_a_dx")
    dyb = _matmul(dtb_, big["w_branch_b"], "nt", F32, "branch_b_dx")

    dq_b, dk_b, dv_b, gsmall["q_norm"], gsmall["k_norm"], gsmall["rel_bias"] = _attention_bwd(
        pb, pbp, small["q_norm"], small["k_norm"], rel, dyb)
    dpb = jnp.concatenate([dq_b, dk_b[ATT_PAD:].astype(BF16), dv_b[ATT_PAD:].astype(BF16)], axis=1)

    def gated_bwd(i, n, o_, z_, dya_, w_):
        _, vjp = jax.vjp(_gated_norm, o_, z_, w_[...])
        return vjp(dya_)

    do, dz, gsmall["gdn_norm"] = _rows(gated_bwd, [(o, "t"), (z, "t"), (dya, "t")], [small["gdn_norm"]],
                                       [(HW, F32), (HW, BF16)], [(1, HEAD_DIM)], 256, "gdn_gated_norm_bwd")
    du, dw, da, dqd, dkt, dtl = _gdn_scan_bwd(do, u, w, aqk, qd, kt, tl, states)
    dqn, dkn, dvv, dg_b, dbeta_b = _gdn_intra_bwd(qn, kn, vv, g_b, beta_b, du, dw, da, dqd, dkt, dtl)

    def prep_bwd(i, n, pa_, prev_, ab_, dq_, dk_, dv_, dg_, db_, cw_, alog_, dtb_, eg_, eb_):
        _, vjp = jax.vjp(_gdn_post, _conv(pa_, prev_, cw_, i))
        (dy,) = vjp((dq_, dk_, dv_))
        e_g_, e_b_ = eg_[...], eb_[...]
        _, vjp_g = jax.vjp(lambda a, b, c: _gdn_gates(a, b, c, e_g_, e_b_), ab_, alog_[...], dtb_[...])
        dab, dalog, ddtb = vjp_g((dg_, db_))
        return dy, dab, dalog, ddtb

    dy_conv, dab, dalog, ddtb = _rows(
        prep_bwd, [(pa, "t"), (pa, "p"), (ab, "t"), (dqn, "t"), (dkn, "t"), (dvv, "t"), (dg_b, "t"), (dbeta_b, "t")],
        [conv_w, alog, dtb, e_g, e_b], [(3 * HW, F32), (LANES, BF16)], [(1, LANES), (1, LANES)], 256,
        "gdn_prep_bwd")
    gsmall["a_log"] = dalog[:, :HEADS]
    gsmall["dt_bias"] = ddtb[:, :HEADS]

    def conv_bwd(i, n, dy_, nxt_, pa_, prev_, cw_):
        dpa = dy_ * cw_[CONV_K - 1:CONV_K, :]
        row = lax.broadcasted_iota(jnp.int32, (SUBLANES, dy_.shape[1]), 0)
        dcw = jnp.where(row == CONV_K - 1, _colsum(dy_ * pa_), 0.0)
        for j in range(CONV_K - 1):
            s = CONV_K - 1 - j
            dpa = dpa + _shift_up(dy_, nxt_, s, i, n) * cw_[j:j + 1, :]
            dcw = dcw + jnp.where(row == j, _colsum(dy_ * _shift_down(pa_, prev_, s, i)), 0.0)
        return dpa, dcw

    dpa, dcw = _rows(conv_bwd, [(dy_conv, "t"), (dy_conv, "n"), (pa, "t"), (pa, "p")], [conv_w],
                     [(3 * HW, BF16)], [(SUBLANES, 3 * HW)], 256, "gdn_conv_bwd")
    gsmall["conv_w"] = dcw[:CONV_K]

    dqz = jnp.concatenate([dpa, dz], axis=1)
    d_w_qz = _matmul(h2, dqz, "tn", BF16, "d_in_qz")
    d_w_ab = _matmul(h2, dab, "tn", BF16, "d_in_ab")
    d_w_qkvb = _matmul(h2, dpb, "tn", BF16, "d_in_qkvb")
    d_w_gg = _matmul(h2, dgg, "tn", BF16, "d_in_gates")
    gbig["w_in"] = jnp.concatenate([d_w_qz, d_w_ab[:, :2 * HEADS], d_w_qkvb, d_w_gg,
                                    jnp.zeros((D_MODEL, IN_PADDED - IN_COLS), BF16)], axis=1)
    dh2 = [_matmul(dqz, w_qz, "nt", F32, "in_qz_dh"), _matmul(dab, w_ab, "nt", F32, "in_ab_dh"),
           _matmul(dpb, w_qkvb, "nt", F32, "in_qkvb_dh"), _matmul(dgg, w_gg, "nt", F32, "in_gates_dh")]
    sent = comm.send("mixer", {n: gbig[n] for n in ("w_out", "w_branch_b", "w_branch_a", "w_in")})
    dx1, dy1, gsmall["mix_norm"] = _residual_norm_bwd(x1, small["mix_norm"] + sent, dh2, dx2, 0.5, "mix_norm_bwd")

    dh1, sent = _ffn_bwd(h1, gu1, act1, dy1, big["ffn1_w_gu"], big["ffn1_w_down"], "ffn1", comm)
    grad_x, _, gsmall["ffn1_norm"] = _residual_norm_bwd(x, small["ffn1_norm"] + sent, [dh1], dx1, 1.0,
                                                        "ffn1_norm_bwd")
    return loss, grad_x, gsmall


GATHER_GROUPS = {"ffn1": ("ffn1_w_gu",),
                 "ffn1_down": ("ffn1_w_down",),
                 "mixer": ("w_in_main", "w_in_edge", "w_branch_a", "w_branch_b", "w_out"),
                 "tail": ("ffn2_w_gu", "ffn2_w_down", "ple_gate", "ple_proj")}


def _kind(name):
    return "cols" if name in COL_SHARDED or name.startswith("w_in_") else "rows"


def _merge_w_in(main, edges):
    edge_w = WIN_W - WIN_STEP
    w_in = jnp.pad(main, ((0, 0), (0, edge_w)))
    for d in range(N_DEV):
        at = WIN_STEP * (d + 1)
        w_in = w_in + jnp.pad(edges[:, d * edge_w:(d + 1) * edge_w], ((0, 0), (at, IN_PADDED - at - edge_w)))
    return w_in


class _Fsdp:
    def __init__(self, wts, first):
        self.wts, self.first_token = wts, first
        main, edge = _roll_w_in(jnp.pad(wts["w_in"], ((0, 0), (0, WIN_W - IN_SHARD))))
        self.shards = {n: wts[n].astype(BF16) for n in BIG if n not in ("w_in", "ffn1_w_gu")}
        self.shards.update(w_in_main=main, w_in_edge=edge)
        self.lands = {n: _place_block(self.shards[n], _kind(n), "own_" + n)
                      for group in ("ffn1_down", "mixer", "tail") for n in GATHER_GROUPS[group]}
        self.flight, self.sent = {}, {}

    def _gather_first(self, after):
        token = self.first_token + after[0, 0].astype(F32) * 0.0
        me = _index(*_me())
        for n, land in self.lands.items():
            r, c = self.shards[n].shape
            at = (me * r, 0) if _kind(n) == "rows" else (0, me * c)
            token = token + lax.dynamic_slice(land, at, (1, 1))[0, 0].astype(F32) * 0.0
        shard = (self.wts["ffn1_w_gu"] + token).astype(BF16)
        self.shards["ffn1_w_gu"] = shard
        first = _all_gather([shard], [_kind("ffn1_w_gu")], 1)[0]
        token = first[0, 0].astype(F32) * 0.0
        for group in ("ffn1_down", "mixer", "tail"):
            names = GATHER_GROUPS[group]
            srcs = [self.shards[n] for n in names]
            lands = [self.lands[n] for n in names]
            make = _gather_copies([s.shape for s in srcs], [_kind(n) for n in names])
            srcs[0] = srcs[0] + token.astype(BF16)
            send_sems, recv_sems, srcs, lands, tok = _split_start(srcs, lands, make, "gather_start_" + group)
            token = token + tok[0, 0]
            self.flight[group] = (send_sems, recv_sems, srcs, lands, make)
        return {"ffn1_w_gu": first, "_token": token}

    def arrive(self, group, after):
        if group == "ffn1":
            return self._gather_first(after)
        send_sems, recv_sems, srcs, lands, make = self.flight[group]
        _, lands = _split_wait(send_sems, recv_sems, srcs, lands, after, make, "gather_wait_" + group)
        full = dict(zip(GATHER_GROUPS[group], lands))
        if group == "mixer":
            full["w_in"] = _merge_w_in(full.pop("w_in_main"), full.pop("w_in_edge"))
        return full

    def send(self, group, grads):
        names = list(grads)
        kinds = ["win" if n == "w_in" else _kind(n) for n in names]
        shapes = [(D_MODEL, WIN_W) if n == "w_in" else self.shards[n].shape for n in names]
        srcs = [grads[n] for n in names]
        lands = [lax.empty((N_DEV,) + tuple(s), BF16) for s in shapes]
        make = _exchange_copies(shapes, kinds)
        send_sems, recv_sems, srcs, lands, tok = _split_start(srcs, lands, make, "grads_start_" + group)
        self.sent[group] = (names, kinds, send_sems, recv_sems, srcs, lands, make)
        return tok[0, 0]

    def received(self, group, after):
        names, kinds, send_sems, recv_sems, srcs, lands, make = self.sent[group]
        srcs, lands = _split_wait(send_sems, recv_sems, srcs, lands, after, make, "grads_wait_" + group)
        return {n: (k, g, r) for n, k, g, r in zip(names, kinds, srcs, lands)}


SMALL_ROWS = ("ffn1_norm", "mix_norm", "ffn2_norm", "ple_norm", "gdn_norm", "q_norm", "k_norm", "a_log", "dt_bias",
              "rel_bias", "conv_w")


def _pack_small(vals):
    rows = []
    for n in SMALL_ROWS:
        v = vals[n]
        if n == "rel_bias":
            v = jnp.pad(v, ((0, 0), (0, 2 * LANES - N_REL)))
        elif n in ("a_log", "dt_bias"):
            v = _pad_lanes(v)
        rows.append(v.reshape(-1, LANES))
    packed = jnp.concatenate(rows, axis=0)
    return jnp.pad(packed, ((0, -packed.shape[0] % SUBLANES), (0, 0)))


def _unpack_small(packed, shapes):
    out, off = {}, 0
    for n in SMALL_ROWS:
        shp = shapes[n]
        if n == "rel_bias":
            out[n] = packed[off:off + 2 * HEADS].reshape(HEADS, 2 * LANES)[:, :N_REL]
            off += 2 * HEADS
        elif n in ("a_log", "dt_bias"):
            out[n] = packed[off:off + 1, :HEADS]
            off += 1
        else:
            r = int(np.prod(shp)) // LANES
            out[n] = packed[off:off + r].reshape(shp)
            off += r
    return out


WEIGHTS = ("ffn1_norm", "ffn1_w_gu", "ffn1_w_down", "mix_norm", "w_in", "conv_w", "a_log", "dt_bias", "gdn_norm",
           "q_norm", "k_norm", "rel_bias", "w_branch_a", "w_branch_b", "w_out", "ffn2_norm", "ffn2_w_gu",
           "ffn2_w_down", "ple_norm", "ple_gate", "ple_proj")


def kernel(x, p, ffn1_norm, ffn1_w_gu, ffn1_w_down, mix_norm, w_in, conv_w, a_log, dt_bias, gdn_norm, q_norm, k_norm, rel_bias, w_branch_a, w_branch_b, w_out, ffn2_norm, ffn2_w_gu, ffn2_w_down, ple_norm, ple_gate, ple_proj, loss_target, m_ffn1_norm, m_ffn1_w_gu, m_ffn1_w_down, m_mix_norm, m_w_in, m_conv_w, m_a_log, m_dt_bias, m_gdn_norm, m_q_norm, m_k_norm, m_rel_bias, m_w_branch_a, m_w_branch_b, m_w_out, m_ffn2_norm, m_ffn2_w_gu, m_ffn2_w_down, m_ple_norm, m_ple_gate, m_ple_proj, v_ffn1_norm, v_ffn1_w_gu, v_ffn1_w_down, v_mix_norm, v_w_in, v_conv_w, v_a_log, v_dt_bias, v_gdn_norm, v_q_norm, v_k_norm, v_rel_bias, v_w_branch_a, v_w_branch_b, v_w_out, v_ffn2_norm, v_ffn2_w_gu, v_ffn2_w_down, v_ple_norm, v_ple_gate, v_ple_proj):
    args = dict(locals())
    def layer0(v):
        return v[0] if v.ndim == 3 else v

    wts = {n: layer0(args[n]) for n in WEIGHTS}
    mom = {n: layer0(args["m_" + n]) for n in WEIGHTS}
    var = {n: layer0(args["v_" + n]) for n in WEIGHTS}
    x2d, p2d, tgt = x[0], p[0, 0], loss_target[0]
    my_index = _index(*_me())

    small = {n: wts[n] for n in SMALL_ROWS if n != "conv_w"}
    conv_shard = wts["conv_w"]
    conv_cols = conv_shard.shape[1]
    conv_packed = jnp.zeros((SUBLANES, N_DEV * conv_cols), F32)
    conv_packed = lax.dynamic_update_slice(conv_packed, jnp.pad(conv_shard, ((0, SUBLANES - CONV_K), (0, 0))),
                                           (0, my_index * conv_cols))
    small["conv_w"] = _all_reduce_small(conv_packed.reshape(-1, LANES), "conv_w_gather").reshape(SUBLANES, -1)[:CONV_K]

    fsdp = _Fsdp(wts, small["conv_w"][0, 0] * 0.0)

    loss, grad_x, gsmall = _local_step(x2d, p2d, tgt, small, fsdp)
    loss = lax.psum(loss, ("x", "y", "c"))

    outs_big, after = {}, grad_x
    for group in list(fsdp.sent):
        for n, (kind, grad, recv) in fsdp.received(group, after).items():
            if n == "w_in":
                g_in = _sum_w_in_windows(recv, grad)[:, :IN_SHARD]
                outs_big[n] = [g_in] + list(_adamw_small(wts[n], g_in, mom[n], var[n], "adamw_w_in"))
            else:
                outs_big[n] = _adamw_recv(recv, grad, kind, wts[n], mom[n], var[n], "adamw_" + n)
            after = outs_big[n][1]

    small_shapes = {n: (small[n].shape if n != "conv_w" else (CONV_K, N_DEV * conv_cols)) for n in SMALL_ROWS}
    gsum = _unpack_small(_all_reduce_small(_pack_small(gsmall), "small_grads_all_reduce"), small_shapes)
    gsum["conv_w"] = lax.dynamic_slice(gsum["conv_w"], (0, my_index * conv_cols), (CONV_K, conv_cols))
    rep = [n for n in SMALL_ROWS if n != "conv_w"]
    rep_shapes = {n: small_shapes[n] for n in rep}

    def pack_rep(vals):
        return _pack_small({**{n: vals[n] for n in rep}, "conv_w": jnp.zeros((CONV_K, LANES), F32)})

    def unpack_rep(packed):
        return _unpack_small(packed, {**rep_shapes, "conv_w": (CONV_K, LANES)})

    outs_small = [unpack_rep(o) for o in _adamw_small(pack_rep(wts), pack_rep(gsum), pack_rep(mom), pack_rep(var),
                                                      "adamw_replicated")]
    pad8 = functools.partial(jnp.pad, pad_width=((0, SUBLANES - CONV_K), (0, 0)))
    outs_conv = [o[:CONV_K] for o in _adamw_small(pad8(conv_shard), pad8(gsum["conv_w"]), pad8(mom["conv_w"]),
                                                   pad8(var["conv_w"]), "adamw_conv")]

    def leaf(kind, n):
        if n in BIG:
            return outs_big[n][kind][None]
        if n == "conv_w":
            return (gsum["conv_w"] if kind == 0 else outs_conv[kind - 1])[None]
        return (gsum[n] if kind == 0 else outs_small[kind - 1][n]).reshape(args[n].shape)

    result = [loss, grad_x[None]]
    for kind in range(4):
        result += [leaf(kind, n) for n in WEIGHTS]
    return tuple(result)
```

```python
import functools

import numpy as np
import jax
import jax.numpy as jnp
from jax import lax
from jax.experimental import pallas as pl
from jax.experimental.pallas import tpu as pltpu

F32 = jnp.float32
BF16 = jnp.bfloat16
HIGHEST = lax.Precision.HIGHEST
MESH = pl.DeviceIdType.MESH

D_MODEL = 2048
D_FF = 5632
HEADS = 8
HEAD_DIM = 128
HW = HEADS * HEAD_DIM
CHUNK = 64
LEFT_CHUNKS = 8
MAX_REL = 128
N_REL = (CHUNK - 1) + MAX_REL + 1
CONV_K = 4
EPS = 1e-6
NEG_INF = -1e30
N_DEV = 8
LANES = 128
SUBLANES = 8
VMEM_LIMIT = 56 * 1024 * 1024

MATMUL_WHOLE_K = 2048

ATT_QB = 256
ATT_KW = ATT_QB + LEFT_CHUNKS * CHUNK
ATT_PAD = LEFT_CHUNKS * CHUNK
GDN_CB = 8
GDN_GROUP = 8

ADAM_LR = 0.001
ADAM_B1 = 0.9
ADAM_B2 = 0.999
ADAM_EPS = 1e-08
ADAM_WD = 0.01
ADAM_STEP = 10

IN_QZ = 3 * HW + HW
IN_AB0 = IN_QZ
IN_QKVB0 = IN_AB0 + 2 * HEADS
IN_GG0 = IN_QKVB0 + 3 * HW
IN_COLS = IN_GG0 + 2 * D_MODEL

BIG = ("ffn1_w_gu", "ffn1_w_down", "w_in", "w_branch_a", "w_branch_b", "w_out",
       "ffn2_w_gu", "ffn2_w_down", "ple_gate", "ple_proj")
COL_SHARDED = ("ffn1_w_gu", "w_in", "w_branch_a", "w_branch_b", "ffn2_w_gu", "ple_proj")


def _params(semantics=None, **kw):
    return pltpu.CompilerParams(dimension_semantics=semantics, vmem_limit_bytes=VMEM_LIMIT, **kw)


def _pick(n, cands):
    for c in cands:
        if n % c == 0:
            return c
    return n


def _matmul(a, b, mode, out_dtype, name):
    if mode == "nn":
        (m, k), (k2, n) = a.shape, b.shape
    elif mode == "nt":
        (m, k), (n, k2) = a.shape, b.shape
    else:
        (k, m), (k2, n) = a.shape, b.shape
    assert k == k2, (a.shape, b.shape, mode)
    tm = _pick(m, (1024, 512, 256, 128))
    tn = _pick(n, (1024, 512, 256, 128))
    tk = k if k <= MATMUL_WHOLE_K else _pick(k, (2816, 2048, 1536, 1024, 512, 256, 128))
    nk = k // tk
    if mode == "nn":
        a_spec = pl.BlockSpec((tm, tk), lambda i, j, kk: (i, kk))
        b_spec = pl.BlockSpec((tk, tn), lambda i, j, kk: (kk, j))
        dims = (((1,), (0,)), ((), ()))
    elif mode == "nt":
        a_spec = pl.BlockSpec((tm, tk), lambda i, j, kk: (i, kk))
        b_spec = pl.BlockSpec((tn, tk), lambda i, j, kk: (j, kk))
        dims = (((1,), (1,)), ((), ()))
    else:
        a_spec = pl.BlockSpec((tk, tm), lambda i, j, kk: (kk, i))
        b_spec = pl.BlockSpec((tk, tn), lambda i, j, kk: (kk, j))
        dims = (((0,), (0,)), ((), ()))

    def body(a_ref, b_ref, o_ref, *acc):
        prod = lax.dot_general(a_ref[...].astype(BF16), b_ref[...].astype(BF16), dims, preferred_element_type=F32)
        if nk == 1:
            o_ref[...] = prod.astype(o_ref.dtype)
            return
        acc_ref, kk = acc[0], pl.program_id(2)

        @pl.when(kk == 0)
        def _():
            acc_ref[...] = prod

        @pl.when((kk > 0) & (kk < nk - 1))
        def _():
            acc_ref[...] += prod

        @pl.when(kk == nk - 1)
        def _():
            o_ref[...] = (acc_ref[...] + prod).astype(o_ref.dtype)

    return pl.pallas_call(
        body, name=name,
        out_shape=jax.ShapeDtypeStruct((m, n), out_dtype),
        grid=(m // tm, n // tn, nk),
        in_specs=[a_spec, b_spec],
        out_specs=pl.BlockSpec((tm, tn), lambda i, j, kk: (i, j)),
        scratch_shapes=[pltpu.VMEM((tm, tn), F32)] if nk > 1 else [],
        compiler_params=_params(("parallel", "parallel", "arbitrary")),
    )(a, b)


def _rows(fn, row_ins, consts, row_outs, acc_outs, tile, name):
    t_rows = row_ins[0][0].shape[0]
    tile = min(tile, t_rows)
    assert t_rows % tile == 0 and tile % SUBLANES == 0
    n = t_rows // tile
    per = tile // SUBLANES
    last8 = t_rows // SUBLANES - 1
    in_specs = []
    for arr, kind in row_ins:
        c = arr.shape[1]
        if kind == "t":
            in_specs.append(pl.BlockSpec((tile, c), lambda i: (i, 0)))
        elif kind == "p":
            in_specs.append(pl.BlockSpec((SUBLANES, c), lambda i: (jnp.maximum(i * per - 1, 0), 0)))
        else:
            in_specs.append(pl.BlockSpec((SUBLANES, c), lambda i: (jnp.minimum((i + 1) * per, last8), 0)))
    for arr in consts:
        in_specs.append(pl.BlockSpec(arr.shape, lambda i, nd=arr.ndim: (0,) * nd))
    out_shape = [jax.ShapeDtypeStruct((t_rows, c), dt) for c, dt in row_outs]
    out_specs = [pl.BlockSpec((tile, c), lambda i: (i, 0)) for c, _ in row_outs]
    for shp in acc_outs:
        out_shape.append(jax.ShapeDtypeStruct(shp, F32))
        out_specs.append(pl.BlockSpec(shp, lambda i, nd=len(shp): (0,) * nd))
    n_in = len(row_ins) + len(consts)
    n_row_out = len(row_outs)

    def body(*refs):
        i = pl.program_id(0)
        vals = [r[...].astype(F32) for r in refs[:len(row_ins)]]
        res = fn(i, n, *vals, *refs[len(row_ins):n_in])
        outs = refs[n_in:]
        for r, v in zip(outs[:n_row_out], res[:n_row_out]):
            r[...] = v.astype(r.dtype)
        if acc_outs:
            @pl.when(i == 0)
            def _():
                for r in outs[n_row_out:]:
                    r[...] = jnp.zeros_like(r)

            for r, v in zip(outs[n_row_out:], res[n_row_out:]):
                r[...] += v

    res = pl.pallas_call(
        body, name=name, out_shape=out_shape, grid=(n,), in_specs=in_specs, out_specs=out_specs,
        compiler_params=_params(("arbitrary",) if acc_outs else ("parallel",)),
    )(*[a for a, _ in row_ins], *consts)
    return res


def _rms(x, w):
    return x * lax.rsqrt(jnp.mean(x * x, axis=-1, keepdims=True) + EPS) * w


def _l2n(x):
    return x * lax.rsqrt(jnp.sum(x * x, axis=-1, keepdims=True) + EPS)


def _sigmoid(x):
    return 1.0 / (1.0 + jnp.exp(-x))


def _silu(x):
    return x * _sigmoid(x)


def _softplus(x):
    return jnp.maximum(x, 0.0) + jnp.log(1.0 + jnp.exp(-jnp.abs(x)))


def _heads(fn, *xs):
    nh = xs[0].shape[1] // HEAD_DIM
    return jnp.concatenate(
        [fn(*[x[:, h * HEAD_DIM:(h + 1) * HEAD_DIM] for x in xs]) for h in range(nh)], axis=1)


def _colsum(x):
    return jnp.sum(x, axis=0, keepdims=True)


def _swiglu(gu):
    return _silu(gu[:, :D_FF]) * gu[:, D_FF:]


def _gated_norm(o, z, w):
    return _heads(lambda oh, zh: _rms(oh, w) * _silu(zh), o, z)


def _mix(gg, ta, tb):
    return _sigmoid(gg[:, :D_MODEL]) * ta + _sigmoid(gg[:, D_MODEL:]) * tb


def _gdn_post(y):
    a = _silu(y)
    q = _heads(lambda v: _l2n(v) * (HEAD_DIM ** -0.5), a[:, :HW])
    k = _heads(_l2n, a[:, HW:2 * HW])
    return q, k, a[:, 2 * HW:]


NN = (((1,), (0,)), ((), ()))
NT = (((1,), (1,)), ((), ()))
TN = (((0,), (0,)), ((), ()))


def _dg(a, b, dims):
    return lax.dot_general(a, b, dims, preferred_element_type=F32)


def _split2(x):
    hi = x.astype(BF16)
    return hi, (x - hi.astype(F32)).astype(BF16)


def _split3(x):
    hi = x.astype(BF16)
    r = x - hi.astype(F32)
    mid = r.astype(BF16)
    return hi, mid, (r - mid.astype(F32)).astype(BF16)


def _dg3(a, b, dims):
    ah, al = _split2(a)
    bh, bl = _split2(b)
    return _dg(ah, bh, dims) + (_dg(ah, bl, dims) + _dg(al, bh, dims))


BNN = (((2,), (1,)), ((0,), (0,)))
BNT = (((2,), (2,)), ((0,), (0,)))
BTN = (((1,), (1,)), ((0,), (0,)))


@jax.custom_vjp
def _mm3(a, b):
    return _dg3(a, b, BNN)


_mm3.defvjp(lambda a, b: (_dg3(a, b, BNN), (a, b)),
            lambda res, g: (_dg3(g, res[1], BNT), _dg3(res[0], g, BTN)))


def _xm(x, m, dims):
    mb = m.astype(BF16)
    parts = _split3(x)
    return _dg(parts[0], mb, dims) + (_dg(parts[1], mb, dims) + _dg(parts[2], mb, dims))


def _mx(m, x, dims):
    mb = m.astype(BF16)
    parts = _split3(x)
    return _dg(mb, parts[0], dims) + (_dg(mb, parts[1], dims) + _dg(mb, parts[2], dims))


@jax.custom_vjp
def _times_const(x, m):
    return _xm(x, m, NN)


_times_const.defvjp(lambda x, m: (_xm(x, m, NN), m),
                    lambda m, g: (_xm(g, m, NT), jnp.zeros_like(m)))


@jax.custom_vjp
def _const_times(m, x):
    return _mx(m, x, NN)


_const_times.defvjp(lambda m, x: (_mx(m, x, NN), m),
                    lambda m, g: (jnp.zeros_like(m), _mx(m, g, TN)))


@jax.custom_vjp
def _lane_mean_cols(x, avg):
    return _mx(avg, x, BNT)


_lane_mean_cols.defvjp(lambda x, avg: (_mx(avg, x, BNT), avg),
                       lambda avg, g: (_xm(g, avg, BTN), jnp.zeros_like(avg)))


def _gdn_gates(ab, alog, dtb, e_g, e_b):
    t = ab.shape[0]
    g = -jnp.exp(alog) * _softplus(ab + dtb)
    beta = _sigmoid(ab)
    ri = lax.broadcasted_iota(jnp.int32, (t, t), 0)
    ci = lax.broadcasted_iota(jnp.int32, (t, t), 1)
    shift = CHUNK.bit_length() - 1
    same = jnp.right_shift(ri, shift) == jnp.right_shift(ci, shift)
    tril = jnp.where(same & (ri >= ci), 1.0, 0.0).astype(F32)
    gc = _const_times(tril, g)
    return _times_const(gc, e_g), _times_const(beta, e_b)


def _shift_down(x, halo, s, i):
    if s == 0:
        return x
    halo = jnp.where(i == 0, 0.0, halo)
    xr = pltpu.roll(x, s, 0)
    hr = pltpu.roll(halo, s, 0)
    row = lax.broadcasted_iota(jnp.int32, (SUBLANES, x.shape[1]), 0)
    top = jnp.where(row < s, hr, xr[:SUBLANES])
    return jnp.concatenate([top, xr[SUBLANES:]], axis=0)


def _shift_up(x, halo, s, i, n):
    if s == 0:
        return x
    t = x.shape[0]
    halo = jnp.where(i == n - 1, 0.0, halo)
    xr = pltpu.roll(x, t - s, 0)
    hr = pltpu.roll(halo, SUBLANES - s, 0)
    row = lax.broadcasted_iota(jnp.int32, (SUBLANES, x.shape[1]), 0)
    bot = jnp.where(row >= SUBLANES - s, hr, xr[t - SUBLANES:])
    return jnp.concatenate([xr[:t - SUBLANES], bot], axis=0)


def _conv(pa, prev, cw_ref, i):
    y = pa * cw_ref[CONV_K - 1:CONV_K, :]
    for j in range(CONV_K - 1):
        y = y + _shift_down(pa, prev, CONV_K - 1 - j, i) * cw_ref[j:j + 1, :]
    return y


def _dot_nt(a, b, precision=None):
    return lax.dot_general(a, b, (((1,), (1,)), ((), ())), precision=precision, preferred_element_type=F32)


def _dot_tn(a, b, precision=None):
    return lax.dot_general(a, b, (((0,), (0,)), ((), ())), precision=precision, preferred_element_type=F32)


def _dot(a, b, precision=None):
    return jnp.dot(a, b, precision=precision, preferred_element_type=F32)


def _bf(x):
    return x.astype(BF16)


def _neumann_inverse(lmat):
    nb, c, _ = lmat.shape
    ri = lax.broadcasted_iota(jnp.int32, (nb, c, c), 1)
    ci = lax.broadcasted_iota(jnp.int32, (nb, c, c), 2)
    pw = -lmat
    inv = jnp.where(ri == ci, 1.0, 0.0).astype(F32) + pw
    for _ in range(5):
        pw = _mm3(pw, pw)
        inv = inv + _mm3(inv, pw)
    return inv


@jax.custom_vjp
def _unit_lower_inverse(lmat):
    return _neumann_inverse(lmat)


def _unit_lower_inverse_fwd(lmat):
    inv = _neumann_inverse(lmat)
    return inv, inv


def _unit_lower_inverse_bwd(inv, g):
    return (-_dg3(_dg3(inv, g, BTN), inv, BNT),)


_unit_lower_inverse.defvjp(_unit_lower_inverse_fwd, _unit_lower_inverse_bwd)


def _gdn_chunk(q, k, v, gc, bb):
    nb, c, _ = q.shape
    ri = lax.broadcasted_iota(jnp.int32, (nb, c, c), 1)
    ci = lax.broadcasted_iota(jnp.int32, (nb, c, c), 2)
    incl = ri >= ci
    strict = ri > ci
    g_row = gc[:, :, :c]
    g_col = _lane_mean_cols(gc, jnp.full((nb, c, LANES), 1.0 / LANES, F32))
    decay = jnp.where(incl, jnp.exp(jnp.where(incl, g_row - g_col, 0.0)), 0.0)
    kb = k * bb
    lmat = jnp.where(strict, _dg(_bf(kb), _bf(k), BNT) * decay, 0.0)
    inv = _unit_lower_inverse(lmat)
    egc = jnp.exp(gc)
    u = _mm3(inv, v * bb)
    w = _mm3(inv, kb * egc)
    aqk = _dg(_bf(q), _bf(k), BNT) * decay
    last = lax.broadcasted_iota(jnp.int32, (nb, c, LANES), 1) == c - 1
    tot = jnp.sum(jnp.where(last, gc, 0.0), axis=1, keepdims=True)
    k_tail = k * jnp.exp(tot - gc)
    tail = jnp.broadcast_to(jnp.exp(tot), (nb, SUBLANES, LANES))
    return u, w, aqk, q * egc, k_tail, tail


def _gdn_intra(qn, kn, vv, g_b, beta_b):
    t_rows = qn.shape[0]
    nc = t_rows // CHUNK
    cb = min(GDN_CB, nc)
    rows = cb * CHUNK
    col = pl.BlockSpec((rows, HEAD_DIM), lambda h, b: (b, h))

    def body(q_ref, k_ref, v_ref, g_ref, b_ref, u_ref, w_ref, a_ref, qd_ref, kt_ref, tl_ref):
        def group(gi, carry):
            r = pl.ds(pl.multiple_of(gi * (grp * CHUNK), grp * CHUNK), grp * CHUNK)
            ins = [ref[r, :].reshape(grp, CHUNK, HEAD_DIM) for ref in (q_ref, k_ref, v_ref, g_ref, b_ref)]
            u, w, aqk, qd, kt, tl = _gdn_chunk(*ins)
            for ref, val in ((u_ref, u), (w_ref, w), (qd_ref, qd), (kt_ref, kt)):
                ref[r, :] = val.reshape(grp * CHUNK, HEAD_DIM)
            a_ref[0, r, :] = aqk.reshape(grp * CHUNK, CHUNK)
            tl_ref[0, pl.ds(gi * grp, grp)] = tl
            return carry

        grp = min(GDN_GROUP, cb)
        lax.fori_loop(0, cb // grp, group, 0)

    full = jax.ShapeDtypeStruct((t_rows, HW), F32)
    return pl.pallas_call(
        body, name="gdn_intra_fwd",
        out_shape=[full, full, jax.ShapeDtypeStruct((HEADS, t_rows, CHUNK), F32), full, full,
                   jax.ShapeDtypeStruct((HEADS, nc, SUBLANES, LANES), F32)],
        grid=(HEADS, nc // cb),
        in_specs=[col] * 5,
        out_specs=[col, col, pl.BlockSpec((1, rows, CHUNK), lambda h, b: (h, b, 0)), col, col,
                   pl.BlockSpec((1, cb, SUBLANES, LANES), lambda h, b: (h, b, 0, 0))],
        compiler_params=_params(("parallel", "parallel")),
    )(qn, kn, vv, g_b, beta_b)


def _gdn_intra_bwd(qn, kn, vv, g_b, beta_b, du, dw, da, dqd, dkt, dtl):
    t_rows = qn.shape[0]
    nc = t_rows // CHUNK
    cb = min(GDN_CB, nc)
    rows = cb * CHUNK
    col = pl.BlockSpec((rows, HEAD_DIM), lambda h, b: (b, h))
    a_spec = pl.BlockSpec((1, rows, CHUNK), lambda h, b: (h, b, 0))
    tl_spec = pl.BlockSpec((1, cb, SUBLANES, LANES), lambda h, b: (h, b, 0, 0))

    def body(q_ref, k_ref, v_ref, g_ref, b_ref, du_ref, dw_ref, da_ref, dqd_ref, dkt_ref, dtl_ref,
             dq_ref, dk_ref, dv_ref, dg_ref, db_ref):
        def group(gi, carry):
            r = pl.ds(pl.multiple_of(gi * (grp * CHUNK), grp * CHUNK), grp * CHUNK)
            wide = (grp, CHUNK, HEAD_DIM)
            ins = [ref[r, :].reshape(wide) for ref in (q_ref, k_ref, v_ref, g_ref, b_ref)]
            cts = (du_ref[r, :].reshape(wide), dw_ref[r, :].reshape(wide),
                   da_ref[0, r, :].reshape(grp, CHUNK, CHUNK), dqd_ref[r, :].reshape(wide),
                   dkt_ref[r, :].reshape(wide), dtl_ref[0, pl.ds(gi * grp, grp)])
            grads = jax.vjp(_gdn_chunk, *ins)[1](cts)
            for ref, val in zip((dq_ref, dk_ref, dv_ref, dg_ref, db_ref), grads):
                ref[r, :] = val.reshape(grp * CHUNK, HEAD_DIM)
            return carry

        grp = min(GDN_GROUP, cb)
        lax.fori_loop(0, cb // grp, group, 0)

    full = jax.ShapeDtypeStruct((t_rows, HW), F32)
    return pl.pallas_call(
        body, name="gdn_intra_bwd",
        out_shape=[full] * 5,
        grid=(HEADS, nc // cb),
        in_specs=[col] * 7 + [a_spec, col, col, tl_spec],
        out_specs=[col] * 5,
        compiler_params=_params(("parallel", "parallel")),
    )(qn, kn, vv, g_b, beta_b, du, dw, da, dqd, dkt, dtl)


def _head_cols(h):
    return slice(h * HEAD_DIM, (h + 1) * HEAD_DIM)


def _gdn_scan(u, w, aqk, qd, kt, tl):
    t_rows = u.shape[0]
    nc = t_rows // CHUNK
    cb = min(GDN_CB, nc)
    rows = cb * CHUNK
    wide = pl.BlockSpec((rows, HW), lambda b: (b, 0))

    def body(u_ref, w_ref, a_ref, qd_ref, kt_ref, tl_ref, o_ref, s_out_ref, s_ref):
        @pl.when(pl.program_id(0) == 0)
        def _():
            s_ref[...] = jnp.zeros_like(s_ref)

        def chunk(ci, carry):
            r = pl.ds(pl.multiple_of(ci * CHUNK, CHUNK), CHUNK)
            for h in range(HEADS):
                hc = _head_cols(h)
                s = s_ref[h]
                s_out_ref[ci, h] = s
                sb = _bf(s)
                vn = u_ref[r, hc] - _dot(_bf(w_ref[r, hc]), sb)
                vnb = _bf(vn)
                o_ref[r, hc] = _dot(_bf(qd_ref[r, hc]), sb) + _dot(_bf(a_ref[h, r, :]), vnb)
                s_ref[h] = s * tl_ref[h, ci, 0:1, :] + _dot_tn(_bf(kt_ref[r, hc]), vnb)
            return carry

        lax.fori_loop(0, cb, chunk, 0)

    return pl.pallas_call(
        body, name="gdn_scan_fwd",
        out_shape=[jax.ShapeDtypeStruct((t_rows, HW), F32),
                   jax.ShapeDtypeStruct((nc, HEADS, HEAD_DIM, HEAD_DIM), F32)],
        grid=(nc // cb,),
        in_specs=[wide, wide, pl.BlockSpec((HEADS, rows, CHUNK), lambda b: (0, b, 0)), wide, wide,
                  pl.BlockSpec((HEADS, cb, SUBLANES, LANES), lambda b: (0, b, 0, 0))],
        out_specs=[wide, pl.BlockSpec((cb, HEADS, HEAD_DIM, HEAD_DIM), lambda b: (b, 0, 0, 0))],
        scratch_shapes=[pltpu.VMEM((HEADS, HEAD_DIM, HEAD_DIM), F32)],
        compiler_params=_params(("arbitrary",)),
    )(u, w, aqk, qd, kt, tl)


def _gdn_scan_bwd(do, u, w, aqk, qd, kt, tl, states):
    t_rows = u.shape[0]
    nc = t_rows // CHUNK
    cb = min(GDN_CB, nc)
    rows = cb * CHUNK
    nb = nc // cb
    wide = pl.BlockSpec((rows, HW), lambda b: (nb - 1 - b, 0))
    a_spec = pl.BlockSpec((HEADS, rows, CHUNK), lambda b: (0, nb - 1 - b, 0))
    tl_spec = pl.BlockSpec((HEADS, cb, SUBLANES, LANES), lambda b: (0, nb - 1 - b, 0, 0))

    def body(do_ref, u_ref, w_ref, a_ref, qd_ref, kt_ref, tl_ref, s_in_ref,
             du_ref, dw_ref, da_ref, dqd_ref, dkt_ref, dtl_ref, ds_ref):
        @pl.when(pl.program_id(0) == 0)
        def _():
            ds_ref[...] = jnp.zeros_like(ds_ref)

        row0 = lax.broadcasted_iota(jnp.int32, (SUBLANES, LANES), 0) == 0

        def chunk(step, carry):
            ci = cb - 1 - step
            r = pl.ds(pl.multiple_of(ci * CHUNK, CHUNK), CHUNK)
            for h in range(HEADS):
                hc = _head_cols(h)
                s = s_in_ref[ci, h]
                ds_next = ds_ref[h]
                sb, dsb = _bf(s), _bf(ds_next)
                wb, ab, ktb, qdb = _bf(w_ref[r, hc]), _bf(a_ref[h, r, :]), _bf(kt_ref[r, hc]), _bf(qd_ref[r, hc])
                dob = _bf(do_ref[r, hc])
                vn = u_ref[r, hc] - _dot(wb, sb)
                vnb = _bf(vn)
                dvn = _dot_tn(ab, dob) + _dot(ktb, dsb)
                dvnb = _bf(dvn)
                du_ref[r, hc] = dvn
                dw_ref[r, hc] = -_dot_nt(dvnb, sb)
                da_ref[h, r, :] = _dot_nt(dob, vnb)
                dqd_ref[r, hc] = _dot_nt(dob, sb)
                dkt_ref[r, hc] = _dot_nt(vnb, dsb)
                dtl_ref[h, ci] = jnp.where(row0, _colsum(s * ds_next), 0.0)
                ds_ref[h] = _dot_tn(qdb, dob) + ds_next * tl_ref[h, ci, 0:1, :] - _dot_tn(wb, dvnb)
            return carry

        lax.fori_loop(0, cb, chunk, 0)

    full = jax.ShapeDtypeStruct((t_rows, HW), F32)
    return pl.pallas_call(
        body, name="gdn_scan_bwd",
        out_shape=[full, full, jax.ShapeDtypeStruct((HEADS, t_rows, CHUNK), F32), full, full,
                   jax.ShapeDtypeStruct((HEADS, nc, SUBLANES, LANES), F32)],
        grid=(nb,),
        in_specs=[wide, wide, wide, a_spec, wide, wide, tl_spec,
                  pl.BlockSpec((cb, HEADS, HEAD_DIM, HEAD_DIM), lambda b: (nb - 1 - b, 0, 0, 0))],
        out_specs=[wide, wide, a_spec, wide, wide, tl_spec],
        scratch_shapes=[pltpu.VMEM((HEADS, HEAD_DIM, HEAD_DIM), F32)],
        compiler_params=_params(("arbitrary",)),
    )(do, u, w, aqk, qd, kt, tl, states)


def _att_rel_index():
    qi = lax.broadcasted_iota(jnp.int32, (ATT_QB, ATT_KW), 0)
    kj = lax.broadcasted_iota(jnp.int32, (ATT_QB, ATT_KW), 1)
    return jnp.clip(qi - kj + ATT_PAD, -(CHUNK - 1), MAX_REL) + (CHUNK - 1)


def _att_in_band():
    qi = lax.broadcasted_iota(jnp.int32, (ATT_QB, ATT_KW), 0)
    kj = lax.broadcasted_iota(jnp.int32, (ATT_QB, ATT_KW), 1)
    shift = CHUNK.bit_length() - 1
    qc = jnp.right_shift(qi, shift)
    kc = jnp.right_shift(kj, shift) - LEFT_CHUNKS
    return (kc <= qc) & (kc >= qc - LEFT_CHUNKS)


def _att_valid(b):
    kj = lax.broadcasted_iota(jnp.int32, (1, ATT_KW), 1)
    return jnp.where(kj + b * ATT_QB >= ATT_PAD, 0.0, NEG_INF)


ATT_STRIP = 16


def _rms_parts(x, w):
    r = lax.rsqrt(jnp.mean(x * x, axis=-1, keepdims=True) + EPS)
    xn = x * r
    return xn * w, xn, r


def _rms_bwd(dy, xn, r, w):
    dxn = dy * w
    dx = r * (dxn - xn * jnp.mean(dxn * xn, axis=-1, keepdims=True))
    return dx, _colsum(dy * xn)


def _att_softmax_strips(s_ref, bias_ref, before_start, p_ref, dp_ref=None, ds_ref=None, dbias_ref=None):
    def strip(i, carry):
        rows = pl.ds(pl.multiple_of(i * ATT_STRIP, ATT_STRIP), ATT_STRIP)
        s = s_ref[rows, :] + bias_ref[rows, :] + before_start
        e = jnp.exp(s - jnp.max(s, axis=-1, keepdims=True))
        p = e * (1.0 / jnp.sum(e, axis=-1, keepdims=True))
        p_ref[rows, :] = p.astype(BF16)
        if dp_ref is not None:
            dp = dp_ref[rows, :]
            ds = p * (dp - jnp.sum(p * dp, axis=-1, keepdims=True))
            ds_ref[rows, :] = ds.astype(BF16)
            dbias_ref[rows, :] += ds
        return carry

    lax.fori_loop(0, ATT_QB // ATT_STRIP, strip, 0)


def _att_specs():
    q_spec = pl.BlockSpec((ATT_QB, HEAD_DIM), lambda h, b: (b, h))
    k_specs = [pl.BlockSpec((ATT_QB, HEAD_DIM), lambda h, b, j=j: (b + j, HEADS + h)) for j in range(3)]
    v_specs = [pl.BlockSpec((ATT_QB, HEAD_DIM), lambda h, b, j=j: (b + j, 2 * HEADS + h)) for j in range(3)]
    w_spec = pl.BlockSpec((1, HEAD_DIM), lambda h, b: (0, 0))
    smem = pl.BlockSpec(memory_space=pltpu.SMEM)
    return q_spec, k_specs, v_specs, w_spec, smem


def _att_fill_bias(bias_ref, rel_ref, h):
    idx = _att_rel_index()

    def fill(r, acc):
        return jnp.where(idx == r, rel_ref[h, r], acc)

    table = lax.fori_loop(0, N_REL, fill, jnp.zeros((ATT_QB, ATT_KW), F32))
    bias_ref[...] = jnp.where(_att_in_band(), table, NEG_INF)


def _attention(pb, pbp, qw, kw, rel):
    t_rows = pb.shape[0]
    q_spec, k_specs, v_specs, w_spec, smem = _att_specs()

    def body(q_ref, k0, k1, k2, v0, v1, v2, qw_ref, kw_ref, rel_ref, o_ref, bias_ref, s_ref, p_ref):
        h, b = pl.program_id(0), pl.program_id(1)

        @pl.when(b == 0)
        def _():
            _att_fill_bias(bias_ref, rel_ref, h)

        kwin = jnp.concatenate([k0[...], k1[...], k2[...]], axis=0)
        vwin = jnp.concatenate([v0[...], v1[...], v2[...]], axis=0)
        q = _rms(q_ref[...], qw_ref[...])
        k = _rms(kwin, kw_ref[...])
        s_ref[...] = _dot_nt(_bf(q), _bf(k)) * (HEAD_DIM ** -0.5)
        _att_softmax_strips(s_ref, bias_ref, _att_valid(b), p_ref)
        o_ref[...] = _dot(p_ref[...], _bf(vwin)).astype(o_ref.dtype)

    return pl.pallas_call(
        body, name="band_attention_fwd",
        out_shape=jax.ShapeDtypeStruct((t_rows, HW), BF16),
        grid=(HEADS, t_rows // ATT_QB),
        in_specs=[q_spec] + k_specs + v_specs + [w_spec, w_spec, smem],
        out_specs=pl.BlockSpec((ATT_QB, HEAD_DIM), lambda h, b: (b, h)),
        scratch_shapes=[pltpu.VMEM((ATT_QB, ATT_KW), F32), pltpu.VMEM((ATT_QB, ATT_KW), F32),
                        pltpu.VMEM((ATT_QB, ATT_KW), BF16)],
        compiler_params=_params(("arbitrary", "arbitrary")),
    )(pb, pbp, pbp, pbp, pbp, pbp, pbp, qw, kw, rel)


def _attention_bwd(pb, pbp, qw, kw, rel, dyb):
    t_rows = pb.shape[0]
    nb = t_rows // ATT_QB
    q_spec, k_specs, v_specs, w_spec, smem = _att_specs()
    pad_rows = t_rows + ATT_PAD
    acc_spec = pl.BlockSpec((pad_rows, HEAD_DIM), lambda h, b: (0, h))

    def body(q_ref, k0, k1, k2, v0, v1, v2, qw_ref, kw_ref, rel_ref, do_ref,
             dq_ref, dk_ref, dv_ref, dqw_ref, dkw_ref, drel_ref, bias_ref, dbias_ref, s_ref, dp_ref, p_ref, ds_ref):
        h, b = pl.program_id(0), pl.program_id(1)

        @pl.when(b == 0)
        def _():
            _att_fill_bias(bias_ref, rel_ref, h)
            dbias_ref[...] = jnp.zeros_like(dbias_ref)
            dk_ref[...] = jnp.zeros_like(dk_ref)
            dv_ref[...] = jnp.zeros_like(dv_ref)

        @pl.when((b == 0) & (h == 0))
        def _():
            dqw_ref[...] = jnp.zeros_like(dqw_ref)
            dkw_ref[...] = jnp.zeros_like(dkw_ref)

        kwin = jnp.concatenate([k0[...], k1[...], k2[...]], axis=0)
        vwin = jnp.concatenate([v0[...], v1[...], v2[...]], axis=0)
        scale = HEAD_DIM ** -0.5
        qw_, kw_ = qw_ref[...], kw_ref[...]
        q, qn, rq = _rms_parts(q_ref[...], qw_)
        k, kn, rk = _rms_parts(kwin, kw_)
        qb, kb, dob = _bf(q), _bf(k), _bf(do_ref[...])
        s_ref[...] = _dot_nt(qb, kb) * scale
        dp_ref[...] = _dot_nt(dob, _bf(vwin))
        _att_softmax_strips(s_ref, bias_ref, _att_valid(b), p_ref, dp_ref, ds_ref, dbias_ref)
        ds = ds_ref[...]
        dq, dqw = _rms_bwd(_dot(ds, kb) * scale, qn, rq, qw_)
        dk, dkw = _rms_bwd(_dot_tn(ds, qb) * scale, kn, rk, kw_)
        dq_ref[...] = dq.astype(dq_ref.dtype)
        win = pl.ds(pl.multiple_of(b * ATT_QB, ATT_QB), ATT_KW)
        dk_ref[win, :] += dk
        dv_ref[win, :] += _dot_tn(p_ref[...], dob)
        dqw_ref[...] += dqw
        dkw_ref[...] += dkw

        @pl.when(b == nb - 1)
        def _():
            idx = _att_rel_index()
            tot = dbias_ref[...]

            def reduce(r, carry):
                drel_ref[h, r] = jnp.sum(jnp.where(idx == r, tot, 0.0))
                return carry

            lax.fori_loop(0, N_REL, reduce, 0)

    return pl.pallas_call(
        body, name="band_attention_bwd",
        out_shape=[jax.ShapeDtypeStruct((t_rows, HW), BF16),
                   jax.ShapeDtypeStruct((pad_rows, HW), F32), jax.ShapeDtypeStruct((pad_rows, HW), F32),
                   jax.ShapeDtypeStruct((1, HEAD_DIM), F32), jax.ShapeDtypeStruct((1, HEAD_DIM), F32),
                   jax.ShapeDtypeStruct((HEADS, N_REL), F32)],
        grid=(HEADS, nb),
        in_specs=[q_spec] + k_specs + v_specs + [w_spec, w_spec, smem, q_spec],
        out_specs=[q_spec, acc_spec, acc_spec, w_spec, w_spec, smem],
        scratch_shapes=[pltpu.VMEM((ATT_QB, ATT_KW), F32)] * 4 + [pltpu.VMEM((ATT_QB, ATT_KW), BF16)] * 2,
        compiler_params=_params(("arbitrary", "arbitrary")),
    )(pb, pbp, pbp, pbp, pbp, pbp, pbp, qw, kw, rel, dyb)


def _me():
    return lax.axis_index("x"), lax.axis_index("y"), lax.axis_index("c")


def _index(x, y, c):
    return 4 * x + 2 * y + c


HBM_SPEC = pl.BlockSpec(memory_space=pl.ANY)


def _block(ref, kind, d, r, c):
    if kind == "rows":
        return ref.at[pl.ds(d * r, r), :]
    if kind == "win":
        return ref.at[:, pl.ds(d * WIN_STEP, c)]
    return ref.at[:, pl.ds(d * c, c)]


def _all_gather(shards, kinds, n_gather):
    n = len(shards)

    def body(*refs):
        x_refs, out_refs = refs[:n], refs[n:2 * n]
        send_sems, recv_sems, local_sems = refs[2 * n:]
        x, y, c = _me()
        me, sibling = (x, y, c), (x, y, 1 - c)
        chips = [(1 - x, y), (x, 1 - y), (1 - x, 1 - y)]

        def copy(i, k, blk, to, src=None):
            r_, c_ = shards[i].shape
            dst = _block(out_refs[i], kinds[i], _index(*blk), r_, c_)
            return pltpu.make_async_remote_copy(
                src_ref=dst if src is None else src, dst_ref=dst,
                send_sem=send_sems.at[i, k], recv_sem=recv_sems.at[i, k], device_id=to, device_id_type=MESH)

        sends, local = [], []
        for i in range(n):
            r_, c_ = shards[i].shape
            mine = pltpu.make_async_copy(x_refs[i], _block(out_refs[i], kinds[i], _index(*me), r_, c_),
                                         local_sems.at[i])
            mine.start()
            local.append(mine)
            if i >= n_gather:
                continue
            first = [copy(i, 0, me, sibling, src=x_refs[i])]
            first += [copy(i, 1 + j, me, (*chip, c), src=x_refs[i]) for j, chip in enumerate(chips)]
            for cp in first:
                cp.start()
            sends += first
        for i in range(n_gather):
            for j, chip in enumerate(chips):
                copy(i, 1 + j, (*chip, c), me).wait_recv()
                passed = copy(i, 4 + j, (*chip, c), sibling)
                passed.start()
                sends.append(passed)
        for i in range(n_gather):
            copy(i, 0, sibling, me).wait_recv()
            for j, chip in enumerate(chips):
                copy(i, 4 + j, (*chip, 1 - c), me).wait_recv()
        for cp in sends:
            cp.wait_send()
        for cp in local:
            cp.wait()

    def full_shape(s, kind):
        r_, c_ = s.shape
        return (N_DEV * r_, c_) if kind == "rows" else (r_, N_DEV * c_)

    return pl.pallas_call(
        body, name="weights_all_gather",
        out_shape=[jax.ShapeDtypeStruct(full_shape(s, k), s.dtype) for s, k in zip(shards, kinds)],
        in_specs=[HBM_SPEC] * n, out_specs=[HBM_SPEC] * n,
        scratch_shapes=[pltpu.SemaphoreType.DMA((n_gather, 7)), pltpu.SemaphoreType.DMA((n_gather, 7)),
                        pltpu.SemaphoreType.DMA((n,))],
        compiler_params=pltpu.CompilerParams(has_side_effects=True),
    )(*shards)


SEM_SPEC = pl.BlockSpec(memory_space=pltpu.SEMAPHORE)
HBM_ONLY = pl.BlockSpec(memory_space=pltpu.HBM)
DATAFLOW = pltpu.SideEffectType.DATAFLOW_SIDE_EFFECTING


def _peers():
    x, y, c = _me()
    return [(x ^ (k >> 2), y ^ ((k >> 1) & 1), c ^ (k & 1)) for k in range(1, N_DEV)]


def _gather_copies(shapes, kinds):
    def make(src_refs, land_refs, send_sems, recv_sems):
        mine = _index(*_me())
        return [pltpu.make_async_remote_copy(
            src_ref=src_refs[i], dst_ref=_block(land_refs[i], kind, mine, r, c),
            send_sem=send_sems.at[7 * i + k], recv_sem=recv_sems.at[7 * i + k], device_id=peer, device_id_type=MESH)
            for i, ((r, c), kind) in enumerate(zip(shapes, kinds)) for k, peer in enumerate(_peers())]

    return make


def _exchange_copies(shapes, kinds):
    def make(src_refs, land_refs, send_sems, recv_sems):
        mine = _index(*_me())
        return [pltpu.make_async_remote_copy(
            src_ref=_block(src_refs[i], kind, _index(*peer), r, c), dst_ref=land_refs[i].at[mine],
            send_sem=send_sems.at[7 * i + k], recv_sem=recv_sems.at[7 * i + k], device_id=peer, device_id_type=MESH)
            for i, ((r, c), kind) in enumerate(zip(shapes, kinds)) for k, peer in enumerate(_peers())]

    return make


def _place_block(shard, kind, name):
    r, c = shard.shape
    tile = _row_tile(r, c)
    nt = r // tile
    full = (N_DEV * r, c) if kind == "rows" else (r, N_DEV * c)

    def body(me_ref, x_ref, out_ref):
        out_ref[...] = x_ref[...]

    if kind == "rows":
        out_spec = pl.BlockSpec((tile, c), lambda i, me: (me[0] * nt + i, 0))
    else:
        out_spec = pl.BlockSpec((tile, c), lambda i, me: (i, me[0]))
    return pl.pallas_call(
        body, name=name, out_shape=jax.ShapeDtypeStruct(full, shard.dtype),
        grid_spec=pltpu.PrefetchScalarGridSpec(
            num_scalar_prefetch=1, grid=(nt,),
            in_specs=[pl.BlockSpec((tile, c), lambda i, me: (i, 0))], out_specs=out_spec),
        compiler_params=_params(("arbitrary",)),
    )(_my_index_operand(), shard)


def _split_start(srcs, lands, make, name):
    n = len(srcs)

    def body(*refs):
        send_sems, recv_sems = refs[2 * n], refs[2 * n + 1]
        for cp in make(refs[:n], refs[n:2 * n], send_sems, recv_sems):
            cp.start()
        refs[-1][...] = jnp.zeros_like(refs[-1])

    arrays = list(srcs) + list(lands)
    out = pl.pallas_call(
        body, name=name,
        out_shape=(pltpu.SemaphoreType.DMA((7 * n,)), pltpu.SemaphoreType.DMA((7 * n,)),
                   *[pltpu.HBM(a.shape, a.dtype) for a in arrays], jax.ShapeDtypeStruct((SUBLANES, LANES), F32)),
        in_specs=[HBM_ONLY] * (2 * n),
        out_specs=(SEM_SPEC, SEM_SPEC, *[HBM_ONLY] * (2 * n), pl.BlockSpec(memory_space=pltpu.VMEM)),
        input_output_aliases={i: 2 + i for i in range(2 * n)},
        compiler_params=pltpu.CompilerParams(has_side_effects=DATAFLOW),
    )(*[pltpu.with_memory_space_constraint(a, pltpu.HBM) for a in arrays])
    return out[0], out[1], list(out[2:2 + n]), list(out[2 + n:2 + 2 * n]), out[-1]


def _split_wait(send_sems, recv_sems, srcs, lands, after, make, name):
    n = len(srcs)

    def body(*refs):
        for cp in make(refs[:n], refs[n:2 * n], refs[2 * n], refs[2 * n + 1]):
            cp.wait_send()
            cp.wait_recv()

    arrays = list(srcs) + list(lands)
    out = pl.pallas_call(
        body, name=name,
        out_shape=tuple(pltpu.HBM(a.shape, a.dtype) for a in arrays),
        in_specs=[HBM_ONLY] * (2 * n) + [SEM_SPEC, SEM_SPEC, pl.BlockSpec(memory_space=pl.ANY)],
        out_specs=tuple([HBM_ONLY] * (2 * n)),
        input_output_aliases={i: i for i in range(2 * n)},
        compiler_params=pltpu.CompilerParams(has_side_effects=DATAFLOW),
    )(*arrays, send_sems, recv_sems, after)
    return list(out[:n]), list(out[n:])


def _all_reduce_small(vals, name):
    rows, width = vals.shape

    def body(x_ref, out_ref, buf_ref, send_sems, recv_sems):
        x, y, c = _me()
        mine = _index(x, y, c)
        buf_ref[mine] = x_ref[...]
        copies = []
        for k in range(1, N_DEV):
            px, py, pc = x ^ (k >> 2), y ^ ((k >> 1) & 1), c ^ (k & 1)
            copies.append(pltpu.make_async_remote_copy(
                src_ref=x_ref, dst_ref=buf_ref.at[mine],
                send_sem=send_sems.at[k - 1], recv_sem=recv_sems.at[k - 1],
                device_id=(px, py, pc), device_id_type=MESH))
        for cp in copies:
            cp.start()
        for cp in copies:
            cp.wait()
        acc = buf_ref[0]
        for j in range(1, N_DEV):
            acc = acc + buf_ref[j]
        out_ref[...] = acc

    vmem = pl.BlockSpec(memory_space=pltpu.VMEM)
    return pl.pallas_call(
        body, name=name,
        out_shape=jax.ShapeDtypeStruct(vals.shape, F32),
        in_specs=[vmem], out_specs=vmem,
        scratch_shapes=[pltpu.VMEM((N_DEV, rows, width), F32),
                        pltpu.SemaphoreType.DMA((7,)), pltpu.SemaphoreType.DMA((7,))],
        compiler_params=pltpu.CompilerParams(has_side_effects=True),
    )(vals)


def _adamw_math(w, g, m, v):
    m = ADAM_B1 * m + (1.0 - ADAM_B1) * g
    v = ADAM_B2 * v + (1.0 - ADAM_B2) * (g * g)
    m_hat = m / (1.0 - ADAM_B1 ** ADAM_STEP)
    v_hat = v / (1.0 - ADAM_B2 ** ADAM_STEP)
    delta = -ADAM_LR * (m_hat / (jnp.sqrt(v_hat) + ADAM_EPS) + ADAM_WD * w)
    return delta, m, v


ROW_TILE_ELEMS = 384 * 1024


def _row_tile(rows, width):
    best = SUBLANES
    for t in range(SUBLANES, rows + 1, SUBLANES):
        if rows % t == 0 and t * width <= ROW_TILE_ELEMS:
            best = t
    return best


def _sum_received(r_ref, own, me):
    g = None
    for j in range(N_DEV):
        term = jnp.where(me == j, own, r_ref[j].astype(F32))
        g = term if g is None else g + term
    return g


def _my_index_operand():
    return _index(*_me()).astype(jnp.int32).reshape(1)


def _adamw_recv(recv, grad, kind, w, m, v, name):
    _, rows, width = recv.shape
    tile = _row_tile(rows, width)
    nt = rows // tile

    def body(me_ref, r_ref, own_ref, w_ref, m_ref, v_ref, g_out, d_out, m_out, v_out):
        g = _sum_received(r_ref, own_ref[...].astype(F32), me_ref[0])
        d, mn, vn = _adamw_math(w_ref[...], g, m_ref[...], v_ref[...])
        g_out[...] = g
        d_out[...] = d
        m_out[...] = mn
        v_out[...] = vn

    if kind == "rows":
        own_spec = pl.BlockSpec((tile, width), lambda i, me: (me[0] * nt + i, 0))
    else:
        own_spec = pl.BlockSpec((tile, width), lambda i, me: (i, me[0]))
    spec = pl.BlockSpec((tile, width), lambda i, me: (i, 0))
    shape = jax.ShapeDtypeStruct((rows, width), F32)
    return pl.pallas_call(
        body, name=name, out_shape=[shape] * 4,
        grid_spec=pltpu.PrefetchScalarGridSpec(
            num_scalar_prefetch=1, grid=(nt,),
            in_specs=[pl.BlockSpec((N_DEV, tile, width), lambda i, me: (0, i, 0)), own_spec, spec, spec, spec],
            out_specs=[spec] * 4),
        compiler_params=_params(("parallel",)),
    )(_my_index_operand(), recv, grad, w, m, v)


WIN_STEP = 1408
WIN_W = 1536
IN_SHARD = IN_COLS // N_DEV
IN_PADDED = WIN_STEP * (N_DEV - 1) + WIN_W


def _roll_w_in(shard_padded):
    rows = shard_padded.shape[0]
    tile = _row_tile(rows, WIN_W)

    def body(x_ref, main_ref, edge_ref):
        win = pltpu.roll(x_ref[...], 2 * _index(*_me()), 1).astype(BF16)
        main_ref[...] = win[:, :WIN_STEP]
        edge_ref[...] = win[:, WIN_STEP:]

    return pl.pallas_call(
        body, name="w_in_window",
        out_shape=[jax.ShapeDtypeStruct((rows, WIN_STEP), BF16), jax.ShapeDtypeStruct((rows, WIN_W - WIN_STEP), BF16)],
        grid=(rows // tile,),
        in_specs=[pl.BlockSpec((tile, WIN_W), lambda i: (i, 0))],
        out_specs=[pl.BlockSpec((tile, WIN_STEP), lambda i: (i, 0)),
                   pl.BlockSpec((tile, WIN_W - WIN_STEP), lambda i: (i, 0))],
        compiler_params=_params(("parallel",)),
    )(shard_padded)


def _sum_w_in_windows(recv, grad):
    _, rows, width = recv.shape
    tile = _row_tile(rows, width)

    def body(me_ref, r_ref, g_ref, g_out, own_ref, sem):
        me = me_ref[0]
        rows_i = pl.ds(pl.multiple_of(pl.program_id(0) * tile, tile), tile)
        own = pltpu.make_async_copy(g_ref.at[rows_i, pl.ds(pl.multiple_of(me * WIN_STEP, LANES), width)], own_ref, sem)
        own.start()
        own.wait()
        g_out[...] = pltpu.roll(_sum_received(r_ref, own_ref[...].astype(F32), me), width - 2 * me, 1)

    return pl.pallas_call(
        body, name="w_in_grad_sum", out_shape=jax.ShapeDtypeStruct((rows, width), F32),
        grid_spec=pltpu.PrefetchScalarGridSpec(
            num_scalar_prefetch=1, grid=(rows // tile,),
            in_specs=[pl.BlockSpec((N_DEV, tile, width), lambda i, me: (0, i, 0)), HBM_SPEC],
            out_specs=pl.BlockSpec((tile, width), lambda i, me: (i, 0)),
            scratch_shapes=[pltpu.VMEM((tile, width), BF16), pltpu.SemaphoreType.DMA]),
        compiler_params=_params(("arbitrary",)),
    )(_my_index_operand(), recv, grad)


def _adamw_small(w, g, m, v, name):
    def fn(i, n, w_, g_, m_, v_):
        return _adamw_math(w_, g_, m_, v_)

    r, c = w.shape
    return _rows(fn, [(w, "t"), (g, "t"), (m, "t"), (v, "t")], [], [(c, F32)] * 3, [], _row_tile(r, c), name)


def _norm_fwd(x, w, name):
    return _rows(lambda i, n, x_, w_: (_rms(x_, w_[...]),), [(x, "t")], [w], [(D_MODEL, BF16)], [], 512, name)[0]


def _residual_norm_fwd(x, y, scale, w, name):
    def fn(i, n, x_, y_, w_):
        xn = x_ + scale * y_
        return xn, _rms(xn, w_[...])

    return _rows(fn, [(x, "t"), (y, "t")], [w], [(D_MODEL, F32), (D_MODEL, BF16)], [], 512, name)


def _residual_norm_bwd(x, w, dhs, dres, scale, name):
    nh = len(dhs)

    def fn(i, n, x_, dres_, *rest):
        dh = rest[0]
        for extra in rest[1:nh]:
            dh = dh + extra
        _, vjp = jax.vjp(_rms, x_, rest[nh][...])
        dx, dw = vjp(dh)
        dx = dx + dres_
        return dx, scale * dx, dw

    return _rows(fn, [(x, "t"), (dres, "t")] + [(d, "t") for d in dhs], [w],
                 [(D_MODEL, F32), (D_MODEL, BF16)], [(1, D_MODEL)], 256, name)


def _ffn_fwd(h, w_gu, get_w_down, tag):
    gu = _matmul(h, w_gu, "nn", BF16, tag + "_gu")
    act = _rows(lambda i, n, gu_: (_swiglu(gu_),), [(gu, "t")], [], [(D_FF, BF16)], [], 128, tag + "_swiglu")[0]
    y = _matmul(act, get_w_down(act), "nn", F32, tag + "_down")
    return gu, act, y


def _ffn_bwd(h, gu, act, dy, w_gu, w_down, tag, comm, more=None):
    dact = _matmul(dy, w_down, "nt", BF16, tag + "_dact")

    def fn(i, n, gu_, dact_):
        _, vjp = jax.vjp(_swiglu, gu_)
        return vjp(dact_)

    dgu = _rows(fn, [(gu, "t"), (dact, "t")], [], [(2 * D_FF, BF16)], [], 128, tag + "_swiglu_bwd")[0]
    sent = comm.send(tag + "_gu", {tag + "_w_gu": _matmul(h, dgu, "tn", BF16, tag + "_d_w_gu")})
    sent = sent + comm.send(tag + "_down", {tag + "_w_down": _matmul(act, dy + sent.astype(BF16), "tn", BF16,
                                                                    tag + "_d_w_down"), **(more or {})})
    dh = _matmul(dgu, w_gu, "nt", F32, tag + "_dh")
    return dh, sent


def _expanders():
    e_g = np.zeros((LANES, HW), np.float32)
    e_b = np.zeros((LANES, HW), np.float32)
    for h in range(HEADS):
        e_g[h, h * HEAD_DIM:(h + 1) * HEAD_DIM] = 1.0
        e_b[HEADS + h, h * HEAD_DIM:(h + 1) * HEAD_DIM] = 1.0
    return jnp.asarray(e_g), jnp.asarray(e_b)


def _pad_lanes(v):
    return jnp.pad(v, ((0, 0), (0, LANES - v.shape[1])))


class _LocalWeights:
    def __init__(self, big):
        self.big, self.sent = big, {}

    def arrive(self, group, after):
        return self.big

    def send(self, group, grads):
        self.sent.update(grads)
        return jnp.zeros((), F32)


def _local_step(x, p, tgt, small, comm):
    e_g, e_b = _expanders()
    alog, dtb = _pad_lanes(small["a_log"]), _pad_lanes(small["dt_bias"])
    conv_w = jnp.pad(small["conv_w"], ((0, SUBLANES - CONV_K), (0, 0)))
    rel = small["rel_bias"]

    h1 = _norm_fwd(x, small["ffn1_norm"], "ffn1_norm")
    big = dict(comm.arrive("ffn1", h1))
    if "_token" in big:
        h1 = h1 + big.pop("_token").astype(BF16)

    def ffn1_w_down(act):
        big.update(comm.arrive("ffn1_down", act))
        return big["ffn1_w_down"]

    gu1, act1, y1 = _ffn_fwd(h1, big["ffn1_w_gu"], ffn1_w_down, "ffn1")
    x1, h2 = _residual_norm_fwd(x, y1, 0.5, small["mix_norm"], "mix_norm")

    big = {**big, **comm.arrive("mixer", h2)}
    w_in = big["w_in"]
    w_qz = w_in[:, :IN_QZ]
    w_ab = jnp.pad(w_in[:, IN_AB0:IN_QKVB0], ((0, 0), (0, LANES - 2 * HEADS)))
    w_qkvb = w_in[:, IN_QKVB0:IN_GG0]
    w_gg = w_in[:, IN_GG0:IN_COLS]
    qz = _matmul(h2, w_qz, "nn", F32, "in_qz")
    ab = _matmul(h2, w_ab, "nn", F32, "in_ab")
    pb = _matmul(h2, w_qkvb, "nn", F32, "in_qkvb")
    gg = _matmul(h2, w_gg, "nn", BF16, "in_gates")
    pa, z = qz[:, :3 * HW], qz[:, 3 * HW:]

    def prep(i, n, pa_, prev_, ab_, cw_, alog_, dtb_, eg_, eb_):
        q, k, v = _gdn_post(_conv(pa_, prev_, cw_, i))
        g_b, beta_b = _gdn_gates(ab_, alog_[...], dtb_[...], eg_[...], eb_[...])
        return q, k, v, g_b, beta_b

    qn, kn, vv, g_b, beta_b = _rows(prep, [(pa, "t"), (pa, "p"), (ab, "t")], [conv_w, alog, dtb, e_g, e_b],
                                    [(HW, F32)] * 5, [], 256, "gdn_prep")
    u, w, aqk, qd, kt, tl = _gdn_intra(qn, kn, vv, g_b, beta_b)
    o, states = _gdn_scan(u, w, aqk, qd, kt, tl)
    ya = _rows(lambda i, n, o_, z_, w_: (_gated_norm(o_, z_, w_[...]),), [(o, "t"), (z, "t")], [small["gdn_norm"]],
               [(HW, BF16)], [], 512, "gdn_gated_norm")[0]

    pbp = jnp.pad(pb, ((ATT_PAD, 0), (0, 0)))
    yb = _attention(pb, pbp, small["q_norm"], small["k_norm"], rel)

    big = {**big, **comm.arrive("branches", yb)}
    ta = _matmul(ya, big["w_branch_a"], "nn", BF16, "branch_a")
    tb = _matmul(yb, big["w_branch_b"], "nn", BF16, "branch_b")
    mixed = _rows(lambda i, n, gg_, ta_, tb_: (_mix(gg_, ta_, tb_),), [(gg, "t"), (ta, "t"), (tb, "t")], [],
                  [(D_MODEL, BF16)], [], 256, "mix")[0]
    m_out = _matmul(mixed, big["w_out"], "nn", F32, "w_out")
    x2, h3 = _residual_norm_fwd(x1, m_out, 1.0, small["ffn2_norm"], "ffn2_norm")
    big = {**big, **comm.arrive("tail", h3)}
    gu2, act2, y2 = _ffn_fwd(h3, big["ffn2_w_gu"], lambda act: big["ffn2_w_down"], "ffn2")
    x3, h4 = _residual_norm_fwd(x2, y2, 0.5, small["ple_norm"], "ple_norm")
    gp = _matmul(h4, big["ple_gate"], "nn", BF16, "ple_gate")
    pp = _matmul(p, big["ple_proj"], "nn", BF16, "ple_proj")

    def head(i, n, x3_, gp_, pp_, tgt_):
        sg = _sigmoid(gp_)
        err = x3_ + sg * pp_ - tgt_
        dx4 = err * (1.0 / D_MODEL)
        sq = _colsum(err * err)
        part = sq[:, :LANES]
        for j in range(1, D_MODEL // LANES):
            part = part + sq[:, j * LANES:(j + 1) * LANES]
        return dx4, dx4 * pp_ * sg * (1.0 - sg), dx4 * sg, (0.5 / D_MODEL) * part

    dx4, dgp, dpp, loss_lanes = _rows(head, [(x3, "t"), (gp, "t"), (pp, "t"), (tgt, "t")], [],
                                      [(D_MODEL, F32), (D_MODEL, BF16), (D_MODEL, BF16)], [(1, LANES)], 256,
                                      "ple_loss_head")
    loss = jnp.sum(loss_lanes)

    gbig, gsmall = {}, {}
    gbig["ple_proj"] = _matmul(p, dpp, "tn", BF16, "d_ple_proj")
    gbig["ple_gate"] = _matmul(h4, dgp, "tn", BF16, "d_ple_gate")
    dh4 = _matmul(dgp, big["ple_gate"], "nt", F32, "ple_gate_dh")
    dx3, dy2, gsmall["ple_norm"] = _residual_norm_bwd(x3, small["ple_norm"], [dh4], dx4, 0.5, "ple_norm_bwd")

    dh3, sent = _ffn_bwd(h3, gu2, act2, dy2, big["ffn2_w_gu"], big["ffn2_w_down"], "ffn2", comm,
                         {n: gbig[n] for n in ("ple_proj", "ple_gate")})
    dx2, dx2b, gsmall["ffn2_norm"] = _residual_norm_bwd(x2, small["ffn2_norm"] + sent, [dh3], dx3, 1.0,
                                                        "ffn2_norm_bwd")

    gbig["w_out"] = _matmul(mixed, dx2b, "tn", BF16, "d_w_out")
    dmixed = _matmul(dx2b, big["w_out"], "nt", BF16, "w_out_dx")

    def mix_bwd(i, n, gg_, ta_, tb_, dm_):
        _, vjp = jax.vjp(_mix, gg_, ta_, tb_)
        return vjp(dm_)

    dgg, dta, dtb_ = _rows(mix_bwd, [(gg, "t"), (ta, "t"), (tb, "t"), (dmixed, "t")], [],
                           [(2 * D_MODEL, BF16), (D_MODEL, BF16), (D_MODEL, BF16)], [], 256, "mix_bwd")
    gbig["w_branch_a"] = _matmul(ya, dta, "tn", BF16, "d_branch_a")
    gbig["w_branch_b"] = _matmul(yb, dtb_, "tn", BF16, "d_branch_b")
    dya = _matmul(dta, big["w_branch_a"], "nt", F32, "branch_a_dx")
    dyb = _matmul(dtb_, big["w_branch_b"], "nt", F32, "branch_b_dx")

    dq_b, dk_b, dv_b, gsmall["q_norm"], gsmall["k_norm"], gsmall["rel_bias"] = _attention_bwd(
        pb, pbp, small["q_norm"], small["k_norm"], rel, dyb)
    dpb = jnp.concatenate([dq_b, dk_b[ATT_PAD:].astype(BF16), dv_b[ATT_PAD:].astype(BF16)], axis=1)

    def gated_bwd(i, n, o_, z_, dya_, w_):
        _, vjp = jax.vjp(_gated_norm, o_, z_, w_[...])
        return vjp(dya_)

    do, dz, gsmall["gdn_norm"] = _rows(gated_bwd, [(o, "t"), (z, "t"), (dya, "t")], [small["gdn_norm"]],
                                       [(HW, F32), (HW, BF16)], [(1, HEAD_DIM)], 256, "gdn_gated_norm_bwd")
    du, dw, da, dqd, dkt, dtl = _gdn_scan_bwd(do, u, w, aqk, qd, kt, tl, states)
    dqn, dkn, dvv, dg_b, dbeta_b = _gdn_intra_bwd(qn, kn, vv, g_b, beta_b, du, dw, da, dqd, dkt, dtl)

    def prep_bwd(i, n, pa_, prev_, ab_, dq_, dk_, dv_, dg_, db_, cw_, alog_, dtb_, eg_, eb_):
        _, vjp = jax.vjp(_gdn_post, _conv(pa_, prev_, cw_, i))
        (dy,) = vjp((dq_, dk_, dv_))
        e_g_, e_b_ = eg_[...], eb_[...]
        _, vjp_g = jax.vjp(lambda a, b, c: _gdn_gates(a, b, c, e_g_, e_b_), ab_, alog_[...], dtb_[...])
        dab, dalog, ddtb = vjp_g((dg_, db_))
        return dy, dab, dalog, ddtb

    dy_conv, dab, dalog, ddtb = _rows(
        prep_bwd, [(pa, "t"), (pa, "p"), (ab, "t"), (dqn, "t"), (dkn, "t"), (dvv, "t"), (dg_b, "t"), (dbeta_b, "t")],
        [conv_w, alog, dtb, e_g, e_b], [(3 * HW, F32), (LANES, BF16)], [(1, LANES), (1, LANES)], 256,
        "gdn_prep_bwd")
    gsmall["a_log"] = dalog[:, :HEADS]
    gsmall["dt_bias"] = ddtb[:, :HEADS]

    def conv_bwd(i, n, dy_, nxt_, pa_, prev_, cw_):
        dpa = dy_ * cw_[CONV_K - 1:CONV_K, :]
        row = lax.broadcasted_iota(jnp.int32, (SUBLANES, dy_.shape[1]), 0)
        dcw = jnp.where(row == CONV_K - 1, _colsum(dy_ * pa_), 0.0)
        for j in range(CONV_K - 1):
            s = CONV_K - 1 - j
            dpa = dpa + _shift_up(dy_, nxt_, s, i, n) * cw_[j:j + 1, :]
            dcw = dcw + jnp.where(row == j, _colsum(dy_ * _shift_down(pa_, prev_, s, i)), 0.0)
        return dpa, dcw

    dpa, dcw = _rows(conv_bwd, [(dy_conv, "t"), (dy_conv, "n"), (pa, "t"), (pa, "p")], [conv_w],
                     [(3 * HW, BF16)], [(SUBLANES, 3 * HW)], 256, "gdn_conv_bwd")
    gsmall["conv_w"] = dcw[:CONV_K]

    dqz = jnp.concatenate([dpa, dz], axis=1)
    d_w_qz = _matmul(h2, dqz, "tn", BF16, "d_in_qz")
    d_w_ab = _matmul(h2, dab, "tn", BF16, "d_in_ab")
    d_w_qkvb = _matmul(h2, dpb, "tn", BF16, "d_in_qkvb")
    d_w_gg = _matmul(h2, dgg, "tn", BF16, "d_in_gates")
    gbig["w_in"] = jnp.concatenate([d_w_qz, d_w_ab[:, :2 * HEADS], d_w_qkvb, d_w_gg,
                                    jnp.zeros((D_MODEL, IN_PADDED - IN_COLS), BF16)], axis=1)
    dh2 = [_matmul(dqz, w_qz, "nt", F32, "in_qz_dh"), _matmul(dab, w_ab, "nt", F32, "in_ab_dh"),
           _matmul(dpb, w_qkvb, "nt", F32, "in_qkvb_dh"), _matmul(dgg, w_gg, "nt", F32, "in_gates_dh")]
    sent = comm.send("mixer", {n: gbig[n] for n in ("w_out", "w_branch_b", "w_branch_a", "w_in")})
    dx1, dy1, gsmall["mix_norm"] = _residual_norm_bwd(x1, small["mix_norm"] + sent, dh2, dx2, 0.5, "mix_norm_bwd")

    dh1, sent = _ffn_bwd(h1, gu1, act1, dy1, big["ffn1_w_gu"], big["ffn1_w_down"], "ffn1", comm)
    grad_x, _, gsmall["ffn1_norm"] = _residual_norm_bwd(x, small["ffn1_norm"] + sent, [dh1], dx1, 1.0,
                                                        "ffn1_norm_bwd")
    return loss, grad_x, gsmall


GATHER_GROUPS = {"ffn1": ("ffn1_w_gu",),
                 "ffn1_down": ("ffn1_w_down",),
                 "mixer": ("w_in_main", "w_in_edge"),
                 "branches": ("w_branch_a", "w_branch_b", "w_out"),
                 "tail": ("ffn2_w_gu", "ffn2_w_down", "ple_gate", "ple_proj")}
SPLIT_GATHERS = ("ffn1_down", "mixer", "branches", "tail")


def _kind(name):
    return "cols" if name in COL_SHARDED or name.startswith("w_in_") else "rows"


def _merge_w_in(main, edges):
    edge_w = WIN_W - WIN_STEP
    w_in = jnp.pad(main, ((0, 0), (0, edge_w)))
    for d in range(N_DEV):
        at = WIN_STEP * (d + 1)
        w_in = w_in + jnp.pad(edges[:, d * edge_w:(d + 1) * edge_w], ((0, 0), (at, IN_PADDED - at - edge_w)))
    return w_in


class _Fsdp:
    def __init__(self, wts, first):
        self.wts, self.first_token = wts, first
        main, edge = _roll_w_in(jnp.pad(wts["w_in"], ((0, 0), (0, WIN_W - IN_SHARD))))
        self.shards = {n: wts[n].astype(BF16) for n in BIG if n not in ("w_in", "ffn1_w_gu")}
        self.shards.update(w_in_main=main, w_in_edge=edge)
        self.lands = {n: _place_block(self.shards[n], _kind(n), "own_" + n)
                      for group in SPLIT_GATHERS for n in GATHER_GROUPS[group]}
        self.flight, self.sent = {}, {}

    def _gather_first(self, after):
        token = self.first_token + after[0, 0].astype(F32) * 0.0
        me = _index(*_me())
        for n, land in self.lands.items():
            r, c = self.shards[n].shape
            at = (me * r, 0) if _kind(n) == "rows" else (0, me * c)
            token = token + lax.dynamic_slice(land, at, (1, 1))[0, 0].astype(F32) * 0.0
        shard = (self.wts["ffn1_w_gu"] + token).astype(BF16)
        self.shards["ffn1_w_gu"] = shard
        first = _all_gather([shard], [_kind("ffn1_w_gu")], 1)[0]
        token = first[0, 0].astype(F32) * 0.0
        for group in SPLIT_GATHERS:
            names = GATHER_GROUPS[group]
            srcs = [self.shards[n] for n in names]
            lands = [self.lands[n] for n in names]
            make = _gather_copies([s.shape for s in srcs], [_kind(n) for n in names])
            srcs[0] = srcs[0] + token.astype(BF16)
            send_sems, recv_sems, srcs, lands, tok = _split_start(srcs, lands, make, "gather_start_" + group)
            token = token + tok[0, 0]
            self.flight[group] = (send_sems, recv_sems, srcs, lands, make)
        return {"ffn1_w_gu": first, "_token": token}

    def arrive(self, group, after):
        if group == "ffn1":
            return self._gather_first(after)
        send_sems, recv_sems, srcs, lands, make = self.flight[group]
        _, lands = _split_wait(send_sems, recv_sems, srcs, lands, after, make, "gather_wait_" + group)
        full = dict(zip(GATHER_GROUPS[group], lands))
        if group == "mixer":
            full["w_in"] = _merge_w_in(full.pop("w_in_main"), full.pop("w_in_edge"))
        return full

    def send(self, group, grads):
        names = list(grads)
        kinds = ["win" if n == "w_in" else _kind(n) for n in names]
        shapes = [(D_MODEL, WIN_W) if n == "w_in" else self.shards[n].shape for n in names]
        srcs = [grads[n] for n in names]
        lands = [lax.empty((N_DEV,) + tuple(s), BF16) for s in shapes]
        make = _exchange_copies(shapes, kinds)
        send_sems, recv_sems, srcs, lands, tok = _split_start(srcs, lands, make, "grads_start_" + group)
        self.sent[group] = (names, kinds, send_sems, recv_sems, srcs, lands, make)
        return tok[0, 0]

    def received(self, group, after):
        names, kinds, send_sems, recv_sems, srcs, lands, make = self.sent[group]
        srcs, lands = _split_wait(send_sems, recv_sems, srcs, lands, after, make, "grads_wait_" + group)
        return {n: (k, g, r) for n, k, g, r in zip(names, kinds, srcs, lands)}


SMALL_ROWS = ("ffn1_norm", "mix_norm", "ffn2_norm", "ple_norm", "gdn_norm", "q_norm", "k_norm", "a_log", "dt_bias",
              "rel_bias", "conv_w")


def _pack_small(vals):
    rows = []
    for n in SMALL_ROWS:
        v = vals[n]
        if n == "rel_bias":
            v = jnp.pad(v, ((0, 0), (0, 2 * LANES - N_REL)))
        elif n in ("a_log", "dt_bias"):
            v = _pad_lanes(v)
        rows.append(v.reshape(-1, LANES))
    packed = jnp.concatenate(rows, axis=0)
    return jnp.pad(packed, ((0, -packed.shape[0] % SUBLANES), (0, 0)))


def _unpack_small(packed, shapes):
    out, off = {}, 0
    for n in SMALL_ROWS:
        shp = shapes[n]
        if n == "rel_bias":
            out[n] = packed[off:off + 2 * HEADS].reshape(HEADS, 2 * LANES)[:, :N_REL]
            off += 2 * HEADS
        elif n in ("a_log", "dt_bias"):
            out[n] = packed[off:off + 1, :HEADS]
            off += 1
        else:
            r = int(np.prod(shp)) // LANES
            out[n] = packed[off:off + r].reshape(shp)
            off += r
    return out


WEIGHTS = ("ffn1_norm", "ffn1_w_gu", "ffn1_w_down", "mix_norm", "w_in", "conv_w", "a_log", "dt_bias", "gdn_norm",
           "q_norm", "k_norm", "rel_bias", "w_branch_a", "w_branch_b", "w_out", "ffn2_norm", "ffn2_w_gu",
           "ffn2_w_down", "ple_norm", "ple_gate", "ple_proj")


def kernel(x, p, ffn1_norm, ffn1_w_gu, ffn1_w_down, mix_norm, w_in, conv_w, a_log, dt_bias, gdn_norm, q_norm, k_norm, rel_bias, w_branch_a, w_branch_b, w_out, ffn2_norm, ffn2_w_gu, ffn2_w_down, ple_norm, ple_gate, ple_proj, loss_target, m_ffn1_norm, m_ffn1_w_gu, m_ffn1_w_down, m_mix_norm, m_w_in, m_conv_w, m_a_log, m_dt_bias, m_gdn_norm, m_q_norm, m_k_norm, m_rel_bias, m_w_branch_a, m_w_branch_b, m_w_out, m_ffn2_norm, m_ffn2_w_gu, m_ffn2_w_down, m_ple_norm, m_ple_gate, m_ple_proj, v_ffn1_norm, v_ffn1_w_gu, v_ffn1_w_down, v_mix_norm, v_w_in, v_conv_w, v_a_log, v_dt_bias, v_gdn_norm, v_q_norm, v_k_norm, v_rel_bias, v_w_branch_a, v_w_branch_b, v_w_out, v_ffn2_norm, v_ffn2_w_gu, v_ffn2_w_down, v_ple_norm, v_ple_gate, v_ple_proj):
    args = dict(locals())
    def layer0(v):
        return v[0] if v.ndim == 3 else v

    wts = {n: layer0(args[n]) for n in WEIGHTS}
    mom = {n: layer0(args["m_" + n]) for n in WEIGHTS}
    var = {n: layer0(args["v_" + n]) for n in WEIGHTS}
    x2d, p2d, tgt = x[0], p[0, 0], loss_target[0]
    my_index = _index(*_me())

    small = {n: wts[n] for n in SMALL_ROWS if n != "conv_w"}
    conv_shard = wts["conv_w"]
    conv_cols = conv_shard.shape[1]
    conv_packed = jnp.zeros((SUBLANES, N_DEV * conv_cols), F32)
    conv_packed = lax.dynamic_update_slice(conv_packed, jnp.pad(conv_shard, ((0, SUBLANES - CONV_K), (0, 0))),
                                           (0, my_index * conv_cols))
    small["conv_w"] = _all_reduce_small(conv_packed.reshape(-1, LANES), "conv_w_gather").reshape(SUBLANES, -1)[:CONV_K]

    fsdp = _Fsdp(wts, small["conv_w"][0, 0] * 0.0)

    loss, grad_x, gsmall = _local_step(x2d, p2d, tgt, small, fsdp)
    loss = lax.psum(loss, ("x", "y", "c"))

    outs_big, after = {}, grad_x
    for group in list(fsdp.sent):
        for n, (kind, grad, recv) in fsdp.received(group, after).items():
            if n == "w_in":
                g_in = _sum_w_in_windows(recv, grad)[:, :IN_SHARD]
                outs_big[n] = [g_in] + list(_adamw_small(wts[n], g_in, mom[n], var[n], "adamw_w_in"))
            else:
                outs_big[n] = _adamw_recv(recv, grad, kind, wts[n], mom[n], var[n], "adamw_" + n)
            after = outs_big[n][1]

    small_shapes = {n: (small[n].shape if n != "conv_w" else (CONV_K, N_DEV * conv_cols)) for n in SMALL_ROWS}
    gsum = _unpack_small(_all_reduce_small(_pack_small(gsmall), "small_grads_all_reduce"), small_shapes)
    gsum["conv_w"] = lax.dynamic_slice(gsum["conv_w"], (0, my_index * conv_cols), (CONV_K, conv_cols))
    rep = [n for n in SMALL_ROWS if n != "conv_w"]
    rep_shapes = {n: small_shapes[n] for n in rep}

    def pack_rep(vals):
        return _pack_small({**{n: vals[n] for n in rep}, "conv_w": jnp.zeros((CONV_K, LANES), F32)})

    def unpack_rep(packed):
        return _unpack_small(packed, {**rep_shapes, "conv_w": (CONV_K, LANES)})

    outs_small = [unpack_rep(o) for o in _adamw_small(pack_rep(wts), pack_rep(gsum), pack_rep(mom), pack_rep(var),
                                                      "adamw_replicated")]
    pad8 = functools.partial(jnp.pad, pad_width=((0, SUBLANES - CONV_K), (0, 0)))
    outs_conv = [o[:CONV_K] for o in _adamw_small(pad8(conv_shard), pad8(gsum["conv_w"]), pad8(mom["conv_w"]),
                                                   pad8(var["conv_w"]), "adamw_conv")]

    def leaf(kind, n):
        if n in BIG:
            return outs_big[n][kind][None]
        if n == "conv_w":
            return (gsum["conv_w"] if kind == 0 else outs_conv[kind - 1])[None]
        return (gsum[n] if kind == 0 else outs_small[kind - 1][n]).reshape(args[n].shape)

    result = [loss, grad_x[None]]
    for kind in range(4):
        result += [leaf(kind, n) for n in WEIGHTS]
    return tuple(result)
```

```python
import functools

import numpy as np
import jax
import jax.numpy as jnp
from jax import lax
from jax.experimental import pallas as pl
from jax.experimental.pallas import tpu as pltpu

F32 = jnp.float32
BF16 = jnp.bfloat16
HIGHEST = lax.Precision.HIGHEST
MESH = pl.DeviceIdType.MESH

D_MODEL = 2048
D_FF = 5632
HEADS = 8
HEAD_DIM = 128
HW = HEADS * HEAD_DIM
CHUNK = 64
LEFT_CHUNKS = 8
MAX_REL = 128
N_REL = (CHUNK - 1) + MAX_REL + 1
CONV_K = 4
EPS = 1e-6
NEG_INF = -1e30
N_DEV = 8
LANES = 128
SUBLANES = 8
VMEM_LIMIT = 56 * 1024 * 1024

MATMUL_WHOLE_K = 2048

ATT_QB = 256
ATT_KW = ATT_QB + LEFT_CHUNKS * CHUNK
ATT_PAD = LEFT_CHUNKS * CHUNK
GDN_CB = 8
GDN_GROUP = 8

ADAM_LR = 0.001
ADAM_B1 = 0.9
ADAM_B2 = 0.999
ADAM_EPS = 1e-08
ADAM_WD = 0.01
ADAM_STEP = 10

IN_QZ = 3 * HW + HW
IN_AB0 = IN_QZ
IN_QKVB0 = IN_AB0 + 2 * HEADS
IN_GG0 = IN_QKVB0 + 3 * HW
IN_COLS = IN_GG0 + 2 * D_MODEL

BIG = ("ffn1_w_gu", "ffn1_w_down", "w_in", "w_branch_a", "w_branch_b", "w_out",
       "ffn2_w_gu", "ffn2_w_down", "ple_gate", "ple_proj")
COL_SHARDED = ("ffn1_w_gu", "w_in", "w_branch_a", "w_branch_b", "ffn2_w_gu", "ple_proj")


def _params(semantics=None, **kw):
    return pltpu.CompilerParams(dimension_semantics=semantics, vmem_limit_bytes=VMEM_LIMIT, **kw)


def _pick(n, cands):
    for c in cands:
        if n % c == 0:
            return c
    return n


def _matmul(a, b, mode, out_dtype, name):
    if mode == "nn":
        (m, k), (k2, n) = a.shape, b.shape
    elif mode == "nt":
        (m, k), (n, k2) = a.shape, b.shape
    else:
        (k, m), (k2, n) = a.shape, b.shape
    assert k == k2, (a.shape, b.shape, mode)
    tm = _pick(m, (1024, 512, 256, 128))
    tn = _pick(n, (1024, 512, 256, 128))
    tk = k if k <= MATMUL_WHOLE_K else _pick(k, (2816, 2048, 1536, 1024, 512, 256, 128))
    nk = k // tk
    if mode == "nn":
        a_spec = pl.BlockSpec((tm, tk), lambda i, j, kk: (i, kk))
        b_spec = pl.BlockSpec((tk, tn), lambda i, j, kk: (kk, j))
        dims = (((1,), (0,)), ((), ()))
    elif mode == "nt":
        a_spec = pl.BlockSpec((tm, tk), lambda i, j, kk: (i, kk))
        b_spec = pl.BlockSpec((tn, tk), lambda i, j, kk: (j, kk))
        dims = (((1,), (1,)), ((), ()))
    else:
        a_spec = pl.BlockSpec((tk, tm), lambda i, j, kk: (kk, i))
        b_spec = pl.BlockSpec((tk, tn), lambda i, j, kk: (kk, j))
        dims = (((0,), (0,)), ((), ()))

    def body(a_ref, b_ref, o_ref, *acc):
        prod = lax.dot_general(a_ref[...].astype(BF16), b_ref[...].astype(BF16), dims, preferred_element_type=F32)
        if nk == 1:
            o_ref[...] = prod.astype(o_ref.dtype)
            return
        acc_ref, kk = acc[0], pl.program_id(2)

        @pl.when(kk == 0)
        def _():
            acc_ref[...] = prod

        @pl.when((kk > 0) & (kk < nk - 1))
        def _():
            acc_ref[...] += prod

        @pl.when(kk == nk - 1)
        def _():
            o_ref[...] = (acc_ref[...] + prod).astype(o_ref.dtype)

    return pl.pallas_call(
        body, name=name,
        out_shape=jax.ShapeDtypeStruct((m, n), out_dtype),
        grid=(m // tm, n // tn, nk),
        in_specs=[a_spec, b_spec],
        out_specs=pl.BlockSpec((tm, tn), lambda i, j, kk: (i, j)),
        scratch_shapes=[pltpu.VMEM((tm, tn), F32)] if nk > 1 else [],
        compiler_params=_params(("parallel", "parallel", "arbitrary")),
    )(a, b)


def _rows(fn, row_ins, consts, row_outs, acc_outs, tile, name):
    t_rows = row_ins[0][0].shape[0]
    tile = min(tile, t_rows)
    assert t_rows % tile == 0 and tile % SUBLANES == 0
    n = t_rows // tile
    per = tile // SUBLANES
    last8 = t_rows // SUBLANES - 1
    in_specs = []
    for arr, kind in row_ins:
        c = arr.shape[1]
        if kind == "t":
            in_specs.append(pl.BlockSpec((tile, c), lambda i: (i, 0)))
        elif kind == "p":
            in_specs.append(pl.BlockSpec((SUBLANES, c), lambda i: (jnp.maximum(i * per - 1, 0), 0)))
        else:
            in_specs.append(pl.BlockSpec((SUBLANES, c), lambda i: (jnp.minimum((i + 1) * per, last8), 0)))
    for arr in consts:
        in_specs.append(pl.BlockSpec(arr.shape, lambda i, nd=arr.ndim: (0,) * nd))
    out_shape = [jax.ShapeDtypeStruct((t_rows, c), dt) for c, dt in row_outs]
    out_specs = [pl.BlockSpec((tile, c), lambda i: (i, 0)) for c, _ in row_outs]
    for shp in acc_outs:
        out_shape.append(jax.ShapeDtypeStruct(shp, F32))
        out_specs.append(pl.BlockSpec(shp, lambda i, nd=len(shp): (0,) * nd))
    n_in = len(row_ins) + len(consts)
    n_row_out = len(row_outs)

    def body(*refs):
        i = pl.program_id(0)
        vals = [r[...].astype(F32) for r in refs[:len(row_ins)]]
        res = fn(i, n, *vals, *refs[len(row_ins):n_in])
        outs = refs[n_in:]
        for r, v in zip(outs[:n_row_out], res[:n_row_out]):
            r[...] = v.astype(r.dtype)
        if acc_outs:
            @pl.when(i == 0)
            def _():
                for r in outs[n_row_out:]:
                    r[...] = jnp.zeros_like(r)

            for r, v in zip(outs[n_row_out:], res[n_row_out:]):
                r[...] += v

    res = pl.pallas_call(
        body, name=name, out_shape=out_shape, grid=(n,), in_specs=in_specs, out_specs=out_specs,
        compiler_params=_params(("arbitrary",) if acc_outs else ("parallel",)),
    )(*[a for a, _ in row_ins], *consts)
    return res


def _rms(x, w):
    return x * lax.rsqrt(jnp.mean(x * x, axis=-1, keepdims=True) + EPS) * w


def _l2n(x):
    return x * lax.rsqrt(jnp.sum(x * x, axis=-1, keepdims=True) + EPS)


def _sigmoid(x):
    return 1.0 / (1.0 + jnp.exp(-x))


def _silu(x):
    return x * _sigmoid(x)


def _softplus(x):
    return jnp.maximum(x, 0.0) + jnp.log(1.0 + jnp.exp(-jnp.abs(x)))


def _heads(fn, *xs):
    nh = xs[0].shape[1] // HEAD_DIM
    return jnp.concatenate(
        [fn(*[x[:, h * HEAD_DIM:(h + 1) * HEAD_DIM] for x in xs]) for h in range(nh)], axis=1)


def _colsum(x):
    return jnp.sum(x, axis=0, keepdims=True)


def _swiglu(gu):
    return _silu(gu[:, :D_FF]) * gu[:, D_FF:]


def _gated_norm(o, z, w):
    return _heads(lambda oh, zh: _rms(oh, w) * _silu(zh), o, z)


def _mix(gg, ta, tb):
    return _sigmoid(gg[:, :D_MODEL]) * ta + _sigmoid(gg[:, D_MODEL:]) * tb


def _gdn_post(y):
    a = _silu(y)
    q = _heads(lambda v: _l2n(v) * (HEAD_DIM ** -0.5), a[:, :HW])
    k = _heads(_l2n, a[:, HW:2 * HW])
    return q, k, a[:, 2 * HW:]


NN = (((1,), (0,)), ((), ()))
NT = (((1,), (1,)), ((), ()))
TN = (((0,), (0,)), ((), ()))


def _dg(a, b, dims):
    return lax.dot_general(a, b, dims, preferred_element_type=F32)


def _split2(x):
    hi = x.astype(BF16)
    return hi, (x - hi.astype(F32)).astype(BF16)


def _split3(x):
    hi = x.astype(BF16)
    r = x - hi.astype(F32)
    mid = r.astype(BF16)
    return hi, mid, (r - mid.astype(F32)).astype(BF16)


def _dg3(a, b, dims):
    ah, al = _split2(a)
    bh, bl = _split2(b)
    return _dg(ah, bh, dims) + (_dg(ah, bl, dims) + _dg(al, bh, dims))


BNN = (((2,), (1,)), ((0,), (0,)))
BNT = (((2,), (2,)), ((0,), (0,)))
BTN = (((1,), (1,)), ((0,), (0,)))


@jax.custom_vjp
def _mm3(a, b):
    return _dg3(a, b, BNN)


_mm3.defvjp(lambda a, b: (_dg3(a, b, BNN), (a, b)),
            lambda res, g: (_dg3(g, res[1], BNT), _dg3(res[0], g, BTN)))


def _xm(x, m, dims):
    mb = m.astype(BF16)
    parts = _split3(x)
    return _dg(parts[0], mb, dims) + (_dg(parts[1], mb, dims) + _dg(parts[2], mb, dims))


def _mx(m, x, dims):
    mb = m.astype(BF16)
    parts = _split3(x)
    return _dg(mb, parts[0], dims) + (_dg(mb, parts[1], dims) + _dg(mb, parts[2], dims))


@jax.custom_vjp
def _times_const(x, m):
    return _xm(x, m, NN)


_times_const.defvjp(lambda x, m: (_xm(x, m, NN), m),
                    lambda m, g: (_xm(g, m, NT), jnp.zeros_like(m)))


@jax.custom_vjp
def _const_times(m, x):
    return _mx(m, x, NN)


_const_times.defvjp(lambda m, x: (_mx(m, x, NN), m),
                    lambda m, g: (jnp.zeros_like(m), _mx(m, g, TN)))


@jax.custom_vjp
def _lane_mean_cols(x, avg):
    return _mx(avg, x, BNT)


_lane_mean_cols.defvjp(lambda x, avg: (_mx(avg, x, BNT), avg),
                       lambda avg, g: (_xm(g, avg, BTN), jnp.zeros_like(avg)))


def _gdn_gates(ab, alog, dtb, e_g, e_b):
    t = ab.shape[0]
    g = -jnp.exp(alog) * _softplus(ab + dtb)
    beta = _sigmoid(ab)
    ri = lax.broadcasted_iota(jnp.int32, (t, t), 0)
    ci = lax.broadcasted_iota(jnp.int32, (t, t), 1)
    shift = CHUNK.bit_length() - 1
    same = jnp.right_shift(ri, shift) == jnp.right_shift(ci, shift)
    tril = jnp.where(same & (ri >= ci), 1.0, 0.0).astype(F32)
    gc = _const_times(tril, g)
    return _times_const(gc, e_g), _times_const(beta, e_b)


def _shift_down(x, halo, s, i):
    if s == 0:
        return x
    halo = jnp.where(i == 0, 0.0, halo)
    xr = pltpu.roll(x, s, 0)
    hr = pltpu.roll(halo, s, 0)
    row = lax.broadcasted_iota(jnp.int32, (SUBLANES, x.shape[1]), 0)
    top = jnp.where(row < s, hr, xr[:SUBLANES])
    return jnp.concatenate([top, xr[SUBLANES:]], axis=0)


def _shift_up(x, halo, s, i, n):
    if s == 0:
        return x
    t = x.shape[0]
    halo = jnp.where(i == n - 1, 0.0, halo)
    xr = pltpu.roll(x, t - s, 0)
    hr = pltpu.roll(halo, SUBLANES - s, 0)
    row = lax.broadcasted_iota(jnp.int32, (SUBLANES, x.shape[1]), 0)
    bot = jnp.where(row >= SUBLANES - s, hr, xr[t - SUBLANES:])
    return jnp.concatenate([xr[:t - SUBLANES], bot], axis=0)


def _conv(pa, prev, cw_ref, i):
    y = pa * cw_ref[CONV_K - 1:CONV_K, :]
    for j in range(CONV_K - 1):
        y = y + _shift_down(pa, prev, CONV_K - 1 - j, i) * cw_ref[j:j + 1, :]
    return y


def _dot_nt(a, b, precision=None):
    return lax.dot_general(a, b, (((1,), (1,)), ((), ())), precision=precision, preferred_element_type=F32)


def _dot_tn(a, b, precision=None):
    return lax.dot_general(a, b, (((0,), (0,)), ((), ())), precision=precision, preferred_element_type=F32)


def _dot(a, b, precision=None):
    return jnp.dot(a, b, precision=precision, preferred_element_type=F32)


def _bf(x):
    return x.astype(BF16)


def _neumann_inverse(lmat):
    nb, c, _ = lmat.shape
    ri = lax.broadcasted_iota(jnp.int32, (nb, c, c), 1)
    ci = lax.broadcasted_iota(jnp.int32, (nb, c, c), 2)
    pw = -lmat
    inv = jnp.where(ri == ci, 1.0, 0.0).astype(F32) + pw
    for _ in range(5):
        pw = _mm3(pw, pw)
        inv = inv + _mm3(inv, pw)
    return inv


@jax.custom_vjp
def _unit_lower_inverse(lmat):
    return _neumann_inverse(lmat)


def _unit_lower_inverse_fwd(lmat):
    inv = _neumann_inverse(lmat)
    return inv, inv


def _unit_lower_inverse_bwd(inv, g):
    return (-_dg3(_dg3(inv, g, BTN), inv, BNT),)


_unit_lower_inverse.defvjp(_unit_lower_inverse_fwd, _unit_lower_inverse_bwd)


def _gdn_chunk(q, k, v, gc, bb):
    nb, c, _ = q.shape
    ri = lax.broadcasted_iota(jnp.int32, (nb, c, c), 1)
    ci = lax.broadcasted_iota(jnp.int32, (nb, c, c), 2)
    incl = ri >= ci
    strict = ri > ci
    g_row = gc[:, :, :c]
    g_col = _lane_mean_cols(gc, jnp.full((nb, c, LANES), 1.0 / LANES, F32))
    decay = jnp.where(incl, jnp.exp(jnp.where(incl, g_row - g_col, 0.0)), 0.0)
    kb = k * bb
    lmat = jnp.where(strict, _dg(_bf(kb), _bf(k), BNT) * decay, 0.0)
    inv = _unit_lower_inverse(lmat)
    egc = jnp.exp(gc)
    u = _mm3(inv, v * bb)
    w = _mm3(inv, kb * egc)
    aqk = _dg(_bf(q), _bf(k), BNT) * decay
    last = lax.broadcasted_iota(jnp.int32, (nb, c, LANES), 1) == c - 1
    tot = jnp.sum(jnp.where(last, gc, 0.0), axis=1, keepdims=True)
    k_tail = k * jnp.exp(tot - gc)
    tail = jnp.broadcast_to(jnp.exp(tot), (nb, SUBLANES, LANES))
    return u, w, aqk, q * egc, k_tail, tail


def _gdn_intra(qn, kn, vv, g_b, beta_b):
    t_rows = qn.shape[0]
    nc = t_rows // CHUNK
    cb = min(GDN_CB, nc)
    rows = cb * CHUNK
    col = pl.BlockSpec((rows, HEAD_DIM), lambda h, b: (b, h))

    def body(q_ref, k_ref, v_ref, g_ref, b_ref, u_ref, w_ref, a_ref, qd_ref, kt_ref, tl_ref):
        def group(gi, carry):
            r = pl.ds(pl.multiple_of(gi * (grp * CHUNK), grp * CHUNK), grp * CHUNK)
            ins = [ref[r, :].reshape(grp, CHUNK, HEAD_DIM) for ref in (q_ref, k_ref, v_ref, g_ref, b_ref)]
            u, w, aqk, qd, kt, tl = _gdn_chunk(*ins)
            for ref, val in ((u_ref, u), (w_ref, w), (qd_ref, qd), (kt_ref, kt)):
                ref[r, :] = val.reshape(grp * CHUNK, HEAD_DIM)
            a_ref[0, r, :] = aqk.reshape(grp * CHUNK, CHUNK)
            tl_ref[0, pl.ds(gi * grp, grp)] = tl
            return carry

        grp = min(GDN_GROUP, cb)
        lax.fori_loop(0, cb // grp, group, 0)

    full = jax.ShapeDtypeStruct((t_rows, HW), F32)
    return pl.pallas_call(
        body, name="gdn_intra_fwd",
        out_shape=[full, full, jax.ShapeDtypeStruct((HEADS, t_rows, CHUNK), F32), full, full,
                   jax.ShapeDtypeStruct((HEADS, nc, SUBLANES, LANES), F32)],
        grid=(HEADS, nc // cb),
        in_specs=[col] * 5,
        out_specs=[col, col, pl.BlockSpec((1, rows, CHUNK), lambda h, b: (h, b, 0)), col, col,
                   pl.BlockSpec((1, cb, SUBLANES, LANES), lambda h, b: (h, b, 0, 0))],
        compiler_params=_params(("parallel", "parallel")),
    )(qn, kn, vv, g_b, beta_b)


def _gdn_intra_bwd(qn, kn, vv, g_b, beta_b, du, dw, da, dqd, dkt, dtl):
    t_rows = qn.shape[0]
    nc = t_rows // CHUNK
    cb = min(GDN_CB, nc)
    rows = cb * CHUNK
    col = pl.BlockSpec((rows, HEAD_DIM), lambda h, b: (b, h))
    a_spec = pl.BlockSpec((1, rows, CHUNK), lambda h, b: (h, b, 0))
    tl_spec = pl.BlockSpec((1, cb, SUBLANES, LANES), lambda h, b: (h, b, 0, 0))

    def body(q_ref, k_ref, v_ref, g_ref, b_ref, du_ref, dw_ref, da_ref, dqd_ref, dkt_ref, dtl_ref,
             dq_ref, dk_ref, dv_ref, dg_ref, db_ref):
        def group(gi, carry):
            r = pl.ds(pl.multiple_of(gi * (grp * CHUNK), grp * CHUNK), grp * CHUNK)
            wide = (grp, CHUNK, HEAD_DIM)
            ins = [ref[r, :].reshape(wide) for ref in (q_ref, k_ref, v_ref, g_ref, b_ref)]
            cts = (du_ref[r, :].reshape(wide), dw_ref[r, :].reshape(wide),
                   da_ref[0, r, :].reshape(grp, CHUNK, CHUNK), dqd_ref[r, :].reshape(wide),
                   dkt_ref[r, :].reshape(wide), dtl_ref[0, pl.ds(gi * grp, grp)])
            grads = jax.vjp(_gdn_chunk, *ins)[1](cts)
            for ref, val in zip((dq_ref, dk_ref, dv_ref, dg_ref, db_ref), grads):
                ref[r, :] = val.reshape(grp * CHUNK, HEAD_DIM)
            return carry

        grp = min(GDN_GROUP, cb)
        lax.fori_loop(0, cb // grp, group, 0)

    full = jax.ShapeDtypeStruct((t_rows, HW), F32)
    return pl.pallas_call(
        body, name="gdn_intra_bwd",
        out_shape=[full] * 5,
        grid=(HEADS, nc // cb),
        in_specs=[col] * 7 + [a_spec, col, col, tl_spec],
        out_specs=[col] * 5,
        compiler_params=_params(("parallel", "parallel")),
    )(qn, kn, vv, g_b, beta_b, du, dw, da, dqd, dkt, dtl)


def _head_cols(h):
    return slice(h * HEAD_DIM, (h + 1) * HEAD_DIM)


def _gdn_scan(u, w, aqk, qd, kt, tl):
    t_rows = u.shape[0]
    nc = t_rows // CHUNK
    cb = min(GDN_CB, nc)
    rows = cb * CHUNK
    wide = pl.BlockSpec((rows, HW), lambda b: (b, 0))

    def body(u_ref, w_ref, a_ref, qd_ref, kt_ref, tl_ref, o_ref, s_out_ref, s_ref):
        @pl.when(pl.program_id(0) == 0)
        def _():
            s_ref[...] = jnp.zeros_like(s_ref)

        def chunk(ci, carry):
            r = pl.ds(pl.multiple_of(ci * CHUNK, CHUNK), CHUNK)
            for h in range(HEADS):
                hc = _head_cols(h)
                s = s_ref[h]
                s_out_ref[ci, h] = s
                sb = _bf(s)
                vn = u_ref[r, hc] - _dot(_bf(w_ref[r, hc]), sb)
                vnb = _bf(vn)
                o_ref[r, hc] = _dot(_bf(qd_ref[r, hc]), sb) + _dot(_bf(a_ref[h, r, :]), vnb)
                s_ref[h] = s * tl_ref[h, ci, 0:1, :] + _dot_tn(_bf(kt_ref[r, hc]), vnb)
            return carry

        lax.fori_loop(0, cb, chunk, 0)

    return pl.pallas_call(
        body, name="gdn_scan_fwd",
        out_shape=[jax.ShapeDtypeStruct((t_rows, HW), F32),
                   jax.ShapeDtypeStruct((nc, HEADS, HEAD_DIM, HEAD_DIM), F32)],
        grid=(nc // cb,),
        in_specs=[wide, wide, pl.BlockSpec((HEADS, rows, CHUNK), lambda b: (0, b, 0)), wide, wide,
                  pl.BlockSpec((HEADS, cb, SUBLANES, LANES), lambda b: (0, b, 0, 0))],
        out_specs=[wide, pl.BlockSpec((cb, HEADS, HEAD_DIM, HEAD_DIM), lambda b: (b, 0, 0, 0))],
        scratch_shapes=[pltpu.VMEM((HEADS, HEAD_DIM, HEAD_DIM), F32)],
        compiler_params=_params(("arbitrary",)),
    )(u, w, aqk, qd, kt, tl)


def _gdn_scan_bwd(do, u, w, aqk, qd, kt, tl, states):
    t_rows = u.shape[0]
    nc = t_rows // CHUNK
    cb = min(GDN_CB, nc)
    rows = cb * CHUNK
    nb = nc // cb
    wide = pl.BlockSpec((rows, HW), lambda b: (nb - 1 - b, 0))
    a_spec = pl.BlockSpec((HEADS, rows, CHUNK), lambda b: (0, nb - 1 - b, 0))
    tl_spec = pl.BlockSpec((HEADS, cb, SUBLANES, LANES), lambda b: (0, nb - 1 - b, 0, 0))

    def body(do_ref, u_ref, w_ref, a_ref, qd_ref, kt_ref, tl_ref, s_in_ref,
             du_ref, dw_ref, da_ref, dqd_ref, dkt_ref, dtl_ref, ds_ref):
        @pl.when(pl.program_id(0) == 0)
        def _():
            ds_ref[...] = jnp.zeros_like(ds_ref)

        row0 = lax.broadcasted_iota(jnp.int32, (SUBLANES, LANES), 0) == 0

        def chunk(step, carry):
            ci = cb - 1 - step
            r = pl.ds(pl.multiple_of(ci * CHUNK, CHUNK), CHUNK)
            for h in range(HEADS):
                hc = _head_cols(h)
                s = s_in_ref[ci, h]
                ds_next = ds_ref[h]
                sb, dsb = _bf(s), _bf(ds_next)
                wb, ab, ktb, qdb = _bf(w_ref[r, hc]), _bf(a_ref[h, r, :]), _bf(kt_ref[r, hc]), _bf(qd_ref[r, hc])
                dob = _bf(do_ref[r, hc])
                vn = u_ref[r, hc] - _dot(wb, sb)
                vnb = _bf(vn)
                dvn = _dot_tn(ab, dob) + _dot(ktb, dsb)
                dvnb = _bf(dvn)
                du_ref[r, hc] = dvn
                dw_ref[r, hc] = -_dot_nt(dvnb, sb)
                da_ref[h, r, :] = _dot_nt(dob, vnb)
                dqd_ref[r, hc] = _dot_nt(dob, sb)
                dkt_ref[r, hc] = _dot_nt(vnb, dsb)
                dtl_ref[h, ci] = jnp.where(row0, _colsum(s * ds_next), 0.0)
                ds_ref[h] = _dot_tn(qdb, dob) + ds_next * tl_ref[h, ci, 0:1, :] - _dot_tn(wb, dvnb)
            return carry

        lax.fori_loop(0, cb, chunk, 0)

    full = jax.ShapeDtypeStruct((t_rows, HW), F32)
    return pl.pallas_call(
        body, name="gdn_scan_bwd",
        out_shape=[full, full, jax.ShapeDtypeStruct((HEADS, t_rows, CHUNK), F32), full, full,
                   jax.ShapeDtypeStruct((HEADS, nc, SUBLANES, LANES), F32)],
        grid=(nb,),
        in_specs=[wide, wide, wide, a_spec, wide, wide, tl_spec,
                  pl.BlockSpec((cb, HEADS, HEAD_DIM, HEAD_DIM), lambda b: (nb - 1 - b, 0, 0, 0))],
        out_specs=[wide, wide, a_spec, wide, wide, tl_spec],
        scratch_shapes=[pltpu.VMEM((HEADS, HEAD_DIM, HEAD_DIM), F32)],
        compiler_params=_params(("arbitrary",)),
    )(do, u, w, aqk, qd, kt, tl, states)


def _att_rel_index():
    qi = lax.broadcasted_iota(jnp.int32, (ATT_QB, ATT_KW), 0)
    kj = lax.broadcasted_iota(jnp.int32, (ATT_QB, ATT_KW), 1)
    return jnp.clip(qi - kj + ATT_PAD, -(CHUNK - 1), MAX_REL) + (CHUNK - 1)


def _att_in_band():
    qi = lax.broadcasted_iota(jnp.int32, (ATT_QB, ATT_KW), 0)
    kj = lax.broadcasted_iota(jnp.int32, (ATT_QB, ATT_KW), 1)
    shift = CHUNK.bit_length() - 1
    qc = jnp.right_shift(qi, shift)
    kc = jnp.right_shift(kj, shift) - LEFT_CHUNKS
    return (kc <= qc) & (kc >= qc - LEFT_CHUNKS)


def _att_valid(b):
    kj = lax.broadcasted_iota(jnp.int32, (1, ATT_KW), 1)
    return jnp.where(kj + b * ATT_QB >= ATT_PAD, 0.0, NEG_INF)


def _rms_parts(x, w):
    r = lax.rsqrt(jnp.mean(x * x, axis=-1, keepdims=True) + EPS)
    xn = x * r
    return xn * w, xn, r


def _rms_bwd(dy, xn, r, w):
    dxn = dy * w
    dx = r * (dxn - xn * jnp.mean(dxn * xn, axis=-1, keepdims=True))
    return dx, _colsum(dy * xn)


def _att_probs(qb, kb, bias, before_start):
    s = _dot_nt(qb, kb) * (HEAD_DIM ** -0.5) + bias + before_start
    e = jnp.exp(s - jnp.max(s, axis=-1, keepdims=True))
    return e * (1.0 / jnp.sum(e, axis=-1, keepdims=True))


def _att_specs():
    q_spec = pl.BlockSpec((ATT_QB, HEAD_DIM), lambda h, b: (b, h))
    k_specs = [pl.BlockSpec((ATT_QB, HEAD_DIM), lambda h, b, j=j: (b + j, HEADS + h)) for j in range(3)]
    v_specs = [pl.BlockSpec((ATT_QB, HEAD_DIM), lambda h, b, j=j: (b + j, 2 * HEADS + h)) for j in range(3)]
    w_spec = pl.BlockSpec((1, HEAD_DIM), lambda h, b: (0, 0))
    smem = pl.BlockSpec(memory_space=pltpu.SMEM)
    return q_spec, k_specs, v_specs, w_spec, smem


BIAS_SPEC = pl.BlockSpec((1, ATT_QB, ATT_KW), lambda h, b: (h, 0, 0))


def _expand_rel_bias(rel):
    def body(rel_ref, bias_ref):
        h = pl.program_id(0)
        idx = _att_rel_index()

        def fill(r, acc):
            return jnp.where(idx == r, rel_ref[h, r], acc)

        table = lax.fori_loop(0, N_REL, fill, jnp.zeros((ATT_QB, ATT_KW), F32))
        bias_ref[0] = jnp.where(_att_in_band(), table, NEG_INF)

    return pl.pallas_call(
        body, name="rel_bias_expand",
        out_shape=jax.ShapeDtypeStruct((HEADS, ATT_QB, ATT_KW), F32), grid=(HEADS,),
        in_specs=[pl.BlockSpec(memory_space=pltpu.SMEM)],
        out_specs=pl.BlockSpec((1, ATT_QB, ATT_KW), lambda h: (h, 0, 0)),
        compiler_params=_params(("parallel",)),
    )(rel)


def _attention(pb, pbp, qw, kw, bias):
    t_rows = pb.shape[0]
    q_spec, k_specs, v_specs, w_spec, _ = _att_specs()

    def body(q_ref, k0, k1, k2, v0, v1, v2, qw_ref, kw_ref, bias_ref, o_ref):
        b = pl.program_id(1)
        kwin = jnp.concatenate([k0[...], k1[...], k2[...]], axis=0)
        vwin = jnp.concatenate([v0[...], v1[...], v2[...]], axis=0)
        q = _rms(q_ref[...], qw_ref[...])
        k = _rms(kwin, kw_ref[...])
        p = _att_probs(_bf(q), _bf(k), bias_ref[0], _att_valid(b))
        o_ref[...] = _dot(_bf(p), _bf(vwin)).astype(o_ref.dtype)

    return pl.pallas_call(
        body, name="band_attention_fwd",
        out_shape=jax.ShapeDtypeStruct((t_rows, HW), BF16),
        grid=(HEADS, t_rows // ATT_QB),
        in_specs=[q_spec] + k_specs + v_specs + [w_spec, w_spec, BIAS_SPEC],
        out_specs=pl.BlockSpec((ATT_QB, HEAD_DIM), lambda h, b: (b, h)),
        compiler_params=_params(("parallel", "arbitrary")),
    )(pb, pbp, pbp, pbp, pbp, pbp, pbp, qw, kw, bias)


def _attention_bwd(pb, pbp, qw, kw, bias, dyb):
    t_rows = pb.shape[0]
    nb = t_rows // ATT_QB
    q_spec, k_specs, v_specs, w_spec, smem = _att_specs()
    pad_rows = t_rows + ATT_PAD
    acc_spec = pl.BlockSpec((pad_rows, HEAD_DIM), lambda h, b: (0, h))

    def body(q_ref, k0, k1, k2, v0, v1, v2, qw_ref, kw_ref, bias_ref, do_ref,
             dq_ref, dk_ref, dv_ref, dqw_ref, dkw_ref, drel_ref, dbias_ref):
        h, b = pl.program_id(0), pl.program_id(1)

        @pl.when(b == 0)
        def _():
            dbias_ref[...] = jnp.zeros_like(dbias_ref)
            dk_ref[...] = jnp.zeros_like(dk_ref)
            dv_ref[...] = jnp.zeros_like(dv_ref)

        @pl.when((b == 0) & (h == 0))
        def _():
            dqw_ref[...] = jnp.zeros_like(dqw_ref)
            dkw_ref[...] = jnp.zeros_like(dkw_ref)

        kwin = jnp.concatenate([k0[...], k1[...], k2[...]], axis=0)
        vwin = jnp.concatenate([v0[...], v1[...], v2[...]], axis=0)
        scale = HEAD_DIM ** -0.5
        qw_, kw_ = qw_ref[...], kw_ref[...]
        q, qn, rq = _rms_parts(q_ref[...], qw_)
        k, kn, rk = _rms_parts(kwin, kw_)
        qb, kb, dob = _bf(q), _bf(k), _bf(do_ref[...])
        p = _att_probs(qb, kb, bias_ref[0], _att_valid(b))
        dp = _dot_nt(dob, _bf(vwin))
        ds = p * (dp - jnp.sum(p * dp, axis=-1, keepdims=True))
        dbias_ref[...] += ds
        ds = _bf(ds)
        dq, dqw = _rms_bwd(_dot(ds, kb) * scale, qn, rq, qw_)
        dk, dkw = _rms_bwd(_dot_tn(ds, qb) * scale, kn, rk, kw_)
        dq_ref[...] = dq.astype(dq_ref.dtype)
        win = pl.ds(pl.multiple_of(b * ATT_QB, ATT_QB), ATT_KW)
        dk_ref[win, :] += dk
        dv_ref[win, :] += _dot_tn(_bf(p), dob)
        dqw_ref[...] += dqw
        dkw_ref[...] += dkw

        @pl.when(b == nb - 1)
        def _():
            idx = _att_rel_index()
            tot = dbias_ref[...]

            def reduce(r, carry):
                drel_ref[h, r] = jnp.sum(jnp.where(idx == r, tot, 0.0))
                return carry

            lax.fori_loop(0, N_REL, reduce, 0)

    return pl.pallas_call(
        body, name="band_attention_bwd",
        out_shape=[jax.ShapeDtypeStruct((t_rows, HW), BF16),
                   jax.ShapeDtypeStruct((pad_rows, HW), F32), jax.ShapeDtypeStruct((pad_rows, HW), F32),
                   jax.ShapeDtypeStruct((1, HEAD_DIM), F32), jax.ShapeDtypeStruct((1, HEAD_DIM), F32),
                   jax.ShapeDtypeStruct((HEADS, N_REL), F32)],
        grid=(HEADS, nb),
        in_specs=[q_spec] + k_specs + v_specs + [w_spec, w_spec, BIAS_SPEC, q_spec],
        out_specs=[q_spec, acc_spec, acc_spec, w_spec, w_spec, smem],
        scratch_shapes=[pltpu.VMEM((ATT_QB, ATT_KW), F32)],
        compiler_params=_params(("arbitrary", "arbitrary")),
    )(pb, pbp, pbp, pbp, pbp, pbp, pbp, qw, kw, bias, dyb)


def _me():
    return lax.axis_index("x"), lax.axis_index("y"), lax.axis_index("c")


def _index(x, y, c):
    return 4 * x + 2 * y + c


HBM_SPEC = pl.BlockSpec(memory_space=pl.ANY)


def _block(ref, kind, d, r, c):
    if kind == "rows":
        return ref.at[pl.ds(d * r, r), :]
    if kind == "win":
        return ref.at[:, pl.ds(d * WIN_STEP, c)]
    return ref.at[:, pl.ds(d * c, c)]


def _all_gather(shards, kinds, n_gather):
    n = len(shards)

    def body(*refs):
        x_refs, out_refs = refs[:n], refs[n:2 * n]
        send_sems, recv_sems, local_sems = refs[2 * n:]
        x, y, c = _me()
        me, sibling = (x, y, c), (x, y, 1 - c)
        chips = [(1 - x, y), (x, 1 - y), (1 - x, 1 - y)]

        def copy(i, k, blk, to, src=None):
            r_, c_ = shards[i].shape
            dst = _block(out_refs[i], kinds[i], _index(*blk), r_, c_)
            return pltpu.make_async_remote_copy(
                src_ref=dst if src is None else src, dst_ref=dst,
                send_sem=send_sems.at[i, k], recv_sem=recv_sems.at[i, k], device_id=to, device_id_type=MESH)

        sends, local = [], []
        for i in range(n):
            r_, c_ = shards[i].shape
            mine = pltpu.make_async_copy(x_refs[i], _block(out_refs[i], kinds[i], _index(*me), r_, c_),
                                         local_sems.at[i])
            mine.start()
            local.append(mine)
            if i >= n_gather:
                continue
            first = [copy(i, 0, me, sibling, src=x_refs[i])]
            first += [copy(i, 1 + j, me, (*chip, c), src=x_refs[i]) for j, chip in enumerate(chips)]
            for cp in first:
                cp.start()
            sends += first
        for i in range(n_gather):
            for j, chip in enumerate(chips):
                copy(i, 1 + j, (*chip, c), me).wait_recv()
                passed = copy(i, 4 + j, (*chip, c), sibling)
                passed.start()
                sends.append(passed)
        for i in range(n_gather):
            copy(i, 0, sibling, me).wait_recv()
            for j, chip in enumerate(chips):
                copy(i, 4 + j, (*chip, 1 - c), me).wait_recv()
        for cp in sends:
            cp.wait_send()
        for cp in local:
            cp.wait()

    def full_shape(s, kind):
        r_, c_ = s.shape
        return (N_DEV * r_, c_) if kind == "rows" else (r_, N_DEV * c_)

    return pl.pallas_call(
        body, name="weights_all_gather",
        out_shape=[jax.ShapeDtypeStruct(full_shape(s, k), s.dtype) for s, k in zip(shards, kinds)],
        in_specs=[HBM_SPEC] * n, out_specs=[HBM_SPEC] * n,
        scratch_shapes=[pltpu.SemaphoreType.DMA((n_gather, 7)), pltpu.SemaphoreType.DMA((n_gather, 7)),
                        pltpu.SemaphoreType.DMA((n,))],
        compiler_params=pltpu.CompilerParams(has_side_effects=True),
    )(*shards)


SEM_SPEC = pl.BlockSpec(memory_space=pltpu.SEMAPHORE)
HBM_ONLY = pl.BlockSpec(memory_space=pltpu.HBM)
DATAFLOW = pltpu.SideEffectType.DATAFLOW_SIDE_EFFECTING


def _peers():
    x, y, c = _me()
    return [(x ^ (k >> 2), y ^ ((k >> 1) & 1), c ^ (k & 1)) for k in range(1, N_DEV)]


def _gather_copies(shapes, kinds):
    def make(src_refs, land_refs, send_sems, recv_sems):
        mine = _index(*_me())
        return [pltpu.make_async_remote_copy(
            src_ref=src_refs[i], dst_ref=_block(land_refs[i], kind, mine, r, c),
            send_sem=send_sems.at[7 * i + k], recv_sem=recv_sems.at[7 * i + k], device_id=peer, device_id_type=MESH)
            for i, ((r, c), kind) in enumerate(zip(shapes, kinds)) for k, peer in enumerate(_peers())]

    return make


def _exchange_copies(shapes, kinds):
    def make(src_refs, land_refs, send_sems, recv_sems):
        mine = _index(*_me())
        return [pltpu.make_async_remote_copy(
            src_ref=_block(src_refs[i], kind, _index(*peer), r, c), dst_ref=land_refs[i].at[mine],
            send_sem=send_sems.at[7 * i + k], recv_sem=recv_sems.at[7 * i + k], device_id=peer, device_id_type=MESH)
            for i, ((r, c), kind) in enumerate(zip(shapes, kinds)) for k, peer in enumerate(_peers())]

    return make


def _place_block(shard, kind, name):
    r, c = shard.shape
    tile = _row_tile(r, c)
    nt = r // tile
    full = (N_DEV * r, c) if kind == "rows" else (r, N_DEV * c)

    def body(me_ref, x_ref, out_ref):
        out_ref[...] = x_ref[...]

    if kind == "rows":
        out_spec = pl.BlockSpec((tile, c), lambda i, me: (me[0] * nt + i, 0))
    else:
        out_spec = pl.BlockSpec((tile, c), lambda i, me: (i, me[0]))
    return pl.pallas_call(
        body, name=name, out_shape=jax.ShapeDtypeStruct(full, shard.dtype),
        grid_spec=pltpu.PrefetchScalarGridSpec(
            num_scalar_prefetch=1, grid=(nt,),
            in_specs=[pl.BlockSpec((tile, c), lambda i, me: (i, 0))], out_specs=out_spec),
        compiler_params=_params(("arbitrary",)),
    )(_my_index_operand(), shard)


def _split_start(srcs, lands, make, name):
    n = len(srcs)

    def body(*refs):
        send_sems, recv_sems = refs[2 * n], refs[2 * n + 1]
        for cp in make(refs[:n], refs[n:2 * n], send_sems, recv_sems):
            cp.start()
        refs[-1][...] = jnp.zeros_like(refs[-1])

    arrays = list(srcs) + list(lands)
    out = pl.pallas_call(
        body, name=name,
        out_shape=(pltpu.SemaphoreType.DMA((7 * n,)), pltpu.SemaphoreType.DMA((7 * n,)),
                   *[pltpu.HBM(a.shape, a.dtype) for a in arrays], jax.ShapeDtypeStruct((SUBLANES, LANES), F32)),
        in_specs=[HBM_ONLY] * (2 * n),
        out_specs=(SEM_SPEC, SEM_SPEC, *[HBM_ONLY] * (2 * n), pl.BlockSpec(memory_space=pltpu.VMEM)),
        input_output_aliases={i: 2 + i for i in range(2 * n)},
        compiler_params=pltpu.CompilerParams(has_side_effects=DATAFLOW),
    )(*[pltpu.with_memory_space_constraint(a, pltpu.HBM) for a in arrays])
    return out[0], out[1], list(out[2:2 + n]), list(out[2 + n:2 + 2 * n]), out[-1]


def _split_wait(send_sems, recv_sems, srcs, lands, after, make, name):
    n = len(srcs)

    def body(*refs):
        for cp in make(refs[:n], refs[n:2 * n], refs[2 * n], refs[2 * n + 1]):
            cp.wait_send()
            cp.wait_recv()

    arrays = list(srcs) + list(lands)
    out = pl.pallas_call(
        body, name=name,
        out_shape=tuple(pltpu.HBM(a.shape, a.dtype) for a in arrays),
        in_specs=[HBM_ONLY] * (2 * n) + [SEM_SPEC, SEM_SPEC, pl.BlockSpec(memory_space=pl.ANY)],
        out_specs=tuple([HBM_ONLY] * (2 * n)),
        input_output_aliases={i: i for i in range(2 * n)},
        compiler_params=pltpu.CompilerParams(has_side_effects=DATAFLOW),
    )(*arrays, send_sems, recv_sems, after)
    return list(out[:n]), list(out[n:])


def _all_reduce_small(vals, name):
    rows, width = vals.shape

    def body(x_ref, out_ref, buf_ref, send_sems, recv_sems):
        x, y, c = _me()
        mine = _index(x, y, c)
        buf_ref[mine] = x_ref[...]
        copies = []
        for k in range(1, N_DEV):
            px, py, pc = x ^ (k >> 2), y ^ ((k >> 1) & 1), c ^ (k & 1)
            copies.append(pltpu.make_async_remote_copy(
                src_ref=x_ref, dst_ref=buf_ref.at[mine],
                send_sem=send_sems.at[k - 1], recv_sem=recv_sems.at[k - 1],
                device_id=(px, py, pc), device_id_type=MESH))
        for cp in copies:
            cp.start()
        for cp in copies:
            cp.wait()
        acc = buf_ref[0]
        for j in range(1, N_DEV):
            acc = acc + buf_ref[j]
        out_ref[...] = acc

    vmem = pl.BlockSpec(memory_space=pltpu.VMEM)
    return pl.pallas_call(
        body, name=name,
        out_shape=jax.ShapeDtypeStruct(vals.shape, F32),
        in_specs=[vmem], out_specs=vmem,
        scratch_shapes=[pltpu.VMEM((N_DEV, rows, width), F32),
                        pltpu.SemaphoreType.DMA((7,)), pltpu.SemaphoreType.DMA((7,))],
        compiler_params=pltpu.CompilerParams(has_side_effects=True),
    )(vals)


def _adamw_math(w, g, m, v):
    m = ADAM_B1 * m + (1.0 - ADAM_B1) * g
    v = ADAM_B2 * v + (1.0 - ADAM_B2) * (g * g)
    m_hat = m / (1.0 - ADAM_B1 ** ADAM_STEP)
    v_hat = v / (1.0 - ADAM_B2 ** ADAM_STEP)
    delta = -ADAM_LR * (m_hat / (jnp.sqrt(v_hat) + ADAM_EPS) + ADAM_WD * w)
    return delta, m, v


ROW_TILE_ELEMS = 384 * 1024


def _row_tile(rows, width):
    best = SUBLANES
    for t in range(SUBLANES, rows + 1, SUBLANES):
        if rows % t == 0 and t * width <= ROW_TILE_ELEMS:
            best = t
    return best


def _sum_received(r_ref, own, me):
    g = None
    for j in range(N_DEV):
        term = jnp.where(me == j, own, r_ref[j].astype(F32))
        g = term if g is None else g + term
    return g


def _my_index_operand():
    return _index(*_me()).astype(jnp.int32).reshape(1)


def _adamw_recv(recv, grad, kind, w, m, v, name):
    _, rows, width = recv.shape
    tile = _row_tile(rows, width)
    nt = rows // tile

    def body(me_ref, r_ref, own_ref, w_ref, m_ref, v_ref, g_out, d_out, m_out, v_out):
        g = _sum_received(r_ref, own_ref[...].astype(F32), me_ref[0])
        d, mn, vn = _adamw_math(w_ref[...], g, m_ref[...], v_ref[...])
        g_out[...] = g
        d_out[...] = d
        m_out[...] = mn
        v_out[...] = vn

    if kind == "rows":
        own_spec = pl.BlockSpec((tile, width), lambda i, me: (me[0] * nt + i, 0))
    else:
        own_spec = pl.BlockSpec((tile, width), lambda i, me: (i, me[0]))
    spec = pl.BlockSpec((tile, width), lambda i, me: (i, 0))
    shape = jax.ShapeDtypeStruct((rows, width), F32)
    return pl.pallas_call(
        body, name=name, out_shape=[shape] * 4,
        grid_spec=pltpu.PrefetchScalarGridSpec(
            num_scalar_prefetch=1, grid=(nt,),
            in_specs=[pl.BlockSpec((N_DEV, tile, width), lambda i, me: (0, i, 0)), own_spec, spec, spec, spec],
            out_specs=[spec] * 4),
        compiler_params=_params(("parallel",)),
    )(_my_index_operand(), recv, grad, w, m, v)


WIN_STEP = 1408
WIN_W = 1536
IN_SHARD = IN_COLS // N_DEV
IN_PADDED = WIN_STEP * (N_DEV - 1) + WIN_W


def _roll_w_in(shard_padded):
    rows = shard_padded.shape[0]
    tile = _row_tile(rows, WIN_W)

    def body(x_ref, main_ref, edge_ref):
        win = pltpu.roll(x_ref[...], 2 * _index(*_me()), 1).astype(BF16)
        main_ref[...] = win[:, :WIN_STEP]
        edge_ref[...] = win[:, WIN_STEP:]

    return pl.pallas_call(
        body, name="w_in_window",
        out_shape=[jax.ShapeDtypeStruct((rows, WIN_STEP), BF16), jax.ShapeDtypeStruct((rows, WIN_W - WIN_STEP), BF16)],
        grid=(rows // tile,),
        in_specs=[pl.BlockSpec((tile, WIN_W), lambda i: (i, 0))],
        out_specs=[pl.BlockSpec((tile, WIN_STEP), lambda i: (i, 0)),
                   pl.BlockSpec((tile, WIN_W - WIN_STEP), lambda i: (i, 0))],
        compiler_params=_params(("parallel",)),
    )(shard_padded)


def _sum_w_in_windows(recv, grad):
    _, rows, width = recv.shape
    tile = _row_tile(rows, width)

    def body(me_ref, r_ref, g_ref, g_out, own_ref, sem):
        me = me_ref[0]
        rows_i = pl.ds(pl.multiple_of(pl.program_id(0) * tile, tile), tile)
        own = pltpu.make_async_copy(g_ref.at[rows_i, pl.ds(pl.multiple_of(me * WIN_STEP, LANES), width)], own_ref, sem)
        own.start()
        own.wait()
        g_out[...] = pltpu.roll(_sum_received(r_ref, own_ref[...].astype(F32), me), width - 2 * me, 1)

    return pl.pallas_call(
        body, name="w_in_grad_sum", out_shape=jax.ShapeDtypeStruct((rows, width), F32),
        grid_spec=pltpu.PrefetchScalarGridSpec(
            num_scalar_prefetch=1, grid=(rows // tile,),
            in_specs=[pl.BlockSpec((N_DEV, tile, width), lambda i, me: (0, i, 0)), HBM_SPEC],
            out_specs=pl.BlockSpec((tile, width), lambda i, me: (i, 0)),
            scratch_shapes=[pltpu.VMEM((tile, width), BF16), pltpu.SemaphoreType.DMA]),
        compiler_params=_params(("arbitrary",)),
    )(_my_index_operand(), recv, grad)


def _adamw_small(w, g, m, v, name):
    def fn(i, n, w_, g_, m_, v_):
        return _adamw_math(w_, g_, m_, v_)

    r, c = w.shape
    return _rows(fn, [(w, "t"), (g, "t"), (m, "t"), (v, "t")], [], [(c, F32)] * 3, [], _row_tile(r, c), name)


def _norm_fwd(x, w, name):
    return _rows(lambda i, n, x_, w_: (_rms(x_, w_[...]),), [(x, "t")], [w], [(D_MODEL, BF16)], [], 512, name)[0]


def _residual_norm_fwd(x, y, scale, w, name):
    def fn(i, n, x_, y_, w_):
        xn = x_ + scale * y_
        return xn, _rms(xn, w_[...])

    return _rows(fn, [(x, "t"), (y, "t")], [w], [(D_MODEL, F32), (D_MODEL, BF16)], [], 512, name)


def _residual_norm_bwd(x, w, dhs, dres, scale, name):
    nh = len(dhs)

    def fn(i, n, x_, dres_, *rest):
        dh = rest[0]
        for extra in rest[1:nh]:
            dh = dh + extra
        _, vjp = jax.vjp(_rms, x_, rest[nh][...])
        dx, dw = vjp(dh)
        dx = dx + dres_
        return dx, scale * dx, dw

    return _rows(fn, [(x, "t"), (dres, "t")] + [(d, "t") for d in dhs], [w],
                 [(D_MODEL, F32), (D_MODEL, BF16)], [(1, D_MODEL)], 256, name)


def _ffn_fwd(h, w_gu, get_w_down, tag):
    gu = _matmul(h, w_gu, "nn", BF16, tag + "_gu")
    act = _rows(lambda i, n, gu_: (_swiglu(gu_),), [(gu, "t")], [], [(D_FF, BF16)], [], 128, tag + "_swiglu")[0]
    y = _matmul(act, get_w_down(act), "nn", F32, tag + "_down")
    return gu, act, y


def _ffn_bwd(h, gu, act, dy, w_gu, w_down, tag, comm, more=None):
    dact = _matmul(dy, w_down, "nt", BF16, tag + "_dact")

    def fn(i, n, gu_, dact_):
        _, vjp = jax.vjp(_swiglu, gu_)
        return vjp(dact_)

    dgu = _rows(fn, [(gu, "t"), (dact, "t")], [], [(2 * D_FF, BF16)], [], 128, tag + "_swiglu_bwd")[0]
    sent = comm.send(tag + "_gu", {tag + "_w_gu": _matmul(h, dgu, "tn", BF16, tag + "_d_w_gu")})
    sent = sent + comm.send(tag + "_down", {tag + "_w_down": _matmul(act, dy + sent.astype(BF16), "tn", BF16,
                                                                    tag + "_d_w_down"), **(more or {})})
    dh = _matmul(dgu, w_gu, "nt", F32, tag + "_dh")
    return dh, sent


def _expanders():
    e_g = np.zeros((LANES, HW), np.float32)
    e_b = np.zeros((LANES, HW), np.float32)
    for h in range(HEADS):
        e_g[h, h * HEAD_DIM:(h + 1) * HEAD_DIM] = 1.0
        e_b[HEADS + h, h * HEAD_DIM:(h + 1) * HEAD_DIM] = 1.0
    return jnp.asarray(e_g), jnp.asarray(e_b)


def _pad_lanes(v):
    return jnp.pad(v, ((0, 0), (0, LANES - v.shape[1])))


class _LocalWeights:
    def __init__(self, big):
        self.big, self.sent = big, {}

    def arrive(self, group, after):
        return self.big

    def send(self, group, grads):
        self.sent.update(grads)
        return jnp.zeros((), F32)


def _local_step(x, p, tgt, small, comm):
    e_g, e_b = _expanders()
    alog, dtb = _pad_lanes(small["a_log"]), _pad_lanes(small["dt_bias"])
    conv_w = jnp.pad(small["conv_w"], ((0, SUBLANES - CONV_K), (0, 0)))
    rel = _expand_rel_bias(small["rel_bias"])

    h1 = _norm_fwd(x, small["ffn1_norm"], "ffn1_norm")
    big = dict(comm.arrive("ffn1", h1))
    if "_token" in big:
        h1 = h1 + big.pop("_token").astype(BF16)

    def ffn1_w_down(act):
        big.update(comm.arrive("ffn1_down", act))
        return big["ffn1_w_down"]

    gu1, act1, y1 = _ffn_fwd(h1, big["ffn1_w_gu"], ffn1_w_down, "ffn1")
    x1, h2 = _residual_norm_fwd(x, y1, 0.5, small["mix_norm"], "mix_norm")

    big = {**big, **comm.arrive("mixer", h2)}
    w_in = big["w_in"]
    w_qz = w_in[:, :IN_QZ]
    w_ab = jnp.pad(w_in[:, IN_AB0:IN_QKVB0], ((0, 0), (0, LANES - 2 * HEADS)))
    w_qkvb = w_in[:, IN_QKVB0:IN_GG0]
    w_gg = w_in[:, IN_GG0:IN_COLS]
    qz = _matmul(h2, w_qz, "nn", F32, "in_qz")
    ab = _matmul(h2, w_ab, "nn", F32, "in_ab")
    pb = _matmul(h2, w_qkvb, "nn", F32, "in_qkvb")
    gg = _matmul(h2, w_gg, "nn", BF16, "in_gates")
    pa, z = qz[:, :3 * HW], qz[:, 3 * HW:]

    def prep(i, n, pa_, prev_, ab_, cw_, alog_, dtb_, eg_, eb_):
        q, k, v = _gdn_post(_conv(pa_, prev_, cw_, i))
        g_b, beta_b = _gdn_gates(ab_, alog_[...], dtb_[...], eg_[...], eb_[...])
        return q, k, v, g_b, beta_b

    qn, kn, vv, g_b, beta_b = _rows(prep, [(pa, "t"), (pa, "p"), (ab, "t")], [conv_w, alog, dtb, e_g, e_b],
                                    [(HW, F32)] * 5, [], 256, "gdn_prep")
    u, w, aqk, qd, kt, tl = _gdn_intra(qn, kn, vv, g_b, beta_b)
    o, states = _gdn_scan(u, w, aqk, qd, kt, tl)
    ya = _rows(lambda i, n, o_, z_, w_: (_gated_norm(o_, z_, w_[...]),), [(o, "t"), (z, "t")], [small["gdn_norm"]],
               [(HW, BF16)], [], 512, "gdn_gated_norm")[0]

    pbp = jnp.pad(pb, ((ATT_PAD, 0), (0, 0)))
    yb = _attention(pb, pbp, small["q_norm"], small["k_norm"], rel)

    big = {**big, **comm.arrive("branches", yb)}
    ta = _matmul(ya, big["w_branch_a"], "nn", BF16, "branch_a")
    tb = _matmul(yb, big["w_branch_b"], "nn", BF16, "branch_b")
    mixed = _rows(lambda i, n, gg_, ta_, tb_: (_mix(gg_, ta_, tb_),), [(gg, "t"), (ta, "t"), (tb, "t")], [],
                  [(D_MODEL, BF16)], [], 256, "mix")[0]
    m_out = _matmul(mixed, big["w_out"], "nn", F32, "w_out")
    x2, h3 = _residual_norm_fwd(x1, m_out, 1.0, small["ffn2_norm"], "ffn2_norm")
    big = {**big, **comm.arrive("tail", h3)}
    gu2, act2, y2 = _ffn_fwd(h3, big["ffn2_w_gu"], lambda act: big["ffn2_w_down"], "ffn2")
    x3, h4 = _residual_norm_fwd(x2, y2, 0.5, small["ple_norm"], "ple_norm")
    gp = _matmul(h4, big["ple_gate"], "nn", BF16, "ple_gate")
    pp = _matmul(p, big["ple_proj"], "nn", BF16, "ple_proj")

    def head(i, n, x3_, gp_, pp_, tgt_):
        sg = _sigmoid(gp_)
        err = x3_ + sg * pp_ - tgt_
        dx4 = err * (1.0 / D_MODEL)
        sq = _colsum(err * err)
        part = sq[:, :LANES]
        for j in range(1, D_MODEL // LANES):
            part = part + sq[:, j * LANES:(j + 1) * LANES]
        return dx4, dx4 * pp_ * sg * (1.0 - sg), dx4 * sg, (0.5 / D_MODEL) * part

    dx4, dgp, dpp, loss_lanes = _rows(head, [(x3, "t"), (gp, "t"), (pp, "t"), (tgt, "t")], [],
                                      [(D_MODEL, F32), (D_MODEL, BF16), (D_MODEL, BF16)], [(1, LANES)], 256,
                                      "ple_loss_head")
    loss = jnp.sum(loss_lanes)

    gbig, gsmall = {}, {}
    gbig["ple_proj"] = _matmul(p, dpp, "tn", BF16, "d_ple_proj")
    gbig["ple_gate"] = _matmul(h4, dgp, "tn", BF16, "d_ple_gate")
    dh4 = _matmul(dgp, big["ple_gate"], "nt", F32, "ple_gate_dh")
    dx3, dy2, gsmall["ple_norm"] = _residual_norm_bwd(x3, small["ple_norm"], [dh4], dx4, 0.5, "ple_norm_bwd")

    dh3, sent = _ffn_bwd(h3, gu2, act2, dy2, big["ffn2_w_gu"], big["ffn2_w_down"], "ffn2", comm,
                         {n: gbig[n] for n in ("ple_proj", "ple_gate")})
    dx2, dx2b, gsmall["ffn2_norm"] = _residual_norm_bwd(x2, small["ffn2_norm"] + sent, [dh3], dx3, 1.0,
                                                        "ffn2_norm_bwd")

    gbig["w_out"] = _matmul(mixed, dx2b, "tn", BF16, "d_w_out")
    dmixed = _matmul(dx2b, big["w_out"], "nt", BF16, "w_out_dx")

    def mix_bwd(i, n, gg_, ta_, tb_, dm_):
        _, vjp = jax.vjp(_mix, gg_, ta_, tb_)
        return vjp(dm_)

    dgg, dta, dtb_ = _rows(mix_bwd, [(gg, "t"), (ta, "t"), (tb, "t"), (dmixed, "t")], [],
                           [(2 * D_MODEL, BF16), (D_MODEL, BF16), (D_MODEL, BF16)], [], 256, "mix_bwd")
    gbig["w_branch_a"] = _matmul(ya, dta, "tn", BF16, "d_branch_a")
    gbig["w_branch_b"] = _matmul(yb, dtb_, "tn", BF16, "d_branch_b")
    dya = _matmul(dta, big["w_branch_a"], "nt", F32, "branch_a_dx")
    dyb = _matmul(dtb_, big["w_branch_b"], "nt", F32, "branch_b_dx")

    dq_b, dk_b, dv_b, gsmall["q_norm"], gsmall["k_norm"], gsmall["rel_bias"] = _attention_bwd(
        pb, pbp, small["q_norm"], small["k_norm"], rel, dyb)
    dpb = jnp.concatenate([dq_b, dk_b[ATT_PAD:].astype(BF16), dv_b[ATT_PAD:].astype(BF16)], axis=1)

    def gated_bwd(i, n, o_, z_, dya_, w_):
        _, vjp = jax.vjp(_gated_norm, o_, z_, w_[...])
        return vjp(dya_)

    do, dz, gsmall["gdn_norm"] = _rows(gated_bwd, [(o, "t"), (z, "t"), (dya, "t")], [small["gdn_norm"]],
                                       [(HW, F32), (HW, BF16)], [(1, HEAD_DIM)], 256, "gdn_gated_norm_bwd")
    du, dw, da, dqd, dkt, dtl = _gdn_scan_bwd(do, u, w, aqk, qd, kt, tl, states)
    dqn, dkn, dvv, dg_b, dbeta_b = _gdn_intra_bwd(qn, kn, vv, g_b, beta_b, du, dw, da, dqd, dkt, dtl)

    def prep_bwd(i, n, pa_, prev_, ab_, dq_, dk_, dv_, dg_, db_, cw_, alog_, dtb_, eg_, eb_):
        _, vjp = jax.vjp(_gdn_post, _conv(pa_, prev_, cw_, i))
        (dy,) = vjp((dq_, dk_, dv_))
        e_g_, e_b_ = eg_[...], eb_[...]
        _, vjp_g = jax.vjp(lambda a, b, c: _gdn_gates(a, b, c, e_g_, e_b_), ab_, alog_[...], dtb_[...])
        dab, dalog, ddtb = vjp_g((dg_, db_))
        return dy, dab, dalog, ddtb

    dy_conv, dab, dalog, ddtb = _rows(
        prep_bwd, [(pa, "t"), (pa, "p"), (ab, "t"), (dqn, "t"), (dkn, "t"), (dvv, "t"), (dg_b, "t"), (dbeta_b, "t")],
        [conv_w, alog, dtb, e_g, e_b], [(3 * HW, F32), (LANES, BF16)], [(1, LANES), (1, LANES)], 256,
        "gdn_prep_bwd")
    gsmall["a_log"] = dalog[:, :HEADS]
    gsmall["dt_bias"] = ddtb[:, :HEADS]

    def conv_bwd(i, n, dy_, nxt_, pa_, prev_, cw_):
        dpa = dy_ * cw_[CONV_K - 1:CONV_K, :]
        row = lax.broadcasted_iota(jnp.int32, (SUBLANES, dy_.shape[1]), 0)
        dcw = jnp.where(row == CONV_K - 1, _colsum(dy_ * pa_), 0.0)
        for j in range(CONV_K - 1):
            s = CONV_K - 1 - j
            dpa = dpa + _shift_up(dy_, nxt_, s, i, n) * cw_[j:j + 1, :]
            dcw = dcw + jnp.where(row == j, _colsum(dy_ * _shift_down(pa_, prev_, s, i)), 0.0)
        return dpa, dcw

    dpa, dcw = _rows(conv_bwd, [(dy_conv, "t"), (dy_conv, "n"), (pa, "t"), (pa, "p")], [conv_w],
                     [(3 * HW, BF16)], [(SUBLANES, 3 * HW)], 256, "gdn_conv_bwd")
    gsmall["conv_w"] = dcw[:CONV_K]

    dqz = jnp.concatenate([dpa, dz], axis=1)
    d_w_qz = _matmul(h2, dqz, "tn", BF16, "d_in_qz")
    d_w_ab = _matmul(h2, dab, "tn", BF16, "d_in_ab")
    d_w_qkvb = _matmul(h2, dpb, "tn", BF16, "d_in_qkvb")
    d_w_gg = _matmul(h2, dgg, "tn", BF16, "d_in_gates")
    gbig["w_in"] = jnp.concatenate([d_w_qz, d_w_ab[:, :2 * HEADS], d_w_qkvb, d_w_gg,
                                    jnp.zeros((D_MODEL, IN_PADDED - IN_COLS), BF16)], axis=1)
    dh2 = [_matmul(dqz, w_qz, "nt", F32, "in_qz_dh"), _matmul(dab, w_ab, "nt", F32, "in_ab_dh"),
           _matmul(dpb, w_qkvb, "nt", F32, "in_qkvb_dh"), _matmul(dgg, w_gg, "nt", F32, "in_gates_dh")]
    sent = comm.send("mixer", {n: gbig[n] for n in ("w_out", "w_branch_b", "w_branch_a", "w_in")})
    dx1, dy1, gsmall["mix_norm"] = _residual_norm_bwd(x1, small["mix_norm"] + sent, dh2, dx2, 0.5, "mix_norm_bwd")

    dh1, sent = _ffn_bwd(h1, gu1, act1, dy1, big["ffn1_w_gu"], big["ffn1_w_down"], "ffn1", comm)
    grad_x, _, gsmall["ffn1_norm"] = _residual_norm_bwd(x, small["ffn1_norm"] + sent, [dh1], dx1, 1.0,
                                                        "ffn1_norm_bwd")
    return loss, grad_x, gsmall


GATHER_GROUPS = {"ffn1": ("ffn1_w_gu",),
                 "ffn1_down": ("ffn1_w_down",),
                 "mixer": ("w_in_main", "w_in_edge"),
                 "branches": ("w_branch_a", "w_branch_b", "w_out"),
                 "tail": ("ffn2_w_gu", "ffn2_w_down", "ple_gate", "ple_proj")}
SPLIT_GATHERS = ("ffn1_down", "mixer", "branches", "tail")


def _kind(name):
    return "cols" if name in COL_SHARDED or name.startswith("w_in_") else "rows"


def _merge_w_in(main, edges):
    edge_w = WIN_W - WIN_STEP
    w_in = jnp.pad(main, ((0, 0), (0, edge_w)))
    for d in range(N_DEV):
        at = WIN_STEP * (d + 1)
        w_in = w_in + jnp.pad(edges[:, d * edge_w:(d + 1) * edge_w], ((0, 0), (at, IN_PADDED - at - edge_w)))
    return w_in


class _Fsdp:
    def __init__(self, wts, first):
        self.wts, self.first_token = wts, first
        main, edge = _roll_w_in(jnp.pad(wts["w_in"], ((0, 0), (0, WIN_W - IN_SHARD))))
        self.shards = {n: wts[n].astype(BF16) for n in BIG if n not in ("w_in", "ffn1_w_gu")}
        self.shards.update(w_in_main=main, w_in_edge=edge)
        self.lands = {n: _place_block(self.shards[n], _kind(n), "own_" + n)
                      for group in SPLIT_GATHERS for n in GATHER_GROUPS[group]}
        self.flight, self.sent = {}, {}

    def _gather_first(self, after):
        token = self.first_token + after[0, 0].astype(F32) * 0.0
        me = _index(*_me())
        for n, land in self.lands.items():
            r, c = self.shards[n].shape
            at = (me * r, 0) if _kind(n) == "rows" else (0, me * c)
            token = token + lax.dynamic_slice(land, at, (1, 1))[0, 0].astype(F32) * 0.0
        shard = (self.wts["ffn1_w_gu"] + token).astype(BF16)
        self.shards["ffn1_w_gu"] = shard
        first = _all_gather([shard], [_kind("ffn1_w_gu")], 1)[0]
        token = first[0, 0].astype(F32) * 0.0
        for group in SPLIT_GATHERS:
            names = GATHER_GROUPS[group]
            srcs = [self.shards[n] for n in names]
            lands = [self.lands[n] for n in names]
            make = _gather_copies([s.shape for s in srcs], [_kind(n) for n in names])
            srcs[0] = srcs[0] + token.astype(BF16)
            send_sems, recv_sems, srcs, lands, tok = _split_start(srcs, lands, make, "gather_start_" + group)
            token = token + tok[0, 0]
            self.flight[group] = (send_sems, recv_sems, srcs, lands, make)
        return {"ffn1_w_gu": first, "_token": token}

    def arrive(self, group, after):
        if group == "ffn1":
            return self._gather_first(after)
        send_sems, recv_sems, srcs, lands, make = self.flight[group]
        _, lands = _split_wait(send_sems, recv_sems, srcs, lands, after, make, "gather_wait_" + group)
        full = dict(zip(GATHER_GROUPS[group], lands))
        if group == "mixer":
            full["w_in"] = _merge_w_in(full.pop("w_in_main"), full.pop("w_in_edge"))
        return full

    def send(self, group, grads):
        names = list(grads)
        kinds = ["win" if n == "w_in" else _kind(n) for n in names]
        shapes = [(D_MODEL, WIN_W) if n == "w_in" else self.shards[n].shape for n in names]
        srcs = [grads[n] for n in names]
        lands = [lax.empty((N_DEV,) + tuple(s), BF16) for s in shapes]
        make = _exchange_copies(shapes, kinds)
        send_sems, recv_sems, srcs, lands, tok = _split_start(srcs, lands, make, "grads_start_" + group)
        self.sent[group] = (names, kinds, send_sems, recv_sems, srcs, lands, make)
        return tok[0, 0]

    def received(self, group, after):
        names, kinds, send_sems, recv_sems, srcs, lands, make = self.sent[group]
        srcs, lands = _split_wait(send_sems, recv_sems, srcs, lands, after, make, "grads_wait_" + group)
        return {n: (k, g, r) for n, k, g, r in zip(names, kinds, srcs, lands)}


SMALL_ROWS = ("ffn1_norm", "mix_norm", "ffn2_norm", "ple_norm", "gdn_norm", "q_norm", "k_norm", "a_log", "dt_bias",
              "rel_bias", "conv_w")


def _pack_small(vals):
    rows = []
    for n in SMALL_ROWS:
        v = vals[n]
        if n == "rel_bias":
            v = jnp.pad(v, ((0, 0), (0, 2 * LANES - N_REL)))
        elif n in ("a_log", "dt_bias"):
            v = _pad_lanes(v)
        rows.append(v.reshape(-1, LANES))
    packed = jnp.concatenate(rows, axis=0)
    return jnp.pad(packed, ((0, -packed.shape[0] % SUBLANES), (0, 0)))


def _unpack_small(packed, shapes):
    out, off = {}, 0
    for n in SMALL_ROWS:
        shp = shapes[n]
        if n == "rel_bias":
            out[n] = packed[off:off + 2 * HEADS].reshape(HEADS, 2 * LANES)[:, :N_REL]
            off += 2 * HEADS
        elif n in ("a_log", "dt_bias"):
            out[n] = packed[off:off + 1, :HEADS]
            off += 1
        else:
            r = int(np.prod(shp)) // LANES
            out[n] = packed[off:off + r].reshape(shp)
            off += r
    return out


WEIGHTS = ("ffn1_norm", "ffn1_w_gu", "ffn1_w_down", "mix_norm", "w_in", "conv_w", "a_log", "dt_bias", "gdn_norm",
           "q_norm", "k_norm", "rel_bias", "w_branch_a", "w_branch_b", "w_out", "ffn2_norm", "ffn2_w_gu",
           "ffn2_w_down", "ple_norm", "ple_gate", "ple_proj")


def kernel(x, p, ffn1_norm, ffn1_w_gu, ffn1_w_down, mix_norm, w_in, conv_w, a_log, dt_bias, gdn_norm, q_norm, k_norm, rel_bias, w_branch_a, w_branch_b, w_out, ffn2_norm, ffn2_w_gu, ffn2_w_down, ple_norm, ple_gate, ple_proj, loss_target, m_ffn1_norm, m_ffn1_w_gu, m_ffn1_w_down, m_mix_norm, m_w_in, m_conv_w, m_a_log, m_dt_bias, m_gdn_norm, m_q_norm, m_k_norm, m_rel_bias, m_w_branch_a, m_w_branch_b, m_w_out, m_ffn2_norm, m_ffn2_w_gu, m_ffn2_w_down, m_ple_norm, m_ple_gate, m_ple_proj, v_ffn1_norm, v_ffn1_w_gu, v_ffn1_w_down, v_mix_norm, v_w_in, v_conv_w, v_a_log, v_dt_bias, v_gdn_norm, v_q_norm, v_k_norm, v_rel_bias, v_w_branch_a, v_w_branch_b, v_w_out, v_ffn2_norm, v_ffn2_w_gu, v_ffn2_w_down, v_ple_norm, v_ple_gate, v_ple_proj):
    args = dict(locals())
    def layer0(v):
        return v[0] if v.ndim == 3 else v

    wts = {n: layer0(args[n]) for n in WEIGHTS}
    mom = {n: layer0(args["m_" + n]) for n in WEIGHTS}
    var = {n: layer0(args["v_" + n]) for n in WEIGHTS}
    x2d, p2d, tgt = x[0], p[0, 0], loss_target[0]
    my_index = _index(*_me())

    small = {n: wts[n] for n in SMALL_ROWS if n != "conv_w"}
    conv_shard = wts["conv_w"]
    conv_cols = conv_shard.shape[1]
    conv_packed = jnp.zeros((SUBLANES, N_DEV * conv_cols), F32)
    conv_packed = lax.dynamic_update_slice(conv_packed, jnp.pad(conv_shard, ((0, SUBLANES - CONV_K), (0, 0))),
                                           (0, my_index * conv_cols))
    small["conv_w"] = _all_reduce_small(conv_packed.reshape(-1, LANES), "conv_w_gather").reshape(SUBLANES, -1)[:CONV_K]

    fsdp = _Fsdp(wts, small["conv_w"][0, 0] * 0.0)

    loss, grad_x, gsmall = _local_step(x2d, p2d, tgt, small, fsdp)
    loss = lax.psum(loss, ("x", "y", "c"))

    outs_big, after = {}, grad_x
    for group in list(fsdp.sent):
        for n, (kind, grad, recv) in fsdp.received(group, after).items():
            if n == "w_in":
                g_in = _sum_w_in_windows(recv, grad)[:, :IN_SHARD]
                outs_big[n] = [g_in] + list(_adamw_small(wts[n], g_in, mom[n], var[n], "adamw_w_in"))
            else:
                outs_big[n] = _adamw_recv(recv, grad, kind, wts[n], mom[n], var[n], "adamw_" + n)
            after = outs_big[n][1]

    small_shapes = {n: (small[n].shape if n != "conv_w" else (CONV_K, N_DEV * conv_cols)) for n in SMALL_ROWS}
    gsum = _unpack_small(_all_reduce_small(_pack_small(gsmall), "small_grads_all_reduce"), small_shapes)
    gsum["conv_w"] = lax.dynamic_slice(gsum["conv_w"], (0, my_index * conv_cols), (CONV_K, conv_cols))
    rep = [n for n in SMALL_ROWS if n != "conv_w"]
    rep_shapes = {n: small_shapes[n] for n in rep}

    def pack_rep(vals):
        return _pack_small({**{n: vals[n] for n in rep}, "conv_w": jnp.zeros((CONV_K, LANES), F32)})

    def unpack_rep(packed):
        return _unpack_small(packed, {**rep_shapes, "conv_w": (CONV_K, LANES)})

    outs_small = [unpack_rep(o) for o in _adamw_small(pack_rep(wts), pack_rep(gsum), pack_rep(mom), pack_rep(var),
                                                      "adamw_replicated")]
    pad8 = functools.partial(jnp.pad, pad_width=((0, SUBLANES - CONV_K), (0, 0)))
    outs_conv = [o[:CONV_K] for o in _adamw_small(pad8(conv_shard), pad8(gsum["conv_w"]), pad8(mom["conv_w"]),
                                                   pad8(var["conv_w"]), "adamw_conv")]

    def leaf(kind, n):
        if n in BIG:
            return outs_big[n][kind][None]
        if n == "conv_w":
            return (gsum["conv_w"] if kind == 0 else outs_conv[kind - 1])[None]
        return (gsum[n] if kind == 0 else outs_small[kind - 1][n]).reshape(args[n].shape)

    result = [loss, grad_x[None]]
    for kind in range(4):
        result += [leaf(kind, n) for n in WEIGHTS]
    return tuple(result)
```

```python
import functools

import numpy as np
import jax
import jax.numpy as jnp
from jax import lax
from jax.experimental import pallas as pl
from jax.experimental.pallas import tpu as pltpu

F32 = jnp.float32
BF16 = jnp.bfloat16
HIGHEST = lax.Precision.HIGHEST
MESH = pl.DeviceIdType.MESH

D_MODEL = 2048
D_FF = 5632
HEADS = 8
HEAD_DIM = 128
HW = HEADS * HEAD_DIM
CHUNK = 64
LEFT_CHUNKS = 8
MAX_REL = 128
N_REL = (CHUNK - 1) + MAX_REL + 1
CONV_K = 4
EPS = 1e-6
NEG_INF = -1e30
N_DEV = 8
LANES = 128
SUBLANES = 8
VMEM_LIMIT = 56 * 1024 * 1024

MATMUL_WHOLE_K = 2048

ATT_QB = 256
ATT_KW = ATT_QB + LEFT_CHUNKS * CHUNK
ATT_PAD = LEFT_CHUNKS * CHUNK
GDN_CB = 8
GDN_GROUP = 8

ADAM_LR = 0.001
ADAM_B1 = 0.9
ADAM_B2 = 0.999
ADAM_EPS = 1e-08
ADAM_WD = 0.01
ADAM_STEP = 10

IN_QZ = 3 * HW + HW
IN_AB0 = IN_QZ
IN_QKVB0 = IN_AB0 + 2 * HEADS
IN_GG0 = IN_QKVB0 + 3 * HW
IN_COLS = IN_GG0 + 2 * D_MODEL

BIG = ("ffn1_w_gu", "ffn1_w_down", "w_in", "w_branch_a", "w_branch_b", "w_out",
       "ffn2_w_gu", "ffn2_w_down", "ple_gate", "ple_proj")
COL_SHARDED = ("ffn1_w_gu", "w_in", "w_branch_a", "w_branch_b", "ffn2_w_gu", "ple_proj")


def _params(semantics=None, **kw):
    return pltpu.CompilerParams(dimension_semantics=semantics, vmem_limit_bytes=VMEM_LIMIT, **kw)


def _pick(n, cands):
    for c in cands:
        if n % c == 0:
            return c
    return n


def _matmul(a, b, mode, out_dtype, name):
    if mode == "nn":
        (m, k), (k2, n) = a.shape, b.shape
    elif mode == "nt":
        (m, k), (n, k2) = a.shape, b.shape
    else:
        (k, m), (k2, n) = a.shape, b.shape
    assert k == k2, (a.shape, b.shape, mode)
    tm = _pick(m, (1024, 512, 256, 128))
    tn = _pick(n, (1024, 512, 256, 128))
    tk = k if k <= MATMUL_WHOLE_K else _pick(k, (2816, 2048, 1536, 1024, 512, 256, 128))
    nk = k // tk
    if mode == "nn":
        a_spec = pl.BlockSpec((tm, tk), lambda i, j, kk: (i, kk))
        b_spec = pl.BlockSpec((tk, tn), lambda i, j, kk: (kk, j))
        dims = (((1,), (0,)), ((), ()))
    elif mode == "nt":
        a_spec = pl.BlockSpec((tm, tk), lambda i, j, kk: (i, kk))
        b_spec = pl.BlockSpec((tn, tk), lambda i, j, kk: (j, kk))
        dims = (((1,), (1,)), ((), ()))
    else:
        a_spec = pl.BlockSpec((tk, tm), lambda i, j, kk: (kk, i))
        b_spec = pl.BlockSpec((tk, tn), lambda i, j, kk: (kk, j))
        dims = (((0,), (0,)), ((), ()))

    def body(a_ref, b_ref, o_ref, *acc):
        prod = lax.dot_general(a_ref[...].astype(BF16), b_ref[...].astype(BF16), dims, preferred_element_type=F32)
        if nk == 1:
            o_ref[...] = prod.astype(o_ref.dtype)
            return
        acc_ref, kk = acc[0], pl.program_id(2)

        @pl.when(kk == 0)
        def _():
            acc_ref[...] = prod

        @pl.when((kk > 0) & (kk < nk - 1))
        def _():
            acc_ref[...] += prod

        @pl.when(kk == nk - 1)
        def _():
            o_ref[...] = (acc_ref[...] + prod).astype(o_ref.dtype)

    return pl.pallas_call(
        body, name=name,
        out_shape=jax.ShapeDtypeStruct((m, n), out_dtype),
        grid=(m // tm, n // tn, nk),
        in_specs=[a_spec, b_spec],
        out_specs=pl.BlockSpec((tm, tn), lambda i, j, kk: (i, j)),
        scratch_shapes=[pltpu.VMEM((tm, tn), F32)] if nk > 1 else [],
        compiler_params=_params(("parallel", "parallel", "arbitrary")),
    )(a, b)


def _rows(fn, row_ins, consts, row_outs, acc_outs, tile, name):
    t_rows = row_ins[0][0].shape[0]
    tile = min(tile, t_rows)
    assert t_rows % tile == 0 and tile % SUBLANES == 0
    n = t_rows // tile
    per = tile // SUBLANES
    last8 = t_rows // SUBLANES - 1
    in_specs = []
    for arr, kind in row_ins:
        c = arr.shape[1]
        if kind == "t":
            in_specs.append(pl.BlockSpec((tile, c), lambda i: (i, 0)))
        elif kind == "p":
            in_specs.append(pl.BlockSpec((SUBLANES, c), lambda i: (jnp.maximum(i * per - 1, 0), 0)))
        else:
            in_specs.append(pl.BlockSpec((SUBLANES, c), lambda i: (jnp.minimum((i + 1) * per, last8), 0)))
    for arr in consts:
        in_specs.append(pl.BlockSpec(arr.shape, lambda i, nd=arr.ndim: (0,) * nd))
    out_shape = [jax.ShapeDtypeStruct((t_rows, c), dt) for c, dt in row_outs]
    out_specs = [pl.BlockSpec((tile, c), lambda i: (i, 0)) for c, _ in row_outs]
    for shp in acc_outs:
        out_shape.append(jax.ShapeDtypeStruct(shp, F32))
        out_specs.append(pl.BlockSpec(shp, lambda i, nd=len(shp): (0,) * nd))
    n_in = len(row_ins) + len(consts)
    n_row_out = len(row_outs)

    def body(*refs):
        i = pl.program_id(0)
        vals = [r[...].astype(F32) for r in refs[:len(row_ins)]]
        res = fn(i, n, *vals, *refs[len(row_ins):n_in])
        outs = refs[n_in:]
        for r, v in zip(outs[:n_row_out], res[:n_row_out]):
            r[...] = v.astype(r.dtype)
        if acc_outs:
            @pl.when(i == 0)
            def _():
                for r in outs[n_row_out:]:
                    r[...] = jnp.zeros_like(r)

            for r, v in zip(outs[n_row_out:], res[n_row_out:]):
                r[...] += v

    res = pl.pallas_call(
        body, name=name, out_shape=out_shape, grid=(n,), in_specs=in_specs, out_specs=out_specs,
        compiler_params=_params(("arbitrary",) if acc_outs else ("parallel",)),
    )(*[a for a, _ in row_ins], *consts)
    return res


def _rms(x, w):
    return x * lax.rsqrt(jnp.mean(x * x, axis=-1, keepdims=True) + EPS) * w


def _l2n(x):
    return x * lax.rsqrt(jnp.sum(x * x, axis=-1, keepdims=True) + EPS)


def _sigmoid(x):
    return 1.0 / (1.0 + jnp.exp(-x))


def _silu(x):
    return x * _sigmoid(x)


def _softplus(x):
    return jnp.maximum(x, 0.0) + jnp.log(1.0 + jnp.exp(-jnp.abs(x)))


def _heads(fn, *xs):
    nh = xs[0].shape[1] // HEAD_DIM
    return jnp.concatenate(
        [fn(*[x[:, h * HEAD_DIM:(h + 1) * HEAD_DIM] for x in xs]) for h in range(nh)], axis=1)


def _colsum(x):
    return jnp.sum(x, axis=0, keepdims=True)


def _swiglu(gu):
    return _silu(gu[:, :D_FF]) * gu[:, D_FF:]


def _gated_norm(o, z, w):
    return _heads(lambda oh, zh: _rms(oh, w) * _silu(zh), o, z)


def _mix(gg, ta, tb):
    return _sigmoid(gg[:, :D_MODEL]) * ta + _sigmoid(gg[:, D_MODEL:]) * tb


def _gdn_post(y):
    a = _silu(y)
    q = _heads(lambda v: _l2n(v) * (HEAD_DIM ** -0.5), a[:, :HW])
    k = _heads(_l2n, a[:, HW:2 * HW])
    return q, k, a[:, 2 * HW:]


NN = (((1,), (0,)), ((), ()))
NT = (((1,), (1,)), ((), ()))
TN = (((0,), (0,)), ((), ()))


def _dg(a, b, dims):
    return lax.dot_general(a, b, dims, preferred_element_type=F32)


def _split2(x):
    hi = x.astype(BF16)
    return hi, (x - hi.astype(F32)).astype(BF16)


def _split3(x):
    hi = x.astype(BF16)
    r = x - hi.astype(F32)
    mid = r.astype(BF16)
    return hi, mid, (r - mid.astype(F32)).astype(BF16)


def _dg3(a, b, dims):
    ah, al = _split2(a)
    bh, bl = _split2(b)
    return _dg(ah, bh, dims) + (_dg(ah, bl, dims) + _dg(al, bh, dims))


BNN = (((2,), (1,)), ((0,), (0,)))
BNT = (((2,), (2,)), ((0,), (0,)))
BTN = (((1,), (1,)), ((0,), (0,)))


@jax.custom_vjp
def _mm3(a, b):
    return _dg3(a, b, BNN)


_mm3.defvjp(lambda a, b: (_dg3(a, b, BNN), (a, b)),
            lambda res, g: (_dg3(g, res[1], BNT), _dg3(res[0], g, BTN)))


def _xm(x, m, dims):
    mb = m.astype(BF16)
    parts = _split3(x)
    return _dg(parts[0], mb, dims) + (_dg(parts[1], mb, dims) + _dg(parts[2], mb, dims))


def _mx(m, x, dims):
    mb = m.astype(BF16)
    parts = _split3(x)
    return _dg(mb, parts[0], dims) + (_dg(mb, parts[1], dims) + _dg(mb, parts[2], dims))


@jax.custom_vjp
def _times_const(x, m):
    return _xm(x, m, NN)


_times_const.defvjp(lambda x, m: (_xm(x, m, NN), m),
                    lambda m, g: (_xm(g, m, NT), jnp.zeros_like(m)))


@jax.custom_vjp
def _const_times(m, x):
    return _mx(m, x, NN)


_const_times.defvjp(lambda m, x: (_mx(m, x, NN), m),
                    lambda m, g: (jnp.zeros_like(m), _mx(m, g, TN)))


@jax.custom_vjp
def _lane_mean_cols(x, avg):
    return _mx(avg, x, BNT)


_lane_mean_cols.defvjp(lambda x, avg: (_mx(avg, x, BNT), avg),
                       lambda avg, g: (_xm(g, avg, BTN), jnp.zeros_like(avg)))


def _gdn_gates(ab, alog, dtb, e_g, e_b):
    t = ab.shape[0]
    g = -jnp.exp(alog) * _softplus(ab + dtb)
    beta = _sigmoid(ab)
    ri = lax.broadcasted_iota(jnp.int32, (t, t), 0)
    ci = lax.broadcasted_iota(jnp.int32, (t, t), 1)
    shift = CHUNK.bit_length() - 1
    same = jnp.right_shift(ri, shift) == jnp.right_shift(ci, shift)
    tril = jnp.where(same & (ri >= ci), 1.0, 0.0).astype(F32)
    gc = _const_times(tril, g)
    return _times_const(gc, e_g), _times_const(beta, e_b)


def _shift_down(x, halo, s, i):
    if s == 0:
        return x
    halo = jnp.where(i == 0, 0.0, halo)
    xr = pltpu.roll(x, s, 0)
    hr = pltpu.roll(halo, s, 0)
    row = lax.broadcasted_iota(jnp.int32, (SUBLANES, x.shape[1]), 0)
    top = jnp.where(row < s, hr, xr[:SUBLANES])
    return jnp.concatenate([top, xr[SUBLANES:]], axis=0)


def _shift_up(x, halo, s, i, n):
    if s == 0:
        return x
    t = x.shape[0]
    halo = jnp.where(i == n - 1, 0.0, halo)
    xr = pltpu.roll(x, t - s, 0)
    hr = pltpu.roll(halo, SUBLANES - s, 0)
    row = lax.broadcasted_iota(jnp.int32, (SUBLANES, x.shape[1]), 0)
    bot = jnp.where(row >= SUBLANES - s, hr, xr[t - SUBLANES:])
    return jnp.concatenate([xr[:t - SUBLANES], bot], axis=0)


def _conv(pa, prev, cw_ref, i):
    y = pa * cw_ref[CONV_K - 1:CONV_K, :]
    for j in range(CONV_K - 1):
        y = y + _shift_down(pa, prev, CONV_K - 1 - j, i) * cw_ref[j:j + 1, :]
    return y


def _dot_nt(a, b, precision=None):
    return lax.dot_general(a, b, (((1,), (1,)), ((), ())), precision=precision, preferred_element_type=F32)


def _dot_tn(a, b, precision=None):
    return lax.dot_general(a, b, (((0,), (0,)), ((), ())), precision=precision, preferred_element_type=F32)


def _dot(a, b, precision=None):
    return jnp.dot(a, b, precision=precision, preferred_element_type=F32)


def _bf(x):
    return x.astype(BF16)


def _neumann_inverse(lmat):
    nb, c, _ = lmat.shape
    ri = lax.broadcasted_iota(jnp.int32, (nb, c, c), 1)
    ci = lax.broadcasted_iota(jnp.int32, (nb, c, c), 2)
    pw = -lmat
    inv = jnp.where(ri == ci, 1.0, 0.0).astype(F32) + pw
    for _ in range(5):
        pw = _mm3(pw, pw)
        inv = inv + _mm3(inv, pw)
    return inv


@jax.custom_vjp
def _unit_lower_inverse(lmat):
    return _neumann_inverse(lmat)


def _unit_lower_inverse_fwd(lmat):
    inv = _neumann_inverse(lmat)
    return inv, inv


def _unit_lower_inverse_bwd(inv, g):
    return (-_dg3(_dg3(inv, g, BTN), inv, BNT),)


_unit_lower_inverse.defvjp(_unit_lower_inverse_fwd, _unit_lower_inverse_bwd)


def _gdn_chunk(q, k, v, gc, bb):
    nb, c, _ = q.shape
    ri = lax.broadcasted_iota(jnp.int32, (nb, c, c), 1)
    ci = lax.broadcasted_iota(jnp.int32, (nb, c, c), 2)
    incl = ri >= ci
    strict = ri > ci
    g_row = gc[:, :, :c]
    g_col = _lane_mean_cols(gc, jnp.full((nb, c, LANES), 1.0 / LANES, F32))
    decay = jnp.where(incl, jnp.exp(jnp.where(incl, g_row - g_col, 0.0)), 0.0)
    kb = k * bb
    lmat = jnp.where(strict, _dg(_bf(kb), _bf(k), BNT) * decay, 0.0)
    inv = _unit_lower_inverse(lmat)
    egc = jnp.exp(gc)
    u = _mm3(inv, v * bb)
    w = _mm3(inv, kb * egc)
    aqk = _dg(_bf(q), _bf(k), BNT) * decay
    last = lax.broadcasted_iota(jnp.int32, (nb, c, LANES), 1) == c - 1
    tot = jnp.sum(jnp.where(last, gc, 0.0), axis=1, keepdims=True)
    k_tail = k * jnp.exp(tot - gc)
    tail = jnp.broadcast_to(jnp.exp(tot), (nb, SUBLANES, LANES))
    return u, w, aqk, q * egc, k_tail, tail


def _gdn_intra(qn, kn, vv, g_b, beta_b):
    t_rows = qn.shape[0]
    nc = t_rows // CHUNK
    cb = min(GDN_CB, nc)
    rows = cb * CHUNK
    col = pl.BlockSpec((rows, HEAD_DIM), lambda h, b: (b, h))

    def body(q_ref, k_ref, v_ref, g_ref, b_ref, u_ref, w_ref, a_ref, qd_ref, kt_ref, tl_ref):
        def group(gi, carry):
            r = pl.ds(pl.multiple_of(gi * (grp * CHUNK), grp * CHUNK), grp * CHUNK)
            ins = [ref[r, :].reshape(grp, CHUNK, HEAD_DIM) for ref in (q_ref, k_ref, v_ref, g_ref, b_ref)]
            u, w, aqk, qd, kt, tl = _gdn_chunk(*ins)
            for ref, val in ((u_ref, u), (w_ref, w), (qd_ref, qd), (kt_ref, kt)):
                ref[r, :] = val.reshape(grp * CHUNK, HEAD_DIM)
            a_ref[0, r, :] = aqk.reshape(grp * CHUNK, CHUNK)
            tl_ref[0, pl.ds(gi * grp, grp)] = tl
            return carry

        grp = min(GDN_GROUP, cb)
        lax.fori_loop(0, cb // grp, group, 0)

    full = jax.ShapeDtypeStruct((t_rows, HW), F32)
    return pl.pallas_call(
        body, name="gdn_intra_fwd",
        out_shape=[full, full, jax.ShapeDtypeStruct((HEADS, t_rows, CHUNK), F32), full, full,
                   jax.ShapeDtypeStruct((HEADS, nc, SUBLANES, LANES), F32)],
        grid=(HEADS, nc // cb),
        in_specs=[col] * 5,
        out_specs=[col, col, pl.BlockSpec((1, rows, CHUNK), lambda h, b: (h, b, 0)), col, col,
                   pl.BlockSpec((1, cb, SUBLANES, LANES), lambda h, b: (h, b, 0, 0))],
        compiler_params=_params(("parallel", "parallel")),
    )(qn, kn, vv, g_b, beta_b)


def _gdn_intra_bwd(qn, kn, vv, g_b, beta_b, du, dw, da, dqd, dkt, dtl):
    t_rows = qn.shape[0]
    nc = t_rows // CHUNK
    cb = min(GDN_CB, nc)
    rows = cb * CHUNK
    col = pl.BlockSpec((rows, HEAD_DIM), lambda h, b: (b, h))
    a_spec = pl.BlockSpec((1, rows, CHUNK), lambda h, b: (h, b, 0))
    tl_spec = pl.BlockSpec((1, cb, SUBLANES, LANES), lambda h, b: (h, b, 0, 0))

    def body(q_ref, k_ref, v_ref, g_ref, b_ref, du_ref, dw_ref, da_ref, dqd_ref, dkt_ref, dtl_ref,
             dq_ref, dk_ref, dv_ref, dg_ref, db_ref):
        def group(gi, carry):
            r = pl.ds(pl.multiple_of(gi * (grp * CHUNK), grp * CHUNK), grp * CHUNK)
            wide = (grp, CHUNK, HEAD_DIM)
            ins = [ref[r, :].reshape(wide) for ref in (q_ref, k_ref, v_ref, g_ref, b_ref)]
            cts = (du_ref[r, :].reshape(wide), dw_ref[r, :].reshape(wide),
                   da_ref[0, r, :].reshape(grp, CHUNK, CHUNK), dqd_ref[r, :].reshape(wide),
                   dkt_ref[r, :].reshape(wide), dtl_ref[0, pl.ds(gi * grp, grp)])
            grads = jax.vjp(_gdn_chunk, *ins)[1](cts)
            for ref, val in zip((dq_ref, dk_ref, dv_ref, dg_ref, db_ref), grads):
                ref[r, :] = val.reshape(grp * CHUNK, HEAD_DIM)
            return carry

        grp = min(GDN_GROUP, cb)
        lax.fori_loop(0, cb // grp, group, 0)

    full = jax.ShapeDtypeStruct((t_rows, HW), F32)
    return pl.pallas_call(
        body, name="gdn_intra_bwd",
        out_shape=[full] * 5,
        grid=(HEADS, nc // cb),
        in_specs=[col] * 7 + [a_spec, col, col, tl_spec],
        out_specs=[col] * 5,
        compiler_params=_params(("parallel", "parallel")),
    )(qn, kn, vv, g_b, beta_b, du, dw, da, dqd, dkt, dtl)


def _head_cols(h):
    return slice(h * HEAD_DIM, (h + 1) * HEAD_DIM)


def _gdn_scan(u, w, aqk, qd, kt, tl):
    t_rows = u.shape[0]
    nc = t_rows // CHUNK
    cb = min(GDN_CB, nc)
    rows = cb * CHUNK
    wide = pl.BlockSpec((rows, HW), lambda b: (b, 0))

    def body(u_ref, w_ref, a_ref, qd_ref, kt_ref, tl_ref, o_ref, s_out_ref, s_ref):
        @pl.when(pl.program_id(0) == 0)
        def _():
            s_ref[...] = jnp.zeros_like(s_ref)

        def chunk(ci, carry):
            r = pl.ds(pl.multiple_of(ci * CHUNK, CHUNK), CHUNK)
            for h in range(HEADS):
                hc = _head_cols(h)
                s = s_ref[h]
                s_out_ref[ci, h] = s
                sb = _bf(s)
                vn = u_ref[r, hc] - _dot(_bf(w_ref[r, hc]), sb)
                vnb = _bf(vn)
                o_ref[r, hc] = _dot(_bf(qd_ref[r, hc]), sb) + _dot(_bf(a_ref[h, r, :]), vnb)
                s_ref[h] = s * tl_ref[h, ci, 0:1, :] + _dot_tn(_bf(kt_ref[r, hc]), vnb)
            return carry

        lax.fori_loop(0, cb, chunk, 0)

    return pl.pallas_call(
        body, name="gdn_scan_fwd",
        out_shape=[jax.ShapeDtypeStruct((t_rows, HW), F32),
                   jax.ShapeDtypeStruct((nc, HEADS, HEAD_DIM, HEAD_DIM), F32)],
        grid=(nc // cb,),
        in_specs=[wide, wide, pl.BlockSpec((HEADS, rows, CHUNK), lambda b: (0, b, 0)), wide, wide,
                  pl.BlockSpec((HEADS, cb, SUBLANES, LANES), lambda b: (0, b, 0, 0))],
        out_specs=[wide, pl.BlockSpec((cb, HEADS, HEAD_DIM, HEAD_DIM), lambda b: (b, 0, 0, 0))],
        scratch_shapes=[pltpu.VMEM((HEADS, HEAD_DIM, HEAD_DIM), F32)],
        compiler_params=_params(("arbitrary",)),
    )(u, w, aqk, qd, kt, tl)


def _gdn_scan_bwd(do, u, w, aqk, qd, kt, tl, states):
    t_rows = u.shape[0]
    nc = t_rows // CHUNK
    cb = min(GDN_CB, nc)
    rows = cb * CHUNK
    nb = nc // cb
    wide = pl.BlockSpec((rows, HW), lambda b: (nb - 1 - b, 0))
    a_spec = pl.BlockSpec((HEADS, rows, CHUNK), lambda b: (0, nb - 1 - b, 0))
    tl_spec = pl.BlockSpec((HEADS, cb, SUBLANES, LANES), lambda b: (0, nb - 1 - b, 0, 0))

    def body(do_ref, u_ref, w_ref, a_ref, qd_ref, kt_ref, tl_ref, s_in_ref,
             du_ref, dw_ref, da_ref, dqd_ref, dkt_ref, dtl_ref, ds_ref):
        @pl.when(pl.program_id(0) == 0)
        def _():
            ds_ref[...] = jnp.zeros_like(ds_ref)

        row0 = lax.broadcasted_iota(jnp.int32, (SUBLANES, LANES), 0) == 0

        def chunk(step, carry):
            ci = cb - 1 - step
            r = pl.ds(pl.multiple_of(ci * CHUNK, CHUNK), CHUNK)
            for h in range(HEADS):
                hc = _head_cols(h)
                s = s_in_ref[ci, h]
                ds_next = ds_ref[h]
                sb, dsb = _bf(s), _bf(ds_next)
                wb, ab, ktb, qdb = _bf(w_ref[r, hc]), _bf(a_ref[h, r, :]), _bf(kt_ref[r, hc]), _bf(qd_ref[r, hc])
                dob = _bf(do_ref[r, hc])
                vn = u_ref[r, hc] - _dot(wb, sb)
                vnb = _bf(vn)
                dvn = _dot_tn(ab, dob) + _dot(ktb, dsb)
                dvnb = _bf(dvn)
                du_ref[r, hc] = dvn
                dw_ref[r, hc] = -_dot_nt(dvnb, sb)
                da_ref[h, r, :] = _dot_nt(dob, vnb)
                dqd_ref[r, hc] = _dot_nt(dob, sb)
                dkt_ref[r, hc] = _dot_nt(vnb, dsb)
                dtl_ref[h, ci] = jnp.where(row0, _colsum(s * ds_next), 0.0)
                ds_ref[h] = _dot_tn(qdb, dob) + ds_next * tl_ref[h, ci, 0:1, :] - _dot_tn(wb, dvnb)
            return carry

        lax.fori_loop(0, cb, chunk, 0)

    full = jax.ShapeDtypeStruct((t_rows, HW), F32)
    return pl.pallas_call(
        body, name="gdn_scan_bwd",
        out_shape=[full, full, jax.ShapeDtypeStruct((HEADS, t_rows, CHUNK), F32), full, full,
                   jax.ShapeDtypeStruct((HEADS, nc, SUBLANES, LANES), F32)],
        grid=(nb,),
        in_specs=[wide, wide, wide, a_spec, wide, wide, tl_spec,
                  pl.BlockSpec((cb, HEADS, HEAD_DIM, HEAD_DIM), lambda b: (nb - 1 - b, 0, 0, 0))],
        out_specs=[wide, wide, a_spec, wide, wide, tl_spec],
        scratch_shapes=[pltpu.VMEM((HEADS, HEAD_DIM, HEAD_DIM), F32)],
        compiler_params=_params(("arbitrary",)),
    )(do, u, w, aqk, qd, kt, tl, states)


def _att_profile_index():
    j = lax.broadcasted_iota(jnp.int32, (SUBLANES, ATT_KW), 1)
    return jnp.clip(ATT_PAD - j, -(CHUNK - 1), MAX_REL) + (CHUNK - 1)


def _att_far_back():
    qi = lax.broadcasted_iota(jnp.int32, (ATT_QB, ATT_KW), 0)
    kj = lax.broadcasted_iota(jnp.int32, (ATT_QB, ATT_KW), 1)
    return kj < qi


def _rotate_rows(x, forward):
    rows, lanes = x.shape
    row = lax.broadcasted_iota(jnp.int32, x.shape, 0)
    for bit in range(rows.bit_length() - 1):
        amount = (1 << bit) if forward else lanes - (1 << bit)
        x = jnp.where(jnp.bitwise_and(jnp.right_shift(row, bit), 1) == 1, pltpu.roll(x, amount, 1), x)
    return x


def _att_in_band():
    qi = lax.broadcasted_iota(jnp.int32, (ATT_QB, ATT_KW), 0)
    kj = lax.broadcasted_iota(jnp.int32, (ATT_QB, ATT_KW), 1)
    shift = CHUNK.bit_length() - 1
    qc = jnp.right_shift(qi, shift)
    kc = jnp.right_shift(kj, shift) - LEFT_CHUNKS
    return (kc <= qc) & (kc >= qc - LEFT_CHUNKS)


def _att_valid(b):
    kj = lax.broadcasted_iota(jnp.int32, (1, ATT_KW), 1)
    return jnp.where(kj + b * ATT_QB >= ATT_PAD, 0.0, NEG_INF)


def _rms_parts(x, w):
    r = lax.rsqrt(jnp.mean(x * x, axis=-1, keepdims=True) + EPS)
    xn = x * r
    return xn * w, xn, r


def _rms_bwd(dy, xn, r, w):
    dxn = dy * w
    dx = r * (dxn - xn * jnp.mean(dxn * xn, axis=-1, keepdims=True))
    return dx, _colsum(dy * xn)


def _att_probs(qb, kb, bias, before_start):
    s = _dot_nt(qb, kb) * (HEAD_DIM ** -0.5) + bias + before_start
    e = jnp.exp(s - jnp.max(s, axis=-1, keepdims=True))
    return e * (1.0 / jnp.sum(e, axis=-1, keepdims=True))


def _att_specs():
    q_spec = pl.BlockSpec((ATT_QB, HEAD_DIM), lambda h, b: (b, h))
    k_specs = [pl.BlockSpec((ATT_QB, HEAD_DIM), lambda h, b, j=j: (b + j, HEADS + h)) for j in range(3)]
    v_specs = [pl.BlockSpec((ATT_QB, HEAD_DIM), lambda h, b, j=j: (b + j, 2 * HEADS + h)) for j in range(3)]
    w_spec = pl.BlockSpec((1, HEAD_DIM), lambda h, b: (0, 0))
    smem = pl.BlockSpec(memory_space=pltpu.SMEM)
    return q_spec, k_specs, v_specs, w_spec, smem


BIAS_SPEC = pl.BlockSpec((1, ATT_QB, ATT_KW), lambda h, b: (h, 0, 0))


def _expand_rel_bias(rel):
    def body(rel_ref, bias_ref):
        h = pl.program_id(0)
        idx = _att_profile_index()

        def fill(r, acc):
            return jnp.where(idx == r, rel_ref[h, r], acc)

        profile = lax.fori_loop(0, N_REL, fill, jnp.zeros((SUBLANES, ATT_KW), F32))
        table = _rotate_rows(jnp.concatenate([profile] * (ATT_QB // SUBLANES), axis=0), True)
        table = jnp.where(_att_far_back(), rel_ref[h, N_REL - 1], table)
        bias_ref[0] = jnp.where(_att_in_band(), table, NEG_INF)

    return pl.pallas_call(
        body, name="rel_bias_expand",
        out_shape=jax.ShapeDtypeStruct((HEADS, ATT_QB, ATT_KW), F32), grid=(HEADS,),
        in_specs=[pl.BlockSpec(memory_space=pltpu.SMEM)],
        out_specs=pl.BlockSpec((1, ATT_QB, ATT_KW), lambda h: (h, 0, 0)),
        compiler_params=_params(("parallel",)),
    )(rel)


def _attention(pb, pbp, qw, kw, bias):
    t_rows = pb.shape[0]
    q_spec, k_specs, v_specs, w_spec, _ = _att_specs()

    def body(q_ref, k0, k1, k2, v0, v1, v2, qw_ref, kw_ref, bias_ref, o_ref):
        b = pl.program_id(1)
        kwin = jnp.concatenate([k0[...], k1[...], k2[...]], axis=0)
        vwin = jnp.concatenate([v0[...], v1[...], v2[...]], axis=0)
        q = _rms(q_ref[...], qw_ref[...])
        k = _rms(kwin, kw_ref[...])
        p = _att_probs(_bf(q), _bf(k), bias_ref[0], _att_valid(b))
        o_ref[...] = _dot(_bf(p), _bf(vwin)).astype(o_ref.dtype)

    return pl.pallas_call(
        body, name="band_attention_fwd",
        out_shape=jax.ShapeDtypeStruct((t_rows, HW), BF16),
        grid=(HEADS, t_rows // ATT_QB),
        in_specs=[q_spec] + k_specs + v_specs + [w_spec, w_spec, BIAS_SPEC],
        out_specs=pl.BlockSpec((ATT_QB, HEAD_DIM), lambda h, b: (b, h)),
        compiler_params=_params(("parallel", "arbitrary")),
    )(pb, pbp, pbp, pbp, pbp, pbp, pbp, qw, kw, bias)


def _attention_bwd(pb, pbp, qw, kw, bias, dyb):
    t_rows = pb.shape[0]
    nb = t_rows // ATT_QB
    q_spec, k_specs, v_specs, w_spec, smem = _att_specs()
    pad_rows = t_rows + ATT_PAD
    acc_spec = pl.BlockSpec((pad_rows, HEAD_DIM), lambda h, b: (0, h))

    def body(q_ref, k0, k1, k2, v0, v1, v2, qw_ref, kw_ref, bias_ref, do_ref,
             dq_ref, dk_ref, dv_ref, dqw_ref, dkw_ref, drel_ref, dbias_ref):
        h, b = pl.program_id(0), pl.program_id(1)

        @pl.when(b == 0)
        def _():
            dbias_ref[...] = jnp.zeros_like(dbias_ref)
            dk_ref[...] = jnp.zeros_like(dk_ref)
            dv_ref[...] = jnp.zeros_like(dv_ref)

        @pl.when((b == 0) & (h == 0))
        def _():
            dqw_ref[...] = jnp.zeros_like(dqw_ref)
            dkw_ref[...] = jnp.zeros_like(dkw_ref)

        kwin = jnp.concatenate([k0[...], k1[...], k2[...]], axis=0)
        vwin = jnp.concatenate([v0[...], v1[...], v2[...]], axis=0)
        scale = HEAD_DIM ** -0.5
        qw_, kw_ = qw_ref[...], kw_ref[...]
        q, qn, rq = _rms_parts(q_ref[...], qw_)
        k, kn, rk = _rms_parts(kwin, kw_)
        qb, kb, dob = _bf(q), _bf(k), _bf(do_ref[...])
        p = _att_probs(qb, kb, bias_ref[0], _att_valid(b))
        dp = _dot_nt(dob, _bf(vwin))
        ds = p * (dp - jnp.sum(p * dp, axis=-1, keepdims=True))
        dbias_ref[...] += ds
        ds = _bf(ds)
        dq, dqw = _rms_bwd(_dot(ds, kb) * scale, qn, rq, qw_)
        dk, dkw = _rms_bwd(_dot_tn(ds, qb) * scale, kn, rk, kw_)
        dq_ref[...] = dq.astype(dq_ref.dtype)
        win = pl.ds(pl.multiple_of(b * ATT_QB, ATT_QB), ATT_KW)
        dk_ref[win, :] += dk
        dv_ref[win, :] += _dot_tn(_bf(p), dob)
        dqw_ref[...] += dqw
        dkw_ref[...] += dkw

        @pl.when(b == nb - 1)
        def _():
            tot, far = dbias_ref[...], _att_far_back()
            far_sum = jnp.sum(jnp.where(far, tot, 0.0))
            per_offset = _colsum(_rotate_rows(jnp.where(far, 0.0, tot), False))
            idx = _att_profile_index()
            first_row = lax.broadcasted_iota(jnp.int32, idx.shape, 0) == 0
            spread = jnp.where(first_row, per_offset, 0.0)

            def reduce(r, carry):
                drel_ref[h, r] = jnp.sum(jnp.where(idx == r, spread, 0.0)) + jnp.where(r == N_REL - 1, far_sum, 0.0)
                return carry

            lax.fori_loop(0, N_REL, reduce, 0)

    return pl.pallas_call(
        body, name="band_attention_bwd",
        out_shape=[jax.ShapeDtypeStruct((t_rows, HW), BF16),
                   jax.ShapeDtypeStruct((pad_rows, HW), F32), jax.ShapeDtypeStruct((pad_rows, HW), F32),
                   jax.ShapeDtypeStruct((1, HEAD_DIM), F32), jax.ShapeDtypeStruct((1, HEAD_DIM), F32),
                   jax.ShapeDtypeStruct((HEADS, N_REL), F32)],
        grid=(HEADS, nb),
        in_specs=[q_spec] + k_specs + v_specs + [w_spec, w_spec, BIAS_SPEC, q_spec],
        out_specs=[q_spec, acc_spec, acc_spec, w_spec, w_spec, smem],
        scratch_shapes=[pltpu.VMEM((ATT_QB, ATT_KW), F32)],
        compiler_params=_params(("arbitrary", "arbitrary")),
    )(pb, pbp, pbp, pbp, pbp, pbp, pbp, qw, kw, bias, dyb)


def _me():
    return lax.axis_index("x"), lax.axis_index("y"), lax.axis_index("c")


def _index(x, y, c):
    return 4 * x + 2 * y + c


HBM_SPEC = pl.BlockSpec(memory_space=pl.ANY)


def _block(ref, kind, d, r, c):
    if kind == "rows":
        return ref.at[pl.ds(d * r, r), :]
    if kind == "win":
        return ref.at[:, pl.ds(d * WIN_STEP, c)]
    return ref.at[:, pl.ds(d * c, c)]


def _all_gather(shards, kinds, n_gather):
    n = len(shards)

    def body(*refs):
        x_refs, out_refs = refs[:n], refs[n:2 * n]
        send_sems, recv_sems, local_sems = refs[2 * n:]
        x, y, c = _me()
        me, sibling = (x, y, c), (x, y, 1 - c)
        chips = [(1 - x, y), (x, 1 - y), (1 - x, 1 - y)]

        def copy(i, k, blk, to, src=None):
            r_, c_ = shards[i].shape
            dst = _block(out_refs[i], kinds[i], _index(*blk), r_, c_)
            return pltpu.make_async_remote_copy(
                src_ref=dst if src is None else src, dst_ref=dst,
                send_sem=send_sems.at[i, k], recv_sem=recv_sems.at[i, k], device_id=to, device_id_type=MESH)

        sends, local = [], []
        for i in range(n):
            r_, c_ = shards[i].shape
            mine = pltpu.make_async_copy(x_refs[i], _block(out_refs[i], kinds[i], _index(*me), r_, c_),
                                         local_sems.at[i])
            mine.start()
            local.append(mine)
            if i >= n_gather:
                continue
            first = [copy(i, 0, me, sibling, src=x_refs[i])]
            first += [copy(i, 1 + j, me, (*chip, c), src=x_refs[i]) for j, chip in enumerate(chips)]
            for cp in first:
                cp.start()
            sends += first
        for i in range(n_gather):
            for j, chip in enumerate(chips):
                copy(i, 1 + j, (*chip, c), me).wait_recv()
                passed = copy(i, 4 + j, (*chip, c), sibling)
                passed.start()
                sends.append(passed)
        for i in range(n_gather):
            copy(i, 0, sibling, me).wait_recv()
            for j, chip in enumerate(chips):
                copy(i, 4 + j, (*chip, 1 - c), me).wait_recv()
        for cp in sends:
            cp.wait_send()
        for cp in local:
            cp.wait()

    def full_shape(s, kind):
        r_, c_ = s.shape
        return (N_DEV * r_, c_) if kind == "rows" else (r_, N_DEV * c_)

    return pl.pallas_call(
        body, name="weights_all_gather",
        out_shape=[jax.ShapeDtypeStruct(full_shape(s, k), s.dtype) for s, k in zip(shards, kinds)],
        in_specs=[HBM_SPEC] * n, out_specs=[HBM_SPEC] * n,
        scratch_shapes=[pltpu.SemaphoreType.DMA((n_gather, 7)), pltpu.SemaphoreType.DMA((n_gather, 7)),
                        pltpu.SemaphoreType.DMA((n,))],
        compiler_params=pltpu.CompilerParams(has_side_effects=True),
    )(*shards)


SEM_SPEC = pl.BlockSpec(memory_space=pltpu.SEMAPHORE)
HBM_ONLY = pl.BlockSpec(memory_space=pltpu.HBM)
DATAFLOW = pltpu.SideEffectType.DATAFLOW_SIDE_EFFECTING


def _peers():
    x, y, c = _me()
    return [(x ^ (k >> 2), y ^ ((k >> 1) & 1), c ^ (k & 1)) for k in range(1, N_DEV)]


def _gather_copies(shapes, kinds):
    def make(src_refs, land_refs, send_sems, recv_sems):
        mine = _index(*_me())
        return [pltpu.make_async_remote_copy(
            src_ref=src_refs[i], dst_ref=_block(land_refs[i], kind, mine, r, c),
            send_sem=send_sems.at[7 * i + k], recv_sem=recv_sems.at[7 * i + k], device_id=peer, device_id_type=MESH)
            for i, ((r, c), kind) in enumerate(zip(shapes, kinds)) for k, peer in enumerate(_peers())]

    return make


def _exchange_copies(shapes, kinds):
    def make(src_refs, land_refs, send_sems, recv_sems):
        mine = _index(*_me())
        return [pltpu.make_async_remote_copy(
            src_ref=_block(src_refs[i], kind, _index(*peer), r, c), dst_ref=land_refs[i].at[mine],
            send_sem=send_sems.at[7 * i + k], recv_sem=recv_sems.at[7 * i + k], device_id=peer, device_id_type=MESH)
            for i, ((r, c), kind) in enumerate(zip(shapes, kinds)) for k, peer in enumerate(_peers())]

    return make


def _place_block(shard, kind, name):
    r, c = shard.shape
    tile = _row_tile(r, c)
    nt = r // tile
    full = (N_DEV * r, c) if kind == "rows" else (r, N_DEV * c)

    def body(me_ref, x_ref, out_ref):
        out_ref[...] = x_ref[...]

    if kind == "rows":
        out_spec = pl.BlockSpec((tile, c), lambda i, me: (me[0] * nt + i, 0))
    else:
        out_spec = pl.BlockSpec((tile, c), lambda i, me: (i, me[0]))
    return pl.pallas_call(
        body, name=name, out_shape=jax.ShapeDtypeStruct(full, shard.dtype),
        grid_spec=pltpu.PrefetchScalarGridSpec(
            num_scalar_prefetch=1, grid=(nt,),
            in_specs=[pl.BlockSpec((tile, c), lambda i, me: (i, 0))], out_specs=out_spec),
        compiler_params=_params(("arbitrary",)),
    )(_my_index_operand(), shard)


def _split_start(srcs, lands, make, name):
    n = len(srcs)

    def body(*refs):
        send_sems, recv_sems = refs[2 * n], refs[2 * n + 1]
        for cp in make(refs[:n], refs[n:2 * n], send_sems, recv_sems):
            cp.start()
        refs[-1][...] = jnp.zeros_like(refs[-1])

    arrays = list(srcs) + list(lands)
    out = pl.pallas_call(
        body, name=name,
        out_shape=(pltpu.SemaphoreType.DMA((7 * n,)), pltpu.SemaphoreType.DMA((7 * n,)),
                   *[pltpu.HBM(a.shape, a.dtype) for a in arrays], jax.ShapeDtypeStruct((SUBLANES, LANES), F32)),
        in_specs=[HBM_ONLY] * (2 * n),
        out_specs=(SEM_SPEC, SEM_SPEC, *[HBM_ONLY] * (2 * n), pl.BlockSpec(memory_space=pltpu.VMEM)),
        input_output_aliases={i: 2 + i for i in range(2 * n)},
        compiler_params=pltpu.CompilerParams(has_side_effects=DATAFLOW),
    )(*[pltpu.with_memory_space_constraint(a, pltpu.HBM) for a in arrays])
    return out[0], out[1], list(out[2:2 + n]), list(out[2 + n:2 + 2 * n]), out[-1]


def _split_wait(send_sems, recv_sems, srcs, lands, after, make, name):
    n = len(srcs)

    def body(*refs):
        for cp in make(refs[:n], refs[n:2 * n], refs[2 * n], refs[2 * n + 1]):
            cp.wait_send()
            cp.wait_recv()

    arrays = list(srcs) + list(lands)
    out = pl.pallas_call(
        body, name=name,
        out_shape=tuple(pltpu.HBM(a.shape, a.dtype) for a in arrays),
        in_specs=[HBM_ONLY] * (2 * n) + [SEM_SPEC, SEM_SPEC, pl.BlockSpec(memory_space=pl.ANY)],
        out_specs=tuple([HBM_ONLY] * (2 * n)),
        input_output_aliases={i: i for i in range(2 * n)},
        compiler_params=pltpu.CompilerParams(has_side_effects=DATAFLOW),
    )(*arrays, send_sems, recv_sems, after)
    return list(out[:n]), list(out[n:])


def _all_reduce_small(vals, name):
    rows, width = vals.shape

    def body(x_ref, out_ref, buf_ref, send_sems, recv_sems):
        x, y, c = _me()
        mine = _index(x, y, c)
        buf_ref[mine] = x_ref[...]
        copies = []
        for k in range(1, N_DEV):
            px, py, pc = x ^ (k >> 2), y ^ ((k >> 1) & 1), c ^ (k & 1)
            copies.append(pltpu.make_async_remote_copy(
                src_ref=x_ref, dst_ref=buf_ref.at[mine],
                send_sem=send_sems.at[k - 1], recv_sem=recv_sems.at[k - 1],
                device_id=(px, py, pc), device_id_type=MESH))
        for cp in copies:
            cp.start()
        for cp in copies:
            cp.wait()
        acc = buf_ref[0]
        for j in range(1, N_DEV):
            acc = acc + buf_ref[j]
        out_ref[...] = acc

    vmem = pl.BlockSpec(memory_space=pltpu.VMEM)
    return pl.pallas_call(
        body, name=name,
        out_shape=jax.ShapeDtypeStruct(vals.shape, F32),
        in_specs=[vmem], out_specs=vmem,
        scratch_shapes=[pltpu.VMEM((N_DEV, rows, width), F32),
                        pltpu.SemaphoreType.DMA((7,)), pltpu.SemaphoreType.DMA((7,))],
        compiler_params=pltpu.CompilerParams(has_side_effects=True),
    )(vals)


def _adamw_math(w, g, m, v):
    m = ADAM_B1 * m + (1.0 - ADAM_B1) * g
    v = ADAM_B2 * v + (1.0 - ADAM_B2) * (g * g)
    m_hat = m / (1.0 - ADAM_B1 ** ADAM_STEP)
    v_hat = v / (1.0 - ADAM_B2 ** ADAM_STEP)
    delta = -ADAM_LR * (m_hat / (jnp.sqrt(v_hat) + ADAM_EPS) + ADAM_WD * w)
    return delta, m, v


ROW_TILE_ELEMS = 384 * 1024


def _row_tile(rows, width):
    best = SUBLANES
    for t in range(SUBLANES, rows + 1, SUBLANES):
        if rows % t == 0 and t * width <= ROW_TILE_ELEMS:
            best = t
    return best


def _sum_received(r_ref, own, me):
    g = None
    for j in range(N_DEV):
        term = jnp.where(me == j, own, r_ref[j].astype(F32))
        g = term if g is None else g + term
    return g


def _my_index_operand():
    return _index(*_me()).astype(jnp.int32).reshape(1)


def _adamw_recv(recv, grad, kind, w, m, v, name):
    _, rows, width = recv.shape
    tile = _row_tile(rows, width)
    nt = rows // tile

    def body(me_ref, r_ref, own_ref, w_ref, m_ref, v_ref, g_out, d_out, m_out, v_out):
        g = _sum_received(r_ref, own_ref[...].astype(F32), me_ref[0])
        d, mn, vn = _adamw_math(w_ref[...], g, m_ref[...], v_ref[...])
        g_out[...] = g
        d_out[...] = d
        m_out[...] = mn
        v_out[...] = vn

    if kind == "rows":
        own_spec = pl.BlockSpec((tile, width), lambda i, me: (me[0] * nt + i, 0))
    else:
        own_spec = pl.BlockSpec((tile, width), lambda i, me: (i, me[0]))
    spec = pl.BlockSpec((tile, width), lambda i, me: (i, 0))
    shape = jax.ShapeDtypeStruct((rows, width), F32)
    return pl.pallas_call(
        body, name=name, out_shape=[shape] * 4,
        grid_spec=pltpu.PrefetchScalarGridSpec(
            num_scalar_prefetch=1, grid=(nt,),
            in_specs=[pl.BlockSpec((N_DEV, tile, width), lambda i, me: (0, i, 0)), own_spec, spec, spec, spec],
            out_specs=[spec] * 4),
        compiler_params=_params(("parallel",)),
    )(_my_index_operand(), recv, grad, w, m, v)


WIN_STEP = 1408
WIN_W = 1536
IN_SHARD = IN_COLS // N_DEV
IN_PADDED = WIN_STEP * (N_DEV - 1) + WIN_W


def _roll_w_in(shard_padded):
    rows = shard_padded.shape[0]
    tile = _row_tile(rows, WIN_W)

    def body(x_ref, main_ref, edge_ref):
        win = pltpu.roll(x_ref[...], 2 * _index(*_me()), 1).astype(BF16)
        main_ref[...] = win[:, :WIN_STEP]
        edge_ref[...] = win[:, WIN_STEP:]

    return pl.pallas_call(
        body, name="w_in_window",
        out_shape=[jax.ShapeDtypeStruct((rows, WIN_STEP), BF16), jax.ShapeDtypeStruct((rows, WIN_W - WIN_STEP), BF16)],
        grid=(rows // tile,),
        in_specs=[pl.BlockSpec((tile, WIN_W), lambda i: (i, 0))],
        out_specs=[pl.BlockSpec((tile, WIN_STEP), lambda i: (i, 0)),
                   pl.BlockSpec((tile, WIN_W - WIN_STEP), lambda i: (i, 0))],
        compiler_params=_params(("parallel",)),
    )(shard_padded)


def _sum_w_in_windows(recv, grad):
    _, rows, width = recv.shape
    tile = _row_tile(rows, width)

    def body(me_ref, r_ref, g_ref, g_out, own_ref, sem):
        me = me_ref[0]
        rows_i = pl.ds(pl.multiple_of(pl.program_id(0) * tile, tile), tile)
        own = pltpu.make_async_copy(g_ref.at[rows_i, pl.ds(pl.multiple_of(me * WIN_STEP, LANES), width)], own_ref, sem)
        own.start()
        own.wait()
        g_out[...] = pltpu.roll(_sum_received(r_ref, own_ref[...].astype(F32), me), width - 2 * me, 1)

    return pl.pallas_call(
        body, name="w_in_grad_sum", out_shape=jax.ShapeDtypeStruct((rows, width), F32),
        grid_spec=pltpu.PrefetchScalarGridSpec(
            num_scalar_prefetch=1, grid=(rows // tile,),
            in_specs=[pl.BlockSpec((N_DEV, tile, width), lambda i, me: (0, i, 0)), HBM_SPEC],
            out_specs=pl.BlockSpec((tile, width), lambda i, me: (i, 0)),
            scratch_shapes=[pltpu.VMEM((tile, width), BF16), pltpu.SemaphoreType.DMA]),
        compiler_params=_params(("arbitrary",)),
    )(_my_index_operand(), recv, grad)


def _adamw_small(w, g, m, v, name):
    def fn(i, n, w_, g_, m_, v_):
        return _adamw_math(w_, g_, m_, v_)

    r, c = w.shape
    return _rows(fn, [(w, "t"), (g, "t"), (m, "t"), (v, "t")], [], [(c, F32)] * 3, [], _row_tile(r, c), name)


def _norm_fwd(x, w, name):
    return _rows(lambda i, n, x_, w_: (_rms(x_, w_[...]),), [(x, "t")], [w], [(D_MODEL, BF16)], [], 512, name)[0]


def _residual_norm_fwd(x, y, scale, w, name):
    def fn(i, n, x_, y_, w_):
        xn = x_ + scale * y_
        return xn, _rms(xn, w_[...])

    return _rows(fn, [(x, "t"), (y, "t")], [w], [(D_MODEL, F32), (D_MODEL, BF16)], [], 512, name)


def _residual_norm_bwd(x, w, dhs, dres, scale, name):
    nh = len(dhs)

    def fn(i, n, x_, dres_, *rest):
        dh = rest[0]
        for extra in rest[1:nh]:
            dh = dh + extra
        _, vjp = jax.vjp(_rms, x_, rest[nh][...])
        dx, dw = vjp(dh)
        dx = dx + dres_
        return dx, scale * dx, dw

    return _rows(fn, [(x, "t"), (dres, "t")] + [(d, "t") for d in dhs], [w],
                 [(D_MODEL, F32), (D_MODEL, BF16)], [(1, D_MODEL)], 256, name)


def _ffn_fwd(h, w_gu, get_w_down, tag):
    gu = _matmul(h, w_gu, "nn", BF16, tag + "_gu")
    act = _rows(lambda i, n, gu_: (_swiglu(gu_),), [(gu, "t")], [], [(D_FF, BF16)], [], 128, tag + "_swiglu")[0]
    y = _matmul(act, get_w_down(act), "nn", F32, tag + "_down")
    return gu, act, y


def _ffn_bwd(h, gu, act, dy, w_gu, w_down, tag, comm, more=None):
    dact = _matmul(dy, w_down, "nt", BF16, tag + "_dact")

    def fn(i, n, gu_, dact_):
        _, vjp = jax.vjp(_swiglu, gu_)
        return vjp(dact_)

    dgu = _rows(fn, [(gu, "t"), (dact, "t")], [], [(2 * D_FF, BF16)], [], 128, tag + "_swiglu_bwd")[0]
    sent = comm.send(tag + "_gu", {tag + "_w_gu": _matmul(h, dgu, "tn", BF16, tag + "_d_w_gu")})
    sent = sent + comm.send(tag + "_down", {tag + "_w_down": _matmul(act, dy + sent.astype(BF16), "tn", BF16,
                                                                    tag + "_d_w_down"), **(more or {})})
    dh = _matmul(dgu, w_gu, "nt", BF16, tag + "_dh")
    return dh, sent


def _expanders():
    e_g = np.zeros((LANES, HW), np.float32)
    e_b = np.zeros((LANES, HW), np.float32)
    for h in range(HEADS):
        e_g[h, h * HEAD_DIM:(h + 1) * HEAD_DIM] = 1.0
        e_b[HEADS + h, h * HEAD_DIM:(h + 1) * HEAD_DIM] = 1.0
    return jnp.asarray(e_g), jnp.asarray(e_b)


def _pad_lanes(v):
    return jnp.pad(v, ((0, 0), (0, LANES - v.shape[1])))


class _LocalWeights:
    def __init__(self, big):
        self.big, self.sent = big, {}

    def arrive(self, group, after):
        return self.big

    def send(self, group, grads):
        self.sent.update(grads)
        return jnp.zeros((), F32)


def _local_step(x, p, tgt, small, comm):
    e_g, e_b = _expanders()
    alog, dtb = _pad_lanes(small["a_log"]), _pad_lanes(small["dt_bias"])
    conv_w = jnp.pad(small["conv_w"], ((0, SUBLANES - CONV_K), (0, 0)))
    rel = _expand_rel_bias(small["rel_bias"])

    h1 = _norm_fwd(x, small["ffn1_norm"], "ffn1_norm")
    big = dict(comm.arrive("ffn1", h1))
    if "_token" in big:
        h1 = h1 + big.pop("_token").astype(BF16)

    def ffn1_w_down(act):
        big.update(comm.arrive("ffn1_down", act))
        return big["ffn1_w_down"]

    gu1, act1, y1 = _ffn_fwd(h1, big["ffn1_w_gu"], ffn1_w_down, "ffn1")
    x1, h2 = _residual_norm_fwd(x, y1, 0.5, small["mix_norm"], "mix_norm")

    big = {**big, **comm.arrive("mixer", h2)}
    w_in = big["w_in"]
    w_qz = w_in[:, :IN_QZ]
    w_ab = jnp.pad(w_in[:, IN_AB0:IN_QKVB0], ((0, 0), (0, LANES - 2 * HEADS)))
    w_qkvb = w_in[:, IN_QKVB0:IN_GG0]
    w_gg = w_in[:, IN_GG0:IN_COLS]
    qz = _matmul(h2, w_qz, "nn", F32, "in_qz")
    ab = _matmul(h2, w_ab, "nn", F32, "in_ab")
    pb = _matmul(h2, w_qkvb, "nn", F32, "in_qkvb")
    gg = _matmul(h2, w_gg, "nn", BF16, "in_gates")
    pa, z = qz[:, :3 * HW], qz[:, 3 * HW:]

    def prep(i, n, pa_, prev_, ab_, cw_, alog_, dtb_, eg_, eb_):
        q, k, v = _gdn_post(_conv(pa_, prev_, cw_, i))
        g_b, beta_b = _gdn_gates(ab_, alog_[...], dtb_[...], eg_[...], eb_[...])
        return q, k, v, g_b, beta_b

    qn, kn, vv, g_b, beta_b = _rows(prep, [(pa, "t"), (pa, "p"), (ab, "t")], [conv_w, alog, dtb, e_g, e_b],
                                    [(HW, F32)] * 5, [], 256, "gdn_prep")
    u, w, aqk, qd, kt, tl = _gdn_intra(qn, kn, vv, g_b, beta_b)
    o, states = _gdn_scan(u, w, aqk, qd, kt, tl)
    ya = _rows(lambda i, n, o_, z_, w_: (_gated_norm(o_, z_, w_[...]),), [(o, "t"), (z, "t")], [small["gdn_norm"]],
               [(HW, BF16)], [], 512, "gdn_gated_norm")[0]

    pbp = jnp.pad(pb, ((ATT_PAD, 0), (0, 0)))
    yb = _attention(pb, pbp, small["q_norm"], small["k_norm"], rel)

    big = {**big, **comm.arrive("branches", yb)}
    ta = _matmul(ya, big["w_branch_a"], "nn", BF16, "branch_a")
    tb = _matmul(yb, big["w_branch_b"], "nn", BF16, "branch_b")
    mixed = _rows(lambda i, n, gg_, ta_, tb_: (_mix(gg_, ta_, tb_),), [(gg, "t"), (ta, "t"), (tb, "t")], [],
                  [(D_MODEL, BF16)], [], 256, "mix")[0]
    m_out = _matmul(mixed, big["w_out"], "nn", F32, "w_out")
    x2, h3 = _residual_norm_fwd(x1, m_out, 1.0, small["ffn2_norm"], "ffn2_norm")
    big = {**big, **comm.arrive("tail", h3)}
    gu2, act2, y2 = _ffn_fwd(h3, big["ffn2_w_gu"], lambda act: big["ffn2_w_down"], "ffn2")
    x3, h4 = _residual_norm_fwd(x2, y2, 0.5, small["ple_norm"], "ple_norm")
    gp = _matmul(h4, big["ple_gate"], "nn", BF16, "ple_gate")
    pp = _matmul(p, big["ple_proj"], "nn", BF16, "ple_proj")

    def head(i, n, x3_, gp_, pp_, tgt_):
        sg = _sigmoid(gp_)
        err = x3_ + sg * pp_ - tgt_
        dx4 = err * (1.0 / D_MODEL)
        sq = _colsum(err * err)
        part = sq[:, :LANES]
        for j in range(1, D_MODEL // LANES):
            part = part + sq[:, j * LANES:(j + 1) * LANES]
        return dx4, dx4 * pp_ * sg * (1.0 - sg), dx4 * sg, (0.5 / D_MODEL) * part

    dx4, dgp, dpp, loss_lanes = _rows(head, [(x3, "t"), (gp, "t"), (pp, "t"), (tgt, "t")], [],
                                      [(D_MODEL, F32), (D_MODEL, BF16), (D_MODEL, BF16)], [(1, LANES)], 256,
                                      "ple_loss_head")
    loss = jnp.sum(loss_lanes)

    gbig, gsmall = {}, {}
    gbig["ple_proj"] = _matmul(p, dpp, "tn", BF16, "d_ple_proj")
    gbig["ple_gate"] = _matmul(h4, dgp, "tn", BF16, "d_ple_gate")
    dh4 = _matmul(dgp, big["ple_gate"], "nt", BF16, "ple_gate_dh")
    dx3, dy2, gsmall["ple_norm"] = _residual_norm_bwd(x3, small["ple_norm"], [dh4], dx4, 0.5, "ple_norm_bwd")

    dh3, sent = _ffn_bwd(h3, gu2, act2, dy2, big["ffn2_w_gu"], big["ffn2_w_down"], "ffn2", comm,
                         {n: gbig[n] for n in ("ple_proj", "ple_gate")})
    dx2, dx2b, gsmall["ffn2_norm"] = _residual_norm_bwd(x2, small["ffn2_norm"] + sent, [dh3], dx3, 1.0,
                                                        "ffn2_norm_bwd")

    gbig["w_out"] = _matmul(mixed, dx2b, "tn", BF16, "d_w_out")
    dmixed = _matmul(dx2b, big["w_out"], "nt", BF16, "w_out_dx")

    def mix_bwd(i, n, gg_, ta_, tb_, dm_):
        _, vjp = jax.vjp(_mix, gg_, ta_, tb_)
        return vjp(dm_)

    dgg, dta, dtb_ = _rows(mix_bwd, [(gg, "t"), (ta, "t"), (tb, "t"), (dmixed, "t")], [],
                           [(2 * D_MODEL, BF16), (D_MODEL, BF16), (D_MODEL, BF16)], [], 256, "mix_bwd")
    gbig["w_branch_a"] = _matmul(ya, dta, "tn", BF16, "d_branch_a")
    gbig["w_branch_b"] = _matmul(yb, dtb_, "tn", BF16, "d_branch_b")
    dya = _matmul(dta, big["w_branch_a"], "nt", BF16, "branch_a_dx")
    dyb = _matmul(dtb_, big["w_branch_b"], "nt", BF16, "branch_b_dx")

    dq_b, dk_b, dv_b, gsmall["q_norm"], gsmall["k_norm"], gsmall["rel_bias"] = _attention_bwd(
        pb, pbp, small["q_norm"], small["k_norm"], rel, dyb)
    dpb = jnp.concatenate([dq_b, dk_b[ATT_PAD:].astype(BF16), dv_b[ATT_PAD:].astype(BF16)], axis=1)

    def gated_bwd(i, n, o_, z_, dya_, w_):
        _, vjp = jax.vjp(_gated_norm, o_, z_, w_[...])
        return vjp(dya_)

    do, dz, gsmall["gdn_norm"] = _rows(gated_bwd, [(o, "t"), (z, "t"), (dya, "t")], [small["gdn_norm"]],
                                       [(HW, F32), (HW, BF16)], [(1, HEAD_DIM)], 256, "gdn_gated_norm_bwd")
    du, dw, da, dqd, dkt, dtl = _gdn_scan_bwd(do, u, w, aqk, qd, kt, tl, states)
    dqn, dkn, dvv, dg_b, dbeta_b = _gdn_intra_bwd(qn, kn, vv, g_b, beta_b, du, dw, da, dqd, dkt, dtl)

    def prep_bwd(i, n, pa_, prev_, ab_, dq_, dk_, dv_, dg_, db_, cw_, alog_, dtb_, eg_, eb_):
        _, vjp = jax.vjp(_gdn_post, _conv(pa_, prev_, cw_, i))
        (dy,) = vjp((dq_, dk_, dv_))
        e_g_, e_b_ = eg_[...], eb_[...]
        _, vjp_g = jax.vjp(lambda a, b, c: _gdn_gates(a, b, c, e_g_, e_b_), ab_, alog_[...], dtb_[...])
        dab, dalog, ddtb = vjp_g((dg_, db_))
        return dy, dab, dalog, ddtb

    dy_conv, dab, dalog, ddtb = _rows(
        prep_bwd, [(pa, "t"), (pa, "p"), (ab, "t"), (dqn, "t"), (dkn, "t"), (dvv, "t"), (dg_b, "t"), (dbeta_b, "t")],
        [conv_w, alog, dtb, e_g, e_b], [(3 * HW, F32), (LANES, BF16)], [(1, LANES), (1, LANES)], 256,
        "gdn_prep_bwd")
    gsmall["a_log"] = dalog[:, :HEADS]
    gsmall["dt_bias"] = ddtb[:, :HEADS]

    def conv_bwd(i, n, dy_, nxt_, pa_, prev_, cw_):
        dpa = dy_ * cw_[CONV_K - 1:CONV_K, :]
        row = lax.broadcasted_iota(jnp.int32, (SUBLANES, dy_.shape[1]), 0)
        dcw = jnp.where(row == CONV_K - 1, _colsum(dy_ * pa_), 0.0)
        for j in range(CONV_K - 1):
            s = CONV_K - 1 - j
            dpa = dpa + _shift_up(dy_, nxt_, s, i, n) * cw_[j:j + 1, :]
            dcw = dcw + jnp.where(row == j, _colsum(dy_ * _shift_down(pa_, prev_, s, i)), 0.0)
        return dpa, dcw

    dpa, dcw = _rows(conv_bwd, [(dy_conv, "t"), (dy_conv, "n"), (pa, "t"), (pa, "p")], [conv_w],
                     [(3 * HW, BF16)], [(SUBLANES, 3 * HW)], 256, "gdn_conv_bwd")
    gsmall["conv_w"] = dcw[:CONV_K]

    dqz = jnp.concatenate([dpa, dz], axis=1)
    d_w_qz = _matmul(h2, dqz, "tn", BF16, "d_in_qz")
    d_w_ab = _matmul(h2, dab, "tn", BF16, "d_in_ab")
    d_w_qkvb = _matmul(h2, dpb, "tn", BF16, "d_in_qkvb")
    d_w_gg = _matmul(h2, dgg, "tn", BF16, "d_in_gates")
    gbig["w_in"] = jnp.concatenate([d_w_qz, d_w_ab[:, :2 * HEADS], d_w_qkvb, d_w_gg,
                                    jnp.zeros((D_MODEL, IN_PADDED - IN_COLS), BF16)], axis=1)
    dh2 = [_matmul(dqz, w_qz, "nt", BF16, "in_qz_dh"), _matmul(dab, w_ab, "nt", BF16, "in_ab_dh"),
           _matmul(dpb, w_qkvb, "nt", BF16, "in_qkvb_dh"), _matmul(dgg, w_gg, "nt", BF16, "in_gates_dh")]
    sent = comm.send("mixer", {n: gbig[n] for n in ("w_out", "w_branch_b", "w_branch_a", "w_in")})
    dx1, dy1, gsmall["mix_norm"] = _residual_norm_bwd(x1, small["mix_norm"] + sent, dh2, dx2, 0.5, "mix_norm_bwd")

    dh1, sent = _ffn_bwd(h1, gu1, act1, dy1, big["ffn1_w_gu"], big["ffn1_w_down"], "ffn1", comm)
    grad_x, _, gsmall["ffn1_norm"] = _residual_norm_bwd(x, small["ffn1_norm"] + sent, [dh1], dx1, 1.0,
                                                        "ffn1_norm_bwd")
    return loss, grad_x, gsmall


GATHER_GROUPS = {"ffn1": ("ffn1_w_gu",),
                 "ffn1_down": ("ffn1_w_down",),
                 "mixer": ("w_in_main", "w_in_edge"),
                 "branches": ("w_branch_a", "w_branch_b", "w_out"),
                 "tail": ("ffn2_w_gu", "ffn2_w_down", "ple_gate", "ple_proj")}
SPLIT_GATHERS = ("ffn1_down", "mixer", "branches", "tail")


def _kind(name):
    return "cols" if name in COL_SHARDED or name.startswith("w_in_") else "rows"


def _merge_w_in(main, edges):
    edge_w = WIN_W - WIN_STEP
    w_in = jnp.pad(main, ((0, 0), (0, edge_w)))
    for d in range(N_DEV):
        at = WIN_STEP * (d + 1)
        w_in = w_in + jnp.pad(edges[:, d * edge_w:(d + 1) * edge_w], ((0, 0), (at, IN_PADDED - at - edge_w)))
    return w_in


class _Fsdp:
    def __init__(self, wts, first):
        self.wts, self.first_token = wts, first
        main, edge = _roll_w_in(jnp.pad(wts["w_in"], ((0, 0), (0, WIN_W - IN_SHARD))))
        self.shards = {n: wts[n].astype(BF16) for n in BIG if n not in ("w_in", "ffn1_w_gu")}
        self.shards.update(w_in_main=main, w_in_edge=edge)
        self.lands = {n: _place_block(self.shards[n], _kind(n), "own_" + n)
                      for group in SPLIT_GATHERS for n in GATHER_GROUPS[group]}
        self.flight, self.sent = {}, {}

    def _gather_first(self, after):
        token = self.first_token + after[0, 0].astype(F32) * 0.0
        me = _index(*_me())
        for n, land in self.lands.items():
            r, c = self.shards[n].shape
            at = (me * r, 0) if _kind(n) == "rows" else (0, me * c)
            token = token + lax.dynamic_slice(land, at, (1, 1))[0, 0].astype(F32) * 0.0
        shard = (self.wts["ffn1_w_gu"] + token).astype(BF16)
        self.shards["ffn1_w_gu"] = shard
        first = _all_gather([shard], [_kind("ffn1_w_gu")], 1)[0]
        token = first[0, 0].astype(F32) * 0.0
        for group in SPLIT_GATHERS:
            names = GATHER_GROUPS[group]
            srcs = [self.shards[n] for n in names]
            lands = [self.lands[n] for n in names]
            make = _gather_copies([s.shape for s in srcs], [_kind(n) for n in names])
            srcs[0] = srcs[0] + token.astype(BF16)
            send_sems, recv_sems, srcs, lands, tok = _split_start(srcs, lands, make, "gather_start_" + group)
            token = token + tok[0, 0]
            self.flight[group] = (send_sems, recv_sems, srcs, lands, make)
        return {"ffn1_w_gu": first, "_token": token}

    def arrive(self, group, after):
        if group == "ffn1":
            return self._gather_first(after)
        send_sems, recv_sems, srcs, lands, make = self.flight[group]
        _, lands = _split_wait(send_sems, recv_sems, srcs, lands, after, make, "gather_wait_" + group)
        full = dict(zip(GATHER_GROUPS[group], lands))
        if group == "mixer":
            full["w_in"] = _merge_w_in(full.pop("w_in_main"), full.pop("w_in_edge"))
        return full

    def send(self, group, grads):
        names = list(grads)
        kinds = ["win" if n == "w_in" else _kind(n) for n in names]
        shapes = [(D_MODEL, WIN_W) if n == "w_in" else self.shards[n].shape for n in names]
        srcs = [grads[n] for n in names]
        lands = [lax.empty((N_DEV,) + tuple(s), BF16) for s in shapes]
        make = _exchange_copies(shapes, kinds)
        send_sems, recv_sems, srcs, lands, tok = _split_start(srcs, lands, make, "grads_start_" + group)
        self.sent[group] = (names, kinds, send_sems, recv_sems, srcs, lands, make)
        return tok[0, 0]

    def received(self, group, after):
        names, kinds, send_sems, recv_sems, srcs, lands, make = self.sent[group]
        srcs, lands = _split_wait(send_sems, recv_sems, srcs, lands, after, make, "grads_wait_" + group)
        return {n: (k, g, r) for n, k, g, r in zip(names, kinds, srcs, lands)}


SMALL_ROWS = ("ffn1_norm", "mix_norm", "ffn2_norm", "ple_norm", "gdn_norm", "q_norm", "k_norm", "a_log", "dt_bias",
              "rel_bias", "conv_w")


def _pack_small(vals):
    rows = []
    for n in SMALL_ROWS:
        v = vals[n]
        if n == "rel_bias":
            v = jnp.pad(v, ((0, 0), (0, 2 * LANES - N_REL)))
        elif n in ("a_log", "dt_bias"):
            v = _pad_lanes(v)
        rows.append(v.reshape(-1, LANES))
    packed = jnp.concatenate(rows, axis=0)
    return jnp.pad(packed, ((0, -packed.shape[0] % SUBLANES), (0, 0)))


def _unpack_small(packed, shapes):
    out, off = {}, 0
    for n in SMALL_ROWS:
        shp = shapes[n]
        if n == "rel_bias":
            out[n] = packed[off:off + 2 * HEADS].reshape(HEADS, 2 * LANES)[:, :N_REL]
            off += 2 * HEADS
        elif n in ("a_log", "dt_bias"):
            out[n] = packed[off:off + 1, :HEADS]
            off += 1
        else:
            r = int(np.prod(shp)) // LANES
            out[n] = packed[off:off + r].reshape(shp)
            off += r
    return out


WEIGHTS = ("ffn1_norm", "ffn1_w_gu", "ffn1_w_down", "mix_norm", "w_in", "conv_w", "a_log", "dt_bias", "gdn_norm",
           "q_norm", "k_norm", "rel_bias", "w_branch_a", "w_branch_b", "w_out", "ffn2_norm", "ffn2_w_gu",
           "ffn2_w_down", "ple_norm", "ple_gate", "ple_proj")


def kernel(x, p, ffn1_norm, ffn1_w_gu, ffn1_w_down, mix_norm, w_in, conv_w, a_log, dt_bias, gdn_norm, q_norm, k_norm, rel_bias, w_branch_a, w_branch_b, w_out, ffn2_norm, ffn2_w_gu, ffn2_w_down, ple_norm, ple_gate, ple_proj, loss_target, m_ffn1_norm, m_ffn1_w_gu, m_ffn1_w_down, m_mix_norm, m_w_in, m_conv_w, m_a_log, m_dt_bias, m_gdn_norm, m_q_norm, m_k_norm, m_rel_bias, m_w_branch_a, m_w_branch_b, m_w_out, m_ffn2_norm, m_ffn2_w_gu, m_ffn2_w_down, m_ple_norm, m_ple_gate, m_ple_proj, v_ffn1_norm, v_ffn1_w_gu, v_ffn1_w_down, v_mix_norm, v_w_in, v_conv_w, v_a_log, v_dt_bias, v_gdn_norm, v_q_norm, v_k_norm, v_rel_bias, v_w_branch_a, v_w_branch_b, v_w_out, v_ffn2_norm, v_ffn2_w_gu, v_ffn2_w_down, v_ple_norm, v_ple_gate, v_ple_proj):
    args = dict(locals())
    def layer0(v):
        return v[0] if v.ndim == 3 else v

    wts = {n: layer0(args[n]) for n in WEIGHTS}
    mom = {n: layer0(args["m_" + n]) for n in WEIGHTS}
    var = {n: layer0(args["v_" + n]) for n in WEIGHTS}
    x2d, p2d, tgt = x[0], p[0, 0], loss_target[0]
    my_index = _index(*_me())

    small = {n: wts[n] for n in SMALL_ROWS if n != "conv_w"}
    conv_shard = wts["conv_w"]
    conv_cols = conv_shard.shape[1]
    conv_packed = jnp.zeros((SUBLANES, N_DEV * conv_cols), F32)
    conv_packed = lax.dynamic_update_slice(conv_packed, jnp.pad(conv_shard, ((0, SUBLANES - CONV_K), (0, 0))),
                                           (0, my_index * conv_cols))
    small["conv_w"] = _all_reduce_small(conv_packed.reshape(-1, LANES), "conv_w_gather").reshape(SUBLANES, -1)[:CONV_K]

    fsdp = _Fsdp(wts, small["conv_w"][0, 0] * 0.0)

    loss, grad_x, gsmall = _local_step(x2d, p2d, tgt, small, fsdp)
    loss = lax.psum(loss, ("x", "y", "c"))

    outs_big, after = {}, grad_x
    for group in list(fsdp.sent):
        for n, (kind, grad, recv) in fsdp.received(group, after).items():
            if n == "w_in":
                g_in = _sum_w_in_windows(recv, grad)[:, :IN_SHARD]
                outs_big[n] = [g_in] + list(_adamw_small(wts[n], g_in, mom[n], var[n], "adamw_w_in"))
            else:
                outs_big[n] = _adamw_recv(recv, grad, kind, wts[n], mom[n], var[n], "adamw_" + n)
            after = outs_big[n][1]

    small_shapes = {n: (small[n].shape if n != "conv_w" else (CONV_K, N_DEV * conv_cols)) for n in SMALL_ROWS}
    gsum = _unpack_small(_all_reduce_small(_pack_small(gsmall), "small_grads_all_reduce"), small_shapes)
    gsum["conv_w"] = lax.dynamic_slice(gsum["conv_w"], (0, my_index * conv_cols), (CONV_K, conv_cols))
    rep = [n for n in SMALL_ROWS if n != "conv_w"]
    rep_shapes = {n: small_shapes[n] for n in rep}

    def pack_rep(vals):
        return _pack_small({**{n: vals[n] for n in rep}, "conv_w": jnp.zeros((CONV_K, LANES), F32)})

    def unpack_rep(packed):
        return _unpack_small(packed, {**rep_shapes, "conv_w": (CONV_K, LANES)})

    outs_small = [unpack_rep(o) for o in _adamw_small(pack_rep(wts), pack_rep(gsum), pack_rep(mom), pack_rep(var),
                                                      "adamw_replicated")]
    pad8 = functools.partial(jnp.pad, pad_width=((0, SUBLANES - CONV_K), (0, 0)))
    outs_conv = [o[:CONV_K] for o in _adamw_small(pad8(conv_shard), pad8(gsum["conv_w"]), pad8(mom["conv_w"]),
                                                   pad8(var["conv_w"]), "adamw_conv")]

    def leaf(kind, n):
        if n in BIG:
            return outs_big[n][kind][None]
        if n == "conv_w":
            return (gsum["conv_w"] if kind == 0 else outs_conv[kind - 1])[None]
        return (gsum[n] if kind == 0 else outs_small[kind - 1][n]).reshape(args[n].shape)

    result = [loss, grad_x[None]]
    for kind in range(4):
        result += [leaf(kind, n) for n in WEIGHTS]
    return tuple(result)
```

```python
import functools

import numpy as np
import jax
import jax.numpy as jnp
from jax import lax
from jax.experimental import pallas as pl
from jax.experimental.pallas import tpu as pltpu

F32 = jnp.float32
BF16 = jnp.bfloat16
HIGHEST = lax.Precision.HIGHEST
MESH = pl.DeviceIdType.MESH

D_MODEL = 2048
D_FF = 5632
HEADS = 8
HEAD_DIM = 128
HW = HEADS * HEAD_DIM
CHUNK = 64
LEFT_CHUNKS = 8
MAX_REL = 128
N_REL = (CHUNK - 1) + MAX_REL + 1
CONV_K = 4
EPS = 1e-6
NEG_INF = -1e30
N_DEV = 8
LANES = 128
SUBLANES = 8
VMEM_LIMIT = 56 * 1024 * 1024

MATMUL_WHOLE_K = 2048

ATT_QB = 256
ATT_KW = ATT_QB + LEFT_CHUNKS * CHUNK
ATT_PAD = LEFT_CHUNKS * CHUNK
GDN_CB = 8
GDN_GROUP = 8

ADAM_LR = 0.001
ADAM_B1 = 0.9
ADAM_B2 = 0.999
ADAM_EPS = 1e-08
ADAM_WD = 0.01
ADAM_STEP = 10

IN_QZ = 3 * HW + HW
IN_AB0 = IN_QZ
IN_QKVB0 = IN_AB0 + 2 * HEADS
IN_GG0 = IN_QKVB0 + 3 * HW
IN_COLS = IN_GG0 + 2 * D_MODEL

BIG = ("ffn1_w_gu", "ffn1_w_down", "w_in", "w_branch_a", "w_branch_b", "w_out",
       "ffn2_w_gu", "ffn2_w_down", "ple_gate", "ple_proj")
COL_SHARDED = ("ffn1_w_gu", "w_in", "w_branch_a", "w_branch_b", "ffn2_w_gu", "ple_proj")


def _params(semantics=None, **kw):
    return pltpu.CompilerParams(dimension_semantics=semantics, vmem_limit_bytes=VMEM_LIMIT, **kw)


def _pick(n, cands):
    for c in cands:
        if n % c == 0:
            return c
    return n


def _matmul(a, b, mode, out_dtype, name):
    if mode == "nn":
        (m, k), (k2, n) = a.shape, b.shape
    elif mode == "nt":
        (m, k), (n, k2) = a.shape, b.shape
    else:
        (k, m), (k2, n) = a.shape, b.shape
    assert k == k2, (a.shape, b.shape, mode)
    tm = _pick(m, (1024, 512, 256, 128))
    tn = _pick(n, (1024, 512, 256, 128))
    tk = k if k <= MATMUL_WHOLE_K else _pick(k, (2816, 2048, 1536, 1024, 512, 256, 128))
    nk = k // tk
    if mode == "nn":
        a_spec = pl.BlockSpec((tm, tk), lambda i, j, kk: (i, kk))
        b_spec = pl.BlockSpec((tk, tn), lambda i, j, kk: (kk, j))
        dims = (((1,), (0,)), ((), ()))
    elif mode == "nt":
        a_spec = pl.BlockSpec((tm, tk), lambda i, j, kk: (i, kk))
        b_spec = pl.BlockSpec((tn, tk), lambda i, j, kk: (j, kk))
        dims = (((1,), (1,)), ((), ()))
    else:
        a_spec = pl.BlockSpec((tk, tm), lambda i, j, kk: (kk, i))
        b_spec = pl.BlockSpec((tk, tn), lambda i, j, kk: (kk, j))
        dims = (((0,), (0,)), ((), ()))

    def body(a_ref, b_ref, o_ref, *acc):
        prod = lax.dot_general(a_ref[...].astype(BF16), b_ref[...].astype(BF16), dims, preferred_element_type=F32)
        if nk == 1:
            o_ref[...] = prod.astype(o_ref.dtype)
            return
        acc_ref, kk = acc[0], pl.program_id(2)

        @pl.when(kk == 0)
        def _():
            acc_ref[...] = prod

        @pl.when((kk > 0) & (kk < nk - 1))
        def _():
            acc_ref[...] += prod

        @pl.when(kk == nk - 1)
        def _():
            o_ref[...] = (acc_ref[...] + prod).astype(o_ref.dtype)

    return pl.pallas_call(
        body, name=name,
        out_shape=jax.ShapeDtypeStruct((m, n), out_dtype),
        grid=(m // tm, n // tn, nk),
        in_specs=[a_spec, b_spec],
        out_specs=pl.BlockSpec((tm, tn), lambda i, j, kk: (i, j)),
        scratch_shapes=[pltpu.VMEM((tm, tn), F32)] if nk > 1 else [],
        compiler_params=_params(("parallel", "parallel", "arbitrary")),
    )(a, b)


def _rows(fn, row_ins, consts, row_outs, acc_outs, tile, name):
    t_rows = row_ins[0][0].shape[0]
    tile = min(tile, t_rows)
    assert t_rows % tile == 0 and tile % SUBLANES == 0
    n = t_rows // tile
    per = tile // SUBLANES
    last8 = t_rows // SUBLANES - 1
    in_specs = []
    for arr, kind in row_ins:
        c = arr.shape[1]
        if kind == "t":
            in_specs.append(pl.BlockSpec((tile, c), lambda i: (i, 0)))
        elif kind == "p":
            in_specs.append(pl.BlockSpec((SUBLANES, c), lambda i: (jnp.maximum(i * per - 1, 0), 0)))
        else:
            in_specs.append(pl.BlockSpec((SUBLANES, c), lambda i: (jnp.minimum((i + 1) * per, last8), 0)))
    for arr in consts:
        in_specs.append(pl.BlockSpec(arr.shape, lambda i, nd=arr.ndim: (0,) * nd))
    out_shape = [jax.ShapeDtypeStruct((t_rows, c), dt) for c, dt in row_outs]
    out_specs = [pl.BlockSpec((tile, c), lambda i: (i, 0)) for c, _ in row_outs]
    for shp in acc_outs:
        out_shape.append(jax.ShapeDtypeStruct(shp, F32))
        out_specs.append(pl.BlockSpec(shp, lambda i, nd=len(shp): (0,) * nd))
    n_in = len(row_ins) + len(consts)
    n_row_out = len(row_outs)

    def body(*refs):
        i = pl.program_id(0)
        vals = [r[...].astype(F32) for r in refs[:len(row_ins)]]
        res = fn(i, n, *vals, *refs[len(row_ins):n_in])
        outs = refs[n_in:]
        for r, v in zip(outs[:n_row_out], res[:n_row_out]):
            r[...] = v.astype(r.dtype)
        if acc_outs:
            @pl.when(i == 0)
            def _():
                for r in outs[n_row_out:]:
                    r[...] = jnp.zeros_like(r)

            for r, v in zip(outs[n_row_out:], res[n_row_out:]):
                r[...] += v

    res = pl.pallas_call(
        body, name=name, out_shape=out_shape, grid=(n,), in_specs=in_specs, out_specs=out_specs,
        compiler_params=_params(("arbitrary",) if acc_outs else ("parallel",)),
    )(*[a for a, _ in row_ins], *consts)
    return res


def _rms(x, w):
    return x * lax.rsqrt(jnp.mean(x * x, axis=-1, keepdims=True) + EPS) * w


def _l2n(x):
    return x * lax.rsqrt(jnp.sum(x * x, axis=-1, keepdims=True) + EPS)


def _sigmoid(x):
    return 1.0 / (1.0 + jnp.exp(-x))


def _silu(x):
    return x * _sigmoid(x)


def _softplus(x):
    return jnp.maximum(x, 0.0) + jnp.log(1.0 + jnp.exp(-jnp.abs(x)))


def _heads(fn, *xs):
    nh = xs[0].shape[1] // HEAD_DIM
    return jnp.concatenate(
        [fn(*[x[:, h * HEAD_DIM:(h + 1) * HEAD_DIM] for x in xs]) for h in range(nh)], axis=1)


def _colsum(x):
    return jnp.sum(x, axis=0, keepdims=True)


def _gated_norm(o, z, w):
    return _heads(lambda oh, zh: _rms(oh, w) * _silu(zh), o, z)


def _mix(gg, ta, tb):
    return _sigmoid(gg[:, :D_MODEL]) * ta + _sigmoid(gg[:, D_MODEL:]) * tb


def _gdn_post(y):
    a = _silu(y)
    q = _heads(lambda v: _l2n(v) * (HEAD_DIM ** -0.5), a[:, :HW])
    k = _heads(_l2n, a[:, HW:2 * HW])
    return q, k, a[:, 2 * HW:]


NN = (((1,), (0,)), ((), ()))
NT = (((1,), (1,)), ((), ()))
TN = (((0,), (0,)), ((), ()))


def _dg(a, b, dims):
    return lax.dot_general(a, b, dims, preferred_element_type=F32)


def _split2(x):
    hi = x.astype(BF16)
    return hi, (x - hi.astype(F32)).astype(BF16)


def _split3(x):
    hi = x.astype(BF16)
    r = x - hi.astype(F32)
    mid = r.astype(BF16)
    return hi, mid, (r - mid.astype(F32)).astype(BF16)


def _dg3(a, b, dims):
    ah, al = _split2(a)
    bh, bl = _split2(b)
    return _dg(ah, bh, dims) + (_dg(ah, bl, dims) + _dg(al, bh, dims))


BNN = (((2,), (1,)), ((0,), (0,)))
BNT = (((2,), (2,)), ((0,), (0,)))
BTN = (((1,), (1,)), ((0,), (0,)))


@jax.custom_vjp
def _mm3(a, b):
    return _dg3(a, b, BNN)


_mm3.defvjp(lambda a, b: (_dg3(a, b, BNN), (a, b)),
            lambda res, g: (_dg3(g, res[1], BNT), _dg3(res[0], g, BTN)))


def _xm(x, m, dims):
    mb = m.astype(BF16)
    parts = _split3(x)
    return _dg(parts[0], mb, dims) + (_dg(parts[1], mb, dims) + _dg(parts[2], mb, dims))


def _mx(m, x, dims):
    mb = m.astype(BF16)
    parts = _split3(x)
    return _dg(mb, parts[0], dims) + (_dg(mb, parts[1], dims) + _dg(mb, parts[2], dims))


@jax.custom_vjp
def _times_const(x, m):
    return _xm(x, m, NN)


_times_const.defvjp(lambda x, m: (_xm(x, m, NN), m),
                    lambda m, g: (_xm(g, m, NT), jnp.zeros_like(m)))


@jax.custom_vjp
def _const_times(m, x):
    return _mx(m, x, NN)


_const_times.defvjp(lambda m, x: (_mx(m, x, NN), m),
                    lambda m, g: (jnp.zeros_like(m), _mx(m, g, TN)))


@jax.custom_vjp
def _lane_mean_cols(x, avg):
    return _mx(avg, x, BNT)


_lane_mean_cols.defvjp(lambda x, avg: (_mx(avg, x, BNT), avg),
                       lambda avg, g: (_xm(g, avg, BTN), jnp.zeros_like(avg)))


def _gdn_gates(ab, alog, dtb, e_g, e_b):
    t = ab.shape[0]
    g = -jnp.exp(alog) * _softplus(ab + dtb)
    beta = _sigmoid(ab)
    ri = lax.broadcasted_iota(jnp.int32, (t, t), 0)
    ci = lax.broadcasted_iota(jnp.int32, (t, t), 1)
    shift = CHUNK.bit_length() - 1
    same = jnp.right_shift(ri, shift) == jnp.right_shift(ci, shift)
    tril = jnp.where(same & (ri >= ci), 1.0, 0.0).astype(F32)
    gc = _const_times(tril, g)
    return _times_const(gc, e_g), _times_const(beta, e_b)


def _shift_down(x, halo, s, i):
    if s == 0:
        return x
    halo = jnp.where(i == 0, 0.0, halo)
    xr = pltpu.roll(x, s, 0)
    hr = pltpu.roll(halo, s, 0)
    row = lax.broadcasted_iota(jnp.int32, (SUBLANES, x.shape[1]), 0)
    top = jnp.where(row < s, hr, xr[:SUBLANES])
    return jnp.concatenate([top, xr[SUBLANES:]], axis=0)


def _shift_up(x, halo, s, i, n):
    if s == 0:
        return x
    t = x.shape[0]
    halo = jnp.where(i == n - 1, 0.0, halo)
    xr = pltpu.roll(x, t - s, 0)
    hr = pltpu.roll(halo, SUBLANES - s, 0)
    row = lax.broadcasted_iota(jnp.int32, (SUBLANES, x.shape[1]), 0)
    bot = jnp.where(row >= SUBLANES - s, hr, xr[t - SUBLANES:])
    return jnp.concatenate([xr[:t - SUBLANES], bot], axis=0)


def _conv(pa, prev, cw_ref, i):
    y = pa * cw_ref[CONV_K - 1:CONV_K, :]
    for j in range(CONV_K - 1):
        y = y + _shift_down(pa, prev, CONV_K - 1 - j, i) * cw_ref[j:j + 1, :]
    return y


def _dot_nt(a, b, precision=None):
    return lax.dot_general(a, b, (((1,), (1,)), ((), ())), precision=precision, preferred_element_type=F32)


def _dot_tn(a, b, precision=None):
    return lax.dot_general(a, b, (((0,), (0,)), ((), ())), precision=precision, preferred_element_type=F32)


def _dot(a, b, precision=None):
    return jnp.dot(a, b, precision=precision, preferred_element_type=F32)


def _bf(x):
    return x.astype(BF16)


def _neumann_inverse(lmat):
    nb, c, _ = lmat.shape
    ri = lax.broadcasted_iota(jnp.int32, (nb, c, c), 1)
    ci = lax.broadcasted_iota(jnp.int32, (nb, c, c), 2)
    pw = -lmat
    inv = jnp.where(ri == ci, 1.0, 0.0).astype(F32) + pw
    for _ in range(5):
        pw = _mm3(pw, pw)
        inv = inv + _mm3(inv, pw)
    return inv


@jax.custom_vjp
def _unit_lower_inverse(lmat):
    return _neumann_inverse(lmat)


def _unit_lower_inverse_fwd(lmat):
    inv = _neumann_inverse(lmat)
    return inv, inv


def _unit_lower_inverse_bwd(inv, g):
    return (-_dg3(_dg3(inv, g, BTN), inv, BNT),)


_unit_lower_inverse.defvjp(_unit_lower_inverse_fwd, _unit_lower_inverse_bwd)


def _gdn_chunk(q, k, v, gc, bb):
    nb, c, _ = q.shape
    ri = lax.broadcasted_iota(jnp.int32, (nb, c, c), 1)
    ci = lax.broadcasted_iota(jnp.int32, (nb, c, c), 2)
    incl = ri >= ci
    strict = ri > ci
    g_row = gc[:, :, :c]
    g_col = _lane_mean_cols(gc, jnp.full((nb, c, LANES), 1.0 / LANES, F32))
    decay = jnp.where(incl, jnp.exp(jnp.where(incl, g_row - g_col, 0.0)), 0.0)
    kb = k * bb
    lmat = jnp.where(strict, _dg(_bf(kb), _bf(k), BNT) * decay, 0.0)
    inv = _unit_lower_inverse(lmat)
    egc = jnp.exp(gc)
    u = _mm3(inv, v * bb)
    w = _mm3(inv, kb * egc)
    aqk = _dg(_bf(q), _bf(k), BNT) * decay
    last = lax.broadcasted_iota(jnp.int32, (nb, c, LANES), 1) == c - 1
    tot = jnp.sum(jnp.where(last, gc, 0.0), axis=1, keepdims=True)
    k_tail = k * jnp.exp(tot - gc)
    tail = jnp.broadcast_to(jnp.exp(tot), (nb, SUBLANES, LANES))
    return u, w, aqk, q * egc, k_tail, tail


def _gdn_intra(qn, kn, vv, g_b, beta_b):
    t_rows = qn.shape[0]
    nc = t_rows // CHUNK
    cb = min(GDN_CB, nc)
    rows = cb * CHUNK
    col = pl.BlockSpec((rows, HEAD_DIM), lambda h, b: (b, h))

    def body(q_ref, k_ref, v_ref, g_ref, b_ref, u_ref, w_ref, a_ref, qd_ref, kt_ref, tl_ref):
        def group(gi, carry):
            r = pl.ds(pl.multiple_of(gi * (grp * CHUNK), grp * CHUNK), grp * CHUNK)
            ins = [ref[r, :].reshape(grp, CHUNK, HEAD_DIM) for ref in (q_ref, k_ref, v_ref, g_ref, b_ref)]
            u, w, aqk, qd, kt, tl = _gdn_chunk(*ins)
            for ref, val in ((u_ref, u), (w_ref, w), (qd_ref, qd), (kt_ref, kt)):
                ref[r, :] = val.reshape(grp * CHUNK, HEAD_DIM)
            a_ref[0, r, :] = aqk.reshape(grp * CHUNK, CHUNK)
            tl_ref[0, pl.ds(gi * grp, grp)] = tl
            return carry

        grp = min(GDN_GROUP, cb)
        lax.fori_loop(0, cb // grp, group, 0)

    full = jax.ShapeDtypeStruct((t_rows, HW), F32)
    return pl.pallas_call(
        body, name="gdn_intra_fwd",
        out_shape=[full, full, jax.ShapeDtypeStruct((HEADS, t_rows, CHUNK), F32), full, full,
                   jax.ShapeDtypeStruct((HEADS, nc, SUBLANES, LANES), F32)],
        grid=(HEADS, nc // cb),
        in_specs=[col] * 5,
        out_specs=[col, col, pl.BlockSpec((1, rows, CHUNK), lambda h, b: (h, b, 0)), col, col,
                   pl.BlockSpec((1, cb, SUBLANES, LANES), lambda h, b: (h, b, 0, 0))],
        compiler_params=_params(("parallel", "parallel")),
    )(qn, kn, vv, g_b, beta_b)


def _gdn_intra_bwd(qn, kn, vv, g_b, beta_b, du, dw, da, dqd, dkt, dtl):
    t_rows = qn.shape[0]
    nc = t_rows // CHUNK
    cb = min(GDN_CB, nc)
    rows = cb * CHUNK
    col = pl.BlockSpec((rows, HEAD_DIM), lambda h, b: (b, h))
    a_spec = pl.BlockSpec((1, rows, CHUNK), lambda h, b: (h, b, 0))
    tl_spec = pl.BlockSpec((1, cb, SUBLANES, LANES), lambda h, b: (h, b, 0, 0))

    def body(q_ref, k_ref, v_ref, g_ref, b_ref, du_ref, dw_ref, da_ref, dqd_ref, dkt_ref, dtl_ref,
             dq_ref, dk_ref, dv_ref, dg_ref, db_ref):
        def group(gi, carry):
            r = pl.ds(pl.multiple_of(gi * (grp * CHUNK), grp * CHUNK), grp * CHUNK)
            wide = (grp, CHUNK, HEAD_DIM)
            ins = [ref[r, :].reshape(wide) for ref in (q_ref, k_ref, v_ref, g_ref, b_ref)]
            cts = (du_ref[r, :].reshape(wide), dw_ref[r, :].reshape(wide),
                   da_ref[0, r, :].reshape(grp, CHUNK, CHUNK), dqd_ref[r, :].reshape(wide),
                   dkt_ref[r, :].reshape(wide), dtl_ref[0, pl.ds(gi * grp, grp)])
            grads = jax.vjp(_gdn_chunk, *ins)[1](cts)
            for ref, val in zip((dq_ref, dk_ref, dv_ref, dg_ref, db_ref), grads):
                ref[r, :] = val.reshape(grp * CHUNK, HEAD_DIM)
            return carry

        grp = min(GDN_GROUP, cb)
        lax.fori_loop(0, cb // grp, group, 0)

    full = jax.ShapeDtypeStruct((t_rows, HW), F32)
    return pl.pallas_call(
        body, name="gdn_intra_bwd",
        out_shape=[full] * 5,
        grid=(HEADS, nc // cb),
        in_specs=[col] * 7 + [a_spec, col, col, tl_spec],
        out_specs=[col] * 5,
        compiler_params=_params(("parallel", "parallel")),
    )(qn, kn, vv, g_b, beta_b, du, dw, da, dqd, dkt, dtl)


def _head_cols(h):
    return slice(h * HEAD_DIM, (h + 1) * HEAD_DIM)


def _gdn_scan(u, w, aqk, qd, kt, tl):
    t_rows = u.shape[0]
    nc = t_rows // CHUNK
    cb = min(GDN_CB, nc)
    rows = cb * CHUNK
    wide = pl.BlockSpec((rows, HW), lambda b: (b, 0))

    def body(u_ref, w_ref, a_ref, qd_ref, kt_ref, tl_ref, o_ref, s_out_ref, s_ref):
        @pl.when(pl.program_id(0) == 0)
        def _():
            s_ref[...] = jnp.zeros_like(s_ref)

        def chunk(ci, carry):
            r = pl.ds(pl.multiple_of(ci * CHUNK, CHUNK), CHUNK)
            for h in range(HEADS):
                hc = _head_cols(h)
                s = s_ref[h]
                s_out_ref[ci, h] = s
                sb = _bf(s)
                vn = u_ref[r, hc] - _dot(_bf(w_ref[r, hc]), sb)
                vnb = _bf(vn)
                o_ref[r, hc] = _dot(_bf(qd_ref[r, hc]), sb) + _dot(_bf(a_ref[h, r, :]), vnb)
                s_ref[h] = s * tl_ref[h, ci, 0:1, :] + _dot_tn(_bf(kt_ref[r, hc]), vnb)
            return carry

        lax.fori_loop(0, cb, chunk, 0)

    return pl.pallas_call(
        body, name="gdn_scan_fwd",
        out_shape=[jax.ShapeDtypeStruct((t_rows, HW), F32),
                   jax.ShapeDtypeStruct((nc, HEADS, HEAD_DIM, HEAD_DIM), F32)],
        grid=(nc // cb,),
        in_specs=[wide, wide, pl.BlockSpec((HEADS, rows, CHUNK), lambda b: (0, b, 0)), wide, wide,
                  pl.BlockSpec((HEADS, cb, SUBLANES, LANES), lambda b: (0, b, 0, 0))],
        out_specs=[wide, pl.BlockSpec((cb, HEADS, HEAD_DIM, HEAD_DIM), lambda b: (b, 0, 0, 0))],
        scratch_shapes=[pltpu.VMEM((HEADS, HEAD_DIM, HEAD_DIM), F32)],
        compiler_params=_params(("arbitrary",)),
    )(u, w, aqk, qd, kt, tl)


def _gdn_scan_bwd(do, u, w, aqk, qd, kt, tl, states):
    t_rows = u.shape[0]
    nc = t_rows // CHUNK
    cb = min(GDN_CB, nc)
    rows = cb * CHUNK
    nb = nc // cb
    wide = pl.BlockSpec((rows, HW), lambda b: (nb - 1 - b, 0))
    a_spec = pl.BlockSpec((HEADS, rows, CHUNK), lambda b: (0, nb - 1 - b, 0))
    tl_spec = pl.BlockSpec((HEADS, cb, SUBLANES, LANES), lambda b: (0, nb - 1 - b, 0, 0))

    def body(do_ref, u_ref, w_ref, a_ref, qd_ref, kt_ref, tl_ref, s_in_ref,
             du_ref, dw_ref, da_ref, dqd_ref, dkt_ref, dtl_ref, ds_ref):
        @pl.when(pl.program_id(0) == 0)
        def _():
            ds_ref[...] = jnp.zeros_like(ds_ref)

        row0 = lax.broadcasted_iota(jnp.int32, (SUBLANES, LANES), 0) == 0

        def chunk(step, carry):
            ci = cb - 1 - step
            r = pl.ds(pl.multiple_of(ci * CHUNK, CHUNK), CHUNK)
            for h in range(HEADS):
                hc = _head_cols(h)
                s = s_in_ref[ci, h]
                ds_next = ds_ref[h]
                sb, dsb = _bf(s), _bf(ds_next)
                wb, ab, ktb, qdb = _bf(w_ref[r, hc]), _bf(a_ref[h, r, :]), _bf(kt_ref[r, hc]), _bf(qd_ref[r, hc])
                dob = _bf(do_ref[r, hc])
                vn = u_ref[r, hc] - _dot(wb, sb)
                vnb = _bf(vn)
                dvn = _dot_tn(ab, dob) + _dot(ktb, dsb)
                dvnb = _bf(dvn)
                du_ref[r, hc] = dvn
                dw_ref[r, hc] = -_dot_nt(dvnb, sb)
                da_ref[h, r, :] = _dot_nt(dob, vnb)
                dqd_ref[r, hc] = _dot_nt(dob, sb)
                dkt_ref[r, hc] = _dot_nt(vnb, dsb)
                dtl_ref[h, ci] = jnp.where(row0, _colsum(s * ds_next), 0.0)
                ds_ref[h] = _dot_tn(qdb, dob) + ds_next * tl_ref[h, ci, 0:1, :] - _dot_tn(wb, dvnb)
            return carry

        lax.fori_loop(0, cb, chunk, 0)

    full = jax.ShapeDtypeStruct((t_rows, HW), F32)
    return pl.pallas_call(
        body, name="gdn_scan_bwd",
        out_shape=[full, full, jax.ShapeDtypeStruct((HEADS, t_rows, CHUNK), F32), full, full,
                   jax.ShapeDtypeStruct((HEADS, nc, SUBLANES, LANES), F32)],
        grid=(nb,),
        in_specs=[wide, wide, wide, a_spec, wide, wide, tl_spec,
                  pl.BlockSpec((cb, HEADS, HEAD_DIM, HEAD_DIM), lambda b: (nb - 1 - b, 0, 0, 0))],
        out_specs=[wide, wide, a_spec, wide, wide, tl_spec],
        scratch_shapes=[pltpu.VMEM((HEADS, HEAD_DIM, HEAD_DIM), F32)],
        compiler_params=_params(("arbitrary",)),
    )(do, u, w, aqk, qd, kt, tl, states)


def _att_profile_index():
    j = lax.broadcasted_iota(jnp.int32, (SUBLANES, ATT_KW), 1)
    return jnp.clip(ATT_PAD - j, -(CHUNK - 1), MAX_REL) + (CHUNK - 1)


def _att_far_back():
    qi = lax.broadcasted_iota(jnp.int32, (ATT_QB, ATT_KW), 0)
    kj = lax.broadcasted_iota(jnp.int32, (ATT_QB, ATT_KW), 1)
    return kj < qi


def _rotate_rows(x, forward):
    rows, lanes = x.shape
    row = lax.broadcasted_iota(jnp.int32, x.shape, 0)
    for bit in range(rows.bit_length() - 1):
        amount = (1 << bit) if forward else lanes - (1 << bit)
        x = jnp.where(jnp.bitwise_and(jnp.right_shift(row, bit), 1) == 1, pltpu.roll(x, amount, 1), x)
    return x


def _att_in_band():
    qi = lax.broadcasted_iota(jnp.int32, (ATT_QB, ATT_KW), 0)
    kj = lax.broadcasted_iota(jnp.int32, (ATT_QB, ATT_KW), 1)
    shift = CHUNK.bit_length() - 1
    qc = jnp.right_shift(qi, shift)
    kc = jnp.right_shift(kj, shift) - LEFT_CHUNKS
    return (kc <= qc) & (kc >= qc - LEFT_CHUNKS)


def _att_valid(b):
    kj = lax.broadcasted_iota(jnp.int32, (1, ATT_KW), 1)
    return jnp.where(kj + b * ATT_QB >= ATT_PAD, 0.0, NEG_INF)


def _rms_parts(x, w):
    r = lax.rsqrt(jnp.mean(x * x, axis=-1, keepdims=True) + EPS)
    xn = x * r
    return xn * w, xn, r


def _rms_bwd(dy, xn, r, w):
    dxn = dy * w
    dx = r * (dxn - xn * jnp.mean(dxn * xn, axis=-1, keepdims=True))
    return dx, _colsum(dy * xn)


def _att_probs(qb, kb, bias, before_start):
    s = _dot_nt(qb, kb) * (HEAD_DIM ** -0.5) + bias + before_start
    e = jnp.exp(s - jnp.max(s, axis=-1, keepdims=True))
    return e * (1.0 / jnp.sum(e, axis=-1, keepdims=True))


def _att_specs():
    q_spec = pl.BlockSpec((ATT_QB, HEAD_DIM), lambda h, b: (b, h))
    k_specs = [pl.BlockSpec((ATT_QB, HEAD_DIM), lambda h, b, j=j: (b + j, HEADS + h)) for j in range(3)]
    v_specs = [pl.BlockSpec((ATT_QB, HEAD_DIM), lambda h, b, j=j: (b + j, 2 * HEADS + h)) for j in range(3)]
    w_spec = pl.BlockSpec((1, HEAD_DIM), lambda h, b: (0, 0))
    smem = pl.BlockSpec(memory_space=pltpu.SMEM)
    return q_spec, k_specs, v_specs, w_spec, smem


BIAS_SPEC = pl.BlockSpec((1, ATT_QB, ATT_KW), lambda h, b: (h, 0, 0))


def _expand_rel_bias(rel):
    def body(rel_ref, bias_ref):
        h = pl.program_id(0)
        idx = _att_profile_index()

        def fill(r, acc):
            return jnp.where(idx == r, rel_ref[h, r], acc)

        profile = lax.fori_loop(0, N_REL, fill, jnp.zeros((SUBLANES, ATT_KW), F32))
        table = _rotate_rows(jnp.concatenate([profile] * (ATT_QB // SUBLANES), axis=0), True)
        table = jnp.where(_att_far_back(), rel_ref[h, N_REL - 1], table)
        bias_ref[0] = jnp.where(_att_in_band(), table, NEG_INF)

    return pl.pallas_call(
        body, name="rel_bias_expand",
        out_shape=jax.ShapeDtypeStruct((HEADS, ATT_QB, ATT_KW), F32), grid=(HEADS,),
        in_specs=[pl.BlockSpec(memory_space=pltpu.SMEM)],
        out_specs=pl.BlockSpec((1, ATT_QB, ATT_KW), lambda h: (h, 0, 0)),
        compiler_params=_params(("parallel",)),
    )(rel)


def _attention(pb, pbp, qw, kw, bias):
    t_rows = pb.shape[0]
    q_spec, k_specs, v_specs, w_spec, _ = _att_specs()

    def body(q_ref, k0, k1, k2, v0, v1, v2, qw_ref, kw_ref, bias_ref, o_ref):
        b = pl.program_id(1)
        kwin = jnp.concatenate([k0[...], k1[...], k2[...]], axis=0)
        vwin = jnp.concatenate([v0[...], v1[...], v2[...]], axis=0)
        q = _rms(q_ref[...], qw_ref[...])
        k = _rms(kwin, kw_ref[...])
        p = _att_probs(_bf(q), _bf(k), bias_ref[0], _att_valid(b))
        o_ref[...] = _dot(_bf(p), _bf(vwin)).astype(o_ref.dtype)

    return pl.pallas_call(
        body, name="band_attention_fwd",
        out_shape=jax.ShapeDtypeStruct((t_rows, HW), BF16),
        grid=(HEADS, t_rows // ATT_QB),
        in_specs=[q_spec] + k_specs + v_specs + [w_spec, w_spec, BIAS_SPEC],
        out_specs=pl.BlockSpec((ATT_QB, HEAD_DIM), lambda h, b: (b, h)),
        compiler_params=_params(("parallel", "arbitrary")),
    )(pb, pbp, pbp, pbp, pbp, pbp, pbp, qw, kw, bias)


def _attention_bwd(pb, pbp, qw, kw, bias, dyb):
    t_rows = pb.shape[0]
    nb = t_rows // ATT_QB
    q_spec, k_specs, v_specs, w_spec, smem = _att_specs()
    pad_rows = t_rows + ATT_PAD
    acc_spec = pl.BlockSpec((pad_rows, HEAD_DIM), lambda h, b: (0, h))

    def body(q_ref, k0, k1, k2, v0, v1, v2, qw_ref, kw_ref, bias_ref, do_ref,
             dq_ref, dk_ref, dv_ref, dqw_ref, dkw_ref, drel_ref, dbias_ref):
        h, b = pl.program_id(0), pl.program_id(1)

        @pl.when(b == 0)
        def _():
            dbias_ref[...] = jnp.zeros_like(dbias_ref)
            dk_ref[...] = jnp.zeros_like(dk_ref)
            dv_ref[...] = jnp.zeros_like(dv_ref)

        @pl.when((b == 0) & (h == 0))
        def _():
            dqw_ref[...] = jnp.zeros_like(dqw_ref)
            dkw_ref[...] = jnp.zeros_like(dkw_ref)

        kwin = jnp.concatenate([k0[...], k1[...], k2[...]], axis=0)
        vwin = jnp.concatenate([v0[...], v1[...], v2[...]], axis=0)
        scale = HEAD_DIM ** -0.5
        qw_, kw_ = qw_ref[...], kw_ref[...]
        q, qn, rq = _rms_parts(q_ref[...], qw_)
        k, kn, rk = _rms_parts(kwin, kw_)
        qb, kb, dob = _bf(q), _bf(k), _bf(do_ref[...])
        p = _att_probs(qb, kb, bias_ref[0], _att_valid(b))
        dp = _dot_nt(dob, _bf(vwin))
        ds = p * (dp - jnp.sum(p * dp, axis=-1, keepdims=True))
        dbias_ref[...] += ds
        ds = _bf(ds)
        dq, dqw = _rms_bwd(_dot(ds, kb) * scale, qn, rq, qw_)
        dk, dkw = _rms_bwd(_dot_tn(ds, qb) * scale, kn, rk, kw_)
        dq_ref[...] = dq.astype(dq_ref.dtype)
        win = pl.ds(pl.multiple_of(b * ATT_QB, ATT_QB), ATT_KW)
        dk_ref[win, :] += dk
        dv_ref[win, :] += _dot_tn(_bf(p), dob)
        dqw_ref[...] += dqw
        dkw_ref[...] += dkw

        @pl.when(b == nb - 1)
        def _():
            tot, far = dbias_ref[...], _att_far_back()
            far_sum = jnp.sum(jnp.where(far, tot, 0.0))
            per_offset = _colsum(_rotate_rows(jnp.where(far, 0.0, tot), False))
            idx = _att_profile_index()
            first_row = lax.broadcasted_iota(jnp.int32, idx.shape, 0) == 0
            spread = jnp.where(first_row, per_offset, 0.0)

            def reduce(r, carry):
                drel_ref[h, r] = jnp.sum(jnp.where(idx == r, spread, 0.0)) + jnp.where(r == N_REL - 1, far_sum, 0.0)
                return carry

            lax.fori_loop(0, N_REL, reduce, 0)

    return pl.pallas_call(
        body, name="band_attention_bwd",
        out_shape=[jax.ShapeDtypeStruct((t_rows, HW), BF16),
                   jax.ShapeDtypeStruct((pad_rows, HW), F32), jax.ShapeDtypeStruct((pad_rows, HW), F32),
                   jax.ShapeDtypeStruct((1, HEAD_DIM), F32), jax.ShapeDtypeStruct((1, HEAD_DIM), F32),
                   jax.ShapeDtypeStruct((HEADS, N_REL), F32)],
        grid=(HEADS, nb),
        in_specs=[q_spec] + k_specs + v_specs + [w_spec, w_spec, BIAS_SPEC, q_spec],
        out_specs=[q_spec, acc_spec, acc_spec, w_spec, w_spec, smem],
        scratch_shapes=[pltpu.VMEM((ATT_QB, ATT_KW), F32)],
        compiler_params=_params(("arbitrary", "arbitrary")),
    )(pb, pbp, pbp, pbp, pbp, pbp, pbp, qw, kw, bias, dyb)


def _me():
    return lax.axis_index("x"), lax.axis_index("y"), lax.axis_index("c")


def _index(x, y, c):
    return 4 * x + 2 * y + c


HBM_SPEC = pl.BlockSpec(memory_space=pl.ANY)


def _block(ref, kind, d, r, c):
    if kind == "all":
        return ref
    if kind == "rows":
        return ref.at[pl.ds(d * r, r), :]
    if kind == "win":
        return ref.at[:, pl.ds(d * WIN_STEP, c)]
    return ref.at[:, pl.ds(d * c, c)]


def _all_gather(shards, kinds, n_gather):
    n = len(shards)

    def body(*refs):
        x_refs, out_refs = refs[:n], refs[n:2 * n]
        send_sems, recv_sems, local_sems = refs[2 * n:]
        x, y, c = _me()
        me, sibling = (x, y, c), (x, y, 1 - c)
        chips = [(1 - x, y), (x, 1 - y), (1 - x, 1 - y)]

        def copy(i, k, blk, to, src=None):
            r_, c_ = shards[i].shape
            dst = _block(out_refs[i], kinds[i], _index(*blk), r_, c_)
            return pltpu.make_async_remote_copy(
                src_ref=dst if src is None else src, dst_ref=dst,
                send_sem=send_sems.at[i, k], recv_sem=recv_sems.at[i, k], device_id=to, device_id_type=MESH)

        sends, local = [], []
        for i in range(n):
            r_, c_ = shards[i].shape
            mine = pltpu.make_async_copy(x_refs[i], _block(out_refs[i], kinds[i], _index(*me), r_, c_),
                                         local_sems.at[i])
            mine.start()
            local.append(mine)
            if i >= n_gather:
                continue
            first = [copy(i, 0, me, sibling, src=x_refs[i])]
            first += [copy(i, 1 + j, me, (*chip, c), src=x_refs[i]) for j, chip in enumerate(chips)]
            for cp in first:
                cp.start()
            sends += first
        for i in range(n_gather):
            for j, chip in enumerate(chips):
                copy(i, 1 + j, (*chip, c), me).wait_recv()
                passed = copy(i, 4 + j, (*chip, c), sibling)
                passed.start()
                sends.append(passed)
        for i in range(n_gather):
            copy(i, 0, sibling, me).wait_recv()
            for j, chip in enumerate(chips):
                copy(i, 4 + j, (*chip, 1 - c), me).wait_recv()
        for cp in sends:
            cp.wait_send()
        for cp in local:
            cp.wait()

    def full_shape(s, kind):
        r_, c_ = s.shape
        return (N_DEV * r_, c_) if kind == "rows" else (r_, N_DEV * c_)

    return pl.pallas_call(
        body, name="weights_all_gather",
        out_shape=[jax.ShapeDtypeStruct(full_shape(s, k), s.dtype) for s, k in zip(shards, kinds)],
        in_specs=[HBM_SPEC] * n, out_specs=[HBM_SPEC] * n,
        scratch_shapes=[pltpu.SemaphoreType.DMA((n_gather, 7)), pltpu.SemaphoreType.DMA((n_gather, 7)),
                        pltpu.SemaphoreType.DMA((n,))],
        compiler_params=pltpu.CompilerParams(has_side_effects=True),
    )(*shards)


SEM_SPEC = pl.BlockSpec(memory_space=pltpu.SEMAPHORE)
HBM_ONLY = pl.BlockSpec(memory_space=pltpu.HBM)
DATAFLOW = pltpu.SideEffectType.DATAFLOW_SIDE_EFFECTING


def _peers():
    x, y, c = _me()
    return [(x ^ (k >> 2), y ^ ((k >> 1) & 1), c ^ (k & 1)) for k in range(1, N_DEV)]


def _gather_copies(shapes, kinds):
    def make(src_refs, land_refs, send_sems, recv_sems):
        mine = _index(*_me())
        return [pltpu.make_async_remote_copy(
            src_ref=src_refs[i], dst_ref=_block(land_refs[i], kind, mine, r, c),
            send_sem=send_sems.at[7 * i + k], recv_sem=recv_sems.at[7 * i + k], device_id=peer, device_id_type=MESH)
            for i, ((r, c), kind) in enumerate(zip(shapes, kinds)) for k, peer in enumerate(_peers())]

    return make


def _exchange_copies(shapes, kinds):
    def make(src_refs, land_refs, send_sems, recv_sems):
        mine = _index(*_me())
        return [pltpu.make_async_remote_copy(
            src_ref=_block(src_refs[i], kind, _index(*peer), r, c), dst_ref=land_refs[i].at[mine],
            send_sem=send_sems.at[7 * i + k], recv_sem=recv_sems.at[7 * i + k], device_id=peer, device_id_type=MESH)
            for i, ((r, c), kind) in enumerate(zip(shapes, kinds)) for k, peer in enumerate(_peers())]

    return make


def _place_block(shard, kind, name):
    r, c = shard.shape
    tile = _row_tile(r, c)
    nt = r // tile
    full = (N_DEV * r, c) if kind == "rows" else (r, N_DEV * c)

    def body(me_ref, x_ref, out_ref):
        out_ref[...] = x_ref[...]

    if kind == "rows":
        out_spec = pl.BlockSpec((tile, c), lambda i, me: (me[0] * nt + i, 0))
    else:
        out_spec = pl.BlockSpec((tile, c), lambda i, me: (i, me[0]))
    return pl.pallas_call(
        body, name=name, out_shape=jax.ShapeDtypeStruct(full, shard.dtype),
        grid_spec=pltpu.PrefetchScalarGridSpec(
            num_scalar_prefetch=1, grid=(nt,),
            in_specs=[pl.BlockSpec((tile, c), lambda i, me: (i, 0))], out_specs=out_spec),
        compiler_params=_params(("arbitrary",)),
    )(_my_index_operand(), shard)


def _split_start(srcs, lands, make, name):
    n = len(srcs)

    def body(*refs):
        send_sems, recv_sems = refs[2 * n], refs[2 * n + 1]
        for cp in make(refs[:n], refs[n:2 * n], send_sems, recv_sems):
            cp.start()
        refs[-1][...] = jnp.zeros_like(refs[-1])

    arrays = list(srcs) + list(lands)
    out = pl.pallas_call(
        body, name=name,
        out_shape=(pltpu.SemaphoreType.DMA((7 * n,)), pltpu.SemaphoreType.DMA((7 * n,)),
                   *[pltpu.HBM(a.shape, a.dtype) for a in arrays], jax.ShapeDtypeStruct((SUBLANES, LANES), F32)),
        in_specs=[HBM_ONLY] * (2 * n),
        out_specs=(SEM_SPEC, SEM_SPEC, *[HBM_ONLY] * (2 * n), pl.BlockSpec(memory_space=pltpu.VMEM)),
        input_output_aliases={i: 2 + i for i in range(2 * n)},
        compiler_params=pltpu.CompilerParams(has_side_effects=DATAFLOW),
    )(*[pltpu.with_memory_space_constraint(a, pltpu.HBM) for a in arrays])
    return out[0], out[1], list(out[2:2 + n]), list(out[2 + n:2 + 2 * n]), out[-1]


def _split_wait(send_sems, recv_sems, srcs, lands, after, make, name):
    n = len(srcs)

    def body(*refs):
        for cp in make(refs[:n], refs[n:2 * n], refs[2 * n], refs[2 * n + 1]):
            cp.wait_send()
            cp.wait_recv()

    arrays = list(srcs) + list(lands)
    out = pl.pallas_call(
        body, name=name,
        out_shape=tuple(pltpu.HBM(a.shape, a.dtype) for a in arrays),
        in_specs=[HBM_ONLY] * (2 * n) + [SEM_SPEC, SEM_SPEC, pl.BlockSpec(memory_space=pl.ANY)],
        out_specs=tuple([HBM_ONLY] * (2 * n)),
        input_output_aliases={i: i for i in range(2 * n)},
        compiler_params=pltpu.CompilerParams(has_side_effects=DATAFLOW),
    )(*arrays, send_sems, recv_sems, after)
    return list(out[:n]), list(out[n:])


def _all_reduce_small(vals, name):
    rows, width = vals.shape

    def body(x_ref, out_ref, buf_ref, send_sems, recv_sems):
        x, y, c = _me()
        mine = _index(x, y, c)
        buf_ref[mine] = x_ref[...]
        copies = []
        for k in range(1, N_DEV):
            px, py, pc = x ^ (k >> 2), y ^ ((k >> 1) & 1), c ^ (k & 1)
            copies.append(pltpu.make_async_remote_copy(
                src_ref=x_ref, dst_ref=buf_ref.at[mine],
                send_sem=send_sems.at[k - 1], recv_sem=recv_sems.at[k - 1],
                device_id=(px, py, pc), device_id_type=MESH))
        for cp in copies:
            cp.start()
        for cp in copies:
            cp.wait()
        acc = buf_ref[0]
        for j in range(1, N_DEV):
            acc = acc + buf_ref[j]
        out_ref[...] = acc

    vmem = pl.BlockSpec(memory_space=pltpu.VMEM)
    return pl.pallas_call(
        body, name=name,
        out_shape=jax.ShapeDtypeStruct(vals.shape, F32),
        in_specs=[vmem], out_specs=vmem,
        scratch_shapes=[pltpu.VMEM((N_DEV, rows, width), F32),
                        pltpu.SemaphoreType.DMA((7,)), pltpu.SemaphoreType.DMA((7,))],
        compiler_params=pltpu.CompilerParams(has_side_effects=True),
    )(vals)


def _adamw_math(w, g, m, v):
    m = ADAM_B1 * m + (1.0 - ADAM_B1) * g
    v = ADAM_B2 * v + (1.0 - ADAM_B2) * (g * g)
    m_hat = m / (1.0 - ADAM_B1 ** ADAM_STEP)
    v_hat = v / (1.0 - ADAM_B2 ** ADAM_STEP)
    delta = -ADAM_LR * (m_hat / (jnp.sqrt(v_hat) + ADAM_EPS) + ADAM_WD * w)
    return delta, m, v


ROW_TILE_ELEMS = 384 * 1024


def _row_tile(rows, width):
    best = SUBLANES
    for t in range(SUBLANES, rows + 1, SUBLANES):
        if rows % t == 0 and t * width <= ROW_TILE_ELEMS:
            best = t
    return best


def _sum_received(r_ref, own, me):
    g = None
    for j in range(N_DEV):
        term = jnp.where(me == j, own, r_ref[j].astype(F32))
        g = term if g is None else g + term
    return g


def _my_index_operand():
    return _index(*_me()).astype(jnp.int32).reshape(1)


def _sum_small(recv, own):
    def body(me_ref, r_ref, own_ref, out_ref):
        out_ref[...] = _sum_received(r_ref, own_ref[...], me_ref[0])

    whole = lambda shape: pl.BlockSpec(shape, lambda i, me, nd=len(shape): (0,) * nd)
    return pl.pallas_call(
        body, name="small_grads_sum", out_shape=jax.ShapeDtypeStruct(own.shape, F32),
        grid_spec=pltpu.PrefetchScalarGridSpec(
            num_scalar_prefetch=1, grid=(1,), in_specs=[whole(recv.shape), whole(own.shape)],
            out_specs=whole(own.shape)),
        compiler_params=_params(("arbitrary",)),
    )(_my_index_operand(), recv, own)


def _adamw_recv(recv, grad, kind, w, m, v, name):
    _, rows, width = recv.shape
    tile = _row_tile(rows, width)
    nt = rows // tile

    def body(me_ref, r_ref, own_ref, w_ref, m_ref, v_ref, g_out, d_out, m_out, v_out):
        g = _sum_received(r_ref, own_ref[...].astype(F32), me_ref[0])
        d, mn, vn = _adamw_math(w_ref[...], g, m_ref[...], v_ref[...])
        g_out[...] = g
        d_out[...] = d
        m_out[...] = mn
        v_out[...] = vn

    if kind == "rows":
        own_spec = pl.BlockSpec((tile, width), lambda i, me: (me[0] * nt + i, 0))
    else:
        own_spec = pl.BlockSpec((tile, width), lambda i, me: (i, me[0]))
    spec = pl.BlockSpec((tile, width), lambda i, me: (i, 0))
    shape = jax.ShapeDtypeStruct((rows, width), F32)
    return pl.pallas_call(
        body, name=name, out_shape=[shape] * 4,
        grid_spec=pltpu.PrefetchScalarGridSpec(
            num_scalar_prefetch=1, grid=(nt,),
            in_specs=[pl.BlockSpec((N_DEV, tile, width), lambda i, me: (0, i, 0)), own_spec, spec, spec, spec],
            out_specs=[spec] * 4),
        compiler_params=_params(("parallel",)),
    )(_my_index_operand(), recv, grad, w, m, v)


WIN_STEP = 1408
WIN_W = 1536
IN_SHARD = IN_COLS // N_DEV
IN_PADDED = WIN_STEP * (N_DEV - 1) + WIN_W


def _roll_w_in(shard_padded):
    rows = shard_padded.shape[0]
    tile = _row_tile(rows, WIN_W)

    def body(x_ref, main_ref, edge_ref):
        win = pltpu.roll(x_ref[...], 2 * _index(*_me()), 1).astype(BF16)
        main_ref[...] = win[:, :WIN_STEP]
        edge_ref[...] = win[:, WIN_STEP:]

    return pl.pallas_call(
        body, name="w_in_window",
        out_shape=[jax.ShapeDtypeStruct((rows, WIN_STEP), BF16), jax.ShapeDtypeStruct((rows, WIN_W - WIN_STEP), BF16)],
        grid=(rows // tile,),
        in_specs=[pl.BlockSpec((tile, WIN_W), lambda i: (i, 0))],
        out_specs=[pl.BlockSpec((tile, WIN_STEP), lambda i: (i, 0)),
                   pl.BlockSpec((tile, WIN_W - WIN_STEP), lambda i: (i, 0))],
        compiler_params=_params(("parallel",)),
    )(shard_padded)


def _sum_w_in_windows(recv, grad):
    _, rows, width = recv.shape
    tile = _row_tile(rows, width)

    def body(me_ref, r_ref, g_ref, g_out, own_ref, sem):
        me = me_ref[0]
        rows_i = pl.ds(pl.multiple_of(pl.program_id(0) * tile, tile), tile)
        own = pltpu.make_async_copy(g_ref.at[rows_i, pl.ds(pl.multiple_of(me * WIN_STEP, LANES), width)], own_ref, sem)
        own.start()
        own.wait()
        g_out[...] = pltpu.roll(_sum_received(r_ref, own_ref[...].astype(F32), me), width - 2 * me, 1)

    return pl.pallas_call(
        body, name="w_in_grad_sum", out_shape=jax.ShapeDtypeStruct((rows, width), F32),
        grid_spec=pltpu.PrefetchScalarGridSpec(
            num_scalar_prefetch=1, grid=(rows // tile,),
            in_specs=[pl.BlockSpec((N_DEV, tile, width), lambda i, me: (0, i, 0)), HBM_SPEC],
            out_specs=pl.BlockSpec((tile, width), lambda i, me: (i, 0)),
            scratch_shapes=[pltpu.VMEM((tile, width), BF16), pltpu.SemaphoreType.DMA]),
        compiler_params=_params(("arbitrary",)),
    )(_my_index_operand(), recv, grad)


def _adamw_small(w, g, m, v, name):
    def fn(i, n, w_, g_, m_, v_):
        return _adamw_math(w_, g_, m_, v_)

    r, c = w.shape
    return _rows(fn, [(w, "t"), (g, "t"), (m, "t"), (v, "t")], [], [(c, F32)] * 3, [], _row_tile(r, c), name)


def _norm_fwd(x, w, name):
    return _rows(lambda i, n, x_, w_: (_rms(x_, w_[...]),), [(x, "t")], [w], [(D_MODEL, BF16)], [], 512, name)[0]


def _residual_norm_fwd(x, y, scale, w, name):
    def fn(i, n, x_, y_, w_):
        xn = x_ + scale * y_
        return xn, _rms(xn, w_[...])

    return _rows(fn, [(x, "t"), (y, "t")], [w], [(D_MODEL, F32), (D_MODEL, BF16)], [], 512, name)


def _residual_norm_bwd(x, w, dhs, dres, scale, name):
    nh = len(dhs)

    def fn(i, n, x_, dres_, *rest):
        dh = rest[0]
        for extra in rest[1:nh]:
            dh = dh + extra
        _, vjp = jax.vjp(_rms, x_, rest[nh][...])
        dx, dw = vjp(dh)
        dx = dx + dres_
        return dx, scale * dx, dw

    return _rows(fn, [(x, "t"), (dres, "t")] + [(d, "t") for d in dhs], [w],
                 [(D_MODEL, F32), (D_MODEL, BF16)], [(1, D_MODEL)], 256, name)


FFN_UP_TN = 512


def _ffn_up(h, w_gu, name):
    t, d = h.shape
    f = w_gu.shape[1] // 2
    tm = _pick(t, (1024, 512, 256, 128))
    nj = f // FFN_UP_TN

    def body(h_ref, wg_ref, wu_ref, g_ref, u_ref, act_ref):
        hb = h_ref[...]
        g = jnp.dot(hb, wg_ref[...], preferred_element_type=F32)
        u = jnp.dot(hb, wu_ref[...], preferred_element_type=F32)
        g_ref[...] = g.astype(BF16)
        u_ref[...] = u.astype(BF16)
        act_ref[...] = (_silu(g) * u).astype(BF16)

    out = pl.BlockSpec((tm, FFN_UP_TN), lambda i, j: (i, j))
    return pl.pallas_call(
        body, name=name, out_shape=[jax.ShapeDtypeStruct((t, f), BF16)] * 3, grid=(t // tm, nj),
        in_specs=[pl.BlockSpec((tm, d), lambda i, j: (i, 0)),
                  pl.BlockSpec((d, FFN_UP_TN), lambda i, j: (0, j)),
                  pl.BlockSpec((d, FFN_UP_TN), lambda i, j: (0, j + nj))],
        out_specs=[out, out, out],
        compiler_params=_params(("parallel", "parallel")),
    )(h, w_gu, w_gu)


def _ffn_fwd(h, w_gu, get_w_down, tag):
    g, u, act = _ffn_up(h, w_gu, tag + "_gu")
    y = _matmul(act, get_w_down(act), "nn", F32, tag + "_down")
    return (g, u), act, y


def _ffn_bwd(h, gu, act, dy, w_gu, w_down, tag, comm, more=None):
    dact = _matmul(dy, w_down, "nt", BF16, tag + "_dact")

    def fn(i, n, g_, u_, dact_):
        _, vjp = jax.vjp(lambda a, b: _silu(a) * b, g_, u_)
        return (jnp.concatenate(vjp(dact_), axis=1),)

    dgu = _rows(fn, [(gu[0], "t"), (gu[1], "t"), (dact, "t")], [], [(2 * D_FF, BF16)], [], 128,
                tag + "_swiglu_bwd")[0]
    sent = comm.send(tag + "_gu", {tag + "_w_gu": _matmul(h, dgu, "tn", BF16, tag + "_d_w_gu")})
    sent = sent + comm.send(tag + "_down", {tag + "_w_down": _matmul(act, dy + sent.astype(BF16), "tn", BF16,
                                                                    tag + "_d_w_down"), **(more or {})})
    dh = _matmul(dgu, w_gu, "nt", BF16, tag + "_dh")
    return dh, sent


def _expanders():
    e_g = np.zeros((LANES, HW), np.float32)
    e_b = np.zeros((LANES, HW), np.float32)
    for h in range(HEADS):
        e_g[h, h * HEAD_DIM:(h + 1) * HEAD_DIM] = 1.0
        e_b[HEADS + h, h * HEAD_DIM:(h + 1) * HEAD_DIM] = 1.0
    return jnp.asarray(e_g), jnp.asarray(e_b)


def _pad_lanes(v):
    return jnp.pad(v, ((0, 0), (0, LANES - v.shape[1])))


class _LocalWeights:
    def __init__(self, big):
        self.big, self.sent = big, {}

    def arrive(self, group, after):
        return self.big

    def send(self, group, grads):
        self.sent.update(grads)
        return jnp.zeros((), F32)


def _local_step(x, p, tgt, small, comm):
    e_g, e_b = _expanders()
    alog, dtb = _pad_lanes(small["a_log"]), _pad_lanes(small["dt_bias"])
    conv_w = jnp.pad(small["conv_w"], ((0, SUBLANES - CONV_K), (0, 0)))
    rel = _expand_rel_bias(small["rel_bias"])

    h1 = _norm_fwd(x, small["ffn1_norm"], "ffn1_norm")
    big = dict(comm.arrive("ffn1", h1))
    if "_token" in big:
        h1 = h1 + big.pop("_token").astype(BF16)

    def ffn1_w_down(act):
        big.update(comm.arrive("ffn1_down", act))
        return big["ffn1_w_down"]

    gu1, act1, y1 = _ffn_fwd(h1, big["ffn1_w_gu"], ffn1_w_down, "ffn1")
    x1, h2 = _residual_norm_fwd(x, y1, 0.5, small["mix_norm"], "mix_norm")

    big = {**big, **comm.arrive("mixer", h2)}
    w_in = big["w_in"]
    w_qz = w_in[:, :IN_QZ]
    w_ab = jnp.pad(w_in[:, IN_AB0:IN_QKVB0], ((0, 0), (0, LANES - 2 * HEADS)))
    w_qkvb = w_in[:, IN_QKVB0:IN_GG0]
    w_gg = w_in[:, IN_GG0:IN_COLS]
    qz = _matmul(h2, w_qz, "nn", F32, "in_qz")
    ab = _matmul(h2, w_ab, "nn", F32, "in_ab")
    pb = _matmul(h2, w_qkvb, "nn", F32, "in_qkvb")
    gg = _matmul(h2, w_gg, "nn", BF16, "in_gates")
    pa, z = qz[:, :3 * HW], qz[:, 3 * HW:]

    def prep(i, n, pa_, prev_, ab_, cw_, alog_, dtb_, eg_, eb_):
        q, k, v = _gdn_post(_conv(pa_, prev_, cw_, i))
        g_b, beta_b = _gdn_gates(ab_, alog_[...], dtb_[...], eg_[...], eb_[...])
        return q, k, v, g_b, beta_b

    qn, kn, vv, g_b, beta_b = _rows(prep, [(pa, "t"), (pa, "p"), (ab, "t")], [conv_w, alog, dtb, e_g, e_b],
                                    [(HW, F32)] * 5, [], 256, "gdn_prep")
    u, w, aqk, qd, kt, tl = _gdn_intra(qn, kn, vv, g_b, beta_b)
    o, states = _gdn_scan(u, w, aqk, qd, kt, tl)
    ya = _rows(lambda i, n, o_, z_, w_: (_gated_norm(o_, z_, w_[...]),), [(o, "t"), (z, "t")], [small["gdn_norm"]],
               [(HW, BF16)], [], 512, "gdn_gated_norm")[0]

    pbp = jnp.pad(pb, ((ATT_PAD, 0), (0, 0)))
    yb = _attention(pb, pbp, small["q_norm"], small["k_norm"], rel)

    big = {**big, **comm.arrive("branches", yb)}
    ta = _matmul(ya, big["w_branch_a"], "nn", BF16, "branch_a")
    tb = _matmul(yb, big["w_branch_b"], "nn", BF16, "branch_b")
    mixed = _rows(lambda i, n, gg_, ta_, tb_: (_mix(gg_, ta_, tb_),), [(gg, "t"), (ta, "t"), (tb, "t")], [],
                  [(D_MODEL, BF16)], [], 256, "mix")[0]
    m_out = _matmul(mixed, big["w_out"], "nn", F32, "w_out")
    x2, h3 = _residual_norm_fwd(x1, m_out, 1.0, small["ffn2_norm"], "ffn2_norm")
    big = {**big, **comm.arrive("tail", h3)}
    gu2, act2, y2 = _ffn_fwd(h3, big["ffn2_w_gu"], lambda act: big["ffn2_w_down"], "ffn2")
    x3, h4 = _residual_norm_fwd(x2, y2, 0.5, small["ple_norm"], "ple_norm")
    gp = _matmul(h4, big["ple_gate"], "nn", BF16, "ple_gate")
    pp = _matmul(p, big["ple_proj"], "nn", BF16, "ple_proj")

    def head(i, n, x3_, gp_, pp_, tgt_):
        sg = _sigmoid(gp_)
        err = x3_ + sg * pp_ - tgt_
        dx4 = err * (1.0 / D_MODEL)
        sq = _colsum(err * err)
        part = sq[:, :LANES]
        for j in range(1, D_MODEL // LANES):
            part = part + sq[:, j * LANES:(j + 1) * LANES]
        return dx4, dx4 * pp_ * sg * (1.0 - sg), dx4 * sg, (0.5 / D_MODEL) * part

    dx4, dgp, dpp, loss_lanes = _rows(head, [(x3, "t"), (gp, "t"), (pp, "t"), (tgt, "t")], [],
                                      [(D_MODEL, F32), (D_MODEL, BF16), (D_MODEL, BF16)], [(1, LANES)], 256,
                                      "ple_loss_head")
    loss = jnp.sum(loss_lanes)

    gbig, gsmall = {}, {}
    gbig["ple_proj"] = _matmul(p, dpp, "tn", BF16, "d_ple_proj")
    gbig["ple_gate"] = _matmul(h4, dgp, "tn", BF16, "d_ple_gate")
    dh4 = _matmul(dgp, big["ple_gate"], "nt", BF16, "ple_gate_dh")
    dx3, dy2, gsmall["ple_norm"] = _residual_norm_bwd(x3, small["ple_norm"], [dh4], dx4, 0.5, "ple_norm_bwd")

    dh3, sent = _ffn_bwd(h3, gu2, act2, dy2, big["ffn2_w_gu"], big["ffn2_w_down"], "ffn2", comm,
                         {n: gbig[n] for n in ("ple_proj", "ple_gate")})
    dx2, dx2b, gsmall["ffn2_norm"] = _residual_norm_bwd(x2, small["ffn2_norm"] + sent, [dh3], dx3, 1.0,
                                                        "ffn2_norm_bwd")

    gbig["w_out"] = _matmul(mixed, dx2b, "tn", BF16, "d_w_out")
    dmixed = _matmul(dx2b, big["w_out"], "nt", BF16, "w_out_dx")

    def mix_bwd(i, n, gg_, ta_, tb_, dm_):
        _, vjp = jax.vjp(_mix, gg_, ta_, tb_)
        return vjp(dm_)

    dgg, dta, dtb_ = _rows(mix_bwd, [(gg, "t"), (ta, "t"), (tb, "t"), (dmixed, "t")], [],
                           [(2 * D_MODEL, BF16), (D_MODEL, BF16), (D_MODEL, BF16)], [], 256, "mix_bwd")
    gbig["w_branch_a"] = _matmul(ya, dta, "tn", BF16, "d_branch_a")
    gbig["w_branch_b"] = _matmul(yb, dtb_, "tn", BF16, "d_branch_b")
    dya = _matmul(dta, big["w_branch_a"], "nt", BF16, "branch_a_dx")
    dyb = _matmul(dtb_, big["w_branch_b"], "nt", BF16, "branch_b_dx")

    dq_b, dk_b, dv_b, gsmall["q_norm"], gsmall["k_norm"], gsmall["rel_bias"] = _attention_bwd(
        pb, pbp, small["q_norm"], small["k_norm"], rel, dyb)
    dpb = jnp.concatenate([dq_b, dk_b[ATT_PAD:].astype(BF16), dv_b[ATT_PAD:].astype(BF16)], axis=1)

    def gated_bwd(i, n, o_, z_, dya_, w_):
        _, vjp = jax.vjp(_gated_norm, o_, z_, w_[...])
        return vjp(dya_)

    do, dz, gsmall["gdn_norm"] = _rows(gated_bwd, [(o, "t"), (z, "t"), (dya, "t")], [small["gdn_norm"]],
                                       [(HW, F32), (HW, BF16)], [(1, HEAD_DIM)], 256, "gdn_gated_norm_bwd")
    du, dw, da, dqd, dkt, dtl = _gdn_scan_bwd(do, u, w, aqk, qd, kt, tl, states)
    dqn, dkn, dvv, dg_b, dbeta_b = _gdn_intra_bwd(qn, kn, vv, g_b, beta_b, du, dw, da, dqd, dkt, dtl)

    def prep_bwd(i, n, pa_, prev_, ab_, dq_, dk_, dv_, dg_, db_, cw_, alog_, dtb_, eg_, eb_):
        _, vjp = jax.vjp(_gdn_post, _conv(pa_, prev_, cw_, i))
        (dy,) = vjp((dq_, dk_, dv_))
        e_g_, e_b_ = eg_[...], eb_[...]
        _, vjp_g = jax.vjp(lambda a, b, c: _gdn_gates(a, b, c, e_g_, e_b_), ab_, alog_[...], dtb_[...])
        dab, dalog, ddtb = vjp_g((dg_, db_))
        return dy, dab, dalog, ddtb

    dy_conv, dab, dalog, ddtb = _rows(
        prep_bwd, [(pa, "t"), (pa, "p"), (ab, "t"), (dqn, "t"), (dkn, "t"), (dvv, "t"), (dg_b, "t"), (dbeta_b, "t")],
        [conv_w, alog, dtb, e_g, e_b], [(3 * HW, F32), (LANES, BF16)], [(1, LANES), (1, LANES)], 256,
        "gdn_prep_bwd")
    gsmall["a_log"] = dalog[:, :HEADS]
    gsmall["dt_bias"] = ddtb[:, :HEADS]

    def conv_bwd(i, n, dy_, nxt_, pa_, prev_, cw_):
        dpa = dy_ * cw_[CONV_K - 1:CONV_K, :]
        row = lax.broadcasted_iota(jnp.int32, (SUBLANES, dy_.shape[1]), 0)
        dcw = jnp.where(row == CONV_K - 1, _colsum(dy_ * pa_), 0.0)
        for j in range(CONV_K - 1):
            s = CONV_K - 1 - j
            dpa = dpa + _shift_up(dy_, nxt_, s, i, n) * cw_[j:j + 1, :]
            dcw = dcw + jnp.where(row == j, _colsum(dy_ * _shift_down(pa_, prev_, s, i)), 0.0)
        return dpa, dcw

    dpa, dcw = _rows(conv_bwd, [(dy_conv, "t"), (dy_conv, "n"), (pa, "t"), (pa, "p")], [conv_w],
                     [(3 * HW, BF16)], [(SUBLANES, 3 * HW)], 256, "gdn_conv_bwd")
    gsmall["conv_w"] = dcw[:CONV_K]

    dqz = jnp.concatenate([dpa, dz], axis=1)
    d_w_qz = _matmul(h2, dqz, "tn", BF16, "d_in_qz")
    d_w_ab = _matmul(h2, dab, "tn", BF16, "d_in_ab")
    d_w_qkvb = _matmul(h2, dpb, "tn", BF16, "d_in_qkvb")
    d_w_gg = _matmul(h2, dgg, "tn", BF16, "d_in_gates")
    gbig["w_in"] = jnp.concatenate([d_w_qz, d_w_ab[:, :2 * HEADS], d_w_qkvb, d_w_gg,
                                    jnp.zeros((D_MODEL, IN_PADDED - IN_COLS), BF16)], axis=1)
    dh2 = [_matmul(dqz, w_qz, "nt", BF16, "in_qz_dh"), _matmul(dab, w_ab, "nt", BF16, "in_ab_dh"),
           _matmul(dpb, w_qkvb, "nt", BF16, "in_qkvb_dh"), _matmul(dgg, w_gg, "nt", BF16, "in_gates_dh")]
    sent = comm.send("mixer", {n: gbig[n] for n in ("w_out", "w_branch_b", "w_branch_a", "w_in")})
    dx1, dy1, gsmall["mix_norm"] = _residual_norm_bwd(x1, small["mix_norm"] + sent, dh2, dx2, 0.5, "mix_norm_bwd")

    dh1, sent = _ffn_bwd(h1, gu1, act1, dy1, big["ffn1_w_gu"], big["ffn1_w_down"], "ffn1", comm)
    grad_x, _, gsmall["ffn1_norm"] = _residual_norm_bwd(x, small["ffn1_norm"] + sent, [dh1], dx1, 1.0,
                                                        "ffn1_norm_bwd")
    return loss, grad_x, gsmall


GATHER_GROUPS = {"ffn1": ("ffn1_w_gu",),
                 "ffn1_down": ("ffn1_w_down",),
                 "mixer": ("w_in_main", "w_in_edge"),
                 "branches": ("w_branch_a", "w_branch_b", "w_out"),
                 "tail": ("ffn2_w_gu", "ffn2_w_down", "ple_gate", "ple_proj")}
SPLIT_GATHERS = ("ffn1_down", "mixer", "branches", "tail")


def _kind(name):
    return "cols" if name in COL_SHARDED or name.startswith("w_in_") else "rows"


def _merge_w_in(main, edges):
    edge_w = WIN_W - WIN_STEP
    w_in = jnp.pad(main, ((0, 0), (0, edge_w)))
    for d in range(N_DEV):
        at = WIN_STEP * (d + 1)
        w_in = w_in + jnp.pad(edges[:, d * edge_w:(d + 1) * edge_w], ((0, 0), (at, IN_PADDED - at - edge_w)))
    return w_in


class _Fsdp:
    def __init__(self, wts, first):
        self.wts, self.first_token = wts, first
        main, edge = _roll_w_in(jnp.pad(wts["w_in"], ((0, 0), (0, WIN_W - IN_SHARD))))
        self.shards = {n: wts[n].astype(BF16) for n in BIG if n not in ("w_in", "ffn1_w_gu")}
        self.shards.update(w_in_main=main, w_in_edge=edge)
        self.lands = {n: _place_block(self.shards[n], _kind(n), "own_" + n)
                      for group in SPLIT_GATHERS for n in GATHER_GROUPS[group]}
        self.flight, self.sent = {}, {}

    def _gather_first(self, after):
        token = self.first_token + after[0, 0].astype(F32) * 0.0
        me = _index(*_me())
        for n, land in self.lands.items():
            r, c = self.shards[n].shape
            at = (me * r, 0) if _kind(n) == "rows" else (0, me * c)
            token = token + lax.dynamic_slice(land, at, (1, 1))[0, 0].astype(F32) * 0.0
        shard = (self.wts["ffn1_w_gu"] + token).astype(BF16)
        self.shards["ffn1_w_gu"] = shard
        first = _all_gather([shard], [_kind("ffn1_w_gu")], 1)[0]
        token = first[0, 0].astype(F32) * 0.0
        for group in SPLIT_GATHERS:
            names = GATHER_GROUPS[group]
            srcs = [self.shards[n] for n in names]
            lands = [self.lands[n] for n in names]
            make = _gather_copies([s.shape for s in srcs], [_kind(n) for n in names])
            srcs[0] = srcs[0] + token.astype(BF16)
            send_sems, recv_sems, srcs, lands, tok = _split_start(srcs, lands, make, "gather_start_" + group)
            token = token + tok[0, 0]
            self.flight[group] = (send_sems, recv_sems, srcs, lands, make)
        return {"ffn1_w_gu": first, "_token": token}

    def arrive(self, group, after):
        if group == "ffn1":
            return self._gather_first(after)
        send_sems, recv_sems, srcs, lands, make = self.flight[group]
        _, lands = _split_wait(send_sems, recv_sems, srcs, lands, after, make, "gather_wait_" + group)
        full = dict(zip(GATHER_GROUPS[group], lands))
        if group == "mixer":
            full["w_in"] = _merge_w_in(full.pop("w_in_main"), full.pop("w_in_edge"))
        return full

    def send(self, group, grads):
        names = list(grads)
        kinds = ["all" if n == "small" else "win" if n == "w_in" else _kind(n) for n in names]
        shapes = [grads[n].shape if n == "small" else (D_MODEL, WIN_W) if n == "w_in" else self.shards[n].shape
                  for n in names]
        srcs = [grads[n] for n in names]
        lands = [lax.empty((N_DEV,) + tuple(s), g.dtype) for s, g in zip(shapes, srcs)]
        make = _exchange_copies(shapes, kinds)
        send_sems, recv_sems, srcs, lands, tok = _split_start(srcs, lands, make, "grads_start_" + group)
        self.sent[group] = (names, kinds, send_sems, recv_sems, srcs, lands, make)
        return tok[0, 0]

    def received(self, group, after):
        names, kinds, send_sems, recv_sems, srcs, lands, make = self.sent[group]
        srcs, lands = _split_wait(send_sems, recv_sems, srcs, lands, after, make, "grads_wait_" + group)
        return {n: (k, g, r) for n, k, g, r in zip(names, kinds, srcs, lands)}


SMALL_ROWS = ("ffn1_norm", "mix_norm", "ffn2_norm", "ple_norm", "gdn_norm", "q_norm", "k_norm", "a_log", "dt_bias",
              "rel_bias", "conv_w")


def _pack_small(vals):
    rows = []
    for n in SMALL_ROWS:
        v = vals[n]
        if n == "rel_bias":
            v = jnp.pad(v, ((0, 0), (0, 2 * LANES - N_REL)))
        elif n in ("a_log", "dt_bias"):
            v = _pad_lanes(v)
        rows.append(v.reshape(-1, LANES))
    packed = jnp.concatenate(rows, axis=0)
    return jnp.pad(packed, ((0, -packed.shape[0] % SUBLANES), (0, 0)))


def _unpack_small(packed, shapes):
    out, off = {}, 0
    for n in SMALL_ROWS:
        shp = shapes[n]
        if n == "rel_bias":
            out[n] = packed[off:off + 2 * HEADS].reshape(HEADS, 2 * LANES)[:, :N_REL]
            off += 2 * HEADS
        elif n in ("a_log", "dt_bias"):
            out[n] = packed[off:off + 1, :HEADS]
            off += 1
        else:
            r = int(np.prod(shp)) // LANES
            out[n] = packed[off:off + r].reshape(shp)
            off += r
    return out


WEIGHTS = ("ffn1_norm", "ffn1_w_gu", "ffn1_w_down", "mix_norm", "w_in", "conv_w", "a_log", "dt_bias", "gdn_norm",
           "q_norm", "k_norm", "rel_bias", "w_branch_a", "w_branch_b", "w_out", "ffn2_norm", "ffn2_w_gu",
           "ffn2_w_down", "ple_norm", "ple_gate", "ple_proj")


def kernel(x, p, ffn1_norm, ffn1_w_gu, ffn1_w_down, mix_norm, w_in, conv_w, a_log, dt_bias, gdn_norm, q_norm, k_norm, rel_bias, w_branch_a, w_branch_b, w_out, ffn2_norm, ffn2_w_gu, ffn2_w_down, ple_norm, ple_gate, ple_proj, loss_target, m_ffn1_norm, m_ffn1_w_gu, m_ffn1_w_down, m_mix_norm, m_w_in, m_conv_w, m_a_log, m_dt_bias, m_gdn_norm, m_q_norm, m_k_norm, m_rel_bias, m_w_branch_a, m_w_branch_b, m_w_out, m_ffn2_norm, m_ffn2_w_gu, m_ffn2_w_down, m_ple_norm, m_ple_gate, m_ple_proj, v_ffn1_norm, v_ffn1_w_gu, v_ffn1_w_down, v_mix_norm, v_w_in, v_conv_w, v_a_log, v_dt_bias, v_gdn_norm, v_q_norm, v_k_norm, v_rel_bias, v_w_branch_a, v_w_branch_b, v_w_out, v_ffn2_norm, v_ffn2_w_gu, v_ffn2_w_down, v_ple_norm, v_ple_gate, v_ple_proj):
    args = dict(locals())
    def layer0(v):
        return v[0] if v.ndim == 3 else v

    wts = {n: layer0(args[n]) for n in WEIGHTS}
    mom = {n: layer0(args["m_" + n]) for n in WEIGHTS}
    var = {n: layer0(args["v_" + n]) for n in WEIGHTS}
    x2d, p2d, tgt = x[0], p[0, 0], loss_target[0]
    my_index = _index(*_me())

    small = {n: wts[n] for n in SMALL_ROWS if n != "conv_w"}
    conv_shard = wts["conv_w"]
    conv_cols = conv_shard.shape[1]
    conv_packed = jnp.zeros((SUBLANES, N_DEV * conv_cols), F32)
    conv_packed = lax.dynamic_update_slice(conv_packed, jnp.pad(conv_shard, ((0, SUBLANES - CONV_K), (0, 0))),
                                           (0, my_index * conv_cols))
    small["conv_w"] = _all_reduce_small(conv_packed.reshape(-1, LANES), "conv_w_gather").reshape(SUBLANES, -1)[:CONV_K]

    fsdp = _Fsdp(wts, small["conv_w"][0, 0] * 0.0)

    loss, grad_x, gsmall = _local_step(x2d, p2d, tgt, small, fsdp)
    loss = lax.psum(loss, ("x", "y", "c"))

    fsdp.send("small", {"small": _pack_small(gsmall)})

    outs_big, after = {}, grad_x
    for group in list(fsdp.sent):
        for n, (kind, grad, recv) in fsdp.received(group, after).items():
            if n == "small":
                small_sum = _sum_small(recv, grad)
            elif n == "w_in":
                g_in = _sum_w_in_windows(recv, grad)[:, :IN_SHARD]
                outs_big[n] = [g_in] + list(_adamw_small(wts[n], g_in, mom[n], var[n], "adamw_w_in"))
            else:
                outs_big[n] = _adamw_recv(recv, grad, kind, wts[n], mom[n], var[n], "adamw_" + n)
            after = small_sum if n == "small" else outs_big[n][1]

    small_shapes = {n: (small[n].shape if n != "conv_w" else (CONV_K, N_DEV * conv_cols)) for n in SMALL_ROWS}
    gsum = _unpack_small(small_sum, small_shapes)
    gsum["conv_w"] = lax.dynamic_slice(gsum["conv_w"], (0, my_index * conv_cols), (CONV_K, conv_cols))
    rep = [n for n in SMALL_ROWS if n != "conv_w"]
    rep_shapes = {n: small_shapes[n] for n in rep}

    def pack_rep(vals):
        return _pack_small({**{n: vals[n] for n in rep}, "conv_w": jnp.zeros((CONV_K, LANES), F32)})

    def unpack_rep(packed):
        return _unpack_small(packed, {**rep_shapes, "conv_w": (CONV_K, LANES)})

    outs_small = [unpack_rep(o) for o in _adamw_small(pack_rep(wts), pack_rep(gsum), pack_rep(mom), pack_rep(var),
                                                      "adamw_replicated")]
    pad8 = functools.partial(jnp.pad, pad_width=((0, SUBLANES - CONV_K), (0, 0)))
    outs_conv = [o[:CONV_K] for o in _adamw_small(pad8(conv_shard), pad8(gsum["conv_w"]), pad8(mom["conv_w"]),
                                                   pad8(var["conv_w"]), "adamw_conv")]

    def leaf(kind, n):
        if n in BIG:
            return outs_big[n][kind][None]
        if n == "conv_w":
            return (gsum["conv_w"] if kind == 0 else outs_conv[kind - 1])[None]
        return (gsum[n] if kind == 0 else outs_small[kind - 1][n]).reshape(args[n].shape)

    result = [loss, grad_x[None]]
    for kind in range(4):
        result += [leaf(kind, n) for n in WEIGHTS]
    return tuple(result)
```

```python
import functools

import numpy as np
import jax
import jax.numpy as jnp
from jax import lax
from jax.experimental import pallas as pl
from jax.experimental.pallas import tpu as pltpu

F32 = jnp.float32
BF16 = jnp.bfloat16
HIGHEST = lax.Precision.HIGHEST
MESH = pl.DeviceIdType.MESH

D_MODEL = 2048
D_FF = 5632
HEADS = 8
HEAD_DIM = 128
HW = HEADS * HEAD_DIM
CHUNK = 64
LEFT_CHUNKS = 8
MAX_REL = 128
N_REL = (CHUNK - 1) + MAX_REL + 1
CONV_K = 4
EPS = 1e-6
NEG_INF = -1e30
N_DEV = 8
LANES = 128
SUBLANES = 8
VMEM_LIMIT = 56 * 1024 * 1024

MATMUL_WHOLE_K = 2048

ATT_QB = 256
ATT_KW = ATT_QB + LEFT_CHUNKS * CHUNK
ATT_PAD = LEFT_CHUNKS * CHUNK
GDN_CB = 8
GDN_GROUP = 8
GDN_SCAN_UNROLL = 2

ADAM_LR = 0.001
ADAM_B1 = 0.9
ADAM_B2 = 0.999
ADAM_EPS = 1e-08
ADAM_WD = 0.01
ADAM_STEP = 10

IN_QZ = 3 * HW + HW
IN_AB0 = IN_QZ
IN_QKVB0 = IN_AB0 + 2 * HEADS
IN_GG0 = IN_QKVB0 + 3 * HW
IN_COLS = IN_GG0 + 2 * D_MODEL

BIG = ("ffn1_w_gu", "ffn1_w_down", "w_in", "w_branch_a", "w_branch_b", "w_out",
       "ffn2_w_gu", "ffn2_w_down", "ple_gate", "ple_proj")
COL_SHARDED = ("ffn1_w_gu", "w_in", "w_branch_a", "w_branch_b", "ffn2_w_gu", "ple_proj")


def _params(semantics=None, **kw):
    return pltpu.CompilerParams(dimension_semantics=semantics, vmem_limit_bytes=VMEM_LIMIT, **kw)


def _pick(n, cands):
    for c in cands:
        if n % c == 0:
            return c
    return n


def _matmul(a, b, mode, out_dtype, name):
    if mode == "nn":
        (m, k), (k2, n) = a.shape, b.shape
    elif mode == "nt":
        (m, k), (n, k2) = a.shape, b.shape
    else:
        (k, m), (k2, n) = a.shape, b.shape
    assert k == k2, (a.shape, b.shape, mode)
    tm = _pick(m, (1024, 512, 256, 128))
    tn = _pick(n, (1024, 512, 256, 128))
    tk = k if k <= MATMUL_WHOLE_K else _pick(k, (2816, 2048, 1536, 1024, 512, 256, 128))
    nk = k // tk
    if mode == "nn":
        a_spec = pl.BlockSpec((tm, tk), lambda i, j, kk: (i, kk))
        b_spec = pl.BlockSpec((tk, tn), lambda i, j, kk: (kk, j))
        dims = (((1,), (0,)), ((), ()))
    elif mode == "nt":
        a_spec = pl.BlockSpec((tm, tk), lambda i, j, kk: (i, kk))
        b_spec = pl.BlockSpec((tn, tk), lambda i, j, kk: (j, kk))
        dims = (((1,), (1,)), ((), ()))
    else:
        a_spec = pl.BlockSpec((tk, tm), lambda i, j, kk: (kk, i))
        b_spec = pl.BlockSpec((tk, tn), lambda i, j, kk: (kk, j))
        dims = (((0,), (0,)), ((), ()))

    def body(a_ref, b_ref, o_ref, *acc):
        prod = lax.dot_general(a_ref[...].astype(BF16), b_ref[...].astype(BF16), dims, preferred_element_type=F32)
        if nk == 1:
            o_ref[...] = prod.astype(o_ref.dtype)
            return
        acc_ref, kk = acc[0], pl.program_id(2)

        @pl.when(kk == 0)
        def _():
            acc_ref[...] = prod

        @pl.when((kk > 0) & (kk < nk - 1))
        def _():
            acc_ref[...] += prod

        @pl.when(kk == nk - 1)
        def _():
            o_ref[...] = (acc_ref[...] + prod).astype(o_ref.dtype)

    return pl.pallas_call(
        body, name=name,
        out_shape=jax.ShapeDtypeStruct((m, n), out_dtype),
        grid=(m // tm, n // tn, nk),
        in_specs=[a_spec, b_spec],
        out_specs=pl.BlockSpec((tm, tn), lambda i, j, kk: (i, j)),
        scratch_shapes=[pltpu.VMEM((tm, tn), F32)] if nk > 1 else [],
        compiler_params=_params(("parallel", "parallel", "arbitrary")),
    )(a, b)


def _rows(fn, row_ins, consts, row_outs, acc_outs, tile, name):
    t_rows = row_ins[0][0].shape[0]
    tile = min(tile, t_rows)
    assert t_rows % tile == 0 and tile % SUBLANES == 0
    n = t_rows // tile
    per = tile // SUBLANES
    last8 = t_rows // SUBLANES - 1
    in_specs = []
    for arr, kind in row_ins:
        c = arr.shape[1]
        if kind == "t":
            in_specs.append(pl.BlockSpec((tile, c), lambda i: (i, 0)))
        elif kind == "p":
            in_specs.append(pl.BlockSpec((SUBLANES, c), lambda i: (jnp.maximum(i * per - 1, 0), 0)))
        else:
            in_specs.append(pl.BlockSpec((SUBLANES, c), lambda i: (jnp.minimum((i + 1) * per, last8), 0)))
    for arr in consts:
        in_specs.append(pl.BlockSpec(arr.shape, lambda i, nd=arr.ndim: (0,) * nd))
    out_shape = [jax.ShapeDtypeStruct((t_rows, c), dt) for c, dt in row_outs]
    out_specs = [pl.BlockSpec((tile, c), lambda i: (i, 0)) for c, _ in row_outs]
    for shp in acc_outs:
        out_shape.append(jax.ShapeDtypeStruct(shp, F32))
        out_specs.append(pl.BlockSpec(shp, lambda i, nd=len(shp): (0,) * nd))
    n_in = len(row_ins) + len(consts)
    n_row_out = len(row_outs)

    def body(*refs):
        i = pl.program_id(0)
        vals = [r[...].astype(F32) for r in refs[:len(row_ins)]]
        res = fn(i, n, *vals, *refs[len(row_ins):n_in])
        outs = refs[n_in:]
        for r, v in zip(outs[:n_row_out], res[:n_row_out]):
            r[...] = v.astype(r.dtype)
        if acc_outs:
            @pl.when(i == 0)
            def _():
                for r in outs[n_row_out:]:
                    r[...] = jnp.zeros_like(r)

            for r, v in zip(outs[n_row_out:], res[n_row_out:]):
                r[...] += v

    res = pl.pallas_call(
        body, name=name, out_shape=out_shape, grid=(n,), in_specs=in_specs, out_specs=out_specs,
        compiler_params=_params(("arbitrary",) if acc_outs else ("parallel",)),
    )(*[a for a, _ in row_ins], *consts)
    return res


def _rms(x, w):
    return x * lax.rsqrt(jnp.mean(x * x, axis=-1, keepdims=True) + EPS) * w


def _l2n(x):
    return x * lax.rsqrt(jnp.sum(x * x, axis=-1, keepdims=True) + EPS)


def _sigmoid(x):
    return 1.0 / (1.0 + jnp.exp(-x))


def _silu(x):
    return x * _sigmoid(x)


def _softplus(x):
    return jnp.maximum(x, 0.0) + jnp.log(1.0 + jnp.exp(-jnp.abs(x)))


def _heads(fn, *xs):
    nh = xs[0].shape[1] // HEAD_DIM
    return jnp.concatenate(
        [fn(*[x[:, h * HEAD_DIM:(h + 1) * HEAD_DIM] for x in xs]) for h in range(nh)], axis=1)


def _colsum(x):
    return jnp.sum(x, axis=0, keepdims=True)


def _gated_norm(o, z, w):
    return _heads(lambda oh, zh: _rms(oh, w) * _silu(zh), o, z)


def _mix(gg, ta, tb):
    return _sigmoid(gg[:, :D_MODEL]) * ta + _sigmoid(gg[:, D_MODEL:]) * tb


def _gdn_post(y):
    a = _silu(y)
    q = _heads(lambda v: _l2n(v) * (HEAD_DIM ** -0.5), a[:, :HW])
    k = _heads(_l2n, a[:, HW:2 * HW])
    return q, k, a[:, 2 * HW:]


NN = (((1,), (0,)), ((), ()))
NT = (((1,), (1,)), ((), ()))
TN = (((0,), (0,)), ((), ()))


def _dg(a, b, dims):
    return lax.dot_general(a, b, dims, preferred_element_type=F32)


def _split2(x):
    hi = x.astype(BF16)
    return hi, (x - hi.astype(F32)).astype(BF16)


def _split3(x):
    hi = x.astype(BF16)
    r = x - hi.astype(F32)
    mid = r.astype(BF16)
    return hi, mid, (r - mid.astype(F32)).astype(BF16)


def _dg3(a, b, dims):
    ah, al = _split2(a)
    bh, bl = _split2(b)
    return _dg(ah, bh, dims) + (_dg(ah, bl, dims) + _dg(al, bh, dims))


BNN = (((2,), (1,)), ((0,), (0,)))
BNT = (((2,), (2,)), ((0,), (0,)))
BTN = (((1,), (1,)), ((0,), (0,)))


@jax.custom_vjp
def _mm3(a, b):
    return _dg3(a, b, BNN)


_mm3.defvjp(lambda a, b: (_dg3(a, b, BNN), (a, b)),
            lambda res, g: (_dg3(g, res[1], BNT), _dg3(res[0], g, BTN)))


def _xm(x, m, dims):
    mb = m.astype(BF16)
    parts = _split3(x)
    return _dg(parts[0], mb, dims) + (_dg(parts[1], mb, dims) + _dg(parts[2], mb, dims))


def _mx(m, x, dims):
    mb = m.astype(BF16)
    parts = _split3(x)
    return _dg(mb, parts[0], dims) + (_dg(mb, parts[1], dims) + _dg(mb, parts[2], dims))


@jax.custom_vjp
def _times_const(x, m):
    return _xm(x, m, NN)


_times_const.defvjp(lambda x, m: (_xm(x, m, NN), m),
                    lambda m, g: (_xm(g, m, NT), jnp.zeros_like(m)))


@jax.custom_vjp
def _const_times(m, x):
    return _mx(m, x, NN)


_const_times.defvjp(lambda m, x: (_mx(m, x, NN), m),
                    lambda m, g: (jnp.zeros_like(m), _mx(m, g, TN)))


@jax.custom_vjp
def _lane_mean_cols(x, avg):
    return _mx(avg, x, BNT)


_lane_mean_cols.defvjp(lambda x, avg: (_mx(avg, x, BNT), avg),
                       lambda avg, g: (_xm(g, avg, BTN), jnp.zeros_like(avg)))


def _gdn_gates(ab, alog, dtb, e_g, e_b):
    t = ab.shape[0]
    g = -jnp.exp(alog) * _softplus(ab + dtb)
    beta = _sigmoid(ab)
    ri = lax.broadcasted_iota(jnp.int32, (t, t), 0)
    ci = lax.broadcasted_iota(jnp.int32, (t, t), 1)
    shift = CHUNK.bit_length() - 1
    same = jnp.right_shift(ri, shift) == jnp.right_shift(ci, shift)
    tril = jnp.where(same & (ri >= ci), 1.0, 0.0).astype(F32)
    gc = _const_times(tril, g)
    return _times_const(gc, e_g), _times_const(beta, e_b)


def _shift_down(x, halo, s, i):
    if s == 0:
        return x
    halo = jnp.where(i == 0, 0.0, halo)
    xr = pltpu.roll(x, s, 0)
    hr = pltpu.roll(halo, s, 0)
    row = lax.broadcasted_iota(jnp.int32, (SUBLANES, x.shape[1]), 0)
    top = jnp.where(row < s, hr, xr[:SUBLANES])
    return jnp.concatenate([top, xr[SUBLANES:]], axis=0)


def _shift_up(x, halo, s, i, n):
    if s == 0:
        return x
    t = x.shape[0]
    halo = jnp.where(i == n - 1, 0.0, halo)
    xr = pltpu.roll(x, t - s, 0)
    hr = pltpu.roll(halo, SUBLANES - s, 0)
    row = lax.broadcasted_iota(jnp.int32, (SUBLANES, x.shape[1]), 0)
    bot = jnp.where(row >= SUBLANES - s, hr, xr[t - SUBLANES:])
    return jnp.concatenate([xr[:t - SUBLANES], bot], axis=0)


def _conv(pa, prev, cw_ref, i):
    y = pa * cw_ref[CONV_K - 1:CONV_K, :]
    for j in range(CONV_K - 1):
        y = y + _shift_down(pa, prev, CONV_K - 1 - j, i) * cw_ref[j:j + 1, :]
    return y


def _dot_nt(a, b, precision=None):
    return lax.dot_general(a, b, (((1,), (1,)), ((), ())), precision=precision, preferred_element_type=F32)


def _dot_tn(a, b, precision=None):
    return lax.dot_general(a, b, (((0,), (0,)), ((), ())), precision=precision, preferred_element_type=F32)


def _dot(a, b, precision=None):
    return jnp.dot(a, b, precision=precision, preferred_element_type=F32)


def _bf(x):
    return x.astype(BF16)


def _neumann_inverse(lmat):
    nb, c, _ = lmat.shape
    ri = lax.broadcasted_iota(jnp.int32, (nb, c, c), 1)
    ci = lax.broadcasted_iota(jnp.int32, (nb, c, c), 2)
    pw = -lmat
    inv = jnp.where(ri == ci, 1.0, 0.0).astype(F32) + pw
    for _ in range(5):
        pw = _mm3(pw, pw)
        inv = inv + _mm3(inv, pw)
    return inv


@jax.custom_vjp
def _unit_lower_inverse(lmat):
    return _neumann_inverse(lmat)


def _unit_lower_inverse_fwd(lmat):
    inv = _neumann_inverse(lmat)
    return inv, inv


def _unit_lower_inverse_bwd(inv, g):
    return (-_dg3(_dg3(inv, g, BTN), inv, BNT),)


_unit_lower_inverse.defvjp(_unit_lower_inverse_fwd, _unit_lower_inverse_bwd)


def _gdn_chunk(q, k, v, gc, bb):
    nb, c, _ = q.shape
    ri = lax.broadcasted_iota(jnp.int32, (nb, c, c), 1)
    ci = lax.broadcasted_iota(jnp.int32, (nb, c, c), 2)
    incl = ri >= ci
    strict = ri > ci
    g_row = gc[:, :, :c]
    g_col = _lane_mean_cols(gc, jnp.full((nb, c, LANES), 1.0 / LANES, F32))
    decay = jnp.where(incl, jnp.exp(jnp.where(incl, g_row - g_col, 0.0)), 0.0)
    kb = k * bb
    lmat = jnp.where(strict, _dg(_bf(kb), _bf(k), BNT) * decay, 0.0)
    inv = _unit_lower_inverse(lmat)
    egc = jnp.exp(gc)
    u = _mm3(inv, v * bb)
    w = _mm3(inv, kb * egc)
    aqk = _dg(_bf(q), _bf(k), BNT) * decay
    last = lax.broadcasted_iota(jnp.int32, (nb, c, LANES), 1) == c - 1
    tot = jnp.sum(jnp.where(last, gc, 0.0), axis=1, keepdims=True)
    k_tail = k * jnp.exp(tot - gc)
    tail = jnp.broadcast_to(jnp.exp(tot), (nb, SUBLANES, LANES))
    return u, w, aqk, q * egc, k_tail, tail


def _gdn_intra(qn, kn, vv, g_b, beta_b):
    t_rows = qn.shape[0]
    nc = t_rows // CHUNK
    cb = min(GDN_CB, nc)
    rows = cb * CHUNK
    col = pl.BlockSpec((rows, HEAD_DIM), lambda h, b: (b, h))

    def body(q_ref, k_ref, v_ref, g_ref, b_ref, u_ref, w_ref, a_ref, qd_ref, kt_ref, tl_ref):
        def group(gi, carry):
            r = pl.ds(pl.multiple_of(gi * (grp * CHUNK), grp * CHUNK), grp * CHUNK)
            ins = [ref[r, :].reshape(grp, CHUNK, HEAD_DIM) for ref in (q_ref, k_ref, v_ref, g_ref, b_ref)]
            u, w, aqk, qd, kt, tl = _gdn_chunk(*ins)
            for ref, val in ((u_ref, u), (w_ref, w), (qd_ref, qd), (kt_ref, kt)):
                ref[r, :] = val.reshape(grp * CHUNK, HEAD_DIM)
            a_ref[0, r, :] = aqk.reshape(grp * CHUNK, CHUNK)
            tl_ref[0, pl.ds(gi * grp, grp)] = tl
            return carry

        grp = min(GDN_GROUP, cb)
        lax.fori_loop(0, cb // grp, group, 0)

    full = jax.ShapeDtypeStruct((t_rows, HW), F32)
    return pl.pallas_call(
        body, name="gdn_intra_fwd",
        out_shape=[full, full, jax.ShapeDtypeStruct((HEADS, t_rows, CHUNK), F32), full, full,
                   jax.ShapeDtypeStruct((HEADS, nc, SUBLANES, LANES), F32)],
        grid=(HEADS, nc // cb),
        in_specs=[col] * 5,
        out_specs=[col, col, pl.BlockSpec((1, rows, CHUNK), lambda h, b: (h, b, 0)), col, col,
                   pl.BlockSpec((1, cb, SUBLANES, LANES), lambda h, b: (h, b, 0, 0))],
        compiler_params=_params(("parallel", "parallel")),
    )(qn, kn, vv, g_b, beta_b)


def _gdn_intra_bwd(qn, kn, vv, g_b, beta_b, du, dw, da, dqd, dkt, dtl):
    t_rows = qn.shape[0]
    nc = t_rows // CHUNK
    cb = min(GDN_CB, nc)
    rows = cb * CHUNK
    col = pl.BlockSpec((rows, HEAD_DIM), lambda h, b: (b, h))
    a_spec = pl.BlockSpec((1, rows, CHUNK), lambda h, b: (h, b, 0))
    tl_spec = pl.BlockSpec((1, cb, SUBLANES, LANES), lambda h, b: (h, b, 0, 0))

    def body(q_ref, k_ref, v_ref, g_ref, b_ref, du_ref, dw_ref, da_ref, dqd_ref, dkt_ref, dtl_ref,
             dq_ref, dk_ref, dv_ref, dg_ref, db_ref):
        def group(gi, carry):
            r = pl.ds(pl.multiple_of(gi * (grp * CHUNK), grp * CHUNK), grp * CHUNK)
            wide = (grp, CHUNK, HEAD_DIM)
            ins = [ref[r, :].reshape(wide) for ref in (q_ref, k_ref, v_ref, g_ref, b_ref)]
            cts = (du_ref[r, :].reshape(wide), dw_ref[r, :].reshape(wide),
                   da_ref[0, r, :].reshape(grp, CHUNK, CHUNK), dqd_ref[r, :].reshape(wide),
                   dkt_ref[r, :].reshape(wide), dtl_ref[0, pl.ds(gi * grp, grp)])
            grads = jax.vjp(_gdn_chunk, *ins)[1](cts)
            for ref, val in zip((dq_ref, dk_ref, dv_ref, dg_ref, db_ref), grads):
                ref[r, :] = val.reshape(grp * CHUNK, HEAD_DIM)
            return carry

        grp = min(GDN_GROUP, cb)
        lax.fori_loop(0, cb // grp, group, 0)

    full = jax.ShapeDtypeStruct((t_rows, HW), F32)
    return pl.pallas_call(
        body, name="gdn_intra_bwd",
        out_shape=[full] * 5,
        grid=(HEADS, nc // cb),
        in_specs=[col] * 7 + [a_spec, col, col, tl_spec],
        out_specs=[col] * 5,
        compiler_params=_params(("parallel", "parallel")),
    )(qn, kn, vv, g_b, beta_b, du, dw, da, dqd, dkt, dtl)


def _head_cols(h):
    return slice(h * HEAD_DIM, (h + 1) * HEAD_DIM)


def _gdn_scan(u, w, aqk, qd, kt, tl):
    t_rows = u.shape[0]
    nc = t_rows // CHUNK
    cb = min(GDN_CB, nc)
    rows = cb * CHUNK
    wide = pl.BlockSpec((rows, HW), lambda b: (b, 0))

    def body(u_ref, w_ref, a_ref, qd_ref, kt_ref, tl_ref, o_ref, s_out_ref, s_ref):
        @pl.when(pl.program_id(0) == 0)
        def _():
            s_ref[...] = jnp.zeros_like(s_ref)

        def chunk(ci, carry):
            r = pl.ds(pl.multiple_of(ci * CHUNK, CHUNK), CHUNK)
            for h in range(HEADS):
                hc = _head_cols(h)
                s = s_ref[h]
                s_out_ref[ci, h] = s
                sb = _bf(s)
                vn = u_ref[r, hc] - _dot(_bf(w_ref[r, hc]), sb)
                vnb = _bf(vn)
                o_ref[r, hc] = _dot(_bf(qd_ref[r, hc]), sb) + _dot(_bf(a_ref[h, r, :]), vnb)
                s_ref[h] = s * tl_ref[h, ci, 0:1, :] + _dot_tn(_bf(kt_ref[r, hc]), vnb)
            return carry

        lax.fori_loop(0, cb, chunk, 0, unroll=GDN_SCAN_UNROLL)

    return pl.pallas_call(
        body, name="gdn_scan_fwd",
        out_shape=[jax.ShapeDtypeStruct((t_rows, HW), F32),
                   jax.ShapeDtypeStruct((nc, HEADS, HEAD_DIM, HEAD_DIM), F32)],
        grid=(nc // cb,),
        in_specs=[wide, wide, pl.BlockSpec((HEADS, rows, CHUNK), lambda b: (0, b, 0)), wide, wide,
                  pl.BlockSpec((HEADS, cb, SUBLANES, LANES), lambda b: (0, b, 0, 0))],
        out_specs=[wide, pl.BlockSpec((cb, HEADS, HEAD_DIM, HEAD_DIM), lambda b: (b, 0, 0, 0))],
        scratch_shapes=[pltpu.VMEM((HEADS, HEAD_DIM, HEAD_DIM), F32)],
        compiler_params=_params(("arbitrary",)),
    )(u, w, aqk, qd, kt, tl)


def _gdn_scan_bwd(do, u, w, aqk, qd, kt, tl, states):
    t_rows = u.shape[0]
    nc = t_rows // CHUNK
    cb = min(GDN_CB, nc)
    rows = cb * CHUNK
    nb = nc // cb
    wide = pl.BlockSpec((rows, HW), lambda b: (nb - 1 - b, 0))
    a_spec = pl.BlockSpec((HEADS, rows, CHUNK), lambda b: (0, nb - 1 - b, 0))
    tl_spec = pl.BlockSpec((HEADS, cb, SUBLANES, LANES), lambda b: (0, nb - 1 - b, 0, 0))

    def body(do_ref, u_ref, w_ref, a_ref, qd_ref, kt_ref, tl_ref, s_in_ref,
             du_ref, dw_ref, da_ref, dqd_ref, dkt_ref, dtl_ref, ds_ref):
        @pl.when(pl.program_id(0) == 0)
        def _():
            ds_ref[...] = jnp.zeros_like(ds_ref)

        row0 = lax.broadcasted_iota(jnp.int32, (SUBLANES, LANES), 0) == 0

        def chunk(step, carry):
            ci = cb - 1 - step
            r = pl.ds(pl.multiple_of(ci * CHUNK, CHUNK), CHUNK)
            for h in range(HEADS):
                hc = _head_cols(h)
                s = s_in_ref[ci, h]
                ds_next = ds_ref[h]
                sb, dsb = _bf(s), _bf(ds_next)
                wb, ab, ktb, qdb = _bf(w_ref[r, hc]), _bf(a_ref[h, r, :]), _bf(kt_ref[r, hc]), _bf(qd_ref[r, hc])
                dob = _bf(do_ref[r, hc])
                vn = u_ref[r, hc] - _dot(wb, sb)
                vnb = _bf(vn)
                dvn = _dot_tn(ab, dob) + _dot(ktb, dsb)
                dvnb = _bf(dvn)
                du_ref[r, hc] = dvn
                dw_ref[r, hc] = -_dot_nt(dvnb, sb)
                da_ref[h, r, :] = _dot_nt(dob, vnb)
                dqd_ref[r, hc] = _dot_nt(dob, sb)
                dkt_ref[r, hc] = _dot_nt(vnb, dsb)
                dtl_ref[h, ci] = jnp.where(row0, _colsum(s * ds_next), 0.0)
                ds_ref[h] = _dot_tn(qdb, dob) + ds_next * tl_ref[h, ci, 0:1, :] - _dot_tn(wb, dvnb)
            return carry

        lax.fori_loop(0, cb, chunk, 0, unroll=GDN_SCAN_UNROLL)

    full = jax.ShapeDtypeStruct((t_rows, HW), F32)
    return pl.pallas_call(
        body, name="gdn_scan_bwd",
        out_shape=[full, full, jax.ShapeDtypeStruct((HEADS, t_rows, CHUNK), F32), full, full,
                   jax.ShapeDtypeStruct((HEADS, nc, SUBLANES, LANES), F32)],
        grid=(nb,),
        in_specs=[wide, wide, wide, a_spec, wide, wide, tl_spec,
                  pl.BlockSpec((cb, HEADS, HEAD_DIM, HEAD_DIM), lambda b: (nb - 1 - b, 0, 0, 0))],
        out_specs=[wide, wide, a_spec, wide, wide, tl_spec],
        scratch_shapes=[pltpu.VMEM((HEADS, HEAD_DIM, HEAD_DIM), F32)],
        compiler_params=_params(("arbitrary",)),
    )(do, u, w, aqk, qd, kt, tl, states)


def _att_profile_index():
    j = lax.broadcasted_iota(jnp.int32, (SUBLANES, ATT_KW), 1)
    return jnp.clip(ATT_PAD - j, -(CHUNK - 1), MAX_REL) + (CHUNK - 1)


def _att_far_back():
    qi = lax.broadcasted_iota(jnp.int32, (ATT_QB, ATT_KW), 0)
    kj = lax.broadcasted_iota(jnp.int32, (ATT_QB, ATT_KW), 1)
    return kj < qi


def _rotate_rows(x, forward):
    rows, lanes = x.shape
    row = lax.broadcasted_iota(jnp.int32, x.shape, 0)
    for bit in range(rows.bit_length() - 1):
        amount = (1 << bit) if forward else lanes - (1 << bit)
        x = jnp.where(jnp.bitwise_and(jnp.right_shift(row, bit), 1) == 1, pltpu.roll(x, amount, 1), x)
    return x


def _att_in_band():
    qi = lax.broadcasted_iota(jnp.int32, (ATT_QB, ATT_KW), 0)
    kj = lax.broadcasted_iota(jnp.int32, (ATT_QB, ATT_KW), 1)
    shift = CHUNK.bit_length() - 1
    qc = jnp.right_shift(qi, shift)
    kc = jnp.right_shift(kj, shift) - LEFT_CHUNKS
    return (kc <= qc) & (kc >= qc - LEFT_CHUNKS)


def _att_valid(b):
    kj = lax.broadcasted_iota(jnp.int32, (1, ATT_KW), 1)
    return jnp.where(kj + b * ATT_QB >= ATT_PAD, 0.0, NEG_INF)


def _rms_parts(x, w):
    r = lax.rsqrt(jnp.mean(x * x, axis=-1, keepdims=True) + EPS)
    xn = x * r
    return xn * w, xn, r


def _rms_bwd(dy, xn, r, w):
    dxn = dy * w
    dx = r * (dxn - xn * jnp.mean(dxn * xn, axis=-1, keepdims=True))
    return dx, _colsum(dy * xn)


def _att_probs(qb, kb, bias, before_start):
    s = _dot_nt(qb, kb) * (HEAD_DIM ** -0.5) + bias + before_start
    e = jnp.exp(s - jnp.max(s, axis=-1, keepdims=True))
    return e * (1.0 / jnp.sum(e, axis=-1, keepdims=True))


def _att_specs():
    q_spec = pl.BlockSpec((ATT_QB, HEAD_DIM), lambda h, b: (b, h))
    back = ATT_PAD // ATT_QB
    k_specs = [pl.BlockSpec((ATT_QB, HEAD_DIM), lambda h, b, j=j: (jnp.maximum(b + j - back, 0), HEADS + h))
               for j in range(3)]
    v_specs = [pl.BlockSpec((ATT_QB, HEAD_DIM), lambda h, b, j=j: (jnp.maximum(b + j - back, 0), 2 * HEADS + h))
               for j in range(3)]
    w_spec = pl.BlockSpec((1, HEAD_DIM), lambda h, b: (0, 0))
    smem = pl.BlockSpec(memory_space=pltpu.SMEM)
    return q_spec, k_specs, v_specs, w_spec, smem


BIAS_SPEC = pl.BlockSpec((1, ATT_QB, ATT_KW), lambda h, b: (h, 0, 0))


def _expand_rel_bias(rel):
    def body(rel_ref, bias_ref):
        h = pl.program_id(0)
        idx = _att_profile_index()

        def fill(r, acc):
            return jnp.where(idx == r, rel_ref[h, r], acc)

        profile = lax.fori_loop(0, N_REL, fill, jnp.zeros((SUBLANES, ATT_KW), F32))
        table = _rotate_rows(jnp.concatenate([profile] * (ATT_QB // SUBLANES), axis=0), True)
        table = jnp.where(_att_far_back(), rel_ref[h, N_REL - 1], table)
        bias_ref[0] = jnp.where(_att_in_band(), table, NEG_INF)

    return pl.pallas_call(
        body, name="rel_bias_expand",
        out_shape=jax.ShapeDtypeStruct((HEADS, ATT_QB, ATT_KW), F32), grid=(HEADS,),
        in_specs=[pl.BlockSpec(memory_space=pltpu.SMEM)],
        out_specs=pl.BlockSpec((1, ATT_QB, ATT_KW), lambda h: (h, 0, 0)),
        compiler_params=_params(("parallel",)),
    )(rel)


def _attention(pb, qw, kw, bias):
    t_rows = pb.shape[0]
    q_spec, k_specs, v_specs, w_spec, _ = _att_specs()

    def body(q_ref, k0, k1, k2, v0, v1, v2, qw_ref, kw_ref, bias_ref, o_ref):
        b = pl.program_id(1)
        kwin = jnp.concatenate([k0[...], k1[...], k2[...]], axis=0)
        vwin = jnp.concatenate([v0[...], v1[...], v2[...]], axis=0)
        q = _rms(q_ref[...], qw_ref[...])
        k = _rms(kwin, kw_ref[...])
        p = _att_probs(_bf(q), _bf(k), bias_ref[0], _att_valid(b))
        o_ref[...] = _dot(_bf(p), _bf(vwin)).astype(o_ref.dtype)

    return pl.pallas_call(
        body, name="band_attention_fwd",
        out_shape=jax.ShapeDtypeStruct((t_rows, HW), BF16),
        grid=(HEADS, t_rows // ATT_QB),
        in_specs=[q_spec] + k_specs + v_specs + [w_spec, w_spec, BIAS_SPEC],
        out_specs=pl.BlockSpec((ATT_QB, HEAD_DIM), lambda h, b: (b, h)),
        compiler_params=_params(("parallel", "arbitrary")),
    )(pb, pb, pb, pb, pb, pb, pb, qw, kw, bias)


def _attention_bwd(pb, qw, kw, bias, dyb):
    t_rows = pb.shape[0]
    nb = t_rows // ATT_QB
    q_spec, k_specs, v_specs, w_spec, smem = _att_specs()
    pad_rows = t_rows + ATT_PAD
    acc_spec = pl.BlockSpec((pad_rows, HEAD_DIM), lambda h, b: (0, h))

    def body(q_ref, k0, k1, k2, v0, v1, v2, qw_ref, kw_ref, bias_ref, do_ref,
             dq_ref, dk_ref, dv_ref, dqw_ref, dkw_ref, drel_ref, dbias_ref):
        h, b = pl.program_id(0), pl.program_id(1)

        @pl.when(b == 0)
        def _():
            dbias_ref[...] = jnp.zeros_like(dbias_ref)
            dk_ref[...] = jnp.zeros_like(dk_ref)
            dv_ref[...] = jnp.zeros_like(dv_ref)

        @pl.when((b == 0) & (h == 0))
        def _():
            dqw_ref[...] = jnp.zeros_like(dqw_ref)
            dkw_ref[...] = jnp.zeros_like(dkw_ref)

        kwin = jnp.concatenate([k0[...], k1[...], k2[...]], axis=0)
        vwin = jnp.concatenate([v0[...], v1[...], v2[...]], axis=0)
        scale = HEAD_DIM ** -0.5
        qw_, kw_ = qw_ref[...], kw_ref[...]
        q, qn, rq = _rms_parts(q_ref[...], qw_)
        k, kn, rk = _rms_parts(kwin, kw_)
        qb, kb, dob = _bf(q), _bf(k), _bf(do_ref[...])
        p = _att_probs(qb, kb, bias_ref[0], _att_valid(b))
        dp = _dot_nt(dob, _bf(vwin))
        ds = p * (dp - jnp.sum(p * dp, axis=-1, keepdims=True))
        dbias_ref[...] += ds
        ds = _bf(ds)
        dq, dqw = _rms_bwd(_dot(ds, kb) * scale, qn, rq, qw_)
        dk, dkw = _rms_bwd(_dot_tn(ds, qb) * scale, kn, rk, kw_)
        dq_ref[...] = dq.astype(dq_ref.dtype)
        win = pl.ds(pl.multiple_of(b * ATT_QB, ATT_QB), ATT_KW)
        dk_ref[win, :] += dk
        dv_ref[win, :] += _dot_tn(_bf(p), dob)
        dqw_ref[...] += dqw
        dkw_ref[...] += dkw

        @pl.when(b == nb - 1)
        def _():
            tot, far = dbias_ref[...], _att_far_back()
            far_sum = jnp.sum(jnp.where(far, tot, 0.0))
            per_offset = _colsum(_rotate_rows(jnp.where(far, 0.0, tot), False))
            idx = _att_profile_index()
            first_row = lax.broadcasted_iota(jnp.int32, idx.shape, 0) == 0
            spread = jnp.where(first_row, per_offset, 0.0)

            def reduce(r, carry):
                drel_ref[h, r] = jnp.sum(jnp.where(idx == r, spread, 0.0)) + jnp.where(r == N_REL - 1, far_sum, 0.0)
                return carry

            lax.fori_loop(0, N_REL, reduce, 0)

    return pl.pallas_call(
        body, name="band_attention_bwd",
        out_shape=[jax.ShapeDtypeStruct((t_rows, HW), BF16),
                   jax.ShapeDtypeStruct((pad_rows, HW), F32), jax.ShapeDtypeStruct((pad_rows, HW), F32),
                   jax.ShapeDtypeStruct((1, HEAD_DIM), F32), jax.ShapeDtypeStruct((1, HEAD_DIM), F32),
                   jax.ShapeDtypeStruct((HEADS, N_REL), F32)],
        grid=(HEADS, nb),
        in_specs=[q_spec] + k_specs + v_specs + [w_spec, w_spec, BIAS_SPEC, q_spec],
        out_specs=[q_spec, acc_spec, acc_spec, w_spec, w_spec, smem],
        scratch_shapes=[pltpu.VMEM((ATT_QB, ATT_KW), F32)],
        compiler_params=_params(("arbitrary", "arbitrary")),
    )(pb, pb, pb, pb, pb, pb, pb, qw, kw, bias, dyb)


def _me():
    return lax.axis_index("x"), lax.axis_index("y"), lax.axis_index("c")


def _index(x, y, c):
    return 4 * x + 2 * y + c


HBM_SPEC = pl.BlockSpec(memory_space=pl.ANY)


def _block(ref, kind, d, r, c):
    if kind == "all":
        return ref
    if kind == "rows":
        return ref.at[pl.ds(d * r, r), :]
    if kind == "win":
        return ref.at[:, pl.ds(d * WIN_STEP, c)]
    return ref.at[:, pl.ds(d * c, c)]


def _all_gather(shards, kinds, n_gather):
    n = len(shards)

    def body(*refs):
        x_refs, out_refs = refs[:n], refs[n:2 * n]
        send_sems, recv_sems, local_sems = refs[2 * n:]
        x, y, c = _me()
        me, sibling = (x, y, c), (x, y, 1 - c)
        chips = [(1 - x, y), (x, 1 - y), (1 - x, 1 - y)]

        def copy(i, k, blk, to, src=None):
            r_, c_ = shards[i].shape
            dst = _block(out_refs[i], kinds[i], _index(*blk), r_, c_)
            return pltpu.make_async_remote_copy(
                src_ref=dst if src is None else src, dst_ref=dst,
                send_sem=send_sems.at[i, k], recv_sem=recv_sems.at[i, k], device_id=to, device_id_type=MESH)

        sends, local = [], []
        for i in range(n):
            r_, c_ = shards[i].shape
            mine = pltpu.make_async_copy(x_refs[i], _block(out_refs[i], kinds[i], _index(*me), r_, c_),
                                         local_sems.at[i])
            mine.start()
            local.append(mine)
            if i >= n_gather:
                continue
            first = [copy(i, 0, me, sibling, src=x_refs[i])]
            first += [copy(i, 1 + j, me, (*chip, c), src=x_refs[i]) for j, chip in enumerate(chips)]
            for cp in first:
                cp.start()
            sends += first
        for i in range(n_gather):
            for j, chip in enumerate(chips):
                copy(i, 1 + j, (*chip, c), me).wait_recv()
                passed = copy(i, 4 + j, (*chip, c), sibling)
                passed.start()
                sends.append(passed)
        for i in range(n_gather):
            copy(i, 0, sibling, me).wait_recv()
            for j, chip in enumerate(chips):
                copy(i, 4 + j, (*chip, 1 - c), me).wait_recv()
        for cp in sends:
            cp.wait_send()
        for cp in local:
            cp.wait()

    def full_shape(s, kind):
        r_, c_ = s.shape
        return (N_DEV * r_, c_) if kind == "rows" else (r_, N_DEV * c_)

    return pl.pallas_call(
        body, name="weights_all_gather",
        out_shape=[jax.ShapeDtypeStruct(full_shape(s, k), s.dtype) for s, k in zip(shards, kinds)],
        in_specs=[HBM_SPEC] * n, out_specs=[HBM_SPEC] * n,
        scratch_shapes=[pltpu.SemaphoreType.DMA((n_gather, 7)), pltpu.SemaphoreType.DMA((n_gather, 7)),
                        pltpu.SemaphoreType.DMA((n,))],
        compiler_params=pltpu.CompilerParams(has_side_effects=True),
    )(*shards)


SEM_SPEC = pl.BlockSpec(memory_space=pltpu.SEMAPHORE)
HBM_ONLY = pl.BlockSpec(memory_space=pltpu.HBM)
DATAFLOW = pltpu.SideEffectType.DATAFLOW_SIDE_EFFECTING


def _peers():
    x, y, c = _me()
    return [(x ^ (k >> 2), y ^ ((k >> 1) & 1), c ^ (k & 1)) for k in range(1, N_DEV)]


def _gather_copies(shapes, kinds):
    def make(src_refs, land_refs, send_sems, recv_sems):
        mine = _index(*_me())
        return [pltpu.make_async_remote_copy(
            src_ref=src_refs[i], dst_ref=_block(land_refs[i], kind, mine, r, c),
            send_sem=send_sems.at[7 * i + k], recv_sem=recv_sems.at[7 * i + k], device_id=peer, device_id_type=MESH)
            for i, ((r, c), kind) in enumerate(zip(shapes, kinds)) for k, peer in enumerate(_peers())]

    return make


def _exchange_copies(shapes, kinds):
    def make(src_refs, land_refs, send_sems, recv_sems):
        mine = _index(*_me())
        return [pltpu.make_async_remote_copy(
            src_ref=_block(src_refs[i], kind, _index(*peer), r, c), dst_ref=land_refs[i].at[mine],
            send_sem=send_sems.at[7 * i + k], recv_sem=recv_sems.at[7 * i + k], device_id=peer, device_id_type=MESH)
            for i, ((r, c), kind) in enumerate(zip(shapes, kinds)) for k, peer in enumerate(_peers())]

    return make


def _place_block(shard, kind, name):
    r, c = shard.shape
    tile = _row_tile(r, c)
    nt = r // tile
    full = (N_DEV * r, c) if kind == "rows" else (r, N_DEV * c)

    def body(me_ref, x_ref, out_ref):
        out_ref[...] = x_ref[...]

    if kind == "rows":
        out_spec = pl.BlockSpec((tile, c), lambda i, me: (me[0] * nt + i, 0))
    else:
        out_spec = pl.BlockSpec((tile, c), lambda i, me: (i, me[0]))
    return pl.pallas_call(
        body, name=name, out_shape=jax.ShapeDtypeStruct(full, shard.dtype),
        grid_spec=pltpu.PrefetchScalarGridSpec(
            num_scalar_prefetch=1, grid=(nt,),
            in_specs=[pl.BlockSpec((tile, c), lambda i, me: (i, 0))], out_specs=out_spec),
        compiler_params=_params(("arbitrary",)),
    )(_my_index_operand(), shard)


def _split_start(srcs, lands, make, name):
    n = len(srcs)

    def body(*refs):
        send_sems, recv_sems = refs[2 * n], refs[2 * n + 1]
        for cp in make(refs[:n], refs[n:2 * n], send_sems, recv_sems):
            cp.start()
        refs[-1][...] = jnp.zeros_like(refs[-1])

    arrays = list(srcs) + list(lands)
    out = pl.pallas_call(
        body, name=name,
        out_shape=(pltpu.SemaphoreType.DMA((7 * n,)), pltpu.SemaphoreType.DMA((7 * n,)),
                   *[pltpu.HBM(a.shape, a.dtype) for a in arrays], jax.ShapeDtypeStruct((SUBLANES, LANES), F32)),
        in_specs=[HBM_ONLY] * (2 * n),
        out_specs=(SEM_SPEC, SEM_SPEC, *[HBM_ONLY] * (2 * n), pl.BlockSpec(memory_space=pltpu.VMEM)),
        input_output_aliases={i: 2 + i for i in range(2 * n)},
        compiler_params=pltpu.CompilerParams(has_side_effects=DATAFLOW),
    )(*[pltpu.with_memory_space_constraint(a, pltpu.HBM) for a in arrays])
    return out[0], out[1], list(out[2:2 + n]), list(out[2 + n:2 + 2 * n]), out[-1]


def _split_wait(send_sems, recv_sems, srcs, lands, after, make, name):
    n = len(srcs)

    def body(*refs):
        for cp in make(refs[:n], refs[n:2 * n], refs[2 * n], refs[2 * n + 1]):
            cp.wait_send()
            cp.wait_recv()

    arrays = list(srcs) + list(lands)
    out = pl.pallas_call(
        body, name=name,
        out_shape=tuple(pltpu.HBM(a.shape, a.dtype) for a in arrays),
        in_specs=[HBM_ONLY] * (2 * n) + [SEM_SPEC, SEM_SPEC, pl.BlockSpec(memory_space=pl.ANY)],
        out_specs=tuple([HBM_ONLY] * (2 * n)),
        input_output_aliases={i: i for i in range(2 * n)},
        compiler_params=pltpu.CompilerParams(has_side_effects=DATAFLOW),
    )(*arrays, send_sems, recv_sems, after)
    return list(out[:n]), list(out[n:])


def _all_reduce_small(vals, name):
    rows, width = vals.shape

    def body(x_ref, out_ref, buf_ref, send_sems, recv_sems):
        x, y, c = _me()
        mine = _index(x, y, c)
        buf_ref[mine] = x_ref[...]
        copies = []
        for k in range(1, N_DEV):
            px, py, pc = x ^ (k >> 2), y ^ ((k >> 1) & 1), c ^ (k & 1)
            copies.append(pltpu.make_async_remote_copy(
                src_ref=x_ref, dst_ref=buf_ref.at[mine],
                send_sem=send_sems.at[k - 1], recv_sem=recv_sems.at[k - 1],
                device_id=(px, py, pc), device_id_type=MESH))
        for cp in copies:
            cp.start()
        for cp in copies:
            cp.wait()
        acc = buf_ref[0]
        for j in range(1, N_DEV):
            acc = acc + buf_ref[j]
        out_ref[...] = acc

    vmem = pl.BlockSpec(memory_space=pltpu.VMEM)
    return pl.pallas_call(
        body, name=name,
        out_shape=jax.ShapeDtypeStruct(vals.shape, F32),
        in_specs=[vmem], out_specs=vmem,
        scratch_shapes=[pltpu.VMEM((N_DEV, rows, width), F32),
                        pltpu.SemaphoreType.DMA((7,)), pltpu.SemaphoreType.DMA((7,))],
        compiler_params=pltpu.CompilerParams(has_side_effects=True),
    )(vals)


def _adamw_math(w, g, m, v):
    m = ADAM_B1 * m + (1.0 - ADAM_B1) * g
    v = ADAM_B2 * v + (1.0 - ADAM_B2) * (g * g)
    m_hat = m / (1.0 - ADAM_B1 ** ADAM_STEP)
    v_hat = v / (1.0 - ADAM_B2 ** ADAM_STEP)
    delta = -ADAM_LR * (m_hat / (jnp.sqrt(v_hat) + ADAM_EPS) + ADAM_WD * w)
    return delta, m, v


ROW_TILE_ELEMS = 384 * 1024


def _row_tile(rows, width):
    best = SUBLANES
    for t in range(SUBLANES, rows + 1, SUBLANES):
        if rows % t == 0 and t * width <= ROW_TILE_ELEMS:
            best = t
    return best


def _sum_received(r_ref, own, me):
    g = None
    for j in range(N_DEV):
        term = jnp.where(me == j, own, r_ref[j].astype(F32))
        g = term if g is None else g + term
    return g


def _my_index_operand():
    return _index(*_me()).astype(jnp.int32).reshape(1)


def _sum_small(recv, own):
    def body(me_ref, r_ref, own_ref, out_ref):
        out_ref[...] = _sum_received(r_ref, own_ref[...], me_ref[0])

    whole = lambda shape: pl.BlockSpec(shape, lambda i, me, nd=len(shape): (0,) * nd)
    return pl.pallas_call(
        body, name="small_grads_sum", out_shape=jax.ShapeDtypeStruct(own.shape, F32),
        grid_spec=pltpu.PrefetchScalarGridSpec(
            num_scalar_prefetch=1, grid=(1,), in_specs=[whole(recv.shape), whole(own.shape)],
            out_specs=whole(own.shape)),
        compiler_params=_params(("arbitrary",)),
    )(_my_index_operand(), recv, own)


def _adamw_recv(recv, grad, kind, w, m, v, name):
    _, rows, width = recv.shape
    tile = _row_tile(rows, width)
    nt = rows // tile

    def body(me_ref, r_ref, own_ref, w_ref, m_ref, v_ref, g_out, d_out, m_out, v_out):
        g = _sum_received(r_ref, own_ref[...].astype(F32), me_ref[0])
        d, mn, vn = _adamw_math(w_ref[...], g, m_ref[...], v_ref[...])
        g_out[...] = g
        d_out[...] = d
        m_out[...] = mn
        v_out[...] = vn

    if kind == "rows":
        own_spec = pl.BlockSpec((tile, width), lambda i, me: (me[0] * nt + i, 0))
    else:
        own_spec = pl.BlockSpec((tile, width), lambda i, me: (i, me[0]))
    spec = pl.BlockSpec((tile, width), lambda i, me: (i, 0))
    shape = jax.ShapeDtypeStruct((rows, width), F32)
    return pl.pallas_call(
        body, name=name, out_shape=[shape] * 4,
        grid_spec=pltpu.PrefetchScalarGridSpec(
            num_scalar_prefetch=1, grid=(nt,),
            in_specs=[pl.BlockSpec((N_DEV, tile, width), lambda i, me: (0, i, 0)), own_spec, spec, spec, spec],
            out_specs=[spec] * 4),
        compiler_params=_params(("parallel",)),
    )(_my_index_operand(), recv, grad, w, m, v)


WIN_STEP = 1408
WIN_W = 1536
IN_SHARD = IN_COLS // N_DEV
IN_PADDED = WIN_STEP * (N_DEV - 1) + WIN_W


def _roll_w_in(shard_padded):
    rows = shard_padded.shape[0]
    tile = _row_tile(rows, WIN_W)

    def body(x_ref, main_ref, edge_ref):
        win = pltpu.roll(x_ref[...], 2 * _index(*_me()), 1).astype(BF16)
        main_ref[...] = win[:, :WIN_STEP]
        edge_ref[...] = win[:, WIN_STEP:]

    return pl.pallas_call(
        body, name="w_in_window",
        out_shape=[jax.ShapeDtypeStruct((rows, WIN_STEP), BF16), jax.ShapeDtypeStruct((rows, WIN_W - WIN_STEP), BF16)],
        grid=(rows // tile,),
        in_specs=[pl.BlockSpec((tile, WIN_W), lambda i: (i, 0))],
        out_specs=[pl.BlockSpec((tile, WIN_STEP), lambda i: (i, 0)),
                   pl.BlockSpec((tile, WIN_W - WIN_STEP), lambda i: (i, 0))],
        compiler_params=_params(("parallel",)),
    )(shard_padded)


def _sum_w_in_windows(recv, grad):
    _, rows, width = recv.shape
    tile = _row_tile(rows, width)

    def body(me_ref, r_ref, g_ref, g_out, own_ref, sem):
        me = me_ref[0]
        rows_i = pl.ds(pl.multiple_of(pl.program_id(0) * tile, tile), tile)
        own = pltpu.make_async_copy(g_ref.at[rows_i, pl.ds(pl.multiple_of(me * WIN_STEP, LANES), width)], own_ref, sem)
        own.start()
        own.wait()
        g_out[...] = pltpu.roll(_sum_received(r_ref, own_ref[...].astype(F32), me), width - 2 * me, 1)

    return pl.pallas_call(
        body, name="w_in_grad_sum", out_shape=jax.ShapeDtypeStruct((rows, width), F32),
        grid_spec=pltpu.PrefetchScalarGridSpec(
            num_scalar_prefetch=1, grid=(rows // tile,),
            in_specs=[pl.BlockSpec((N_DEV, tile, width), lambda i, me: (0, i, 0)), HBM_SPEC],
            out_specs=pl.BlockSpec((tile, width), lambda i, me: (i, 0)),
            scratch_shapes=[pltpu.VMEM((tile, width), BF16), pltpu.SemaphoreType.DMA]),
        compiler_params=_params(("arbitrary",)),
    )(_my_index_operand(), recv, grad)


def _adamw_small(w, g, m, v, name):
    def fn(i, n, w_, g_, m_, v_):
        return _adamw_math(w_, g_, m_, v_)

    r, c = w.shape
    return _rows(fn, [(w, "t"), (g, "t"), (m, "t"), (v, "t")], [], [(c, F32)] * 3, [], _row_tile(r, c), name)


def _norm_fwd(x, w, name):
    return _rows(lambda i, n, x_, w_: (_rms(x_, w_[...]),), [(x, "t")], [w], [(D_MODEL, BF16)], [], 512, name)[0]


def _residual_norm_fwd(x, y, scale, w, name):
    def fn(i, n, x_, y_, w_):
        xn = x_ + scale * y_
        return xn, _rms(xn, w_[...])

    return _rows(fn, [(x, "t"), (y, "t")], [w], [(D_MODEL, F32), (D_MODEL, BF16)], [], 512, name)


def _residual_norm_bwd(x, w, dhs, dres, scale, name):
    nh = len(dhs)

    def fn(i, n, x_, dres_, *rest):
        dh = rest[0]
        for extra in rest[1:nh]:
            dh = dh + extra
        _, vjp = jax.vjp(_rms, x_, rest[nh][...])
        dx, dw = vjp(dh)
        dx = dx + dres_
        return dx, scale * dx, dw

    return _rows(fn, [(x, "t"), (dres, "t")] + [(d, "t") for d in dhs], [w],
                 [(D_MODEL, F32), (D_MODEL, BF16)], [(1, D_MODEL)], 256, name)


FFN_UP_TN = 512


def _ffn_up(h, w_gu, name):
    t, d = h.shape
    f = w_gu.shape[1] // 2
    tm = _pick(t, (1024, 512, 256, 128))
    nj = f // FFN_UP_TN

    def body(h_ref, wg_ref, wu_ref, g_ref, u_ref, act_ref):
        hb = h_ref[...]
        g = jnp.dot(hb, wg_ref[...], preferred_element_type=F32)
        u = jnp.dot(hb, wu_ref[...], preferred_element_type=F32)
        g_ref[...] = g.astype(BF16)
        u_ref[...] = u.astype(BF16)
        act_ref[...] = (_silu(g) * u).astype(BF16)

    out = pl.BlockSpec((tm, FFN_UP_TN), lambda i, j: (i, j))
    return pl.pallas_call(
        body, name=name, out_shape=[jax.ShapeDtypeStruct((t, f), BF16)] * 3, grid=(t // tm, nj),
        in_specs=[pl.BlockSpec((tm, d), lambda i, j: (i, 0)),
                  pl.BlockSpec((d, FFN_UP_TN), lambda i, j: (0, j)),
                  pl.BlockSpec((d, FFN_UP_TN), lambda i, j: (0, j + nj))],
        out_specs=[out, out, out],
        compiler_params=_params(("parallel", "parallel")),
    )(h, w_gu, w_gu)


def _ffn_fwd(h, w_gu, get_w_down, tag):
    g, u, act = _ffn_up(h, w_gu, tag + "_gu")
    y = _matmul(act, get_w_down(act), "nn", F32, tag + "_down")
    return (g, u), act, y


def _ffn_bwd(h, gu, act, dy, w_gu, w_down, tag, comm, more=None):
    dact = _matmul(dy, w_down, "nt", BF16, tag + "_dact")

    def fn(i, n, g_, u_, dact_):
        _, vjp = jax.vjp(lambda a, b: _silu(a) * b, g_, u_)
        return (jnp.concatenate(vjp(dact_), axis=1),)

    dgu = _rows(fn, [(gu[0], "t"), (gu[1], "t"), (dact, "t")], [], [(2 * D_FF, BF16)], [], 128,
                tag + "_swiglu_bwd")[0]
    sent = comm.send(tag + "_gu", {tag + "_w_gu": _matmul(h, dgu, "tn", BF16, tag + "_d_w_gu")})
    sent = sent + comm.send(tag + "_down", {tag + "_w_down": _matmul(act, dy + sent.astype(BF16), "tn", BF16,
                                                                    tag + "_d_w_down"), **(more or {})})
    dh = _matmul(dgu, w_gu, "nt", BF16, tag + "_dh")
    return dh, sent


def _expanders():
    e_g = np.zeros((LANES, HW), np.float32)
    e_b = np.zeros((LANES, HW), np.float32)
    for h in range(HEADS):
        e_g[h, h * HEAD_DIM:(h + 1) * HEAD_DIM] = 1.0
        e_b[HEADS + h, h * HEAD_DIM:(h + 1) * HEAD_DIM] = 1.0
    return jnp.asarray(e_g), jnp.asarray(e_b)


def _pad_lanes(v):
    return jnp.pad(v, ((0, 0), (0, LANES - v.shape[1])))


class _LocalWeights:
    def __init__(self, big):
        self.big, self.sent = big, {}

    def arrive(self, group, after):
        return self.big

    def send(self, group, grads):
        self.sent.update(grads)
        return jnp.zeros((), F32)


def _local_step(x, p, tgt, small, comm):
    e_g, e_b = _expanders()
    alog, dtb = _pad_lanes(small["a_log"]), _pad_lanes(small["dt_bias"])
    conv_w = jnp.pad(small["conv_w"], ((0, SUBLANES - CONV_K), (0, 0)))
    rel = _expand_rel_bias(small["rel_bias"])

    h1 = _norm_fwd(x, small["ffn1_norm"], "ffn1_norm")
    big = dict(comm.arrive("ffn1", h1))
    if "_token" in big:
        h1 = h1 + big.pop("_token").astype(BF16)

    def ffn1_w_down(act):
        big.update(comm.arrive("ffn1_down", act))
        return big["ffn1_w_down"]

    gu1, act1, y1 = _ffn_fwd(h1, big["ffn1_w_gu"], ffn1_w_down, "ffn1")
    x1, h2 = _residual_norm_fwd(x, y1, 0.5, small["mix_norm"], "mix_norm")

    big = {**big, **comm.arrive("mixer", h2)}
    w_in = big["w_in"]
    w_qz = w_in[:, :IN_QZ]
    w_ab = jnp.pad(w_in[:, IN_AB0:IN_QKVB0], ((0, 0), (0, LANES - 2 * HEADS)))
    w_qkvb = w_in[:, IN_QKVB0:IN_GG0]
    w_gg = w_in[:, IN_GG0:IN_COLS]
    qz = _matmul(h2, w_qz, "nn", F32, "in_qz")
    ab = _matmul(h2, w_ab, "nn", F32, "in_ab")
    pb = _matmul(h2, w_qkvb, "nn", F32, "in_qkvb")
    gg = _matmul(h2, w_gg, "nn", BF16, "in_gates")
    pa, z = qz[:, :3 * HW], qz[:, 3 * HW:]

    def prep(i, n, pa_, prev_, ab_, cw_, alog_, dtb_, eg_, eb_):
        q, k, v = _gdn_post(_conv(pa_, prev_, cw_, i))
        g_b, beta_b = _gdn_gates(ab_, alog_[...], dtb_[...], eg_[...], eb_[...])
        return q, k, v, g_b, beta_b

    qn, kn, vv, g_b, beta_b = _rows(prep, [(pa, "t"), (pa, "p"), (ab, "t")], [conv_w, alog, dtb, e_g, e_b],
                                    [(HW, F32)] * 5, [], 256, "gdn_prep")
    u, w, aqk, qd, kt, tl = _gdn_intra(qn, kn, vv, g_b, beta_b)
    o, states = _gdn_scan(u, w, aqk, qd, kt, tl)
    ya = _rows(lambda i, n, o_, z_, w_: (_gated_norm(o_, z_, w_[...]),), [(o, "t"), (z, "t")], [small["gdn_norm"]],
               [(HW, BF16)], [], 512, "gdn_gated_norm")[0]

    yb = _attention(pb, small["q_norm"], small["k_norm"], rel)

    big = {**big, **comm.arrive("branches", yb)}
    ta = _matmul(ya, big["w_branch_a"], "nn", BF16, "branch_a")
    tb = _matmul(yb, big["w_branch_b"], "nn", BF16, "branch_b")
    mixed = _rows(lambda i, n, gg_, ta_, tb_: (_mix(gg_, ta_, tb_),), [(gg, "t"), (ta, "t"), (tb, "t")], [],
                  [(D_MODEL, BF16)], [], 256, "mix")[0]
    m_out = _matmul(mixed, big["w_out"], "nn", F32, "w_out")
    x2, h3 = _residual_norm_fwd(x1, m_out, 1.0, small["ffn2_norm"], "ffn2_norm")
    big = {**big, **comm.arrive("tail", h3)}
    gu2, act2, y2 = _ffn_fwd(h3, big["ffn2_w_gu"], lambda act: big["ffn2_w_down"], "ffn2")
    x3, h4 = _residual_norm_fwd(x2, y2, 0.5, small["ple_norm"], "ple_norm")
    gp = _matmul(h4, big["ple_gate"], "nn", BF16, "ple_gate")
    pp = _matmul(p, big["ple_proj"], "nn", BF16, "ple_proj")

    def head(i, n, x3_, gp_, pp_, tgt_):
        sg = _sigmoid(gp_)
        err = x3_ + sg * pp_ - tgt_
        dx4 = err * (1.0 / D_MODEL)
        sq = _colsum(err * err)
        part = sq[:, :LANES]
        for j in range(1, D_MODEL // LANES):
            part = part + sq[:, j * LANES:(j + 1) * LANES]
        return dx4, dx4 * pp_ * sg * (1.0 - sg), dx4 * sg, (0.5 / D_MODEL) * part

    dx4, dgp, dpp, loss_lanes = _rows(head, [(x3, "t"), (gp, "t"), (pp, "t"), (tgt, "t")], [],
                                      [(D_MODEL, F32), (D_MODEL, BF16), (D_MODEL, BF16)], [(1, LANES)], 256,
                                      "ple_loss_head")
    loss = jnp.sum(loss_lanes)

    gbig, gsmall = {}, {}
    gbig["ple_proj"] = _matmul(p, dpp, "tn", BF16, "d_ple_proj")
    gbig["ple_gate"] = _matmul(h4, dgp, "tn", BF16, "d_ple_gate")
    dh4 = _matmul(dgp, big["ple_gate"], "nt", BF16, "ple_gate_dh")
    dx3, dy2, gsmall["ple_norm"] = _residual_norm_bwd(x3, small["ple_norm"], [dh4], dx4, 0.5, "ple_norm_bwd")

    dh3, sent = _ffn_bwd(h3, gu2, act2, dy2, big["ffn2_w_gu"], big["ffn2_w_down"], "ffn2", comm,
                         {n: gbig[n] for n in ("ple_proj", "ple_gate")})
    dx2, dx2b, gsmall["ffn2_norm"] = _residual_norm_bwd(x2, small["ffn2_norm"] + sent, [dh3], dx3, 1.0,
                                                        "ffn2_norm_bwd")

    gbig["w_out"] = _matmul(mixed, dx2b, "tn", BF16, "d_w_out")
    dmixed = _matmul(dx2b, big["w_out"], "nt", BF16, "w_out_dx")

    def mix_bwd(i, n, gg_, ta_, tb_, dm_):
        _, vjp = jax.vjp(_mix, gg_, ta_, tb_)
        return vjp(dm_)

    dgg, dta, dtb_ = _rows(mix_bwd, [(gg, "t"), (ta, "t"), (tb, "t"), (dmixed, "t")], [],
                           [(2 * D_MODEL, BF16), (D_MODEL, BF16), (D_MODEL, BF16)], [], 256, "mix_bwd")
    gbig["w_branch_a"] = _matmul(ya, dta, "tn", BF16, "d_branch_a")
    gbig["w_branch_b"] = _matmul(yb, dtb_, "tn", BF16, "d_branch_b")
    dya = _matmul(dta, big["w_branch_a"], "nt", BF16, "branch_a_dx")
    dyb = _matmul(dtb_, big["w_branch_b"], "nt", BF16, "branch_b_dx")

    dq_b, dk_b, dv_b, gsmall["q_norm"], gsmall["k_norm"], gsmall["rel_bias"] = _attention_bwd(
        pb, small["q_norm"], small["k_norm"], rel, dyb)
    dpb = jnp.concatenate([dq_b, dk_b[ATT_PAD:].astype(BF16), dv_b[ATT_PAD:].astype(BF16)], axis=1)

    def gated_bwd(i, n, o_, z_, dya_, w_):
        _, vjp = jax.vjp(_gated_norm, o_, z_, w_[...])
        return vjp(dya_)

    do, dz, gsmall["gdn_norm"] = _rows(gated_bwd, [(o, "t"), (z, "t"), (dya, "t")], [small["gdn_norm"]],
                                       [(HW, F32), (HW, BF16)], [(1, HEAD_DIM)], 256, "gdn_gated_norm_bwd")
    du, dw, da, dqd, dkt, dtl = _gdn_scan_bwd(do, u, w, aqk, qd, kt, tl, states)
    dqn, dkn, dvv, dg_b, dbeta_b = _gdn_intra_bwd(qn, kn, vv, g_b, beta_b, du, dw, da, dqd, dkt, dtl)

    def prep_bwd(i, n, pa_, prev_, ab_, dq_, dk_, dv_, dg_, db_, cw_, alog_, dtb_, eg_, eb_):
        _, vjp = jax.vjp(_gdn_post, _conv(pa_, prev_, cw_, i))
        (dy,) = vjp((dq_, dk_, dv_))
        e_g_, e_b_ = eg_[...], eb_[...]
        _, vjp_g = jax.vjp(lambda a, b, c: _gdn_gates(a, b, c, e_g_, e_b_), ab_, alog_[...], dtb_[...])
        dab, dalog, ddtb = vjp_g((dg_, db_))
        return dy, dab, dalog, ddtb

    dy_conv, dab, dalog, ddtb = _rows(
        prep_bwd, [(pa, "t"), (pa, "p"), (ab, "t"), (dqn, "t"), (dkn, "t"), (dvv, "t"), (dg_b, "t"), (dbeta_b, "t")],
        [conv_w, alog, dtb, e_g, e_b], [(3 * HW, F32), (LANES, BF16)], [(1, LANES), (1, LANES)], 256,
        "gdn_prep_bwd")
    gsmall["a_log"] = dalog[:, :HEADS]
    gsmall["dt_bias"] = ddtb[:, :HEADS]

    def conv_bwd(i, n, dy_, nxt_, pa_, prev_, cw_):
        dpa = dy_ * cw_[CONV_K - 1:CONV_K, :]
        row = lax.broadcasted_iota(jnp.int32, (SUBLANES, dy_.shape[1]), 0)
        dcw = jnp.where(row == CONV_K - 1, _colsum(dy_ * pa_), 0.0)
        for j in range(CONV_K - 1):
            s = CONV_K - 1 - j
            dpa = dpa + _shift_up(dy_, nxt_, s, i, n) * cw_[j:j + 1, :]
            dcw = dcw + jnp.where(row == j, _colsum(dy_ * _shift_down(pa_, prev_, s, i)), 0.0)
        return dpa, dcw

    dpa, dcw = _rows(conv_bwd, [(dy_conv, "t"), (dy_conv, "n"), (pa, "t"), (pa, "p")], [conv_w],
                     [(3 * HW, BF16)], [(SUBLANES, 3 * HW)], 256, "gdn_conv_bwd")
    gsmall["conv_w"] = dcw[:CONV_K]

    dqz = jnp.concatenate([dpa, dz], axis=1)
    d_w_qz = _matmul(h2, dqz, "tn", BF16, "d_in_qz")
    d_w_ab = _matmul(h2, dab, "tn", BF16, "d_in_ab")
    d_w_qkvb = _matmul(h2, dpb, "tn", BF16, "d_in_qkvb")
    d_w_gg = _matmul(h2, dgg, "tn", BF16, "d_in_gates")
    gbig["w_in"] = jnp.concatenate([d_w_qz, d_w_ab[:, :2 * HEADS], d_w_qkvb, d_w_gg,
                                    jnp.zeros((D_MODEL, IN_PADDED - IN_COLS), BF16)], axis=1)
    dh2 = [_matmul(dqz, w_qz, "nt", BF16, "in_qz_dh"), _matmul(dab, w_ab, "nt", BF16, "in_ab_dh"),
           _matmul(dpb, w_qkvb, "nt", BF16, "in_qkvb_dh"), _matmul(dgg, w_gg, "nt", BF16, "in_gates_dh")]
    sent = comm.send("mixer", {n: gbig[n] for n in ("w_out", "w_branch_b", "w_branch_a", "w_in")})
    dx1, dy1, gsmall["mix_norm"] = _residual_norm_bwd(x1, small["mix_norm"] + sent, dh2, dx2, 0.5, "mix_norm_bwd")

    dh1, sent = _ffn_bwd(h1, gu1, act1, dy1, big["ffn1_w_gu"], big["ffn1_w_down"], "ffn1", comm)
    grad_x, _, gsmall["ffn1_norm"] = _residual_norm_bwd(x, small["ffn1_norm"] + sent, [dh1], dx1, 1.0,
                                                        "ffn1_norm_bwd")
    return loss, grad_x, gsmall


GATHER_GROUPS = {"ffn1": ("ffn1_w_gu",),
                 "ffn1_down": ("ffn1_w_down",),
                 "mixer": ("w_in_main", "w_in_edge"),
                 "branches": ("w_branch_a", "w_branch_b", "w_out"),
                 "tail": ("ffn2_w_gu", "ffn2_w_down", "ple_gate", "ple_proj")}
SPLIT_GATHERS = ("ffn1_down", "mixer", "branches", "tail")


def _kind(name):
    return "cols" if name in COL_SHARDED or name.startswith("w_in_") else "rows"


def _merge_w_in(main, edges):
    edge_w = WIN_W - WIN_STEP
    w_in = jnp.pad(main, ((0, 0), (0, edge_w)))
    for d in range(N_DEV):
        at = WIN_STEP * (d + 1)
        w_in = w_in + jnp.pad(edges[:, d * edge_w:(d + 1) * edge_w], ((0, 0), (at, IN_PADDED - at - edge_w)))
    return w_in


class _Fsdp:
    def __init__(self, wts, first):
        self.wts, self.first_token = wts, first
        main, edge = _roll_w_in(jnp.pad(wts["w_in"], ((0, 0), (0, WIN_W - IN_SHARD))))
        self.shards = {n: wts[n].astype(BF16) for n in BIG if n not in ("w_in", "ffn1_w_gu")}
        self.shards.update(w_in_main=main, w_in_edge=edge)
        self.lands = {n: _place_block(self.shards[n], _kind(n), "own_" + n)
                      for group in SPLIT_GATHERS for n in GATHER_GROUPS[group]}
        self.flight, self.sent = {}, {}

    def _gather_first(self, after):
        token = self.first_token + after[0, 0].astype(F32) * 0.0
        me = _index(*_me())
        for n, land in self.lands.items():
            r, c = self.shards[n].shape
            at = (me * r, 0) if _kind(n) == "rows" else (0, me * c)
            token = token + lax.dynamic_slice(land, at, (1, 1))[0, 0].astype(F32) * 0.0
        shard = (self.wts["ffn1_w_gu"] + token).astype(BF16)
        self.shards["ffn1_w_gu"] = shard
        first = _all_gather([shard], [_kind("ffn1_w_gu")], 1)[0]
        token = first[0, 0].astype(F32) * 0.0
        for group in SPLIT_GATHERS:
            names = GATHER_GROUPS[group]
            srcs = [self.shards[n] for n in names]
            lands = [self.lands[n] for n in names]
            make = _gather_copies([s.shape for s in srcs], [_kind(n) for n in names])
            srcs[0] = srcs[0] + token.astype(BF16)
            send_sems, recv_sems, srcs, lands, tok = _split_start(srcs, lands, make, "gather_start_" + group)
            token = token + tok[0, 0]
            self.flight[group] = (send_sems, recv_sems, srcs, lands, make)
        return {"ffn1_w_gu": first, "_token": token}

    def arrive(self, group, after):
        if group == "ffn1":
            return self._gather_first(after)
        send_sems, recv_sems, srcs, lands, make = self.flight[group]
        _, lands = _split_wait(send_sems, recv_sems, srcs, lands, after, make, "gather_wait_" + group)
        full = dict(zip(GATHER_GROUPS[group], lands))
        if group == "mixer":
            full["w_in"] = _merge_w_in(full.pop("w_in_main"), full.pop("w_in_edge"))
        return full

    def send(self, group, grads):
        names = list(grads)
        kinds = ["all" if n == "small" else "win" if n == "w_in" else _kind(n) for n in names]
        shapes = [grads[n].shape if n == "small" else (D_MODEL, WIN_W) if n == "w_in" else self.shards[n].shape
                  for n in names]
        srcs = [grads[n] for n in names]
        lands = [lax.empty((N_DEV,) + tuple(s), g.dtype) for s, g in zip(shapes, srcs)]
        make = _exchange_copies(shapes, kinds)
        send_sems, recv_sems, srcs, lands, tok = _split_start(srcs, lands, make, "grads_start_" + group)
        self.sent[group] = (names, kinds, send_sems, recv_sems, srcs, lands, make)
        return tok[0, 0]

    def received(self, group, after):
        names, kinds, send_sems, recv_sems, srcs, lands, make = self.sent[group]
        srcs, lands = _split_wait(send_sems, recv_sems, srcs, lands, after, make, "grads_wait_" + group)
        return {n: (k, g, r) for n, k, g, r in zip(names, kinds, srcs, lands)}


SMALL_ROWS = ("ffn1_norm", "mix_norm", "ffn2_norm", "ple_norm", "gdn_norm", "q_norm", "k_norm", "a_log", "dt_bias",
              "rel_bias", "conv_w")


def _pack_small(vals):
    rows = []
    for n in SMALL_ROWS:
        v = vals[n]
        if n == "rel_bias":
            v = jnp.pad(v, ((0, 0), (0, 2 * LANES - N_REL)))
        elif n in ("a_log", "dt_bias"):
            v = _pad_lanes(v)
        rows.append(v.reshape(-1, LANES))
    packed = jnp.concatenate(rows, axis=0)
    return jnp.pad(packed, ((0, -packed.shape[0] % SUBLANES), (0, 0)))


def _unpack_small(packed, shapes):
    out, off = {}, 0
    for n in SMALL_ROWS:
        shp = shapes[n]
        if n == "rel_bias":
            out[n] = packed[off:off + 2 * HEADS].reshape(HEADS, 2 * LANES)[:, :N_REL]
            off += 2 * HEADS
        elif n in ("a_log", "dt_bias"):
            out[n] = packed[off:off + 1, :HEADS]
            off += 1
        else:
            r = int(np.prod(shp)) // LANES
            out[n] = packed[off:off + r].reshape(shp)
            off += r
    return out


WEIGHTS = ("ffn1_norm", "ffn1_w_gu", "ffn1_w_down", "mix_norm", "w_in", "conv_w", "a_log", "dt_bias", "gdn_norm",
           "q_norm", "k_norm", "rel_bias", "w_branch_a", "w_branch_b", "w_out", "ffn2_norm", "ffn2_w_gu",
           "ffn2_w_down", "ple_norm", "ple_gate", "ple_proj")


def kernel(x, p, ffn1_norm, ffn1_w_gu, ffn1_w_down, mix_norm, w_in, conv_w, a_log, dt_bias, gdn_norm, q_norm, k_norm, rel_bias, w_branch_a, w_branch_b, w_out, ffn2_norm, ffn2_w_gu, ffn2_w_down, ple_norm, ple_gate, ple_proj, loss_target, m_ffn1_norm, m_ffn1_w_gu, m_ffn1_w_down, m_mix_norm, m_w_in, m_conv_w, m_a_log, m_dt_bias, m_gdn_norm, m_q_norm, m_k_norm, m_rel_bias, m_w_branch_a, m_w_branch_b, m_w_out, m_ffn2_norm, m_ffn2_w_gu, m_ffn2_w_down, m_ple_norm, m_ple_gate, m_ple_proj, v_ffn1_norm, v_ffn1_w_gu, v_ffn1_w_down, v_mix_norm, v_w_in, v_conv_w, v_a_log, v_dt_bias, v_gdn_norm, v_q_norm, v_k_norm, v_rel_bias, v_w_branch_a, v_w_branch_b, v_w_out, v_ffn2_norm, v_ffn2_w_gu, v_ffn2_w_down, v_ple_norm, v_ple_gate, v_ple_proj):
    args = dict(locals())
    def layer0(v):
        return v[0] if v.ndim == 3 else v

    wts = {n: layer0(args[n]) for n in WEIGHTS}
    mom = {n: layer0(args["m_" + n]) for n in WEIGHTS}
    var = {n: layer0(args["v_" + n]) for n in WEIGHTS}
    x2d, p2d, tgt = x[0], p[0, 0], loss_target[0]
    my_index = _index(*_me())

    small = {n: wts[n] for n in SMALL_ROWS if n != "conv_w"}
    conv_shard = wts["conv_w"]
    conv_cols = conv_shard.shape[1]
    conv_packed = jnp.zeros((SUBLANES, N_DEV * conv_cols), F32)
    conv_packed = lax.dynamic_update_slice(conv_packed, jnp.pad(conv_shard, ((0, SUBLANES - CONV_K), (0, 0))),
                                           (0, my_index * conv_cols))
    small["conv_w"] = _all_reduce_small(conv_packed.reshape(-1, LANES), "conv_w_gather").reshape(SUBLANES, -1)[:CONV_K]

    fsdp = _Fsdp(wts, small["conv_w"][0, 0] * 0.0)

    loss, grad_x, gsmall = _local_step(x2d, p2d, tgt, small, fsdp)
    loss = lax.psum(loss, ("x", "y", "c"))

    fsdp.send("small", {"small": _pack_small(gsmall)})

    outs_big, after = {}, grad_x
    for group in list(fsdp.sent):
        for n, (kind, grad, recv) in fsdp.received(group, after).items():
            if n == "small":
                small_sum = _sum_small(recv, grad)
            elif n == "w_in":
                g_in = _sum_w_in_windows(recv, grad)[:, :IN_SHARD]
                outs_big[n] = [g_in] + list(_adamw_small(wts[n], g_in, mom[n], var[n], "adamw_w_in"))
            else:
                outs_big[n] = _adamw_recv(recv, grad, kind, wts[n], mom[n], var[n], "adamw_" + n)
            after = small_sum if n == "small" else outs_big[n][1]

    small_shapes = {n: (small[n].shape if n != "conv_w" else (CONV_K, N_DEV * conv_cols)) for n in SMALL_ROWS}
    gsum = _unpack_small(small_sum, small_shapes)
    gsum["conv_w"] = lax.dynamic_slice(gsum["conv_w"], (0, my_index * conv_cols), (CONV_K, conv_cols))
    rep = [n for n in SMALL_ROWS if n != "conv_w"]
    rep_shapes = {n: small_shapes[n] for n in rep}

    def pack_rep(vals):
        return _pack_small({**{n: vals[n] for n in rep}, "conv_w": jnp.zeros((CONV_K, LANES), F32)})

    def unpack_rep(packed):
        return _unpack_small(packed, {**rep_shapes, "conv_w": (CONV_K, LANES)})

    outs_small = [unpack_rep(o) for o in _adamw_small(pack_rep(wts), pack_rep(gsum), pack_rep(mom), pack_rep(var),
                                                      "adamw_replicated")]
    pad8 = functools.partial(jnp.pad, pad_width=((0, SUBLANES - CONV_K), (0, 0)))
    outs_conv = [o[:CONV_K] for o in _adamw_small(pad8(conv_shard), pad8(gsum["conv_w"]), pad8(mom["conv_w"]),
                                                   pad8(var["conv_w"]), "adamw_conv")]

    def leaf(kind, n):
        if n in BIG:
            return outs_big[n][kind][None]
        if n == "conv_w":
            return (gsum["conv_w"] if kind == 0 else outs_conv[kind - 1])[None]
        return (gsum[n] if kind == 0 else outs_small[kind - 1][n]).reshape(args[n].shape)

    result = [loss, grad_x[None]]
    for kind in range(4):
        result += [leaf(kind, n) for n in WEIGHTS]
    return tuple(result)
```

```python
import functools

import numpy as np
import jax
import jax.numpy as jnp
from jax import lax
from jax.experimental import pallas as pl
from jax.experimental.pallas import tpu as pltpu

F32 = jnp.float32
BF16 = jnp.bfloat16
HIGHEST = lax.Precision.HIGHEST
MESH = pl.DeviceIdType.MESH

D_MODEL = 2048
D_FF = 5632
HEADS = 8
HEAD_DIM = 128
HW = HEADS * HEAD_DIM
CHUNK = 64
LEFT_CHUNKS = 8
MAX_REL = 128
N_REL = (CHUNK - 1) + MAX_REL + 1
CONV_K = 4
EPS = 1e-6
NEG_INF = -1e30
N_DEV = 8
LANES = 128
SUBLANES = 8
VMEM_LIMIT = 56 * 1024 * 1024

MATMUL_WHOLE_K = 2048

ATT_QB = 256
ATT_KW = ATT_QB + LEFT_CHUNKS * CHUNK
ATT_PAD = LEFT_CHUNKS * CHUNK
GDN_CB = 8
GDN_GROUP = 8
GDN_SCAN_UNROLL = 2

ADAM_LR = 0.001
ADAM_B1 = 0.9
ADAM_B2 = 0.999
ADAM_EPS = 1e-08
ADAM_WD = 0.01
ADAM_STEP = 10

IN_QZ = 3 * HW + HW
IN_AB0 = IN_QZ
IN_QKVB0 = IN_AB0 + 2 * HEADS
IN_GG0 = IN_QKVB0 + 3 * HW
IN_COLS = IN_GG0 + 2 * D_MODEL

BIG = ("ffn1_w_gu", "ffn1_w_down", "w_in", "w_branch_a", "w_branch_b", "w_out",
       "ffn2_w_gu", "ffn2_w_down", "ple_gate", "ple_proj")
COL_SHARDED = ("ffn1_w_gu", "w_in", "w_branch_a", "w_branch_b", "ffn2_w_gu", "ple_proj")


def _params(semantics=None, **kw):
    return pltpu.CompilerParams(dimension_semantics=semantics, vmem_limit_bytes=VMEM_LIMIT, **kw)


def _pick(n, cands):
    for c in cands:
        if n % c == 0:
            return c
    return n


def _matmul(a, b, mode, out_dtype, name):
    halves = (a.ndim == 3 and mode == "nt") or (b.ndim == 3 and mode == "tn")
    if mode == "nn":
        (m, k), (k2, n) = a.shape, b.shape
    elif mode == "nt":
        (m, k), (n, k2) = (a.shape[-2], a.shape[-1] * (a.ndim - 1)), b.shape
    else:
        (k, m), (k2, n) = a.shape, (b.shape[-2], b.shape[-1] * (b.ndim - 1))
    assert k == k2 and a.ndim + b.ndim == (5 if halves else 4), (a.shape, b.shape, mode)
    tm = _pick(m, (1024, 512, 256, 128))
    if halves and mode == "tn":
        tn = _pick(n // 2, (1408, 1024, 512, 256, 128))
        tk = _pick(k, (2048, 1024, 512, 256, 128))
    elif halves:
        tn = _pick(n, (1024, 512, 256, 128))
        tk = _pick(k // 2, (2816, 2048, 1536, 1024, 512, 256, 128))
    else:
        tn = _pick(n, (1024, 512, 256, 128))
        tk = k if k <= MATMUL_WHOLE_K else _pick(k, (2816, 2048, 1536, 1024, 512, 256, 128))
    nk = k // tk
    per_half = (n // 2) // tn if mode == "tn" else (k // 2) // tk
    if mode == "nn":
        a_spec = pl.BlockSpec((tm, tk), lambda i, j, kk: (i, kk))
        b_spec = pl.BlockSpec((tk, tn), lambda i, j, kk: (kk, j))
        dims = (((1,), (0,)), ((), ()))
    elif mode == "nt":
        a_spec = pl.BlockSpec((tm, tk), lambda i, j, kk: (i, kk))
        b_spec = pl.BlockSpec((tn, tk), lambda i, j, kk: (j, kk))
        dims = (((1,), (1,)), ((), ()))
        if halves:
            a_spec = pl.BlockSpec((None, tm, tk), lambda i, j, kk: (kk // per_half, i, kk % per_half))
    else:
        a_spec = pl.BlockSpec((tk, tm), lambda i, j, kk: (kk, i))
        b_spec = pl.BlockSpec((tk, tn), lambda i, j, kk: (kk, j))
        dims = (((0,), (0,)), ((), ()))
        if halves:
            b_spec = pl.BlockSpec((None, tk, tn), lambda i, j, kk: (j // per_half, kk, j % per_half))

    def body(a_ref, b_ref, o_ref, *acc):
        prod = lax.dot_general(a_ref[...].astype(BF16), b_ref[...].astype(BF16), dims, preferred_element_type=F32)
        if nk == 1:
            o_ref[...] = prod.astype(o_ref.dtype)
            return
        acc_ref, kk = acc[0], pl.program_id(2)

        @pl.when(kk == 0)
        def _():
            acc_ref[...] = prod

        @pl.when((kk > 0) & (kk < nk - 1))
        def _():
            acc_ref[...] += prod

        @pl.when(kk == nk - 1)
        def _():
            o_ref[...] = (acc_ref[...] + prod).astype(o_ref.dtype)

    return pl.pallas_call(
        body, name=name,
        out_shape=jax.ShapeDtypeStruct((m, n), out_dtype),
        grid=(m // tm, n // tn, nk),
        in_specs=[a_spec, b_spec],
        out_specs=pl.BlockSpec((tm, tn), lambda i, j, kk: (i, j)),
        scratch_shapes=[pltpu.VMEM((tm, tn), F32)] if nk > 1 else [],
        compiler_params=_params(("parallel", "parallel", "arbitrary")),
    )(a, b)


def _rows(fn, row_ins, consts, row_outs, acc_outs, tile, name):
    t_rows = row_ins[0][0].shape[0]
    tile = min(tile, t_rows)
    assert t_rows % tile == 0 and tile % SUBLANES == 0
    n = t_rows // tile
    per = tile // SUBLANES
    last8 = t_rows // SUBLANES - 1
    in_specs = []
    for arr, kind in row_ins:
        c = arr.shape[1]
        if kind == "t":
            in_specs.append(pl.BlockSpec((tile, c), lambda i: (i, 0)))
        elif kind == "p":
            in_specs.append(pl.BlockSpec((SUBLANES, c), lambda i: (jnp.maximum(i * per - 1, 0), 0)))
        else:
            in_specs.append(pl.BlockSpec((SUBLANES, c), lambda i: (jnp.minimum((i + 1) * per, last8), 0)))
    for arr in consts:
        in_specs.append(pl.BlockSpec(arr.shape, lambda i, nd=arr.ndim: (0,) * nd))
    out_shape = [jax.ShapeDtypeStruct((t_rows, c), dt) for c, dt in row_outs]
    out_specs = [pl.BlockSpec((tile, c), lambda i: (i, 0)) for c, _ in row_outs]
    for shp in acc_outs:
        out_shape.append(jax.ShapeDtypeStruct(shp, F32))
        out_specs.append(pl.BlockSpec(shp, lambda i, nd=len(shp): (0,) * nd))
    n_in = len(row_ins) + len(consts)
    n_row_out = len(row_outs)

    def body(*refs):
        i = pl.program_id(0)
        vals = [r[...].astype(F32) for r in refs[:len(row_ins)]]
        res = fn(i, n, *vals, *refs[len(row_ins):n_in])
        outs = refs[n_in:]
        for r, v in zip(outs[:n_row_out], res[:n_row_out]):
            r[...] = v.astype(r.dtype)
        if acc_outs:
            @pl.when(i == 0)
            def _():
                for r in outs[n_row_out:]:
                    r[...] = jnp.zeros_like(r)

            for r, v in zip(outs[n_row_out:], res[n_row_out:]):
                r[...] += v

    res = pl.pallas_call(
        body, name=name, out_shape=out_shape, grid=(n,), in_specs=in_specs, out_specs=out_specs,
        compiler_params=_params(("arbitrary",) if acc_outs else ("parallel",)),
    )(*[a for a, _ in row_ins], *consts)
    return res


def _rms(x, w):
    return x * lax.rsqrt(jnp.mean(x * x, axis=-1, keepdims=True) + EPS) * w


def _l2n(x):
    return x * lax.rsqrt(jnp.sum(x * x, axis=-1, keepdims=True) + EPS)


def _sigmoid(x):
    return 1.0 / (1.0 + jnp.exp(-x))


def _silu(x):
    return x * _sigmoid(x)


def _softplus(x):
    return jnp.maximum(x, 0.0) + jnp.log(1.0 + jnp.exp(-jnp.abs(x)))


def _heads(fn, *xs):
    nh = xs[0].shape[1] // HEAD_DIM
    return jnp.concatenate(
        [fn(*[x[:, h * HEAD_DIM:(h + 1) * HEAD_DIM] for x in xs]) for h in range(nh)], axis=1)


def _colsum(x):
    return jnp.sum(x, axis=0, keepdims=True)


def _gated_norm(o, z, w):
    return _heads(lambda oh, zh: _rms(oh, w) * _silu(zh), o, z)


def _mix(gg, ta, tb):
    return _sigmoid(gg[:, :D_MODEL]) * ta + _sigmoid(gg[:, D_MODEL:]) * tb


def _gdn_post(y):
    a = _silu(y)
    q = _heads(lambda v: _l2n(v) * (HEAD_DIM ** -0.5), a[:, :HW])
    k = _heads(_l2n, a[:, HW:2 * HW])
    return q, k, a[:, 2 * HW:]


NN = (((1,), (0,)), ((), ()))
NT = (((1,), (1,)), ((), ()))
TN = (((0,), (0,)), ((), ()))


def _dg(a, b, dims):
    return lax.dot_general(a, b, dims, preferred_element_type=F32)


def _split2(x):
    hi = x.astype(BF16)
    return hi, (x - hi.astype(F32)).astype(BF16)


def _split3(x):
    hi = x.astype(BF16)
    r = x - hi.astype(F32)
    mid = r.astype(BF16)
    return hi, mid, (r - mid.astype(F32)).astype(BF16)


def _dg3(a, b, dims):
    ah, al = _split2(a)
    bh, bl = _split2(b)
    return _dg(ah, bh, dims) + (_dg(ah, bl, dims) + _dg(al, bh, dims))


BNN = (((2,), (1,)), ((0,), (0,)))
BNT = (((2,), (2,)), ((0,), (0,)))
BTN = (((1,), (1,)), ((0,), (0,)))


@jax.custom_vjp
def _mm3(a, b):
    return _dg3(a, b, BNN)


_mm3.defvjp(lambda a, b: (_dg3(a, b, BNN), (a, b)),
            lambda res, g: (_dg3(g, res[1], BNT), _dg3(res[0], g, BTN)))


def _xm(x, m, dims):
    mb = m.astype(BF16)
    parts = _split3(x)
    return _dg(parts[0], mb, dims) + (_dg(parts[1], mb, dims) + _dg(parts[2], mb, dims))


def _mx(m, x, dims):
    mb = m.astype(BF16)
    parts = _split3(x)
    return _dg(mb, parts[0], dims) + (_dg(mb, parts[1], dims) + _dg(mb, parts[2], dims))


@jax.custom_vjp
def _times_const(x, m):
    return _xm(x, m, NN)


_times_const.defvjp(lambda x, m: (_xm(x, m, NN), m),
                    lambda m, g: (_xm(g, m, NT), jnp.zeros_like(m)))


@jax.custom_vjp
def _const_times(m, x):
    return _mx(m, x, NN)


_const_times.defvjp(lambda m, x: (_mx(m, x, NN), m),
                    lambda m, g: (jnp.zeros_like(m), _mx(m, g, TN)))


@jax.custom_vjp
def _lane_mean_cols(x, avg):
    return _mx(avg, x, BNT)


_lane_mean_cols.defvjp(lambda x, avg: (_mx(avg, x, BNT), avg),
                       lambda avg, g: (_xm(g, avg, BTN), jnp.zeros_like(avg)))


def _gdn_gates(ab, alog, dtb, e_g, e_b):
    t = ab.shape[0]
    g = -jnp.exp(alog) * _softplus(ab + dtb)
    beta = _sigmoid(ab)
    ri = lax.broadcasted_iota(jnp.int32, (t, t), 0)
    ci = lax.broadcasted_iota(jnp.int32, (t, t), 1)
    shift = CHUNK.bit_length() - 1
    same = jnp.right_shift(ri, shift) == jnp.right_shift(ci, shift)
    tril = jnp.where(same & (ri >= ci), 1.0, 0.0).astype(F32)
    gc = _const_times(tril, g)
    return _times_const(gc, e_g), _times_const(beta, e_b)


def _shift_down(x, halo, s, i):
    if s == 0:
        return x
    halo = jnp.where(i == 0, 0.0, halo)
    xr = pltpu.roll(x, s, 0)
    hr = pltpu.roll(halo, s, 0)
    row = lax.broadcasted_iota(jnp.int32, (SUBLANES, x.shape[1]), 0)
    top = jnp.where(row < s, hr, xr[:SUBLANES])
    return jnp.concatenate([top, xr[SUBLANES:]], axis=0)


def _shift_up(x, halo, s, i, n):
    if s == 0:
        return x
    t = x.shape[0]
    halo = jnp.where(i == n - 1, 0.0, halo)
    xr = pltpu.roll(x, t - s, 0)
    hr = pltpu.roll(halo, SUBLANES - s, 0)
    row = lax.broadcasted_iota(jnp.int32, (SUBLANES, x.shape[1]), 0)
    bot = jnp.where(row >= SUBLANES - s, hr, xr[t - SUBLANES:])
    return jnp.concatenate([xr[:t - SUBLANES], bot], axis=0)


def _conv(pa, prev, cw_ref, i):
    y = pa * cw_ref[CONV_K - 1:CONV_K, :]
    for j in range(CONV_K - 1):
        y = y + _shift_down(pa, prev, CONV_K - 1 - j, i) * cw_ref[j:j + 1, :]
    return y


def _dot_nt(a, b, precision=None):
    return lax.dot_general(a, b, (((1,), (1,)), ((), ())), precision=precision, preferred_element_type=F32)


def _dot_tn(a, b, precision=None):
    return lax.dot_general(a, b, (((0,), (0,)), ((), ())), precision=precision, preferred_element_type=F32)


def _dot(a, b, precision=None):
    return jnp.dot(a, b, precision=precision, preferred_element_type=F32)


def _bf(x):
    return x.astype(BF16)


def _neumann_inverse(lmat):
    nb, c, _ = lmat.shape
    ri = lax.broadcasted_iota(jnp.int32, (nb, c, c), 1)
    ci = lax.broadcasted_iota(jnp.int32, (nb, c, c), 2)
    pw = -lmat
    inv = jnp.where(ri == ci, 1.0, 0.0).astype(F32) + pw
    for _ in range(5):
        pw = _mm3(pw, pw)
        inv = inv + _mm3(inv, pw)
    return inv


@jax.custom_vjp
def _unit_lower_inverse(lmat):
    return _neumann_inverse(lmat)


def _unit_lower_inverse_fwd(lmat):
    inv = _neumann_inverse(lmat)
    return inv, inv


def _unit_lower_inverse_bwd(inv, g):
    return (-_dg3(_dg3(inv, g, BTN), inv, BNT),)


_unit_lower_inverse.defvjp(_unit_lower_inverse_fwd, _unit_lower_inverse_bwd)


def _gdn_chunk(q, k, v, gc, bb):
    nb, c, _ = q.shape
    ri = lax.broadcasted_iota(jnp.int32, (nb, c, c), 1)
    ci = lax.broadcasted_iota(jnp.int32, (nb, c, c), 2)
    incl = ri >= ci
    strict = ri > ci
    g_row = gc[:, :, :c]
    g_col = _lane_mean_cols(gc, jnp.full((nb, c, LANES), 1.0 / LANES, F32))
    decay = jnp.where(incl, jnp.exp(jnp.where(incl, g_row - g_col, 0.0)), 0.0)
    kb = k * bb
    lmat = jnp.where(strict, _dg(_bf(kb), _bf(k), BNT) * decay, 0.0)
    inv = _unit_lower_inverse(lmat)
    egc = jnp.exp(gc)
    u = _mm3(inv, v * bb)
    w = _mm3(inv, kb * egc)
    aqk = _dg(_bf(q), _bf(k), BNT) * decay
    last = lax.broadcasted_iota(jnp.int32, (nb, c, LANES), 1) == c - 1
    tot = jnp.sum(jnp.where(last, gc, 0.0), axis=1, keepdims=True)
    k_tail = k * jnp.exp(tot - gc)
    tail = jnp.broadcast_to(jnp.exp(tot), (nb, SUBLANES, LANES))
    return u, w, aqk, q * egc, k_tail, tail


def _gdn_intra(qn, kn, vv, g_b, beta_b):
    t_rows = qn.shape[0]
    nc = t_rows // CHUNK
    cb = min(GDN_CB, nc)
    rows = cb * CHUNK
    col = pl.BlockSpec((rows, HEAD_DIM), lambda h, b: (b, h))

    def body(q_ref, k_ref, v_ref, g_ref, b_ref, u_ref, w_ref, a_ref, qd_ref, kt_ref, tl_ref):
        def group(gi, carry):
            r = pl.ds(pl.multiple_of(gi * (grp * CHUNK), grp * CHUNK), grp * CHUNK)
            ins = [ref[r, :].reshape(grp, CHUNK, HEAD_DIM) for ref in (q_ref, k_ref, v_ref, g_ref, b_ref)]
            u, w, aqk, qd, kt, tl = _gdn_chunk(*ins)
            for ref, val in ((u_ref, u), (w_ref, w), (qd_ref, qd), (kt_ref, kt)):
                ref[r, :] = val.reshape(grp * CHUNK, HEAD_DIM)
            a_ref[0, r, :] = aqk.reshape(grp * CHUNK, CHUNK)
            tl_ref[0, pl.ds(gi * grp, grp)] = tl
            return carry

        grp = min(GDN_GROUP, cb)
        lax.fori_loop(0, cb // grp, group, 0)

    full = jax.ShapeDtypeStruct((t_rows, HW), F32)
    return pl.pallas_call(
        body, name="gdn_intra_fwd",
        out_shape=[full, full, jax.ShapeDtypeStruct((HEADS, t_rows, CHUNK), F32), full, full,
                   jax.ShapeDtypeStruct((HEADS, nc, SUBLANES, LANES), F32)],
        grid=(HEADS, nc // cb),
        in_specs=[col] * 5,
        out_specs=[col, col, pl.BlockSpec((1, rows, CHUNK), lambda h, b: (h, b, 0)), col, col,
                   pl.BlockSpec((1, cb, SUBLANES, LANES), lambda h, b: (h, b, 0, 0))],
        compiler_params=_params(("parallel", "parallel")),
    )(qn, kn, vv, g_b, beta_b)


def _gdn_intra_bwd(qn, kn, vv, g_b, beta_b, du, dw, da, dqd, dkt, dtl):
    t_rows = qn.shape[0]
    nc = t_rows // CHUNK
    cb = min(GDN_CB, nc)
    rows = cb * CHUNK
    col = pl.BlockSpec((rows, HEAD_DIM), lambda h, b: (b, h))
    a_spec = pl.BlockSpec((1, rows, CHUNK), lambda h, b: (h, b, 0))
    tl_spec = pl.BlockSpec((1, cb, SUBLANES, LANES), lambda h, b: (h, b, 0, 0))

    def body(q_ref, k_ref, v_ref, g_ref, b_ref, du_ref, dw_ref, da_ref, dqd_ref, dkt_ref, dtl_ref,
             dq_ref, dk_ref, dv_ref, dg_ref, db_ref):
        def group(gi, carry):
            r = pl.ds(pl.multiple_of(gi * (grp * CHUNK), grp * CHUNK), grp * CHUNK)
            wide = (grp, CHUNK, HEAD_DIM)
            ins = [ref[r, :].reshape(wide) for ref in (q_ref, k_ref, v_ref, g_ref, b_ref)]
            cts = (du_ref[r, :].reshape(wide), dw_ref[r, :].reshape(wide),
                   da_ref[0, r, :].reshape(grp, CHUNK, CHUNK), dqd_ref[r, :].reshape(wide),
                   dkt_ref[r, :].reshape(wide), dtl_ref[0, pl.ds(gi * grp, grp)])
            grads = jax.vjp(_gdn_chunk, *ins)[1](cts)
            for ref, val in zip((dq_ref, dk_ref, dv_ref, dg_ref, db_ref), grads):
                ref[r, :] = val.reshape(grp * CHUNK, HEAD_DIM)
            return carry

        grp = min(GDN_GROUP, cb)
        lax.fori_loop(0, cb // grp, group, 0)

    full = jax.ShapeDtypeStruct((t_rows, HW), F32)
    return pl.pallas_call(
        body, name="gdn_intra_bwd",
        out_shape=[full] * 5,
        grid=(HEADS, nc // cb),
        in_specs=[col] * 7 + [a_spec, col, col, tl_spec],
        out_specs=[col] * 5,
        compiler_params=_params(("parallel", "parallel")),
    )(qn, kn, vv, g_b, beta_b, du, dw, da, dqd, dkt, dtl)


def _head_cols(h):
    return slice(h * HEAD_DIM, (h + 1) * HEAD_DIM)


def _gdn_scan(u, w, aqk, qd, kt, tl):
    t_rows = u.shape[0]
    nc = t_rows // CHUNK
    cb = min(GDN_CB, nc)
    rows = cb * CHUNK
    wide = pl.BlockSpec((rows, HW), lambda b: (b, 0))

    def body(u_ref, w_ref, a_ref, qd_ref, kt_ref, tl_ref, o_ref, s_out_ref, s_ref):
        @pl.when(pl.program_id(0) == 0)
        def _():
            s_ref[...] = jnp.zeros_like(s_ref)

        def chunk(ci, carry):
            r = pl.ds(pl.multiple_of(ci * CHUNK, CHUNK), CHUNK)
            for h in range(HEADS):
                hc = _head_cols(h)
                s = s_ref[h]
                s_out_ref[ci, h] = s
                sb = _bf(s)
                vn = u_ref[r, hc] - _dot(_bf(w_ref[r, hc]), sb)
                vnb = _bf(vn)
                o_ref[r, hc] = _dot(_bf(qd_ref[r, hc]), sb) + _dot(_bf(a_ref[h, r, :]), vnb)
                s_ref[h] = s * tl_ref[h, ci, 0:1, :] + _dot_tn(_bf(kt_ref[r, hc]), vnb)
            return carry

        lax.fori_loop(0, cb, chunk, 0, unroll=GDN_SCAN_UNROLL)

    return pl.pallas_call(
        body, name="gdn_scan_fwd",
        out_shape=[jax.ShapeDtypeStruct((t_rows, HW), F32),
                   jax.ShapeDtypeStruct((nc, HEADS, HEAD_DIM, HEAD_DIM), F32)],
        grid=(nc // cb,),
        in_specs=[wide, wide, pl.BlockSpec((HEADS, rows, CHUNK), lambda b: (0, b, 0)), wide, wide,
                  pl.BlockSpec((HEADS, cb, SUBLANES, LANES), lambda b: (0, b, 0, 0))],
        out_specs=[wide, pl.BlockSpec((cb, HEADS, HEAD_DIM, HEAD_DIM), lambda b: (b, 0, 0, 0))],
        scratch_shapes=[pltpu.VMEM((HEADS, HEAD_DIM, HEAD_DIM), F32)],
        compiler_params=_params(("arbitrary",)),
    )(u, w, aqk, qd, kt, tl)


def _gdn_scan_bwd(do, u, w, aqk, qd, kt, tl, states):
    t_rows = u.shape[0]
    nc = t_rows // CHUNK
    cb = min(GDN_CB, nc)
    rows = cb * CHUNK
    nb = nc // cb
    wide = pl.BlockSpec((rows, HW), lambda b: (nb - 1 - b, 0))
    a_spec = pl.BlockSpec((HEADS, rows, CHUNK), lambda b: (0, nb - 1 - b, 0))
    tl_spec = pl.BlockSpec((HEADS, cb, SUBLANES, LANES), lambda b: (0, nb - 1 - b, 0, 0))

    def body(do_ref, u_ref, w_ref, a_ref, qd_ref, kt_ref, tl_ref, s_in_ref,
             du_ref, dw_ref, da_ref, dqd_ref, dkt_ref, dtl_ref, ds_ref):
        @pl.when(pl.program_id(0) == 0)
        def _():
            ds_ref[...] = jnp.zeros_like(ds_ref)

        row0 = lax.broadcasted_iota(jnp.int32, (SUBLANES, LANES), 0) == 0

        def chunk(step, carry):
            ci = cb - 1 - step
            r = pl.ds(pl.multiple_of(ci * CHUNK, CHUNK), CHUNK)
            for h in range(HEADS):
                hc = _head_cols(h)
                s = s_in_ref[ci, h]
                ds_next = ds_ref[h]
                sb, dsb = _bf(s), _bf(ds_next)
                wb, ab, ktb, qdb = _bf(w_ref[r, hc]), _bf(a_ref[h, r, :]), _bf(kt_ref[r, hc]), _bf(qd_ref[r, hc])
                dob = _bf(do_ref[r, hc])
                vn = u_ref[r, hc] - _dot(wb, sb)
                vnb = _bf(vn)
                dvn = _dot_tn(ab, dob) + _dot(ktb, dsb)
                dvnb = _bf(dvn)
                du_ref[r, hc] = dvn
                dw_ref[r, hc] = -_dot_nt(dvnb, sb)
                da_ref[h, r, :] = _dot_nt(dob, vnb)
                dqd_ref[r, hc] = _dot_nt(dob, sb)
                dkt_ref[r, hc] = _dot_nt(vnb, dsb)
                dtl_ref[h, ci] = jnp.where(row0, _colsum(s * ds_next), 0.0)
                ds_ref[h] = _dot_tn(qdb, dob) + ds_next * tl_ref[h, ci, 0:1, :] - _dot_tn(wb, dvnb)
            return carry

        lax.fori_loop(0, cb, chunk, 0, unroll=GDN_SCAN_UNROLL)

    full = jax.ShapeDtypeStruct((t_rows, HW), F32)
    return pl.pallas_call(
        body, name="gdn_scan_bwd",
        out_shape=[full, full, jax.ShapeDtypeStruct((HEADS, t_rows, CHUNK), F32), full, full,
                   jax.ShapeDtypeStruct((HEADS, nc, SUBLANES, LANES), F32)],
        grid=(nb,),
        in_specs=[wide, wide, wide, a_spec, wide, wide, tl_spec,
                  pl.BlockSpec((cb, HEADS, HEAD_DIM, HEAD_DIM), lambda b: (nb - 1 - b, 0, 0, 0))],
        out_specs=[wide, wide, a_spec, wide, wide, tl_spec],
        scratch_shapes=[pltpu.VMEM((HEADS, HEAD_DIM, HEAD_DIM), F32)],
        compiler_params=_params(("arbitrary",)),
    )(do, u, w, aqk, qd, kt, tl, states)


def _att_profile_index():
    j = lax.broadcasted_iota(jnp.int32, (SUBLANES, ATT_KW), 1)
    return jnp.clip(ATT_PAD - j, -(CHUNK - 1), MAX_REL) + (CHUNK - 1)


def _att_far_back():
    qi = lax.broadcasted_iota(jnp.int32, (ATT_QB, ATT_KW), 0)
    kj = lax.broadcasted_iota(jnp.int32, (ATT_QB, ATT_KW), 1)
    return kj < qi


def _rotate_rows(x, forward):
    rows, lanes = x.shape
    row = lax.broadcasted_iota(jnp.int32, x.shape, 0)
    for bit in range(rows.bit_length() - 1):
        amount = (1 << bit) if forward else lanes - (1 << bit)
        x = jnp.where(jnp.bitwise_and(jnp.right_shift(row, bit), 1) == 1, pltpu.roll(x, amount, 1), x)
    return x


def _att_in_band():
    qi = lax.broadcasted_iota(jnp.int32, (ATT_QB, ATT_KW), 0)
    kj = lax.broadcasted_iota(jnp.int32, (ATT_QB, ATT_KW), 1)
    shift = CHUNK.bit_length() - 1
    qc = jnp.right_shift(qi, shift)
    kc = jnp.right_shift(kj, shift) - LEFT_CHUNKS
    return (kc <= qc) & (kc >= qc - LEFT_CHUNKS)


def _att_valid(b):
    kj = lax.broadcasted_iota(jnp.int32, (1, ATT_KW), 1)
    return jnp.where(kj + b * ATT_QB >= ATT_PAD, 0.0, NEG_INF)


def _rms_parts(x, w):
    r = lax.rsqrt(jnp.mean(x * x, axis=-1, keepdims=True) + EPS)
    xn = x * r
    return xn * w, xn, r


def _rms_bwd(dy, xn, r, w):
    dxn = dy * w
    dx = r * (dxn - xn * jnp.mean(dxn * xn, axis=-1, keepdims=True))
    return dx, _colsum(dy * xn)


def _att_probs(qb, kb, bias, before_start):
    s = _dot_nt(qb, kb) * (HEAD_DIM ** -0.5) + bias + before_start
    e = jnp.exp(s - jnp.max(s, axis=-1, keepdims=True))
    return e * (1.0 / jnp.sum(e, axis=-1, keepdims=True))


def _att_specs():
    q_spec = pl.BlockSpec((ATT_QB, HEAD_DIM), lambda h, b: (b, h))
    back = ATT_PAD // ATT_QB
    k_specs = [pl.BlockSpec((ATT_QB, HEAD_DIM), lambda h, b, j=j: (jnp.maximum(b + j - back, 0), HEADS + h))
               for j in range(3)]
    v_specs = [pl.BlockSpec((ATT_QB, HEAD_DIM), lambda h, b, j=j: (jnp.maximum(b + j - back, 0), 2 * HEADS + h))
               for j in range(3)]
    w_spec = pl.BlockSpec((1, HEAD_DIM), lambda h, b: (0, 0))
    smem = pl.BlockSpec(memory_space=pltpu.SMEM)
    return q_spec, k_specs, v_specs, w_spec, smem


BIAS_SPEC = pl.BlockSpec((1, ATT_QB, ATT_KW), lambda h, b: (h, 0, 0))


def _expand_rel_bias(rel):
    def body(rel_ref, bias_ref):
        h = pl.program_id(0)
        idx = _att_profile_index()

        def fill(r, acc):
            return jnp.where(idx == r, rel_ref[h, r], acc)

        profile = lax.fori_loop(0, N_REL, fill, jnp.zeros((SUBLANES, ATT_KW), F32))
        table = _rotate_rows(jnp.concatenate([profile] * (ATT_QB // SUBLANES), axis=0), True)
        table = jnp.where(_att_far_back(), rel_ref[h, N_REL - 1], table)
        bias_ref[0] = jnp.where(_att_in_band(), table, NEG_INF)

    return pl.pallas_call(
        body, name="rel_bias_expand",
        out_shape=jax.ShapeDtypeStruct((HEADS, ATT_QB, ATT_KW), F32), grid=(HEADS,),
        in_specs=[pl.BlockSpec(memory_space=pltpu.SMEM)],
        out_specs=pl.BlockSpec((1, ATT_QB, ATT_KW), lambda h: (h, 0, 0)),
        compiler_params=_params(("parallel",)),
    )(rel)


def _attention(pb, qw, kw, bias):
    t_rows = pb.shape[0]
    q_spec, k_specs, v_specs, w_spec, _ = _att_specs()

    def body(q_ref, k0, k1, k2, v0, v1, v2, qw_ref, kw_ref, bias_ref, o_ref):
        b = pl.program_id(1)
        kwin = jnp.concatenate([k0[...], k1[...], k2[...]], axis=0)
        vwin = jnp.concatenate([v0[...], v1[...], v2[...]], axis=0)
        q = _rms(q_ref[...], qw_ref[...])
        k = _rms(kwin, kw_ref[...])
        p = _att_probs(_bf(q), _bf(k), bias_ref[0], _att_valid(b))
        o_ref[...] = _dot(_bf(p), _bf(vwin)).astype(o_ref.dtype)

    return pl.pallas_call(
        body, name="band_attention_fwd",
        out_shape=jax.ShapeDtypeStruct((t_rows, HW), BF16),
        grid=(HEADS, t_rows // ATT_QB),
        in_specs=[q_spec] + k_specs + v_specs + [w_spec, w_spec, BIAS_SPEC],
        out_specs=pl.BlockSpec((ATT_QB, HEAD_DIM), lambda h, b: (b, h)),
        compiler_params=_params(("parallel", "arbitrary")),
    )(pb, pb, pb, pb, pb, pb, pb, qw, kw, bias)


def _attention_bwd(pb, qw, kw, bias, dyb):
    t_rows = pb.shape[0]
    nb = t_rows // ATT_QB
    q_spec, k_specs, v_specs, w_spec, smem = _att_specs()
    pad_rows = t_rows + ATT_PAD
    acc_spec = pl.BlockSpec((pad_rows, HEAD_DIM), lambda h, b: (0, h))

    def body(q_ref, k0, k1, k2, v0, v1, v2, qw_ref, kw_ref, bias_ref, do_ref,
             dq_ref, dk_ref, dv_ref, dqw_ref, dkw_ref, drel_ref, dbias_ref):
        h, b = pl.program_id(0), pl.program_id(1)

        @pl.when(b == 0)
        def _():
            dbias_ref[...] = jnp.zeros_like(dbias_ref)
            dk_ref[...] = jnp.zeros_like(dk_ref)
            dv_ref[...] = jnp.zeros_like(dv_ref)

        @pl.when((b == 0) & (h == 0))
        def _():
            dqw_ref[...] = jnp.zeros_like(dqw_ref)
            dkw_ref[...] = jnp.zeros_like(dkw_ref)

        kwin = jnp.concatenate([k0[...], k1[...], k2[...]], axis=0)
        vwin = jnp.concatenate([v0[...], v1[...], v2[...]], axis=0)
        scale = HEAD_DIM ** -0.5
        qw_, kw_ = qw_ref[...], kw_ref[...]
        q, qn, rq = _rms_parts(q_ref[...], qw_)
        k, kn, rk = _rms_parts(kwin, kw_)
        qb, kb, dob = _bf(q), _bf(k), _bf(do_ref[...])
        p = _att_probs(qb, kb, bias_ref[0], _att_valid(b))
        dp = _dot_nt(dob, _bf(vwin))
        ds = p * (dp - jnp.sum(p * dp, axis=-1, keepdims=True))
        dbias_ref[...] += ds
        ds = _bf(ds)
        dq, dqw = _rms_bwd(_dot(ds, kb) * scale, qn, rq, qw_)
        dk, dkw = _rms_bwd(_dot_tn(ds, qb) * scale, kn, rk, kw_)
        dq_ref[...] = dq.astype(dq_ref.dtype)
        win = pl.ds(pl.multiple_of(b * ATT_QB, ATT_QB), ATT_KW)
        dk_ref[win, :] += dk
        dv_ref[win, :] += _dot_tn(_bf(p), dob)
        dqw_ref[...] += dqw
        dkw_ref[...] += dkw

        @pl.when(b == nb - 1)
        def _():
            tot, far = dbias_ref[...], _att_far_back()
            far_sum = jnp.sum(jnp.where(far, tot, 0.0))
            per_offset = _colsum(_rotate_rows(jnp.where(far, 0.0, tot), False))
            idx = _att_profile_index()
            first_row = lax.broadcasted_iota(jnp.int32, idx.shape, 0) == 0
            spread = jnp.where(first_row, per_offset, 0.0)

            def reduce(r, carry):
                drel_ref[h, r] = jnp.sum(jnp.where(idx == r, spread, 0.0)) + jnp.where(r == N_REL - 1, far_sum, 0.0)
                return carry

            lax.fori_loop(0, N_REL, reduce, 0)

    return pl.pallas_call(
        body, name="band_attention_bwd",
        out_shape=[jax.ShapeDtypeStruct((t_rows, HW), BF16),
                   jax.ShapeDtypeStruct((pad_rows, HW), F32), jax.ShapeDtypeStruct((pad_rows, HW), F32),
                   jax.ShapeDtypeStruct((1, HEAD_DIM), F32), jax.ShapeDtypeStruct((1, HEAD_DIM), F32),
                   jax.ShapeDtypeStruct((HEADS, N_REL), F32)],
        grid=(HEADS, nb),
        in_specs=[q_spec] + k_specs + v_specs + [w_spec, w_spec, BIAS_SPEC, q_spec],
        out_specs=[q_spec, acc_spec, acc_spec, w_spec, w_spec, smem],
        scratch_shapes=[pltpu.VMEM((ATT_QB, ATT_KW), F32)],
        compiler_params=_params(("arbitrary", "arbitrary")),
    )(pb, pb, pb, pb, pb, pb, pb, qw, kw, bias, dyb)


def _me():
    return lax.axis_index("x"), lax.axis_index("y"), lax.axis_index("c")


def _index(x, y, c):
    return 4 * x + 2 * y + c


HBM_SPEC = pl.BlockSpec(memory_space=pl.ANY)


def _block(ref, kind, d, r, c):
    if kind == "all":
        return ref
    if kind == "rows":
        return ref.at[pl.ds(d * r, r), :]
    if kind == "win":
        return ref.at[:, pl.ds(d * WIN_STEP, c)]
    return ref.at[:, pl.ds(d * c, c)]


def _all_gather(shards, kinds, n_gather):
    n = len(shards)

    def body(*refs):
        x_refs, out_refs = refs[:n], refs[n:2 * n]
        send_sems, recv_sems, local_sems = refs[2 * n:]
        x, y, c = _me()
        me, sibling = (x, y, c), (x, y, 1 - c)
        chips = [(1 - x, y), (x, 1 - y), (1 - x, 1 - y)]

        def copy(i, k, blk, to, src=None):
            r_, c_ = shards[i].shape
            dst = _block(out_refs[i], kinds[i], _index(*blk), r_, c_)
            return pltpu.make_async_remote_copy(
                src_ref=dst if src is None else src, dst_ref=dst,
                send_sem=send_sems.at[i, k], recv_sem=recv_sems.at[i, k], device_id=to, device_id_type=MESH)

        sends, local = [], []
        for i in range(n):
            r_, c_ = shards[i].shape
            mine = pltpu.make_async_copy(x_refs[i], _block(out_refs[i], kinds[i], _index(*me), r_, c_),
                                         local_sems.at[i])
            mine.start()
            local.append(mine)
            if i >= n_gather:
                continue
            first = [copy(i, 0, me, sibling, src=x_refs[i])]
            first += [copy(i, 1 + j, me, (*chip, c), src=x_refs[i]) for j, chip in enumerate(chips)]
            for cp in first:
                cp.start()
            sends += first
        for i in range(n_gather):
            for j, chip in enumerate(chips):
                copy(i, 1 + j, (*chip, c), me).wait_recv()
                passed = copy(i, 4 + j, (*chip, c), sibling)
                passed.start()
                sends.append(passed)
        for i in range(n_gather):
            copy(i, 0, sibling, me).wait_recv()
            for j, chip in enumerate(chips):
                copy(i, 4 + j, (*chip, 1 - c), me).wait_recv()
        for cp in sends:
            cp.wait_send()
        for cp in local:
            cp.wait()

    def full_shape(s, kind):
        r_, c_ = s.shape
        return (N_DEV * r_, c_) if kind == "rows" else (r_, N_DEV * c_)

    return pl.pallas_call(
        body, name="weights_all_gather",
        out_shape=[jax.ShapeDtypeStruct(full_shape(s, k), s.dtype) for s, k in zip(shards, kinds)],
        in_specs=[HBM_SPEC] * n, out_specs=[HBM_SPEC] * n,
        scratch_shapes=[pltpu.SemaphoreType.DMA((n_gather, 7)), pltpu.SemaphoreType.DMA((n_gather, 7)),
                        pltpu.SemaphoreType.DMA((n,))],
        compiler_params=pltpu.CompilerParams(has_side_effects=True),
    )(*shards)


SEM_SPEC = pl.BlockSpec(memory_space=pltpu.SEMAPHORE)
HBM_ONLY = pl.BlockSpec(memory_space=pltpu.HBM)
DATAFLOW = pltpu.SideEffectType.DATAFLOW_SIDE_EFFECTING


def _peers():
    x, y, c = _me()
    return [(x ^ (k >> 2), y ^ ((k >> 1) & 1), c ^ (k & 1)) for k in range(1, N_DEV)]


def _gather_copies(shapes, kinds):
    def make(src_refs, land_refs, send_sems, recv_sems):
        mine = _index(*_me())
        return [pltpu.make_async_remote_copy(
            src_ref=src_refs[i], dst_ref=_block(land_refs[i], kind, mine, r, c),
            send_sem=send_sems.at[7 * i + k], recv_sem=recv_sems.at[7 * i + k], device_id=peer, device_id_type=MESH)
            for i, ((r, c), kind) in enumerate(zip(shapes, kinds)) for k, peer in enumerate(_peers())]

    return make


def _exchange_copies(shapes, kinds):
    def make(src_refs, land_refs, send_sems, recv_sems):
        mine = _index(*_me())
        return [pltpu.make_async_remote_copy(
            src_ref=_block(src_refs[i], kind, _index(*peer), r, c), dst_ref=land_refs[i].at[mine],
            send_sem=send_sems.at[7 * i + k], recv_sem=recv_sems.at[7 * i + k], device_id=peer, device_id_type=MESH)
            for i, ((r, c), kind) in enumerate(zip(shapes, kinds)) for k, peer in enumerate(_peers())]

    return make


def _place_block(shard, kind, name):
    r, c = shard.shape
    tile = _row_tile(r, c)
    nt = r // tile
    full = (N_DEV * r, c) if kind == "rows" else (r, N_DEV * c)

    def body(me_ref, x_ref, out_ref):
        out_ref[...] = x_ref[...]

    if kind == "rows":
        out_spec = pl.BlockSpec((tile, c), lambda i, me: (me[0] * nt + i, 0))
    else:
        out_spec = pl.BlockSpec((tile, c), lambda i, me: (i, me[0]))
    return pl.pallas_call(
        body, name=name, out_shape=jax.ShapeDtypeStruct(full, shard.dtype),
        grid_spec=pltpu.PrefetchScalarGridSpec(
            num_scalar_prefetch=1, grid=(nt,),
            in_specs=[pl.BlockSpec((tile, c), lambda i, me: (i, 0))], out_specs=out_spec),
        compiler_params=_params(("arbitrary",)),
    )(_my_index_operand(), shard)


def _split_start(srcs, lands, make, name):
    n = len(srcs)

    def body(*refs):
        send_sems, recv_sems = refs[2 * n], refs[2 * n + 1]
        for cp in make(refs[:n], refs[n:2 * n], send_sems, recv_sems):
            cp.start()
        refs[-1][...] = jnp.zeros_like(refs[-1])

    arrays = list(srcs) + list(lands)
    out = pl.pallas_call(
        body, name=name,
        out_shape=(pltpu.SemaphoreType.DMA((7 * n,)), pltpu.SemaphoreType.DMA((7 * n,)),
                   *[pltpu.HBM(a.shape, a.dtype) for a in arrays], jax.ShapeDtypeStruct((SUBLANES, LANES), F32)),
        in_specs=[HBM_ONLY] * (2 * n),
        out_specs=(SEM_SPEC, SEM_SPEC, *[HBM_ONLY] * (2 * n), pl.BlockSpec(memory_space=pltpu.VMEM)),
        input_output_aliases={i: 2 + i for i in range(2 * n)},
        compiler_params=pltpu.CompilerParams(has_side_effects=DATAFLOW),
    )(*[pltpu.with_memory_space_constraint(a, pltpu.HBM) for a in arrays])
    return out[0], out[1], list(out[2:2 + n]), list(out[2 + n:2 + 2 * n]), out[-1]


def _split_wait(send_sems, recv_sems, srcs, lands, after, make, name):
    n = len(srcs)

    def body(*refs):
        for cp in make(refs[:n], refs[n:2 * n], refs[2 * n], refs[2 * n + 1]):
            cp.wait_send()
            cp.wait_recv()

    arrays = list(srcs) + list(lands)
    out = pl.pallas_call(
        body, name=name,
        out_shape=tuple(pltpu.HBM(a.shape, a.dtype) for a in arrays),
        in_specs=[HBM_ONLY] * (2 * n) + [SEM_SPEC, SEM_SPEC, pl.BlockSpec(memory_space=pl.ANY)],
        out_specs=tuple([HBM_ONLY] * (2 * n)),
        input_output_aliases={i: i for i in range(2 * n)},
        compiler_params=pltpu.CompilerParams(has_side_effects=DATAFLOW),
    )(*arrays, send_sems, recv_sems, after)
    return list(out[:n]), list(out[n:])


def _all_reduce_small(vals, name):
    rows, width = vals.shape

    def body(x_ref, out_ref, buf_ref, send_sems, recv_sems):
        x, y, c = _me()
        mine = _index(x, y, c)
        buf_ref[mine] = x_ref[...]
        copies = []
        for k in range(1, N_DEV):
            px, py, pc = x ^ (k >> 2), y ^ ((k >> 1) & 1), c ^ (k & 1)
            copies.append(pltpu.make_async_remote_copy(
                src_ref=x_ref, dst_ref=buf_ref.at[mine],
                send_sem=send_sems.at[k - 1], recv_sem=recv_sems.at[k - 1],
                device_id=(px, py, pc), device_id_type=MESH))
        for cp in copies:
            cp.start()
        for cp in copies:
            cp.wait()
        acc = buf_ref[0]
        for j in range(1, N_DEV):
            acc = acc + buf_ref[j]
        out_ref[...] = acc

    vmem = pl.BlockSpec(memory_space=pltpu.VMEM)
    return pl.pallas_call(
        body, name=name,
        out_shape=jax.ShapeDtypeStruct(vals.shape, F32),
        in_specs=[vmem], out_specs=vmem,
        scratch_shapes=[pltpu.VMEM((N_DEV, rows, width), F32),
                        pltpu.SemaphoreType.DMA((7,)), pltpu.SemaphoreType.DMA((7,))],
        compiler_params=pltpu.CompilerParams(has_side_effects=True),
    )(vals)


def _adamw_math(w, g, m, v):
    m = ADAM_B1 * m + (1.0 - ADAM_B1) * g
    v = ADAM_B2 * v + (1.0 - ADAM_B2) * (g * g)
    m_hat = m / (1.0 - ADAM_B1 ** ADAM_STEP)
    v_hat = v / (1.0 - ADAM_B2 ** ADAM_STEP)
    delta = -ADAM_LR * (m_hat / (jnp.sqrt(v_hat) + ADAM_EPS) + ADAM_WD * w)
    return delta, m, v


ROW_TILE_ELEMS = 384 * 1024


def _row_tile(rows, width):
    best = SUBLANES
    for t in range(SUBLANES, rows + 1, SUBLANES):
        if rows % t == 0 and t * width <= ROW_TILE_ELEMS:
            best = t
    return best


def _sum_received(r_ref, own, me):
    g = None
    for j in range(N_DEV):
        term = jnp.where(me == j, own, r_ref[j].astype(F32))
        g = term if g is None else g + term
    return g


def _my_index_operand():
    return _index(*_me()).astype(jnp.int32).reshape(1)


def _sum_small(recv, own):
    def body(me_ref, r_ref, own_ref, out_ref):
        out_ref[...] = _sum_received(r_ref, own_ref[...], me_ref[0])

    whole = lambda shape: pl.BlockSpec(shape, lambda i, me, nd=len(shape): (0,) * nd)
    return pl.pallas_call(
        body, name="small_grads_sum", out_shape=jax.ShapeDtypeStruct(own.shape, F32),
        grid_spec=pltpu.PrefetchScalarGridSpec(
            num_scalar_prefetch=1, grid=(1,), in_specs=[whole(recv.shape), whole(own.shape)],
            out_specs=whole(own.shape)),
        compiler_params=_params(("arbitrary",)),
    )(_my_index_operand(), recv, own)


def _adamw_recv(recv, grad, kind, w, m, v, name):
    _, rows, width = recv.shape
    tile = _row_tile(rows, width)
    nt = rows // tile

    def body(me_ref, r_ref, own_ref, w_ref, m_ref, v_ref, g_out, d_out, m_out, v_out):
        g = _sum_received(r_ref, own_ref[...].astype(F32), me_ref[0])
        d, mn, vn = _adamw_math(w_ref[...], g, m_ref[...], v_ref[...])
        g_out[...] = g
        d_out[...] = d
        m_out[...] = mn
        v_out[...] = vn

    if kind == "rows":
        own_spec = pl.BlockSpec((tile, width), lambda i, me: (me[0] * nt + i, 0))
    else:
        own_spec = pl.BlockSpec((tile, width), lambda i, me: (i, me[0]))
    spec = pl.BlockSpec((tile, width), lambda i, me: (i, 0))
    shape = jax.ShapeDtypeStruct((rows, width), F32)
    return pl.pallas_call(
        body, name=name, out_shape=[shape] * 4,
        grid_spec=pltpu.PrefetchScalarGridSpec(
            num_scalar_prefetch=1, grid=(nt,),
            in_specs=[pl.BlockSpec((N_DEV, tile, width), lambda i, me: (0, i, 0)), own_spec, spec, spec, spec],
            out_specs=[spec] * 4),
        compiler_params=_params(("parallel",)),
    )(_my_index_operand(), recv, grad, w, m, v)


WIN_STEP = 1408
WIN_W = 1536
IN_SHARD = IN_COLS // N_DEV
IN_PADDED = WIN_STEP * (N_DEV - 1) + WIN_W


def _roll_w_in(shard_padded):
    rows = shard_padded.shape[0]
    tile = _row_tile(rows, WIN_W)

    def body(x_ref, main_ref, edge_ref):
        win = pltpu.roll(x_ref[...], 2 * _index(*_me()), 1).astype(BF16)
        main_ref[...] = win[:, :WIN_STEP]
        edge_ref[...] = win[:, WIN_STEP:]

    return pl.pallas_call(
        body, name="w_in_window",
        out_shape=[jax.ShapeDtypeStruct((rows, WIN_STEP), BF16), jax.ShapeDtypeStruct((rows, WIN_W - WIN_STEP), BF16)],
        grid=(rows // tile,),
        in_specs=[pl.BlockSpec((tile, WIN_W), lambda i: (i, 0))],
        out_specs=[pl.BlockSpec((tile, WIN_STEP), lambda i: (i, 0)),
                   pl.BlockSpec((tile, WIN_W - WIN_STEP), lambda i: (i, 0))],
        compiler_params=_params(("parallel",)),
    )(shard_padded)


def _sum_w_in_windows(recv, grad):
    _, rows, width = recv.shape
    tile = _row_tile(rows, width)

    def body(me_ref, r_ref, g_ref, g_out, own_ref, sem):
        me = me_ref[0]
        rows_i = pl.ds(pl.multiple_of(pl.program_id(0) * tile, tile), tile)
        own = pltpu.make_async_copy(g_ref.at[rows_i, pl.ds(pl.multiple_of(me * WIN_STEP, LANES), width)], own_ref, sem)
        own.start()
        own.wait()
        g_out[...] = pltpu.roll(_sum_received(r_ref, own_ref[...].astype(F32), me), width - 2 * me, 1)

    return pl.pallas_call(
        body, name="w_in_grad_sum", out_shape=jax.ShapeDtypeStruct((rows, width), F32),
        grid_spec=pltpu.PrefetchScalarGridSpec(
            num_scalar_prefetch=1, grid=(rows // tile,),
            in_specs=[pl.BlockSpec((N_DEV, tile, width), lambda i, me: (0, i, 0)), HBM_SPEC],
            out_specs=pl.BlockSpec((tile, width), lambda i, me: (i, 0)),
            scratch_shapes=[pltpu.VMEM((tile, width), BF16), pltpu.SemaphoreType.DMA]),
        compiler_params=_params(("arbitrary",)),
    )(_my_index_operand(), recv, grad)


def _adamw_small(w, g, m, v, name):
    def fn(i, n, w_, g_, m_, v_):
        return _adamw_math(w_, g_, m_, v_)

    r, c = w.shape
    return _rows(fn, [(w, "t"), (g, "t"), (m, "t"), (v, "t")], [], [(c, F32)] * 3, [], _row_tile(r, c), name)


def _norm_fwd(x, w, name):
    return _rows(lambda i, n, x_, w_: (_rms(x_, w_[...]),), [(x, "t")], [w], [(D_MODEL, BF16)], [], 512, name)[0]


def _residual_norm_fwd(x, y, scale, w, name):
    def fn(i, n, x_, y_, w_):
        xn = x_ + scale * y_
        return xn, _rms(xn, w_[...])

    return _rows(fn, [(x, "t"), (y, "t")], [w], [(D_MODEL, F32), (D_MODEL, BF16)], [], 512, name)


def _residual_norm_bwd(x, w, dhs, dres, scale, name):
    nh = len(dhs)

    def fn(i, n, x_, dres_, *rest):
        dh = rest[0]
        for extra in rest[1:nh]:
            dh = dh + extra
        _, vjp = jax.vjp(_rms, x_, rest[nh][...])
        dx, dw = vjp(dh)
        dx = dx + dres_
        return dx, scale * dx, dw

    return _rows(fn, [(x, "t"), (dres, "t")] + [(d, "t") for d in dhs], [w],
                 [(D_MODEL, F32), (D_MODEL, BF16)], [(1, D_MODEL)], 256, name)


FFN_UP_TN = 512


def _ffn_up(h, w_gu, name):
    t, d = h.shape
    f = w_gu.shape[1] // 2
    tm = _pick(t, (1024, 512, 256, 128))
    nj = f // FFN_UP_TN

    def body(h_ref, wg_ref, wu_ref, g_ref, u_ref, act_ref):
        hb = h_ref[...]
        g = jnp.dot(hb, wg_ref[...], preferred_element_type=F32)
        u = jnp.dot(hb, wu_ref[...], preferred_element_type=F32)
        g_ref[...] = g.astype(BF16)
        u_ref[...] = u.astype(BF16)
        act_ref[...] = (_silu(g) * u).astype(BF16)

    out = pl.BlockSpec((tm, FFN_UP_TN), lambda i, j: (i, j))
    return pl.pallas_call(
        body, name=name, out_shape=[jax.ShapeDtypeStruct((t, f), BF16)] * 3, grid=(t // tm, nj),
        in_specs=[pl.BlockSpec((tm, d), lambda i, j: (i, 0)),
                  pl.BlockSpec((d, FFN_UP_TN), lambda i, j: (0, j)),
                  pl.BlockSpec((d, FFN_UP_TN), lambda i, j: (0, j + nj))],
        out_specs=[out, out, out],
        compiler_params=_params(("parallel", "parallel")),
    )(h, w_gu, w_gu)


def _ffn_fwd(h, w_gu, get_w_down, tag):
    g, u, act = _ffn_up(h, w_gu, tag + "_gu")
    y = _matmul(act, get_w_down(act), "nn", F32, tag + "_down")
    return (g, u), act, y


def _ffn_dact(dy, w_down, g, u, name):
    t, d = dy.shape
    f = w_down.shape[0]
    tm = _pick(t, (1024, 512, 256, 128))

    def body(dy_ref, w_ref, g_ref, u_ref, out_ref):
        dact = lax.dot_general(dy_ref[...], w_ref[...], NT, preferred_element_type=F32)
        g_, u_ = g_ref[...].astype(F32), u_ref[...].astype(F32)
        sg = _sigmoid(g_)
        out_ref[0] = (dact * u_ * (sg * (1.0 + g_ * (1.0 - sg)))).astype(BF16)
        out_ref[1] = (dact * (g_ * sg)).astype(BF16)

    tile = pl.BlockSpec((tm, FFN_UP_TN), lambda i, j: (i, j))
    return pl.pallas_call(
        body, name=name, out_shape=jax.ShapeDtypeStruct((2, t, f), BF16), grid=(t // tm, f // FFN_UP_TN),
        in_specs=[pl.BlockSpec((tm, d), lambda i, j: (i, 0)), pl.BlockSpec((FFN_UP_TN, d), lambda i, j: (j, 0)),
                  tile, tile],
        out_specs=pl.BlockSpec((2, tm, FFN_UP_TN), lambda i, j: (0, i, j)),
        compiler_params=_params(("parallel", "parallel")),
    )(dy, w_down, g, u)


def _ffn_bwd(h, gu, act, dy, w_gu, w_down, tag, comm, more=None):
    dgu = _ffn_dact(dy, w_down, gu[0], gu[1], tag + "_dact")
    sent = comm.send(tag + "_gu", {tag + "_w_gu": _matmul(h, dgu, "tn", BF16, tag + "_d_w_gu")})
    sent = sent + comm.send(tag + "_down", {tag + "_w_down": _matmul(act, dy + sent.astype(BF16), "tn", BF16,
                                                                    tag + "_d_w_down"), **(more or {})})
    dh = _matmul(dgu, w_gu, "nt", BF16, tag + "_dh")
    return dh, sent


def _expanders():
    e_g = np.zeros((LANES, HW), np.float32)
    e_b = np.zeros((LANES, HW), np.float32)
    for h in range(HEADS):
        e_g[h, h * HEAD_DIM:(h + 1) * HEAD_DIM] = 1.0
        e_b[HEADS + h, h * HEAD_DIM:(h + 1) * HEAD_DIM] = 1.0
    return jnp.asarray(e_g), jnp.asarray(e_b)


def _pad_lanes(v):
    return jnp.pad(v, ((0, 0), (0, LANES - v.shape[1])))


class _LocalWeights:
    def __init__(self, big):
        self.big, self.sent = big, {}

    def arrive(self, group, after):
        return self.big

    def send(self, group, grads):
        self.sent.update(grads)
        return jnp.zeros((), F32)


def _local_step(x, p, tgt, small, comm):
    e_g, e_b = _expanders()
    alog, dtb = _pad_lanes(small["a_log"]), _pad_lanes(small["dt_bias"])
    conv_w = jnp.pad(small["conv_w"], ((0, SUBLANES - CONV_K), (0, 0)))
    rel = _expand_rel_bias(small["rel_bias"])

    h1 = _norm_fwd(x, small["ffn1_norm"], "ffn1_norm")
    big = dict(comm.arrive("ffn1", h1))
    if "_token" in big:
        h1 = h1 + big.pop("_token").astype(BF16)

    def ffn1_w_down(act):
        big.update(comm.arrive("ffn1_down", act))
        return big["ffn1_w_down"]

    gu1, act1, y1 = _ffn_fwd(h1, big["ffn1_w_gu"], ffn1_w_down, "ffn1")
    x1, h2 = _residual_norm_fwd(x, y1, 0.5, small["mix_norm"], "mix_norm")

    big = {**big, **comm.arrive("mixer", h2)}
    w_in = big["w_in"]
    w_qz = w_in[:, :IN_QZ]
    w_ab = jnp.pad(w_in[:, IN_AB0:IN_QKVB0], ((0, 0), (0, LANES - 2 * HEADS)))
    w_qkvb = w_in[:, IN_QKVB0:IN_GG0]
    w_gg = w_in[:, IN_GG0:IN_COLS]
    qz = _matmul(h2, w_qz, "nn", F32, "in_qz")
    ab = _matmul(h2, w_ab, "nn", F32, "in_ab")
    pb = _matmul(h2, w_qkvb, "nn", F32, "in_qkvb")
    gg = _matmul(h2, w_gg, "nn", BF16, "in_gates")
    pa, z = qz[:, :3 * HW], qz[:, 3 * HW:]

    def prep(i, n, pa_, prev_, ab_, cw_, alog_, dtb_, eg_, eb_):
        q, k, v = _gdn_post(_conv(pa_, prev_, cw_, i))
        g_b, beta_b = _gdn_gates(ab_, alog_[...], dtb_[...], eg_[...], eb_[...])
        return q, k, v, g_b, beta_b

    qn, kn, vv, g_b, beta_b = _rows(prep, [(pa, "t"), (pa, "p"), (ab, "t")], [conv_w, alog, dtb, e_g, e_b],
                                    [(HW, F32)] * 5, [], 256, "gdn_prep")
    u, w, aqk, qd, kt, tl = _gdn_intra(qn, kn, vv, g_b, beta_b)
    o, states = _gdn_scan(u, w, aqk, qd, kt, tl)
    ya = _rows(lambda i, n, o_, z_, w_: (_gated_norm(o_, z_, w_[...]),), [(o, "t"), (z, "t")], [small["gdn_norm"]],
               [(HW, BF16)], [], 512, "gdn_gated_norm")[0]

    yb = _attention(pb, small["q_norm"], small["k_norm"], rel)

    big = {**big, **comm.arrive("branches", yb)}
    ta = _matmul(ya, big["w_branch_a"], "nn", BF16, "branch_a")
    tb = _matmul(yb, big["w_branch_b"], "nn", BF16, "branch_b")
    mixed = _rows(lambda i, n, gg_, ta_, tb_: (_mix(gg_, ta_, tb_),), [(gg, "t"), (ta, "t"), (tb, "t")], [],
                  [(D_MODEL, BF16)], [], 256, "mix")[0]
    m_out = _matmul(mixed, big["w_out"], "nn", F32, "w_out")
    x2, h3 = _residual_norm_fwd(x1, m_out, 1.0, small["ffn2_norm"], "ffn2_norm")
    big = {**big, **comm.arrive("tail", h3)}
    gu2, act2, y2 = _ffn_fwd(h3, big["ffn2_w_gu"], lambda act: big["ffn2_w_down"], "ffn2")
    x3, h4 = _residual_norm_fwd(x2, y2, 0.5, small["ple_norm"], "ple_norm")
    gp = _matmul(h4, big["ple_gate"], "nn", BF16, "ple_gate")
    pp = _matmul(p, big["ple_proj"], "nn", BF16, "ple_proj")

    def head(i, n, x3_, gp_, pp_, tgt_):
        sg = _sigmoid(gp_)
        err = x3_ + sg * pp_ - tgt_
        dx4 = err * (1.0 / D_MODEL)
        sq = _colsum(err * err)
        part = sq[:, :LANES]
        for j in range(1, D_MODEL // LANES):
            part = part + sq[:, j * LANES:(j + 1) * LANES]
        return dx4, dx4 * pp_ * sg * (1.0 - sg), dx4 * sg, (0.5 / D_MODEL) * part

    dx4, dgp, dpp, loss_lanes = _rows(head, [(x3, "t"), (gp, "t"), (pp, "t"), (tgt, "t")], [],
                                      [(D_MODEL, F32), (D_MODEL, BF16), (D_MODEL, BF16)], [(1, LANES)], 256,
                                      "ple_loss_head")
    loss = jnp.sum(loss_lanes)

    gbig, gsmall = {}, {}
    gbig["ple_proj"] = _matmul(p, dpp, "tn", BF16, "d_ple_proj")
    gbig["ple_gate"] = _matmul(h4, dgp, "tn", BF16, "d_ple_gate")
    dh4 = _matmul(dgp, big["ple_gate"], "nt", BF16, "ple_gate_dh")
    dx3, dy2, gsmall["ple_norm"] = _residual_norm_bwd(x3, small["ple_norm"], [dh4], dx4, 0.5, "ple_norm_bwd")

    dh3, sent = _ffn_bwd(h3, gu2, act2, dy2, big["ffn2_w_gu"], big["ffn2_w_down"], "ffn2", comm,
                         {n: gbig[n] for n in ("ple_proj", "ple_gate")})
    dx2, dx2b, gsmall["ffn2_norm"] = _residual_norm_bwd(x2, small["ffn2_norm"] + sent, [dh3], dx3, 1.0,
                                                        "ffn2_norm_bwd")

    gbig["w_out"] = _matmul(mixed, dx2b, "tn", BF16, "d_w_out")
    dmixed = _matmul(dx2b, big["w_out"], "nt", BF16, "w_out_dx")

    def mix_bwd(i, n, gg_, ta_, tb_, dm_):
        _, vjp = jax.vjp(_mix, gg_, ta_, tb_)
        return vjp(dm_)

    dgg, dta, dtb_ = _rows(mix_bwd, [(gg, "t"), (ta, "t"), (tb, "t"), (dmixed, "t")], [],
                           [(2 * D_MODEL, BF16), (D_MODEL, BF16), (D_MODEL, BF16)], [], 256, "mix_bwd")
    gbig["w_branch_a"] = _matmul(ya, dta, "tn", BF16, "d_branch_a")
    gbig["w_branch_b"] = _matmul(yb, dtb_, "tn", BF16, "d_branch_b")
    dya = _matmul(dta, big["w_branch_a"], "nt", BF16, "branch_a_dx")
    dyb = _matmul(dtb_, big["w_branch_b"], "nt", BF16, "branch_b_dx")

    dq_b, dk_b, dv_b, gsmall["q_norm"], gsmall["k_norm"], gsmall["rel_bias"] = _attention_bwd(
        pb, small["q_norm"], small["k_norm"], rel, dyb)
    dpb = jnp.concatenate([dq_b, dk_b[ATT_PAD:].astype(BF16), dv_b[ATT_PAD:].astype(BF16)], axis=1)

    def gated_bwd(i, n, o_, z_, dya_, w_):
        _, vjp = jax.vjp(_gated_norm, o_, z_, w_[...])
        return vjp(dya_)

    do, dz, gsmall["gdn_norm"] = _rows(gated_bwd, [(o, "t"), (z, "t"), (dya, "t")], [small["gdn_norm"]],
                                       [(HW, F32), (HW, BF16)], [(1, HEAD_DIM)], 256, "gdn_gated_norm_bwd")
    du, dw, da, dqd, dkt, dtl = _gdn_scan_bwd(do, u, w, aqk, qd, kt, tl, states)
    dqn, dkn, dvv, dg_b, dbeta_b = _gdn_intra_bwd(qn, kn, vv, g_b, beta_b, du, dw, da, dqd, dkt, dtl)

    def prep_bwd(i, n, pa_, prev_, ab_, dq_, dk_, dv_, dg_, db_, cw_, alog_, dtb_, eg_, eb_):
        _, vjp = jax.vjp(_gdn_post, _conv(pa_, prev_, cw_, i))
        (dy,) = vjp((dq_, dk_, dv_))
        e_g_, e_b_ = eg_[...], eb_[...]
        _, vjp_g = jax.vjp(lambda a, b, c: _gdn_gates(a, b, c, e_g_, e_b_), ab_, alog_[...], dtb_[...])
        dab, dalog, ddtb = vjp_g((dg_, db_))
        return dy, dab, dalog, ddtb

    dy_conv, dab, dalog, ddtb = _rows(
        prep_bwd, [(pa, "t"), (pa, "p"), (ab, "t"), (dqn, "t"), (dkn, "t"), (dvv, "t"), (dg_b, "t"), (dbeta_b, "t")],
        [conv_w, alog, dtb, e_g, e_b], [(3 * HW, F32), (LANES, BF16)], [(1, LANES), (1, LANES)], 256,
        "gdn_prep_bwd")
    gsmall["a_log"] = dalog[:, :HEADS]
    gsmall["dt_bias"] = ddtb[:, :HEADS]

    def conv_bwd(i, n, dy_, nxt_, pa_, prev_, cw_):
        dpa = dy_ * cw_[CONV_K - 1:CONV_K, :]
        row = lax.broadcasted_iota(jnp.int32, (SUBLANES, dy_.shape[1]), 0)
        dcw = jnp.where(row == CONV_K - 1, _colsum(dy_ * pa_), 0.0)
        for j in range(CONV_K - 1):
            s = CONV_K - 1 - j
            dpa = dpa + _shift_up(dy_, nxt_, s, i, n) * cw_[j:j + 1, :]
            dcw = dcw + jnp.where(row == j, _colsum(dy_ * _shift_down(pa_, prev_, s, i)), 0.0)
        return dpa, dcw

    dpa, dcw = _rows(conv_bwd, [(dy_conv, "t"), (dy_conv, "n"), (pa, "t"), (pa, "p")], [conv_w],
                     [(3 * HW, BF16)], [(SUBLANES, 3 * HW)], 256, "gdn_conv_bwd")
    gsmall["conv_w"] = dcw[:CONV_K]

    dqz = jnp.concatenate([dpa, dz], axis=1)
    d_w_qz = _matmul(h2, dqz, "tn", BF16, "d_in_qz")
    d_w_ab = _matmul(h2, dab, "tn", BF16, "d_in_ab")
    d_w_qkvb = _matmul(h2, dpb, "tn", BF16, "d_in_qkvb")
    d_w_gg = _matmul(h2, dgg, "tn", BF16, "d_in_gates")
    gbig["w_in"] = jnp.concatenate([d_w_qz, d_w_ab[:, :2 * HEADS], d_w_qkvb, d_w_gg,
                                    jnp.zeros((D_MODEL, IN_PADDED - IN_COLS), BF16)], axis=1)
    dh2 = [_matmul(dqz, w_qz, "nt", BF16, "in_qz_dh"), _matmul(dab, w_ab, "nt", BF16, "in_ab_dh"),
           _matmul(dpb, w_qkvb, "nt", BF16, "in_qkvb_dh"), _matmul(dgg, w_gg, "nt", BF16, "in_gates_dh")]
    sent = comm.send("mixer", {n: gbig[n] for n in ("w_out", "w_branch_b", "w_branch_a", "w_in")})
    dx1, dy1, gsmall["mix_norm"] = _residual_norm_bwd(x1, small["mix_norm"] + sent, dh2, dx2, 0.5, "mix_norm_bwd")

    dh1, sent = _ffn_bwd(h1, gu1, act1, dy1, big["ffn1_w_gu"], big["ffn1_w_down"], "ffn1", comm)
    grad_x, _, gsmall["ffn1_norm"] = _residual_norm_bwd(x, small["ffn1_norm"] + sent, [dh1], dx1, 1.0,
                                                        "ffn1_norm_bwd")
    return loss, grad_x, gsmall


GATHER_GROUPS = {"ffn1": ("ffn1_w_gu",),
                 "ffn1_down": ("ffn1_w_down",),
                 "mixer": ("w_in_main", "w_in_edge"),
                 "branches": ("w_branch_a", "w_branch_b", "w_out"),
                 "tail": ("ffn2_w_gu", "ffn2_w_down", "ple_gate", "ple_proj")}
SPLIT_GATHERS = ("ffn1_down", "mixer", "branches", "tail")


def _kind(name):
    return "cols" if name in COL_SHARDED or name.startswith("w_in_") else "rows"


def _merge_w_in(main, edges):
    edge_w = WIN_W - WIN_STEP
    w_in = jnp.pad(main, ((0, 0), (0, edge_w)))
    for d in range(N_DEV):
        at = WIN_STEP * (d + 1)
        w_in = w_in + jnp.pad(edges[:, d * edge_w:(d + 1) * edge_w], ((0, 0), (at, IN_PADDED - at - edge_w)))
    return w_in


class _Fsdp:
    def __init__(self, wts, first):
        self.wts, self.first_token = wts, first
        main, edge = _roll_w_in(jnp.pad(wts["w_in"], ((0, 0), (0, WIN_W - IN_SHARD))))
        self.shards = {n: wts[n].astype(BF16) for n in BIG if n not in ("w_in", "ffn1_w_gu")}
        self.shards.update(w_in_main=main, w_in_edge=edge)
        self.lands = {n: _place_block(self.shards[n], _kind(n), "own_" + n)
                      for group in SPLIT_GATHERS for n in GATHER_GROUPS[group]}
        self.flight, self.sent = {}, {}

    def _gather_first(self, after):
        token = self.first_token + after[0, 0].astype(F32) * 0.0
        me = _index(*_me())
        for n, land in self.lands.items():
            r, c = self.shards[n].shape
            at = (me * r, 0) if _kind(n) == "rows" else (0, me * c)
            token = token + lax.dynamic_slice(land, at, (1, 1))[0, 0].astype(F32) * 0.0
        shard = (self.wts["ffn1_w_gu"] + token).astype(BF16)
        self.shards["ffn1_w_gu"] = shard
        first = _all_gather([shard], [_kind("ffn1_w_gu")], 1)[0]
        token = first[0, 0].astype(F32) * 0.0
        for group in SPLIT_GATHERS:
            names = GATHER_GROUPS[group]
            srcs = [self.shards[n] for n in names]
            lands = [self.lands[n] for n in names]
            make = _gather_copies([s.shape for s in srcs], [_kind(n) for n in names])
            srcs[0] = srcs[0] + token.astype(BF16)
            send_sems, recv_sems, srcs, lands, tok = _split_start(srcs, lands, make, "gather_start_" + group)
            token = token + tok[0, 0]
            self.flight[group] = (send_sems, recv_sems, srcs, lands, make)
        return {"ffn1_w_gu": first, "_token": token}

    def arrive(self, group, after):
        if group == "ffn1":
            return self._gather_first(after)
        send_sems, recv_sems, srcs, lands, make = self.flight[group]
        _, lands = _split_wait(send_sems, recv_sems, srcs, lands, after, make, "gather_wait_" + group)
        full = dict(zip(GATHER_GROUPS[group], lands))
        if group == "mixer":
            full["w_in"] = _merge_w_in(full.pop("w_in_main"), full.pop("w_in_edge"))
        return full

    def send(self, group, grads):
        names = list(grads)
        kinds = ["all" if n == "small" else "win" if n == "w_in" else _kind(n) for n in names]
        shapes = [grads[n].shape if n == "small" else (D_MODEL, WIN_W) if n == "w_in" else self.shards[n].shape
                  for n in names]
        srcs = [grads[n] for n in names]
        lands = [lax.empty((N_DEV,) + tuple(s), g.dtype) for s, g in zip(shapes, srcs)]
        make = _exchange_copies(shapes, kinds)
        send_sems, recv_sems, srcs, lands, tok = _split_start(srcs, lands, make, "grads_start_" + group)
        self.sent[group] = (names, kinds, send_sems, recv_sems, srcs, lands, make)
        return tok[0, 0]

    def received(self, group, after):
        names, kinds, send_sems, recv_sems, srcs, lands, make = self.sent[group]
        srcs, lands = _split_wait(send_sems, recv_sems, srcs, lands, after, make, "grads_wait_" + group)
        return {n: (k, g, r) for n, k, g, r in zip(names, kinds, srcs, lands)}


SMALL_ROWS = ("ffn1_norm", "mix_norm", "ffn2_norm", "ple_norm", "gdn_norm", "q_norm", "k_norm", "a_log", "dt_bias",
              "rel_bias", "conv_w")


def _pack_small(vals):
    rows = []
    for n in SMALL_ROWS:
        v = vals[n]
        if n == "rel_bias":
            v = jnp.pad(v, ((0, 0), (0, 2 * LANES - N_REL)))
        elif n in ("a_log", "dt_bias"):
            v = _pad_lanes(v)
        rows.append(v.reshape(-1, LANES))
    packed = jnp.concatenate(rows, axis=0)
    return jnp.pad(packed, ((0, -packed.shape[0] % SUBLANES), (0, 0)))


def _unpack_small(packed, shapes):
    out, off = {}, 0
    for n in SMALL_ROWS:
        shp = shapes[n]
        if n == "rel_bias":
            out[n] = packed[off:off + 2 * HEADS].reshape(HEADS, 2 * LANES)[:, :N_REL]
            off += 2 * HEADS
        elif n in ("a_log", "dt_bias"):
            out[n] = packed[off:off + 1, :HEADS]
            off += 1
        else:
            r = int(np.prod(shp)) // LANES
            out[n] = packed[off:off + r].reshape(shp)
            off += r
    return out


WEIGHTS = ("ffn1_norm", "ffn1_w_gu", "ffn1_w_down", "mix_norm", "w_in", "conv_w", "a_log", "dt_bias", "gdn_norm",
           "q_norm", "k_norm", "rel_bias", "w_branch_a", "w_branch_b", "w_out", "ffn2_norm", "ffn2_w_gu",
           "ffn2_w_down", "ple_norm", "ple_gate", "ple_proj")


def kernel(x, p, ffn1_norm, ffn1_w_gu, ffn1_w_down, mix_norm, w_in, conv_w, a_log, dt_bias, gdn_norm, q_norm, k_norm, rel_bias, w_branch_a, w_branch_b, w_out, ffn2_norm, ffn2_w_gu, ffn2_w_down, ple_norm, ple_gate, ple_proj, loss_target, m_ffn1_norm, m_ffn1_w_gu, m_ffn1_w_down, m_mix_norm, m_w_in, m_conv_w, m_a_log, m_dt_bias, m_gdn_norm, m_q_norm, m_k_norm, m_rel_bias, m_w_branch_a, m_w_branch_b, m_w_out, m_ffn2_norm, m_ffn2_w_gu, m_ffn2_w_down, m_ple_norm, m_ple_gate, m_ple_proj, v_ffn1_norm, v_ffn1_w_gu, v_ffn1_w_down, v_mix_norm, v_w_in, v_conv_w, v_a_log, v_dt_bias, v_gdn_norm, v_q_norm, v_k_norm, v_rel_bias, v_w_branch_a, v_w_branch_b, v_w_out, v_ffn2_norm, v_ffn2_w_gu, v_ffn2_w_down, v_ple_norm, v_ple_gate, v_ple_proj):
    args = dict(locals())
    def layer0(v):
        return v[0] if v.ndim == 3 else v

    wts = {n: layer0(args[n]) for n in WEIGHTS}
    mom = {n: layer0(args["m_" + n]) for n in WEIGHTS}
    var = {n: layer0(args["v_" + n]) for n in WEIGHTS}
    x2d, p2d, tgt = x[0], p[0, 0], loss_target[0]
    my_index = _index(*_me())

    small = {n: wts[n] for n in SMALL_ROWS if n != "conv_w"}
    conv_shard = wts["conv_w"]
    conv_cols = conv_shard.shape[1]
    conv_packed = jnp.zeros((SUBLANES, N_DEV * conv_cols), F32)
    conv_packed = lax.dynamic_update_slice(conv_packed, jnp.pad(conv_shard, ((0, SUBLANES - CONV_K), (0, 0))),
                                           (0, my_index * conv_cols))
    small["conv_w"] = _all_reduce_small(conv_packed.reshape(-1, LANES), "conv_w_gather").reshape(SUBLANES, -1)[:CONV_K]

    fsdp = _Fsdp(wts, small["conv_w"][0, 0] * 0.0)

    loss, grad_x, gsmall = _local_step(x2d, p2d, tgt, small, fsdp)
    loss = lax.psum(loss, ("x", "y", "c"))

    fsdp.send("small", {"small": _pack_small(gsmall)})

    outs_big, after = {}, grad_x
    for group in list(fsdp.sent):
        for n, (kind, grad, recv) in fsdp.received(group, after).items():
            if n == "small":
                small_sum = _sum_small(recv, grad)
            elif n == "w_in":
                g_in = _sum_w_in_windows(recv, grad)[:, :IN_SHARD]
                outs_big[n] = [g_in] + list(_adamw_small(wts[n], g_in, mom[n], var[n], "adamw_w_in"))
            else:
                outs_big[n] = _adamw_recv(recv, grad, kind, wts[n], mom[n], var[n], "adamw_" + n)
            after = small_sum if n == "small" else outs_big[n][1]

    small_shapes = {n: (small[n].shape if n != "conv_w" else (CONV_K, N_DEV * conv_cols)) for n in SMALL_ROWS}
    gsum = _unpack_small(small_sum, small_shapes)
    gsum["conv_w"] = lax.dynamic_slice(gsum["conv_w"], (0, my_index * conv_cols), (CONV_K, conv_cols))
    rep = [n for n in SMALL_ROWS if n != "conv_w"]
    rep_shapes = {n: small_shapes[n] for n in rep}

    def pack_rep(vals):
        return _pack_small({**{n: vals[n] for n in rep}, "conv_w": jnp.zeros((CONV_K, LANES), F32)})

    def unpack_rep(packed):
        return _unpack_small(packed, {**rep_shapes, "conv_w": (CONV_K, LANES)})

    outs_small = [unpack_rep(o) for o in _adamw_small(pack_rep(wts), pack_rep(gsum), pack_rep(mom), pack_rep(var),
                                                      "adamw_replicated")]
    pad8 = functools.partial(jnp.pad, pad_width=((0, SUBLANES - CONV_K), (0, 0)))
    outs_conv = [o[:CONV_K] for o in _adamw_small(pad8(conv_shard), pad8(gsum["conv_w"]), pad8(mom["conv_w"]),
                                                   pad8(var["conv_w"]), "adamw_conv")]

    def leaf(kind, n):
        if n in BIG:
            return outs_big[n][kind][None]
        if n == "conv_w":
            return (gsum["conv_w"] if kind == 0 else outs_conv[kind - 1])[None]
        return (gsum[n] if kind == 0 else outs_small[kind - 1][n]).reshape(args[n].shape)

    result = [loss, grad_x[None]]
    for kind in range(4):
        result += [leaf(kind, n) for n in WEIGHTS]
    return tuple(result)
```

```python
import functools

import numpy as np
import jax
import jax.numpy as jnp
from jax import lax
from jax.experimental import pallas as pl
from jax.experimental.pallas import tpu as pltpu

F32 = jnp.float32
BF16 = jnp.bfloat16
HIGHEST = lax.Precision.HIGHEST
MESH = pl.DeviceIdType.MESH

D_MODEL = 2048
D_FF = 5632
HEADS = 8
HEAD_DIM = 128
HW = HEADS * HEAD_DIM
CHUNK = 64
LEFT_CHUNKS = 8
MAX_REL = 128
N_REL = (CHUNK - 1) + MAX_REL + 1
CONV_K = 4
EPS = 1e-6
NEG_INF = -1e30
N_DEV = 8
LANES = 128
SUBLANES = 8
VMEM_LIMIT = 56 * 1024 * 1024

MATMUL_WHOLE_K = 2048

ATT_QB = 256
ATT_KW = ATT_QB + LEFT_CHUNKS * CHUNK
ATT_PAD = LEFT_CHUNKS * CHUNK
GDN_CB = 8
GDN_GROUP = 16
GDN_SCAN_UNROLL = 2

ADAM_LR = 0.001
ADAM_B1 = 0.9
ADAM_B2 = 0.999
ADAM_EPS = 1e-08
ADAM_WD = 0.01
ADAM_STEP = 10

IN_QZ = 3 * HW + HW
IN_AB0 = IN_QZ
IN_QKVB0 = IN_AB0 + 2 * HEADS
IN_GG0 = IN_QKVB0 + 3 * HW
IN_COLS = IN_GG0 + 2 * D_MODEL

BIG = ("ffn1_w_gu", "ffn1_w_down", "w_in", "w_branch_a", "w_branch_b", "w_out",
       "ffn2_w_gu", "ffn2_w_down", "ple_gate", "ple_proj")
COL_SHARDED = ("ffn1_w_gu", "w_in", "w_branch_a", "w_branch_b", "ffn2_w_gu", "ple_proj")


def _params(semantics=None, **kw):
    return pltpu.CompilerParams(dimension_semantics=semantics, vmem_limit_bytes=VMEM_LIMIT, **kw)


def _pick(n, cands):
    for c in cands:
        if n % c == 0:
            return c
    return n


def _matmul(a, b, mode, out_dtype, name):
    halves = (a.ndim == 3 and mode == "nt") or (b.ndim == 3 and mode == "tn")
    if mode == "nn":
        (m, k), (k2, n) = a.shape, b.shape
    elif mode == "nt":
        (m, k), (n, k2) = (a.shape[-2], a.shape[-1] * (a.ndim - 1)), b.shape
    else:
        (k, m), (k2, n) = a.shape, (b.shape[-2], b.shape[-1] * (b.ndim - 1))
    assert k == k2 and a.ndim + b.ndim == (5 if halves else 4), (a.shape, b.shape, mode)
    tm = _pick(m, (1024, 512, 256, 128))
    if halves and mode == "tn":
        tn = _pick(n // 2, (1408, 1024, 512, 256, 128))
        tk = _pick(k, (2048, 1024, 512, 256, 128))
    elif halves:
        tn = _pick(n, (1024, 512, 256, 128))
        tk = _pick(k // 2, (2816, 2048, 1536, 1024, 512, 256, 128))
    else:
        tn = _pick(n, (1024, 512, 256, 128))
        tk = k if k <= MATMUL_WHOLE_K else _pick(k, (2816, 2048, 1536, 1024, 512, 256, 128))
    nk = k // tk
    per_half = (n // 2) // tn if mode == "tn" else (k // 2) // tk
    if mode == "nn":
        a_spec = pl.BlockSpec((tm, tk), lambda i, j, kk: (i, kk))
        b_spec = pl.BlockSpec((tk, tn), lambda i, j, kk: (kk, j))
        dims = (((1,), (0,)), ((), ()))
    elif mode == "nt":
        a_spec = pl.BlockSpec((tm, tk), lambda i, j, kk: (i, kk))
        b_spec = pl.BlockSpec((tn, tk), lambda i, j, kk: (j, kk))
        dims = (((1,), (1,)), ((), ()))
        if halves:
            a_spec = pl.BlockSpec((None, tm, tk), lambda i, j, kk: (kk // per_half, i, kk % per_half))
    else:
        a_spec = pl.BlockSpec((tk, tm), lambda i, j, kk: (kk, i))
        b_spec = pl.BlockSpec((tk, tn), lambda i, j, kk: (kk, j))
        dims = (((0,), (0,)), ((), ()))
        if halves:
            b_spec = pl.BlockSpec((None, tk, tn), lambda i, j, kk: (j // per_half, kk, j % per_half))

    def body(a_ref, b_ref, o_ref, *acc):
        prod = lax.dot_general(a_ref[...].astype(BF16), b_ref[...].astype(BF16), dims, preferred_element_type=F32)
        if nk == 1:
            o_ref[...] = prod.astype(o_ref.dtype)
            return
        acc_ref, kk = acc[0], pl.program_id(2)

        @pl.when(kk == 0)
        def _():
            acc_ref[...] = prod

        @pl.when((kk > 0) & (kk < nk - 1))
        def _():
            acc_ref[...] += prod

        @pl.when(kk == nk - 1)
        def _():
            o_ref[...] = (acc_ref[...] + prod).astype(o_ref.dtype)

    return pl.pallas_call(
        body, name=name,
        out_shape=jax.ShapeDtypeStruct((m, n), out_dtype),
        grid=(m // tm, n // tn, nk),
        in_specs=[a_spec, b_spec],
        out_specs=pl.BlockSpec((tm, tn), lambda i, j, kk: (i, j)),
        scratch_shapes=[pltpu.VMEM((tm, tn), F32)] if nk > 1 else [],
        compiler_params=_params(("parallel", "parallel", "arbitrary")),
    )(a, b)


def _rows(fn, row_ins, consts, row_outs, acc_outs, tile, name):
    t_rows = row_ins[0][0].shape[0]
    tile = min(tile, t_rows)
    assert t_rows % tile == 0 and tile % SUBLANES == 0
    n = t_rows // tile
    per = tile // SUBLANES
    last8 = t_rows // SUBLANES - 1
    in_specs = []
    for arr, kind in row_ins:
        c = arr.shape[1]
        if kind == "t":
            in_specs.append(pl.BlockSpec((tile, c), lambda i: (i, 0)))
        elif kind == "p":
            in_specs.append(pl.BlockSpec((SUBLANES, c), lambda i: (jnp.maximum(i * per - 1, 0), 0)))
        else:
            in_specs.append(pl.BlockSpec((SUBLANES, c), lambda i: (jnp.minimum((i + 1) * per, last8), 0)))
    for arr in consts:
        in_specs.append(pl.BlockSpec(arr.shape, lambda i, nd=arr.ndim: (0,) * nd))
    out_shape = [jax.ShapeDtypeStruct((t_rows, c), dt) for c, dt in row_outs]
    out_specs = [pl.BlockSpec((tile, c), lambda i: (i, 0)) for c, _ in row_outs]
    for shp in acc_outs:
        out_shape.append(jax.ShapeDtypeStruct(shp, F32))
        out_specs.append(pl.BlockSpec(shp, lambda i, nd=len(shp): (0,) * nd))
    n_in = len(row_ins) + len(consts)
    n_row_out = len(row_outs)

    def body(*refs):
        i = pl.program_id(0)
        vals = [r[...].astype(F32) for r in refs[:len(row_ins)]]
        res = fn(i, n, *vals, *refs[len(row_ins):n_in])
        outs = refs[n_in:]
        for r, v in zip(outs[:n_row_out], res[:n_row_out]):
            r[...] = v.astype(r.dtype)
        if acc_outs:
            @pl.when(i == 0)
            def _():
                for r in outs[n_row_out:]:
                    r[...] = jnp.zeros_like(r)

            for r, v in zip(outs[n_row_out:], res[n_row_out:]):
                r[...] += v

    res = pl.pallas_call(
        body, name=name, out_shape=out_shape, grid=(n,), in_specs=in_specs, out_specs=out_specs,
        compiler_params=_params(("arbitrary",) if acc_outs else ("parallel",)),
    )(*[a for a, _ in row_ins], *consts)
    return res


def _rms(x, w):
    return x * lax.rsqrt(jnp.mean(x * x, axis=-1, keepdims=True) + EPS) * w


def _l2n(x):
    return x * lax.rsqrt(jnp.sum(x * x, axis=-1, keepdims=True) + EPS)


def _sigmoid(x):
    return 1.0 / (1.0 + jnp.exp(-x))


def _silu(x):
    return x * _sigmoid(x)


def _softplus(x):
    return jnp.maximum(x, 0.0) + jnp.log(1.0 + jnp.exp(-jnp.abs(x)))


def _heads(fn, *xs):
    nh = xs[0].shape[1] // HEAD_DIM
    return jnp.concatenate(
        [fn(*[x[:, h * HEAD_DIM:(h + 1) * HEAD_DIM] for x in xs]) for h in range(nh)], axis=1)


def _colsum(x):
    return jnp.sum(x, axis=0, keepdims=True)


def _gated_norm(o, z, w):
    return _heads(lambda oh, zh: _rms(oh, w) * _silu(zh), o, z)


def _mix(gg, ta, tb):
    return _sigmoid(gg[:, :D_MODEL]) * ta + _sigmoid(gg[:, D_MODEL:]) * tb


def _gdn_post(y):
    a = _silu(y)
    q = _heads(lambda v: _l2n(v) * (HEAD_DIM ** -0.5), a[:, :HW])
    k = _heads(_l2n, a[:, HW:2 * HW])
    return q, k, a[:, 2 * HW:]


NN = (((1,), (0,)), ((), ()))
NT = (((1,), (1,)), ((), ()))
TN = (((0,), (0,)), ((), ()))


def _dg(a, b, dims):
    return lax.dot_general(a, b, dims, preferred_element_type=F32)


def _split2(x):
    hi = x.astype(BF16)
    return hi, (x - hi.astype(F32)).astype(BF16)


def _split3(x):
    hi = x.astype(BF16)
    r = x - hi.astype(F32)
    mid = r.astype(BF16)
    return hi, mid, (r - mid.astype(F32)).astype(BF16)


def _dg3(a, b, dims):
    ah, al = _split2(a)
    bh, bl = _split2(b)
    return _dg(ah, bh, dims) + (_dg(ah, bl, dims) + _dg(al, bh, dims))


BNN = (((2,), (1,)), ((0,), (0,)))
BNT = (((2,), (2,)), ((0,), (0,)))
BTN = (((1,), (1,)), ((0,), (0,)))


@jax.custom_vjp
def _mm3(a, b):
    return _dg3(a, b, BNN)


_mm3.defvjp(lambda a, b: (_dg3(a, b, BNN), (a, b)),
            lambda res, g: (_dg3(g, res[1], BNT), _dg3(res[0], g, BTN)))


def _xm(x, m, dims):
    mb = m.astype(BF16)
    parts = _split3(x)
    return _dg(parts[0], mb, dims) + (_dg(parts[1], mb, dims) + _dg(parts[2], mb, dims))


def _mx(m, x, dims):
    mb = m.astype(BF16)
    parts = _split3(x)
    return _dg(mb, parts[0], dims) + (_dg(mb, parts[1], dims) + _dg(mb, parts[2], dims))


@jax.custom_vjp
def _times_const(x, m):
    return _xm(x, m, NN)


_times_const.defvjp(lambda x, m: (_xm(x, m, NN), m),
                    lambda m, g: (_xm(g, m, NT), jnp.zeros_like(m)))


@jax.custom_vjp
def _const_times(m, x):
    return _mx(m, x, NN)


_const_times.defvjp(lambda m, x: (_mx(m, x, NN), m),
                    lambda m, g: (jnp.zeros_like(m), _mx(m, g, TN)))


@jax.custom_vjp
def _lane_mean_cols(x, avg):
    return _mx(avg, x, BNT)


_lane_mean_cols.defvjp(lambda x, avg: (_mx(avg, x, BNT), avg),
                       lambda avg, g: (_xm(g, avg, BTN), jnp.zeros_like(avg)))


def _gdn_gates(ab, alog, dtb, e_g, e_b):
    t = ab.shape[0]
    g = -jnp.exp(alog) * _softplus(ab + dtb)
    beta = _sigmoid(ab)
    ri = lax.broadcasted_iota(jnp.int32, (t, t), 0)
    ci = lax.broadcasted_iota(jnp.int32, (t, t), 1)
    shift = CHUNK.bit_length() - 1
    same = jnp.right_shift(ri, shift) == jnp.right_shift(ci, shift)
    tril = jnp.where(same & (ri >= ci), 1.0, 0.0).astype(F32)
    gc = _const_times(tril, g)
    return _times_const(gc, e_g), _times_const(beta, e_b)


def _shift_down(x, halo, s, i):
    if s == 0:
        return x
    halo = jnp.where(i == 0, 0.0, halo)
    xr = pltpu.roll(x, s, 0)
    hr = pltpu.roll(halo, s, 0)
    row = lax.broadcasted_iota(jnp.int32, (SUBLANES, x.shape[1]), 0)
    top = jnp.where(row < s, hr, xr[:SUBLANES])
    return jnp.concatenate([top, xr[SUBLANES:]], axis=0)


def _shift_up(x, halo, s, i, n):
    if s == 0:
        return x
    t = x.shape[0]
    halo = jnp.where(i == n - 1, 0.0, halo)
    xr = pltpu.roll(x, t - s, 0)
    hr = pltpu.roll(halo, SUBLANES - s, 0)
    row = lax.broadcasted_iota(jnp.int32, (SUBLANES, x.shape[1]), 0)
    bot = jnp.where(row >= SUBLANES - s, hr, xr[t - SUBLANES:])
    return jnp.concatenate([xr[:t - SUBLANES], bot], axis=0)


def _conv(pa, prev, cw_ref, i):
    y = pa * cw_ref[CONV_K - 1:CONV_K, :]
    for j in range(CONV_K - 1):
        y = y + _shift_down(pa, prev, CONV_K - 1 - j, i) * cw_ref[j:j + 1, :]
    return y


def _dot_nt(a, b, precision=None):
    return lax.dot_general(a, b, (((1,), (1,)), ((), ())), precision=precision, preferred_element_type=F32)


def _dot_tn(a, b, precision=None):
    return lax.dot_general(a, b, (((0,), (0,)), ((), ())), precision=precision, preferred_element_type=F32)


def _dot(a, b, precision=None):
    return jnp.dot(a, b, precision=precision, preferred_element_type=F32)


def _bf(x):
    return x.astype(BF16)


def _neumann_inverse(lmat):
    nb, c, _ = lmat.shape
    ri = lax.broadcasted_iota(jnp.int32, (nb, c, c), 1)
    ci = lax.broadcasted_iota(jnp.int32, (nb, c, c), 2)
    pw = -lmat
    inv = jnp.where(ri == ci, 1.0, 0.0).astype(F32) + pw
    for _ in range(5):
        pw = _mm3(pw, pw)
        inv = inv + _mm3(inv, pw)
    return inv


@jax.custom_vjp
def _unit_lower_inverse(lmat):
    return _neumann_inverse(lmat)


def _unit_lower_inverse_fwd(lmat):
    inv = _neumann_inverse(lmat)
    return inv, inv


def _unit_lower_inverse_bwd(inv, g):
    return (-_dg3(_dg3(inv, g, BTN), inv, BNT),)


_unit_lower_inverse.defvjp(_unit_lower_inverse_fwd, _unit_lower_inverse_bwd)


def _gdn_chunk(q, k, v, gc, bb):
    nb, c, _ = q.shape
    ri = lax.broadcasted_iota(jnp.int32, (nb, c, c), 1)
    ci = lax.broadcasted_iota(jnp.int32, (nb, c, c), 2)
    incl = ri >= ci
    strict = ri > ci
    g_row = gc[:, :, :c]
    g_col = _lane_mean_cols(gc, jnp.full((nb, c, LANES), 1.0 / LANES, F32))
    decay = jnp.where(incl, jnp.exp(jnp.where(incl, g_row - g_col, 0.0)), 0.0)
    kb = k * bb
    lmat = jnp.where(strict, _dg(_bf(kb), _bf(k), BNT) * decay, 0.0)
    inv = _unit_lower_inverse(lmat)
    egc = jnp.exp(gc)
    u = _mm3(inv, v * bb)
    w = _mm3(inv, kb * egc)
    aqk = _dg(_bf(q), _bf(k), BNT) * decay
    last = lax.broadcasted_iota(jnp.int32, (nb, c, LANES), 1) == c - 1
    tot = jnp.sum(jnp.where(last, gc, 0.0), axis=1, keepdims=True)
    k_tail = k * jnp.exp(tot - gc)
    tail = jnp.broadcast_to(jnp.exp(tot), (nb, SUBLANES, LANES))
    return u, w, aqk, q * egc, k_tail, tail


def _gdn_intra(qn, kn, vv, g_b, beta_b):
    t_rows = qn.shape[0]
    nc = t_rows // CHUNK
    cb = min(GDN_GROUP, nc)
    rows = cb * CHUNK
    col = pl.BlockSpec((rows, HEAD_DIM), lambda h, b: (b, h))

    def body(q_ref, k_ref, v_ref, g_ref, b_ref, u_ref, w_ref, a_ref, qd_ref, kt_ref, tl_ref):
        def group(gi, carry):
            r = pl.ds(pl.multiple_of(gi * (grp * CHUNK), grp * CHUNK), grp * CHUNK)
            ins = [ref[r, :].reshape(grp, CHUNK, HEAD_DIM) for ref in (q_ref, k_ref, v_ref, g_ref, b_ref)]
            u, w, aqk, qd, kt, tl = _gdn_chunk(*ins)
            for ref, val in ((u_ref, u), (w_ref, w), (qd_ref, qd), (kt_ref, kt)):
                ref[r, :] = val.reshape(grp * CHUNK, HEAD_DIM)
            a_ref[0, r, :] = aqk.reshape(grp * CHUNK, CHUNK)
            tl_ref[0, pl.ds(gi * grp, grp)] = tl
            return carry

        grp = min(GDN_GROUP, cb)
        lax.fori_loop(0, cb // grp, group, 0)

    full = jax.ShapeDtypeStruct((t_rows, HW), F32)
    return pl.pallas_call(
        body, name="gdn_intra_fwd",
        out_shape=[full, full, jax.ShapeDtypeStruct((HEADS, t_rows, CHUNK), F32), full, full,
                   jax.ShapeDtypeStruct((HEADS, nc, SUBLANES, LANES), F32)],
        grid=(HEADS, nc // cb),
        in_specs=[col] * 5,
        out_specs=[col, col, pl.BlockSpec((1, rows, CHUNK), lambda h, b: (h, b, 0)), col, col,
                   pl.BlockSpec((1, cb, SUBLANES, LANES), lambda h, b: (h, b, 0, 0))],
        compiler_params=_params(("parallel", "parallel")),
    )(qn, kn, vv, g_b, beta_b)


def _gdn_intra_bwd(qn, kn, vv, g_b, beta_b, du, dw, da, dqd, dkt, dtl):
    t_rows = qn.shape[0]
    nc = t_rows // CHUNK
    cb = min(GDN_GROUP, nc)
    rows = cb * CHUNK
    col = pl.BlockSpec((rows, HEAD_DIM), lambda h, b: (b, h))
    a_spec = pl.BlockSpec((1, rows, CHUNK), lambda h, b: (h, b, 0))
    tl_spec = pl.BlockSpec((1, cb, SUBLANES, LANES), lambda h, b: (h, b, 0, 0))

    def body(q_ref, k_ref, v_ref, g_ref, b_ref, du_ref, dw_ref, da_ref, dqd_ref, dkt_ref, dtl_ref,
             dq_ref, dk_ref, dv_ref, dg_ref, db_ref):
        def group(gi, carry):
            r = pl.ds(pl.multiple_of(gi * (grp * CHUNK), grp * CHUNK), grp * CHUNK)
            wide = (grp, CHUNK, HEAD_DIM)
            ins = [ref[r, :].reshape(wide) for ref in (q_ref, k_ref, v_ref, g_ref, b_ref)]
            cts = (du_ref[r, :].reshape(wide), dw_ref[r, :].reshape(wide),
                   da_ref[0, r, :].reshape(grp, CHUNK, CHUNK), dqd_ref[r, :].reshape(wide),
                   dkt_ref[r, :].reshape(wide), dtl_ref[0, pl.ds(gi * grp, grp)])
            grads = jax.vjp(_gdn_chunk, *ins)[1](cts)
            for ref, val in zip((dq_ref, dk_ref, dv_ref, dg_ref, db_ref), grads):
                ref[r, :] = val.reshape(grp * CHUNK, HEAD_DIM)
            return carry

        grp = min(GDN_GROUP, cb)
        lax.fori_loop(0, cb // grp, group, 0)

    full = jax.ShapeDtypeStruct((t_rows, HW), F32)
    return pl.pallas_call(
        body, name="gdn_intra_bwd",
        out_shape=[full] * 5,
        grid=(HEADS, nc // cb),
        in_specs=[col] * 7 + [a_spec, col, col, tl_spec],
        out_specs=[col] * 5,
        compiler_params=_params(("parallel", "parallel")),
    )(qn, kn, vv, g_b, beta_b, du, dw, da, dqd, dkt, dtl)


def _head_cols(h):
    return slice(h * HEAD_DIM, (h + 1) * HEAD_DIM)


def _gdn_scan(u, w, aqk, qd, kt, tl):
    t_rows = u.shape[0]
    nc = t_rows // CHUNK
    cb = min(GDN_CB, nc)
    rows = cb * CHUNK
    wide = pl.BlockSpec((rows, HW), lambda b: (b, 0))

    def body(u_ref, w_ref, a_ref, qd_ref, kt_ref, tl_ref, o_ref, s_out_ref, s_ref):
        @pl.when(pl.program_id(0) == 0)
        def _():
            s_ref[...] = jnp.zeros_like(s_ref)

        def chunk(ci, carry):
            r = pl.ds(pl.multiple_of(ci * CHUNK, CHUNK), CHUNK)
            for h in range(HEADS):
                hc = _head_cols(h)
                s = s_ref[h]
                s_out_ref[ci, h] = s
                sb = _bf(s)
                vn = u_ref[r, hc] - _dot(_bf(w_ref[r, hc]), sb)
                vnb = _bf(vn)
                o_ref[r, hc] = _dot(_bf(qd_ref[r, hc]), sb) + _dot(_bf(a_ref[h, r, :]), vnb)
                s_ref[h] = s * tl_ref[h, ci, 0:1, :] + _dot_tn(_bf(kt_ref[r, hc]), vnb)
            return carry

        lax.fori_loop(0, cb, chunk, 0, unroll=GDN_SCAN_UNROLL)

    return pl.pallas_call(
        body, name="gdn_scan_fwd",
        out_shape=[jax.ShapeDtypeStruct((t_rows, HW), F32),
                   jax.ShapeDtypeStruct((nc, HEADS, HEAD_DIM, HEAD_DIM), F32)],
        grid=(nc // cb,),
        in_specs=[wide, wide, pl.BlockSpec((HEADS, rows, CHUNK), lambda b: (0, b, 0)), wide, wide,
                  pl.BlockSpec((HEADS, cb, SUBLANES, LANES), lambda b: (0, b, 0, 0))],
        out_specs=[wide, pl.BlockSpec((cb, HEADS, HEAD_DIM, HEAD_DIM), lambda b: (b, 0, 0, 0))],
        scratch_shapes=[pltpu.VMEM((HEADS, HEAD_DIM, HEAD_DIM), F32)],
        compiler_params=_params(("arbitrary",)),
    )(u, w, aqk, qd, kt, tl)


def _gdn_scan_bwd(do, u, w, aqk, qd, kt, tl, states):
    t_rows = u.shape[0]
    nc = t_rows // CHUNK
    cb = min(GDN_CB, nc)
    rows = cb * CHUNK
    nb = nc // cb
    wide = pl.BlockSpec((rows, HW), lambda b: (nb - 1 - b, 0))
    a_spec = pl.BlockSpec((HEADS, rows, CHUNK), lambda b: (0, nb - 1 - b, 0))
    tl_spec = pl.BlockSpec((HEADS, cb, SUBLANES, LANES), lambda b: (0, nb - 1 - b, 0, 0))

    def body(do_ref, u_ref, w_ref, a_ref, qd_ref, kt_ref, tl_ref, s_in_ref,
             du_ref, dw_ref, da_ref, dqd_ref, dkt_ref, dtl_ref, ds_ref):
        @pl.when(pl.program_id(0) == 0)
        def _():
            ds_ref[...] = jnp.zeros_like(ds_ref)

        row0 = lax.broadcasted_iota(jnp.int32, (SUBLANES, LANES), 0) == 0

        def chunk(step, carry):
            ci = cb - 1 - step
            r = pl.ds(pl.multiple_of(ci * CHUNK, CHUNK), CHUNK)
            for h in range(HEADS):
                hc = _head_cols(h)
                s = s_in_ref[ci, h]
                ds_next = ds_ref[h]
                sb, dsb = _bf(s), _bf(ds_next)
                wb, ab, ktb, qdb = _bf(w_ref[r, hc]), _bf(a_ref[h, r, :]), _bf(kt_ref[r, hc]), _bf(qd_ref[r, hc])
                dob = _bf(do_ref[r, hc])
                vn = u_ref[r, hc] - _dot(wb, sb)
                vnb = _bf(vn)
                dvn = _dot_tn(ab, dob) + _dot(ktb, dsb)
                dvnb = _bf(dvn)
                du_ref[r, hc] = dvn
                dw_ref[r, hc] = -_dot_nt(dvnb, sb)
                da_ref[h, r, :] = _dot_nt(dob, vnb)
                dqd_ref[r, hc] = _dot_nt(dob, sb)
                dkt_ref[r, hc] = _dot_nt(vnb, dsb)
                dtl_ref[h, ci] = jnp.where(row0, _colsum(s * ds_next), 0.0)
                ds_ref[h] = _dot_tn(qdb, dob) + ds_next * tl_ref[h, ci, 0:1, :] - _dot_tn(wb, dvnb)
            return carry

        lax.fori_loop(0, cb, chunk, 0, unroll=GDN_SCAN_UNROLL)

    full = jax.ShapeDtypeStruct((t_rows, HW), F32)
    return pl.pallas_call(
        body, name="gdn_scan_bwd",
        out_shape=[full, full, jax.ShapeDtypeStruct((HEADS, t_rows, CHUNK), F32), full, full,
                   jax.ShapeDtypeStruct((HEADS, nc, SUBLANES, LANES), F32)],
        grid=(nb,),
        in_specs=[wide, wide, wide, a_spec, wide, wide, tl_spec,
                  pl.BlockSpec((cb, HEADS, HEAD_DIM, HEAD_DIM), lambda b: (nb - 1 - b, 0, 0, 0))],
        out_specs=[wide, wide, a_spec, wide, wide, tl_spec],
        scratch_shapes=[pltpu.VMEM((HEADS, HEAD_DIM, HEAD_DIM), F32)],
        compiler_params=_params(("arbitrary",)),
    )(do, u, w, aqk, qd, kt, tl, states)


def _att_profile_index():
    j = lax.broadcasted_iota(jnp.int32, (SUBLANES, ATT_KW), 1)
    return jnp.clip(ATT_PAD - j, -(CHUNK - 1), MAX_REL) + (CHUNK - 1)


def _att_far_back():
    qi = lax.broadcasted_iota(jnp.int32, (ATT_QB, ATT_KW), 0)
    kj = lax.broadcasted_iota(jnp.int32, (ATT_QB, ATT_KW), 1)
    return kj < qi


def _rotate_rows(x, forward):
    rows, lanes = x.shape
    row = lax.broadcasted_iota(jnp.int32, x.shape, 0)
    for bit in range(rows.bit_length() - 1):
        amount = (1 << bit) if forward else lanes - (1 << bit)
        x = jnp.where(jnp.bitwise_and(jnp.right_shift(row, bit), 1) == 1, pltpu.roll(x, amount, 1), x)
    return x


def _att_in_band():
    qi = lax.broadcasted_iota(jnp.int32, (ATT_QB, ATT_KW), 0)
    kj = lax.broadcasted_iota(jnp.int32, (ATT_QB, ATT_KW), 1)
    shift = CHUNK.bit_length() - 1
    qc = jnp.right_shift(qi, shift)
    kc = jnp.right_shift(kj, shift) - LEFT_CHUNKS
    return (kc <= qc) & (kc >= qc - LEFT_CHUNKS)


def _att_valid(b):
    kj = lax.broadcasted_iota(jnp.int32, (1, ATT_KW), 1)
    return jnp.where(kj + b * ATT_QB >= ATT_PAD, 0.0, NEG_INF)


def _rms_parts(x, w):
    r = lax.rsqrt(jnp.mean(x * x, axis=-1, keepdims=True) + EPS)
    xn = x * r
    return xn * w, xn, r


def _rms_bwd(dy, xn, r, w):
    dxn = dy * w
    dx = r * (dxn - xn * jnp.mean(dxn * xn, axis=-1, keepdims=True))
    return dx, _colsum(dy * xn)


def _att_probs(qb, kb, bias, before_start):
    s = _dot_nt(qb, kb) * (HEAD_DIM ** -0.5) + bias + before_start
    e = jnp.exp(s - jnp.max(s, axis=-1, keepdims=True))
    return e * (1.0 / jnp.sum(e, axis=-1, keepdims=True))


def _att_specs():
    q_spec = pl.BlockSpec((ATT_QB, HEAD_DIM), lambda h, b: (b, h))
    back = ATT_PAD // ATT_QB
    k_specs = [pl.BlockSpec((ATT_QB, HEAD_DIM), lambda h, b, j=j: (jnp.maximum(b + j - back, 0), HEADS + h))
               for j in range(3)]
    v_specs = [pl.BlockSpec((ATT_QB, HEAD_DIM), lambda h, b, j=j: (jnp.maximum(b + j - back, 0), 2 * HEADS + h))
               for j in range(3)]
    w_spec = pl.BlockSpec((1, HEAD_DIM), lambda h, b: (0, 0))
    smem = pl.BlockSpec(memory_space=pltpu.SMEM)
    return q_spec, k_specs, v_specs, w_spec, smem


BIAS_SPEC = pl.BlockSpec((1, ATT_QB, ATT_KW), lambda h, b: (h, 0, 0))


def _expand_rel_bias(rel):
    def body(rel_ref, bias_ref):
        h = pl.program_id(0)
        idx = _att_profile_index()

        def fill(r, acc):
            return jnp.where(idx == r, rel_ref[h, r], acc)

        profile = lax.fori_loop(0, N_REL, fill, jnp.zeros((SUBLANES, ATT_KW), F32))
        table = _rotate_rows(jnp.concatenate([profile] * (ATT_QB // SUBLANES), axis=0), True)
        table = jnp.where(_att_far_back(), rel_ref[h, N_REL - 1], table)
        bias_ref[0] = jnp.where(_att_in_band(), table, NEG_INF)

    return pl.pallas_call(
        body, name="rel_bias_expand",
        out_shape=jax.ShapeDtypeStruct((HEADS, ATT_QB, ATT_KW), F32), grid=(HEADS,),
        in_specs=[pl.BlockSpec(memory_space=pltpu.SMEM)],
        out_specs=pl.BlockSpec((1, ATT_QB, ATT_KW), lambda h: (h, 0, 0)),
        compiler_params=_params(("parallel",)),
    )(rel)


def _attention(pb, qw, kw, bias):
    t_rows = pb.shape[0]
    q_spec, k_specs, v_specs, w_spec, _ = _att_specs()

    def body(q_ref, k0, k1, k2, v0, v1, v2, qw_ref, kw_ref, bias_ref, o_ref):
        b = pl.program_id(1)
        kwin = jnp.concatenate([k0[...], k1[...], k2[...]], axis=0)
        vwin = jnp.concatenate([v0[...], v1[...], v2[...]], axis=0)
        q = _rms(q_ref[...], qw_ref[...])
        k = _rms(kwin, kw_ref[...])
        p = _att_probs(_bf(q), _bf(k), bias_ref[0], _att_valid(b))
        o_ref[...] = _dot(_bf(p), _bf(vwin)).astype(o_ref.dtype)

    return pl.pallas_call(
        body, name="band_attention_fwd",
        out_shape=jax.ShapeDtypeStruct((t_rows, HW), BF16),
        grid=(HEADS, t_rows // ATT_QB),
        in_specs=[q_spec] + k_specs + v_specs + [w_spec, w_spec, BIAS_SPEC],
        out_specs=pl.BlockSpec((ATT_QB, HEAD_DIM), lambda h, b: (b, h)),
        compiler_params=_params(("parallel", "arbitrary")),
    )(pb, pb, pb, pb, pb, pb, pb, qw, kw, bias)


def _attention_bwd(pb, qw, kw, bias, dyb):
    t_rows = pb.shape[0]
    nb = t_rows // ATT_QB
    q_spec, k_specs, v_specs, w_spec, smem = _att_specs()
    pad_rows = t_rows + ATT_PAD
    acc_spec = pl.BlockSpec((pad_rows, HEAD_DIM), lambda h, b: (0, h))

    def body(q_ref, k0, k1, k2, v0, v1, v2, qw_ref, kw_ref, bias_ref, do_ref,
             dq_ref, dk_ref, dv_ref, dqw_ref, dkw_ref, drel_ref, dbias_ref):
        h, b = pl.program_id(0), pl.program_id(1)

        @pl.when(b == 0)
        def _():
            dbias_ref[...] = jnp.zeros_like(dbias_ref)
            dk_ref[...] = jnp.zeros_like(dk_ref)
            dv_ref[...] = jnp.zeros_like(dv_ref)

        @pl.when((b == 0) & (h == 0))
        def _():
            dqw_ref[...] = jnp.zeros_like(dqw_ref)
            dkw_ref[...] = jnp.zeros_like(dkw_ref)

        kwin = jnp.concatenate([k0[...], k1[...], k2[...]], axis=0)
        vwin = jnp.concatenate([v0[...], v1[...], v2[...]], axis=0)
        scale = HEAD_DIM ** -0.5
        qw_, kw_ = qw_ref[...], kw_ref[...]
        q, qn, rq = _rms_parts(q_ref[...], qw_)
        k, kn, rk = _rms_parts(kwin, kw_)
        qb, kb, dob = _bf(q), _bf(k), _bf(do_ref[...])
        p = _att_probs(qb, kb, bias_ref[0], _att_valid(b))
        dp = _dot_nt(dob, _bf(vwin))
        ds = p * (dp - jnp.sum(p * dp, axis=-1, keepdims=True))
        dbias_ref[...] += ds
        ds = _bf(ds)
        dq, dqw = _rms_bwd(_dot(ds, kb) * scale, qn, rq, qw_)
        dk, dkw = _rms_bwd(_dot_tn(ds, qb) * scale, kn, rk, kw_)
        dq_ref[...] = dq.astype(dq_ref.dtype)
        win = pl.ds(pl.multiple_of(b * ATT_QB, ATT_QB), ATT_KW)
        dk_ref[win, :] += dk
        dv_ref[win, :] += _dot_tn(_bf(p), dob)
        dqw_ref[...] += dqw
        dkw_ref[...] += dkw

        @pl.when(b == nb - 1)
        def _():
            tot, far = dbias_ref[...], _att_far_back()
            far_sum = jnp.sum(jnp.where(far, tot, 0.0))
            per_offset = _colsum(_rotate_rows(jnp.where(far, 0.0, tot), False))
            idx = _att_profile_index()
            first_row = lax.broadcasted_iota(jnp.int32, idx.shape, 0) == 0
            spread = jnp.where(first_row, per_offset, 0.0)

            def reduce(r, carry):
                drel_ref[h, r] = jnp.sum(jnp.where(idx == r, spread, 0.0)) + jnp.where(r == N_REL - 1, far_sum, 0.0)
                return carry

            lax.fori_loop(0, N_REL, reduce, 0)

    return pl.pallas_call(
        body, name="band_attention_bwd",
        out_shape=[jax.ShapeDtypeStruct((t_rows, HW), BF16),
                   jax.ShapeDtypeStruct((pad_rows, HW), F32), jax.ShapeDtypeStruct((pad_rows, HW), F32),
                   jax.ShapeDtypeStruct((1, HEAD_DIM), F32), jax.ShapeDtypeStruct((1, HEAD_DIM), F32),
                   jax.ShapeDtypeStruct((HEADS, N_REL), F32)],
        grid=(HEADS, nb),
        in_specs=[q_spec] + k_specs + v_specs + [w_spec, w_spec, BIAS_SPEC, q_spec],
        out_specs=[q_spec, acc_spec, acc_spec, w_spec, w_spec, smem],
        scratch_shapes=[pltpu.VMEM((ATT_QB, ATT_KW), F32)],
        compiler_params=_params(("arbitrary", "arbitrary")),
    )(pb, pb, pb, pb, pb, pb, pb, qw, kw, bias, dyb)


def _me():
    return lax.axis_index("x"), lax.axis_index("y"), lax.axis_index("c")


def _index(x, y, c):
    return 4 * x + 2 * y + c


HBM_SPEC = pl.BlockSpec(memory_space=pl.ANY)


def _block(ref, kind, d, r, c):
    if kind == "all":
        return ref
    if kind == "rows":
        return ref.at[pl.ds(d * r, r), :]
    if kind == "win":
        return ref.at[:, pl.ds(d * WIN_STEP, c)]
    return ref.at[:, pl.ds(d * c, c)]


def _all_gather(shards, kinds, n_gather):
    n = len(shards)

    def body(*refs):
        x_refs, out_refs = refs[:n], refs[n:2 * n]
        send_sems, recv_sems, local_sems = refs[2 * n:]
        x, y, c = _me()
        me, sibling = (x, y, c), (x, y, 1 - c)
        chips = [(1 - x, y), (x, 1 - y), (1 - x, 1 - y)]

        def copy(i, k, blk, to, src=None):
            r_, c_ = shards[i].shape
            dst = _block(out_refs[i], kinds[i], _index(*blk), r_, c_)
            return pltpu.make_async_remote_copy(
                src_ref=dst if src is None else src, dst_ref=dst,
                send_sem=send_sems.at[i, k], recv_sem=recv_sems.at[i, k], device_id=to, device_id_type=MESH)

        sends, local = [], []
        for i in range(n):
            r_, c_ = shards[i].shape
            mine = pltpu.make_async_copy(x_refs[i], _block(out_refs[i], kinds[i], _index(*me), r_, c_),
                                         local_sems.at[i])
            mine.start()
            local.append(mine)
            if i >= n_gather:
                continue
            first = [copy(i, 0, me, sibling, src=x_refs[i])]
            first += [copy(i, 1 + j, me, (*chip, c), src=x_refs[i]) for j, chip in enumerate(chips)]
            for cp in first:
                cp.start()
            sends += first
        for i in range(n_gather):
            for j, chip in enumerate(chips):
                copy(i, 1 + j, (*chip, c), me).wait_recv()
                passed = copy(i, 4 + j, (*chip, c), sibling)
                passed.start()
                sends.append(passed)
        for i in range(n_gather):
            copy(i, 0, sibling, me).wait_recv()
            for j, chip in enumerate(chips):
                copy(i, 4 + j, (*chip, 1 - c), me).wait_recv()
        for cp in sends:
            cp.wait_send()
        for cp in local:
            cp.wait()

    def full_shape(s, kind):
        r_, c_ = s.shape
        return (N_DEV * r_, c_) if kind == "rows" else (r_, N_DEV * c_)

    return pl.pallas_call(
        body, name="weights_all_gather",
        out_shape=[jax.ShapeDtypeStruct(full_shape(s, k), s.dtype) for s, k in zip(shards, kinds)],
        in_specs=[HBM_SPEC] * n, out_specs=[HBM_SPEC] * n,
        scratch_shapes=[pltpu.SemaphoreType.DMA((n_gather, 7)), pltpu.SemaphoreType.DMA((n_gather, 7)),
                        pltpu.SemaphoreType.DMA((n,))],
        compiler_params=pltpu.CompilerParams(has_side_effects=True),
    )(*shards)


SEM_SPEC = pl.BlockSpec(memory_space=pltpu.SEMAPHORE)
HBM_ONLY = pl.BlockSpec(memory_space=pltpu.HBM)
DATAFLOW = pltpu.SideEffectType.DATAFLOW_SIDE_EFFECTING


def _peers():
    x, y, c = _me()
    return [(x ^ (k >> 2), y ^ ((k >> 1) & 1), c ^ (k & 1)) for k in range(1, N_DEV)]


def _gather_copies(shapes, kinds):
    def make(src_refs, land_refs, send_sems, recv_sems):
        mine = _index(*_me())
        return [pltpu.make_async_remote_copy(
            src_ref=src_refs[i], dst_ref=_block(land_refs[i], kind, mine, r, c),
            send_sem=send_sems.at[7 * i + k], recv_sem=recv_sems.at[7 * i + k], device_id=peer, device_id_type=MESH)
            for i, ((r, c), kind) in enumerate(zip(shapes, kinds)) for k, peer in enumerate(_peers())]

    return make


def _exchange_copies(shapes, kinds):
    def make(src_refs, land_refs, send_sems, recv_sems):
        mine = _index(*_me())
        return [pltpu.make_async_remote_copy(
            src_ref=_block(src_refs[i], kind, _index(*peer), r, c), dst_ref=land_refs[i].at[mine],
            send_sem=send_sems.at[7 * i + k], recv_sem=recv_sems.at[7 * i + k], device_id=peer, device_id_type=MESH)
            for i, ((r, c), kind) in enumerate(zip(shapes, kinds)) for k, peer in enumerate(_peers())]

    return make


def _place_block(shard, kind, name):
    r, c = shard.shape
    tile = _row_tile(r, c)
    nt = r // tile
    full = (N_DEV * r, c) if kind == "rows" else (r, N_DEV * c)

    def body(me_ref, x_ref, out_ref):
        out_ref[...] = x_ref[...]

    if kind == "rows":
        out_spec = pl.BlockSpec((tile, c), lambda i, me: (me[0] * nt + i, 0))
    else:
        out_spec = pl.BlockSpec((tile, c), lambda i, me: (i, me[0]))
    return pl.pallas_call(
        body, name=name, out_shape=jax.ShapeDtypeStruct(full, shard.dtype),
        grid_spec=pltpu.PrefetchScalarGridSpec(
            num_scalar_prefetch=1, grid=(nt,),
            in_specs=[pl.BlockSpec((tile, c), lambda i, me: (i, 0))], out_specs=out_spec),
        compiler_params=_params(("arbitrary",)),
    )(_my_index_operand(), shard)


def _split_start(srcs, lands, make, name):
    n = len(srcs)

    def body(*refs):
        send_sems, recv_sems = refs[2 * n], refs[2 * n + 1]
        for cp in make(refs[:n], refs[n:2 * n], send_sems, recv_sems):
            cp.start()
        refs[-1][...] = jnp.zeros_like(refs[-1])

    arrays = list(srcs) + list(lands)
    out = pl.pallas_call(
        body, name=name,
        out_shape=(pltpu.SemaphoreType.DMA((7 * n,)), pltpu.SemaphoreType.DMA((7 * n,)),
                   *[pltpu.HBM(a.shape, a.dtype) for a in arrays], jax.ShapeDtypeStruct((SUBLANES, LANES), F32)),
        in_specs=[HBM_ONLY] * (2 * n),
        out_specs=(SEM_SPEC, SEM_SPEC, *[HBM_ONLY] * (2 * n), pl.BlockSpec(memory_space=pltpu.VMEM)),
        input_output_aliases={i: 2 + i for i in range(2 * n)},
        compiler_params=pltpu.CompilerParams(has_side_effects=DATAFLOW),
    )(*[pltpu.with_memory_space_constraint(a, pltpu.HBM) for a in arrays])
    return out[0], out[1], list(out[2:2 + n]), list(out[2 + n:2 + 2 * n]), out[-1]


def _split_wait(send_sems, recv_sems, srcs, lands, after, make, name):
    n = len(srcs)

    def body(*refs):
        for cp in make(refs[:n], refs[n:2 * n], refs[2 * n], refs[2 * n + 1]):
            cp.wait_send()
            cp.wait_recv()

    arrays = list(srcs) + list(lands)
    out = pl.pallas_call(
        body, name=name,
        out_shape=tuple(pltpu.HBM(a.shape, a.dtype) for a in arrays),
        in_specs=[HBM_ONLY] * (2 * n) + [SEM_SPEC, SEM_SPEC, pl.BlockSpec(memory_space=pl.ANY)],
        out_specs=tuple([HBM_ONLY] * (2 * n)),
        input_output_aliases={i: i for i in range(2 * n)},
        compiler_params=pltpu.CompilerParams(has_side_effects=DATAFLOW),
    )(*arrays, send_sems, recv_sems, after)
    return list(out[:n]), list(out[n:])


def _all_reduce_small(vals, name):
    rows, width = vals.shape

    def body(x_ref, out_ref, buf_ref, send_sems, recv_sems):
        x, y, c = _me()
        mine = _index(x, y, c)
        buf_ref[mine] = x_ref[...]
        copies = []
        for k in range(1, N_DEV):
            px, py, pc = x ^ (k >> 2), y ^ ((k >> 1) & 1), c ^ (k & 1)
            copies.append(pltpu.make_async_remote_copy(
                src_ref=x_ref, dst_ref=buf_ref.at[mine],
                send_sem=send_sems.at[k - 1], recv_sem=recv_sems.at[k - 1],
                device_id=(px, py, pc), device_id_type=MESH))
        for cp in copies:
            cp.start()
        for cp in copies:
            cp.wait()
        acc = buf_ref[0]
        for j in range(1, N_DEV):
            acc = acc + buf_ref[j]
        out_ref[...] = acc

    vmem = pl.BlockSpec(memory_space=pltpu.VMEM)
    return pl.pallas_call(
        body, name=name,
        out_shape=jax.ShapeDtypeStruct(vals.shape, F32),
        in_specs=[vmem], out_specs=vmem,
        scratch_shapes=[pltpu.VMEM((N_DEV, rows, width), F32),
                        pltpu.SemaphoreType.DMA((7,)), pltpu.SemaphoreType.DMA((7,))],
        compiler_params=pltpu.CompilerParams(has_side_effects=True),
    )(vals)


def _adamw_math(w, g, m, v):
    m = ADAM_B1 * m + (1.0 - ADAM_B1) * g
    v = ADAM_B2 * v + (1.0 - ADAM_B2) * (g * g)
    m_hat = m / (1.0 - ADAM_B1 ** ADAM_STEP)
    v_hat = v / (1.0 - ADAM_B2 ** ADAM_STEP)
    delta = -ADAM_LR * (m_hat / (jnp.sqrt(v_hat) + ADAM_EPS) + ADAM_WD * w)
    return delta, m, v


ROW_TILE_ELEMS = 384 * 1024


def _row_tile(rows, width):
    best = SUBLANES
    for t in range(SUBLANES, rows + 1, SUBLANES):
        if rows % t == 0 and t * width <= ROW_TILE_ELEMS:
            best = t
    return best


def _sum_received(r_ref, own, me):
    g = None
    for j in range(N_DEV):
        term = jnp.where(me == j, own, r_ref[j].astype(F32))
        g = term if g is None else g + term
    return g


def _my_index_operand():
    return _index(*_me()).astype(jnp.int32).reshape(1)


def _sum_small(recv, own):
    def body(me_ref, r_ref, own_ref, out_ref):
        out_ref[...] = _sum_received(r_ref, own_ref[...], me_ref[0])

    whole = lambda shape: pl.BlockSpec(shape, lambda i, me, nd=len(shape): (0,) * nd)
    return pl.pallas_call(
        body, name="small_grads_sum", out_shape=jax.ShapeDtypeStruct(own.shape, F32),
        grid_spec=pltpu.PrefetchScalarGridSpec(
            num_scalar_prefetch=1, grid=(1,), in_specs=[whole(recv.shape), whole(own.shape)],
            out_specs=whole(own.shape)),
        compiler_params=_params(("arbitrary",)),
    )(_my_index_operand(), recv, own)


def _adamw_recv(recv, grad, kind, w, m, v, name):
    _, rows, width = recv.shape
    tile = _row_tile(rows, width)
    nt = rows // tile

    def body(me_ref, r_ref, own_ref, w_ref, m_ref, v_ref, g_out, d_out, m_out, v_out):
        g = _sum_received(r_ref, own_ref[...].astype(F32), me_ref[0])
        d, mn, vn = _adamw_math(w_ref[...], g, m_ref[...], v_ref[...])
        g_out[...] = g
        d_out[...] = d
        m_out[...] = mn
        v_out[...] = vn

    if kind == "rows":
        own_spec = pl.BlockSpec((tile, width), lambda i, me: (me[0] * nt + i, 0))
    else:
        own_spec = pl.BlockSpec((tile, width), lambda i, me: (i, me[0]))
    spec = pl.BlockSpec((tile, width), lambda i, me: (i, 0))
    shape = jax.ShapeDtypeStruct((rows, width), F32)
    return pl.pallas_call(
        body, name=name, out_shape=[shape] * 4,
        grid_spec=pltpu.PrefetchScalarGridSpec(
            num_scalar_prefetch=1, grid=(nt,),
            in_specs=[pl.BlockSpec((N_DEV, tile, width), lambda i, me: (0, i, 0)), own_spec, spec, spec, spec],
            out_specs=[spec] * 4),
        compiler_params=_params(("parallel",)),
    )(_my_index_operand(), recv, grad, w, m, v)


WIN_STEP = 1408
WIN_W = 1536
IN_SHARD = IN_COLS // N_DEV
IN_PADDED = WIN_STEP * (N_DEV - 1) + WIN_W


def _roll_w_in(shard_padded):
    rows = shard_padded.shape[0]
    tile = _row_tile(rows, WIN_W)

    def body(x_ref, main_ref, edge_ref):
        win = pltpu.roll(x_ref[...], 2 * _index(*_me()), 1).astype(BF16)
        main_ref[...] = win[:, :WIN_STEP]
        edge_ref[...] = win[:, WIN_STEP:]

    return pl.pallas_call(
        body, name="w_in_window",
        out_shape=[jax.ShapeDtypeStruct((rows, WIN_STEP), BF16), jax.ShapeDtypeStruct((rows, WIN_W - WIN_STEP), BF16)],
        grid=(rows // tile,),
        in_specs=[pl.BlockSpec((tile, WIN_W), lambda i: (i, 0))],
        out_specs=[pl.BlockSpec((tile, WIN_STEP), lambda i: (i, 0)),
                   pl.BlockSpec((tile, WIN_W - WIN_STEP), lambda i: (i, 0))],
        compiler_params=_params(("parallel",)),
    )(shard_padded)


def _sum_w_in_windows(recv, grad):
    _, rows, width = recv.shape
    tile = _row_tile(rows, width)

    def body(me_ref, r_ref, g_ref, g_out, own_ref, sem):
        me = me_ref[0]
        rows_i = pl.ds(pl.multiple_of(pl.program_id(0) * tile, tile), tile)
        own = pltpu.make_async_copy(g_ref.at[rows_i, pl.ds(pl.multiple_of(me * WIN_STEP, LANES), width)], own_ref, sem)
        own.start()
        own.wait()
        g_out[...] = pltpu.roll(_sum_received(r_ref, own_ref[...].astype(F32), me), width - 2 * me, 1)

    return pl.pallas_call(
        body, name="w_in_grad_sum", out_shape=jax.ShapeDtypeStruct((rows, width), F32),
        grid_spec=pltpu.PrefetchScalarGridSpec(
            num_scalar_prefetch=1, grid=(rows // tile,),
            in_specs=[pl.BlockSpec((N_DEV, tile, width), lambda i, me: (0, i, 0)), HBM_SPEC],
            out_specs=pl.BlockSpec((tile, width), lambda i, me: (i, 0)),
            scratch_shapes=[pltpu.VMEM((tile, width), BF16), pltpu.SemaphoreType.DMA]),
        compiler_params=_params(("arbitrary",)),
    )(_my_index_operand(), recv, grad)


def _adamw_small(w, g, m, v, name):
    def fn(i, n, w_, g_, m_, v_):
        return _adamw_math(w_, g_, m_, v_)

    r, c = w.shape
    return _rows(fn, [(w, "t"), (g, "t"), (m, "t"), (v, "t")], [], [(c, F32)] * 3, [], _row_tile(r, c), name)


def _norm_fwd(x, w, name):
    return _rows(lambda i, n, x_, w_: (_rms(x_, w_[...]),), [(x, "t")], [w], [(D_MODEL, BF16)], [], 512, name)[0]


def _residual_norm_fwd(x, y, scale, w, name):
    def fn(i, n, x_, y_, w_):
        xn = x_ + scale * y_
        return xn, _rms(xn, w_[...])

    return _rows(fn, [(x, "t"), (y, "t")], [w], [(D_MODEL, F32), (D_MODEL, BF16)], [], 512, name)


def _residual_norm_bwd(x, w, dhs, dres, scale, name):
    nh = len(dhs)

    def fn(i, n, x_, dres_, *rest):
        dh = rest[0]
        for extra in rest[1:nh]:
            dh = dh + extra
        _, vjp = jax.vjp(_rms, x_, rest[nh][...])
        dx, dw = vjp(dh)
        dx = dx + dres_
        return dx, scale * dx, dw

    return _rows(fn, [(x, "t"), (dres, "t")] + [(d, "t") for d in dhs], [w],
                 [(D_MODEL, F32), (D_MODEL, BF16)], [(1, D_MODEL)], 256, name)


FFN_UP_TN = 512


def _ffn_up(h, w_gu, name):
    t, d = h.shape
    f = w_gu.shape[1] // 2
    tm = _pick(t, (1024, 512, 256, 128))
    nj = f // FFN_UP_TN

    def body(h_ref, wg_ref, wu_ref, g_ref, u_ref, act_ref):
        hb = h_ref[...]
        g = jnp.dot(hb, wg_ref[...], preferred_element_type=F32)
        u = jnp.dot(hb, wu_ref[...], preferred_element_type=F32)
        g_ref[...] = g.astype(BF16)
        u_ref[...] = u.astype(BF16)
        act_ref[...] = (_silu(g) * u).astype(BF16)

    out = pl.BlockSpec((tm, FFN_UP_TN), lambda i, j: (i, j))
    return pl.pallas_call(
        body, name=name, out_shape=[jax.ShapeDtypeStruct((t, f), BF16)] * 3, grid=(t // tm, nj),
        in_specs=[pl.BlockSpec((tm, d), lambda i, j: (i, 0)),
                  pl.BlockSpec((d, FFN_UP_TN), lambda i, j: (0, j)),
                  pl.BlockSpec((d, FFN_UP_TN), lambda i, j: (0, j + nj))],
        out_specs=[out, out, out],
        compiler_params=_params(("parallel", "parallel")),
    )(h, w_gu, w_gu)


def _ffn_fwd(h, w_gu, get_w_down, tag):
    g, u, act = _ffn_up(h, w_gu, tag + "_gu")
    y = _matmul(act, get_w_down(act), "nn", F32, tag + "_down")
    return (g, u), act, y


def _ffn_dact(dy, w_down, g, u, name):
    t, d = dy.shape
    f = w_down.shape[0]
    tm = _pick(t, (1024, 512, 256, 128))

    def body(dy_ref, w_ref, g_ref, u_ref, out_ref):
        dact = lax.dot_general(dy_ref[...], w_ref[...], NT, preferred_element_type=F32)
        g_, u_ = g_ref[...].astype(F32), u_ref[...].astype(F32)
        sg = _sigmoid(g_)
        out_ref[0] = (dact * u_ * (sg * (1.0 + g_ * (1.0 - sg)))).astype(BF16)
        out_ref[1] = (dact * (g_ * sg)).astype(BF16)

    tile = pl.BlockSpec((tm, FFN_UP_TN), lambda i, j: (i, j))
    return pl.pallas_call(
        body, name=name, out_shape=jax.ShapeDtypeStruct((2, t, f), BF16), grid=(t // tm, f // FFN_UP_TN),
        in_specs=[pl.BlockSpec((tm, d), lambda i, j: (i, 0)), pl.BlockSpec((FFN_UP_TN, d), lambda i, j: (j, 0)),
                  tile, tile],
        out_specs=pl.BlockSpec((2, tm, FFN_UP_TN), lambda i, j: (0, i, j)),
        compiler_params=_params(("parallel", "parallel")),
    )(dy, w_down, g, u)


def _ffn_bwd(h, gu, act, dy, w_gu, w_down, tag, comm, more=None):
    dgu = _ffn_dact(dy, w_down, gu[0], gu[1], tag + "_dact")
    sent = comm.send(tag + "_gu", {tag + "_w_gu": _matmul(h, dgu, "tn", BF16, tag + "_d_w_gu")})
    sent = sent + comm.send(tag + "_down", {tag + "_w_down": _matmul(act, dy + sent.astype(BF16), "tn", BF16,
                                                                    tag + "_d_w_down"), **(more or {})})
    dh = _matmul(dgu, w_gu, "nt", BF16, tag + "_dh")
    return dh, sent


def _expanders():
    e_g = np.zeros((LANES, HW), np.float32)
    e_b = np.zeros((LANES, HW), np.float32)
    for h in range(HEADS):
        e_g[h, h * HEAD_DIM:(h + 1) * HEAD_DIM] = 1.0
        e_b[HEADS + h, h * HEAD_DIM:(h + 1) * HEAD_DIM] = 1.0
    return jnp.asarray(e_g), jnp.asarray(e_b)


def _pad_lanes(v):
    return jnp.pad(v, ((0, 0), (0, LANES - v.shape[1])))


class _LocalWeights:
    def __init__(self, big):
        self.big, self.sent = big, {}

    def arrive(self, group, after):
        return self.big

    def send(self, group, grads):
        self.sent.update(grads)
        return jnp.zeros((), F32)


def _local_step(x, p, tgt, small, comm):
    e_g, e_b = _expanders()
    alog, dtb = _pad_lanes(small["a_log"]), _pad_lanes(small["dt_bias"])
    conv_w = jnp.pad(small["conv_w"], ((0, SUBLANES - CONV_K), (0, 0)))
    rel = _expand_rel_bias(small["rel_bias"])

    h1 = _norm_fwd(x, small["ffn1_norm"], "ffn1_norm")
    big = dict(comm.arrive("ffn1", h1))
    if "_token" in big:
        h1 = h1 + big.pop("_token").astype(BF16)

    def ffn1_w_down(act):
        big.update(comm.arrive("ffn1_down", act))
        return big["ffn1_w_down"]

    gu1, act1, y1 = _ffn_fwd(h1, big["ffn1_w_gu"], ffn1_w_down, "ffn1")
    x1, h2 = _residual_norm_fwd(x, y1, 0.5, small["mix_norm"], "mix_norm")

    big = {**big, **comm.arrive("mixer", h2)}
    w_in = big["w_in"]
    w_qz = w_in[:, :IN_QZ]
    w_ab = jnp.pad(w_in[:, IN_AB0:IN_QKVB0], ((0, 0), (0, LANES - 2 * HEADS)))
    w_qkvb = w_in[:, IN_QKVB0:IN_GG0]
    w_gg = w_in[:, IN_GG0:IN_COLS]
    qz = _matmul(h2, w_qz, "nn", F32, "in_qz")
    ab = _matmul(h2, w_ab, "nn", F32, "in_ab")
    pb = _matmul(h2, w_qkvb, "nn", F32, "in_qkvb")
    gg = _matmul(h2, w_gg, "nn", BF16, "in_gates")
    pa, z = qz[:, :3 * HW], qz[:, 3 * HW:]

    def prep(i, n, pa_, prev_, ab_, cw_, alog_, dtb_, eg_, eb_):
        q, k, v = _gdn_post(_conv(pa_, prev_, cw_, i))
        g_b, beta_b = _gdn_gates(ab_, alog_[...], dtb_[...], eg_[...], eb_[...])
        return q, k, v, g_b, beta_b

    qn, kn, vv, g_b, beta_b = _rows(prep, [(pa, "t"), (pa, "p"), (ab, "t")], [conv_w, alog, dtb, e_g, e_b],
                                    [(HW, F32)] * 5, [], 256, "gdn_prep")
    u, w, aqk, qd, kt, tl = _gdn_intra(qn, kn, vv, g_b, beta_b)
    o, states = _gdn_scan(u, w, aqk, qd, kt, tl)
    ya = _rows(lambda i, n, o_, z_, w_: (_gated_norm(o_, z_, w_[...]),), [(o, "t"), (z, "t")], [small["gdn_norm"]],
               [(HW, BF16)], [], 512, "gdn_gated_norm")[0]

    yb = _attention(pb, small["q_norm"], small["k_norm"], rel)

    big = {**big, **comm.arrive("branches", yb)}
    ta = _matmul(ya, big["w_branch_a"], "nn", BF16, "branch_a")
    tb = _matmul(yb, big["w_branch_b"], "nn", BF16, "branch_b")
    mixed = _rows(lambda i, n, gg_, ta_, tb_: (_mix(gg_, ta_, tb_),), [(gg, "t"), (ta, "t"), (tb, "t")], [],
                  [(D_MODEL, BF16)], [], 256, "mix")[0]
    m_out = _matmul(mixed, big["w_out"], "nn", F32, "w_out")
    x2, h3 = _residual_norm_fwd(x1, m_out, 1.0, small["ffn2_norm"], "ffn2_norm")
    big = {**big, **comm.arrive("tail", h3)}
    gu2, act2, y2 = _ffn_fwd(h3, big["ffn2_w_gu"], lambda act: big["ffn2_w_down"], "ffn2")
    x3, h4 = _residual_norm_fwd(x2, y2, 0.5, small["ple_norm"], "ple_norm")
    gp = _matmul(h4, big["ple_gate"], "nn", BF16, "ple_gate")
    pp = _matmul(p, big["ple_proj"], "nn", BF16, "ple_proj")

    def head(i, n, x3_, gp_, pp_, tgt_):
        sg = _sigmoid(gp_)
        err = x3_ + sg * pp_ - tgt_
        dx4 = err * (1.0 / D_MODEL)
        sq = _colsum(err * err)
        part = sq[:, :LANES]
        for j in range(1, D_MODEL // LANES):
            part = part + sq[:, j * LANES:(j + 1) * LANES]
        return dx4, dx4 * pp_ * sg * (1.0 - sg), dx4 * sg, (0.5 / D_MODEL) * part

    dx4, dgp, dpp, loss_lanes = _rows(head, [(x3, "t"), (gp, "t"), (pp, "t"), (tgt, "t")], [],
                                      [(D_MODEL, F32), (D_MODEL, BF16), (D_MODEL, BF16)], [(1, LANES)], 256,
                                      "ple_loss_head")
    loss = jnp.sum(loss_lanes)

    gbig, gsmall = {}, {}
    gbig["ple_proj"] = _matmul(p, dpp, "tn", BF16, "d_ple_proj")
    gbig["ple_gate"] = _matmul(h4, dgp, "tn", BF16, "d_ple_gate")
    dh4 = _matmul(dgp, big["ple_gate"], "nt", BF16, "ple_gate_dh")
    dx3, dy2, gsmall["ple_norm"] = _residual_norm_bwd(x3, small["ple_norm"], [dh4], dx4, 0.5, "ple_norm_bwd")

    dh3, sent = _ffn_bwd(h3, gu2, act2, dy2, big["ffn2_w_gu"], big["ffn2_w_down"], "ffn2", comm,
                         {n: gbig[n] for n in ("ple_proj", "ple_gate")})
    dx2, dx2b, gsmall["ffn2_norm"] = _residual_norm_bwd(x2, small["ffn2_norm"] + sent, [dh3], dx3, 1.0,
                                                        "ffn2_norm_bwd")

    gbig["w_out"] = _matmul(mixed, dx2b, "tn", BF16, "d_w_out")
    dmixed = _matmul(dx2b, big["w_out"], "nt", BF16, "w_out_dx")

    def mix_bwd(i, n, gg_, ta_, tb_, dm_):
        _, vjp = jax.vjp(_mix, gg_, ta_, tb_)
        return vjp(dm_)

    dgg, dta, dtb_ = _rows(mix_bwd, [(gg, "t"), (ta, "t"), (tb, "t"), (dmixed, "t")], [],
                           [(2 * D_MODEL, BF16), (D_MODEL, BF16), (D_MODEL, BF16)], [], 256, "mix_bwd")
    gbig["w_branch_a"] = _matmul(ya, dta, "tn", BF16, "d_branch_a")
    gbig["w_branch_b"] = _matmul(yb, dtb_, "tn", BF16, "d_branch_b")
    dya = _matmul(dta, big["w_branch_a"], "nt", BF16, "branch_a_dx")
    dyb = _matmul(dtb_, big["w_branch_b"], "nt", BF16, "branch_b_dx")

    dq_b, dk_b, dv_b, gsmall["q_norm"], gsmall["k_norm"], gsmall["rel_bias"] = _attention_bwd(
        pb, small["q_norm"], small["k_norm"], rel, dyb)
    dpb = jnp.concatenate([dq_b, dk_b[ATT_PAD:].astype(BF16), dv_b[ATT_PAD:].astype(BF16)], axis=1)

    def gated_bwd(i, n, o_, z_, dya_, w_):
        _, vjp = jax.vjp(_gated_norm, o_, z_, w_[...])
        return vjp(dya_)

    do, dz, gsmall["gdn_norm"] = _rows(gated_bwd, [(o, "t"), (z, "t"), (dya, "t")], [small["gdn_norm"]],
                                       [(HW, F32), (HW, BF16)], [(1, HEAD_DIM)], 256, "gdn_gated_norm_bwd")
    du, dw, da, dqd, dkt, dtl = _gdn_scan_bwd(do, u, w, aqk, qd, kt, tl, states)
    dqn, dkn, dvv, dg_b, dbeta_b = _gdn_intra_bwd(qn, kn, vv, g_b, beta_b, du, dw, da, dqd, dkt, dtl)

    def prep_bwd(i, n, pa_, prev_, ab_, dq_, dk_, dv_, dg_, db_, cw_, alog_, dtb_, eg_, eb_):
        _, vjp = jax.vjp(_gdn_post, _conv(pa_, prev_, cw_, i))
        (dy,) = vjp((dq_, dk_, dv_))
        e_g_, e_b_ = eg_[...], eb_[...]
        _, vjp_g = jax.vjp(lambda a, b, c: _gdn_gates(a, b, c, e_g_, e_b_), ab_, alog_[...], dtb_[...])
        dab, dalog, ddtb = vjp_g((dg_, db_))
        return dy, dab, dalog, ddtb

    dy_conv, dab, dalog, ddtb = _rows(
        prep_bwd, [(pa, "t"), (pa, "p"), (ab, "t"), (dqn, "t"), (dkn, "t"), (dvv, "t"), (dg_b, "t"), (dbeta_b, "t")],
        [conv_w, alog, dtb, e_g, e_b], [(3 * HW, F32), (LANES, BF16)], [(1, LANES), (1, LANES)], 256,
        "gdn_prep_bwd")
    gsmall["a_log"] = dalog[:, :HEADS]
    gsmall["dt_bias"] = ddtb[:, :HEADS]

    def conv_bwd(i, n, dy_, nxt_, pa_, prev_, cw_):
        dpa = dy_ * cw_[CONV_K - 1:CONV_K, :]
        row = lax.broadcasted_iota(jnp.int32, (SUBLANES, dy_.shape[1]), 0)
        dcw = jnp.where(row == CONV_K - 1, _colsum(dy_ * pa_), 0.0)
        for j in range(CONV_K - 1):
            s = CONV_K - 1 - j
            dpa = dpa + _shift_up(dy_, nxt_, s, i, n) * cw_[j:j + 1, :]
            dcw = dcw + jnp.where(row == j, _colsum(dy_ * _shift_down(pa_, prev_, s, i)), 0.0)
        return dpa, dcw

    dpa, dcw = _rows(conv_bwd, [(dy_conv, "t"), (dy_conv, "n"), (pa, "t"), (pa, "p")], [conv_w],
                     [(3 * HW, BF16)], [(SUBLANES, 3 * HW)], 256, "gdn_conv_bwd")
    gsmall["conv_w"] = dcw[:CONV_K]

    dqz = jnp.concatenate([dpa, dz], axis=1)
    d_w_qz = _matmul(h2, dqz, "tn", BF16, "d_in_qz")
    d_w_ab = _matmul(h2, dab, "tn", BF16, "d_in_ab")
    d_w_qkvb = _matmul(h2, dpb, "tn", BF16, "d_in_qkvb")
    d_w_gg = _matmul(h2, dgg, "tn", BF16, "d_in_gates")
    gbig["w_in"] = jnp.concatenate([d_w_qz, d_w_ab[:, :2 * HEADS], d_w_qkvb, d_w_gg,
                                    jnp.zeros((D_MODEL, IN_PADDED - IN_COLS), BF16)], axis=1)
    dh2 = [_matmul(dqz, w_qz, "nt", BF16, "in_qz_dh"), _matmul(dab, w_ab, "nt", BF16, "in_ab_dh"),
           _matmul(dpb, w_qkvb, "nt", BF16, "in_qkvb_dh"), _matmul(dgg, w_gg, "nt", BF16, "in_gates_dh")]
    sent = comm.send("mixer", {n: gbig[n] for n in ("w_out", "w_branch_b", "w_branch_a", "w_in")})
    dx1, dy1, gsmall["mix_norm"] = _residual_norm_bwd(x1, small["mix_norm"] + sent, dh2, dx2, 0.5, "mix_norm_bwd")

    dh1, sent = _ffn_bwd(h1, gu1, act1, dy1, big["ffn1_w_gu"], big["ffn1_w_down"], "ffn1", comm)
    grad_x, _, gsmall["ffn1_norm"] = _residual_norm_bwd(x, small["ffn1_norm"] + sent, [dh1], dx1, 1.0,
                                                        "ffn1_norm_bwd")
    return loss, grad_x, gsmall


GATHER_GROUPS = {"ffn1": ("ffn1_w_gu",),
                 "ffn1_down": ("ffn1_w_down",),
                 "mixer": ("w_in_main", "w_in_edge"),
                 "branches": ("w_branch_a", "w_branch_b", "w_out"),
                 "tail": ("ffn2_w_gu", "ffn2_w_down", "ple_gate", "ple_proj")}
SPLIT_GATHERS = ("ffn1_down", "mixer", "branches", "tail")


def _kind(name):
    return "cols" if name in COL_SHARDED or name.startswith("w_in_") else "rows"


def _merge_w_in(main, edges):
    edge_w = WIN_W - WIN_STEP
    w_in = jnp.pad(main, ((0, 0), (0, edge_w)))
    for d in range(N_DEV):
        at = WIN_STEP * (d + 1)
        w_in = w_in + jnp.pad(edges[:, d * edge_w:(d + 1) * edge_w], ((0, 0), (at, IN_PADDED - at - edge_w)))
    return w_in


class _Fsdp:
    def __init__(self, wts, first):
        self.wts, self.first_token = wts, first
        main, edge = _roll_w_in(jnp.pad(wts["w_in"], ((0, 0), (0, WIN_W - IN_SHARD))))
        self.shards = {n: wts[n].astype(BF16) for n in BIG if n not in ("w_in", "ffn1_w_gu")}
        self.shards.update(w_in_main=main, w_in_edge=edge)
        self.lands = {n: _place_block(self.shards[n], _kind(n), "own_" + n)
                      for group in SPLIT_GATHERS for n in GATHER_GROUPS[group]}
        self.flight, self.sent = {}, {}

    def _gather_first(self, after):
        token = self.first_token + after[0, 0].astype(F32) * 0.0
        me = _index(*_me())
        for n, land in self.lands.items():
            r, c = self.shards[n].shape
            at = (me * r, 0) if _kind(n) == "rows" else (0, me * c)
            token = token + lax.dynamic_slice(land, at, (1, 1))[0, 0].astype(F32) * 0.0
        shard = (self.wts["ffn1_w_gu"] + token).astype(BF16)
        self.shards["ffn1_w_gu"] = shard
        first = _all_gather([shard], [_kind("ffn1_w_gu")], 1)[0]
        token = first[0, 0].astype(F32) * 0.0
        for group in SPLIT_GATHERS:
            names = GATHER_GROUPS[group]
            srcs = [self.shards[n] for n in names]
            lands = [self.lands[n] for n in names]
            make = _gather_copies([s.shape for s in srcs], [_kind(n) for n in names])
            srcs[0] = srcs[0] + token.astype(BF16)
            send_sems, recv_sems, srcs, lands, tok = _split_start(srcs, lands, make, "gather_start_" + group)
            token = token + tok[0, 0]
            self.flight[group] = (send_sems, recv_sems, srcs, lands, make)
        return {"ffn1_w_gu": first, "_token": token}

    def arrive(self, group, after):
        if group == "ffn1":
            return self._gather_first(after)
        send_sems, recv_sems, srcs, lands, make = self.flight[group]
        _, lands = _split_wait(send_sems, recv_sems, srcs, lands, after, make, "gather_wait_" + group)
        full = dict(zip(GATHER_GROUPS[group], lands))
        if group == "mixer":
            full["w_in"] = _merge_w_in(full.pop("w_in_main"), full.pop("w_in_edge"))
        return full

    def send(self, group, grads):
        names = list(grads)
        kinds = ["all" if n == "small" else "win" if n == "w_in" else _kind(n) for n in names]
        shapes = [grads[n].shape if n == "small" else (D_MODEL, WIN_W) if n == "w_in" else self.shards[n].shape
                  for n in names]
        srcs = [grads[n] for n in names]
        lands = [lax.empty((N_DEV,) + tuple(s), g.dtype) for s, g in zip(shapes, srcs)]
        make = _exchange_copies(shapes, kinds)
        send_sems, recv_sems, srcs, lands, tok = _split_start(srcs, lands, make, "grads_start_" + group)
        self.sent[group] = (names, kinds, send_sems, recv_sems, srcs, lands, make)
        return tok[0, 0]

    def received(self, group, after):
        names, kinds, send_sems, recv_sems, srcs, lands, make = self.sent[group]
        srcs, lands = _split_wait(send_sems, recv_sems, srcs, lands, after, make, "grads_wait_" + group)
        return {n: (k, g, r) for n, k, g, r in zip(names, kinds, srcs, lands)}


SMALL_ROWS = ("ffn1_norm", "mix_norm", "ffn2_norm", "ple_norm", "gdn_norm", "q_norm", "k_norm", "a_log", "dt_bias",
              "rel_bias", "conv_w")


def _pack_small(vals):
    rows = []
    for n in SMALL_ROWS:
        v = vals[n]
        if n == "rel_bias":
            v = jnp.pad(v, ((0, 0), (0, 2 * LANES - N_REL)))
        elif n in ("a_log", "dt_bias"):
            v = _pad_lanes(v)
        rows.append(v.reshape(-1, LANES))
    packed = jnp.concatenate(rows, axis=0)
    return jnp.pad(packed, ((0, -packed.shape[0] % SUBLANES), (0, 0)))


def _unpack_small(packed, shapes):
    out, off = {}, 0
    for n in SMALL_ROWS:
        shp = shapes[n]
        if n == "rel_bias":
            out[n] = packed[off:off + 2 * HEADS].reshape(HEADS, 2 * LANES)[:, :N_REL]
            off += 2 * HEADS
        elif n in ("a_log", "dt_bias"):
            out[n] = packed[off:off + 1, :HEADS]
            off += 1
        else:
            r = int(np.prod(shp)) // LANES
            out[n] = packed[off:off + r].reshape(shp)
            off += r
    return out


WEIGHTS = ("ffn1_norm", "ffn1_w_gu", "ffn1_w_down", "mix_norm", "w_in", "conv_w", "a_log", "dt_bias", "gdn_norm",
           "q_norm", "k_norm", "rel_bias", "w_branch_a", "w_branch_b", "w_out", "ffn2_norm", "ffn2_w_gu",
           "ffn2_w_down", "ple_norm", "ple_gate", "ple_proj")


def kernel(x, p, ffn1_norm, ffn1_w_gu, ffn1_w_down, mix_norm, w_in, conv_w, a_log, dt_bias, gdn_norm, q_norm, k_norm, rel_bias, w_branch_a, w_branch_b, w_out, ffn2_norm, ffn2_w_gu, ffn2_w_down, ple_norm, ple_gate, ple_proj, loss_target, m_ffn1_norm, m_ffn1_w_gu, m_ffn1_w_down, m_mix_norm, m_w_in, m_conv_w, m_a_log, m_dt_bias, m_gdn_norm, m_q_norm, m_k_norm, m_rel_bias, m_w_branch_a, m_w_branch_b, m_w_out, m_ffn2_norm, m_ffn2_w_gu, m_ffn2_w_down, m_ple_norm, m_ple_gate, m_ple_proj, v_ffn1_norm, v_ffn1_w_gu, v_ffn1_w_down, v_mix_norm, v_w_in, v_conv_w, v_a_log, v_dt_bias, v_gdn_norm, v_q_norm, v_k_norm, v_rel_bias, v_w_branch_a, v_w_branch_b, v_w_out, v_ffn2_norm, v_ffn2_w_gu, v_ffn2_w_down, v_ple_norm, v_ple_gate, v_ple_proj):
    args = dict(locals())
    def layer0(v):
        return v[0] if v.ndim == 3 else v

    wts = {n: layer0(args[n]) for n in WEIGHTS}
    mom = {n: layer0(args["m_" + n]) for n in WEIGHTS}
    var = {n: layer0(args["v_" + n]) for n in WEIGHTS}
    x2d, p2d, tgt = x[0], p[0, 0], loss_target[0]
    my_index = _index(*_me())

    small = {n: wts[n] for n in SMALL_ROWS if n != "conv_w"}
    conv_shard = wts["conv_w"]
    conv_cols = conv_shard.shape[1]
    conv_packed = jnp.zeros((SUBLANES, N_DEV * conv_cols), F32)
    conv_packed = lax.dynamic_update_slice(conv_packed, jnp.pad(conv_shard, ((0, SUBLANES - CONV_K), (0, 0))),
                                           (0, my_index * conv_cols))
    small["conv_w"] = _all_reduce_small(conv_packed.reshape(-1, LANES), "conv_w_gather").reshape(SUBLANES, -1)[:CONV_K]

    fsdp = _Fsdp(wts, small["conv_w"][0, 0] * 0.0)

    loss, grad_x, gsmall = _local_step(x2d, p2d, tgt, small, fsdp)
    loss = lax.psum(loss, ("x", "y", "c"))

    fsdp.send("small", {"small": _pack_small(gsmall)})

    outs_big, after = {}, grad_x
    for group in list(fsdp.sent):
        for n, (kind, grad, recv) in fsdp.received(group, after).items():
            if n == "small":
                small_sum = _sum_small(recv, grad)
            elif n == "w_in":
                g_in = _sum_w_in_windows(recv, grad)[:, :IN_SHARD]
                outs_big[n] = [g_in] + list(_adamw_small(wts[n], g_in, mom[n], var[n], "adamw_w_in"))
            else:
                outs_big[n] = _adamw_recv(recv, grad, kind, wts[n], mom[n], var[n], "adamw_" + n)
            after = small_sum if n == "small" else outs_big[n][1]

    small_shapes = {n: (small[n].shape if n != "conv_w" else (CONV_K, N_DEV * conv_cols)) for n in SMALL_ROWS}
    gsum = _unpack_small(small_sum, small_shapes)
    gsum["conv_w"] = lax.dynamic_slice(gsum["conv_w"], (0, my_index * conv_cols), (CONV_K, conv_cols))
    rep = [n for n in SMALL_ROWS if n != "conv_w"]
    rep_shapes = {n: small_shapes[n] for n in rep}

    def pack_rep(vals):
        return _pack_small({**{n: vals[n] for n in rep}, "conv_w": jnp.zeros((CONV_K, LANES), F32)})

    def unpack_rep(packed):
        return _unpack_small(packed, {**rep_shapes, "conv_w": (CONV_K, LANES)})

    outs_small = [unpack_rep(o) for o in _adamw_small(pack_rep(wts), pack_rep(gsum), pack_rep(mom), pack_rep(var),
                                                      "adamw_replicated")]
    pad8 = functools.partial(jnp.pad, pad_width=((0, SUBLANES - CONV_K), (0, 0)))
    outs_conv = [o[:CONV_K] for o in _adamw_small(pad8(conv_shard), pad8(gsum["conv_w"]), pad8(mom["conv_w"]),
                                                   pad8(var["conv_w"]), "adamw_conv")]

    def leaf(kind, n):
        if n in BIG:
            return outs_big[n][kind][None]
        if n == "conv_w":
            return (gsum["conv_w"] if kind == 0 else outs_conv[kind - 1])[None]
        return (gsum[n] if kind == 0 else outs_small[kind - 1][n]).reshape(args[n].shape)

    result = [loss, grad_x[None]]
    for kind in range(4):
        result += [leaf(kind, n) for n in WEIGHTS]
    return tuple(result)
```

```python
import functools

import numpy as np
import jax
import jax.numpy as jnp
from jax import lax
from jax.experimental import pallas as pl
from jax.experimental.pallas import tpu as pltpu

F32 = jnp.float32
BF16 = jnp.bfloat16
HIGHEST = lax.Precision.HIGHEST
MESH = pl.DeviceIdType.MESH

D_MODEL = 2048
D_FF = 5632
HEADS = 8
HEAD_DIM = 128
HW = HEADS * HEAD_DIM
CHUNK = 64
LEFT_CHUNKS = 8
MAX_REL = 128
N_REL = (CHUNK - 1) + MAX_REL + 1
CONV_K = 4
EPS = 1e-6
NEG_INF = -1e30
N_DEV = 8
LANES = 128
SUBLANES = 8
VMEM_LIMIT = 56 * 1024 * 1024

MATMUL_WHOLE_K = 2048

ATT_QB = 256
ATT_KW = ATT_QB + LEFT_CHUNKS * CHUNK
ATT_PAD = LEFT_CHUNKS * CHUNK
GDN_CB = 8
GDN_GROUP = 32
GDN_SCAN_UNROLL = 4

ADAM_LR = 0.001
ADAM_B1 = 0.9
ADAM_B2 = 0.999
ADAM_EPS = 1e-08
ADAM_WD = 0.01
ADAM_STEP = 10

IN_QZ = 3 * HW + HW
IN_AB0 = IN_QZ
IN_QKVB0 = IN_AB0 + 2 * HEADS
IN_GG0 = IN_QKVB0 + 3 * HW
IN_COLS = IN_GG0 + 2 * D_MODEL

BIG = ("ffn1_w_gu", "ffn1_w_down", "w_in", "w_branch_a", "w_branch_b", "w_out",
       "ffn2_w_gu", "ffn2_w_down", "ple_gate", "ple_proj")
COL_SHARDED = ("ffn1_w_gu", "w_in", "w_branch_a", "w_branch_b", "ffn2_w_gu", "ple_proj")


def _params(semantics=None, **kw):
    return pltpu.CompilerParams(dimension_semantics=semantics, vmem_limit_bytes=VMEM_LIMIT, **kw)


def _pick(n, cands):
    for c in cands:
        if n % c == 0:
            return c
    return n


SMEM_SPEC = pl.BlockSpec(memory_space=pltpu.SMEM)


def _after(token):
    return ([], []) if token is None else ([SMEM_SPEC], [jnp.reshape(token, (1,)).astype(F32)])


def _matmul(a, b, mode, out_dtype, name, after=None):
    halves = (a.ndim == 3 and mode == "nt") or (b.ndim == 3 and mode == "tn")
    if mode == "nn":
        (m, k), (k2, n) = a.shape, b.shape
    elif mode == "nt":
        (m, k), (n, k2) = (a.shape[-2], a.shape[-1] * (a.ndim - 1)), b.shape
    else:
        (k, m), (k2, n) = a.shape, (b.shape[-2], b.shape[-1] * (b.ndim - 1))
    assert k == k2 and a.ndim + b.ndim == (5 if halves else 4), (a.shape, b.shape, mode)
    tm = _pick(m, (1024, 512, 256, 128))
    if halves and mode == "tn":
        tn = _pick(n // 2, (1408, 1024, 512, 256, 128))
        tk = _pick(k, (2048, 1024, 512, 256, 128))
    elif halves:
        tn = _pick(n, (1024, 512, 256, 128))
        tk = _pick(k // 2, (2816, 2048, 1536, 1024, 512, 256, 128))
    else:
        tn = _pick(n, (1024, 512, 256, 128))
        tk = k if k <= MATMUL_WHOLE_K else _pick(k, (2816, 2048, 1536, 1024, 512, 256, 128))
    nk = k // tk
    per_half = (n // 2) // tn if mode == "tn" else (k // 2) // tk
    if mode == "nn":
        a_spec = pl.BlockSpec((tm, tk), lambda i, j, kk: (i, kk))
        b_spec = pl.BlockSpec((tk, tn), lambda i, j, kk: (kk, j))
        dims = (((1,), (0,)), ((), ()))
    elif mode == "nt":
        a_spec = pl.BlockSpec((tm, tk), lambda i, j, kk: (i, kk))
        b_spec = pl.BlockSpec((tn, tk), lambda i, j, kk: (j, kk))
        dims = (((1,), (1,)), ((), ()))
        if halves:
            a_spec = pl.BlockSpec((None, tm, tk), lambda i, j, kk: (kk // per_half, i, kk % per_half))
    else:
        a_spec = pl.BlockSpec((tk, tm), lambda i, j, kk: (kk, i))
        b_spec = pl.BlockSpec((tk, tn), lambda i, j, kk: (kk, j))
        dims = (((0,), (0,)), ((), ()))
        if halves:
            b_spec = pl.BlockSpec((None, tk, tn), lambda i, j, kk: (j // per_half, kk, j % per_half))

    after_specs, after_args = _after(after)

    def body(a_ref, b_ref, *rest):
        o_ref, acc = rest[len(after_args)], rest[len(after_args) + 1:]
        prod = lax.dot_general(a_ref[...].astype(BF16), b_ref[...].astype(BF16), dims, preferred_element_type=F32)
        if nk == 1:
            o_ref[...] = prod.astype(o_ref.dtype)
            return
        acc_ref, kk = acc[0], pl.program_id(2)

        @pl.when(kk == 0)
        def _():
            acc_ref[...] = prod

        @pl.when((kk > 0) & (kk < nk - 1))
        def _():
            acc_ref[...] += prod

        @pl.when(kk == nk - 1)
        def _():
            o_ref[...] = (acc_ref[...] + prod).astype(o_ref.dtype)

    return pl.pallas_call(
        body, name=name,
        out_shape=jax.ShapeDtypeStruct((m, n), out_dtype),
        grid=(m // tm, n // tn, nk),
        in_specs=[a_spec, b_spec] + after_specs,
        out_specs=pl.BlockSpec((tm, tn), lambda i, j, kk: (i, j)),
        scratch_shapes=[pltpu.VMEM((tm, tn), F32)] if nk > 1 else [],
        compiler_params=_params(("parallel", "parallel", "arbitrary")),
    )(a, b, *after_args)


def _rows(fn, row_ins, consts, row_outs, acc_outs, tile, name):
    t_rows = row_ins[0][0].shape[0]
    tile = min(tile, t_rows)
    assert t_rows % tile == 0 and tile % SUBLANES == 0
    n = t_rows // tile
    per = tile // SUBLANES
    last8 = t_rows // SUBLANES - 1
    in_specs = []
    for arr, kind in row_ins:
        c = arr.shape[1]
        if kind == "t":
            in_specs.append(pl.BlockSpec((tile, c), lambda i: (i, 0)))
        elif kind == "p":
            in_specs.append(pl.BlockSpec((SUBLANES, c), lambda i: (jnp.maximum(i * per - 1, 0), 0)))
        else:
            in_specs.append(pl.BlockSpec((SUBLANES, c), lambda i: (jnp.minimum((i + 1) * per, last8), 0)))
    for arr in consts:
        in_specs.append(pl.BlockSpec(arr.shape, lambda i, nd=arr.ndim: (0,) * nd))
    out_shape = [jax.ShapeDtypeStruct((t_rows, c), dt) for c, dt in row_outs]
    out_specs = [pl.BlockSpec((tile, c), lambda i: (i, 0)) for c, _ in row_outs]
    for shp in acc_outs:
        out_shape.append(jax.ShapeDtypeStruct(shp, F32))
        out_specs.append(pl.BlockSpec(shp, lambda i, nd=len(shp): (0,) * nd))
    n_in = len(row_ins) + len(consts)
    n_row_out = len(row_outs)

    def body(*refs):
        i = pl.program_id(0)
        vals = [r[...].astype(F32) for r in refs[:len(row_ins)]]
        res = fn(i, n, *vals, *refs[len(row_ins):n_in])
        outs = refs[n_in:]
        for r, v in zip(outs[:n_row_out], res[:n_row_out]):
            r[...] = v.astype(r.dtype)
        if acc_outs:
            @pl.when(i == 0)
            def _():
                for r in outs[n_row_out:]:
                    r[...] = jnp.zeros_like(r)

            for r, v in zip(outs[n_row_out:], res[n_row_out:]):
                r[...] += v

    res = pl.pallas_call(
        body, name=name, out_shape=out_shape, grid=(n,), in_specs=in_specs, out_specs=out_specs,
        compiler_params=_params(("arbitrary",) if acc_outs else ("parallel",)),
    )(*[a for a, _ in row_ins], *consts)
    return res


def _rms(x, w):
    return x * lax.rsqrt(jnp.mean(x * x, axis=-1, keepdims=True) + EPS) * w


def _l2n(x):
    return x * lax.rsqrt(jnp.sum(x * x, axis=-1, keepdims=True) + EPS)


def _sigmoid(x):
    return 1.0 / (1.0 + jnp.exp(-x))


def _silu(x):
    return x * _sigmoid(x)


def _softplus(x):
    return jnp.maximum(x, 0.0) + jnp.log(1.0 + jnp.exp(-jnp.abs(x)))


def _heads(fn, *xs):
    nh = xs[0].shape[1] // HEAD_DIM
    return jnp.concatenate(
        [fn(*[x[:, h * HEAD_DIM:(h + 1) * HEAD_DIM] for x in xs]) for h in range(nh)], axis=1)


def _colsum(x):
    return jnp.sum(x, axis=0, keepdims=True)


def _gated_norm(o, z, w):
    return _heads(lambda oh, zh: _rms(oh, w) * _silu(zh), o, z)


def _mix(gg, ta, tb):
    return _sigmoid(gg[:, :D_MODEL]) * ta + _sigmoid(gg[:, D_MODEL:]) * tb


def _gdn_post(y):
    a = _silu(y)
    q = _heads(lambda v: _l2n(v) * (HEAD_DIM ** -0.5), a[:, :HW])
    k = _heads(_l2n, a[:, HW:2 * HW])
    return q, k, a[:, 2 * HW:]


NN = (((1,), (0,)), ((), ()))
NT = (((1,), (1,)), ((), ()))
TN = (((0,), (0,)), ((), ()))


def _dg(a, b, dims):
    return lax.dot_general(a, b, dims, preferred_element_type=F32)


def _split2(x):
    hi = x.astype(BF16)
    return hi, (x - hi.astype(F32)).astype(BF16)


def _split3(x):
    hi = x.astype(BF16)
    r = x - hi.astype(F32)
    mid = r.astype(BF16)
    return hi, mid, (r - mid.astype(F32)).astype(BF16)


def _dg3(a, b, dims):
    ah, al = _split2(a)
    bh, bl = _split2(b)
    return _dg(ah, bh, dims) + (_dg(ah, bl, dims) + _dg(al, bh, dims))


BNN = (((2,), (1,)), ((0,), (0,)))
BNT = (((2,), (2,)), ((0,), (0,)))
BTN = (((1,), (1,)), ((0,), (0,)))


@jax.custom_vjp
def _mm3(a, b):
    return _dg3(a, b, BNN)


_mm3.defvjp(lambda a, b: (_dg3(a, b, BNN), (a, b)),
            lambda res, g: (_dg3(g, res[1], BNT), _dg3(res[0], g, BTN)))


def _xm(x, m, dims):
    mb = m.astype(BF16)
    parts = _split3(x)
    return _dg(parts[0], mb, dims) + (_dg(parts[1], mb, dims) + _dg(parts[2], mb, dims))


def _mx(m, x, dims):
    mb = m.astype(BF16)
    parts = _split3(x)
    return _dg(mb, parts[0], dims) + (_dg(mb, parts[1], dims) + _dg(mb, parts[2], dims))


@jax.custom_vjp
def _times_const(x, m):
    return _xm(x, m, NN)


_times_const.defvjp(lambda x, m: (_xm(x, m, NN), m),
                    lambda m, g: (_xm(g, m, NT), jnp.zeros_like(m)))


@jax.custom_vjp
def _const_times(m, x):
    return _mx(m, x, NN)


_const_times.defvjp(lambda m, x: (_mx(m, x, NN), m),
                    lambda m, g: (jnp.zeros_like(m), _mx(m, g, TN)))


@jax.custom_vjp
def _lane_mean_cols(x, avg):
    return _mx(avg, x, BNT)


_lane_mean_cols.defvjp(lambda x, avg: (_mx(avg, x, BNT), avg),
                       lambda avg, g: (_xm(g, avg, BTN), jnp.zeros_like(avg)))


def _gdn_gates(ab, alog, dtb, e_g, e_b):
    t = ab.shape[0]
    g = -jnp.exp(alog) * _softplus(ab + dtb)
    beta = _sigmoid(ab)
    ri = lax.broadcasted_iota(jnp.int32, (t, t), 0)
    ci = lax.broadcasted_iota(jnp.int32, (t, t), 1)
    shift = CHUNK.bit_length() - 1
    same = jnp.right_shift(ri, shift) == jnp.right_shift(ci, shift)
    tril = jnp.where(same & (ri >= ci), 1.0, 0.0).astype(F32)
    gc = _const_times(tril, g)
    return _times_const(gc, e_g), _times_const(beta, e_b)


def _shift_down(x, halo, s, i):
    if s == 0:
        return x
    halo = jnp.where(i == 0, 0.0, halo)
    xr = pltpu.roll(x, s, 0)
    hr = pltpu.roll(halo, s, 0)
    row = lax.broadcasted_iota(jnp.int32, (SUBLANES, x.shape[1]), 0)
    top = jnp.where(row < s, hr, xr[:SUBLANES])
    return jnp.concatenate([top, xr[SUBLANES:]], axis=0)


def _shift_up(x, halo, s, i, n):
    if s == 0:
        return x
    t = x.shape[0]
    halo = jnp.where(i == n - 1, 0.0, halo)
    xr = pltpu.roll(x, t - s, 0)
    hr = pltpu.roll(halo, SUBLANES - s, 0)
    row = lax.broadcasted_iota(jnp.int32, (SUBLANES, x.shape[1]), 0)
    bot = jnp.where(row >= SUBLANES - s, hr, xr[t - SUBLANES:])
    return jnp.concatenate([xr[:t - SUBLANES], bot], axis=0)


def _conv(pa, prev, cw_ref, i):
    y = pa * cw_ref[CONV_K - 1:CONV_K, :]
    for j in range(CONV_K - 1):
        y = y + _shift_down(pa, prev, CONV_K - 1 - j, i) * cw_ref[j:j + 1, :]
    return y


def _dot_nt(a, b, precision=None):
    return lax.dot_general(a, b, (((1,), (1,)), ((), ())), precision=precision, preferred_element_type=F32)


def _dot_tn(a, b, precision=None):
    return lax.dot_general(a, b, (((0,), (0,)), ((), ())), precision=precision, preferred_element_type=F32)


def _dot(a, b, precision=None):
    return jnp.dot(a, b, precision=precision, preferred_element_type=F32)


def _bf(x):
    return x.astype(BF16)


def _neumann_inverse(lmat):
    nb, c, _ = lmat.shape
    ri = lax.broadcasted_iota(jnp.int32, (nb, c, c), 1)
    ci = lax.broadcasted_iota(jnp.int32, (nb, c, c), 2)
    pw = -lmat
    inv = jnp.where(ri == ci, 1.0, 0.0).astype(F32) + pw
    for _ in range(5):
        pw = _mm3(pw, pw)
        inv = inv + _mm3(inv, pw)
    return inv


@jax.custom_vjp
def _unit_lower_inverse(lmat):
    return _neumann_inverse(lmat)


def _unit_lower_inverse_fwd(lmat):
    inv = _neumann_inverse(lmat)
    return inv, inv


def _unit_lower_inverse_bwd(inv, g):
    return (-_dg3(_dg3(inv, g, BTN), inv, BNT),)


_unit_lower_inverse.defvjp(_unit_lower_inverse_fwd, _unit_lower_inverse_bwd)


def _gdn_chunk(q, k, v, gc, bb):
    nb, c, _ = q.shape
    ri = lax.broadcasted_iota(jnp.int32, (nb, c, c), 1)
    ci = lax.broadcasted_iota(jnp.int32, (nb, c, c), 2)
    incl = ri >= ci
    strict = ri > ci
    g_row = gc[:, :, :c]
    g_col = _lane_mean_cols(gc, jnp.full((nb, c, LANES), 1.0 / LANES, F32))
    decay = jnp.where(incl, jnp.exp(jnp.where(incl, g_row - g_col, 0.0)), 0.0)
    kb = k * bb
    lmat = jnp.where(strict, _dg(_bf(kb), _bf(k), BNT) * decay, 0.0)
    inv = _unit_lower_inverse(lmat)
    egc = jnp.exp(gc)
    u = _mm3(inv, v * bb)
    w = _mm3(inv, kb * egc)
    aqk = _dg(_bf(q), _bf(k), BNT) * decay
    last = lax.broadcasted_iota(jnp.int32, (nb, c, LANES), 1) == c - 1
    tot = jnp.sum(jnp.where(last, gc, 0.0), axis=1, keepdims=True)
    k_tail = k * jnp.exp(tot - gc)
    tail = jnp.broadcast_to(jnp.exp(tot), (nb, SUBLANES, LANES))
    return u, w, aqk, q * egc, k_tail, tail


def _gdn_intra(qn, kn, vv, g_b, beta_b):
    t_rows = qn.shape[0]
    nc = t_rows // CHUNK
    cb = min(GDN_GROUP, nc)
    rows = cb * CHUNK
    col = pl.BlockSpec((rows, HEAD_DIM), lambda h, b: (b, h))

    def body(q_ref, k_ref, v_ref, g_ref, b_ref, u_ref, w_ref, a_ref, qd_ref, kt_ref, tl_ref):
        def group(gi, carry):
            r = pl.ds(pl.multiple_of(gi * (grp * CHUNK), grp * CHUNK), grp * CHUNK)
            ins = [ref[r, :].reshape(grp, CHUNK, HEAD_DIM) for ref in (q_ref, k_ref, v_ref, g_ref, b_ref)]
            u, w, aqk, qd, kt, tl = _gdn_chunk(*ins)
            for ref, val in ((u_ref, u), (w_ref, w), (qd_ref, qd), (kt_ref, kt)):
                ref[r, :] = val.reshape(grp * CHUNK, HEAD_DIM)
            a_ref[0, r, :] = aqk.reshape(grp * CHUNK, CHUNK)
            tl_ref[0, pl.ds(gi * grp, grp)] = tl
            return carry

        grp = min(GDN_GROUP, cb)
        lax.fori_loop(0, cb // grp, group, 0)

    full = jax.ShapeDtypeStruct((t_rows, HW), F32)
    return pl.pallas_call(
        body, name="gdn_intra_fwd",
        out_shape=[full, full, jax.ShapeDtypeStruct((HEADS, t_rows, CHUNK), F32), full, full,
                   jax.ShapeDtypeStruct((HEADS, nc, SUBLANES, LANES), F32)],
        grid=(HEADS, nc // cb),
        in_specs=[col] * 5,
        out_specs=[col, col, pl.BlockSpec((1, rows, CHUNK), lambda h, b: (h, b, 0)), col, col,
                   pl.BlockSpec((1, cb, SUBLANES, LANES), lambda h, b: (h, b, 0, 0))],
        compiler_params=_params(("parallel", "parallel")),
    )(qn, kn, vv, g_b, beta_b)


def _gdn_intra_bwd(qn, kn, vv, g_b, beta_b, du, dw, da, dqd, dkt, dtl):
    t_rows = qn.shape[0]
    nc = t_rows // CHUNK
    cb = min(GDN_GROUP, nc)
    rows = cb * CHUNK
    col = pl.BlockSpec((rows, HEAD_DIM), lambda h, b: (b, h))
    a_spec = pl.BlockSpec((1, rows, CHUNK), lambda h, b: (h, b, 0))
    tl_spec = pl.BlockSpec((1, cb, SUBLANES, LANES), lambda h, b: (h, b, 0, 0))

    def body(q_ref, k_ref, v_ref, g_ref, b_ref, du_ref, dw_ref, da_ref, dqd_ref, dkt_ref, dtl_ref,
             dq_ref, dk_ref, dv_ref, dg_ref, db_ref):
        def group(gi, carry):
            r = pl.ds(pl.multiple_of(gi * (grp * CHUNK), grp * CHUNK), grp * CHUNK)
            wide = (grp, CHUNK, HEAD_DIM)
            ins = [ref[r, :].reshape(wide) for ref in (q_ref, k_ref, v_ref, g_ref, b_ref)]
            cts = (du_ref[r, :].reshape(wide), dw_ref[r, :].reshape(wide),
                   da_ref[0, r, :].reshape(grp, CHUNK, CHUNK), dqd_ref[r, :].reshape(wide),
                   dkt_ref[r, :].reshape(wide), dtl_ref[0, pl.ds(gi * grp, grp)])
            grads = jax.vjp(_gdn_chunk, *ins)[1](cts)
            for ref, val in zip((dq_ref, dk_ref, dv_ref, dg_ref, db_ref), grads):
                ref[r, :] = val.reshape(grp * CHUNK, HEAD_DIM)
            return carry

        grp = min(GDN_GROUP, cb)
        lax.fori_loop(0, cb // grp, group, 0)

    full = jax.ShapeDtypeStruct((t_rows, HW), F32)
    return pl.pallas_call(
        body, name="gdn_intra_bwd",
        out_shape=[full] * 5,
        grid=(HEADS, nc // cb),
        in_specs=[col] * 7 + [a_spec, col, col, tl_spec],
        out_specs=[col] * 5,
        compiler_params=_params(("parallel", "parallel")),
    )(qn, kn, vv, g_b, beta_b, du, dw, da, dqd, dkt, dtl)


def _head_cols(h):
    return slice(h * HEAD_DIM, (h + 1) * HEAD_DIM)


def _gdn_scan(u, w, aqk, qd, kt, tl):
    t_rows = u.shape[0]
    nc = t_rows // CHUNK
    cb = min(GDN_CB, nc)
    rows = cb * CHUNK
    wide = pl.BlockSpec((rows, HW), lambda b: (b, 0))

    def body(u_ref, w_ref, a_ref, qd_ref, kt_ref, tl_ref, o_ref, s_out_ref, s_ref):
        @pl.when(pl.program_id(0) == 0)
        def _():
            s_ref[...] = jnp.zeros_like(s_ref)

        def chunk(ci, carry):
            r = pl.ds(pl.multiple_of(ci * CHUNK, CHUNK), CHUNK)
            for h in range(HEADS):
                hc = _head_cols(h)
                s = s_ref[h]
                s_out_ref[ci, h] = s
                sb = _bf(s)
                vn = u_ref[r, hc] - _dot(_bf(w_ref[r, hc]), sb)
                vnb = _bf(vn)
                o_ref[r, hc] = _dot(_bf(qd_ref[r, hc]), sb) + _dot(_bf(a_ref[h, r, :]), vnb)
                s_ref[h] = s * tl_ref[h, ci, 0:1, :] + _dot_tn(_bf(kt_ref[r, hc]), vnb)
            return carry

        lax.fori_loop(0, cb, chunk, 0, unroll=GDN_SCAN_UNROLL)

    return pl.pallas_call(
        body, name="gdn_scan_fwd",
        out_shape=[jax.ShapeDtypeStruct((t_rows, HW), F32),
                   jax.ShapeDtypeStruct((nc, HEADS, HEAD_DIM, HEAD_DIM), F32)],
        grid=(nc // cb,),
        in_specs=[wide, wide, pl.BlockSpec((HEADS, rows, CHUNK), lambda b: (0, b, 0)), wide, wide,
                  pl.BlockSpec((HEADS, cb, SUBLANES, LANES), lambda b: (0, b, 0, 0))],
        out_specs=[wide, pl.BlockSpec((cb, HEADS, HEAD_DIM, HEAD_DIM), lambda b: (b, 0, 0, 0))],
        scratch_shapes=[pltpu.VMEM((HEADS, HEAD_DIM, HEAD_DIM), F32)],
        compiler_params=_params(("arbitrary",)),
    )(u, w, aqk, qd, kt, tl)


def _gdn_scan_bwd(do, u, w, aqk, qd, kt, tl, states):
    t_rows = u.shape[0]
    nc = t_rows // CHUNK
    cb = min(GDN_CB, nc)
    rows = cb * CHUNK
    nb = nc // cb
    wide = pl.BlockSpec((rows, HW), lambda b: (nb - 1 - b, 0))
    a_spec = pl.BlockSpec((HEADS, rows, CHUNK), lambda b: (0, nb - 1 - b, 0))
    tl_spec = pl.BlockSpec((HEADS, cb, SUBLANES, LANES), lambda b: (0, nb - 1 - b, 0, 0))

    def body(do_ref, u_ref, w_ref, a_ref, qd_ref, kt_ref, tl_ref, s_in_ref,
             du_ref, dw_ref, da_ref, dqd_ref, dkt_ref, dtl_ref, ds_ref):
        @pl.when(pl.program_id(0) == 0)
        def _():
            ds_ref[...] = jnp.zeros_like(ds_ref)

        row0 = lax.broadcasted_iota(jnp.int32, (SUBLANES, LANES), 0) == 0

        def chunk(step, carry):
            ci = cb - 1 - step
            r = pl.ds(pl.multiple_of(ci * CHUNK, CHUNK), CHUNK)
            for h in range(HEADS):
                hc = _head_cols(h)
                s = s_in_ref[ci, h]
                ds_next = ds_ref[h]
                sb, dsb = _bf(s), _bf(ds_next)
                wb, ab, ktb, qdb = _bf(w_ref[r, hc]), _bf(a_ref[h, r, :]), _bf(kt_ref[r, hc]), _bf(qd_ref[r, hc])
                dob = _bf(do_ref[r, hc])
                vn = u_ref[r, hc] - _dot(wb, sb)
                vnb = _bf(vn)
                dvn = _dot_tn(ab, dob) + _dot(ktb, dsb)
                dvnb = _bf(dvn)
                du_ref[r, hc] = dvn
                dw_ref[r, hc] = -_dot_nt(dvnb, sb)
                da_ref[h, r, :] = _dot_nt(dob, vnb)
                dqd_ref[r, hc] = _dot_nt(dob, sb)
                dkt_ref[r, hc] = _dot_nt(vnb, dsb)
                dtl_ref[h, ci] = jnp.where(row0, _colsum(s * ds_next), 0.0)
                ds_ref[h] = _dot_tn(qdb, dob) + ds_next * tl_ref[h, ci, 0:1, :] - _dot_tn(wb, dvnb)
            return carry

        lax.fori_loop(0, cb, chunk, 0, unroll=GDN_SCAN_UNROLL)

    full = jax.ShapeDtypeStruct((t_rows, HW), F32)
    return pl.pallas_call(
        body, name="gdn_scan_bwd",
        out_shape=[full, full, jax.ShapeDtypeStruct((HEADS, t_rows, CHUNK), F32), full, full,
                   jax.ShapeDtypeStruct((HEADS, nc, SUBLANES, LANES), F32)],
        grid=(nb,),
        in_specs=[wide, wide, wide, a_spec, wide, wide, tl_spec,
                  pl.BlockSpec((cb, HEADS, HEAD_DIM, HEAD_DIM), lambda b: (nb - 1 - b, 0, 0, 0))],
        out_specs=[wide, wide, a_spec, wide, wide, tl_spec],
        scratch_shapes=[pltpu.VMEM((HEADS, HEAD_DIM, HEAD_DIM), F32)],
        compiler_params=_params(("arbitrary",)),
    )(do, u, w, aqk, qd, kt, tl, states)


def _att_profile_index():
    j = lax.broadcasted_iota(jnp.int32, (SUBLANES, ATT_KW), 1)
    return jnp.clip(ATT_PAD - j, -(CHUNK - 1), MAX_REL) + (CHUNK - 1)


def _att_far_back():
    qi = lax.broadcasted_iota(jnp.int32, (ATT_QB, ATT_KW), 0)
    kj = lax.broadcasted_iota(jnp.int32, (ATT_QB, ATT_KW), 1)
    return kj < qi


def _rotate_rows(x, forward):
    rows, lanes = x.shape
    row = lax.broadcasted_iota(jnp.int32, x.shape, 0)
    for bit in range(rows.bit_length() - 1):
        amount = (1 << bit) if forward else lanes - (1 << bit)
        x = jnp.where(jnp.bitwise_and(jnp.right_shift(row, bit), 1) == 1, pltpu.roll(x, amount, 1), x)
    return x


def _att_in_band():
    qi = lax.broadcasted_iota(jnp.int32, (ATT_QB, ATT_KW), 0)
    kj = lax.broadcasted_iota(jnp.int32, (ATT_QB, ATT_KW), 1)
    shift = CHUNK.bit_length() - 1
    qc = jnp.right_shift(qi, shift)
    kc = jnp.right_shift(kj, shift) - LEFT_CHUNKS
    return (kc <= qc) & (kc >= qc - LEFT_CHUNKS)


def _att_valid(b):
    kj = lax.broadcasted_iota(jnp.int32, (1, ATT_KW), 1)
    return jnp.where(kj + b * ATT_QB >= ATT_PAD, 0.0, NEG_INF)


def _rms_parts(x, w):
    r = lax.rsqrt(jnp.mean(x * x, axis=-1, keepdims=True) + EPS)
    xn = x * r
    return xn * w, xn, r


def _rms_bwd(dy, xn, r, w):
    dxn = dy * w
    dx = r * (dxn - xn * jnp.mean(dxn * xn, axis=-1, keepdims=True))
    return dx, _colsum(dy * xn)


def _att_probs(qb, kb, bias, before_start):
    s = _dot_nt(qb, kb) * (HEAD_DIM ** -0.5) + bias + before_start
    e = jnp.exp(s - jnp.max(s, axis=-1, keepdims=True))
    return e * (1.0 / jnp.sum(e, axis=-1, keepdims=True))


def _att_specs():
    q_spec = pl.BlockSpec((ATT_QB, HEAD_DIM), lambda h, b: (b, h))
    back = ATT_PAD // ATT_QB
    k_specs = [pl.BlockSpec((ATT_QB, HEAD_DIM), lambda h, b, j=j: (jnp.maximum(b + j - back, 0), HEADS + h))
               for j in range(3)]
    v_specs = [pl.BlockSpec((ATT_QB, HEAD_DIM), lambda h, b, j=j: (jnp.maximum(b + j - back, 0), 2 * HEADS + h))
               for j in range(3)]
    w_spec = pl.BlockSpec((1, HEAD_DIM), lambda h, b: (0, 0))
    smem = pl.BlockSpec(memory_space=pltpu.SMEM)
    return q_spec, k_specs, v_specs, w_spec, smem


BIAS_SPEC = pl.BlockSpec((1, ATT_QB, ATT_KW), lambda h, b: (h, 0, 0))


def _expand_rel_bias(rel):
    def body(rel_ref, bias_ref):
        h = pl.program_id(0)
        idx = _att_profile_index()

        def fill(r, acc):
            return jnp.where(idx == r, rel_ref[h, r], acc)

        profile = lax.fori_loop(0, N_REL, fill, jnp.zeros((SUBLANES, ATT_KW), F32))
        table = _rotate_rows(jnp.concatenate([profile] * (ATT_QB // SUBLANES), axis=0), True)
        table = jnp.where(_att_far_back(), rel_ref[h, N_REL - 1], table)
        bias_ref[0] = jnp.where(_att_in_band(), table, NEG_INF)

    return pl.pallas_call(
        body, name="rel_bias_expand",
        out_shape=jax.ShapeDtypeStruct((HEADS, ATT_QB, ATT_KW), F32), grid=(HEADS,),
        in_specs=[pl.BlockSpec(memory_space=pltpu.SMEM)],
        out_specs=pl.BlockSpec((1, ATT_QB, ATT_KW), lambda h: (h, 0, 0)),
        compiler_params=_params(("parallel",)),
    )(rel)


def _attention(pb, qw, kw, bias):
    t_rows = pb.shape[0]
    q_spec, k_specs, v_specs, w_spec, _ = _att_specs()

    def body(q_ref, k0, k1, k2, v0, v1, v2, qw_ref, kw_ref, bias_ref, o_ref):
        b = pl.program_id(1)
        kwin = jnp.concatenate([k0[...], k1[...], k2[...]], axis=0)
        vwin = jnp.concatenate([v0[...], v1[...], v2[...]], axis=0)
        q = _rms(q_ref[...], qw_ref[...])
        k = _rms(kwin, kw_ref[...])
        p = _att_probs(_bf(q), _bf(k), bias_ref[0], _att_valid(b))
        o_ref[...] = _dot(_bf(p), _bf(vwin)).astype(o_ref.dtype)

    return pl.pallas_call(
        body, name="band_attention_fwd",
        out_shape=jax.ShapeDtypeStruct((t_rows, HW), BF16),
        grid=(HEADS, t_rows // ATT_QB),
        in_specs=[q_spec] + k_specs + v_specs + [w_spec, w_spec, BIAS_SPEC],
        out_specs=pl.BlockSpec((ATT_QB, HEAD_DIM), lambda h, b: (b, h)),
        compiler_params=_params(("parallel", "arbitrary")),
    )(pb, pb, pb, pb, pb, pb, pb, qw, kw, bias)


def _attention_bwd(pb, qw, kw, bias, dyb):
    t_rows = pb.shape[0]
    nb = t_rows // ATT_QB
    q_spec, k_specs, v_specs, w_spec, smem = _att_specs()
    pad_rows = t_rows + ATT_PAD
    acc_spec = pl.BlockSpec((pad_rows, HEAD_DIM), lambda h, b: (0, h))

    def body(q_ref, k0, k1, k2, v0, v1, v2, qw_ref, kw_ref, bias_ref, do_ref,
             dq_ref, dk_ref, dv_ref, dqw_ref, dkw_ref, drel_ref, dbias_ref):
        h, b = pl.program_id(0), pl.program_id(1)

        @pl.when(b == 0)
        def _():
            dbias_ref[...] = jnp.zeros_like(dbias_ref)
            dk_ref[...] = jnp.zeros_like(dk_ref)
            dv_ref[...] = jnp.zeros_like(dv_ref)

        @pl.when((b == 0) & (h == 0))
        def _():
            dqw_ref[...] = jnp.zeros_like(dqw_ref)
            dkw_ref[...] = jnp.zeros_like(dkw_ref)

        kwin = jnp.concatenate([k0[...], k1[...], k2[...]], axis=0)
        vwin = jnp.concatenate([v0[...], v1[...], v2[...]], axis=0)
        scale = HEAD_DIM ** -0.5
        qw_, kw_ = qw_ref[...], kw_ref[...]
        q, qn, rq = _rms_parts(q_ref[...], qw_)
        k, kn, rk = _rms_parts(kwin, kw_)
        qb, kb, dob = _bf(q), _bf(k), _bf(do_ref[...])
        p = _att_probs(qb, kb, bias_ref[0], _att_valid(b))
        dp = _dot_nt(dob, _bf(vwin))
        ds = p * (dp - jnp.sum(p * dp, axis=-1, keepdims=True))
        dbias_ref[...] += ds
        ds = _bf(ds)
        dq, dqw = _rms_bwd(_dot(ds, kb) * scale, qn, rq, qw_)
        dk, dkw = _rms_bwd(_dot_tn(ds, qb) * scale, kn, rk, kw_)
        dq_ref[...] = dq.astype(dq_ref.dtype)
        win = pl.ds(pl.multiple_of(b * ATT_QB, ATT_QB), ATT_KW)
        dk_ref[win, :] += dk
        dv_ref[win, :] += _dot_tn(_bf(p), dob)
        dqw_ref[...] += dqw
        dkw_ref[...] += dkw

        @pl.when(b == nb - 1)
        def _():
            tot, far = dbias_ref[...], _att_far_back()
            far_sum = jnp.sum(jnp.where(far, tot, 0.0))
            per_offset = _colsum(_rotate_rows(jnp.where(far, 0.0, tot), False))
            idx = _att_profile_index()
            first_row = lax.broadcasted_iota(jnp.int32, idx.shape, 0) == 0
            spread = jnp.where(first_row, per_offset, 0.0)

            def reduce(r, carry):
                drel_ref[h, r] = jnp.sum(jnp.where(idx == r, spread, 0.0)) + jnp.where(r == N_REL - 1, far_sum, 0.0)
                return carry

            lax.fori_loop(0, N_REL, reduce, 0)

    return pl.pallas_call(
        body, name="band_attention_bwd",
        out_shape=[jax.ShapeDtypeStruct((t_rows, HW), BF16),
                   jax.ShapeDtypeStruct((pad_rows, HW), F32), jax.ShapeDtypeStruct((pad_rows, HW), F32),
                   jax.ShapeDtypeStruct((1, HEAD_DIM), F32), jax.ShapeDtypeStruct((1, HEAD_DIM), F32),
                   jax.ShapeDtypeStruct((HEADS, N_REL), F32)],
        grid=(HEADS, nb),
        in_specs=[q_spec] + k_specs + v_specs + [w_spec, w_spec, BIAS_SPEC, q_spec],
        out_specs=[q_spec, acc_spec, acc_spec, w_spec, w_spec, smem],
        scratch_shapes=[pltpu.VMEM((ATT_QB, ATT_KW), F32)],
        compiler_params=_params(("arbitrary", "arbitrary")),
    )(pb, pb, pb, pb, pb, pb, pb, qw, kw, bias, dyb)


def _me():
    return lax.axis_index("x"), lax.axis_index("y"), lax.axis_index("c")


def _index(x, y, c):
    return 4 * x + 2 * y + c


HBM_SPEC = pl.BlockSpec(memory_space=pl.ANY)


def _block(ref, kind, d, r, c):
    if kind == "all":
        return ref
    if kind == "rows":
        return ref.at[pl.ds(d * r, r), :]
    if kind == "win":
        return ref.at[:, pl.ds(d * WIN_STEP, c)]
    return ref.at[:, pl.ds(d * c, c)]


def _all_gather(shards, kinds, n_gather):
    n = len(shards)

    def body(*refs):
        x_refs, out_refs = refs[:n], refs[n:2 * n]
        send_sems, recv_sems, local_sems = refs[2 * n:]
        x, y, c = _me()
        me, sibling = (x, y, c), (x, y, 1 - c)
        chips = [(1 - x, y), (x, 1 - y), (1 - x, 1 - y)]

        def copy(i, k, blk, to, src=None):
            r_, c_ = shards[i].shape
            dst = _block(out_refs[i], kinds[i], _index(*blk), r_, c_)
            return pltpu.make_async_remote_copy(
                src_ref=dst if src is None else src, dst_ref=dst,
                send_sem=send_sems.at[i, k], recv_sem=recv_sems.at[i, k], device_id=to, device_id_type=MESH)

        sends, local = [], []
        for i in range(n):
            r_, c_ = shards[i].shape
            mine = pltpu.make_async_copy(x_refs[i], _block(out_refs[i], kinds[i], _index(*me), r_, c_),
                                         local_sems.at[i])
            mine.start()
            local.append(mine)
            if i >= n_gather:
                continue
            first = [copy(i, 0, me, sibling, src=x_refs[i])]
            first += [copy(i, 1 + j, me, (*chip, c), src=x_refs[i]) for j, chip in enumerate(chips)]
            for cp in first:
                cp.start()
            sends += first
        for i in range(n_gather):
            for j, chip in enumerate(chips):
                copy(i, 1 + j, (*chip, c), me).wait_recv()
                passed = copy(i, 4 + j, (*chip, c), sibling)
                passed.start()
                sends.append(passed)
        for i in range(n_gather):
            copy(i, 0, sibling, me).wait_recv()
            for j, chip in enumerate(chips):
                copy(i, 4 + j, (*chip, 1 - c), me).wait_recv()
        for cp in sends:
            cp.wait_send()
        for cp in local:
            cp.wait()

    def full_shape(s, kind):
        r_, c_ = s.shape
        return (N_DEV * r_, c_) if kind == "rows" else (r_, N_DEV * c_)

    return pl.pallas_call(
        body, name="weights_all_gather",
        out_shape=[jax.ShapeDtypeStruct(full_shape(s, k), s.dtype) for s, k in zip(shards, kinds)],
        in_specs=[HBM_SPEC] * n, out_specs=[HBM_SPEC] * n,
        scratch_shapes=[pltpu.SemaphoreType.DMA((n_gather, 7)), pltpu.SemaphoreType.DMA((n_gather, 7)),
                        pltpu.SemaphoreType.DMA((n,))],
        compiler_params=pltpu.CompilerParams(has_side_effects=True),
    )(*shards)


SEM_SPEC = pl.BlockSpec(memory_space=pltpu.SEMAPHORE)
HBM_ONLY = pl.BlockSpec(memory_space=pltpu.HBM)
DATAFLOW = pltpu.SideEffectType.DATAFLOW_SIDE_EFFECTING


def _peers():
    x, y, c = _me()
    return [(x ^ (k >> 2), y ^ ((k >> 1) & 1), c ^ (k & 1)) for k in range(1, N_DEV)]


def _gather_copies(shapes, kinds):
    def make(src_refs, land_refs, send_sems, recv_sems):
        mine = _index(*_me())
        return [pltpu.make_async_remote_copy(
            src_ref=src_refs[i], dst_ref=_block(land_refs[i], kind, mine, r, c),
            send_sem=send_sems.at[7 * i + k], recv_sem=recv_sems.at[7 * i + k], device_id=peer, device_id_type=MESH)
            for i, ((r, c), kind) in enumerate(zip(shapes, kinds)) for k, peer in enumerate(_peers())]

    return make


def _exchange_copies(shapes, kinds):
    def make(src_refs, land_refs, send_sems, recv_sems):
        mine = _index(*_me())
        return [pltpu.make_async_remote_copy(
            src_ref=_block(src_refs[i], kind, _index(*peer), r, c), dst_ref=land_refs[i].at[mine],
            send_sem=send_sems.at[7 * i + k], recv_sem=recv_sems.at[7 * i + k], device_id=peer, device_id_type=MESH)
            for i, ((r, c), kind) in enumerate(zip(shapes, kinds)) for k, peer in enumerate(_peers())]

    return make


def _place_block(shard, kind, name):
    r, c = shard.shape
    tile = _row_tile(r, c)
    nt = r // tile
    full = (N_DEV * r, c) if kind == "rows" else (r, N_DEV * c)

    def body(me_ref, x_ref, out_ref):
        out_ref[...] = x_ref[...]

    if kind == "rows":
        out_spec = pl.BlockSpec((tile, c), lambda i, me: (me[0] * nt + i, 0))
    else:
        out_spec = pl.BlockSpec((tile, c), lambda i, me: (i, me[0]))
    return pl.pallas_call(
        body, name=name, out_shape=jax.ShapeDtypeStruct(full, shard.dtype),
        grid_spec=pltpu.PrefetchScalarGridSpec(
            num_scalar_prefetch=1, grid=(nt,),
            in_specs=[pl.BlockSpec((tile, c), lambda i, me: (i, 0))], out_specs=out_spec),
        compiler_params=_params(("arbitrary",)),
    )(_my_index_operand(), shard)


def _split_start(srcs, lands, make, name):
    n = len(srcs)

    def body(*refs):
        send_sems, recv_sems = refs[2 * n], refs[2 * n + 1]
        for cp in make(refs[:n], refs[n:2 * n], send_sems, recv_sems):
            cp.start()
        refs[-1][...] = jnp.zeros_like(refs[-1])

    arrays = list(srcs) + list(lands)
    out = pl.pallas_call(
        body, name=name,
        out_shape=(pltpu.SemaphoreType.DMA((7 * n,)), pltpu.SemaphoreType.DMA((7 * n,)),
                   *[pltpu.HBM(a.shape, a.dtype) for a in arrays], jax.ShapeDtypeStruct((SUBLANES, LANES), F32)),
        in_specs=[HBM_ONLY] * (2 * n),
        out_specs=(SEM_SPEC, SEM_SPEC, *[HBM_ONLY] * (2 * n), pl.BlockSpec(memory_space=pltpu.VMEM)),
        input_output_aliases={i: 2 + i for i in range(2 * n)},
        compiler_params=pltpu.CompilerParams(has_side_effects=DATAFLOW),
    )(*[pltpu.with_memory_space_constraint(a, pltpu.HBM) for a in arrays])
    return out[0], out[1], list(out[2:2 + n]), list(out[2 + n:2 + 2 * n]), out[-1]


def _split_wait(send_sems, recv_sems, srcs, lands, after, make, name):
    n = len(srcs)

    def body(*refs):
        for cp in make(refs[:n], refs[n:2 * n], refs[2 * n], refs[2 * n + 1]):
            cp.wait_send()
            cp.wait_recv()

    arrays = list(srcs) + list(lands)
    out = pl.pallas_call(
        body, name=name,
        out_shape=tuple(pltpu.HBM(a.shape, a.dtype) for a in arrays),
        in_specs=[HBM_ONLY] * (2 * n) + [SEM_SPEC, SEM_SPEC, pl.BlockSpec(memory_space=pl.ANY)],
        out_specs=tuple([HBM_ONLY] * (2 * n)),
        input_output_aliases={i: i for i in range(2 * n)},
        compiler_params=pltpu.CompilerParams(has_side_effects=DATAFLOW),
    )(*arrays, send_sems, recv_sems, after)
    return list(out[:n]), list(out[n:])


def _all_reduce_small(vals, name):
    rows, width = vals.shape

    def body(x_ref, out_ref, buf_ref, send_sems, recv_sems):
        x, y, c = _me()
        mine = _index(x, y, c)
        buf_ref[mine] = x_ref[...]
        copies = []
        for k in range(1, N_DEV):
            px, py, pc = x ^ (k >> 2), y ^ ((k >> 1) & 1), c ^ (k & 1)
            copies.append(pltpu.make_async_remote_copy(
                src_ref=x_ref, dst_ref=buf_ref.at[mine],
                send_sem=send_sems.at[k - 1], recv_sem=recv_sems.at[k - 1],
                device_id=(px, py, pc), device_id_type=MESH))
        for cp in copies:
            cp.start()
        for cp in copies:
            cp.wait()
        acc = buf_ref[0]
        for j in range(1, N_DEV):
            acc = acc + buf_ref[j]
        out_ref[...] = acc

    vmem = pl.BlockSpec(memory_space=pltpu.VMEM)
    return pl.pallas_call(
        body, name=name,
        out_shape=jax.ShapeDtypeStruct(vals.shape, F32),
        in_specs=[vmem], out_specs=vmem,
        scratch_shapes=[pltpu.VMEM((N_DEV, rows, width), F32),
                        pltpu.SemaphoreType.DMA((7,)), pltpu.SemaphoreType.DMA((7,))],
        compiler_params=pltpu.CompilerParams(has_side_effects=True),
    )(vals)


def _adamw_math(w, g, m, v):
    m = ADAM_B1 * m + (1.0 - ADAM_B1) * g
    v = ADAM_B2 * v + (1.0 - ADAM_B2) * (g * g)
    m_hat = m / (1.0 - ADAM_B1 ** ADAM_STEP)
    v_hat = v / (1.0 - ADAM_B2 ** ADAM_STEP)
    delta = -ADAM_LR * (m_hat / (jnp.sqrt(v_hat) + ADAM_EPS) + ADAM_WD * w)
    return delta, m, v


ROW_TILE_ELEMS = 384 * 1024


def _row_tile(rows, width):
    best = SUBLANES
    for t in range(SUBLANES, rows + 1, SUBLANES):
        if rows % t == 0 and t * width <= ROW_TILE_ELEMS:
            best = t
    return best


def _sum_received(r_ref, own, me):
    g = None
    for j in range(N_DEV):
        term = jnp.where(me == j, own, r_ref[j].astype(F32))
        g = term if g is None else g + term
    return g


def _my_index_operand():
    return _index(*_me()).astype(jnp.int32).reshape(1)


def _sum_small(recv, own):
    def body(me_ref, r_ref, own_ref, out_ref):
        out_ref[...] = _sum_received(r_ref, own_ref[...], me_ref[0])

    whole = lambda shape: pl.BlockSpec(shape, lambda i, me, nd=len(shape): (0,) * nd)
    return pl.pallas_call(
        body, name="small_grads_sum", out_shape=jax.ShapeDtypeStruct(own.shape, F32),
        grid_spec=pltpu.PrefetchScalarGridSpec(
            num_scalar_prefetch=1, grid=(1,), in_specs=[whole(recv.shape), whole(own.shape)],
            out_specs=whole(own.shape)),
        compiler_params=_params(("arbitrary",)),
    )(_my_index_operand(), recv, own)


def _adamw_recv(recv, grad, kind, w, m, v, name):
    _, rows, width = recv.shape
    tile = _row_tile(rows, width)
    nt = rows // tile

    def body(me_ref, r_ref, own_ref, w_ref, m_ref, v_ref, g_out, d_out, m_out, v_out):
        g = _sum_received(r_ref, own_ref[...].astype(F32), me_ref[0])
        d, mn, vn = _adamw_math(w_ref[...], g, m_ref[...], v_ref[...])
        g_out[...] = g
        d_out[...] = d
        m_out[...] = mn
        v_out[...] = vn

    if kind == "rows":
        own_spec = pl.BlockSpec((tile, width), lambda i, me: (me[0] * nt + i, 0))
    else:
        own_spec = pl.BlockSpec((tile, width), lambda i, me: (i, me[0]))
    spec = pl.BlockSpec((tile, width), lambda i, me: (i, 0))
    shape = jax.ShapeDtypeStruct((rows, width), F32)
    return pl.pallas_call(
        body, name=name, out_shape=[shape] * 4,
        grid_spec=pltpu.PrefetchScalarGridSpec(
            num_scalar_prefetch=1, grid=(nt,),
            in_specs=[pl.BlockSpec((N_DEV, tile, width), lambda i, me: (0, i, 0)), own_spec, spec, spec, spec],
            out_specs=[spec] * 4),
        compiler_params=_params(("parallel",)),
    )(_my_index_operand(), recv, grad, w, m, v)


WIN_STEP = 1408
WIN_W = 1536
IN_SHARD = IN_COLS // N_DEV
IN_PADDED = WIN_STEP * (N_DEV - 1) + WIN_W


def _roll_w_in(shard_padded):
    rows = shard_padded.shape[0]
    tile = _row_tile(rows, WIN_W)

    def body(x_ref, main_ref, edge_ref):
        win = pltpu.roll(x_ref[...], 2 * _index(*_me()), 1).astype(BF16)
        main_ref[...] = win[:, :WIN_STEP]
        edge_ref[...] = win[:, WIN_STEP:]

    return pl.pallas_call(
        body, name="w_in_window",
        out_shape=[jax.ShapeDtypeStruct((rows, WIN_STEP), BF16), jax.ShapeDtypeStruct((rows, WIN_W - WIN_STEP), BF16)],
        grid=(rows // tile,),
        in_specs=[pl.BlockSpec((tile, WIN_W), lambda i: (i, 0))],
        out_specs=[pl.BlockSpec((tile, WIN_STEP), lambda i: (i, 0)),
                   pl.BlockSpec((tile, WIN_W - WIN_STEP), lambda i: (i, 0))],
        compiler_params=_params(("parallel",)),
    )(shard_padded)


def _sum_w_in_windows(recv, grad):
    _, rows, width = recv.shape
    tile = _row_tile(rows, width)

    def body(me_ref, r_ref, g_ref, g_out, own_ref, sem):
        me = me_ref[0]
        rows_i = pl.ds(pl.multiple_of(pl.program_id(0) * tile, tile), tile)
        own = pltpu.make_async_copy(g_ref.at[rows_i, pl.ds(pl.multiple_of(me * WIN_STEP, LANES), width)], own_ref, sem)
        own.start()
        own.wait()
        g_out[...] = pltpu.roll(_sum_received(r_ref, own_ref[...].astype(F32), me), width - 2 * me, 1)

    return pl.pallas_call(
        body, name="w_in_grad_sum", out_shape=jax.ShapeDtypeStruct((rows, width), F32),
        grid_spec=pltpu.PrefetchScalarGridSpec(
            num_scalar_prefetch=1, grid=(rows // tile,),
            in_specs=[pl.BlockSpec((N_DEV, tile, width), lambda i, me: (0, i, 0)), HBM_SPEC],
            out_specs=pl.BlockSpec((tile, width), lambda i, me: (i, 0)),
            scratch_shapes=[pltpu.VMEM((tile, width), BF16), pltpu.SemaphoreType.DMA]),
        compiler_params=_params(("arbitrary",)),
    )(_my_index_operand(), recv, grad)


def _adamw_small(w, g, m, v, name):
    def fn(i, n, w_, g_, m_, v_):
        return _adamw_math(w_, g_, m_, v_)

    r, c = w.shape
    return _rows(fn, [(w, "t"), (g, "t"), (m, "t"), (v, "t")], [], [(c, F32)] * 3, [], _row_tile(r, c), name)


def _norm_fwd(x, w, name):
    return _rows(lambda i, n, x_, w_: (_rms(x_, w_[...]),), [(x, "t")], [w], [(D_MODEL, BF16)], [], 512, name)[0]


def _residual_norm_fwd(x, y, scale, w, name):
    def fn(i, n, x_, y_, w_):
        xn = x_ + scale * y_
        return xn, _rms(xn, w_[...])

    return _rows(fn, [(x, "t"), (y, "t")], [w], [(D_MODEL, F32), (D_MODEL, BF16)], [], 512, name)


def _residual_norm_bwd(x, w, dhs, dres, scale, name):
    nh = len(dhs)

    def fn(i, n, x_, dres_, *rest):
        dh = rest[0]
        for extra in rest[1:nh]:
            dh = dh + extra
        _, vjp = jax.vjp(_rms, x_, rest[nh][...])
        dx, dw = vjp(dh)
        dx = dx + dres_
        return dx, scale * dx, dw

    return _rows(fn, [(x, "t"), (dres, "t")] + [(d, "t") for d in dhs], [w],
                 [(D_MODEL, F32), (D_MODEL, BF16)], [(1, D_MODEL)], 256, name)


FFN_UP_TN = 512


def _ffn_up(h, w_gu, name, after=None):
    t, d = h.shape
    f = w_gu.shape[1] // 2
    tm = _pick(t, (1024, 512, 256, 128))
    nj = f // FFN_UP_TN
    after_specs, after_args = _after(after)

    def body(h_ref, wg_ref, wu_ref, *rest):
        g_ref, u_ref, act_ref = rest[len(after_args):]
        hb = h_ref[...]
        g = jnp.dot(hb, wg_ref[...], preferred_element_type=F32)
        u = jnp.dot(hb, wu_ref[...], preferred_element_type=F32)
        g_ref[...] = g.astype(BF16)
        u_ref[...] = u.astype(BF16)
        act_ref[...] = (_silu(g) * u).astype(BF16)

    out = pl.BlockSpec((tm, FFN_UP_TN), lambda i, j: (i, j))
    return pl.pallas_call(
        body, name=name, out_shape=[jax.ShapeDtypeStruct((t, f), BF16)] * 3, grid=(t // tm, nj),
        in_specs=[pl.BlockSpec((tm, d), lambda i, j: (i, 0)),
                  pl.BlockSpec((d, FFN_UP_TN), lambda i, j: (0, j)),
                  pl.BlockSpec((d, FFN_UP_TN), lambda i, j: (0, j + nj))] + after_specs,
        out_specs=[out, out, out],
        compiler_params=_params(("parallel", "parallel")),
    )(h, w_gu, w_gu, *after_args)


def _ffn_fwd(h, w_gu, get_w_down, tag, after=None):
    g, u, act = _ffn_up(h, w_gu, tag + "_gu", after)
    y = _matmul(act, get_w_down(act), "nn", F32, tag + "_down")
    return (g, u), act, y


def _ffn_dact(dy, w_down, g, u, name):
    t, d = dy.shape
    f = w_down.shape[0]
    tm = _pick(t, (1024, 512, 256, 128))

    def body(dy_ref, w_ref, g_ref, u_ref, out_ref):
        dact = lax.dot_general(dy_ref[...], w_ref[...], NT, preferred_element_type=F32)
        g_, u_ = g_ref[...].astype(F32), u_ref[...].astype(F32)
        sg = _sigmoid(g_)
        out_ref[0] = (dact * u_ * (sg * (1.0 + g_ * (1.0 - sg)))).astype(BF16)
        out_ref[1] = (dact * (g_ * sg)).astype(BF16)

    tile = pl.BlockSpec((tm, FFN_UP_TN), lambda i, j: (i, j))
    return pl.pallas_call(
        body, name=name, out_shape=jax.ShapeDtypeStruct((2, t, f), BF16), grid=(t // tm, f // FFN_UP_TN),
        in_specs=[pl.BlockSpec((tm, d), lambda i, j: (i, 0)), pl.BlockSpec((FFN_UP_TN, d), lambda i, j: (j, 0)),
                  tile, tile],
        out_specs=pl.BlockSpec((2, tm, FFN_UP_TN), lambda i, j: (0, i, j)),
        compiler_params=_params(("parallel", "parallel")),
    )(dy, w_down, g, u)


def _ffn_bwd(h, gu, act, dy, w_gu, w_down, tag, comm, more=None):
    dgu = _ffn_dact(dy, w_down, gu[0], gu[1], tag + "_dact")
    sent = comm.send(tag + "_gu", {tag + "_w_gu": _matmul(h, dgu, "tn", BF16, tag + "_d_w_gu")})
    sent = sent + comm.send(tag + "_down", {tag + "_w_down": _matmul(act, dy, "tn", BF16, tag + "_d_w_down", sent),
                                            **(more or {})})
    dh = _matmul(dgu, w_gu, "nt", BF16, tag + "_dh")
    return dh, sent


def _expanders():
    e_g = np.zeros((LANES, HW), np.float32)
    e_b = np.zeros((LANES, HW), np.float32)
    for h in range(HEADS):
        e_g[h, h * HEAD_DIM:(h + 1) * HEAD_DIM] = 1.0
        e_b[HEADS + h, h * HEAD_DIM:(h + 1) * HEAD_DIM] = 1.0
    return jnp.asarray(e_g), jnp.asarray(e_b)


def _pad_lanes(v):
    return jnp.pad(v, ((0, 0), (0, LANES - v.shape[1])))


class _LocalWeights:
    def __init__(self, big):
        self.big, self.sent = big, {}

    def arrive(self, group, after):
        return self.big

    def send(self, group, grads):
        self.sent.update(grads)
        return jnp.zeros((), F32)


def _local_step(x, p, tgt, small, comm):
    e_g, e_b = _expanders()
    alog, dtb = _pad_lanes(small["a_log"]), _pad_lanes(small["dt_bias"])
    conv_w = jnp.pad(small["conv_w"], ((0, SUBLANES - CONV_K), (0, 0)))
    rel = _expand_rel_bias(small["rel_bias"])

    h1 = _norm_fwd(x, small["ffn1_norm"], "ffn1_norm")
    big = dict(comm.arrive("ffn1", h1))
    started = big.pop("_token", None)

    def ffn1_w_down(act):
        big.update(comm.arrive("ffn1_down", act))
        return big["ffn1_w_down"]

    gu1, act1, y1 = _ffn_fwd(h1, big["ffn1_w_gu"], ffn1_w_down, "ffn1", started)
    x1, h2 = _residual_norm_fwd(x, y1, 0.5, small["mix_norm"], "mix_norm")

    big = {**big, **comm.arrive("mixer", h2)}
    w_in = big["w_in"]
    w_qz = w_in[:, :IN_QZ]
    w_ab = jnp.pad(w_in[:, IN_AB0:IN_QKVB0], ((0, 0), (0, LANES - 2 * HEADS)))
    w_qkvb = w_in[:, IN_QKVB0:IN_GG0]
    w_gg = w_in[:, IN_GG0:IN_COLS]
    qz = _matmul(h2, w_qz, "nn", F32, "in_qz")
    ab = _matmul(h2, w_ab, "nn", F32, "in_ab")
    pb = _matmul(h2, w_qkvb, "nn", F32, "in_qkvb")
    gg = _matmul(h2, w_gg, "nn", BF16, "in_gates")
    pa, z = qz[:, :3 * HW], qz[:, 3 * HW:]

    def prep(i, n, pa_, prev_, ab_, cw_, alog_, dtb_, eg_, eb_):
        q, k, v = _gdn_post(_conv(pa_, prev_, cw_, i))
        g_b, beta_b = _gdn_gates(ab_, alog_[...], dtb_[...], eg_[...], eb_[...])
        return q, k, v, g_b, beta_b

    qn, kn, vv, g_b, beta_b = _rows(prep, [(pa, "t"), (pa, "p"), (ab, "t")], [conv_w, alog, dtb, e_g, e_b],
                                    [(HW, F32)] * 5, [], 256, "gdn_prep")
    u, w, aqk, qd, kt, tl = _gdn_intra(qn, kn, vv, g_b, beta_b)
    o, states = _gdn_scan(u, w, aqk, qd, kt, tl)
    ya = _rows(lambda i, n, o_, z_, w_: (_gated_norm(o_, z_, w_[...]),), [(o, "t"), (z, "t")], [small["gdn_norm"]],
               [(HW, BF16)], [], 512, "gdn_gated_norm")[0]

    yb = _attention(pb, small["q_norm"], small["k_norm"], rel)

    big = {**big, **comm.arrive("branches", yb)}
    ta = _matmul(ya, big["w_branch_a"], "nn", BF16, "branch_a")
    tb = _matmul(yb, big["w_branch_b"], "nn", BF16, "branch_b")
    mixed = _rows(lambda i, n, gg_, ta_, tb_: (_mix(gg_, ta_, tb_),), [(gg, "t"), (ta, "t"), (tb, "t")], [],
                  [(D_MODEL, BF16)], [], 256, "mix")[0]
    m_out = _matmul(mixed, big["w_out"], "nn", F32, "w_out")
    x2, h3 = _residual_norm_fwd(x1, m_out, 1.0, small["ffn2_norm"], "ffn2_norm")
    big = {**big, **comm.arrive("tail", h3)}
    gu2, act2, y2 = _ffn_fwd(h3, big["ffn2_w_gu"], lambda act: big["ffn2_w_down"], "ffn2")
    x3, h4 = _residual_norm_fwd(x2, y2, 0.5, small["ple_norm"], "ple_norm")
    gp = _matmul(h4, big["ple_gate"], "nn", BF16, "ple_gate")
    pp = _matmul(p, big["ple_proj"], "nn", BF16, "ple_proj")

    def head(i, n, x3_, gp_, pp_, tgt_):
        sg = _sigmoid(gp_)
        err = x3_ + sg * pp_ - tgt_
        dx4 = err * (1.0 / D_MODEL)
        sq = _colsum(err * err)
        part = sq[:, :LANES]
        for j in range(1, D_MODEL // LANES):
            part = part + sq[:, j * LANES:(j + 1) * LANES]
        return dx4, dx4 * pp_ * sg * (1.0 - sg), dx4 * sg, (0.5 / D_MODEL) * part

    dx4, dgp, dpp, loss_lanes = _rows(head, [(x3, "t"), (gp, "t"), (pp, "t"), (tgt, "t")], [],
                                      [(D_MODEL, F32), (D_MODEL, BF16), (D_MODEL, BF16)], [(1, LANES)], 256,
                                      "ple_loss_head")
    loss = jnp.sum(loss_lanes)

    gbig, gsmall = {}, {}
    gbig["ple_proj"] = _matmul(p, dpp, "tn", BF16, "d_ple_proj")
    gbig["ple_gate"] = _matmul(h4, dgp, "tn", BF16, "d_ple_gate")
    dh4 = _matmul(dgp, big["ple_gate"], "nt", BF16, "ple_gate_dh")
    dx3, dy2, gsmall["ple_norm"] = _residual_norm_bwd(x3, small["ple_norm"], [dh4], dx4, 0.5, "ple_norm_bwd")

    dh3, sent = _ffn_bwd(h3, gu2, act2, dy2, big["ffn2_w_gu"], big["ffn2_w_down"], "ffn2", comm,
                         {n: gbig[n] for n in ("ple_proj", "ple_gate")})
    dx2, dx2b, gsmall["ffn2_norm"] = _residual_norm_bwd(x2, small["ffn2_norm"] + sent, [dh3], dx3, 1.0,
                                                        "ffn2_norm_bwd")

    gbig["w_out"] = _matmul(mixed, dx2b, "tn", BF16, "d_w_out")
    dmixed = _matmul(dx2b, big["w_out"], "nt", BF16, "w_out_dx")

    def mix_bwd(i, n, gg_, ta_, tb_, dm_):
        _, vjp = jax.vjp(_mix, gg_, ta_, tb_)
        return vjp(dm_)

    dgg, dta, dtb_ = _rows(mix_bwd, [(gg, "t"), (ta, "t"), (tb, "t"), (dmixed, "t")], [],
                           [(2 * D_MODEL, BF16), (D_MODEL, BF16), (D_MODEL, BF16)], [], 256, "mix_bwd")
    gbig["w_branch_a"] = _matmul(ya, dta, "tn", BF16, "d_branch_a")
    gbig["w_branch_b"] = _matmul(yb, dtb_, "tn", BF16, "d_branch_b")
    dya = _matmul(dta, big["w_branch_a"], "nt", BF16, "branch_a_dx")
    dyb = _matmul(dtb_, big["w_branch_b"], "nt", BF16, "branch_b_dx")

    dq_b, dk_b, dv_b, gsmall["q_norm"], gsmall["k_norm"], gsmall["rel_bias"] = _attention_bwd(
        pb, small["q_norm"], small["k_norm"], rel, dyb)
    dpb = jnp.concatenate([dq_b, dk_b[ATT_PAD:].astype(BF16), dv_b[ATT_PAD:].astype(BF16)], axis=1)

    def gated_bwd(i, n, o_, z_, dya_, w_):
        _, vjp = jax.vjp(_gated_norm, o_, z_, w_[...])
        return vjp(dya_)

    do, dz, gsmall["gdn_norm"] = _rows(gated_bwd, [(o, "t"), (z, "t"), (dya, "t")], [small["gdn_norm"]],
                                       [(HW, F32), (HW, BF16)], [(1, HEAD_DIM)], 256, "gdn_gated_norm_bwd")
    du, dw, da, dqd, dkt, dtl = _gdn_scan_bwd(do, u, w, aqk, qd, kt, tl, states)
    dqn, dkn, dvv, dg_b, dbeta_b = _gdn_intra_bwd(qn, kn, vv, g_b, beta_b, du, dw, da, dqd, dkt, dtl)

    def prep_bwd(i, n, pa_, prev_, ab_, dq_, dk_, dv_, dg_, db_, cw_, alog_, dtb_, eg_, eb_):
        _, vjp = jax.vjp(_gdn_post, _conv(pa_, prev_, cw_, i))
        (dy,) = vjp((dq_, dk_, dv_))
        e_g_, e_b_ = eg_[...], eb_[...]
        _, vjp_g = jax.vjp(lambda a, b, c: _gdn_gates(a, b, c, e_g_, e_b_), ab_, alog_[...], dtb_[...])
        dab, dalog, ddtb = vjp_g((dg_, db_))
        return dy, dab, dalog, ddtb

    dy_conv, dab, dalog, ddtb = _rows(
        prep_bwd, [(pa, "t"), (pa, "p"), (ab, "t"), (dqn, "t"), (dkn, "t"), (dvv, "t"), (dg_b, "t"), (dbeta_b, "t")],
        [conv_w, alog, dtb, e_g, e_b], [(3 * HW, F32), (LANES, BF16)], [(1, LANES), (1, LANES)], 256,
        "gdn_prep_bwd")
    gsmall["a_log"] = dalog[:, :HEADS]
    gsmall["dt_bias"] = ddtb[:, :HEADS]

    def conv_bwd(i, n, dy_, nxt_, pa_, prev_, cw_):
        dpa = dy_ * cw_[CONV_K - 1:CONV_K, :]
        row = lax.broadcasted_iota(jnp.int32, (SUBLANES, dy_.shape[1]), 0)
        dcw = jnp.where(row == CONV_K - 1, _colsum(dy_ * pa_), 0.0)
        for j in range(CONV_K - 1):
            s = CONV_K - 1 - j
            dpa = dpa + _shift_up(dy_, nxt_, s, i, n) * cw_[j:j + 1, :]
            dcw = dcw + jnp.where(row == j, _colsum(dy_ * _shift_down(pa_, prev_, s, i)), 0.0)
        return dpa, dcw

    dpa, dcw = _rows(conv_bwd, [(dy_conv, "t"), (dy_conv, "n"), (pa, "t"), (pa, "p")], [conv_w],
                     [(3 * HW, BF16)], [(SUBLANES, 3 * HW)], 256, "gdn_conv_bwd")
    gsmall["conv_w"] = dcw[:CONV_K]

    dqz = jnp.concatenate([dpa, dz], axis=1)
    d_w_qz = _matmul(h2, dqz, "tn", BF16, "d_in_qz")
    d_w_ab = _matmul(h2, dab, "tn", BF16, "d_in_ab")
    d_w_qkvb = _matmul(h2, dpb, "tn", BF16, "d_in_qkvb")
    d_w_gg = _matmul(h2, dgg, "tn", BF16, "d_in_gates")
    gbig["w_in"] = jnp.concatenate([d_w_qz, d_w_ab[:, :2 * HEADS], d_w_qkvb, d_w_gg,
                                    jnp.zeros((D_MODEL, IN_PADDED - IN_COLS), BF16)], axis=1)
    dh2 = [_matmul(dqz, w_qz, "nt", BF16, "in_qz_dh"), _matmul(dab, w_ab, "nt", BF16, "in_ab_dh"),
           _matmul(dpb, w_qkvb, "nt", BF16, "in_qkvb_dh"), _matmul(dgg, w_gg, "nt", BF16, "in_gates_dh")]
    sent = comm.send("mixer", {n: gbig[n] for n in ("w_out", "w_branch_b", "w_branch_a", "w_in")})
    dx1, dy1, gsmall["mix_norm"] = _residual_norm_bwd(x1, small["mix_norm"] + sent, dh2, dx2, 0.5, "mix_norm_bwd")

    dh1, sent = _ffn_bwd(h1, gu1, act1, dy1, big["ffn1_w_gu"], big["ffn1_w_down"], "ffn1", comm)
    grad_x, _, gsmall["ffn1_norm"] = _residual_norm_bwd(x, small["ffn1_norm"] + sent, [dh1], dx1, 1.0,
                                                        "ffn1_norm_bwd")
    return loss, grad_x, gsmall


GATHER_GROUPS = {"ffn1": ("ffn1_w_gu",),
                 "ffn1_down": ("ffn1_w_down",),
                 "mixer": ("w_in_main", "w_in_edge"),
                 "branches": ("w_branch_a", "w_branch_b", "w_out"),
                 "tail": ("ffn2_w_gu", "ffn2_w_down", "ple_gate", "ple_proj")}
SPLIT_GATHERS = ("ffn1_down", "mixer", "branches", "tail")


def _kind(name):
    return "cols" if name in COL_SHARDED or name.startswith("w_in_") else "rows"


def _merge_w_in(main, edges):
    edge_w = WIN_W - WIN_STEP
    w_in = jnp.pad(main, ((0, 0), (0, edge_w)))
    for d in range(N_DEV):
        at = WIN_STEP * (d + 1)
        w_in = w_in + jnp.pad(edges[:, d * edge_w:(d + 1) * edge_w], ((0, 0), (at, IN_PADDED - at - edge_w)))
    return w_in


class _Fsdp:
    def __init__(self, wts, first):
        self.wts, self.first_token = wts, first
        main, edge = _roll_w_in(jnp.pad(wts["w_in"], ((0, 0), (0, WIN_W - IN_SHARD))))
        self.shards = {n: wts[n].astype(BF16) for n in BIG if n not in ("w_in", "ffn1_w_gu")}
        self.shards.update(w_in_main=main, w_in_edge=edge)
        self.lands = {n: _place_block(self.shards[n], _kind(n), "own_" + n)
                      for group in SPLIT_GATHERS for n in GATHER_GROUPS[group]}
        self.flight, self.sent = {}, {}

    def _gather_first(self, after):
        token = self.first_token + after[0, 0].astype(F32) * 0.0
        me = _index(*_me())
        for n, land in self.lands.items():
            r, c = self.shards[n].shape
            at = (me * r, 0) if _kind(n) == "rows" else (0, me * c)
            token = token + lax.dynamic_slice(land, at, (1, 1))[0, 0].astype(F32) * 0.0
        shard = (self.wts["ffn1_w_gu"] + token).astype(BF16)
        self.shards["ffn1_w_gu"] = shard
        first = _all_gather([shard], [_kind("ffn1_w_gu")], 1)[0]
        token = first[0, 0].astype(F32) * 0.0
        for group in SPLIT_GATHERS:
            names = GATHER_GROUPS[group]
            srcs = [self.shards[n] for n in names]
            lands = [self.lands[n] for n in names]
            make = _gather_copies([s.shape for s in srcs], [_kind(n) for n in names])
            srcs[0] = srcs[0] + token.astype(BF16)
            send_sems, recv_sems, srcs, lands, tok = _split_start(srcs, lands, make, "gather_start_" + group)
            token = token + tok[0, 0]
            self.flight[group] = (send_sems, recv_sems, srcs, lands, make)
        return {"ffn1_w_gu": first, "_token": token}

    def arrive(self, group, after):
        if group == "ffn1":
            return self._gather_first(after)
        send_sems, recv_sems, srcs, lands, make = self.flight[group]
        _, lands = _split_wait(send_sems, recv_sems, srcs, lands, after, make, "gather_wait_" + group)
        full = dict(zip(GATHER_GROUPS[group], lands))
        if group == "mixer":
            full["w_in"] = _merge_w_in(full.pop("w_in_main"), full.pop("w_in_edge"))
        return full

    def send(self, group, grads):
        names = list(grads)
        kinds = ["all" if n == "small" else "win" if n == "w_in" else _kind(n) for n in names]
        shapes = [grads[n].shape if n == "small" else (D_MODEL, WIN_W) if n == "w_in" else self.shards[n].shape
                  for n in names]
        srcs = [grads[n] for n in names]
        lands = [lax.empty((N_DEV,) + tuple(s), g.dtype) for s, g in zip(shapes, srcs)]
        make = _exchange_copies(shapes, kinds)
        send_sems, recv_sems, srcs, lands, tok = _split_start(srcs, lands, make, "grads_start_" + group)
        self.sent[group] = (names, kinds, send_sems, recv_sems, srcs, lands, make)
        return tok[0, 0]

    def received(self, group, after):
        names, kinds, send_sems, recv_sems, srcs, lands, make = self.sent[group]
        srcs, lands = _split_wait(send_sems, recv_sems, srcs, lands, after, make, "grads_wait_" + group)
        return {n: (k, g, r) for n, k, g, r in zip(names, kinds, srcs, lands)}


SMALL_ROWS = ("ffn1_norm", "mix_norm", "ffn2_norm", "ple_norm", "gdn_norm", "q_norm", "k_norm", "a_log", "dt_bias",
              "rel_bias", "conv_w")


def _pack_small(vals):
    rows = []
    for n in SMALL_ROWS:
        v = vals[n]
        if n == "rel_bias":
            v = jnp.pad(v, ((0, 0), (0, 2 * LANES - N_REL)))
        elif n in ("a_log", "dt_bias"):
            v = _pad_lanes(v)
        rows.append(v.reshape(-1, LANES))
    packed = jnp.concatenate(rows, axis=0)
    return jnp.pad(packed, ((0, -packed.shape[0] % SUBLANES), (0, 0)))


def _unpack_small(packed, shapes):
    out, off = {}, 0
    for n in SMALL_ROWS:
        shp = shapes[n]
        if n == "rel_bias":
            out[n] = packed[off:off + 2 * HEADS].reshape(HEADS, 2 * LANES)[:, :N_REL]
            off += 2 * HEADS
        elif n in ("a_log", "dt_bias"):
            out[n] = packed[off:off + 1, :HEADS]
            off += 1
        else:
            r = int(np.prod(shp)) // LANES
            out[n] = packed[off:off + r].reshape(shp)
            off += r
    return out


WEIGHTS = ("ffn1_norm", "ffn1_w_gu", "ffn1_w_down", "mix_norm", "w_in", "conv_w", "a_log", "dt_bias", "gdn_norm",
           "q_norm", "k_norm", "rel_bias", "w_branch_a", "w_branch_b", "w_out", "ffn2_norm", "ffn2_w_gu",
           "ffn2_w_down", "ple_norm", "ple_gate", "ple_proj")


def kernel(x, p, ffn1_norm, ffn1_w_gu, ffn1_w_down, mix_norm, w_in, conv_w, a_log, dt_bias, gdn_norm, q_norm, k_norm, rel_bias, w_branch_a, w_branch_b, w_out, ffn2_norm, ffn2_w_gu, ffn2_w_down, ple_norm, ple_gate, ple_proj, loss_target, m_ffn1_norm, m_ffn1_w_gu, m_ffn1_w_down, m_mix_norm, m_w_in, m_conv_w, m_a_log, m_dt_bias, m_gdn_norm, m_q_norm, m_k_norm, m_rel_bias, m_w_branch_a, m_w_branch_b, m_w_out, m_ffn2_norm, m_ffn2_w_gu, m_ffn2_w_down, m_ple_norm, m_ple_gate, m_ple_proj, v_ffn1_norm, v_ffn1_w_gu, v_ffn1_w_down, v_mix_norm, v_w_in, v_conv_w, v_a_log, v_dt_bias, v_gdn_norm, v_q_norm, v_k_norm, v_rel_bias, v_w_branch_a, v_w_branch_b, v_w_out, v_ffn2_norm, v_ffn2_w_gu, v_ffn2_w_down, v_ple_norm, v_ple_gate, v_ple_proj):
    args = dict(locals())
    def layer0(v):
        return v[0] if v.ndim == 3 else v

    wts = {n: layer0(args[n]) for n in WEIGHTS}
    mom = {n: layer0(args["m_" + n]) for n in WEIGHTS}
    var = {n: layer0(args["v_" + n]) for n in WEIGHTS}
    x2d, p2d, tgt = x[0], p[0, 0], loss_target[0]
    my_index = _index(*_me())

    small = {n: wts[n] for n in SMALL_ROWS if n != "conv_w"}
    conv_shard = wts["conv_w"]
    conv_cols = conv_shard.shape[1]
    conv_packed = jnp.zeros((SUBLANES, N_DEV * conv_cols), F32)
    conv_packed = lax.dynamic_update_slice(conv_packed, jnp.pad(conv_shard, ((0, SUBLANES - CONV_K), (0, 0))),
                                           (0, my_index * conv_cols))
    small["conv_w"] = _all_reduce_small(conv_packed.reshape(-1, LANES), "conv_w_gather").reshape(SUBLANES, -1)[:CONV_K]

    fsdp = _Fsdp(wts, small["conv_w"][0, 0] * 0.0)

    loss, grad_x, gsmall = _local_step(x2d, p2d, tgt, small, fsdp)
    loss = lax.psum(loss, ("x", "y", "c"))

    fsdp.send("small", {"small": _pack_small(gsmall)})

    outs_big, after = {}, grad_x
    for group in list(fsdp.sent):
        for n, (kind, grad, recv) in fsdp.received(group, after).items():
            if n == "small":
                small_sum = _sum_small(recv, grad)
            elif n == "w_in":
                g_in = _sum_w_in_windows(recv, grad)[:, :IN_SHARD]
                outs_big[n] = [g_in] + list(_adamw_small(wts[n], g_in, mom[n], var[n], "adamw_w_in"))
            else:
                outs_big[n] = _adamw_recv(recv, grad, kind, wts[n], mom[n], var[n], "adamw_" + n)
            after = small_sum if n == "small" else outs_big[n][1]

    small_shapes = {n: (small[n].shape if n != "conv_w" else (CONV_K, N_DEV * conv_cols)) for n in SMALL_ROWS}
    gsum = _unpack_small(small_sum, small_shapes)
    gsum["conv_w"] = lax.dynamic_slice(gsum["conv_w"], (0, my_index * conv_cols), (CONV_K, conv_cols))
    rep = [n for n in SMALL_ROWS if n != "conv_w"]
    rep_shapes = {n: small_shapes[n] for n in rep}

    def pack_rep(vals):
        return _pack_small({**{n: vals[n] for n in rep}, "conv_w": jnp.zeros((CONV_K, LANES), F32)})

    def unpack_rep(packed):
        return _unpack_small(packed, {**rep_shapes, "conv_w": (CONV_K, LANES)})

    outs_small = [unpack_rep(o) for o in _adamw_small(pack_rep(wts), pack_rep(gsum), pack_rep(mom), pack_rep(var),
                                                      "adamw_replicated")]
    pad8 = functools.partial(jnp.pad, pad_width=((0, SUBLANES - CONV_K), (0, 0)))
    outs_conv = [o[:CONV_K] for o in _adamw_small(pad8(conv_shard), pad8(gsum["conv_w"]), pad8(mom["conv_w"]),
                                                   pad8(var["conv_w"]), "adamw_conv")]

    def leaf(kind, n):
        if n in BIG:
            return outs_big[n][kind][None]
        if n == "conv_w":
            return (gsum["conv_w"] if kind == 0 else outs_conv[kind - 1])[None]
        return (gsum[n] if kind == 0 else outs_small[kind - 1][n]).reshape(args[n].shape)

    result = [loss, grad_x[None]]
    for kind in range(4):
        result += [leaf(kind, n) for n in WEIGHTS]
    return tuple(result)
```

```python
import functools

import numpy as np
import jax
import jax.numpy as jnp
from jax import lax
from jax.experimental import pallas as pl
from jax.experimental.pallas import tpu as pltpu

F32 = jnp.float32
BF16 = jnp.bfloat16
HIGHEST = lax.Precision.HIGHEST
MESH = pl.DeviceIdType.MESH

D_MODEL = 2048
D_FF = 5632
HEADS = 8
HEAD_DIM = 128
HW = HEADS * HEAD_DIM
CHUNK = 64
LEFT_CHUNKS = 8
MAX_REL = 128
N_REL = (CHUNK - 1) + MAX_REL + 1
CONV_K = 4
EPS = 1e-6
NEG_INF = -1e30
N_DEV = 8
LANES = 128
SUBLANES = 8
VMEM_LIMIT = 56 * 1024 * 1024

MATMUL_WHOLE_K = 2048

ATT_QB = 256
ATT_KW = ATT_QB + LEFT_CHUNKS * CHUNK
ATT_PAD = LEFT_CHUNKS * CHUNK
GDN_CB = 8
GDN_GROUP = 32
GDN_SCAN_UNROLL = 4

ADAM_LR = 0.001
ADAM_B1 = 0.9
ADAM_B2 = 0.999
ADAM_EPS = 1e-08
ADAM_WD = 0.01
ADAM_STEP = 10

IN_QZ = 3 * HW + HW
IN_AB0 = IN_QZ
IN_QKVB0 = IN_AB0 + 2 * HEADS
IN_GG0 = IN_QKVB0 + 3 * HW
IN_COLS = IN_GG0 + 2 * D_MODEL

BIG = ("ffn1_w_gu", "ffn1_w_down", "w_in", "w_branch_a", "w_branch_b", "w_out",
       "ffn2_w_gu", "ffn2_w_down", "ple_gate", "ple_proj")
COL_SHARDED = ("ffn1_w_gu", "w_in", "w_branch_a", "w_branch_b", "ffn2_w_gu", "ple_proj")


def _params(semantics=None, **kw):
    return pltpu.CompilerParams(dimension_semantics=semantics, vmem_limit_bytes=VMEM_LIMIT, **kw)


def _pick(n, cands):
    for c in cands:
        if n % c == 0:
            return c
    return n


SMEM_SPEC = pl.BlockSpec(memory_space=pltpu.SMEM)


def _after(token):
    return ([], []) if token is None else ([SMEM_SPEC], [jnp.reshape(token, (1,)).astype(F32)])


def _matmul(a, b, mode, out_dtype, name, after=None):
    halves = (a.ndim == 3 and mode == "nt") or (b.ndim == 3 and mode == "tn")
    if mode == "nn":
        (m, k), (k2, n) = a.shape, b.shape
    elif mode == "nt":
        (m, k), (n, k2) = (a.shape[-2], a.shape[-1] * (a.ndim - 1)), b.shape
    else:
        (k, m), (k2, n) = a.shape, (b.shape[-2], b.shape[-1] * (b.ndim - 1))
    assert k == k2 and a.ndim + b.ndim == (5 if halves else 4), (a.shape, b.shape, mode)
    tm = _pick(m, (1024, 512, 256, 128))
    if halves and mode == "tn":
        tn = _pick(n // 2, (2816, 1408, 1024, 512, 256, 128))
        tk = _pick(k, (1024, 512, 256, 128))
    elif halves:
        tn = _pick(n, (1024, 512, 256, 128))
        tk = _pick(k // 2, (2816, 2048, 1536, 1024, 512, 256, 128))
    else:
        tn = _pick(n, (1024, 512, 256, 128))
        tk = k if k <= MATMUL_WHOLE_K else _pick(k, (2816, 2048, 1536, 1024, 512, 256, 128))
    nk = k // tk
    per_half = (n // 2) // tn if mode == "tn" else (k // 2) // tk
    if mode == "nn":
        a_spec = pl.BlockSpec((tm, tk), lambda i, j, kk: (i, kk))
        b_spec = pl.BlockSpec((tk, tn), lambda i, j, kk: (kk, j))
        dims = (((1,), (0,)), ((), ()))
    elif mode == "nt":
        a_spec = pl.BlockSpec((tm, tk), lambda i, j, kk: (i, kk))
        b_spec = pl.BlockSpec((tn, tk), lambda i, j, kk: (j, kk))
        dims = (((1,), (1,)), ((), ()))
        if halves:
            a_spec = pl.BlockSpec((None, tm, tk), lambda i, j, kk: (kk // per_half, i, kk % per_half))
    else:
        a_spec = pl.BlockSpec((tk, tm), lambda i, j, kk: (kk, i))
        b_spec = pl.BlockSpec((tk, tn), lambda i, j, kk: (kk, j))
        dims = (((0,), (0,)), ((), ()))
        if halves:
            b_spec = pl.BlockSpec((None, tk, tn), lambda i, j, kk: (j // per_half, kk, j % per_half))

    after_specs, after_args = _after(after)

    def body(a_ref, b_ref, *rest):
        o_ref, acc = rest[len(after_args)], rest[len(after_args) + 1:]
        prod = lax.dot_general(a_ref[...].astype(BF16), b_ref[...].astype(BF16), dims, preferred_element_type=F32)
        if nk == 1:
            o_ref[...] = prod.astype(o_ref.dtype)
            return
        acc_ref, kk = acc[0], pl.program_id(2)

        @pl.when(kk == 0)
        def _():
            acc_ref[...] = prod

        @pl.when((kk > 0) & (kk < nk - 1))
        def _():
            acc_ref[...] += prod

        @pl.when(kk == nk - 1)
        def _():
            o_ref[...] = (acc_ref[...] + prod).astype(o_ref.dtype)

    return pl.pallas_call(
        body, name=name,
        out_shape=jax.ShapeDtypeStruct((m, n), out_dtype),
        grid=(m // tm, n // tn, nk),
        in_specs=[a_spec, b_spec] + after_specs,
        out_specs=pl.BlockSpec((tm, tn), lambda i, j, kk: (i, j)),
        scratch_shapes=[pltpu.VMEM((tm, tn), F32)] if nk > 1 else [],
        compiler_params=_params(("parallel", "parallel", "arbitrary")),
    )(a, b, *after_args)


def _rows(fn, row_ins, consts, row_outs, acc_outs, tile, name):
    t_rows = row_ins[0][0].shape[0]
    tile = min(tile, t_rows)
    assert t_rows % tile == 0 and tile % SUBLANES == 0
    n = t_rows // tile
    per = tile // SUBLANES
    last8 = t_rows // SUBLANES - 1
    in_specs = []
    for arr, kind in row_ins:
        c = arr.shape[1]
        if kind == "t":
            in_specs.append(pl.BlockSpec((tile, c), lambda i: (i, 0)))
        elif kind == "p":
            in_specs.append(pl.BlockSpec((SUBLANES, c), lambda i: (jnp.maximum(i * per - 1, 0), 0)))
        else:
            in_specs.append(pl.BlockSpec((SUBLANES, c), lambda i: (jnp.minimum((i + 1) * per, last8), 0)))
    for arr in consts:
        in_specs.append(pl.BlockSpec(arr.shape, lambda i, nd=arr.ndim: (0,) * nd))
    out_shape = [jax.ShapeDtypeStruct((t_rows, c), dt) for c, dt in row_outs]
    out_specs = [pl.BlockSpec((tile, c), lambda i: (i, 0)) for c, _ in row_outs]
    for shp in acc_outs:
        out_shape.append(jax.ShapeDtypeStruct(shp, F32))
        out_specs.append(pl.BlockSpec(shp, lambda i, nd=len(shp): (0,) * nd))
    n_in = len(row_ins) + len(consts)
    n_row_out = len(row_outs)

    def body(*refs):
        i = pl.program_id(0)
        vals = [r[...].astype(F32) for r in refs[:len(row_ins)]]
        res = fn(i, n, *vals, *refs[len(row_ins):n_in])
        outs = refs[n_in:]
        for r, v in zip(outs[:n_row_out], res[:n_row_out]):
            r[...] = v.astype(r.dtype)
        if acc_outs:
            @pl.when(i == 0)
            def _():
                for r in outs[n_row_out:]:
                    r[...] = jnp.zeros_like(r)

            for r, v in zip(outs[n_row_out:], res[n_row_out:]):
                r[...] += v

    res = pl.pallas_call(
        body, name=name, out_shape=out_shape, grid=(n,), in_specs=in_specs, out_specs=out_specs,
        compiler_params=_params(("arbitrary",) if acc_outs else ("parallel",)),
    )(*[a for a, _ in row_ins], *consts)
    return res


def _rms(x, w):
    return x * lax.rsqrt(jnp.mean(x * x, axis=-1, keepdims=True) + EPS) * w


def _l2n(x):
    return x * lax.rsqrt(jnp.sum(x * x, axis=-1, keepdims=True) + EPS)


def _sigmoid(x):
    return 1.0 / (1.0 + jnp.exp(-x))


def _silu(x):
    return x * _sigmoid(x)


def _softplus(x):
    return jnp.maximum(x, 0.0) + jnp.log(1.0 + jnp.exp(-jnp.abs(x)))


def _heads(fn, *xs):
    nh = xs[0].shape[1] // HEAD_DIM
    return jnp.concatenate(
        [fn(*[x[:, h * HEAD_DIM:(h + 1) * HEAD_DIM] for x in xs]) for h in range(nh)], axis=1)


def _colsum(x):
    return jnp.sum(x, axis=0, keepdims=True)


def _gated_norm(o, z, w):
    return _heads(lambda oh, zh: _rms(oh, w) * _silu(zh), o, z)


def _mix(gg, ta, tb):
    return _sigmoid(gg[:, :D_MODEL]) * ta + _sigmoid(gg[:, D_MODEL:]) * tb


def _gdn_post(y):
    a = _silu(y)
    q = _heads(lambda v: _l2n(v) * (HEAD_DIM ** -0.5), a[:, :HW])
    k = _heads(_l2n, a[:, HW:2 * HW])
    return q, k, a[:, 2 * HW:]


NN = (((1,), (0,)), ((), ()))
NT = (((1,), (1,)), ((), ()))
TN = (((0,), (0,)), ((), ()))


def _dg(a, b, dims):
    return lax.dot_general(a, b, dims, preferred_element_type=F32)


def _split2(x):
    hi = x.astype(BF16)
    return hi, (x - hi.astype(F32)).astype(BF16)


def _split3(x):
    hi = x.astype(BF16)
    r = x - hi.astype(F32)
    mid = r.astype(BF16)
    return hi, mid, (r - mid.astype(F32)).astype(BF16)


def _dg3(a, b, dims):
    ah, al = _split2(a)
    bh, bl = _split2(b)
    return _dg(ah, bh, dims) + (_dg(ah, bl, dims) + _dg(al, bh, dims))


BNN = (((2,), (1,)), ((0,), (0,)))
BNT = (((2,), (2,)), ((0,), (0,)))
BTN = (((1,), (1,)), ((0,), (0,)))


@jax.custom_vjp
def _mm3(a, b):
    return _dg3(a, b, BNN)


_mm3.defvjp(lambda a, b: (_dg3(a, b, BNN), (a, b)),
            lambda res, g: (_dg3(g, res[1], BNT), _dg3(res[0], g, BTN)))


def _xm(x, m, dims):
    mb = m.astype(BF16)
    parts = _split3(x)
    return _dg(parts[0], mb, dims) + (_dg(parts[1], mb, dims) + _dg(parts[2], mb, dims))


def _mx(m, x, dims):
    mb = m.astype(BF16)
    parts = _split3(x)
    return _dg(mb, parts[0], dims) + (_dg(mb, parts[1], dims) + _dg(mb, parts[2], dims))


@jax.custom_vjp
def _times_const(x, m):
    return _xm(x, m, NN)


_times_const.defvjp(lambda x, m: (_xm(x, m, NN), m),
                    lambda m, g: (_xm(g, m, NT), jnp.zeros_like(m)))


@jax.custom_vjp
def _const_times(m, x):
    return _mx(m, x, NN)


_const_times.defvjp(lambda m, x: (_mx(m, x, NN), m),
                    lambda m, g: (jnp.zeros_like(m), _mx(m, g, TN)))


@jax.custom_vjp
def _lane_mean_cols(x, avg):
    return _mx(avg, x, BNT)


_lane_mean_cols.defvjp(lambda x, avg: (_mx(avg, x, BNT), avg),
                       lambda avg, g: (_xm(g, avg, BTN), jnp.zeros_like(avg)))


def _gdn_gates(ab, alog, dtb, e_g, e_b):
    t = ab.shape[0]
    g = -jnp.exp(alog) * _softplus(ab + dtb)
    beta = _sigmoid(ab)
    ri = lax.broadcasted_iota(jnp.int32, (t, t), 0)
    ci = lax.broadcasted_iota(jnp.int32, (t, t), 1)
    shift = CHUNK.bit_length() - 1
    same = jnp.right_shift(ri, shift) == jnp.right_shift(ci, shift)
    tril = jnp.where(same & (ri >= ci), 1.0, 0.0).astype(F32)
    gc = _const_times(tril, g)
    return _times_const(gc, e_g), _times_const(beta, e_b)


def _shift_down(x, halo, s, i):
    if s == 0:
        return x
    halo = jnp.where(i == 0, 0.0, halo)
    xr = pltpu.roll(x, s, 0)
    hr = pltpu.roll(halo, s, 0)
    row = lax.broadcasted_iota(jnp.int32, (SUBLANES, x.shape[1]), 0)
    top = jnp.where(row < s, hr, xr[:SUBLANES])
    return jnp.concatenate([top, xr[SUBLANES:]], axis=0)


def _shift_up(x, halo, s, i, n):
    if s == 0:
        return x
    t = x.shape[0]
    halo = jnp.where(i == n - 1, 0.0, halo)
    xr = pltpu.roll(x, t - s, 0)
    hr = pltpu.roll(halo, SUBLANES - s, 0)
    row = lax.broadcasted_iota(jnp.int32, (SUBLANES, x.shape[1]), 0)
    bot = jnp.where(row >= SUBLANES - s, hr, xr[t - SUBLANES:])
    return jnp.concatenate([xr[:t - SUBLANES], bot], axis=0)


def _conv(pa, prev, cw_ref, i):
    y = pa * cw_ref[CONV_K - 1:CONV_K, :]
    for j in range(CONV_K - 1):
        y = y + _shift_down(pa, prev, CONV_K - 1 - j, i) * cw_ref[j:j + 1, :]
    return y


def _dot_nt(a, b, precision=None):
    return lax.dot_general(a, b, (((1,), (1,)), ((), ())), precision=precision, preferred_element_type=F32)


def _dot_tn(a, b, precision=None):
    return lax.dot_general(a, b, (((0,), (0,)), ((), ())), precision=precision, preferred_element_type=F32)


def _dot(a, b, precision=None):
    return jnp.dot(a, b, precision=precision, preferred_element_type=F32)


def _bf(x):
    return x.astype(BF16)


def _neumann_inverse(lmat):
    nb, c, _ = lmat.shape
    ri = lax.broadcasted_iota(jnp.int32, (nb, c, c), 1)
    ci = lax.broadcasted_iota(jnp.int32, (nb, c, c), 2)
    pw = -lmat
    inv = jnp.where(ri == ci, 1.0, 0.0).astype(F32) + pw
    for _ in range(5):
        pw = _mm3(pw, pw)
        inv = inv + _mm3(inv, pw)
    return inv


@jax.custom_vjp
def _unit_lower_inverse(lmat):
    return _neumann_inverse(lmat)


def _unit_lower_inverse_fwd(lmat):
    inv = _neumann_inverse(lmat)
    return inv, inv


def _unit_lower_inverse_bwd(inv, g):
    return (-_dg3(_dg3(inv, g, BTN), inv, BNT),)


_unit_lower_inverse.defvjp(_unit_lower_inverse_fwd, _unit_lower_inverse_bwd)


def _gdn_chunk(q, k, v, gc, bb):
    nb, c, _ = q.shape
    ri = lax.broadcasted_iota(jnp.int32, (nb, c, c), 1)
    ci = lax.broadcasted_iota(jnp.int32, (nb, c, c), 2)
    incl = ri >= ci
    strict = ri > ci
    g_row = gc[:, :, :c]
    g_col = _lane_mean_cols(gc, jnp.full((nb, c, LANES), 1.0 / LANES, F32))
    decay = jnp.where(incl, jnp.exp(jnp.where(incl, g_row - g_col, 0.0)), 0.0)
    kb = k * bb
    lmat = jnp.where(strict, _dg(_bf(kb), _bf(k), BNT) * decay, 0.0)
    inv = _unit_lower_inverse(lmat)
    egc = jnp.exp(gc)
    u = _mm3(inv, v * bb)
    w = _mm3(inv, kb * egc)
    aqk = _dg(_bf(q), _bf(k), BNT) * decay
    last = lax.broadcasted_iota(jnp.int32, (nb, c, LANES), 1) == c - 1
    tot = jnp.sum(jnp.where(last, gc, 0.0), axis=1, keepdims=True)
    k_tail = k * jnp.exp(tot - gc)
    tail = jnp.broadcast_to(jnp.exp(tot), (nb, SUBLANES, LANES))
    return u, w, aqk, q * egc, k_tail, tail


def _gdn_intra(qn, kn, vv, g_b, beta_b):
    t_rows = qn.shape[0]
    nc = t_rows // CHUNK
    cb = min(GDN_GROUP, nc)
    rows = cb * CHUNK
    col = pl.BlockSpec((rows, HEAD_DIM), lambda h, b: (b, h))

    def body(q_ref, k_ref, v_ref, g_ref, b_ref, u_ref, w_ref, a_ref, qd_ref, kt_ref, tl_ref):
        def group(gi, carry):
            r = pl.ds(pl.multiple_of(gi * (grp * CHUNK), grp * CHUNK), grp * CHUNK)
            ins = [ref[r, :].reshape(grp, CHUNK, HEAD_DIM) for ref in (q_ref, k_ref, v_ref, g_ref, b_ref)]
            u, w, aqk, qd, kt, tl = _gdn_chunk(*ins)
            for ref, val in ((u_ref, u), (w_ref, w), (qd_ref, qd), (kt_ref, kt)):
                ref[r, :] = val.reshape(grp * CHUNK, HEAD_DIM)
            a_ref[0, r, :] = aqk.reshape(grp * CHUNK, CHUNK)
            tl_ref[0, pl.ds(gi * grp, grp)] = tl
            return carry

        grp = min(GDN_GROUP, cb)
        lax.fori_loop(0, cb // grp, group, 0)

    full = jax.ShapeDtypeStruct((t_rows, HW), F32)
    return pl.pallas_call(
        body, name="gdn_intra_fwd",
        out_shape=[full, full, jax.ShapeDtypeStruct((HEADS, t_rows, CHUNK), F32), full, full,
                   jax.ShapeDtypeStruct((HEADS, nc, SUBLANES, LANES), F32)],
        grid=(HEADS, nc // cb),
        in_specs=[col] * 5,
        out_specs=[col, col, pl.BlockSpec((1, rows, CHUNK), lambda h, b: (h, b, 0)), col, col,
                   pl.BlockSpec((1, cb, SUBLANES, LANES), lambda h, b: (h, b, 0, 0))],
        compiler_params=_params(("parallel", "parallel")),
    )(qn, kn, vv, g_b, beta_b)


def _gdn_intra_bwd(qn, kn, vv, g_b, beta_b, du, dw, da, dqd, dkt, dtl):
    t_rows = qn.shape[0]
    nc = t_rows // CHUNK
    cb = min(GDN_GROUP, nc)
    rows = cb * CHUNK
    col = pl.BlockSpec((rows, HEAD_DIM), lambda h, b: (b, h))
    a_spec = pl.BlockSpec((1, rows, CHUNK), lambda h, b: (h, b, 0))
    tl_spec = pl.BlockSpec((1, cb, SUBLANES, LANES), lambda h, b: (h, b, 0, 0))

    def body(q_ref, k_ref, v_ref, g_ref, b_ref, du_ref, dw_ref, da_ref, dqd_ref, dkt_ref, dtl_ref,
             dq_ref, dk_ref, dv_ref, dg_ref, db_ref):
        def group(gi, carry):
            r = pl.ds(pl.multiple_of(gi * (grp * CHUNK), grp * CHUNK), grp * CHUNK)
            wide = (grp, CHUNK, HEAD_DIM)
            ins = [ref[r, :].reshape(wide) for ref in (q_ref, k_ref, v_ref, g_ref, b_ref)]
            cts = (du_ref[r, :].reshape(wide), dw_ref[r, :].reshape(wide),
                   da_ref[0, r, :].reshape(grp, CHUNK, CHUNK), dqd_ref[r, :].reshape(wide),
                   dkt_ref[r, :].reshape(wide), dtl_ref[0, pl.ds(gi * grp, grp)])
            grads = jax.vjp(_gdn_chunk, *ins)[1](cts)
            for ref, val in zip((dq_ref, dk_ref, dv_ref, dg_ref, db_ref), grads):
                ref[r, :] = val.reshape(grp * CHUNK, HEAD_DIM)
            return carry

        grp = min(GDN_GROUP, cb)
        lax.fori_loop(0, cb // grp, group, 0)

    full = jax.ShapeDtypeStruct((t_rows, HW), F32)
    return pl.pallas_call(
        body, name="gdn_intra_bwd",
        out_shape=[full] * 5,
        grid=(HEADS, nc // cb),
        in_specs=[col] * 7 + [a_spec, col, col, tl_spec],
        out_specs=[col] * 5,
        compiler_params=_params(("parallel", "parallel")),
    )(qn, kn, vv, g_b, beta_b, du, dw, da, dqd, dkt, dtl)


def _head_cols(h):
    return slice(h * HEAD_DIM, (h + 1) * HEAD_DIM)


def _gdn_scan(u, w, aqk, qd, kt, tl):
    t_rows = u.shape[0]
    nc = t_rows // CHUNK
    cb = min(GDN_CB, nc)
    rows = cb * CHUNK
    wide = pl.BlockSpec((rows, HW), lambda b: (b, 0))

    def body(u_ref, w_ref, a_ref, qd_ref, kt_ref, tl_ref, o_ref, s_out_ref, s_ref):
        @pl.when(pl.program_id(0) == 0)
        def _():
            s_ref[...] = jnp.zeros_like(s_ref)

        def chunk(ci, carry):
            r = pl.ds(pl.multiple_of(ci * CHUNK, CHUNK), CHUNK)
            for h in range(HEADS):
                hc = _head_cols(h)
                s = s_ref[h]
                s_out_ref[ci, h] = s
                sb = _bf(s)
                vn = u_ref[r, hc] - _dot(_bf(w_ref[r, hc]), sb)
                vnb = _bf(vn)
                o_ref[r, hc] = _dot(_bf(qd_ref[r, hc]), sb) + _dot(_bf(a_ref[h, r, :]), vnb)
                s_ref[h] = s * tl_ref[h, ci, 0:1, :] + _dot_tn(_bf(kt_ref[r, hc]), vnb)
            return carry

        lax.fori_loop(0, cb, chunk, 0, unroll=GDN_SCAN_UNROLL)

    return pl.pallas_call(
        body, name="gdn_scan_fwd",
        out_shape=[jax.ShapeDtypeStruct((t_rows, HW), F32),
                   jax.ShapeDtypeStruct((nc, HEADS, HEAD_DIM, HEAD_DIM), F32)],
        grid=(nc // cb,),
        in_specs=[wide, wide, pl.BlockSpec((HEADS, rows, CHUNK), lambda b: (0, b, 0)), wide, wide,
                  pl.BlockSpec((HEADS, cb, SUBLANES, LANES), lambda b: (0, b, 0, 0))],
        out_specs=[wide, pl.BlockSpec((cb, HEADS, HEAD_DIM, HEAD_DIM), lambda b: (b, 0, 0, 0))],
        scratch_shapes=[pltpu.VMEM((HEADS, HEAD_DIM, HEAD_DIM), F32)],
        compiler_params=_params(("arbitrary",)),
    )(u, w, aqk, qd, kt, tl)


def _gdn_scan_bwd(do, u, w, aqk, qd, kt, tl, states):
    t_rows = u.shape[0]
    nc = t_rows // CHUNK
    cb = min(GDN_CB, nc)
    rows = cb * CHUNK
    nb = nc // cb
    wide = pl.BlockSpec((rows, HW), lambda b: (nb - 1 - b, 0))
    a_spec = pl.BlockSpec((HEADS, rows, CHUNK), lambda b: (0, nb - 1 - b, 0))
    tl_spec = pl.BlockSpec((HEADS, cb, SUBLANES, LANES), lambda b: (0, nb - 1 - b, 0, 0))

    def body(do_ref, u_ref, w_ref, a_ref, qd_ref, kt_ref, tl_ref, s_in_ref,
             du_ref, dw_ref, da_ref, dqd_ref, dkt_ref, dtl_ref, ds_ref):
        @pl.when(pl.program_id(0) == 0)
        def _():
            ds_ref[...] = jnp.zeros_like(ds_ref)

        row0 = lax.broadcasted_iota(jnp.int32, (SUBLANES, LANES), 0) == 0

        def chunk(step, carry):
            ci = cb - 1 - step
            r = pl.ds(pl.multiple_of(ci * CHUNK, CHUNK), CHUNK)
            for h in range(HEADS):
                hc = _head_cols(h)
                s = s_in_ref[ci, h]
                ds_next = ds_ref[h]
                sb, dsb = _bf(s), _bf(ds_next)
                wb, ab, ktb, qdb = _bf(w_ref[r, hc]), _bf(a_ref[h, r, :]), _bf(kt_ref[r, hc]), _bf(qd_ref[r, hc])
                dob = _bf(do_ref[r, hc])
                vn = u_ref[r, hc] - _dot(wb, sb)
                vnb = _bf(vn)
                dvn = _dot_tn(ab, dob) + _dot(ktb, dsb)
                dvnb = _bf(dvn)
                du_ref[r, hc] = dvn
                dw_ref[r, hc] = -_dot_nt(dvnb, sb)
                da_ref[h, r, :] = _dot_nt(dob, vnb)
                dqd_ref[r, hc] = _dot_nt(dob, sb)
                dkt_ref[r, hc] = _dot_nt(vnb, dsb)
                dtl_ref[h, ci] = jnp.where(row0, _colsum(s * ds_next), 0.0)
                ds_ref[h] = _dot_tn(qdb, dob) + ds_next * tl_ref[h, ci, 0:1, :] - _dot_tn(wb, dvnb)
            return carry

        lax.fori_loop(0, cb, chunk, 0, unroll=GDN_SCAN_UNROLL)

    full = jax.ShapeDtypeStruct((t_rows, HW), F32)
    return pl.pallas_call(
        body, name="gdn_scan_bwd",
        out_shape=[full, full, jax.ShapeDtypeStruct((HEADS, t_rows, CHUNK), F32), full, full,
                   jax.ShapeDtypeStruct((HEADS, nc, SUBLANES, LANES), F32)],
        grid=(nb,),
        in_specs=[wide, wide, wide, a_spec, wide, wide, tl_spec,
                  pl.BlockSpec((cb, HEADS, HEAD_DIM, HEAD_DIM), lambda b: (nb - 1 - b, 0, 0, 0))],
        out_specs=[wide, wide, a_spec, wide, wide, tl_spec],
        scratch_shapes=[pltpu.VMEM((HEADS, HEAD_DIM, HEAD_DIM), F32)],
        compiler_params=_params(("arbitrary",)),
    )(do, u, w, aqk, qd, kt, tl, states)


def _att_profile_index():
    j = lax.broadcasted_iota(jnp.int32, (SUBLANES, ATT_KW), 1)
    return jnp.clip(ATT_PAD - j, -(CHUNK - 1), MAX_REL) + (CHUNK - 1)


def _att_far_back():
    qi = lax.broadcasted_iota(jnp.int32, (ATT_QB, ATT_KW), 0)
    kj = lax.broadcasted_iota(jnp.int32, (ATT_QB, ATT_KW), 1)
    return kj < qi


def _rotate_rows(x, forward):
    rows, lanes = x.shape
    row = lax.broadcasted_iota(jnp.int32, x.shape, 0)
    for bit in range(rows.bit_length() - 1):
        amount = (1 << bit) if forward else lanes - (1 << bit)
        x = jnp.where(jnp.bitwise_and(jnp.right_shift(row, bit), 1) == 1, pltpu.roll(x, amount, 1), x)
    return x


def _att_in_band():
    qi = lax.broadcasted_iota(jnp.int32, (ATT_QB, ATT_KW), 0)
    kj = lax.broadcasted_iota(jnp.int32, (ATT_QB, ATT_KW), 1)
    shift = CHUNK.bit_length() - 1
    qc = jnp.right_shift(qi, shift)
    kc = jnp.right_shift(kj, shift) - LEFT_CHUNKS
    return (kc <= qc) & (kc >= qc - LEFT_CHUNKS)


def _att_valid(b):
    kj = lax.broadcasted_iota(jnp.int32, (1, ATT_KW), 1)
    return jnp.where(kj + b * ATT_QB >= ATT_PAD, 0.0, NEG_INF)


def _rms_parts(x, w):
    r = lax.rsqrt(jnp.mean(x * x, axis=-1, keepdims=True) + EPS)
    xn = x * r
    return xn * w, xn, r


def _rms_bwd(dy, xn, r, w):
    dxn = dy * w
    dx = r * (dxn - xn * jnp.mean(dxn * xn, axis=-1, keepdims=True))
    return dx, _colsum(dy * xn)


def _att_probs(qb, kb, bias, before_start):
    s = _dot_nt(qb, kb) * (HEAD_DIM ** -0.5) + bias + before_start
    e = jnp.exp(s - jnp.max(s, axis=-1, keepdims=True))
    return e * (1.0 / jnp.sum(e, axis=-1, keepdims=True))


def _att_specs():
    q_spec = pl.BlockSpec((ATT_QB, HEAD_DIM), lambda h, b: (b, h))
    back = ATT_PAD // ATT_QB
    k_specs = [pl.BlockSpec((ATT_QB, HEAD_DIM), lambda h, b, j=j: (jnp.maximum(b + j - back, 0), HEADS + h))
               for j in range(3)]
    v_specs = [pl.BlockSpec((ATT_QB, HEAD_DIM), lambda h, b, j=j: (jnp.maximum(b + j - back, 0), 2 * HEADS + h))
               for j in range(3)]
    w_spec = pl.BlockSpec((1, HEAD_DIM), lambda h, b: (0, 0))
    smem = pl.BlockSpec(memory_space=pltpu.SMEM)
    return q_spec, k_specs, v_specs, w_spec, smem


BIAS_SPEC = pl.BlockSpec((1, ATT_QB, ATT_KW), lambda h, b: (h, 0, 0))


def _expand_rel_bias(rel):
    def body(rel_ref, bias_ref):
        h = pl.program_id(0)
        idx = _att_profile_index()

        def fill(r, acc):
            return jnp.where(idx == r, rel_ref[h, r], acc)

        profile = lax.fori_loop(0, N_REL, fill, jnp.zeros((SUBLANES, ATT_KW), F32))
        table = _rotate_rows(jnp.concatenate([profile] * (ATT_QB // SUBLANES), axis=0), True)
        table = jnp.where(_att_far_back(), rel_ref[h, N_REL - 1], table)
        bias_ref[0] = jnp.where(_att_in_band(), table, NEG_INF)

    return pl.pallas_call(
        body, name="rel_bias_expand",
        out_shape=jax.ShapeDtypeStruct((HEADS, ATT_QB, ATT_KW), F32), grid=(HEADS,),
        in_specs=[pl.BlockSpec(memory_space=pltpu.SMEM)],
        out_specs=pl.BlockSpec((1, ATT_QB, ATT_KW), lambda h: (h, 0, 0)),
        compiler_params=_params(("parallel",)),
    )(rel)


def _attention(pb, qw, kw, bias):
    t_rows = pb.shape[0]
    q_spec, k_specs, v_specs, w_spec, _ = _att_specs()

    def body(q_ref, k0, k1, k2, v0, v1, v2, qw_ref, kw_ref, bias_ref, o_ref):
        b = pl.program_id(1)
        kwin = jnp.concatenate([k0[...], k1[...], k2[...]], axis=0)
        vwin = jnp.concatenate([v0[...], v1[...], v2[...]], axis=0)
        q = _rms(q_ref[...], qw_ref[...])
        k = _rms(kwin, kw_ref[...])
        p = _att_probs(_bf(q), _bf(k), bias_ref[0], _att_valid(b))
        o_ref[...] = _dot(_bf(p), _bf(vwin)).astype(o_ref.dtype)

    return pl.pallas_call(
        body, name="band_attention_fwd",
        out_shape=jax.ShapeDtypeStruct((t_rows, HW), BF16),
        grid=(HEADS, t_rows // ATT_QB),
        in_specs=[q_spec] + k_specs + v_specs + [w_spec, w_spec, BIAS_SPEC],
        out_specs=pl.BlockSpec((ATT_QB, HEAD_DIM), lambda h, b: (b, h)),
        compiler_params=_params(("parallel", "arbitrary")),
    )(pb, pb, pb, pb, pb, pb, pb, qw, kw, bias)


def _attention_bwd(pb, qw, kw, bias, dyb):
    t_rows = pb.shape[0]
    nb = t_rows // ATT_QB
    q_spec, k_specs, v_specs, w_spec, smem = _att_specs()
    pad_rows = t_rows + ATT_PAD
    acc_spec = pl.BlockSpec((pad_rows, HEAD_DIM), lambda h, b: (0, h))

    def body(q_ref, k0, k1, k2, v0, v1, v2, qw_ref, kw_ref, bias_ref, do_ref,
             dq_ref, dk_ref, dv_ref, dqw_ref, dkw_ref, drel_ref, dbias_ref):
        h, b = pl.program_id(0), pl.program_id(1)

        @pl.when(b == 0)
        def _():
            dbias_ref[...] = jnp.zeros_like(dbias_ref)
            dk_ref[...] = jnp.zeros_like(dk_ref)
            dv_ref[...] = jnp.zeros_like(dv_ref)

        @pl.when((b == 0) & (h == 0))
        def _():
            dqw_ref[...] = jnp.zeros_like(dqw_ref)
            dkw_ref[...] = jnp.zeros_like(dkw_ref)

        kwin = jnp.concatenate([k0[...], k1[...], k2[...]], axis=0)
        vwin = jnp.concatenate([v0[...], v1[...], v2[...]], axis=0)
        scale = HEAD_DIM ** -0.5
        qw_, kw_ = qw_ref[...], kw_ref[...]
        q, qn, rq = _rms_parts(q_ref[...], qw_)
        k, kn, rk = _rms_parts(kwin, kw_)
        qb, kb, dob = _bf(q), _bf(k), _bf(do_ref[...])
        p = _att_probs(qb, kb, bias_ref[0], _att_valid(b))
        dp = _dot_nt(dob, _bf(vwin))
        ds = p * (dp - jnp.sum(p * dp, axis=-1, keepdims=True))
        dbias_ref[...] += ds
        ds = _bf(ds)
        dq, dqw = _rms_bwd(_dot(ds, kb) * scale, qn, rq, qw_)
        dk, dkw = _rms_bwd(_dot_tn(ds, qb) * scale, kn, rk, kw_)
        dq_ref[...] = dq.astype(dq_ref.dtype)
        win = pl.ds(pl.multiple_of(b * ATT_QB, ATT_QB), ATT_KW)
        dk_ref[win, :] += dk
        dv_ref[win, :] += _dot_tn(_bf(p), dob)
        dqw_ref[...] += dqw
        dkw_ref[...] += dkw

        @pl.when(b == nb - 1)
        def _():
            tot, far = dbias_ref[...], _att_far_back()
            far_sum = jnp.sum(jnp.where(far, tot, 0.0))
            per_offset = _colsum(_rotate_rows(jnp.where(far, 0.0, tot), False))
            idx = _att_profile_index()
            first_row = lax.broadcasted_iota(jnp.int32, idx.shape, 0) == 0
            spread = jnp.where(first_row, per_offset, 0.0)

            def reduce(r, carry):
                drel_ref[h, r] = jnp.sum(jnp.where(idx == r, spread, 0.0)) + jnp.where(r == N_REL - 1, far_sum, 0.0)
                return carry

            lax.fori_loop(0, N_REL, reduce, 0)

    return pl.pallas_call(
        body, name="band_attention_bwd",
        out_shape=[jax.ShapeDtypeStruct((t_rows, HW), BF16),
                   jax.ShapeDtypeStruct((pad_rows, HW), F32), jax.ShapeDtypeStruct((pad_rows, HW), F32),
                   jax.ShapeDtypeStruct((1, HEAD_DIM), F32), jax.ShapeDtypeStruct((1, HEAD_DIM), F32),
                   jax.ShapeDtypeStruct((HEADS, N_REL), F32)],
        grid=(HEADS, nb),
        in_specs=[q_spec] + k_specs + v_specs + [w_spec, w_spec, BIAS_SPEC, q_spec],
        out_specs=[q_spec, acc_spec, acc_spec, w_spec, w_spec, smem],
        scratch_shapes=[pltpu.VMEM((ATT_QB, ATT_KW), F32)],
        compiler_params=_params(("arbitrary", "arbitrary")),
    )(pb, pb, pb, pb, pb, pb, pb, qw, kw, bias, dyb)


def _me():
    return lax.axis_index("x"), lax.axis_index("y"), lax.axis_index("c")


def _index(x, y, c):
    return 4 * x + 2 * y + c


HBM_SPEC = pl.BlockSpec(memory_space=pl.ANY)


def _block(ref, kind, d, r, c):
    if kind == "all":
        return ref
    if kind == "rows":
        return ref.at[pl.ds(d * r, r), :]
    if kind == "win":
        return ref.at[:, pl.ds(d * WIN_STEP, c)]
    return ref.at[:, pl.ds(d * c, c)]


def _all_gather(shards, kinds, n_gather):
    n = len(shards)

    def body(*refs):
        x_refs, out_refs = refs[:n], refs[n:2 * n]
        send_sems, recv_sems, local_sems = refs[2 * n:]
        x, y, c = _me()
        me, sibling = (x, y, c), (x, y, 1 - c)
        chips = [(1 - x, y), (x, 1 - y), (1 - x, 1 - y)]

        def copy(i, k, blk, to, src=None):
            r_, c_ = shards[i].shape
            dst = _block(out_refs[i], kinds[i], _index(*blk), r_, c_)
            return pltpu.make_async_remote_copy(
                src_ref=dst if src is None else src, dst_ref=dst,
                send_sem=send_sems.at[i, k], recv_sem=recv_sems.at[i, k], device_id=to, device_id_type=MESH)

        sends, local = [], []
        for i in range(n):
            r_, c_ = shards[i].shape
            mine = pltpu.make_async_copy(x_refs[i], _block(out_refs[i], kinds[i], _index(*me), r_, c_),
                                         local_sems.at[i])
            mine.start()
            local.append(mine)
            if i >= n_gather:
                continue
            first = [copy(i, 0, me, sibling, src=x_refs[i])]
            first += [copy(i, 1 + j, me, (*chip, c), src=x_refs[i]) for j, chip in enumerate(chips)]
            for cp in first:
                cp.start()
            sends += first
        for i in range(n_gather):
            for j, chip in enumerate(chips):
                copy(i, 1 + j, (*chip, c), me).wait_recv()
                passed = copy(i, 4 + j, (*chip, c), sibling)
                passed.start()
                sends.append(passed)
        for i in range(n_gather):
            copy(i, 0, sibling, me).wait_recv()
            for j, chip in enumerate(chips):
                copy(i, 4 + j, (*chip, 1 - c), me).wait_recv()
        for cp in sends:
            cp.wait_send()
        for cp in local:
            cp.wait()

    def full_shape(s, kind):
        r_, c_ = s.shape
        return (N_DEV * r_, c_) if kind == "rows" else (r_, N_DEV * c_)

    return pl.pallas_call(
        body, name="weights_all_gather",
        out_shape=[jax.ShapeDtypeStruct(full_shape(s, k), s.dtype) for s, k in zip(shards, kinds)],
        in_specs=[HBM_SPEC] * n, out_specs=[HBM_SPEC] * n,
        scratch_shapes=[pltpu.SemaphoreType.DMA((n_gather, 7)), pltpu.SemaphoreType.DMA((n_gather, 7)),
                        pltpu.SemaphoreType.DMA((n,))],
        compiler_params=pltpu.CompilerParams(has_side_effects=True),
    )(*shards)


SEM_SPEC = pl.BlockSpec(memory_space=pltpu.SEMAPHORE)
HBM_ONLY = pl.BlockSpec(memory_space=pltpu.HBM)
DATAFLOW = pltpu.SideEffectType.DATAFLOW_SIDE_EFFECTING


def _peers():
    x, y, c = _me()
    return [(x ^ (k >> 2), y ^ ((k >> 1) & 1), c ^ (k & 1)) for k in range(1, N_DEV)]


def _gather_copies(shapes, kinds):
    def make(src_refs, land_refs, send_sems, recv_sems):
        mine = _index(*_me())
        return [pltpu.make_async_remote_copy(
            src_ref=src_refs[i], dst_ref=_block(land_refs[i], kind, mine, r, c),
            send_sem=send_sems.at[7 * i + k], recv_sem=recv_sems.at[7 * i + k], device_id=peer, device_id_type=MESH)
            for i, ((r, c), kind) in enumerate(zip(shapes, kinds)) for k, peer in enumerate(_peers())]

    return make


def _exchange_copies(shapes, kinds):
    def make(src_refs, land_refs, send_sems, recv_sems):
        mine = _index(*_me())
        return [pltpu.make_async_remote_copy(
            src_ref=_block(src_refs[i], kind, _index(*peer), r, c), dst_ref=land_refs[i].at[mine],
            send_sem=send_sems.at[7 * i + k], recv_sem=recv_sems.at[7 * i + k], device_id=peer, device_id_type=MESH)
            for i, ((r, c), kind) in enumerate(zip(shapes, kinds)) for k, peer in enumerate(_peers())]

    return make


def _place_block(shard, kind, name):
    r, c = shard.shape
    tile = _row_tile(r, c)
    nt = r // tile
    full = (N_DEV * r, c) if kind == "rows" else (r, N_DEV * c)

    def body(me_ref, x_ref, out_ref):
        out_ref[...] = x_ref[...]

    if kind == "rows":
        out_spec = pl.BlockSpec((tile, c), lambda i, me: (me[0] * nt + i, 0))
    else:
        out_spec = pl.BlockSpec((tile, c), lambda i, me: (i, me[0]))
    return pl.pallas_call(
        body, name=name, out_shape=jax.ShapeDtypeStruct(full, shard.dtype),
        grid_spec=pltpu.PrefetchScalarGridSpec(
            num_scalar_prefetch=1, grid=(nt,),
            in_specs=[pl.BlockSpec((tile, c), lambda i, me: (i, 0))], out_specs=out_spec),
        compiler_params=_params(("arbitrary",)),
    )(_my_index_operand(), shard)


def _split_start(srcs, lands, make, name):
    n = len(srcs)

    def body(*refs):
        send_sems, recv_sems = refs[2 * n], refs[2 * n + 1]
        for cp in make(refs[:n], refs[n:2 * n], send_sems, recv_sems):
            cp.start()
        refs[-1][...] = jnp.zeros_like(refs[-1])

    arrays = list(srcs) + list(lands)
    out = pl.pallas_call(
        body, name=name,
        out_shape=(pltpu.SemaphoreType.DMA((7 * n,)), pltpu.SemaphoreType.DMA((7 * n,)),
                   *[pltpu.HBM(a.shape, a.dtype) for a in arrays], jax.ShapeDtypeStruct((SUBLANES, LANES), F32)),
        in_specs=[HBM_ONLY] * (2 * n),
        out_specs=(SEM_SPEC, SEM_SPEC, *[HBM_ONLY] * (2 * n), pl.BlockSpec(memory_space=pltpu.VMEM)),
        input_output_aliases={i: 2 + i for i in range(2 * n)},
        compiler_params=pltpu.CompilerParams(has_side_effects=DATAFLOW),
    )(*[pltpu.with_memory_space_constraint(a, pltpu.HBM) for a in arrays])
    return out[0], out[1], list(out[2:2 + n]), list(out[2 + n:2 + 2 * n]), out[-1]


def _split_wait(send_sems, recv_sems, srcs, lands, after, make, name):
    n = len(srcs)

    def body(*refs):
        for cp in make(refs[:n], refs[n:2 * n], refs[2 * n], refs[2 * n + 1]):
            cp.wait_send()
            cp.wait_recv()

    arrays = list(srcs) + list(lands)
    out = pl.pallas_call(
        body, name=name,
        out_shape=tuple(pltpu.HBM(a.shape, a.dtype) for a in arrays),
        in_specs=[HBM_ONLY] * (2 * n) + [SEM_SPEC, SEM_SPEC, pl.BlockSpec(memory_space=pl.ANY)],
        out_specs=tuple([HBM_ONLY] * (2 * n)),
        input_output_aliases={i: i for i in range(2 * n)},
        compiler_params=pltpu.CompilerParams(has_side_effects=DATAFLOW),
    )(*arrays, send_sems, recv_sems, after)
    return list(out[:n]), list(out[n:])


def _all_reduce_small(vals, name):
    rows, width = vals.shape

    def body(x_ref, out_ref, buf_ref, send_sems, recv_sems):
        x, y, c = _me()
        mine = _index(x, y, c)
        buf_ref[mine] = x_ref[...]
        copies = []
        for k in range(1, N_DEV):
            px, py, pc = x ^ (k >> 2), y ^ ((k >> 1) & 1), c ^ (k & 1)
            copies.append(pltpu.make_async_remote_copy(
                src_ref=x_ref, dst_ref=buf_ref.at[mine],
                send_sem=send_sems.at[k - 1], recv_sem=recv_sems.at[k - 1],
                device_id=(px, py, pc), device_id_type=MESH))
        for cp in copies:
            cp.start()
        for cp in copies:
            cp.wait()
        acc = buf_ref[0]
        for j in range(1, N_DEV):
            acc = acc + buf_ref[j]
        out_ref[...] = acc

    vmem = pl.BlockSpec(memory_space=pltpu.VMEM)
    return pl.pallas_call(
        body, name=name,
        out_shape=jax.ShapeDtypeStruct(vals.shape, F32),
        in_specs=[vmem], out_specs=vmem,
        scratch_shapes=[pltpu.VMEM((N_DEV, rows, width), F32),
                        pltpu.SemaphoreType.DMA((7,)), pltpu.SemaphoreType.DMA((7,))],
        compiler_params=pltpu.CompilerParams(has_side_effects=True),
    )(vals)


def _adamw_math(w, g, m, v):
    m = ADAM_B1 * m + (1.0 - ADAM_B1) * g
    v = ADAM_B2 * v + (1.0 - ADAM_B2) * (g * g)
    m_hat = m / (1.0 - ADAM_B1 ** ADAM_STEP)
    v_hat = v / (1.0 - ADAM_B2 ** ADAM_STEP)
    delta = -ADAM_LR * (m_hat / (jnp.sqrt(v_hat) + ADAM_EPS) + ADAM_WD * w)
    return delta, m, v


ROW_TILE_ELEMS = 384 * 1024


def _row_tile(rows, width):
    best = SUBLANES
    for t in range(SUBLANES, rows + 1, SUBLANES):
        if rows % t == 0 and t * width <= ROW_TILE_ELEMS:
            best = t
    return best


def _sum_received(r_ref, own, me):
    g = None
    for j in range(N_DEV):
        term = jnp.where(me == j, own, r_ref[j].astype(F32))
        g = term if g is None else g + term
    return g


def _my_index_operand():
    return _index(*_me()).astype(jnp.int32).reshape(1)


def _sum_small(recv, own):
    def body(me_ref, r_ref, own_ref, out_ref):
        out_ref[...] = _sum_received(r_ref, own_ref[...], me_ref[0])

    whole = lambda shape: pl.BlockSpec(shape, lambda i, me, nd=len(shape): (0,) * nd)
    return pl.pallas_call(
        body, name="small_grads_sum", out_shape=jax.ShapeDtypeStruct(own.shape, F32),
        grid_spec=pltpu.PrefetchScalarGridSpec(
            num_scalar_prefetch=1, grid=(1,), in_specs=[whole(recv.shape), whole(own.shape)],
            out_specs=whole(own.shape)),
        compiler_params=_params(("arbitrary",)),
    )(_my_index_operand(), recv, own)


def _adamw_recv(recv, grad, kind, w, m, v, name):
    _, rows, width = recv.shape
    tile = _row_tile(rows, width)
    nt = rows // tile

    def body(me_ref, r_ref, own_ref, w_ref, m_ref, v_ref, g_out, d_out, m_out, v_out):
        g = _sum_received(r_ref, own_ref[...].astype(F32), me_ref[0])
        d, mn, vn = _adamw_math(w_ref[...], g, m_ref[...], v_ref[...])
        g_out[...] = g
        d_out[...] = d
        m_out[...] = mn
        v_out[...] = vn

    if kind == "rows":
        own_spec = pl.BlockSpec((tile, width), lambda i, me: (me[0] * nt + i, 0))
    else:
        own_spec = pl.BlockSpec((tile, width), lambda i, me: (i, me[0]))
    spec = pl.BlockSpec((tile, width), lambda i, me: (i, 0))
    shape = jax.ShapeDtypeStruct((rows, width), F32)
    return pl.pallas_call(
        body, name=name, out_shape=[shape] * 4,
        grid_spec=pltpu.PrefetchScalarGridSpec(
            num_scalar_prefetch=1, grid=(nt,),
            in_specs=[pl.BlockSpec((N_DEV, tile, width), lambda i, me: (0, i, 0)), own_spec, spec, spec, spec],
            out_specs=[spec] * 4),
        compiler_params=_params(("parallel",)),
    )(_my_index_operand(), recv, grad, w, m, v)


WIN_STEP = 1408
WIN_W = 1536
IN_SHARD = IN_COLS // N_DEV
IN_PADDED = WIN_STEP * (N_DEV - 1) + WIN_W


def _roll_w_in(shard_padded):
    rows = shard_padded.shape[0]
    tile = _row_tile(rows, WIN_W)

    def body(x_ref, main_ref, edge_ref):
        win = pltpu.roll(x_ref[...], 2 * _index(*_me()), 1).astype(BF16)
        main_ref[...] = win[:, :WIN_STEP]
        edge_ref[...] = win[:, WIN_STEP:]

    return pl.pallas_call(
        body, name="w_in_window",
        out_shape=[jax.ShapeDtypeStruct((rows, WIN_STEP), BF16), jax.ShapeDtypeStruct((rows, WIN_W - WIN_STEP), BF16)],
        grid=(rows // tile,),
        in_specs=[pl.BlockSpec((tile, WIN_W), lambda i: (i, 0))],
        out_specs=[pl.BlockSpec((tile, WIN_STEP), lambda i: (i, 0)),
                   pl.BlockSpec((tile, WIN_W - WIN_STEP), lambda i: (i, 0))],
        compiler_params=_params(("parallel",)),
    )(shard_padded)


def _sum_w_in_windows(recv, grad):
    _, rows, width = recv.shape
    tile = _row_tile(rows, width)

    def body(me_ref, r_ref, g_ref, g_out, own_ref, sem):
        me = me_ref[0]
        rows_i = pl.ds(pl.multiple_of(pl.program_id(0) * tile, tile), tile)
        own = pltpu.make_async_copy(g_ref.at[rows_i, pl.ds(pl.multiple_of(me * WIN_STEP, LANES), width)], own_ref, sem)
        own.start()
        own.wait()
        g_out[...] = pltpu.roll(_sum_received(r_ref, own_ref[...].astype(F32), me), width - 2 * me, 1)

    return pl.pallas_call(
        body, name="w_in_grad_sum", out_shape=jax.ShapeDtypeStruct((rows, width), F32),
        grid_spec=pltpu.PrefetchScalarGridSpec(
            num_scalar_prefetch=1, grid=(rows // tile,),
            in_specs=[pl.BlockSpec((N_DEV, tile, width), lambda i, me: (0, i, 0)), HBM_SPEC],
            out_specs=pl.BlockSpec((tile, width), lambda i, me: (i, 0)),
            scratch_shapes=[pltpu.VMEM((tile, width), BF16), pltpu.SemaphoreType.DMA]),
        compiler_params=_params(("arbitrary",)),
    )(_my_index_operand(), recv, grad)


def _adamw_small(w, g, m, v, name):
    def fn(i, n, w_, g_, m_, v_):
        return _adamw_math(w_, g_, m_, v_)

    r, c = w.shape
    return _rows(fn, [(w, "t"), (g, "t"), (m, "t"), (v, "t")], [], [(c, F32)] * 3, [], _row_tile(r, c), name)


def _norm_fwd(x, w, name):
    return _rows(lambda i, n, x_, w_: (_rms(x_, w_[...]),), [(x, "t")], [w], [(D_MODEL, BF16)], [], 512, name)[0]


def _residual_norm_fwd(x, y, scale, w, name):
    def fn(i, n, x_, y_, w_):
        xn = x_ + scale * y_
        return xn, _rms(xn, w_[...])

    return _rows(fn, [(x, "t"), (y, "t")], [w], [(D_MODEL, F32), (D_MODEL, BF16)], [], 512, name)


def _residual_norm_bwd(x, w, dhs, dres, scale, name):
    nh = len(dhs)

    def fn(i, n, x_, dres_, *rest):
        dh = rest[0]
        for extra in rest[1:nh]:
            dh = dh + extra
        _, vjp = jax.vjp(_rms, x_, rest[nh][...])
        dx, dw = vjp(dh)
        dx = dx + dres_
        return dx, scale * dx, dw

    return _rows(fn, [(x, "t"), (dres, "t")] + [(d, "t") for d in dhs], [w],
                 [(D_MODEL, F32), (D_MODEL, BF16)], [(1, D_MODEL)], 256, name)


FFN_UP_TN = 512


def _ffn_up(h, w_gu, name, after=None):
    t, d = h.shape
    f = w_gu.shape[1] // 2
    tm = _pick(t, (1024, 512, 256, 128))
    nj = f // FFN_UP_TN
    after_specs, after_args = _after(after)

    def body(h_ref, wg_ref, wu_ref, *rest):
        g_ref, u_ref, act_ref = rest[len(after_args):]
        hb = h_ref[...]
        g = jnp.dot(hb, wg_ref[...], preferred_element_type=F32)
        u = jnp.dot(hb, wu_ref[...], preferred_element_type=F32)
        g_ref[...] = g.astype(BF16)
        u_ref[...] = u.astype(BF16)
        act_ref[...] = (_silu(g) * u).astype(BF16)

    out = pl.BlockSpec((tm, FFN_UP_TN), lambda i, j: (i, j))
    return pl.pallas_call(
        body, name=name, out_shape=[jax.ShapeDtypeStruct((t, f), BF16)] * 3, grid=(t // tm, nj),
        in_specs=[pl.BlockSpec((tm, d), lambda i, j: (i, 0)),
                  pl.BlockSpec((d, FFN_UP_TN), lambda i, j: (0, j)),
                  pl.BlockSpec((d, FFN_UP_TN), lambda i, j: (0, j + nj))] + after_specs,
        out_specs=[out, out, out],
        compiler_params=_params(("parallel", "parallel")),
    )(h, w_gu, w_gu, *after_args)


def _ffn_fwd(h, w_gu, get_w_down, tag, after=None):
    g, u, act = _ffn_up(h, w_gu, tag + "_gu", after)
    y = _matmul(act, get_w_down(act), "nn", F32, tag + "_down")
    return (g, u), act, y


def _ffn_dact(dy, w_down, g, u, name):
    t, d = dy.shape
    f = w_down.shape[0]
    tm = _pick(t, (1024, 512, 256, 128))

    def body(dy_ref, w_ref, g_ref, u_ref, out_ref):
        dact = lax.dot_general(dy_ref[...], w_ref[...], NT, preferred_element_type=F32)
        g_, u_ = g_ref[...].astype(F32), u_ref[...].astype(F32)
        sg = _sigmoid(g_)
        out_ref[0] = (dact * u_ * (sg * (1.0 + g_ * (1.0 - sg)))).astype(BF16)
        out_ref[1] = (dact * (g_ * sg)).astype(BF16)

    tile = pl.BlockSpec((tm, FFN_UP_TN), lambda i, j: (i, j))
    return pl.pallas_call(
        body, name=name, out_shape=jax.ShapeDtypeStruct((2, t, f), BF16), grid=(t // tm, f // FFN_UP_TN),
        in_specs=[pl.BlockSpec((tm, d), lambda i, j: (i, 0)), pl.BlockSpec((FFN_UP_TN, d), lambda i, j: (j, 0)),
                  tile, tile],
        out_specs=pl.BlockSpec((2, tm, FFN_UP_TN), lambda i, j: (0, i, j)),
        compiler_params=_params(("parallel", "parallel")),
    )(dy, w_down, g, u)


def _ffn_bwd(h, gu, act, dy, w_gu, w_down, tag, comm, more=None):
    dgu = _ffn_dact(dy, w_down, gu[0], gu[1], tag + "_dact")
    sent = comm.send(tag + "_gu", {tag + "_w_gu": _matmul(h, dgu, "tn", BF16, tag + "_d_w_gu")})
    sent = sent + comm.send(tag + "_down", {tag + "_w_down": _matmul(act, dy, "tn", BF16, tag + "_d_w_down", sent),
                                            **(more or {})})
    dh = _matmul(dgu, w_gu, "nt", BF16, tag + "_dh")
    return dh, sent


def _expanders():
    e_g = np.zeros((LANES, HW), np.float32)
    e_b = np.zeros((LANES, HW), np.float32)
    for h in range(HEADS):
        e_g[h, h * HEAD_DIM:(h + 1) * HEAD_DIM] = 1.0
        e_b[HEADS + h, h * HEAD_DIM:(h + 1) * HEAD_DIM] = 1.0
    return jnp.asarray(e_g), jnp.asarray(e_b)


def _pad_lanes(v):
    return jnp.pad(v, ((0, 0), (0, LANES - v.shape[1])))


class _LocalWeights:
    def __init__(self, big):
        self.big, self.sent = big, {}

    def arrive(self, group, after):
        return self.big

    def send(self, group, grads):
        self.sent.update(grads)
        return jnp.zeros((), F32)


def _local_step(x, p, tgt, small, comm):
    e_g, e_b = _expanders()
    alog, dtb = _pad_lanes(small["a_log"]), _pad_lanes(small["dt_bias"])
    conv_w = jnp.pad(small["conv_w"], ((0, SUBLANES - CONV_K), (0, 0)))
    rel = _expand_rel_bias(small["rel_bias"])

    h1 = _norm_fwd(x, small["ffn1_norm"], "ffn1_norm")
    big = dict(comm.arrive("ffn1", h1))
    started = big.pop("_token", None)

    def ffn1_w_down(act):
        big.update(comm.arrive("ffn1_down", act))
        return big["ffn1_w_down"]

    gu1, act1, y1 = _ffn_fwd(h1, big["ffn1_w_gu"], ffn1_w_down, "ffn1", started)
    x1, h2 = _residual_norm_fwd(x, y1, 0.5, small["mix_norm"], "mix_norm")

    big = {**big, **comm.arrive("mixer", h2)}
    w_in = big["w_in"]
    w_qz = w_in[:, :IN_QZ]
    w_ab = jnp.pad(w_in[:, IN_AB0:IN_QKVB0], ((0, 0), (0, LANES - 2 * HEADS)))
    w_qkvb = w_in[:, IN_QKVB0:IN_GG0]
    w_gg = w_in[:, IN_GG0:IN_COLS]
    qz = _matmul(h2, w_qz, "nn", F32, "in_qz")
    ab = _matmul(h2, w_ab, "nn", F32, "in_ab")
    pb = _matmul(h2, w_qkvb, "nn", F32, "in_qkvb")
    gg = _matmul(h2, w_gg, "nn", BF16, "in_gates")
    pa, z = qz[:, :3 * HW], qz[:, 3 * HW:]

    def prep(i, n, pa_, prev_, ab_, cw_, alog_, dtb_, eg_, eb_):
        q, k, v = _gdn_post(_conv(pa_, prev_, cw_, i))
        g_b, beta_b = _gdn_gates(ab_, alog_[...], dtb_[...], eg_[...], eb_[...])
        return q, k, v, g_b, beta_b

    qn, kn, vv, g_b, beta_b = _rows(prep, [(pa, "t"), (pa, "p"), (ab, "t")], [conv_w, alog, dtb, e_g, e_b],
                                    [(HW, F32)] * 5, [], 256, "gdn_prep")
    u, w, aqk, qd, kt, tl = _gdn_intra(qn, kn, vv, g_b, beta_b)
    o, states = _gdn_scan(u, w, aqk, qd, kt, tl)
    ya = _rows(lambda i, n, o_, z_, w_: (_gated_norm(o_, z_, w_[...]),), [(o, "t"), (z, "t")], [small["gdn_norm"]],
               [(HW, BF16)], [], 512, "gdn_gated_norm")[0]

    yb = _attention(pb, small["q_norm"], small["k_norm"], rel)

    big = {**big, **comm.arrive("branches", yb)}
    ta = _matmul(ya, big["w_branch_a"], "nn", BF16, "branch_a")
    tb = _matmul(yb, big["w_branch_b"], "nn", BF16, "branch_b")
    mixed = _rows(lambda i, n, gg_, ta_, tb_: (_mix(gg_, ta_, tb_),), [(gg, "t"), (ta, "t"), (tb, "t")], [],
                  [(D_MODEL, BF16)], [], 256, "mix")[0]
    m_out = _matmul(mixed, big["w_out"], "nn", F32, "w_out")
    x2, h3 = _residual_norm_fwd(x1, m_out, 1.0, small["ffn2_norm"], "ffn2_norm")
    big = {**big, **comm.arrive("tail", h3)}
    gu2, act2, y2 = _ffn_fwd(h3, big["ffn2_w_gu"], lambda act: big["ffn2_w_down"], "ffn2")
    x3, h4 = _residual_norm_fwd(x2, y2, 0.5, small["ple_norm"], "ple_norm")
    gp = _matmul(h4, big["ple_gate"], "nn", BF16, "ple_gate")
    pp = _matmul(p, big["ple_proj"], "nn", BF16, "ple_proj")

    def head(i, n, x3_, gp_, pp_, tgt_):
        sg = _sigmoid(gp_)
        err = x3_ + sg * pp_ - tgt_
        dx4 = err * (1.0 / D_MODEL)
        sq = _colsum(err * err)
        part = sq[:, :LANES]
        for j in range(1, D_MODEL // LANES):
            part = part + sq[:, j * LANES:(j + 1) * LANES]
        return dx4, dx4 * pp_ * sg * (1.0 - sg), dx4 * sg, (0.5 / D_MODEL) * part

    dx4, dgp, dpp, loss_lanes = _rows(head, [(x3, "t"), (gp, "t"), (pp, "t"), (tgt, "t")], [],
                                      [(D_MODEL, F32), (D_MODEL, BF16), (D_MODEL, BF16)], [(1, LANES)], 256,
                                      "ple_loss_head")
    loss = jnp.sum(loss_lanes)

    gbig, gsmall = {}, {}
    gbig["ple_proj"] = _matmul(p, dpp, "tn", BF16, "d_ple_proj")
    gbig["ple_gate"] = _matmul(h4, dgp, "tn", BF16, "d_ple_gate")
    dh4 = _matmul(dgp, big["ple_gate"], "nt", BF16, "ple_gate_dh")
    dx3, dy2, gsmall["ple_norm"] = _residual_norm_bwd(x3, small["ple_norm"], [dh4], dx4, 0.5, "ple_norm_bwd")

    dh3, sent = _ffn_bwd(h3, gu2, act2, dy2, big["ffn2_w_gu"], big["ffn2_w_down"], "ffn2", comm,
                         {n: gbig[n] for n in ("ple_proj", "ple_gate")})
    dx2, dx2b, gsmall["ffn2_norm"] = _residual_norm_bwd(x2, small["ffn2_norm"] + sent, [dh3], dx3, 1.0,
                                                        "ffn2_norm_bwd")

    gbig["w_out"] = _matmul(mixed, dx2b, "tn", BF16, "d_w_out")
    dmixed = _matmul(dx2b, big["w_out"], "nt", BF16, "w_out_dx")

    def mix_bwd(i, n, gg_, ta_, tb_, dm_):
        _, vjp = jax.vjp(_mix, gg_, ta_, tb_)
        return vjp(dm_)

    dgg, dta, dtb_ = _rows(mix_bwd, [(gg, "t"), (ta, "t"), (tb, "t"), (dmixed, "t")], [],
                           [(2 * D_MODEL, BF16), (D_MODEL, BF16), (D_MODEL, BF16)], [], 256, "mix_bwd")
    gbig["w_branch_a"] = _matmul(ya, dta, "tn", BF16, "d_branch_a")
    gbig["w_branch_b"] = _matmul(yb, dtb_, "tn", BF16, "d_branch_b")
    dya = _matmul(dta, big["w_branch_a"], "nt", BF16, "branch_a_dx")
    dyb = _matmul(dtb_, big["w_branch_b"], "nt", BF16, "branch_b_dx")

    dq_b, dk_b, dv_b, gsmall["q_norm"], gsmall["k_norm"], gsmall["rel_bias"] = _attention_bwd(
        pb, small["q_norm"], small["k_norm"], rel, dyb)
    dpb = jnp.concatenate([dq_b, dk_b[ATT_PAD:].astype(BF16), dv_b[ATT_PAD:].astype(BF16)], axis=1)

    def gated_bwd(i, n, o_, z_, dya_, w_):
        _, vjp = jax.vjp(_gated_norm, o_, z_, w_[...])
        return vjp(dya_)

    do, dz, gsmall["gdn_norm"] = _rows(gated_bwd, [(o, "t"), (z, "t"), (dya, "t")], [small["gdn_norm"]],
                                       [(HW, F32), (HW, BF16)], [(1, HEAD_DIM)], 256, "gdn_gated_norm_bwd")
    du, dw, da, dqd, dkt, dtl = _gdn_scan_bwd(do, u, w, aqk, qd, kt, tl, states)
    dqn, dkn, dvv, dg_b, dbeta_b = _gdn_intra_bwd(qn, kn, vv, g_b, beta_b, du, dw, da, dqd, dkt, dtl)

    def prep_bwd(i, n, pa_, prev_, ab_, dq_, dk_, dv_, dg_, db_, cw_, alog_, dtb_, eg_, eb_):
        _, vjp = jax.vjp(_gdn_post, _conv(pa_, prev_, cw_, i))
        (dy,) = vjp((dq_, dk_, dv_))
        e_g_, e_b_ = eg_[...], eb_[...]
        _, vjp_g = jax.vjp(lambda a, b, c: _gdn_gates(a, b, c, e_g_, e_b_), ab_, alog_[...], dtb_[...])
        dab, dalog, ddtb = vjp_g((dg_, db_))
        return dy, dab, dalog, ddtb

    dy_conv, dab, dalog, ddtb = _rows(
        prep_bwd, [(pa, "t"), (pa, "p"), (ab, "t"), (dqn, "t"), (dkn, "t"), (dvv, "t"), (dg_b, "t"), (dbeta_b, "t")],
        [conv_w, alog, dtb, e_g, e_b], [(3 * HW, F32), (LANES, BF16)], [(1, LANES), (1, LANES)], 256,
        "gdn_prep_bwd")
    gsmall["a_log"] = dalog[:, :HEADS]
    gsmall["dt_bias"] = ddtb[:, :HEADS]

    def conv_bwd(i, n, dy_, nxt_, pa_, prev_, cw_):
        dpa = dy_ * cw_[CONV_K - 1:CONV_K, :]
        row = lax.broadcasted_iota(jnp.int32, (SUBLANES, dy_.shape[1]), 0)
        dcw = jnp.where(row == CONV_K - 1, _colsum(dy_ * pa_), 0.0)
        for j in range(CONV_K - 1):
            s = CONV_K - 1 - j
            dpa = dpa + _shift_up(dy_, nxt_, s, i, n) * cw_[j:j + 1, :]
            dcw = dcw + jnp.where(row == j, _colsum(dy_ * _shift_down(pa_, prev_, s, i)), 0.0)
        return dpa, dcw

    dpa, dcw = _rows(conv_bwd, [(dy_conv, "t"), (dy_conv, "n"), (pa, "t"), (pa, "p")], [conv_w],
                     [(3 * HW, BF16)], [(SUBLANES, 3 * HW)], 256, "gdn_conv_bwd")
    gsmall["conv_w"] = dcw[:CONV_K]

    dqz = jnp.concatenate([dpa, dz], axis=1)
    d_w_qz = _matmul(h2, dqz, "tn", BF16, "d_in_qz")
    d_w_ab = _matmul(h2, dab, "tn", BF16, "d_in_ab")
    d_w_qkvb = _matmul(h2, dpb, "tn", BF16, "d_in_qkvb")
    d_w_gg = _matmul(h2, dgg, "tn", BF16, "d_in_gates")
    gbig["w_in"] = jnp.concatenate([d_w_qz, d_w_ab[:, :2 * HEADS], d_w_qkvb, d_w_gg,
                                    jnp.zeros((D_MODEL, IN_PADDED - IN_COLS), BF16)], axis=1)
    dh2 = [_matmul(dqz, w_qz, "nt", BF16, "in_qz_dh"), _matmul(dab, w_ab, "nt", BF16, "in_ab_dh"),
           _matmul(dpb, w_qkvb, "nt", BF16, "in_qkvb_dh"), _matmul(dgg, w_gg, "nt", BF16, "in_gates_dh")]
    sent = comm.send("mixer", {n: gbig[n] for n in ("w_out", "w_branch_b", "w_branch_a", "w_in")})
    dx1, dy1, gsmall["mix_norm"] = _residual_norm_bwd(x1, small["mix_norm"] + sent, dh2, dx2, 0.5, "mix_norm_bwd")

    dh1, sent = _ffn_bwd(h1, gu1, act1, dy1, big["ffn1_w_gu"], big["ffn1_w_down"], "ffn1", comm)
    grad_x, _, gsmall["ffn1_norm"] = _residual_norm_bwd(x, small["ffn1_norm"] + sent, [dh1], dx1, 1.0,
                                                        "ffn1_norm_bwd")
    return loss, grad_x, gsmall


GATHER_GROUPS = {"ffn1": ("ffn1_w_gu",),
                 "ffn1_down": ("ffn1_w_down",),
                 "mixer": ("w_in_main", "w_in_edge"),
                 "branches": ("w_branch_a", "w_branch_b", "w_out"),
                 "tail": ("ffn2_w_gu", "ffn2_w_down", "ple_gate", "ple_proj")}
SPLIT_GATHERS = ("ffn1_down", "mixer", "branches", "tail")


def _kind(name):
    return "cols" if name in COL_SHARDED or name.startswith("w_in_") else "rows"


def _merge_w_in(main, edges):
    edge_w = WIN_W - WIN_STEP
    w_in = jnp.pad(main, ((0, 0), (0, edge_w)))
    for d in range(N_DEV):
        at = WIN_STEP * (d + 1)
        w_in = w_in + jnp.pad(edges[:, d * edge_w:(d + 1) * edge_w], ((0, 0), (at, IN_PADDED - at - edge_w)))
    return w_in


class _Fsdp:
    def __init__(self, wts, first):
        self.wts, self.first_token = wts, first
        main, edge = _roll_w_in(jnp.pad(wts["w_in"], ((0, 0), (0, WIN_W - IN_SHARD))))
        self.shards = {n: wts[n].astype(BF16) for n in BIG if n not in ("w_in", "ffn1_w_gu")}
        self.shards.update(w_in_main=main, w_in_edge=edge)
        self.lands = {n: _place_block(self.shards[n], _kind(n), "own_" + n)
                      for group in SPLIT_GATHERS for n in GATHER_GROUPS[group]}
        self.flight, self.sent = {}, {}

    def _gather_first(self, after):
        token = self.first_token + after[0, 0].astype(F32) * 0.0
        me = _index(*_me())
        for n, land in self.lands.items():
            r, c = self.shards[n].shape
            at = (me * r, 0) if _kind(n) == "rows" else (0, me * c)
            token = token + lax.dynamic_slice(land, at, (1, 1))[0, 0].astype(F32) * 0.0
        shard = (self.wts["ffn1_w_gu"] + token).astype(BF16)
        self.shards["ffn1_w_gu"] = shard
        first = _all_gather([shard], [_kind("ffn1_w_gu")], 1)[0]
        token = first[0, 0].astype(F32) * 0.0
        for group in SPLIT_GATHERS:
            names = GATHER_GROUPS[group]
            srcs = [self.shards[n] for n in names]
            lands = [self.lands[n] for n in names]
            make = _gather_copies([s.shape for s in srcs], [_kind(n) for n in names])
            srcs[0] = srcs[0] + token.astype(BF16)
            send_sems, recv_sems, srcs, lands, tok = _split_start(srcs, lands, make, "gather_start_" + group)
            token = token + tok[0, 0]
            self.flight[group] = (send_sems, recv_sems, srcs, lands, make)
        return {"ffn1_w_gu": first, "_token": token}

    def arrive(self, group, after):
        if group == "ffn1":
            return self._gather_first(after)
        send_sems, recv_sems, srcs, lands, make = self.flight[group]
        _, lands = _split_wait(send_sems, recv_sems, srcs, lands, after, make, "gather_wait_" + group)
        full = dict(zip(GATHER_GROUPS[group], lands))
        if group == "mixer":
            full["w_in"] = _merge_w_in(full.pop("w_in_main"), full.pop("w_in_edge"))
        return full

    def send(self, group, grads):
        names = list(grads)
        kinds = ["all" if n == "small" else "win" if n == "w_in" else _kind(n) for n in names]
        shapes = [grads[n].shape if n == "small" else (D_MODEL, WIN_W) if n == "w_in" else self.shards[n].shape
                  for n in names]
        srcs = [grads[n] for n in names]
        lands = [lax.empty((N_DEV,) + tuple(s), g.dtype) for s, g in zip(shapes, srcs)]
        make = _exchange_copies(shapes, kinds)
        send_sems, recv_sems, srcs, lands, tok = _split_start(srcs, lands, make, "grads_start_" + group)
        self.sent[group] = (names, kinds, send_sems, recv_sems, srcs, lands, make)
        return tok[0, 0]

    def received(self, group, after):
        names, kinds, send_sems, recv_sems, srcs, lands, make = self.sent[group]
        srcs, lands = _split_wait(send_sems, recv_sems, srcs, lands, after, make, "grads_wait_" + group)
        return {n: (k, g, r) for n, k, g, r in zip(names, kinds, srcs, lands)}


SMALL_ROWS = ("ffn1_norm", "mix_norm", "ffn2_norm", "ple_norm", "gdn_norm", "q_norm", "k_norm", "a_log", "dt_bias",
              "rel_bias", "conv_w")


def _pack_small(vals):
    rows = []
    for n in SMALL_ROWS:
        v = vals[n]
        if n == "rel_bias":
            v = jnp.pad(v, ((0, 0), (0, 2 * LANES - N_REL)))
        elif n in ("a_log", "dt_bias"):
            v = _pad_lanes(v)
        rows.append(v.reshape(-1, LANES))
    packed = jnp.concatenate(rows, axis=0)
    return jnp.pad(packed, ((0, -packed.shape[0] % SUBLANES), (0, 0)))


def _unpack_small(packed, shapes):
    out, off = {}, 0
    for n in SMALL_ROWS:
        shp = shapes[n]
        if n == "rel_bias":
            out[n] = packed[off:off + 2 * HEADS].reshape(HEADS, 2 * LANES)[:, :N_REL]
            off += 2 * HEADS
        elif n in ("a_log", "dt_bias"):
            out[n] = packed[off:off + 1, :HEADS]
            off += 1
        else:
            r = int(np.prod(shp)) // LANES
            out[n] = packed[off:off + r].reshape(shp)
            off += r
    return out


WEIGHTS = ("ffn1_norm", "ffn1_w_gu", "ffn1_w_down", "mix_norm", "w_in", "conv_w", "a_log", "dt_bias", "gdn_norm",
           "q_norm", "k_norm", "rel_bias", "w_branch_a", "w_branch_b", "w_out", "ffn2_norm", "ffn2_w_gu",
           "ffn2_w_down", "ple_norm", "ple_gate", "ple_proj")


def kernel(x, p, ffn1_norm, ffn1_w_gu, ffn1_w_down, mix_norm, w_in, conv_w, a_log, dt_bias, gdn_norm, q_norm, k_norm, rel_bias, w_branch_a, w_branch_b, w_out, ffn2_norm, ffn2_w_gu, ffn2_w_down, ple_norm, ple_gate, ple_proj, loss_target, m_ffn1_norm, m_ffn1_w_gu, m_ffn1_w_down, m_mix_norm, m_w_in, m_conv_w, m_a_log, m_dt_bias, m_gdn_norm, m_q_norm, m_k_norm, m_rel_bias, m_w_branch_a, m_w_branch_b, m_w_out, m_ffn2_norm, m_ffn2_w_gu, m_ffn2_w_down, m_ple_norm, m_ple_gate, m_ple_proj, v_ffn1_norm, v_ffn1_w_gu, v_ffn1_w_down, v_mix_norm, v_w_in, v_conv_w, v_a_log, v_dt_bias, v_gdn_norm, v_q_norm, v_k_norm, v_rel_bias, v_w_branch_a, v_w_branch_b, v_w_out, v_ffn2_norm, v_ffn2_w_gu, v_ffn2_w_down, v_ple_norm, v_ple_gate, v_ple_proj):
    args = dict(locals())
    def layer0(v):
        return v[0] if v.ndim == 3 else v

    wts = {n: layer0(args[n]) for n in WEIGHTS}
    mom = {n: layer0(args["m_" + n]) for n in WEIGHTS}
    var = {n: layer0(args["v_" + n]) for n in WEIGHTS}
    x2d, p2d, tgt = x[0], p[0, 0], loss_target[0]
    my_index = _index(*_me())

    small = {n: wts[n] for n in SMALL_ROWS if n != "conv_w"}
    conv_shard = wts["conv_w"]
    conv_cols = conv_shard.shape[1]
    conv_packed = jnp.zeros((SUBLANES, N_DEV * conv_cols), F32)
    conv_packed = lax.dynamic_update_slice(conv_packed, jnp.pad(conv_shard, ((0, SUBLANES - CONV_K), (0, 0))),
                                           (0, my_index * conv_cols))
    small["conv_w"] = _all_reduce_small(conv_packed.reshape(-1, LANES), "conv_w_gather").reshape(SUBLANES, -1)[:CONV_K]

    fsdp = _Fsdp(wts, small["conv_w"][0, 0] * 0.0)

    loss, grad_x, gsmall = _local_step(x2d, p2d, tgt, small, fsdp)
    loss = lax.psum(loss, ("x", "y", "c"))

    fsdp.send("small", {"small": _pack_small(gsmall)})

    outs_big, after = {}, grad_x
    for group in list(fsdp.sent):
        for n, (kind, grad, recv) in fsdp.received(group, after).items():
            if n == "small":
                small_sum = _sum_small(recv, grad)
            elif n == "w_in":
                g_in = _sum_w_in_windows(recv, grad)[:, :IN_SHARD]
                outs_big[n] = [g_in] + list(_adamw_small(wts[n], g_in, mom[n], var[n], "adamw_w_in"))
            else:
                outs_big[n] = _adamw_recv(recv, grad, kind, wts[n], mom[n], var[n], "adamw_" + n)
            after = small_sum if n == "small" else outs_big[n][1]

    small_shapes = {n: (small[n].shape if n != "conv_w" else (CONV_K, N_DEV * conv_cols)) for n in SMALL_ROWS}
    gsum = _unpack_small(small_sum, small_shapes)
    gsum["conv_w"] = lax.dynamic_slice(gsum["conv_w"], (0, my_index * conv_cols), (CONV_K, conv_cols))
    rep = [n for n in SMALL_ROWS if n != "conv_w"]
    rep_shapes = {n: small_shapes[n] for n in rep}

    def pack_rep(vals):
        return _pack_small({**{n: vals[n] for n in rep}, "conv_w": jnp.zeros((CONV_K, LANES), F32)})

    def unpack_rep(packed):
        return _unpack_small(packed, {**rep_shapes, "conv_w": (CONV_K, LANES)})

    outs_small = [unpack_rep(o) for o in _adamw_small(pack_rep(wts), pack_rep(gsum), pack_rep(mom), pack_rep(var),
                                                      "adamw_replicated")]
    pad8 = functools.partial(jnp.pad, pad_width=((0, SUBLANES - CONV_K), (0, 0)))
    outs_conv = [o[:CONV_K] for o in _adamw_small(pad8(conv_shard), pad8(gsum["conv_w"]), pad8(mom["conv_w"]),
                                                   pad8(var["conv_w"]), "adamw_conv")]

    def leaf(kind, n):
        if n in BIG:
            return outs_big[n][kind][None]
        if n == "conv_w":
            return (gsum["conv_w"] if kind == 0 else outs_conv[kind - 1])[None]
        return (gsum[n] if kind == 0 else outs_small[kind - 1][n]).reshape(args[n].shape)

    result = [loss, grad_x[None]]
    for kind in range(4):
        result += [leaf(kind, n) for n in WEIGHTS]
    return tuple(result)
```

```python
import functools

import numpy as np
import jax
import jax.numpy as jnp
from jax import lax
from jax.experimental import pallas as pl
from jax.experimental.pallas import tpu as pltpu

F32 = jnp.float32
BF16 = jnp.bfloat16
HIGHEST = lax.Precision.HIGHEST
MESH = pl.DeviceIdType.MESH

D_MODEL = 2048
D_FF = 5632
HEADS = 8
HEAD_DIM = 128
HW = HEADS * HEAD_DIM
CHUNK = 64
LEFT_CHUNKS = 8
MAX_REL = 128
N_REL = (CHUNK - 1) + MAX_REL + 1
CONV_K = 4
EPS = 1e-6
NEG_INF = -1e30
N_DEV = 8
LANES = 128
SUBLANES = 8
VMEM_LIMIT = 56 * 1024 * 1024

MATMUL_WHOLE_K = 2048

ATT_QB = 256
ATT_KW = ATT_QB + LEFT_CHUNKS * CHUNK
ATT_PAD = LEFT_CHUNKS * CHUNK
GDN_CB = 8
GDN_GROUP = 32
GDN_SCAN_UNROLL = 4

ADAM_LR = 0.001
ADAM_B1 = 0.9
ADAM_B2 = 0.999
ADAM_EPS = 1e-08
ADAM_WD = 0.01
ADAM_STEP = 10

IN_QZ = 3 * HW + HW
IN_AB0 = IN_QZ
IN_QKVB0 = IN_AB0 + 2 * HEADS
IN_GG0 = IN_QKVB0 + 3 * HW
IN_COLS = IN_GG0 + 2 * D_MODEL

BIG = ("ffn1_w_gu", "ffn1_w_down", "w_in", "w_branch_a", "w_branch_b", "w_out",
       "ffn2_w_gu", "ffn2_w_down", "ple_gate", "ple_proj")
COL_SHARDED = ("ffn1_w_gu", "w_in", "w_branch_a", "w_branch_b", "ffn2_w_gu", "ple_proj")


def _params(semantics=None, **kw):
    return pltpu.CompilerParams(dimension_semantics=semantics, vmem_limit_bytes=VMEM_LIMIT, **kw)


def _pick(n, cands):
    for c in cands:
        if n % c == 0:
            return c
    return n


SMEM_SPEC = pl.BlockSpec(memory_space=pltpu.SMEM)


def _after(token):
    return ([], []) if token is None else ([SMEM_SPEC], [jnp.reshape(token, (1,)).astype(F32)])


def _matmul(a, b, mode, out_dtype, name, after=None):
    halves = (a.ndim == 3 and mode == "nt") or (b.ndim == 3 and mode == "tn")
    if mode == "nn":
        (m, k), (k2, n) = a.shape, b.shape
    elif mode == "nt":
        (m, k), (n, k2) = (a.shape[-2], a.shape[-1] * (a.ndim - 1)), b.shape
    else:
        (k, m), (k2, n) = a.shape, (b.shape[-2], b.shape[-1] * (b.ndim - 1))
    assert k == k2 and a.ndim + b.ndim == (5 if halves else 4), (a.shape, b.shape, mode)
    tm = _pick(m, (1024, 512, 256, 128))
    if halves and mode == "tn":
        tn = _pick(n // 2, (1408, 1024, 512, 256, 128))
        tk = _pick(k, (2048, 1024, 512, 256, 128))
    elif halves:
        tn = _pick(n, (1024, 512, 256, 128))
        tk = _pick(k // 2, (2816, 2048, 1536, 1024, 512, 256, 128))
    else:
        tn = _pick(n, (1024, 512, 256, 1408, 128))
        tk = k if k <= MATMUL_WHOLE_K else _pick(k, (2816, 2048, 1536, 1024, 512, 256, 1408, 128))
    nk = k // tk
    per_half = (n // 2) // tn if mode == "tn" else (k // 2) // tk
    if mode == "nn":
        a_spec = pl.BlockSpec((tm, tk), lambda i, j, kk: (i, kk))
        b_spec = pl.BlockSpec((tk, tn), lambda i, j, kk: (kk, j))
        dims = (((1,), (0,)), ((), ()))
    elif mode == "nt":
        a_spec = pl.BlockSpec((tm, tk), lambda i, j, kk: (i, kk))
        b_spec = pl.BlockSpec((tn, tk), lambda i, j, kk: (j, kk))
        dims = (((1,), (1,)), ((), ()))
        if halves:
            a_spec = pl.BlockSpec((None, tm, tk), lambda i, j, kk: (kk // per_half, i, kk % per_half))
    else:
        a_spec = pl.BlockSpec((tk, tm), lambda i, j, kk: (kk, i))
        b_spec = pl.BlockSpec((tk, tn), lambda i, j, kk: (kk, j))
        dims = (((0,), (0,)), ((), ()))
        if halves:
            b_spec = pl.BlockSpec((None, tk, tn), lambda i, j, kk: (j // per_half, kk, j % per_half))

    after_specs, after_args = _after(after)

    def body(a_ref, b_ref, *rest):
        o_ref, acc = rest[len(after_args)], rest[len(after_args) + 1:]
        prod = lax.dot_general(a_ref[...].astype(BF16), b_ref[...].astype(BF16), dims, preferred_element_type=F32)
        if nk == 1:
            o_ref[...] = prod.astype(o_ref.dtype)
            return
        acc_ref, kk = acc[0], pl.program_id(2)

        @pl.when(kk == 0)
        def _():
            acc_ref[...] = prod

        @pl.when((kk > 0) & (kk < nk - 1))
        def _():
            acc_ref[...] += prod

        @pl.when(kk == nk - 1)
        def _():
            o_ref[...] = (acc_ref[...] + prod).astype(o_ref.dtype)

    return pl.pallas_call(
        body, name=name,
        out_shape=jax.ShapeDtypeStruct((m, n), out_dtype),
        grid=(m // tm, n // tn, nk),
        in_specs=[a_spec, b_spec] + after_specs,
        out_specs=pl.BlockSpec((tm, tn), lambda i, j, kk: (i, j)),
        scratch_shapes=[pltpu.VMEM((tm, tn), F32)] if nk > 1 else [],
        compiler_params=_params(("parallel", "parallel", "arbitrary")),
    )(a, b, *after_args)


def _rows(fn, row_ins, consts, row_outs, acc_outs, tile, name):
    t_rows = row_ins[0][0].shape[0]
    tile = min(tile, t_rows)
    assert t_rows % tile == 0 and tile % SUBLANES == 0
    n = t_rows // tile
    per = tile // SUBLANES
    last8 = t_rows // SUBLANES - 1
    in_specs = []
    for arr, kind in row_ins:
        c = arr.shape[1]
        if kind == "t":
            in_specs.append(pl.BlockSpec((tile, c), lambda i: (i, 0)))
        elif kind == "p":
            in_specs.append(pl.BlockSpec((SUBLANES, c), lambda i: (jnp.maximum(i * per - 1, 0), 0)))
        else:
            in_specs.append(pl.BlockSpec((SUBLANES, c), lambda i: (jnp.minimum((i + 1) * per, last8), 0)))
    for arr in consts:
        in_specs.append(pl.BlockSpec(arr.shape, lambda i, nd=arr.ndim: (0,) * nd))
    out_shape = [jax.ShapeDtypeStruct((t_rows, c), dt) for c, dt in row_outs]
    out_specs = [pl.BlockSpec((tile, c), lambda i: (i, 0)) for c, _ in row_outs]
    for shp in acc_outs:
        out_shape.append(jax.ShapeDtypeStruct(shp, F32))
        out_specs.append(pl.BlockSpec(shp, lambda i, nd=len(shp): (0,) * nd))
    n_in = len(row_ins) + len(consts)
    n_row_out = len(row_outs)

    def body(*refs):
        i = pl.program_id(0)
        vals = [r[...].astype(F32) for r in refs[:len(row_ins)]]
        res = fn(i, n, *vals, *refs[len(row_ins):n_in])
        outs = refs[n_in:]
        for r, v in zip(outs[:n_row_out], res[:n_row_out]):
            r[...] = v.astype(r.dtype)
        if acc_outs:
            @pl.when(i == 0)
            def _():
                for r in outs[n_row_out:]:
                    r[...] = jnp.zeros_like(r)

            for r, v in zip(outs[n_row_out:], res[n_row_out:]):
                r[...] += v

    res = pl.pallas_call(
        body, name=name, out_shape=out_shape, grid=(n,), in_specs=in_specs, out_specs=out_specs,
        compiler_params=_params(("arbitrary",) if acc_outs else ("parallel",)),
    )(*[a for a, _ in row_ins], *consts)
    return res


def _rms(x, w):
    return x * lax.rsqrt(jnp.mean(x * x, axis=-1, keepdims=True) + EPS) * w


def _l2n(x):
    return x * lax.rsqrt(jnp.sum(x * x, axis=-1, keepdims=True) + EPS)


def _sigmoid(x):
    return 1.0 / (1.0 + jnp.exp(-x))


def _silu(x):
    return x * _sigmoid(x)


def _softplus(x):
    return jnp.maximum(x, 0.0) + jnp.log(1.0 + jnp.exp(-jnp.abs(x)))


def _heads(fn, *xs):
    nh = xs[0].shape[1] // HEAD_DIM
    return jnp.concatenate(
        [fn(*[x[:, h * HEAD_DIM:(h + 1) * HEAD_DIM] for x in xs]) for h in range(nh)], axis=1)


def _colsum(x):
    return jnp.sum(x, axis=0, keepdims=True)


def _gated_norm(o, z, w):
    return _heads(lambda oh, zh: _rms(oh, w) * _silu(zh), o, z)


def _mix(gg, ta, tb):
    return _sigmoid(gg[:, :D_MODEL]) * ta + _sigmoid(gg[:, D_MODEL:]) * tb


def _gdn_post(y):
    a = _silu(y)
    q = _heads(lambda v: _l2n(v) * (HEAD_DIM ** -0.5), a[:, :HW])
    k = _heads(_l2n, a[:, HW:2 * HW])
    return q, k, a[:, 2 * HW:]


NN = (((1,), (0,)), ((), ()))
NT = (((1,), (1,)), ((), ()))
TN = (((0,), (0,)), ((), ()))


def _dg(a, b, dims):
    return lax.dot_general(a, b, dims, preferred_element_type=F32)


def _split2(x):
    hi = x.astype(BF16)
    return hi, (x - hi.astype(F32)).astype(BF16)


def _split3(x):
    hi = x.astype(BF16)
    r = x - hi.astype(F32)
    mid = r.astype(BF16)
    return hi, mid, (r - mid.astype(F32)).astype(BF16)


def _dg3(a, b, dims):
    ah, al = _split2(a)
    bh, bl = _split2(b)
    return _dg(ah, bh, dims) + (_dg(ah, bl, dims) + _dg(al, bh, dims))


BNN = (((2,), (1,)), ((0,), (0,)))
BNT = (((2,), (2,)), ((0,), (0,)))
BTN = (((1,), (1,)), ((0,), (0,)))


@jax.custom_vjp
def _mm3(a, b):
    return _dg3(a, b, BNN)


_mm3.defvjp(lambda a, b: (_dg3(a, b, BNN), (a, b)),
            lambda res, g: (_dg3(g, res[1], BNT), _dg3(res[0], g, BTN)))


def _xm(x, m, dims):
    mb = m.astype(BF16)
    parts = _split3(x)
    return _dg(parts[0], mb, dims) + (_dg(parts[1], mb, dims) + _dg(parts[2], mb, dims))


def _mx(m, x, dims):
    mb = m.astype(BF16)
    parts = _split3(x)
    return _dg(mb, parts[0], dims) + (_dg(mb, parts[1], dims) + _dg(mb, parts[2], dims))


@jax.custom_vjp
def _times_const(x, m):
    return _xm(x, m, NN)


_times_const.defvjp(lambda x, m: (_xm(x, m, NN), m),
                    lambda m, g: (_xm(g, m, NT), jnp.zeros_like(m)))


@jax.custom_vjp
def _const_times(m, x):
    return _mx(m, x, NN)


_const_times.defvjp(lambda m, x: (_mx(m, x, NN), m),
                    lambda m, g: (jnp.zeros_like(m), _mx(m, g, TN)))


@jax.custom_vjp
def _lane_mean_cols(x, avg):
    return _mx(avg, x, BNT)


_lane_mean_cols.defvjp(lambda x, avg: (_mx(avg, x, BNT), avg),
                       lambda avg, g: (_xm(g, avg, BTN), jnp.zeros_like(avg)))


def _gdn_gates(ab, alog, dtb, e_g, e_b):
    t = ab.shape[0]
    g = -jnp.exp(alog) * _softplus(ab + dtb)
    beta = _sigmoid(ab)
    ri = lax.broadcasted_iota(jnp.int32, (t, t), 0)
    ci = lax.broadcasted_iota(jnp.int32, (t, t), 1)
    shift = CHUNK.bit_length() - 1
    same = jnp.right_shift(ri, shift) == jnp.right_shift(ci, shift)
    tril = jnp.where(same & (ri >= ci), 1.0, 0.0).astype(F32)
    gc = _const_times(tril, g)
    return _times_const(gc, e_g), _times_const(beta, e_b)


def _shift_down(x, halo, s, i):
    if s == 0:
        return x
    halo = jnp.where(i == 0, 0.0, halo)
    xr = pltpu.roll(x, s, 0)
    hr = pltpu.roll(halo, s, 0)
    row = lax.broadcasted_iota(jnp.int32, (SUBLANES, x.shape[1]), 0)
    top = jnp.where(row < s, hr, xr[:SUBLANES])
    return jnp.concatenate([top, xr[SUBLANES:]], axis=0)


def _shift_up(x, halo, s, i, n):
    if s == 0:
        return x
    t = x.shape[0]
    halo = jnp.where(i == n - 1, 0.0, halo)
    xr = pltpu.roll(x, t - s, 0)
    hr = pltpu.roll(halo, SUBLANES - s, 0)
    row = lax.broadcasted_iota(jnp.int32, (SUBLANES, x.shape[1]), 0)
    bot = jnp.where(row >= SUBLANES - s, hr, xr[t - SUBLANES:])
    return jnp.concatenate([xr[:t - SUBLANES], bot], axis=0)


def _conv(pa, prev, cw_ref, i):
    y = pa * cw_ref[CONV_K - 1:CONV_K, :]
    for j in range(CONV_K - 1):
        y = y + _shift_down(pa, prev, CONV_K - 1 - j, i) * cw_ref[j:j + 1, :]
    return y


def _dot_nt(a, b, precision=None):
    return lax.dot_general(a, b, (((1,), (1,)), ((), ())), precision=precision, preferred_element_type=F32)


def _dot_tn(a, b, precision=None):
    return lax.dot_general(a, b, (((0,), (0,)), ((), ())), precision=precision, preferred_element_type=F32)


def _dot(a, b, precision=None):
    return jnp.dot(a, b, precision=precision, preferred_element_type=F32)


def _bf(x):
    return x.astype(BF16)


def _neumann_inverse(lmat):
    nb, c, _ = lmat.shape
    ri = lax.broadcasted_iota(jnp.int32, (nb, c, c), 1)
    ci = lax.broadcasted_iota(jnp.int32, (nb, c, c), 2)
    pw = -lmat
    inv = jnp.where(ri == ci, 1.0, 0.0).astype(F32) + pw
    for _ in range(5):
        pw = _mm3(pw, pw)
        inv = inv + _mm3(inv, pw)
    return inv


@jax.custom_vjp
def _unit_lower_inverse(lmat):
    return _neumann_inverse(lmat)


def _unit_lower_inverse_fwd(lmat):
    inv = _neumann_inverse(lmat)
    return inv, inv


def _unit_lower_inverse_bwd(inv, g):
    return (-_dg3(_dg3(inv, g, BTN), inv, BNT),)


_unit_lower_inverse.defvjp(_unit_lower_inverse_fwd, _unit_lower_inverse_bwd)


def _gdn_chunk(q, k, v, gc, bb):
    nb, c, _ = q.shape
    ri = lax.broadcasted_iota(jnp.int32, (nb, c, c), 1)
    ci = lax.broadcasted_iota(jnp.int32, (nb, c, c), 2)
    incl = ri >= ci
    strict = ri > ci
    g_row = gc[:, :, :c]
    g_col = _lane_mean_cols(gc, jnp.full((nb, c, LANES), 1.0 / LANES, F32))
    decay = jnp.where(incl, jnp.exp(jnp.where(incl, g_row - g_col, 0.0)), 0.0)
    kb = k * bb
    lmat = jnp.where(strict, _dg(_bf(kb), _bf(k), BNT) * decay, 0.0)
    inv = _unit_lower_inverse(lmat)
    egc = jnp.exp(gc)
    u = _mm3(inv, v * bb)
    w = _mm3(inv, kb * egc)
    aqk = _dg(_bf(q), _bf(k), BNT) * decay
    last = lax.broadcasted_iota(jnp.int32, (nb, c, LANES), 1) == c - 1
    tot = jnp.sum(jnp.where(last, gc, 0.0), axis=1, keepdims=True)
    k_tail = k * jnp.exp(tot - gc)
    tail = jnp.broadcast_to(jnp.exp(tot), (nb, SUBLANES, LANES))
    return u, w, aqk, q * egc, k_tail, tail


def _gdn_intra(qn, kn, vv, g_b, beta_b):
    t_rows = qn.shape[0]
    nc = t_rows // CHUNK
    cb = min(GDN_GROUP, nc)
    rows = cb * CHUNK
    col = pl.BlockSpec((rows, HEAD_DIM), lambda h, b: (b, h))

    def body(q_ref, k_ref, v_ref, g_ref, b_ref, u_ref, w_ref, a_ref, qd_ref, kt_ref, tl_ref):
        def group(gi, carry):
            r = pl.ds(pl.multiple_of(gi * (grp * CHUNK), grp * CHUNK), grp * CHUNK)
            ins = [ref[r, :].reshape(grp, CHUNK, HEAD_DIM) for ref in (q_ref, k_ref, v_ref, g_ref, b_ref)]
            u, w, aqk, qd, kt, tl = _gdn_chunk(*ins)
            for ref, val in ((u_ref, u), (w_ref, w), (qd_ref, qd), (kt_ref, kt)):
                ref[r, :] = val.reshape(grp * CHUNK, HEAD_DIM)
            a_ref[0, r, :] = aqk.reshape(grp * CHUNK, CHUNK)
            tl_ref[0, pl.ds(gi * grp, grp)] = tl
            return carry

        grp = min(GDN_GROUP, cb)
        lax.fori_loop(0, cb // grp, group, 0)

    full = jax.ShapeDtypeStruct((t_rows, HW), F32)
    return pl.pallas_call(
        body, name="gdn_intra_fwd",
        out_shape=[full, full, jax.ShapeDtypeStruct((HEADS, t_rows, CHUNK), F32), full, full,
                   jax.ShapeDtypeStruct((HEADS, nc, SUBLANES, LANES), F32)],
        grid=(HEADS, nc // cb),
        in_specs=[col] * 5,
        out_specs=[col, col, pl.BlockSpec((1, rows, CHUNK), lambda h, b: (h, b, 0)), col, col,
                   pl.BlockSpec((1, cb, SUBLANES, LANES), lambda h, b: (h, b, 0, 0))],
        compiler_params=_params(("parallel", "parallel")),
    )(qn, kn, vv, g_b, beta_b)


def _gdn_intra_bwd(qn, kn, vv, g_b, beta_b, du, dw, da, dqd, dkt, dtl):
    t_rows = qn.shape[0]
    nc = t_rows // CHUNK
    cb = min(GDN_GROUP, nc)
    rows = cb * CHUNK
    col = pl.BlockSpec((rows, HEAD_DIM), lambda h, b: (b, h))
    a_spec = pl.BlockSpec((1, rows, CHUNK), lambda h, b: (h, b, 0))
    tl_spec = pl.BlockSpec((1, cb, SUBLANES, LANES), lambda h, b: (h, b, 0, 0))

    def body(q_ref, k_ref, v_ref, g_ref, b_ref, du_ref, dw_ref, da_ref, dqd_ref, dkt_ref, dtl_ref,
             dq_ref, dk_ref, dv_ref, dg_ref, db_ref):
        def group(gi, carry):
            r = pl.ds(pl.multiple_of(gi * (grp * CHUNK), grp * CHUNK), grp * CHUNK)
            wide = (grp, CHUNK, HEAD_DIM)
            ins = [ref[r, :].reshape(wide) for ref in (q_ref, k_ref, v_ref, g_ref, b_ref)]
            cts = (du_ref[r, :].reshape(wide), dw_ref[r, :].reshape(wide),
                   da_ref[0, r, :].reshape(grp, CHUNK, CHUNK), dqd_ref[r, :].reshape(wide),
                   dkt_ref[r, :].reshape(wide), dtl_ref[0, pl.ds(gi * grp, grp)])
            grads = jax.vjp(_gdn_chunk, *ins)[1](cts)
            for ref, val in zip((dq_ref, dk_ref, dv_ref, dg_ref, db_ref), grads):
                ref[r, :] = val.reshape(grp * CHUNK, HEAD_DIM)
            return carry

        grp = min(GDN_GROUP, cb)
        lax.fori_loop(0, cb // grp, group, 0)

    full = jax.ShapeDtypeStruct((t_rows, HW), F32)
    return pl.pallas_call(
        body, name="gdn_intra_bwd",
        out_shape=[full] * 5,
        grid=(HEADS, nc // cb),
        in_specs=[col] * 7 + [a_spec, col, col, tl_spec],
        out_specs=[col] * 5,
        compiler_params=_params(("parallel", "parallel")),
    )(qn, kn, vv, g_b, beta_b, du, dw, da, dqd, dkt, dtl)


def _head_cols(h):
    return slice(h * HEAD_DIM, (h + 1) * HEAD_DIM)


def _gdn_scan(u, w, aqk, qd, kt, tl):
    t_rows = u.shape[0]
    nc = t_rows // CHUNK
    cb = min(GDN_CB, nc)
    rows = cb * CHUNK
    wide = pl.BlockSpec((rows, HW), lambda b: (b, 0))

    def body(u_ref, w_ref, a_ref, qd_ref, kt_ref, tl_ref, o_ref, s_out_ref, s_ref):
        @pl.when(pl.program_id(0) == 0)
        def _():
            s_ref[...] = jnp.zeros_like(s_ref)

        def chunk(ci, carry):
            r = pl.ds(pl.multiple_of(ci * CHUNK, CHUNK), CHUNK)
            for h in range(HEADS):
                hc = _head_cols(h)
                s = s_ref[h]
                s_out_ref[ci, h] = s
                sb = _bf(s)
                vn = u_ref[r, hc] - _dot(_bf(w_ref[r, hc]), sb)
                vnb = _bf(vn)
                o_ref[r, hc] = _dot(_bf(qd_ref[r, hc]), sb) + _dot(_bf(a_ref[h, r, :]), vnb)
                s_ref[h] = s * tl_ref[h, ci, 0:1, :] + _dot_tn(_bf(kt_ref[r, hc]), vnb)
            return carry

        lax.fori_loop(0, cb, chunk, 0, unroll=GDN_SCAN_UNROLL)

    return pl.pallas_call(
        body, name="gdn_scan_fwd",
        out_shape=[jax.ShapeDtypeStruct((t_rows, HW), F32),
                   jax.ShapeDtypeStruct((nc, HEADS, HEAD_DIM, HEAD_DIM), F32)],
        grid=(nc // cb,),
        in_specs=[wide, wide, pl.BlockSpec((HEADS, rows, CHUNK), lambda b: (0, b, 0)), wide, wide,
                  pl.BlockSpec((HEADS, cb, SUBLANES, LANES), lambda b: (0, b, 0, 0))],
        out_specs=[wide, pl.BlockSpec((cb, HEADS, HEAD_DIM, HEAD_DIM), lambda b: (b, 0, 0, 0))],
        scratch_shapes=[pltpu.VMEM((HEADS, HEAD_DIM, HEAD_DIM), F32)],
        compiler_params=_params(("arbitrary",)),
    )(u, w, aqk, qd, kt, tl)


def _gdn_scan_bwd(do, u, w, aqk, qd, kt, tl, states):
    t_rows = u.shape[0]
    nc = t_rows // CHUNK
    cb = min(GDN_CB, nc)
    rows = cb * CHUNK
    nb = nc // cb
    wide = pl.BlockSpec((rows, HW), lambda b: (nb - 1 - b, 0))
    a_spec = pl.BlockSpec((HEADS, rows, CHUNK), lambda b: (0, nb - 1 - b, 0))
    tl_spec = pl.BlockSpec((HEADS, cb, SUBLANES, LANES), lambda b: (0, nb - 1 - b, 0, 0))

    def body(do_ref, u_ref, w_ref, a_ref, qd_ref, kt_ref, tl_ref, s_in_ref,
             du_ref, dw_ref, da_ref, dqd_ref, dkt_ref, dtl_ref, ds_ref):
        @pl.when(pl.program_id(0) == 0)
        def _():
            ds_ref[...] = jnp.zeros_like(ds_ref)

        row0 = lax.broadcasted_iota(jnp.int32, (SUBLANES, LANES), 0) == 0

        def chunk(step, carry):
            ci = cb - 1 - step
            r = pl.ds(pl.multiple_of(ci * CHUNK, CHUNK), CHUNK)
            for h in range(HEADS):
                hc = _head_cols(h)
                s = s_in_ref[ci, h]
                ds_next = ds_ref[h]
                sb, dsb = _bf(s), _bf(ds_next)
                wb, ab, ktb, qdb = _bf(w_ref[r, hc]), _bf(a_ref[h, r, :]), _bf(kt_ref[r, hc]), _bf(qd_ref[r, hc])
                dob = _bf(do_ref[r, hc])
                vn = u_ref[r, hc] - _dot(wb, sb)
                vnb = _bf(vn)
                dvn = _dot_tn(ab, dob) + _dot(ktb, dsb)
                dvnb = _bf(dvn)
                du_ref[r, hc] = dvn
                dw_ref[r, hc] = -_dot_nt(dvnb, sb)
                da_ref[h, r, :] = _dot_nt(dob, vnb)
                dqd_ref[r, hc] = _dot_nt(dob, sb)
                dkt_ref[r, hc] = _dot_nt(vnb, dsb)
                dtl_ref[h, ci] = jnp.where(row0, _colsum(s * ds_next), 0.0)
                ds_ref[h] = _dot_tn(qdb, dob) + ds_next * tl_ref[h, ci, 0:1, :] - _dot_tn(wb, dvnb)
            return carry

        lax.fori_loop(0, cb, chunk, 0, unroll=GDN_SCAN_UNROLL)

    full = jax.ShapeDtypeStruct((t_rows, HW), F32)
    return pl.pallas_call(
        body, name="gdn_scan_bwd",
        out_shape=[full, full, jax.ShapeDtypeStruct((HEADS, t_rows, CHUNK), F32), full, full,
                   jax.ShapeDtypeStruct((HEADS, nc, SUBLANES, LANES), F32)],
        grid=(nb,),
        in_specs=[wide, wide, wide, a_spec, wide, wide, tl_spec,
                  pl.BlockSpec((cb, HEADS, HEAD_DIM, HEAD_DIM), lambda b: (nb - 1 - b, 0, 0, 0))],
        out_specs=[wide, wide, a_spec, wide, wide, tl_spec],
        scratch_shapes=[pltpu.VMEM((HEADS, HEAD_DIM, HEAD_DIM), F32)],
        compiler_params=_params(("arbitrary",)),
    )(do, u, w, aqk, qd, kt, tl, states)


def _att_profile_index():
    j = lax.broadcasted_iota(jnp.int32, (SUBLANES, ATT_KW), 1)
    return jnp.clip(ATT_PAD - j, -(CHUNK - 1), MAX_REL) + (CHUNK - 1)


def _att_far_back():
    qi = lax.broadcasted_iota(jnp.int32, (ATT_QB, ATT_KW), 0)
    kj = lax.broadcasted_iota(jnp.int32, (ATT_QB, ATT_KW), 1)
    return kj < qi


def _rotate_rows(x, forward):
    rows, lanes = x.shape
    row = lax.broadcasted_iota(jnp.int32, x.shape, 0)
    for bit in range(rows.bit_length() - 1):
        amount = (1 << bit) if forward else lanes - (1 << bit)
        x = jnp.where(jnp.bitwise_and(jnp.right_shift(row, bit), 1) == 1, pltpu.roll(x, amount, 1), x)
    return x


def _att_in_band():
    qi = lax.broadcasted_iota(jnp.int32, (ATT_QB, ATT_KW), 0)
    kj = lax.broadcasted_iota(jnp.int32, (ATT_QB, ATT_KW), 1)
    shift = CHUNK.bit_length() - 1
    qc = jnp.right_shift(qi, shift)
    kc = jnp.right_shift(kj, shift) - LEFT_CHUNKS
    return (kc <= qc) & (kc >= qc - LEFT_CHUNKS)


def _att_valid(b):
    kj = lax.broadcasted_iota(jnp.int32, (1, ATT_KW), 1)
    return jnp.where(kj + b * ATT_QB >= ATT_PAD, 0.0, NEG_INF)


def _rms_parts(x, w):
    r = lax.rsqrt(jnp.mean(x * x, axis=-1, keepdims=True) + EPS)
    xn = x * r
    return xn * w, xn, r


def _rms_bwd(dy, xn, r, w):
    dxn = dy * w
    dx = r * (dxn - xn * jnp.mean(dxn * xn, axis=-1, keepdims=True))
    return dx, _colsum(dy * xn)


def _att_probs(qb, kb, bias, before_start):
    s = _dot_nt(qb, kb) * (HEAD_DIM ** -0.5) + bias + before_start
    e = jnp.exp(s - jnp.max(s, axis=-1, keepdims=True))
    return e * (1.0 / jnp.sum(e, axis=-1, keepdims=True))


def _att_specs():
    q_spec = pl.BlockSpec((ATT_QB, HEAD_DIM), lambda h, b: (b, h))
    back = ATT_PAD // ATT_QB
    k_specs = [pl.BlockSpec((ATT_QB, HEAD_DIM), lambda h, b, j=j: (jnp.maximum(b + j - back, 0), HEADS + h))
               for j in range(3)]
    v_specs = [pl.BlockSpec((ATT_QB, HEAD_DIM), lambda h, b, j=j: (jnp.maximum(b + j - back, 0), 2 * HEADS + h))
               for j in range(3)]
    w_spec = pl.BlockSpec((1, HEAD_DIM), lambda h, b: (0, 0))
    smem = pl.BlockSpec(memory_space=pltpu.SMEM)
    return q_spec, k_specs, v_specs, w_spec, smem


BIAS_SPEC = pl.BlockSpec((1, ATT_QB, ATT_KW), lambda h, b: (h, 0, 0))


def _expand_rel_bias(rel):
    def body(rel_ref, bias_ref):
        h = pl.program_id(0)
        idx = _att_profile_index()

        def fill(r, acc):
            return jnp.where(idx == r, rel_ref[h, r], acc)

        profile = lax.fori_loop(0, N_REL, fill, jnp.zeros((SUBLANES, ATT_KW), F32))
        table = _rotate_rows(jnp.concatenate([profile] * (ATT_QB // SUBLANES), axis=0), True)
        table = jnp.where(_att_far_back(), rel_ref[h, N_REL - 1], table)
        bias_ref[0] = jnp.where(_att_in_band(), table, NEG_INF)

    return pl.pallas_call(
        body, name="rel_bias_expand",
        out_shape=jax.ShapeDtypeStruct((HEADS, ATT_QB, ATT_KW), F32), grid=(HEADS,),
        in_specs=[pl.BlockSpec(memory_space=pltpu.SMEM)],
        out_specs=pl.BlockSpec((1, ATT_QB, ATT_KW), lambda h: (h, 0, 0)),
        compiler_params=_params(("parallel",)),
    )(rel)


def _attention(pb, qw, kw, bias):
    t_rows = pb.shape[0]
    q_spec, k_specs, v_specs, w_spec, _ = _att_specs()

    def body(q_ref, k0, k1, k2, v0, v1, v2, qw_ref, kw_ref, bias_ref, o_ref):
        b = pl.program_id(1)
        kwin = jnp.concatenate([k0[...], k1[...], k2[...]], axis=0)
        vwin = jnp.concatenate([v0[...], v1[...], v2[...]], axis=0)
        q = _rms(q_ref[...], qw_ref[...])
        k = _rms(kwin, kw_ref[...])
        p = _att_probs(_bf(q), _bf(k), bias_ref[0], _att_valid(b))
        o_ref[...] = _dot(_bf(p), _bf(vwin)).astype(o_ref.dtype)

    return pl.pallas_call(
        body, name="band_attention_fwd",
        out_shape=jax.ShapeDtypeStruct((t_rows, HW), BF16),
        grid=(HEADS, t_rows // ATT_QB),
        in_specs=[q_spec] + k_specs + v_specs + [w_spec, w_spec, BIAS_SPEC],
        out_specs=pl.BlockSpec((ATT_QB, HEAD_DIM), lambda h, b: (b, h)),
        compiler_params=_params(("parallel", "arbitrary")),
    )(pb, pb, pb, pb, pb, pb, pb, qw, kw, bias)


def _attention_bwd(pb, qw, kw, bias, dyb):
    t_rows = pb.shape[0]
    nb = t_rows // ATT_QB
    q_spec, k_specs, v_specs, w_spec, smem = _att_specs()
    pad_rows = t_rows + ATT_PAD
    acc_spec = pl.BlockSpec((pad_rows, HEAD_DIM), lambda h, b: (0, h))

    def body(q_ref, k0, k1, k2, v0, v1, v2, qw_ref, kw_ref, bias_ref, do_ref,
             dq_ref, dk_ref, dv_ref, dqw_ref, dkw_ref, drel_ref, dbias_ref):
        h, b = pl.program_id(0), pl.program_id(1)

        @pl.when(b == 0)
        def _():
            dbias_ref[...] = jnp.zeros_like(dbias_ref)
            dk_ref[...] = jnp.zeros_like(dk_ref)
            dv_ref[...] = jnp.zeros_like(dv_ref)

        @pl.when((b == 0) & (h == 0))
        def _():
            dqw_ref[...] = jnp.zeros_like(dqw_ref)
            dkw_ref[...] = jnp.zeros_like(dkw_ref)

        kwin = jnp.concatenate([k0[...], k1[...], k2[...]], axis=0)
        vwin = jnp.concatenate([v0[...], v1[...], v2[...]], axis=0)
        scale = HEAD_DIM ** -0.5
        qw_, kw_ = qw_ref[...], kw_ref[...]
        q, qn, rq = _rms_parts(q_ref[...], qw_)
        k, kn, rk = _rms_parts(kwin, kw_)
        qb, kb, dob = _bf(q), _bf(k), _bf(do_ref[...])
        p = _att_probs(qb, kb, bias_ref[0], _att_valid(b))
        dp = _dot_nt(dob, _bf(vwin))
        ds = p * (dp - jnp.sum(p * dp, axis=-1, keepdims=True))
        dbias_ref[...] += ds
        ds = _bf(ds)
        dq, dqw = _rms_bwd(_dot(ds, kb) * scale, qn, rq, qw_)
        dk, dkw = _rms_bwd(_dot_tn(ds, qb) * scale, kn, rk, kw_)
        dq_ref[...] = dq.astype(dq_ref.dtype)
        win = pl.ds(pl.multiple_of(b * ATT_QB, ATT_QB), ATT_KW)
        dk_ref[win, :] += dk
        dv_ref[win, :] += _dot_tn(_bf(p), dob)
        dqw_ref[...] += dqw
        dkw_ref[...] += dkw

        @pl.when(b == nb - 1)
        def _():
            tot, far = dbias_ref[...], _att_far_back()
            far_sum = jnp.sum(jnp.where(far, tot, 0.0))
            per_offset = _colsum(_rotate_rows(jnp.where(far, 0.0, tot), False))
            idx = _att_profile_index()
            first_row = lax.broadcasted_iota(jnp.int32, idx.shape, 0) == 0
            spread = jnp.where(first_row, per_offset, 0.0)

            def reduce(r, carry):
                drel_ref[h, r] = jnp.sum(jnp.where(idx == r, spread, 0.0)) + jnp.where(r == N_REL - 1, far_sum, 0.0)
                return carry

            lax.fori_loop(0, N_REL, reduce, 0)

    return pl.pallas_call(
        body, name="band_attention_bwd",
        out_shape=[jax.ShapeDtypeStruct((t_rows, HW), BF16),
                   jax.ShapeDtypeStruct((pad_rows, HW), F32), jax.ShapeDtypeStruct((pad_rows, HW), F32),
                   jax.ShapeDtypeStruct((1, HEAD_DIM), F32), jax.ShapeDtypeStruct((1, HEAD_DIM), F32),
                   jax.ShapeDtypeStruct((HEADS, N_REL), F32)],
        grid=(HEADS, nb),
        in_specs=[q_spec] + k_specs + v_specs + [w_spec, w_spec, BIAS_SPEC, q_spec],
        out_specs=[q_spec, acc_spec, acc_spec, w_spec, w_spec, smem],
        scratch_shapes=[pltpu.VMEM((ATT_QB, ATT_KW), F32)],
        compiler_params=_params(("arbitrary", "arbitrary")),
    )(pb, pb, pb, pb, pb, pb, pb, qw, kw, bias, dyb)


def _me():
    return lax.axis_index("x"), lax.axis_index("y"), lax.axis_index("c")


def _index(x, y, c):
    return 4 * x + 2 * y + c


HBM_SPEC = pl.BlockSpec(memory_space=pl.ANY)


def _block(ref, kind, d, r, c):
    if kind == "all":
        return ref
    if kind == "rows":
        return ref.at[pl.ds(d * r, r), :]
    if kind == "win":
        return ref.at[:, pl.ds(d * WIN_STEP, c)]
    return ref.at[:, pl.ds(d * c, c)]


def _all_gather(shards, kinds, n_gather):
    n = len(shards)

    def body(*refs):
        x_refs, out_refs = refs[:n], refs[n:2 * n]
        send_sems, recv_sems, local_sems = refs[2 * n:]
        x, y, c = _me()
        me, sibling = (x, y, c), (x, y, 1 - c)
        chips = [(1 - x, y), (x, 1 - y), (1 - x, 1 - y)]

        def copy(i, k, blk, to, src=None):
            r_, c_ = shards[i].shape
            dst = _block(out_refs[i], kinds[i], _index(*blk), r_, c_)
            return pltpu.make_async_remote_copy(
                src_ref=dst if src is None else src, dst_ref=dst,
                send_sem=send_sems.at[i, k], recv_sem=recv_sems.at[i, k], device_id=to, device_id_type=MESH)

        sends, local = [], []
        for i in range(n):
            r_, c_ = shards[i].shape
            mine = pltpu.make_async_copy(x_refs[i], _block(out_refs[i], kinds[i], _index(*me), r_, c_),
                                         local_sems.at[i])
            mine.start()
            local.append(mine)
            if i >= n_gather:
                continue
            first = [copy(i, 0, me, sibling, src=x_refs[i])]
            first += [copy(i, 1 + j, me, (*chip, c), src=x_refs[i]) for j, chip in enumerate(chips)]
            for cp in first:
                cp.start()
            sends += first
        for i in range(n_gather):
            for j, chip in enumerate(chips):
                copy(i, 1 + j, (*chip, c), me).wait_recv()
                passed = copy(i, 4 + j, (*chip, c), sibling)
                passed.start()
                sends.append(passed)
        for i in range(n_gather):
            copy(i, 0, sibling, me).wait_recv()
            for j, chip in enumerate(chips):
                copy(i, 4 + j, (*chip, 1 - c), me).wait_recv()
        for cp in sends:
            cp.wait_send()
        for cp in local:
            cp.wait()

    def full_shape(s, kind):
        r_, c_ = s.shape
        return (N_DEV * r_, c_) if kind == "rows" else (r_, N_DEV * c_)

    return pl.pallas_call(
        body, name="weights_all_gather",
        out_shape=[jax.ShapeDtypeStruct(full_shape(s, k), s.dtype) for s, k in zip(shards, kinds)],
        in_specs=[HBM_SPEC] * n, out_specs=[HBM_SPEC] * n,
        scratch_shapes=[pltpu.SemaphoreType.DMA((n_gather, 7)), pltpu.SemaphoreType.DMA((n_gather, 7)),
                        pltpu.SemaphoreType.DMA((n,))],
        compiler_params=pltpu.CompilerParams(has_side_effects=True),
    )(*shards)


SEM_SPEC = pl.BlockSpec(memory_space=pltpu.SEMAPHORE)
HBM_ONLY = pl.BlockSpec(memory_space=pltpu.HBM)
DATAFLOW = pltpu.SideEffectType.DATAFLOW_SIDE_EFFECTING


def _peers():
    x, y, c = _me()
    return [(x ^ (k >> 2), y ^ ((k >> 1) & 1), c ^ (k & 1)) for k in range(1, N_DEV)]


def _gather_copies(shapes, kinds):
    def make(src_refs, land_refs, send_sems, recv_sems):
        mine = _index(*_me())
        return [pltpu.make_async_remote_copy(
            src_ref=src_refs[i], dst_ref=_block(land_refs[i], kind, mine, r, c),
            send_sem=send_sems.at[7 * i + k], recv_sem=recv_sems.at[7 * i + k], device_id=peer, device_id_type=MESH)
            for i, ((r, c), kind) in enumerate(zip(shapes, kinds)) for k, peer in enumerate(_peers())]

    return make


def _exchange_copies(shapes, kinds):
    def make(src_refs, land_refs, send_sems, recv_sems):
        mine = _index(*_me())
        return [pltpu.make_async_remote_copy(
            src_ref=_block(src_refs[i], kind, _index(*peer), r, c), dst_ref=land_refs[i].at[mine],
            send_sem=send_sems.at[7 * i + k], recv_sem=recv_sems.at[7 * i + k], device_id=peer, device_id_type=MESH)
            for i, ((r, c), kind) in enumerate(zip(shapes, kinds)) for k, peer in enumerate(_peers())]

    return make


def _place_block(shard, kind, name):
    r, c = shard.shape
    tile = _row_tile(r, c)
    nt = r // tile
    full = (N_DEV * r, c) if kind == "rows" else (r, N_DEV * c)

    def body(me_ref, x_ref, out_ref):
        out_ref[...] = x_ref[...]

    if kind == "rows":
        out_spec = pl.BlockSpec((tile, c), lambda i, me: (me[0] * nt + i, 0))
    else:
        out_spec = pl.BlockSpec((tile, c), lambda i, me: (i, me[0]))
    return pl.pallas_call(
        body, name=name, out_shape=jax.ShapeDtypeStruct(full, shard.dtype),
        grid_spec=pltpu.PrefetchScalarGridSpec(
            num_scalar_prefetch=1, grid=(nt,),
            in_specs=[pl.BlockSpec((tile, c), lambda i, me: (i, 0))], out_specs=out_spec),
        compiler_params=_params(("arbitrary",)),
    )(_my_index_operand(), shard)


def _split_start(srcs, lands, make, name):
    n = len(srcs)

    def body(*refs):
        send_sems, recv_sems = refs[2 * n], refs[2 * n + 1]
        for cp in make(refs[:n], refs[n:2 * n], send_sems, recv_sems):
            cp.start()
        refs[-1][...] = jnp.zeros_like(refs[-1])

    arrays = list(srcs) + list(lands)
    out = pl.pallas_call(
        body, name=name,
        out_shape=(pltpu.SemaphoreType.DMA((7 * n,)), pltpu.SemaphoreType.DMA((7 * n,)),
                   *[pltpu.HBM(a.shape, a.dtype) for a in arrays], jax.ShapeDtypeStruct((SUBLANES, LANES), F32)),
        in_specs=[HBM_ONLY] * (2 * n),
        out_specs=(SEM_SPEC, SEM_SPEC, *[HBM_ONLY] * (2 * n), pl.BlockSpec(memory_space=pltpu.VMEM)),
        input_output_aliases={i: 2 + i for i in range(2 * n)},
        compiler_params=pltpu.CompilerParams(has_side_effects=DATAFLOW),
    )(*[pltpu.with_memory_space_constraint(a, pltpu.HBM) for a in arrays])
    return out[0], out[1], list(out[2:2 + n]), list(out[2 + n:2 + 2 * n]), out[-1]


def _split_wait(send_sems, recv_sems, srcs, lands, after, make, name):
    n = len(srcs)

    def body(*refs):
        for cp in make(refs[:n], refs[n:2 * n], refs[2 * n], refs[2 * n + 1]):
            cp.wait_send()
            cp.wait_recv()

    arrays = list(srcs) + list(lands)
    out = pl.pallas_call(
        body, name=name,
        out_shape=tuple(pltpu.HBM(a.shape, a.dtype) for a in arrays),
        in_specs=[HBM_ONLY] * (2 * n) + [SEM_SPEC, SEM_SPEC, pl.BlockSpec(memory_space=pl.ANY)],
        out_specs=tuple([HBM_ONLY] * (2 * n)),
        input_output_aliases={i: i for i in range(2 * n)},
        compiler_params=pltpu.CompilerParams(has_side_effects=DATAFLOW),
    )(*arrays, send_sems, recv_sems, after)
    return list(out[:n]), list(out[n:])


def _all_reduce_small(vals, name):
    rows, width = vals.shape

    def body(x_ref, out_ref, buf_ref, send_sems, recv_sems):
        x, y, c = _me()
        mine = _index(x, y, c)
        buf_ref[mine] = x_ref[...]
        copies = []
        for k in range(1, N_DEV):
            px, py, pc = x ^ (k >> 2), y ^ ((k >> 1) & 1), c ^ (k & 1)
            copies.append(pltpu.make_async_remote_copy(
                src_ref=x_ref, dst_ref=buf_ref.at[mine],
                send_sem=send_sems.at[k - 1], recv_sem=recv_sems.at[k - 1],
                device_id=(px, py, pc), device_id_type=MESH))
        for cp in copies:
            cp.start()
        for cp in copies:
            cp.wait()
        acc = buf_ref[0]
        for j in range(1, N_DEV):
            acc = acc + buf_ref[j]
        out_ref[...] = acc

    vmem = pl.BlockSpec(memory_space=pltpu.VMEM)
    return pl.pallas_call(
        body, name=name,
        out_shape=jax.ShapeDtypeStruct(vals.shape, F32),
        in_specs=[vmem], out_specs=vmem,
        scratch_shapes=[pltpu.VMEM((N_DEV, rows, width), F32),
                        pltpu.SemaphoreType.DMA((7,)), pltpu.SemaphoreType.DMA((7,))],
        compiler_params=pltpu.CompilerParams(has_side_effects=True),
    )(vals)


def _adamw_math(w, g, m, v):
    m = ADAM_B1 * m + (1.0 - ADAM_B1) * g
    v = ADAM_B2 * v + (1.0 - ADAM_B2) * (g * g)
    m_hat = m / (1.0 - ADAM_B1 ** ADAM_STEP)
    v_hat = v / (1.0 - ADAM_B2 ** ADAM_STEP)
    delta = -ADAM_LR * (m_hat / (jnp.sqrt(v_hat) + ADAM_EPS) + ADAM_WD * w)
    return delta, m, v


ROW_TILE_ELEMS = 384 * 1024


def _row_tile(rows, width):
    best = SUBLANES
    for t in range(SUBLANES, rows + 1, SUBLANES):
        if rows % t == 0 and t * width <= ROW_TILE_ELEMS:
            best = t
    return best


def _sum_received(r_ref, own, me):
    g = None
    for j in range(N_DEV):
        term = jnp.where(me == j, own, r_ref[j].astype(F32))
        g = term if g is None else g + term
    return g


def _my_index_operand():
    return _index(*_me()).astype(jnp.int32).reshape(1)


def _sum_small(recv, own):
    def body(me_ref, r_ref, own_ref, out_ref):
        out_ref[...] = _sum_received(r_ref, own_ref[...], me_ref[0])

    whole = lambda shape: pl.BlockSpec(shape, lambda i, me, nd=len(shape): (0,) * nd)
    return pl.pallas_call(
        body, name="small_grads_sum", out_shape=jax.ShapeDtypeStruct(own.shape, F32),
        grid_spec=pltpu.PrefetchScalarGridSpec(
            num_scalar_prefetch=1, grid=(1,), in_specs=[whole(recv.shape), whole(own.shape)],
            out_specs=whole(own.shape)),
        compiler_params=_params(("arbitrary",)),
    )(_my_index_operand(), recv, own)


def _adamw_recv(recv, grad, kind, w, m, v, name):
    _, rows, width = recv.shape
    tile = _row_tile(rows, width)
    nt = rows // tile

    def body(me_ref, r_ref, own_ref, w_ref, m_ref, v_ref, g_out, d_out, m_out, v_out):
        g = _sum_received(r_ref, own_ref[...].astype(F32), me_ref[0])
        d, mn, vn = _adamw_math(w_ref[...], g, m_ref[...], v_ref[...])
        g_out[...] = g
        d_out[...] = d
        m_out[...] = mn
        v_out[...] = vn

    if kind == "rows":
        own_spec = pl.BlockSpec((tile, width), lambda i, me: (me[0] * nt + i, 0))
    else:
        own_spec = pl.BlockSpec((tile, width), lambda i, me: (i, me[0]))
    spec = pl.BlockSpec((tile, width), lambda i, me: (i, 0))
    shape = jax.ShapeDtypeStruct((rows, width), F32)
    return pl.pallas_call(
        body, name=name, out_shape=[shape] * 4,
        grid_spec=pltpu.PrefetchScalarGridSpec(
            num_scalar_prefetch=1, grid=(nt,),
            in_specs=[pl.BlockSpec((N_DEV, tile, width), lambda i, me: (0, i, 0)), own_spec, spec, spec, spec],
            out_specs=[spec] * 4),
        compiler_params=_params(("parallel",)),
    )(_my_index_operand(), recv, grad, w, m, v)


WIN_STEP = 1408
WIN_W = 1536
IN_SHARD = IN_COLS // N_DEV
IN_PADDED = WIN_STEP * (N_DEV - 1) + WIN_W


def _roll_w_in(shard_padded):
    rows = shard_padded.shape[0]
    tile = _row_tile(rows, WIN_W)

    def body(x_ref, main_ref, edge_ref):
        win = pltpu.roll(x_ref[...], 2 * _index(*_me()), 1).astype(BF16)
        main_ref[...] = win[:, :WIN_STEP]
        edge_ref[...] = win[:, WIN_STEP:]

    return pl.pallas_call(
        body, name="w_in_window",
        out_shape=[jax.ShapeDtypeStruct((rows, WIN_STEP), BF16), jax.ShapeDtypeStruct((rows, WIN_W - WIN_STEP), BF16)],
        grid=(rows // tile,),
        in_specs=[pl.BlockSpec((tile, WIN_W), lambda i: (i, 0))],
        out_specs=[pl.BlockSpec((tile, WIN_STEP), lambda i: (i, 0)),
                   pl.BlockSpec((tile, WIN_W - WIN_STEP), lambda i: (i, 0))],
        compiler_params=_params(("parallel",)),
    )(shard_padded)


def _sum_w_in_windows(recv, grad):
    _, rows, width = recv.shape
    tile = _row_tile(rows, width)

    def body(me_ref, r_ref, g_ref, g_out, own_ref, sem):
        me = me_ref[0]
        rows_i = pl.ds(pl.multiple_of(pl.program_id(0) * tile, tile), tile)
        own = pltpu.make_async_copy(g_ref.at[rows_i, pl.ds(pl.multiple_of(me * WIN_STEP, LANES), width)], own_ref, sem)
        own.start()
        own.wait()
        g_out[...] = pltpu.roll(_sum_received(r_ref, own_ref[...].astype(F32), me), width - 2 * me, 1)

    return pl.pallas_call(
        body, name="w_in_grad_sum", out_shape=jax.ShapeDtypeStruct((rows, width), F32),
        grid_spec=pltpu.PrefetchScalarGridSpec(
            num_scalar_prefetch=1, grid=(rows // tile,),
            in_specs=[pl.BlockSpec((N_DEV, tile, width), lambda i, me: (0, i, 0)), HBM_SPEC],
            out_specs=pl.BlockSpec((tile, width), lambda i, me: (i, 0)),
            scratch_shapes=[pltpu.VMEM((tile, width), BF16), pltpu.SemaphoreType.DMA]),
        compiler_params=_params(("arbitrary",)),
    )(_my_index_operand(), recv, grad)


def _adamw_small(w, g, m, v, name):
    def fn(i, n, w_, g_, m_, v_):
        return _adamw_math(w_, g_, m_, v_)

    r, c = w.shape
    return _rows(fn, [(w, "t"), (g, "t"), (m, "t"), (v, "t")], [], [(c, F32)] * 3, [], _row_tile(r, c), name)


def _norm_fwd(x, w, name):
    return _rows(lambda i, n, x_, w_: (_rms(x_, w_[...]),), [(x, "t")], [w], [(D_MODEL, BF16)], [], 512, name)[0]


def _residual_norm_fwd(x, y, scale, w, name):
    def fn(i, n, x_, y_, w_):
        xn = x_ + scale * y_
        return xn, _rms(xn, w_[...])

    return _rows(fn, [(x, "t"), (y, "t")], [w], [(D_MODEL, F32), (D_MODEL, BF16)], [], 512, name)


def _residual_norm_bwd(x, w, dhs, dres, scale, name):
    nh = len(dhs)

    def fn(i, n, x_, dres_, *rest):
        dh = rest[0]
        for extra in rest[1:nh]:
            dh = dh + extra
        _, vjp = jax.vjp(_rms, x_, rest[nh][...])
        dx, dw = vjp(dh)
        dx = dx + dres_
        return dx, scale * dx, dw

    return _rows(fn, [(x, "t"), (dres, "t")] + [(d, "t") for d in dhs], [w],
                 [(D_MODEL, F32), (D_MODEL, BF16)], [(1, D_MODEL)], 256, name)


FFN_UP_TN = 512


def _ffn_up(h, w_gu, name, after=None):
    t, d = h.shape
    f = w_gu.shape[1] // 2
    tm = _pick(t, (1024, 512, 256, 128))
    nj = f // FFN_UP_TN
    after_specs, after_args = _after(after)

    def body(h_ref, wg_ref, wu_ref, *rest):
        g_ref, u_ref, act_ref = rest[len(after_args):]
        hb = h_ref[...]
        g = jnp.dot(hb, wg_ref[...], preferred_element_type=F32)
        u = jnp.dot(hb, wu_ref[...], preferred_element_type=F32)
        g_ref[...] = g.astype(BF16)
        u_ref[...] = u.astype(BF16)
        act_ref[...] = (_silu(g) * u).astype(BF16)

    out = pl.BlockSpec((tm, FFN_UP_TN), lambda i, j: (i, j))
    return pl.pallas_call(
        body, name=name, out_shape=[jax.ShapeDtypeStruct((t, f), BF16)] * 3, grid=(t // tm, nj),
        in_specs=[pl.BlockSpec((tm, d), lambda i, j: (i, 0)),
                  pl.BlockSpec((d, FFN_UP_TN), lambda i, j: (0, j)),
                  pl.BlockSpec((d, FFN_UP_TN), lambda i, j: (0, j + nj))] + after_specs,
        out_specs=[out, out, out],
        compiler_params=_params(("parallel", "parallel")),
    )(h, w_gu, w_gu, *after_args)


def _ffn_fwd(h, w_gu, get_w_down, tag, after=None):
    g, u, act = _ffn_up(h, w_gu, tag + "_gu", after)
    y = _matmul(act, get_w_down(act), "nn", F32, tag + "_down")
    return (g, u), act, y


def _ffn_dact(dy, w_down, g, u, name):
    t, d = dy.shape
    f = w_down.shape[0]
    tm = _pick(t, (1024, 512, 256, 128))

    def body(dy_ref, w_ref, g_ref, u_ref, out_ref):
        dact = lax.dot_general(dy_ref[...], w_ref[...], NT, preferred_element_type=F32)
        g_, u_ = g_ref[...].astype(F32), u_ref[...].astype(F32)
        sg = _sigmoid(g_)
        out_ref[0] = (dact * u_ * (sg * (1.0 + g_ * (1.0 - sg)))).astype(BF16)
        out_ref[1] = (dact * (g_ * sg)).astype(BF16)

    tile = pl.BlockSpec((tm, FFN_UP_TN), lambda i, j: (i, j))
    return pl.pallas_call(
        body, name=name, out_shape=jax.ShapeDtypeStruct((2, t, f), BF16), grid=(t // tm, f // FFN_UP_TN),
        in_specs=[pl.BlockSpec((tm, d), lambda i, j: (i, 0)), pl.BlockSpec((FFN_UP_TN, d), lambda i, j: (j, 0)),
                  tile, tile],
        out_specs=pl.BlockSpec((2, tm, FFN_UP_TN), lambda i, j: (0, i, j)),
        compiler_params=_params(("parallel", "parallel")),
    )(dy, w_down, g, u)


def _ffn_bwd(h, gu, act, dy, w_gu, w_down, tag, comm, more=None):
    dgu = _ffn_dact(dy, w_down, gu[0], gu[1], tag + "_dact")
    sent = comm.send(tag + "_gu", {tag + "_w_gu": _matmul(h, dgu, "tn", BF16, tag + "_d_w_gu")})
    sent = sent + comm.send(tag + "_down", {tag + "_w_down": _matmul(act, dy, "tn", BF16, tag + "_d_w_down", sent),
                                            **(more or {})})
    dh = _matmul(dgu, w_gu, "nt", BF16, tag + "_dh")
    return dh, sent


def _expanders():
    e_g = np.zeros((LANES, HW), np.float32)
    e_b = np.zeros((LANES, HW), np.float32)
    for h in range(HEADS):
        e_g[h, h * HEAD_DIM:(h + 1) * HEAD_DIM] = 1.0
        e_b[HEADS + h, h * HEAD_DIM:(h + 1) * HEAD_DIM] = 1.0
    return jnp.asarray(e_g), jnp.asarray(e_b)


def _pad_lanes(v):
    return jnp.pad(v, ((0, 0), (0, LANES - v.shape[1])))


class _LocalWeights:
    def __init__(self, big):
        self.big, self.sent = big, {}

    def arrive(self, group, after):
        return self.big

    def send(self, group, grads):
        self.sent.update(grads)
        return jnp.zeros((), F32)


def _local_step(x, p, tgt, small, comm):
    e_g, e_b = _expanders()
    alog, dtb = _pad_lanes(small["a_log"]), _pad_lanes(small["dt_bias"])
    conv_w = jnp.pad(small["conv_w"], ((0, SUBLANES - CONV_K), (0, 0)))
    rel = _expand_rel_bias(small["rel_bias"])

    h1 = _norm_fwd(x, small["ffn1_norm"], "ffn1_norm")
    big = dict(comm.arrive("ffn1", h1))
    started = big.pop("_token", None)

    def ffn1_w_down(act):
        big.update(comm.arrive("ffn1_down", act))
        return big["ffn1_w_down"]

    gu1, act1, y1 = _ffn_fwd(h1, big["ffn1_w_gu"], ffn1_w_down, "ffn1", started)
    x1, h2 = _residual_norm_fwd(x, y1, 0.5, small["mix_norm"], "mix_norm")

    big = {**big, **comm.arrive("mixer", h2)}
    w_in = big["w_in"]
    w_qz = w_in[:, :IN_QZ + LANES]
    w_qkvb = w_in[:, IN_QKVB0:IN_GG0]
    w_gg = w_in[:, IN_GG0:IN_COLS]
    qz = _matmul(h2, w_qz, "nn", F32, "in_qz")
    pb = _matmul(h2, w_qkvb, "nn", F32, "in_qkvb")
    gg = _matmul(h2, w_gg, "nn", BF16, "in_gates")
    pa, z, ab = qz[:, :3 * HW], qz[:, 3 * HW:IN_QZ], qz[:, IN_QZ:]

    def prep(i, n, pa_, prev_, ab_, cw_, alog_, dtb_, eg_, eb_):
        q, k, v = _gdn_post(_conv(pa_, prev_, cw_, i))
        g_b, beta_b = _gdn_gates(ab_, alog_[...], dtb_[...], eg_[...], eb_[...])
        return q, k, v, g_b, beta_b

    qn, kn, vv, g_b, beta_b = _rows(prep, [(pa, "t"), (pa, "p"), (ab, "t")], [conv_w, alog, dtb, e_g, e_b],
                                    [(HW, F32)] * 5, [], 256, "gdn_prep")
    u, w, aqk, qd, kt, tl = _gdn_intra(qn, kn, vv, g_b, beta_b)
    o, states = _gdn_scan(u, w, aqk, qd, kt, tl)
    ya = _rows(lambda i, n, o_, z_, w_: (_gated_norm(o_, z_, w_[...]),), [(o, "t"), (z, "t")], [small["gdn_norm"]],
               [(HW, BF16)], [], 512, "gdn_gated_norm")[0]

    yb = _attention(pb, small["q_norm"], small["k_norm"], rel)

    big = {**big, **comm.arrive("branches", yb)}
    ta = _matmul(ya, big["w_branch_a"], "nn", BF16, "branch_a")
    tb = _matmul(yb, big["w_branch_b"], "nn", BF16, "branch_b")
    mixed = _rows(lambda i, n, gg_, ta_, tb_: (_mix(gg_, ta_, tb_),), [(gg, "t"), (ta, "t"), (tb, "t")], [],
                  [(D_MODEL, BF16)], [], 256, "mix")[0]
    m_out = _matmul(mixed, big["w_out"], "nn", F32, "w_out")
    x2, h3 = _residual_norm_fwd(x1, m_out, 1.0, small["ffn2_norm"], "ffn2_norm")
    big = {**big, **comm.arrive("tail", h3)}
    gu2, act2, y2 = _ffn_fwd(h3, big["ffn2_w_gu"], lambda act: big["ffn2_w_down"], "ffn2")
    x3, h4 = _residual_norm_fwd(x2, y2, 0.5, small["ple_norm"], "ple_norm")
    gp = _matmul(h4, big["ple_gate"], "nn", BF16, "ple_gate")
    pp = _matmul(p, big["ple_proj"], "nn", BF16, "ple_proj")

    def head(i, n, x3_, gp_, pp_, tgt_):
        sg = _sigmoid(gp_)
        err = x3_ + sg * pp_ - tgt_
        dx4 = err * (1.0 / D_MODEL)
        sq = _colsum(err * err)
        part = sq[:, :LANES]
        for j in range(1, D_MODEL // LANES):
            part = part + sq[:, j * LANES:(j + 1) * LANES]
        return dx4, dx4 * pp_ * sg * (1.0 - sg), dx4 * sg, (0.5 / D_MODEL) * part

    dx4, dgp, dpp, loss_lanes = _rows(head, [(x3, "t"), (gp, "t"), (pp, "t"), (tgt, "t")], [],
                                      [(D_MODEL, F32), (D_MODEL, BF16), (D_MODEL, BF16)], [(1, LANES)], 256,
                                      "ple_loss_head")
    loss = jnp.sum(loss_lanes)

    gbig, gsmall = {}, {}
    gbig["ple_proj"] = _matmul(p, dpp, "tn", BF16, "d_ple_proj")
    gbig["ple_gate"] = _matmul(h4, dgp, "tn", BF16, "d_ple_gate")
    dh4 = _matmul(dgp, big["ple_gate"], "nt", BF16, "ple_gate_dh")
    dx3, dy2, gsmall["ple_norm"] = _residual_norm_bwd(x3, small["ple_norm"], [dh4], dx4, 0.5, "ple_norm_bwd")

    dh3, sent = _ffn_bwd(h3, gu2, act2, dy2, big["ffn2_w_gu"], big["ffn2_w_down"], "ffn2", comm,
                         {n: gbig[n] for n in ("ple_proj", "ple_gate")})
    dx2, dx2b, gsmall["ffn2_norm"] = _residual_norm_bwd(x2, small["ffn2_norm"] + sent, [dh3], dx3, 1.0,
                                                        "ffn2_norm_bwd")

    gbig["w_out"] = _matmul(mixed, dx2b, "tn", BF16, "d_w_out")
    dmixed = _matmul(dx2b, big["w_out"], "nt", BF16, "w_out_dx")

    def mix_bwd(i, n, gg_, ta_, tb_, dm_):
        _, vjp = jax.vjp(_mix, gg_, ta_, tb_)
        return vjp(dm_)

    dgg, dta, dtb_ = _rows(mix_bwd, [(gg, "t"), (ta, "t"), (tb, "t"), (dmixed, "t")], [],
                           [(2 * D_MODEL, BF16), (D_MODEL, BF16), (D_MODEL, BF16)], [], 256, "mix_bwd")
    gbig["w_branch_a"] = _matmul(ya, dta, "tn", BF16, "d_branch_a")
    gbig["w_branch_b"] = _matmul(yb, dtb_, "tn", BF16, "d_branch_b")
    dya = _matmul(dta, big["w_branch_a"], "nt", BF16, "branch_a_dx")
    dyb = _matmul(dtb_, big["w_branch_b"], "nt", BF16, "branch_b_dx")

    dq_b, dk_b, dv_b, gsmall["q_norm"], gsmall["k_norm"], gsmall["rel_bias"] = _attention_bwd(
        pb, small["q_norm"], small["k_norm"], rel, dyb)
    dpb = jnp.concatenate([dq_b, dk_b[ATT_PAD:].astype(BF16), dv_b[ATT_PAD:].astype(BF16)], axis=1)

    def gated_bwd(i, n, o_, z_, dya_, w_):
        _, vjp = jax.vjp(_gated_norm, o_, z_, w_[...])
        return vjp(dya_)

    do, dz, gsmall["gdn_norm"] = _rows(gated_bwd, [(o, "t"), (z, "t"), (dya, "t")], [small["gdn_norm"]],
                                       [(HW, F32), (HW, BF16)], [(1, HEAD_DIM)], 256, "gdn_gated_norm_bwd")
    du, dw, da, dqd, dkt, dtl = _gdn_scan_bwd(do, u, w, aqk, qd, kt, tl, states)
    dqn, dkn, dvv, dg_b, dbeta_b = _gdn_intra_bwd(qn, kn, vv, g_b, beta_b, du, dw, da, dqd, dkt, dtl)

    def prep_bwd(i, n, pa_, prev_, ab_, dq_, dk_, dv_, dg_, db_, cw_, alog_, dtb_, eg_, eb_):
        _, vjp = jax.vjp(_gdn_post, _conv(pa_, prev_, cw_, i))
        (dy,) = vjp((dq_, dk_, dv_))
        e_g_, e_b_ = eg_[...], eb_[...]
        _, vjp_g = jax.vjp(lambda a, b, c: _gdn_gates(a, b, c, e_g_, e_b_), ab_, alog_[...], dtb_[...])
        dab, dalog, ddtb = vjp_g((dg_, db_))
        return dy, dab, dalog, ddtb

    dy_conv, dab, dalog, ddtb = _rows(
        prep_bwd, [(pa, "t"), (pa, "p"), (ab, "t"), (dqn, "t"), (dkn, "t"), (dvv, "t"), (dg_b, "t"), (dbeta_b, "t")],
        [conv_w, alog, dtb, e_g, e_b], [(3 * HW, F32), (LANES, BF16)], [(1, LANES), (1, LANES)], 256,
        "gdn_prep_bwd")
    gsmall["a_log"] = dalog[:, :HEADS]
    gsmall["dt_bias"] = ddtb[:, :HEADS]

    def conv_bwd(i, n, dy_, nxt_, pa_, prev_, cw_):
        dpa = dy_ * cw_[CONV_K - 1:CONV_K, :]
        row = lax.broadcasted_iota(jnp.int32, (SUBLANES, dy_.shape[1]), 0)
        dcw = jnp.where(row == CONV_K - 1, _colsum(dy_ * pa_), 0.0)
        for j in range(CONV_K - 1):
            s = CONV_K - 1 - j
            dpa = dpa + _shift_up(dy_, nxt_, s, i, n) * cw_[j:j + 1, :]
            dcw = dcw + jnp.where(row == j, _colsum(dy_ * _shift_down(pa_, prev_, s, i)), 0.0)
        return dpa, dcw

    dpa, dcw = _rows(conv_bwd, [(dy_conv, "t"), (dy_conv, "n"), (pa, "t"), (pa, "p")], [conv_w],
                     [(3 * HW, BF16)], [(SUBLANES, 3 * HW)], 256, "gdn_conv_bwd")
    gsmall["conv_w"] = dcw[:CONV_K]

    dqz = jnp.concatenate([dpa, dz, dab], axis=1)
    d_w_qz = _matmul(h2, dqz, "tn", BF16, "d_in_qz")
    d_w_qkvb = _matmul(h2, dpb, "tn", BF16, "d_in_qkvb")
    d_w_gg = _matmul(h2, dgg, "tn", BF16, "d_in_gates")
    gbig["w_in"] = jnp.concatenate([d_w_qz[:, :IN_QKVB0], d_w_qkvb, d_w_gg,
                                    jnp.zeros((D_MODEL, IN_PADDED - IN_COLS), BF16)], axis=1)
    dh2 = [_matmul(dqz, w_qz, "nt", BF16, "in_qz_dh"), _matmul(dpb, w_qkvb, "nt", BF16, "in_qkvb_dh"),
           _matmul(dgg, w_gg, "nt", BF16, "in_gates_dh")]
    sent = comm.send("mixer", {n: gbig[n] for n in ("w_out", "w_branch_b", "w_branch_a", "w_in")})
    dx1, dy1, gsmall["mix_norm"] = _residual_norm_bwd(x1, small["mix_norm"] + sent, dh2, dx2, 0.5, "mix_norm_bwd")

    dh1, sent = _ffn_bwd(h1, gu1, act1, dy1, big["ffn1_w_gu"], big["ffn1_w_down"], "ffn1", comm)
    grad_x, _, gsmall["ffn1_norm"] = _residual_norm_bwd(x, small["ffn1_norm"] + sent, [dh1], dx1, 1.0,
                                                        "ffn1_norm_bwd")
    return loss, grad_x, gsmall


GATHER_GROUPS = {"ffn1": ("ffn1_w_gu",),
                 "ffn1_down": ("ffn1_w_down",),
                 "mixer": ("w_in_main", "w_in_edge"),
                 "branches": ("w_branch_a", "w_branch_b", "w_out"),
                 "tail": ("ffn2_w_gu", "ffn2_w_down", "ple_gate", "ple_proj")}
SPLIT_GATHERS = ("ffn1_down", "mixer", "branches", "tail")


def _kind(name):
    return "cols" if name in COL_SHARDED or name.startswith("w_in_") else "rows"


def _merge_w_in(main, edges):
    edge_w = WIN_W - WIN_STEP
    w_in = jnp.pad(main, ((0, 0), (0, edge_w)))
    for d in range(N_DEV):
        at = WIN_STEP * (d + 1)
        w_in = w_in + jnp.pad(edges[:, d * edge_w:(d + 1) * edge_w], ((0, 0), (at, IN_PADDED - at - edge_w)))
    return w_in


class _Fsdp:
    def __init__(self, wts, first):
        self.wts, self.first_token = wts, first
        main, edge = _roll_w_in(jnp.pad(wts["w_in"], ((0, 0), (0, WIN_W - IN_SHARD))))
        self.shards = {n: wts[n].astype(BF16) for n in BIG if n not in ("w_in", "ffn1_w_gu")}
        self.shards.update(w_in_main=main, w_in_edge=edge)
        self.lands = {n: _place_block(self.shards[n], _kind(n), "own_" + n)
                      for group in SPLIT_GATHERS for n in GATHER_GROUPS[group]}
        self.flight, self.sent = {}, {}

    def _gather_first(self, after):
        token = self.first_token + after[0, 0].astype(F32) * 0.0
        me = _index(*_me())
        for n, land in self.lands.items():
            r, c = self.shards[n].shape
            at = (me * r, 0) if _kind(n) == "rows" else (0, me * c)
            token = token + lax.dynamic_slice(land, at, (1, 1))[0, 0].astype(F32) * 0.0
        shard = (self.wts["ffn1_w_gu"] + token).astype(BF16)
        self.shards["ffn1_w_gu"] = shard
        first = _all_gather([shard], [_kind("ffn1_w_gu")], 1)[0]
        token = first[0, 0].astype(F32) * 0.0
        for group in SPLIT_GATHERS:
            names = GATHER_GROUPS[group]
            srcs = [self.shards[n] for n in names]
            lands = [self.lands[n] for n in names]
            make = _gather_copies([s.shape for s in srcs], [_kind(n) for n in names])
            srcs[0] = srcs[0] + token.astype(BF16)
            send_sems, recv_sems, srcs, lands, tok = _split_start(srcs, lands, make, "gather_start_" + group)
            token = token + tok[0, 0]
            self.flight[group] = (send_sems, recv_sems, srcs, lands, make)
        return {"ffn1_w_gu": first, "_token": token}

    def arrive(self, group, after):
        if group == "ffn1":
            return self._gather_first(after)
        send_sems, recv_sems, srcs, lands, make = self.flight[group]
        _, lands = _split_wait(send_sems, recv_sems, srcs, lands, after, make, "gather_wait_" + group)
        full = dict(zip(GATHER_GROUPS[group], lands))
        if group == "mixer":
            full["w_in"] = _merge_w_in(full.pop("w_in_main"), full.pop("w_in_edge"))
        return full

    def send(self, group, grads):
        names = list(grads)
        kinds = ["all" if n == "small" else "win" if n == "w_in" else _kind(n) for n in names]
        shapes = [grads[n].shape if n == "small" else (D_MODEL, WIN_W) if n == "w_in" else self.shards[n].shape
                  for n in names]
        srcs = [grads[n] for n in names]
        lands = [lax.empty((N_DEV,) + tuple(s), g.dtype) for s, g in zip(shapes, srcs)]
        make = _exchange_copies(shapes, kinds)
        send_sems, recv_sems, srcs, lands, tok = _split_start(srcs, lands, make, "grads_start_" + group)
        self.sent[group] = (names, kinds, send_sems, recv_sems, srcs, lands, make)
        return tok[0, 0]

    def received(self, group, after):
        names, kinds, send_sems, recv_sems, srcs, lands, make = self.sent[group]
        srcs, lands = _split_wait(send_sems, recv_sems, srcs, lands, after, make, "grads_wait_" + group)
        return {n: (k, g, r) for n, k, g, r in zip(names, kinds, srcs, lands)}


SMALL_ROWS = ("ffn1_norm", "mix_norm", "ffn2_norm", "ple_norm", "gdn_norm", "q_norm", "k_norm", "a_log", "dt_bias",
              "rel_bias", "conv_w")


def _pack_small(vals):
    rows = []
    for n in SMALL_ROWS:
        v = vals[n]
        if n == "rel_bias":
            v = jnp.pad(v, ((0, 0), (0, 2 * LANES - N_REL)))
        elif n in ("a_log", "dt_bias"):
            v = _pad_lanes(v)
        rows.append(v.reshape(-1, LANES))
    packed = jnp.concatenate(rows, axis=0)
    return jnp.pad(packed, ((0, -packed.shape[0] % SUBLANES), (0, 0)))


def _unpack_small(packed, shapes):
    out, off = {}, 0
    for n in SMALL_ROWS:
        shp = shapes[n]
        if n == "rel_bias":
            out[n] = packed[off:off + 2 * HEADS].reshape(HEADS, 2 * LANES)[:, :N_REL]
            off += 2 * HEADS
        elif n in ("a_log", "dt_bias"):
            out[n] = packed[off:off + 1, :HEADS]
            off += 1
        else:
            r = int(np.prod(shp)) // LANES
            out[n] = packed[off:off + r].reshape(shp)
            off += r
    return out


WEIGHTS = ("ffn1_norm", "ffn1_w_gu", "ffn1_w_down", "mix_norm", "w_in", "conv_w", "a_log", "dt_bias", "gdn_norm",
           "q_norm", "k_norm", "rel_bias", "w_branch_a", "w_branch_b", "w_out", "ffn2_norm", "ffn2_w_gu",
           "ffn2_w_down", "ple_norm", "ple_gate", "ple_proj")


def kernel(x, p, ffn1_norm, ffn1_w_gu, ffn1_w_down, mix_norm, w_in, conv_w, a_log, dt_bias, gdn_norm, q_norm, k_norm, rel_bias, w_branch_a, w_branch_b, w_out, ffn2_norm, ffn2_w_gu, ffn2_w_down, ple_norm, ple_gate, ple_proj, loss_target, m_ffn1_norm, m_ffn1_w_gu, m_ffn1_w_down, m_mix_norm, m_w_in, m_conv_w, m_a_log, m_dt_bias, m_gdn_norm, m_q_norm, m_k_norm, m_rel_bias, m_w_branch_a, m_w_branch_b, m_w_out, m_ffn2_norm, m_ffn2_w_gu, m_ffn2_w_down, m_ple_norm, m_ple_gate, m_ple_proj, v_ffn1_norm, v_ffn1_w_gu, v_ffn1_w_down, v_mix_norm, v_w_in, v_conv_w, v_a_log, v_dt_bias, v_gdn_norm, v_q_norm, v_k_norm, v_rel_bias, v_w_branch_a, v_w_branch_b, v_w_out, v_ffn2_norm, v_ffn2_w_gu, v_ffn2_w_down, v_ple_norm, v_ple_gate, v_ple_proj):
    args = dict(locals())
    def layer0(v):
        return v[0] if v.ndim == 3 else v

    wts = {n: layer0(args[n]) for n in WEIGHTS}
    mom = {n: layer0(args["m_" + n]) for n in WEIGHTS}
    var = {n: layer0(args["v_" + n]) for n in WEIGHTS}
    x2d, p2d, tgt = x[0], p[0, 0], loss_target[0]
    my_index = _index(*_me())

    small = {n: wts[n] for n in SMALL_ROWS if n != "conv_w"}
    conv_shard = wts["conv_w"]
    conv_cols = conv_shard.shape[1]
    conv_packed = jnp.zeros((SUBLANES, N_DEV * conv_cols), F32)
    conv_packed = lax.dynamic_update_slice(conv_packed, jnp.pad(conv_shard, ((0, SUBLANES - CONV_K), (0, 0))),
                                           (0, my_index * conv_cols))
    small["conv_w"] = _all_reduce_small(conv_packed.reshape(-1, LANES), "conv_w_gather").reshape(SUBLANES, -1)[:CONV_K]

    fsdp = _Fsdp(wts, small["conv_w"][0, 0] * 0.0)

    loss, grad_x, gsmall = _local_step(x2d, p2d, tgt, small, fsdp)
    loss = lax.psum(loss, ("x", "y", "c"))

    fsdp.send("small", {"small": _pack_small(gsmall)})

    outs_big, after = {}, grad_x
    for group in list(fsdp.sent):
        for n, (kind, grad, recv) in fsdp.received(group, after).items():
            if n == "small":
                small_sum = _sum_small(recv, grad)
            elif n == "w_in":
                g_in = _sum_w_in_windows(recv, grad)[:, :IN_SHARD]
                outs_big[n] = [g_in] + list(_adamw_small(wts[n], g_in, mom[n], var[n], "adamw_w_in"))
            else:
                outs_big[n] = _adamw_recv(recv, grad, kind, wts[n], mom[n], var[n], "adamw_" + n)
            after = small_sum if n == "small" else outs_big[n][1]

    small_shapes = {n: (small[n].shape if n != "conv_w" else (CONV_K, N_DEV * conv_cols)) for n in SMALL_ROWS}
    gsum = _unpack_small(small_sum, small_shapes)
    gsum["conv_w"] = lax.dynamic_slice(gsum["conv_w"], (0, my_index * conv_cols), (CONV_K, conv_cols))
    rep = [n for n in SMALL_ROWS if n != "conv_w"]
    rep_shapes = {n: small_shapes[n] for n in rep}

    def pack_rep(vals):
        return _pack_small({**{n: vals[n] for n in rep}, "conv_w": jnp.zeros((CONV_K, LANES), F32)})

    def unpack_rep(packed):
        return _unpack_small(packed, {**rep_shapes, "conv_w": (CONV_K, LANES)})

    outs_small = [unpack_rep(o) for o in _adamw_small(pack_rep(wts), pack_rep(gsum), pack_rep(mom), pack_rep(var),
                                                      "adamw_replicated")]
    pad8 = functools.partial(jnp.pad, pad_width=((0, SUBLANES - CONV_K), (0, 0)))
    outs_conv = [o[:CONV_K] for o in _adamw_small(pad8(conv_shard), pad8(gsum["conv_w"]), pad8(mom["conv_w"]),
                                                   pad8(var["conv_w"]), "adamw_conv")]

    def leaf(kind, n):
        if n in BIG:
            return outs_big[n][kind][None]
        if n == "conv_w":
            return (gsum["conv_w"] if kind == 0 else outs_conv[kind - 1])[None]
        return (gsum[n] if kind == 0 else outs_small[kind - 1][n]).reshape(args[n].shape)

    result = [loss, grad_x[None]]
    for kind in range(4):
        result += [leaf(kind, n) for n in WEIGHTS]
    return tuple(result)
```

```python
import functools

import numpy as np
import jax
import jax.numpy as jnp
from jax import lax
from jax.experimental import pallas as pl
from jax.experimental.pallas import tpu as pltpu

F32 = jnp.float32
BF16 = jnp.bfloat16
HIGHEST = lax.Precision.HIGHEST
MESH = pl.DeviceIdType.MESH

D_MODEL = 2048
D_FF = 5632
HEADS = 8
HEAD_DIM = 128
HW = HEADS * HEAD_DIM
CHUNK = 64
LEFT_CHUNKS = 8
MAX_REL = 128
N_REL = (CHUNK - 1) + MAX_REL + 1
CONV_K = 4
EPS = 1e-6
NEG_INF = -1e30
N_DEV = 8
LANES = 128
SUBLANES = 8
VMEM_LIMIT = 56 * 1024 * 1024

MATMUL_WHOLE_K = 2048

ATT_QB = 256
ATT_KW = ATT_QB + LEFT_CHUNKS * CHUNK
ATT_PAD = LEFT_CHUNKS * CHUNK
GDN_CB = 8
GDN_GROUP = 32
GDN_SCAN_UNROLL = 4

ADAM_LR = 0.001
ADAM_B1 = 0.9
ADAM_B2 = 0.999
ADAM_EPS = 1e-08
ADAM_WD = 0.01
ADAM_STEP = 10

IN_QZ = 3 * HW + HW
IN_AB0 = IN_QZ
IN_QKVB0 = IN_AB0 + 2 * HEADS
IN_GG0 = IN_QKVB0 + 3 * HW
IN_COLS = IN_GG0 + 2 * D_MODEL

BIG = ("ffn1_w_gu", "ffn1_w_down", "w_in", "w_branch_a", "w_branch_b", "w_out",
       "ffn2_w_gu", "ffn2_w_down", "ple_gate", "ple_proj")
COL_SHARDED = ("ffn1_w_gu", "w_in", "w_branch_a", "w_branch_b", "ffn2_w_gu", "ple_proj")


def _params(semantics=None, **kw):
    return pltpu.CompilerParams(dimension_semantics=semantics, vmem_limit_bytes=VMEM_LIMIT, **kw)


def _pick(n, cands):
    for c in cands:
        if n % c == 0:
            return c
    return n


SMEM_SPEC = pl.BlockSpec(memory_space=pltpu.SMEM)


def _after(token):
    return ([], []) if token is None else ([SMEM_SPEC], [jnp.reshape(token, (1,)).astype(F32)])


def _matmul(a, b, mode, out_dtype, name, after=None):
    halves = (a.ndim == 3 and mode == "nt") or (b.ndim == 3 and mode == "tn")
    if mode == "nn":
        (m, k), (k2, n) = a.shape, b.shape
    elif mode == "nt":
        (m, k), (n, k2) = (a.shape[-2], a.shape[-1] * (a.ndim - 1)), b.shape
    else:
        (k, m), (k2, n) = a.shape, (b.shape[-2], b.shape[-1] * (b.ndim - 1))
    assert k == k2 and a.ndim + b.ndim == (5 if halves else 4), (a.shape, b.shape, mode)
    tm = _pick(m, (1024, 512, 256, 128))
    if halves and mode == "tn":
        tn = _pick(n // 2, (1408, 1024, 512, 256, 128))
        tk = _pick(k, (2048, 1024, 512, 256, 128))
    elif halves:
        tn = _pick(n, (1024, 512, 256, 128))
        tk = _pick(k // 2, (2816, 2048, 1536, 1024, 512, 256, 128))
    else:
        tn = _pick(n, (1024, 512, 256, 128))
        tk = k if k <= MATMUL_WHOLE_K else _pick(k, (2816, 2048, 1536, 1024, 512, 256, 128))
    nk = k // tk
    per_half = (n // 2) // tn if mode == "tn" else (k // 2) // tk
    if mode == "nn":
        a_spec = pl.BlockSpec((tm, tk), lambda i, j, kk: (i, kk))
        b_spec = pl.BlockSpec((tk, tn), lambda i, j, kk: (kk, j))
        dims = (((1,), (0,)), ((), ()))
    elif mode == "nt":
        a_spec = pl.BlockSpec((tm, tk), lambda i, j, kk: (i, kk))
        b_spec = pl.BlockSpec((tn, tk), lambda i, j, kk: (j, kk))
        dims = (((1,), (1,)), ((), ()))
        if halves:
            a_spec = pl.BlockSpec((None, tm, tk), lambda i, j, kk: (kk // per_half, i, kk % per_half))
    else:
        a_spec = pl.BlockSpec((tk, tm), lambda i, j, kk: (kk, i))
        b_spec = pl.BlockSpec((tk, tn), lambda i, j, kk: (kk, j))
        dims = (((0,), (0,)), ((), ()))
        if halves:
            b_spec = pl.BlockSpec((None, tk, tn), lambda i, j, kk: (j // per_half, kk, j % per_half))

    after_specs, after_args = _after(after)

    def body(a_ref, b_ref, *rest):
        o_ref, acc = rest[len(after_args)], rest[len(after_args) + 1:]
        prod = lax.dot_general(a_ref[...].astype(BF16), b_ref[...].astype(BF16), dims, preferred_element_type=F32)
        if nk == 1:
            o_ref[...] = prod.astype(o_ref.dtype)
            return
        acc_ref, kk = acc[0], pl.program_id(2)

        @pl.when(kk == 0)
        def _():
            acc_ref[...] = prod

        @pl.when((kk > 0) & (kk < nk - 1))
        def _():
            acc_ref[...] += prod

        @pl.when(kk == nk - 1)
        def _():
            o_ref[...] = (acc_ref[...] + prod).astype(o_ref.dtype)

    return pl.pallas_call(
        body, name=name,
        out_shape=jax.ShapeDtypeStruct((m, n), out_dtype),
        grid=(m // tm, n // tn, nk),
        in_specs=[a_spec, b_spec] + after_specs,
        out_specs=pl.BlockSpec((tm, tn), lambda i, j, kk: (i, j)),
        scratch_shapes=[pltpu.VMEM((tm, tn), F32)] if nk > 1 else [],
        compiler_params=_params(("parallel", "parallel", "arbitrary")),
    )(a, b, *after_args)


def _rows(fn, row_ins, consts, row_outs, acc_outs, tile, name):
    t_rows = row_ins[0][0].shape[0]
    tile = min(tile, t_rows)
    assert t_rows % tile == 0 and tile % SUBLANES == 0
    n = t_rows // tile
    per = tile // SUBLANES
    last8 = t_rows // SUBLANES - 1
    in_specs = []
    for arr, kind in row_ins:
        c = arr.shape[1]
        if kind == "t":
            in_specs.append(pl.BlockSpec((tile, c), lambda i: (i, 0)))
        elif kind == "p":
            in_specs.append(pl.BlockSpec((SUBLANES, c), lambda i: (jnp.maximum(i * per - 1, 0), 0)))
        else:
            in_specs.append(pl.BlockSpec((SUBLANES, c), lambda i: (jnp.minimum((i + 1) * per, last8), 0)))
    for arr in consts:
        in_specs.append(pl.BlockSpec(arr.shape, lambda i, nd=arr.ndim: (0,) * nd))
    out_shape = [jax.ShapeDtypeStruct((t_rows, c), dt) for c, dt in row_outs]
    out_specs = [pl.BlockSpec((tile, c), lambda i: (i, 0)) for c, _ in row_outs]
    for shp in acc_outs:
        out_shape.append(jax.ShapeDtypeStruct(shp, F32))
        out_specs.append(pl.BlockSpec(shp, lambda i, nd=len(shp): (0,) * nd))
    n_in = len(row_ins) + len(consts)
    n_row_out = len(row_outs)

    def body(*refs):
        i = pl.program_id(0)
        vals = [r[...].astype(F32) for r in refs[:len(row_ins)]]
        res = fn(i, n, *vals, *refs[len(row_ins):n_in])
        outs = refs[n_in:]
        for r, v in zip(outs[:n_row_out], res[:n_row_out]):
            r[...] = v.astype(r.dtype)
        if acc_outs:
            @pl.when(i == 0)
            def _():
                for r in outs[n_row_out:]:
                    r[...] = jnp.zeros_like(r)

            for r, v in zip(outs[n_row_out:], res[n_row_out:]):
                r[...] += v

    res = pl.pallas_call(
        body, name=name, out_shape=out_shape, grid=(n,), in_specs=in_specs, out_specs=out_specs,
        compiler_params=_params(("arbitrary",) if acc_outs else ("parallel",)),
    )(*[a for a, _ in row_ins], *consts)
    return res


def _rms(x, w):
    return x * lax.rsqrt(jnp.mean(x * x, axis=-1, keepdims=True) + EPS) * w


def _l2n(x):
    return x * lax.rsqrt(jnp.sum(x * x, axis=-1, keepdims=True) + EPS)


def _sigmoid(x):
    return 1.0 / (1.0 + jnp.exp(-x))


def _silu(x):
    return x * _sigmoid(x)


def _softplus(x):
    return jnp.maximum(x, 0.0) + jnp.log(1.0 + jnp.exp(-jnp.abs(x)))


def _heads(fn, *xs):
    nh = xs[0].shape[1] // HEAD_DIM
    return jnp.concatenate(
        [fn(*[x[:, h * HEAD_DIM:(h + 1) * HEAD_DIM] for x in xs]) for h in range(nh)], axis=1)


def _colsum(x):
    return jnp.sum(x, axis=0, keepdims=True)


def _gated_norm(o, z, w):
    return _heads(lambda oh, zh: _rms(oh, w) * _silu(zh), o, z)


def _mix(gg, ta, tb):
    return _sigmoid(gg[:, :D_MODEL]) * ta + _sigmoid(gg[:, D_MODEL:]) * tb


def _gdn_post(y):
    a = _silu(y)
    q = _heads(lambda v: _l2n(v) * (HEAD_DIM ** -0.5), a[:, :HW])
    k = _heads(_l2n, a[:, HW:2 * HW])
    return q, k, a[:, 2 * HW:]


NN = (((1,), (0,)), ((), ()))
NT = (((1,), (1,)), ((), ()))
TN = (((0,), (0,)), ((), ()))


def _dg(a, b, dims):
    return lax.dot_general(a, b, dims, preferred_element_type=F32)


def _split2(x):
    hi = x.astype(BF16)
    return hi, (x - hi.astype(F32)).astype(BF16)


def _split3(x):
    hi = x.astype(BF16)
    r = x - hi.astype(F32)
    mid = r.astype(BF16)
    return hi, mid, (r - mid.astype(F32)).astype(BF16)


def _dg3(a, b, dims):
    ah, al = _split2(a)
    bh, bl = _split2(b)
    return _dg(ah, bh, dims) + (_dg(ah, bl, dims) + _dg(al, bh, dims))


BNN = (((2,), (1,)), ((0,), (0,)))
BNT = (((2,), (2,)), ((0,), (0,)))
BTN = (((1,), (1,)), ((0,), (0,)))


@jax.custom_vjp
def _mm3(a, b):
    return _dg3(a, b, BNN)


_mm3.defvjp(lambda a, b: (_dg3(a, b, BNN), (a, b)),
            lambda res, g: (_dg3(g, res[1], BNT), _dg3(res[0], g, BTN)))


def _xm(x, m, dims):
    mb = m.astype(BF16)
    parts = _split3(x)
    return _dg(parts[0], mb, dims) + (_dg(parts[1], mb, dims) + _dg(parts[2], mb, dims))


def _mx(m, x, dims):
    mb = m.astype(BF16)
    parts = _split3(x)
    return _dg(mb, parts[0], dims) + (_dg(mb, parts[1], dims) + _dg(mb, parts[2], dims))


@jax.custom_vjp
def _times_const(x, m):
    return _xm(x, m, NN)


_times_const.defvjp(lambda x, m: (_xm(x, m, NN), m),
                    lambda m, g: (_xm(g, m, NT), jnp.zeros_like(m)))


@jax.custom_vjp
def _const_times(m, x):
    return _mx(m, x, NN)


_const_times.defvjp(lambda m, x: (_mx(m, x, NN), m),
                    lambda m, g: (jnp.zeros_like(m), _mx(m, g, TN)))


@jax.custom_vjp
def _lane_mean_cols(x, avg):
    return _mx(avg, x, BNT)


_lane_mean_cols.defvjp(lambda x, avg: (_mx(avg, x, BNT), avg),
                       lambda avg, g: (_xm(g, avg, BTN), jnp.zeros_like(avg)))


def _gdn_gates(ab, alog, dtb, e_g, e_b):
    t = ab.shape[0]
    g = -jnp.exp(alog) * _softplus(ab + dtb)
    beta = _sigmoid(ab)
    ri = lax.broadcasted_iota(jnp.int32, (t, t), 0)
    ci = lax.broadcasted_iota(jnp.int32, (t, t), 1)
    shift = CHUNK.bit_length() - 1
    same = jnp.right_shift(ri, shift) == jnp.right_shift(ci, shift)
    tril = jnp.where(same & (ri >= ci), 1.0, 0.0).astype(F32)
    gc = _const_times(tril, g)
    return _times_const(gc, e_g), _times_const(beta, e_b)


def _shift_down(x, halo, s, i):
    if s == 0:
        return x
    halo = jnp.where(i == 0, 0.0, halo)
    xr = pltpu.roll(x, s, 0)
    hr = pltpu.roll(halo, s, 0)
    row = lax.broadcasted_iota(jnp.int32, (SUBLANES, x.shape[1]), 0)
    top = jnp.where(row < s, hr, xr[:SUBLANES])
    return jnp.concatenate([top, xr[SUBLANES:]], axis=0)


def _shift_up(x, halo, s, i, n):
    if s == 0:
        return x
    t = x.shape[0]
    halo = jnp.where(i == n - 1, 0.0, halo)
    xr = pltpu.roll(x, t - s, 0)
    hr = pltpu.roll(halo, SUBLANES - s, 0)
    row = lax.broadcasted_iota(jnp.int32, (SUBLANES, x.shape[1]), 0)
    bot = jnp.where(row >= SUBLANES - s, hr, xr[t - SUBLANES:])
    return jnp.concatenate([xr[:t - SUBLANES], bot], axis=0)


def _conv(pa, prev, cw_ref, i):
    y = pa * cw_ref[CONV_K - 1:CONV_K, :]
    for j in range(CONV_K - 1):
        y = y + _shift_down(pa, prev, CONV_K - 1 - j, i) * cw_ref[j:j + 1, :]
    return y


def _dot_nt(a, b, precision=None):
    return lax.dot_general(a, b, (((1,), (1,)), ((), ())), precision=precision, preferred_element_type=F32)


def _dot_tn(a, b, precision=None):
    return lax.dot_general(a, b, (((0,), (0,)), ((), ())), precision=precision, preferred_element_type=F32)


def _dot(a, b, precision=None):
    return jnp.dot(a, b, precision=precision, preferred_element_type=F32)


def _bf(x):
    return x.astype(BF16)


def _neumann_inverse(lmat):
    nb, c, _ = lmat.shape
    ri = lax.broadcasted_iota(jnp.int32, (nb, c, c), 1)
    ci = lax.broadcasted_iota(jnp.int32, (nb, c, c), 2)
    pw = -lmat
    inv = jnp.where(ri == ci, 1.0, 0.0).astype(F32) + pw
    for _ in range(5):
        pw = _mm3(pw, pw)
        inv = inv + _mm3(inv, pw)
    return inv


@jax.custom_vjp
def _unit_lower_inverse(lmat):
    return _neumann_inverse(lmat)


def _unit_lower_inverse_fwd(lmat):
    inv = _neumann_inverse(lmat)
    return inv, inv


def _unit_lower_inverse_bwd(inv, g):
    return (-_dg3(_dg3(inv, g, BTN), inv, BNT),)


_unit_lower_inverse.defvjp(_unit_lower_inverse_fwd, _unit_lower_inverse_bwd)


def _gdn_chunk(q, k, v, gc, bb):
    nb, c, _ = q.shape
    ri = lax.broadcasted_iota(jnp.int32, (nb, c, c), 1)
    ci = lax.broadcasted_iota(jnp.int32, (nb, c, c), 2)
    incl = ri >= ci
    strict = ri > ci
    g_row = gc[:, :, :c]
    g_col = _lane_mean_cols(gc, jnp.full((nb, c, LANES), 1.0 / LANES, F32))
    decay = jnp.where(incl, jnp.exp(jnp.where(incl, g_row - g_col, 0.0)), 0.0)
    kb = k * bb
    lmat = jnp.where(strict, _dg(_bf(kb), _bf(k), BNT) * decay, 0.0)
    inv = _unit_lower_inverse(lmat)
    egc = jnp.exp(gc)
    u = _mm3(inv, v * bb)
    w = _mm3(inv, kb * egc)
    aqk = _dg(_bf(q), _bf(k), BNT) * decay
    last = lax.broadcasted_iota(jnp.int32, (nb, c, LANES), 1) == c - 1
    tot = jnp.sum(jnp.where(last, gc, 0.0), axis=1, keepdims=True)
    k_tail = k * jnp.exp(tot - gc)
    tail = jnp.broadcast_to(jnp.exp(tot), (nb, SUBLANES, LANES))
    return u, w, aqk, q * egc, k_tail, tail


def _gdn_intra(qn, kn, vv, g_b, beta_b):
    t_rows = qn.shape[0]
    nc = t_rows // CHUNK
    cb = min(GDN_GROUP, nc)
    rows = cb * CHUNK
    col = pl.BlockSpec((rows, HEAD_DIM), lambda h, b: (b, h))

    def body(q_ref, k_ref, v_ref, g_ref, b_ref, u_ref, w_ref, a_ref, qd_ref, kt_ref, tl_ref):
        def group(gi, carry):
            r = pl.ds(pl.multiple_of(gi * (grp * CHUNK), grp * CHUNK), grp * CHUNK)
            ins = [ref[r, :].reshape(grp, CHUNK, HEAD_DIM) for ref in (q_ref, k_ref, v_ref, g_ref, b_ref)]
            u, w, aqk, qd, kt, tl = _gdn_chunk(*ins)
            for ref, val in ((u_ref, u), (w_ref, w), (qd_ref, qd), (kt_ref, kt)):
                ref[r, :] = val.reshape(grp * CHUNK, HEAD_DIM)
            a_ref[0, r, :] = aqk.reshape(grp * CHUNK, CHUNK)
            tl_ref[0, pl.ds(gi * grp, grp)] = tl
            return carry

        grp = min(GDN_GROUP, cb)
        lax.fori_loop(0, cb // grp, group, 0)

    full = jax.ShapeDtypeStruct((t_rows, HW), F32)
    return pl.pallas_call(
        body, name="gdn_intra_fwd",
        out_shape=[full, full, jax.ShapeDtypeStruct((HEADS, t_rows, CHUNK), F32), full, full,
                   jax.ShapeDtypeStruct((HEADS, nc, SUBLANES, LANES), F32)],
        grid=(HEADS, nc // cb),
        in_specs=[col] * 5,
        out_specs=[col, col, pl.BlockSpec((1, rows, CHUNK), lambda h, b: (h, b, 0)), col, col,
                   pl.BlockSpec((1, cb, SUBLANES, LANES), lambda h, b: (h, b, 0, 0))],
        compiler_params=_params(("parallel", "parallel")),
    )(qn, kn, vv, g_b, beta_b)


def _gdn_intra_bwd(qn, kn, vv, g_b, beta_b, du, dw, da, dqd, dkt, dtl):
    t_rows = qn.shape[0]
    nc = t_rows // CHUNK
    cb = min(GDN_GROUP, nc)
    rows = cb * CHUNK
    col = pl.BlockSpec((rows, HEAD_DIM), lambda h, b: (b, h))
    a_spec = pl.BlockSpec((1, rows, CHUNK), lambda h, b: (h, b, 0))
    tl_spec = pl.BlockSpec((1, cb, SUBLANES, LANES), lambda h, b: (h, b, 0, 0))

    def body(q_ref, k_ref, v_ref, g_ref, b_ref, du_ref, dw_ref, da_ref, dqd_ref, dkt_ref, dtl_ref,
             dq_ref, dk_ref, dv_ref, dg_ref, db_ref):
        def group(gi, carry):
            r = pl.ds(pl.multiple_of(gi * (grp * CHUNK), grp * CHUNK), grp * CHUNK)
            wide = (grp, CHUNK, HEAD_DIM)
            ins = [ref[r, :].reshape(wide) for ref in (q_ref, k_ref, v_ref, g_ref, b_ref)]
            cts = (du_ref[r, :].reshape(wide), dw_ref[r, :].reshape(wide),
                   da_ref[0, r, :].reshape(grp, CHUNK, CHUNK), dqd_ref[r, :].reshape(wide),
                   dkt_ref[r, :].reshape(wide), dtl_ref[0, pl.ds(gi * grp, grp)])
            grads = jax.vjp(_gdn_chunk, *ins)[1](cts)
            for ref, val in zip((dq_ref, dk_ref, dv_ref, dg_ref, db_ref), grads):
                ref[r, :] = val.reshape(grp * CHUNK, HEAD_DIM)
            return carry

        grp = min(GDN_GROUP, cb)
        lax.fori_loop(0, cb // grp, group, 0)

    full = jax.ShapeDtypeStruct((t_rows, HW), F32)
    return pl.pallas_call(
        body, name="gdn_intra_bwd",
        out_shape=[full] * 5,
        grid=(HEADS, nc // cb),
        in_specs=[col] * 7 + [a_spec, col, col, tl_spec],
        out_specs=[col] * 5,
        compiler_params=_params(("parallel", "parallel")),
    )(qn, kn, vv, g_b, beta_b, du, dw, da, dqd, dkt, dtl)


def _head_cols(h):
    return slice(h * HEAD_DIM, (h + 1) * HEAD_DIM)


def _gdn_scan(u, w, aqk, qd, kt, tl):
    t_rows = u.shape[0]
    nc = t_rows // CHUNK
    cb = min(GDN_CB, nc)
    rows = cb * CHUNK
    wide = pl.BlockSpec((rows, HW), lambda b: (b, 0))

    def body(u_ref, w_ref, a_ref, qd_ref, kt_ref, tl_ref, o_ref, s_out_ref, s_ref):
        @pl.when(pl.program_id(0) == 0)
        def _():
            s_ref[...] = jnp.zeros_like(s_ref)

        def chunk(ci, carry):
            r = pl.ds(pl.multiple_of(ci * CHUNK, CHUNK), CHUNK)
            for h in range(HEADS):
                hc = _head_cols(h)
                s = s_ref[h]
                s_out_ref[ci, h] = s
                sb = _bf(s)
                vn = u_ref[r, hc] - _dot(_bf(w_ref[r, hc]), sb)
                vnb = _bf(vn)
                o_ref[r, hc] = _dot(_bf(qd_ref[r, hc]), sb) + _dot(_bf(a_ref[h, r, :]), vnb)
                s_ref[h] = s * tl_ref[h, ci, 0:1, :] + _dot_tn(_bf(kt_ref[r, hc]), vnb)
            return carry

        lax.fori_loop(0, cb, chunk, 0, unroll=GDN_SCAN_UNROLL)

    return pl.pallas_call(
        body, name="gdn_scan_fwd",
        out_shape=[jax.ShapeDtypeStruct((t_rows, HW), F32),
                   jax.ShapeDtypeStruct((nc, HEADS, HEAD_DIM, HEAD_DIM), F32)],
        grid=(nc // cb,),
        in_specs=[wide, wide, pl.BlockSpec((HEADS, rows, CHUNK), lambda b: (0, b, 0)), wide, wide,
                  pl.BlockSpec((HEADS, cb, SUBLANES, LANES), lambda b: (0, b, 0, 0))],
        out_specs=[wide, pl.BlockSpec((cb, HEADS, HEAD_DIM, HEAD_DIM), lambda b: (b, 0, 0, 0))],
        scratch_shapes=[pltpu.VMEM((HEADS, HEAD_DIM, HEAD_DIM), F32)],
        compiler_params=_params(("arbitrary",)),
    )(u, w, aqk, qd, kt, tl)


def _gdn_scan_bwd(do, u, w, aqk, qd, kt, tl, states):
    t_rows = u.shape[0]
    nc = t_rows // CHUNK
    cb = min(GDN_CB, nc)
    rows = cb * CHUNK
    nb = nc // cb
    wide = pl.BlockSpec((rows, HW), lambda b: (nb - 1 - b, 0))
    a_spec = pl.BlockSpec((HEADS, rows, CHUNK), lambda b: (0, nb - 1 - b, 0))
    tl_spec = pl.BlockSpec((HEADS, cb, SUBLANES, LANES), lambda b: (0, nb - 1 - b, 0, 0))

    def body(do_ref, u_ref, w_ref, a_ref, qd_ref, kt_ref, tl_ref, s_in_ref,
             du_ref, dw_ref, da_ref, dqd_ref, dkt_ref, dtl_ref, ds_ref):
        @pl.when(pl.program_id(0) == 0)
        def _():
            ds_ref[...] = jnp.zeros_like(ds_ref)

        row0 = lax.broadcasted_iota(jnp.int32, (SUBLANES, LANES), 0) == 0

        def chunk(step, carry):
            ci = cb - 1 - step
            r = pl.ds(pl.multiple_of(ci * CHUNK, CHUNK), CHUNK)
            for h in range(HEADS):
                hc = _head_cols(h)
                s = s_in_ref[ci, h]
                ds_next = ds_ref[h]
                sb, dsb = _bf(s), _bf(ds_next)
                wb, ab, ktb, qdb = _bf(w_ref[r, hc]), _bf(a_ref[h, r, :]), _bf(kt_ref[r, hc]), _bf(qd_ref[r, hc])
                dob = _bf(do_ref[r, hc])
                vn = u_ref[r, hc] - _dot(wb, sb)
                vnb = _bf(vn)
                dvn = _dot_tn(ab, dob) + _dot(ktb, dsb)
                dvnb = _bf(dvn)
                du_ref[r, hc] = dvn
                dw_ref[r, hc] = -_dot_nt(dvnb, sb)
                da_ref[h, r, :] = _dot_nt(dob, vnb)
                dqd_ref[r, hc] = _dot_nt(dob, sb)
                dkt_ref[r, hc] = _dot_nt(vnb, dsb)
                dtl_ref[h, ci] = jnp.where(row0, _colsum(s * ds_next), 0.0)
                ds_ref[h] = _dot_tn(qdb, dob) + ds_next * tl_ref[h, ci, 0:1, :] - _dot_tn(wb, dvnb)
            return carry

        lax.fori_loop(0, cb, chunk, 0, unroll=GDN_SCAN_UNROLL)

    full = jax.ShapeDtypeStruct((t_rows, HW), F32)
    return pl.pallas_call(
        body, name="gdn_scan_bwd",
        out_shape=[full, full, jax.ShapeDtypeStruct((HEADS, t_rows, CHUNK), F32), full, full,
                   jax.ShapeDtypeStruct((HEADS, nc, SUBLANES, LANES), F32)],
        grid=(nb,),
        in_specs=[wide, wide, wide, a_spec, wide, wide, tl_spec,
                  pl.BlockSpec((cb, HEADS, HEAD_DIM, HEAD_DIM), lambda b: (nb - 1 - b, 0, 0, 0))],
        out_specs=[wide, wide, a_spec, wide, wide, tl_spec],
        scratch_shapes=[pltpu.VMEM((HEADS, HEAD_DIM, HEAD_DIM), F32)],
        compiler_params=_params(("arbitrary",)),
    )(do, u, w, aqk, qd, kt, tl, states)


def _att_profile_index():
    j = lax.broadcasted_iota(jnp.int32, (SUBLANES, ATT_KW), 1)
    return jnp.clip(ATT_PAD - j, -(CHUNK - 1), MAX_REL) + (CHUNK - 1)


def _att_far_back():
    qi = lax.broadcasted_iota(jnp.int32, (ATT_QB, ATT_KW), 0)
    kj = lax.broadcasted_iota(jnp.int32, (ATT_QB, ATT_KW), 1)
    return kj < qi


def _rotate_rows(x, forward):
    rows, lanes = x.shape
    row = lax.broadcasted_iota(jnp.int32, x.shape, 0)
    for bit in range(rows.bit_length() - 1):
        amount = (1 << bit) if forward else lanes - (1 << bit)
        x = jnp.where(jnp.bitwise_and(jnp.right_shift(row, bit), 1) == 1, pltpu.roll(x, amount, 1), x)
    return x


def _att_in_band():
    qi = lax.broadcasted_iota(jnp.int32, (ATT_QB, ATT_KW), 0)
    kj = lax.broadcasted_iota(jnp.int32, (ATT_QB, ATT_KW), 1)
    shift = CHUNK.bit_length() - 1
    qc = jnp.right_shift(qi, shift)
    kc = jnp.right_shift(kj, shift) - LEFT_CHUNKS
    return (kc <= qc) & (kc >= qc - LEFT_CHUNKS)


def _att_valid(b):
    kj = lax.broadcasted_iota(jnp.int32, (1, ATT_KW), 1)
    return jnp.where(kj + b * ATT_QB >= ATT_PAD, 0.0, NEG_INF)


def _rms_parts(x, w):
    r = lax.rsqrt(jnp.mean(x * x, axis=-1, keepdims=True) + EPS)
    xn = x * r
    return xn * w, xn, r


def _rms_bwd(dy, xn, r, w):
    dxn = dy * w
    dx = r * (dxn - xn * jnp.mean(dxn * xn, axis=-1, keepdims=True))
    return dx, _colsum(dy * xn)


def _att_probs(qb, kb, bias, before_start):
    s = _dot_nt(qb, kb) * (HEAD_DIM ** -0.5) + bias + before_start
    e = jnp.exp(s - jnp.max(s, axis=-1, keepdims=True))
    return e * (1.0 / jnp.sum(e, axis=-1, keepdims=True))


def _att_specs():
    q_spec = pl.BlockSpec((ATT_QB, HEAD_DIM), lambda h, b: (b, h))
    back = ATT_PAD // ATT_QB
    k_specs = [pl.BlockSpec((ATT_QB, HEAD_DIM), lambda h, b, j=j: (jnp.maximum(b + j - back, 0), HEADS + h))
               for j in range(3)]
    v_specs = [pl.BlockSpec((ATT_QB, HEAD_DIM), lambda h, b, j=j: (jnp.maximum(b + j - back, 0), 2 * HEADS + h))
               for j in range(3)]
    w_spec = pl.BlockSpec((1, HEAD_DIM), lambda h, b: (0, 0))
    smem = pl.BlockSpec(memory_space=pltpu.SMEM)
    return q_spec, k_specs, v_specs, w_spec, smem


BIAS_SPEC = pl.BlockSpec((1, ATT_QB, ATT_KW), lambda h, b: (h, 0, 0))


def _expand_rel_bias(rel):
    def body(rel_ref, bias_ref):
        h = pl.program_id(0)
        idx = _att_profile_index()

        def fill(r, acc):
            return jnp.where(idx == r, rel_ref[h, r], acc)

        profile = lax.fori_loop(0, N_REL, fill, jnp.zeros((SUBLANES, ATT_KW), F32))
        table = _rotate_rows(jnp.concatenate([profile] * (ATT_QB // SUBLANES), axis=0), True)
        table = jnp.where(_att_far_back(), rel_ref[h, N_REL - 1], table)
        bias_ref[0] = jnp.where(_att_in_band(), table, NEG_INF)

    return pl.pallas_call(
        body, name="rel_bias_expand",
        out_shape=jax.ShapeDtypeStruct((HEADS, ATT_QB, ATT_KW), F32), grid=(HEADS,),
        in_specs=[pl.BlockSpec(memory_space=pltpu.SMEM)],
        out_specs=pl.BlockSpec((1, ATT_QB, ATT_KW), lambda h: (h, 0, 0)),
        compiler_params=_params(("parallel",)),
    )(rel)


def _attention(pb, qw, kw, bias):
    t_rows = pb.shape[0]
    q_spec, k_specs, v_specs, w_spec, _ = _att_specs()

    def body(q_ref, k0, k1, k2, v0, v1, v2, qw_ref, kw_ref, bias_ref, o_ref):
        b = pl.program_id(1)
        kwin = jnp.concatenate([k0[...], k1[...], k2[...]], axis=0)
        vwin = jnp.concatenate([v0[...], v1[...], v2[...]], axis=0)
        q = _rms(q_ref[...], qw_ref[...])
        k = _rms(kwin, kw_ref[...])
        p = _att_probs(_bf(q), _bf(k), bias_ref[0], _att_valid(b))
        o_ref[...] = _dot(_bf(p), _bf(vwin)).astype(o_ref.dtype)

    return pl.pallas_call(
        body, name="band_attention_fwd",
        out_shape=jax.ShapeDtypeStruct((t_rows, HW), BF16),
        grid=(HEADS, t_rows // ATT_QB),
        in_specs=[q_spec] + k_specs + v_specs + [w_spec, w_spec, BIAS_SPEC],
        out_specs=pl.BlockSpec((ATT_QB, HEAD_DIM), lambda h, b: (b, h)),
        compiler_params=_params(("parallel", "arbitrary")),
    )(pb, pb, pb, pb, pb, pb, pb, qw, kw, bias)


def _attention_bwd(pb, qw, kw, bias, dyb):
    t_rows = pb.shape[0]
    nb = t_rows // ATT_QB
    q_spec, k_specs, v_specs, w_spec, smem = _att_specs()
    pad_rows = t_rows + ATT_PAD
    acc_spec = pl.BlockSpec((pad_rows, HEAD_DIM), lambda h, b: (0, h))

    def body(q_ref, k0, k1, k2, v0, v1, v2, qw_ref, kw_ref, bias_ref, do_ref,
             dq_ref, dk_ref, dv_ref, dqw_ref, dkw_ref, drel_ref, dbias_ref):
        h, b = pl.program_id(0), pl.program_id(1)

        @pl.when(b == 0)
        def _():
            dbias_ref[...] = jnp.zeros_like(dbias_ref)
            dk_ref[...] = jnp.zeros_like(dk_ref)
            dv_ref[...] = jnp.zeros_like(dv_ref)

        @pl.when((b == 0) & (h == 0))
        def _():
            dqw_ref[...] = jnp.zeros_like(dqw_ref)
            dkw_ref[...] = jnp.zeros_like(dkw_ref)

        kwin = jnp.concatenate([k0[...], k1[...], k2[...]], axis=0)
        vwin = jnp.concatenate([v0[...], v1[...], v2[...]], axis=0)
        scale = HEAD_DIM ** -0.5
        qw_, kw_ = qw_ref[...], kw_ref[...]
        q, qn, rq = _rms_parts(q_ref[...], qw_)
        k, kn, rk = _rms_parts(kwin, kw_)
        qb, kb, dob = _bf(q), _bf(k), _bf(do_ref[...])
        p = _att_probs(qb, kb, bias_ref[0], _att_valid(b))
        dp = _dot_nt(dob, _bf(vwin))
        ds = p * (dp - jnp.sum(p * dp, axis=-1, keepdims=True))
        dbias_ref[...] += ds
        ds = _bf(ds)
        dq, dqw = _rms_bwd(_dot(ds, kb) * scale, qn, rq, qw_)
        dk, dkw = _rms_bwd(_dot_tn(ds, qb) * scale, kn, rk, kw_)
        dq_ref[...] = dq.astype(dq_ref.dtype)
        win = pl.ds(pl.multiple_of(b * ATT_QB, ATT_QB), ATT_KW)
        dk_ref[win, :] += dk
        dv_ref[win, :] += _dot_tn(_bf(p), dob)
        dqw_ref[...] += dqw
        dkw_ref[...] += dkw

        @pl.when(b == nb - 1)
        def _():
            tot, far = dbias_ref[...], _att_far_back()
            far_sum = jnp.sum(jnp.where(far, tot, 0.0))
            per_offset = _colsum(_rotate_rows(jnp.where(far, 0.0, tot), False))
            idx = _att_profile_index()
            first_row = lax.broadcasted_iota(jnp.int32, idx.shape, 0) == 0
            spread = jnp.where(first_row, per_offset, 0.0)

            def reduce(r, carry):
                drel_ref[h, r] = jnp.sum(jnp.where(idx == r, spread, 0.0)) + jnp.where(r == N_REL - 1, far_sum, 0.0)
                return carry

            lax.fori_loop(0, N_REL, reduce, 0)

    return pl.pallas_call(
        body, name="band_attention_bwd",
        out_shape=[jax.ShapeDtypeStruct((t_rows, HW), BF16),
                   jax.ShapeDtypeStruct((pad_rows, HW), F32), jax.ShapeDtypeStruct((pad_rows, HW), F32),
                   jax.ShapeDtypeStruct((1, HEAD_DIM), F32), jax.ShapeDtypeStruct((1, HEAD_DIM), F32),
                   jax.ShapeDtypeStruct((HEADS, N_REL), F32)],
        grid=(HEADS, nb),
        in_specs=[q_spec] + k_specs + v_specs + [w_spec, w_spec, BIAS_SPEC, q_spec],
        out_specs=[q_spec, acc_spec, acc_spec, w_spec, w_spec, smem],
        scratch_shapes=[pltpu.VMEM((ATT_QB, ATT_KW), F32)],
        compiler_params=_params(("arbitrary", "arbitrary")),
    )(pb, pb, pb, pb, pb, pb, pb, qw, kw, bias, dyb)


def _me():
    return lax.axis_index("x"), lax.axis_index("y"), lax.axis_index("c")


def _index(x, y, c):
    return 4 * x + 2 * y + c


HBM_SPEC = pl.BlockSpec(memory_space=pl.ANY)


def _block(ref, kind, d, r, c):
    if kind == "all":
        return ref
    if kind == "rows":
        return ref.at[pl.ds(d * r, r), :]
    if kind == "win":
        return ref.at[:, pl.ds(d * WIN_STEP, c)]
    return ref.at[:, pl.ds(d * c, c)]


def _all_gather(shards, kinds, n_gather):
    n = len(shards)

    def body(*refs):
        x_refs, out_refs = refs[:n], refs[n:2 * n]
        send_sems, recv_sems, local_sems = refs[2 * n:]
        x, y, c = _me()
        me, sibling = (x, y, c), (x, y, 1 - c)
        chips = [(1 - x, y), (x, 1 - y), (1 - x, 1 - y)]

        def copy(i, k, blk, to, src=None):
            r_, c_ = shards[i].shape
            dst = _block(out_refs[i], kinds[i], _index(*blk), r_, c_)
            return pltpu.make_async_remote_copy(
                src_ref=dst if src is None else src, dst_ref=dst,
                send_sem=send_sems.at[i, k], recv_sem=recv_sems.at[i, k], device_id=to, device_id_type=MESH)

        sends, local = [], []
        for i in range(n):
            r_, c_ = shards[i].shape
            mine = pltpu.make_async_copy(x_refs[i], _block(out_refs[i], kinds[i], _index(*me), r_, c_),
                                         local_sems.at[i])
            mine.start()
            local.append(mine)
            if i >= n_gather:
                continue
            first = [copy(i, 0, me, sibling, src=x_refs[i])]
            first += [copy(i, 1 + j, me, (*chip, c), src=x_refs[i]) for j, chip in enumerate(chips)]
            for cp in first:
                cp.start()
            sends += first
        for i in range(n_gather):
            for j, chip in enumerate(chips):
                copy(i, 1 + j, (*chip, c), me).wait_recv()
                passed = copy(i, 4 + j, (*chip, c), sibling)
                passed.start()
                sends.append(passed)
        for i in range(n_gather):
            copy(i, 0, sibling, me).wait_recv()
            for j, chip in enumerate(chips):
                copy(i, 4 + j, (*chip, 1 - c), me).wait_recv()
        for cp in sends:
            cp.wait_send()
        for cp in local:
            cp.wait()

    def full_shape(s, kind):
        r_, c_ = s.shape
        return (N_DEV * r_, c_) if kind == "rows" else (r_, N_DEV * c_)

    return pl.pallas_call(
        body, name="weights_all_gather",
        out_shape=[jax.ShapeDtypeStruct(full_shape(s, k), s.dtype) for s, k in zip(shards, kinds)],
        in_specs=[HBM_SPEC] * n, out_specs=[HBM_SPEC] * n,
        scratch_shapes=[pltpu.SemaphoreType.DMA((n_gather, 7)), pltpu.SemaphoreType.DMA((n_gather, 7)),
                        pltpu.SemaphoreType.DMA((n,))],
        compiler_params=pltpu.CompilerParams(has_side_effects=True),
    )(*shards)


SEM_SPEC = pl.BlockSpec(memory_space=pltpu.SEMAPHORE)
HBM_ONLY = pl.BlockSpec(memory_space=pltpu.HBM)
DATAFLOW = pltpu.SideEffectType.DATAFLOW_SIDE_EFFECTING


def _peers():
    x, y, c = _me()
    return [(x ^ (k >> 2), y ^ ((k >> 1) & 1), c ^ (k & 1)) for k in range(1, N_DEV)]


def _gather_copies(shapes, kinds):
    def make(src_refs, land_refs, send_sems, recv_sems):
        mine = _index(*_me())
        return [pltpu.make_async_remote_copy(
            src_ref=src_refs[i], dst_ref=_block(land_refs[i], kind, mine, r, c),
            send_sem=send_sems.at[7 * i + k], recv_sem=recv_sems.at[7 * i + k], device_id=peer, device_id_type=MESH)
            for i, ((r, c), kind) in enumerate(zip(shapes, kinds)) for k, peer in enumerate(_peers())]

    return make


def _exchange_copies(shapes, kinds):
    def make(src_refs, land_refs, send_sems, recv_sems):
        mine = _index(*_me())
        return [pltpu.make_async_remote_copy(
            src_ref=_block(src_refs[i], kind, _index(*peer), r, c), dst_ref=land_refs[i].at[mine],
            send_sem=send_sems.at[7 * i + k], recv_sem=recv_sems.at[7 * i + k], device_id=peer, device_id_type=MESH)
            for i, ((r, c), kind) in enumerate(zip(shapes, kinds)) for k, peer in enumerate(_peers())]

    return make


def _place_block(shard, kind, name):
    r, c = shard.shape
    tile = _row_tile(r, c)
    nt = r // tile
    full = (N_DEV * r, c) if kind == "rows" else (r, N_DEV * c)

    def body(me_ref, x_ref, out_ref):
        out_ref[...] = x_ref[...]

    if kind == "rows":
        out_spec = pl.BlockSpec((tile, c), lambda i, me: (me[0] * nt + i, 0))
    else:
        out_spec = pl.BlockSpec((tile, c), lambda i, me: (i, me[0]))
    return pl.pallas_call(
        body, name=name, out_shape=jax.ShapeDtypeStruct(full, shard.dtype),
        grid_spec=pltpu.PrefetchScalarGridSpec(
            num_scalar_prefetch=1, grid=(nt,),
            in_specs=[pl.BlockSpec((tile, c), lambda i, me: (i, 0))], out_specs=out_spec),
        compiler_params=_params(("arbitrary",)),
    )(_my_index_operand(), shard)


def _split_start(srcs, lands, make, name):
    n = len(srcs)

    def body(*refs):
        send_sems, recv_sems = refs[2 * n], refs[2 * n + 1]
        for cp in make(refs[:n], refs[n:2 * n], send_sems, recv_sems):
            cp.start()
        refs[-1][...] = jnp.zeros_like(refs[-1])

    arrays = list(srcs) + list(lands)
    out = pl.pallas_call(
        body, name=name,
        out_shape=(pltpu.SemaphoreType.DMA((7 * n,)), pltpu.SemaphoreType.DMA((7 * n,)),
                   *[pltpu.HBM(a.shape, a.dtype) for a in arrays], jax.ShapeDtypeStruct((SUBLANES, LANES), F32)),
        in_specs=[HBM_ONLY] * (2 * n),
        out_specs=(SEM_SPEC, SEM_SPEC, *[HBM_ONLY] * (2 * n), pl.BlockSpec(memory_space=pltpu.VMEM)),
        input_output_aliases={i: 2 + i for i in range(2 * n)},
        compiler_params=pltpu.CompilerParams(has_side_effects=DATAFLOW),
    )(*[pltpu.with_memory_space_constraint(a, pltpu.HBM) for a in arrays])
    return out[0], out[1], list(out[2:2 + n]), list(out[2 + n:2 + 2 * n]), out[-1]


def _split_wait(send_sems, recv_sems, srcs, lands, after, make, name):
    n = len(srcs)

    def body(*refs):
        for cp in make(refs[:n], refs[n:2 * n], refs[2 * n], refs[2 * n + 1]):
            cp.wait_send()
            cp.wait_recv()

    arrays = list(srcs) + list(lands)
    out = pl.pallas_call(
        body, name=name,
        out_shape=tuple(pltpu.HBM(a.shape, a.dtype) for a in arrays),
        in_specs=[HBM_ONLY] * (2 * n) + [SEM_SPEC, SEM_SPEC, pl.BlockSpec(memory_space=pl.ANY)],
        out_specs=tuple([HBM_ONLY] * (2 * n)),
        input_output_aliases={i: i for i in range(2 * n)},
        compiler_params=pltpu.CompilerParams(has_side_effects=DATAFLOW),
    )(*arrays, send_sems, recv_sems, after)
    return list(out[:n]), list(out[n:])


def _all_reduce_small(vals, name):
    rows, width = vals.shape

    def body(x_ref, out_ref, buf_ref, send_sems, recv_sems):
        x, y, c = _me()
        mine = _index(x, y, c)
        buf_ref[mine] = x_ref[...]
        copies = []
        for k in range(1, N_DEV):
            px, py, pc = x ^ (k >> 2), y ^ ((k >> 1) & 1), c ^ (k & 1)
            copies.append(pltpu.make_async_remote_copy(
                src_ref=x_ref, dst_ref=buf_ref.at[mine],
                send_sem=send_sems.at[k - 1], recv_sem=recv_sems.at[k - 1],
                device_id=(px, py, pc), device_id_type=MESH))
        for cp in copies:
            cp.start()
        for cp in copies:
            cp.wait()
        acc = buf_ref[0]
        for j in range(1, N_DEV):
            acc = acc + buf_ref[j]
        out_ref[...] = acc

    vmem = pl.BlockSpec(memory_space=pltpu.VMEM)
    return pl.pallas_call(
        body, name=name,
        out_shape=jax.ShapeDtypeStruct(vals.shape, F32),
        in_specs=[vmem], out_specs=vmem,
        scratch_shapes=[pltpu.VMEM((N_DEV, rows, width), F32),
                        pltpu.SemaphoreType.DMA((7,)), pltpu.SemaphoreType.DMA((7,))],
        compiler_params=pltpu.CompilerParams(has_side_effects=True),
    )(vals)


def _adamw_math(w, g, m, v):
    m = ADAM_B1 * m + (1.0 - ADAM_B1) * g
    v = ADAM_B2 * v + (1.0 - ADAM_B2) * (g * g)
    m_hat = m / (1.0 - ADAM_B1 ** ADAM_STEP)
    v_hat = v / (1.0 - ADAM_B2 ** ADAM_STEP)
    delta = -ADAM_LR * (m_hat / (jnp.sqrt(v_hat) + ADAM_EPS) + ADAM_WD * w)
    return delta, m, v


ROW_TILE_ELEMS = 384 * 1024


def _row_tile(rows, width):
    best = SUBLANES
    for t in range(SUBLANES, rows + 1, SUBLANES):
        if rows % t == 0 and t * width <= ROW_TILE_ELEMS:
            best = t
    return best


def _sum_received(r_ref, own, me):
    g = None
    for j in range(N_DEV):
        term = jnp.where(me == j, own, r_ref[j].astype(F32))
        g = term if g is None else g + term
    return g


def _my_index_operand():
    return _index(*_me()).astype(jnp.int32).reshape(1)


def _sum_small(recv, own):
    def body(me_ref, r_ref, own_ref, out_ref):
        out_ref[...] = _sum_received(r_ref, own_ref[...], me_ref[0])

    whole = lambda shape: pl.BlockSpec(shape, lambda i, me, nd=len(shape): (0,) * nd)
    return pl.pallas_call(
        body, name="small_grads_sum", out_shape=jax.ShapeDtypeStruct(own.shape, F32),
        grid_spec=pltpu.PrefetchScalarGridSpec(
            num_scalar_prefetch=1, grid=(1,), in_specs=[whole(recv.shape), whole(own.shape)],
            out_specs=whole(own.shape)),
        compiler_params=_params(("arbitrary",)),
    )(_my_index_operand(), recv, own)


def _adamw_recv(recv, grad, kind, w, m, v, name):
    _, rows, width = recv.shape
    tile = _row_tile(rows, width)
    nt = rows // tile

    def body(me_ref, r_ref, own_ref, w_ref, m_ref, v_ref, g_out, d_out, m_out, v_out):
        g = _sum_received(r_ref, own_ref[...].astype(F32), me_ref[0])
        d, mn, vn = _adamw_math(w_ref[...], g, m_ref[...], v_ref[...])
        g_out[...] = g
        d_out[...] = d
        m_out[...] = mn
        v_out[...] = vn

    if kind == "rows":
        own_spec = pl.BlockSpec((tile, width), lambda i, me: (me[0] * nt + i, 0))
    else:
        own_spec = pl.BlockSpec((tile, width), lambda i, me: (i, me[0]))
    spec = pl.BlockSpec((tile, width), lambda i, me: (i, 0))
    shape = jax.ShapeDtypeStruct((rows, width), F32)
    return pl.pallas_call(
        body, name=name, out_shape=[shape] * 4,
        grid_spec=pltpu.PrefetchScalarGridSpec(
            num_scalar_prefetch=1, grid=(nt,),
            in_specs=[pl.BlockSpec((N_DEV, tile, width), lambda i, me: (0, i, 0)), own_spec, spec, spec, spec],
            out_specs=[spec] * 4),
        compiler_params=_params(("parallel",)),
    )(_my_index_operand(), recv, grad, w, m, v)


WIN_STEP = 1408
WIN_W = 1536
IN_SHARD = IN_COLS // N_DEV
IN_PADDED = WIN_STEP * (N_DEV - 1) + WIN_W


def _roll_w_in(shard_padded):
    rows = shard_padded.shape[0]
    tile = _row_tile(rows, WIN_W)

    def body(x_ref, main_ref, edge_ref):
        win = pltpu.roll(x_ref[...], 2 * _index(*_me()), 1).astype(BF16)
        main_ref[...] = win[:, :WIN_STEP]
        edge_ref[...] = win[:, WIN_STEP:]

    return pl.pallas_call(
        body, name="w_in_window",
        out_shape=[jax.ShapeDtypeStruct((rows, WIN_STEP), BF16), jax.ShapeDtypeStruct((rows, WIN_W - WIN_STEP), BF16)],
        grid=(rows // tile,),
        in_specs=[pl.BlockSpec((tile, WIN_W), lambda i: (i, 0))],
        out_specs=[pl.BlockSpec((tile, WIN_STEP), lambda i: (i, 0)),
                   pl.BlockSpec((tile, WIN_W - WIN_STEP), lambda i: (i, 0))],
        compiler_params=_params(("parallel",)),
    )(shard_padded)


def _sum_w_in_windows(recv, grad):
    _, rows, width = recv.shape
    tile = _row_tile(rows, width)

    def body(me_ref, r_ref, g_ref, g_out, own_ref, sem):
        me = me_ref[0]
        rows_i = pl.ds(pl.multiple_of(pl.program_id(0) * tile, tile), tile)
        own = pltpu.make_async_copy(g_ref.at[rows_i, pl.ds(pl.multiple_of(me * WIN_STEP, LANES), width)], own_ref, sem)
        own.start()
        own.wait()
        g_out[...] = pltpu.roll(_sum_received(r_ref, own_ref[...].astype(F32), me), width - 2 * me, 1)

    return pl.pallas_call(
        body, name="w_in_grad_sum", out_shape=jax.ShapeDtypeStruct((rows, width), F32),
        grid_spec=pltpu.PrefetchScalarGridSpec(
            num_scalar_prefetch=1, grid=(rows // tile,),
            in_specs=[pl.BlockSpec((N_DEV, tile, width), lambda i, me: (0, i, 0)), HBM_SPEC],
            out_specs=pl.BlockSpec((tile, width), lambda i, me: (i, 0)),
            scratch_shapes=[pltpu.VMEM((tile, width), BF16), pltpu.SemaphoreType.DMA]),
        compiler_params=_params(("arbitrary",)),
    )(_my_index_operand(), recv, grad)


def _adamw_small(w, g, m, v, name):
    def fn(i, n, w_, g_, m_, v_):
        return _adamw_math(w_, g_, m_, v_)

    r, c = w.shape
    return _rows(fn, [(w, "t"), (g, "t"), (m, "t"), (v, "t")], [], [(c, F32)] * 3, [], _row_tile(r, c), name)


def _norm_fwd(x, w, name):
    return _rows(lambda i, n, x_, w_: (_rms(x_, w_[...]),), [(x, "t")], [w], [(D_MODEL, BF16)], [], 512, name)[0]


def _residual_norm_fwd(x, y, scale, w, name):
    def fn(i, n, x_, y_, w_):
        xn = x_ + scale * y_
        return xn, _rms(xn, w_[...])

    return _rows(fn, [(x, "t"), (y, "t")], [w], [(D_MODEL, F32), (D_MODEL, BF16)], [], 512, name)


def _residual_norm_bwd(x, w, dhs, dres, scale, name):
    nh = len(dhs)

    def fn(i, n, x_, dres_, *rest):
        dh = rest[0]
        for extra in rest[1:nh]:
            dh = dh + extra
        _, vjp = jax.vjp(_rms, x_, rest[nh][...])
        dx, dw = vjp(dh)
        dx = dx + dres_
        return dx, scale * dx, dw

    return _rows(fn, [(x, "t"), (dres, "t")] + [(d, "t") for d in dhs], [w],
                 [(D_MODEL, F32), (D_MODEL, BF16)], [(1, D_MODEL)], 256, name)


FFN_UP_TN = 512


def _ffn_up(h, w_gu, name, after=None):
    t, d = h.shape
    f = w_gu.shape[1] // 2
    tm = _pick(t, (1024, 512, 256, 128))
    nj = f // FFN_UP_TN
    after_specs, after_args = _after(after)

    def body(h_ref, wg_ref, wu_ref, *rest):
        g_ref, u_ref, act_ref = rest[len(after_args):]
        hb = h_ref[...]
        g = jnp.dot(hb, wg_ref[...], preferred_element_type=F32)
        u = jnp.dot(hb, wu_ref[...], preferred_element_type=F32)
        g_ref[...] = g.astype(BF16)
        u_ref[...] = u.astype(BF16)
        act_ref[...] = (_silu(g) * u).astype(BF16)

    out = pl.BlockSpec((tm, FFN_UP_TN), lambda i, j: (i, j))
    return pl.pallas_call(
        body, name=name, out_shape=[jax.ShapeDtypeStruct((t, f), BF16)] * 3, grid=(t // tm, nj),
        in_specs=[pl.BlockSpec((tm, d), lambda i, j: (i, 0)),
                  pl.BlockSpec((d, FFN_UP_TN), lambda i, j: (0, j)),
                  pl.BlockSpec((d, FFN_UP_TN), lambda i, j: (0, j + nj))] + after_specs,
        out_specs=[out, out, out],
        compiler_params=_params(("parallel", "parallel")),
    )(h, w_gu, w_gu, *after_args)


def _project_residual_norm(a, b, x, scale, w, name):
    m, k = a.shape
    n = b.shape[1]
    tm = _pick(m, (512, 256, 128))
    tk = k if k <= MATMUL_WHOLE_K else _pick(k, (1408, 1024, 512, 256, 128))
    nk = k // tk

    def body(a_ref, b_ref, x_ref, w_ref, xo_ref, h_ref, *acc):
        prod = jnp.dot(a_ref[...], b_ref[...], preferred_element_type=F32)

        def finish(y):
            xn = x_ref[...] + scale * y
            xo_ref[...] = xn
            h_ref[...] = _rms(xn, w_ref[...]).astype(BF16)

        if nk == 1:
            finish(prod)
            return
        acc_ref, kk = acc[0], pl.program_id(1)

        @pl.when(kk == 0)
        def _():
            acc_ref[...] = prod

        @pl.when((kk > 0) & (kk < nk - 1))
        def _():
            acc_ref[...] += prod

        @pl.when(kk == nk - 1)
        def _():
            finish(acc_ref[...] + prod)

    row = pl.BlockSpec((tm, n), lambda i, kk: (i, 0))
    return pl.pallas_call(
        body, name=name,
        out_shape=[jax.ShapeDtypeStruct((m, n), F32), jax.ShapeDtypeStruct((m, n), BF16)],
        grid=(m // tm, nk),
        in_specs=[pl.BlockSpec((tm, tk), lambda i, kk: (i, kk)), pl.BlockSpec((tk, n), lambda i, kk: (kk, 0)),
                  row, pl.BlockSpec((1, n), lambda i, kk: (0, 0))],
        out_specs=[row, row],
        scratch_shapes=[pltpu.VMEM((tm, n), F32)] if nk > 1 else [],
        compiler_params=_params(("parallel", "arbitrary")),
    )(a, b, x, w)


def _ffn_fwd(h, w_gu, get_w_down, tag, x, w_next, after=None):
    g, u, act = _ffn_up(h, w_gu, tag + "_gu", after)
    x_new, h_next = _project_residual_norm(act, get_w_down(act), x, 0.5, w_next, tag + "_down")
    return (g, u), act, x_new, h_next


def _ffn_dact(dy, w_down, g, u, name):
    t, d = dy.shape
    f = w_down.shape[0]
    tm = _pick(t, (1024, 512, 256, 128))

    def body(dy_ref, w_ref, g_ref, u_ref, out_ref):
        dact = lax.dot_general(dy_ref[...], w_ref[...], NT, preferred_element_type=F32)
        g_, u_ = g_ref[...].astype(F32), u_ref[...].astype(F32)
        sg = _sigmoid(g_)
        out_ref[0] = (dact * u_ * (sg * (1.0 + g_ * (1.0 - sg)))).astype(BF16)
        out_ref[1] = (dact * (g_ * sg)).astype(BF16)

    tile = pl.BlockSpec((tm, FFN_UP_TN), lambda i, j: (i, j))
    return pl.pallas_call(
        body, name=name, out_shape=jax.ShapeDtypeStruct((2, t, f), BF16), grid=(t // tm, f // FFN_UP_TN),
        in_specs=[pl.BlockSpec((tm, d), lambda i, j: (i, 0)), pl.BlockSpec((FFN_UP_TN, d), lambda i, j: (j, 0)),
                  tile, tile],
        out_specs=pl.BlockSpec((2, tm, FFN_UP_TN), lambda i, j: (0, i, j)),
        compiler_params=_params(("parallel", "parallel")),
    )(dy, w_down, g, u)


def _ffn_bwd(h, gu, act, dy, w_gu, w_down, tag, comm, more=None):
    dgu = _ffn_dact(dy, w_down, gu[0], gu[1], tag + "_dact")
    sent = comm.send(tag + "_gu", {tag + "_w_gu": _matmul(h, dgu, "tn", BF16, tag + "_d_w_gu")})
    sent = sent + comm.send(tag + "_down", {tag + "_w_down": _matmul(act, dy, "tn", BF16, tag + "_d_w_down", sent),
                                            **(more or {})})
    dh = _matmul(dgu, w_gu, "nt", BF16, tag + "_dh")
    return dh, sent


def _expanders():
    e_g = np.zeros((LANES, HW), np.float32)
    e_b = np.zeros((LANES, HW), np.float32)
    for h in range(HEADS):
        e_g[h, h * HEAD_DIM:(h + 1) * HEAD_DIM] = 1.0
        e_b[HEADS + h, h * HEAD_DIM:(h + 1) * HEAD_DIM] = 1.0
    return jnp.asarray(e_g), jnp.asarray(e_b)


def _pad_lanes(v):
    return jnp.pad(v, ((0, 0), (0, LANES - v.shape[1])))


class _LocalWeights:
    def __init__(self, big):
        self.big, self.sent = big, {}

    def arrive(self, group, after):
        return self.big

    def send(self, group, grads):
        self.sent.update(grads)
        return jnp.zeros((), F32)


def _local_step(x, p, tgt, small, comm):
    e_g, e_b = _expanders()
    alog, dtb = _pad_lanes(small["a_log"]), _pad_lanes(small["dt_bias"])
    conv_w = jnp.pad(small["conv_w"], ((0, SUBLANES - CONV_K), (0, 0)))
    rel = _expand_rel_bias(small["rel_bias"])

    h1 = _norm_fwd(x, small["ffn1_norm"], "ffn1_norm")
    big = dict(comm.arrive("ffn1", h1))
    started = big.pop("_token", None)

    def ffn1_w_down(act):
        big.update(comm.arrive("ffn1_down", act))
        return big["ffn1_w_down"]

    gu1, act1, x1, h2 = _ffn_fwd(h1, big["ffn1_w_gu"], ffn1_w_down, "ffn1", x, small["mix_norm"], started)

    big = {**big, **comm.arrive("mixer", h2)}
    w_in = big["w_in"]
    w_qz = w_in[:, :IN_QZ]
    w_ab = jnp.pad(w_in[:, IN_AB0:IN_QKVB0], ((0, 0), (0, LANES - 2 * HEADS)))
    w_qkvb = w_in[:, IN_QKVB0:IN_GG0]
    w_gg = w_in[:, IN_GG0:IN_COLS]
    qz = _matmul(h2, w_qz, "nn", F32, "in_qz")
    ab = _matmul(h2, w_ab, "nn", F32, "in_ab")
    pb = _matmul(h2, w_qkvb, "nn", F32, "in_qkvb")
    gg = _matmul(h2, w_gg, "nn", BF16, "in_gates")
    pa, z = qz[:, :3 * HW], qz[:, 3 * HW:]

    def prep(i, n, pa_, prev_, ab_, cw_, alog_, dtb_, eg_, eb_):
        q, k, v = _gdn_post(_conv(pa_, prev_, cw_, i))
        g_b, beta_b = _gdn_gates(ab_, alog_[...], dtb_[...], eg_[...], eb_[...])
        return q, k, v, g_b, beta_b

    qn, kn, vv, g_b, beta_b = _rows(prep, [(pa, "t"), (pa, "p"), (ab, "t")], [conv_w, alog, dtb, e_g, e_b],
                                    [(HW, F32)] * 5, [], 256, "gdn_prep")
    u, w, aqk, qd, kt, tl = _gdn_intra(qn, kn, vv, g_b, beta_b)
    o, states = _gdn_scan(u, w, aqk, qd, kt, tl)
    ya = _rows(lambda i, n, o_, z_, w_: (_gated_norm(o_, z_, w_[...]),), [(o, "t"), (z, "t")], [small["gdn_norm"]],
               [(HW, BF16)], [], 512, "gdn_gated_norm")[0]

    yb = _attention(pb, small["q_norm"], small["k_norm"], rel)

    big = {**big, **comm.arrive("branches", yb)}
    ta = _matmul(ya, big["w_branch_a"], "nn", BF16, "branch_a")
    tb = _matmul(yb, big["w_branch_b"], "nn", BF16, "branch_b")
    mixed = _rows(lambda i, n, gg_, ta_, tb_: (_mix(gg_, ta_, tb_),), [(gg, "t"), (ta, "t"), (tb, "t")], [],
                  [(D_MODEL, BF16)], [], 256, "mix")[0]
    x2, h3 = _project_residual_norm(mixed, big["w_out"], x1, 1.0, small["ffn2_norm"], "w_out")
    big = {**big, **comm.arrive("tail", h3)}
    gu2, act2, x3, h4 = _ffn_fwd(h3, big["ffn2_w_gu"], lambda act: big["ffn2_w_down"], "ffn2", x2,
                                 small["ple_norm"])
    gp = _matmul(h4, big["ple_gate"], "nn", BF16, "ple_gate")
    pp = _matmul(p, big["ple_proj"], "nn", BF16, "ple_proj")

    def head(i, n, x3_, gp_, pp_, tgt_):
        sg = _sigmoid(gp_)
        err = x3_ + sg * pp_ - tgt_
        dx4 = err * (1.0 / D_MODEL)
        sq = _colsum(err * err)
        part = sq[:, :LANES]
        for j in range(1, D_MODEL // LANES):
            part = part + sq[:, j * LANES:(j + 1) * LANES]
        return dx4, dx4 * pp_ * sg * (1.0 - sg), dx4 * sg, (0.5 / D_MODEL) * part

    dx4, dgp, dpp, loss_lanes = _rows(head, [(x3, "t"), (gp, "t"), (pp, "t"), (tgt, "t")], [],
                                      [(D_MODEL, F32), (D_MODEL, BF16), (D_MODEL, BF16)], [(1, LANES)], 256,
                                      "ple_loss_head")
    loss = jnp.sum(loss_lanes)

    gbig, gsmall = {}, {}
    gbig["ple_proj"] = _matmul(p, dpp, "tn", BF16, "d_ple_proj")
    gbig["ple_gate"] = _matmul(h4, dgp, "tn", BF16, "d_ple_gate")
    dh4 = _matmul(dgp, big["ple_gate"], "nt", BF16, "ple_gate_dh")
    dx3, dy2, gsmall["ple_norm"] = _residual_norm_bwd(x3, small["ple_norm"], [dh4], dx4, 0.5, "ple_norm_bwd")

    dh3, sent = _ffn_bwd(h3, gu2, act2, dy2, big["ffn2_w_gu"], big["ffn2_w_down"], "ffn2", comm,
                         {n: gbig[n] for n in ("ple_proj", "ple_gate")})
    dx2, dx2b, gsmall["ffn2_norm"] = _residual_norm_bwd(x2, small["ffn2_norm"] + sent, [dh3], dx3, 1.0,
                                                        "ffn2_norm_bwd")

    gbig["w_out"] = _matmul(mixed, dx2b, "tn", BF16, "d_w_out")
    dmixed = _matmul(dx2b, big["w_out"], "nt", BF16, "w_out_dx")

    def mix_bwd(i, n, gg_, ta_, tb_, dm_):
        _, vjp = jax.vjp(_mix, gg_, ta_, tb_)
        return vjp(dm_)

    dgg, dta, dtb_ = _rows(mix_bwd, [(gg, "t"), (ta, "t"), (tb, "t"), (dmixed, "t")], [],
                           [(2 * D_MODEL, BF16), (D_MODEL, BF16), (D_MODEL, BF16)], [], 256, "mix_bwd")
    gbig["w_branch_a"] = _matmul(ya, dta, "tn", BF16, "d_branch_a")
    gbig["w_branch_b"] = _matmul(yb, dtb_, "tn", BF16, "d_branch_b")
    dya = _matmul(dta, big["w_branch_a"], "nt", BF16, "branch_a_dx")
    dyb = _matmul(dtb_, big["w_branch_b"], "nt", BF16, "branch_b_dx")

    dq_b, dk_b, dv_b, gsmall["q_norm"], gsmall["k_norm"], gsmall["rel_bias"] = _attention_bwd(
        pb, small["q_norm"], small["k_norm"], rel, dyb)
    dpb = jnp.concatenate([dq_b, dk_b[ATT_PAD:].astype(BF16), dv_b[ATT_PAD:].astype(BF16)], axis=1)

    def gated_bwd(i, n, o_, z_, dya_, w_):
        _, vjp = jax.vjp(_gated_norm, o_, z_, w_[...])
        return vjp(dya_)

    do, dz, gsmall["gdn_norm"] = _rows(gated_bwd, [(o, "t"), (z, "t"), (dya, "t")], [small["gdn_norm"]],
                                       [(HW, F32), (HW, BF16)], [(1, HEAD_DIM)], 256, "gdn_gated_norm_bwd")
    du, dw, da, dqd, dkt, dtl = _gdn_scan_bwd(do, u, w, aqk, qd, kt, tl, states)
    dqn, dkn, dvv, dg_b, dbeta_b = _gdn_intra_bwd(qn, kn, vv, g_b, beta_b, du, dw, da, dqd, dkt, dtl)

    def prep_bwd(i, n, pa_, prev_, ab_, dq_, dk_, dv_, dg_, db_, cw_, alog_, dtb_, eg_, eb_):
        _, vjp = jax.vjp(_gdn_post, _conv(pa_, prev_, cw_, i))
        (dy,) = vjp((dq_, dk_, dv_))
        e_g_, e_b_ = eg_[...], eb_[...]
        _, vjp_g = jax.vjp(lambda a, b, c: _gdn_gates(a, b, c, e_g_, e_b_), ab_, alog_[...], dtb_[...])
        dab, dalog, ddtb = vjp_g((dg_, db_))
        return dy, dab, dalog, ddtb

    dy_conv, dab, dalog, ddtb = _rows(
        prep_bwd, [(pa, "t"), (pa, "p"), (ab, "t"), (dqn, "t"), (dkn, "t"), (dvv, "t"), (dg_b, "t"), (dbeta_b, "t")],
        [conv_w, alog, dtb, e_g, e_b], [(3 * HW, F32), (LANES, BF16)], [(1, LANES), (1, LANES)], 256,
        "gdn_prep_bwd")
    gsmall["a_log"] = dalog[:, :HEADS]
    gsmall["dt_bias"] = ddtb[:, :HEADS]

    def conv_bwd(i, n, dy_, nxt_, pa_, prev_, cw_):
        dpa = dy_ * cw_[CONV_K - 1:CONV_K, :]
        row = lax.broadcasted_iota(jnp.int32, (SUBLANES, dy_.shape[1]), 0)
        dcw = jnp.where(row == CONV_K - 1, _colsum(dy_ * pa_), 0.0)
        for j in range(CONV_K - 1):
            s = CONV_K - 1 - j
            dpa = dpa + _shift_up(dy_, nxt_, s, i, n) * cw_[j:j + 1, :]
            dcw = dcw + jnp.where(row == j, _colsum(dy_ * _shift_down(pa_, prev_, s, i)), 0.0)
        return dpa, dcw

    dpa, dcw = _rows(conv_bwd, [(dy_conv, "t"), (dy_conv, "n"), (pa, "t"), (pa, "p")], [conv_w],
                     [(3 * HW, BF16)], [(SUBLANES, 3 * HW)], 256, "gdn_conv_bwd")
    gsmall["conv_w"] = dcw[:CONV_K]

    dqz = jnp.concatenate([dpa, dz], axis=1)
    d_w_qz = _matmul(h2, dqz, "tn", BF16, "d_in_qz")
    d_w_ab = _matmul(h2, dab, "tn", BF16, "d_in_ab")
    d_w_qkvb = _matmul(h2, dpb, "tn", BF16, "d_in_qkvb")
    d_w_gg = _matmul(h2, dgg, "tn", BF16, "d_in_gates")
    gbig["w_in"] = jnp.concatenate([d_w_qz, d_w_ab[:, :2 * HEADS], d_w_qkvb, d_w_gg,
                                    jnp.zeros((D_MODEL, IN_PADDED - IN_COLS), BF16)], axis=1)
    dh2 = [_matmul(dqz, w_qz, "nt", BF16, "in_qz_dh"), _matmul(dab, w_ab, "nt", BF16, "in_ab_dh"),
           _matmul(dpb, w_qkvb, "nt", BF16, "in_qkvb_dh"), _matmul(dgg, w_gg, "nt", BF16, "in_gates_dh")]
    sent = comm.send("mixer", {n: gbig[n] for n in ("w_out", "w_branch_b", "w_branch_a", "w_in")})
    dx1, dy1, gsmall["mix_norm"] = _residual_norm_bwd(x1, small["mix_norm"] + sent, dh2, dx2, 0.5, "mix_norm_bwd")

    dh1, sent = _ffn_bwd(h1, gu1, act1, dy1, big["ffn1_w_gu"], big["ffn1_w_down"], "ffn1", comm)
    grad_x, _, gsmall["ffn1_norm"] = _residual_norm_bwd(x, small["ffn1_norm"] + sent, [dh1], dx1, 1.0,
                                                        "ffn1_norm_bwd")
    return loss, grad_x, gsmall


GATHER_GROUPS = {"ffn1": ("ffn1_w_gu",),
                 "ffn1_down": ("ffn1_w_down",),
                 "mixer": ("w_in_main", "w_in_edge"),
                 "branches": ("w_branch_a", "w_branch_b", "w_out"),
                 "tail": ("ffn2_w_gu", "ffn2_w_down", "ple_gate", "ple_proj")}
SPLIT_GATHERS = ("ffn1_down", "mixer", "branches", "tail")


def _kind(name):
    return "cols" if name in COL_SHARDED or name.startswith("w_in_") else "rows"


def _merge_w_in(main, edges):
    edge_w = WIN_W - WIN_STEP
    w_in = jnp.pad(main, ((0, 0), (0, edge_w)))
    for d in range(N_DEV):
        at = WIN_STEP * (d + 1)
        w_in = w_in + jnp.pad(edges[:, d * edge_w:(d + 1) * edge_w], ((0, 0), (at, IN_PADDED - at - edge_w)))
    return w_in


class _Fsdp:
    def __init__(self, wts, first):
        self.wts, self.first_token = wts, first
        main, edge = _roll_w_in(jnp.pad(wts["w_in"], ((0, 0), (0, WIN_W - IN_SHARD))))
        self.shards = {n: wts[n].astype(BF16) for n in BIG if n not in ("w_in", "ffn1_w_gu")}
        self.shards.update(w_in_main=main, w_in_edge=edge)
        self.lands = {n: _place_block(self.shards[n], _kind(n), "own_" + n)
                      for group in SPLIT_GATHERS for n in GATHER_GROUPS[group]}
        self.flight, self.sent = {}, {}

    def _gather_first(self, after):
        token = self.first_token + after[0, 0].astype(F32) * 0.0
        me = _index(*_me())
        for n, land in self.lands.items():
            r, c = self.shards[n].shape
            at = (me * r, 0) if _kind(n) == "rows" else (0, me * c)
            token = token + lax.dynamic_slice(land, at, (1, 1))[0, 0].astype(F32) * 0.0
        shard = (self.wts["ffn1_w_gu"] + token).astype(BF16)
        self.shards["ffn1_w_gu"] = shard
        first = _all_gather([shard], [_kind("ffn1_w_gu")], 1)[0]
        token = first[0, 0].astype(F32) * 0.0
        for group in SPLIT_GATHERS:
            names = GATHER_GROUPS[group]
            srcs = [self.shards[n] for n in names]
            lands = [self.lands[n] for n in names]
            make = _gather_copies([s.shape for s in srcs], [_kind(n) for n in names])
            srcs[0] = srcs[0] + token.astype(BF16)
            send_sems, recv_sems, srcs, lands, tok = _split_start(srcs, lands, make, "gather_start_" + group)
            token = token + tok[0, 0]
            self.flight[group] = (send_sems, recv_sems, srcs, lands, make)
        return {"ffn1_w_gu": first, "_token": token}

    def arrive(self, group, after):
        if group == "ffn1":
            return self._gather_first(after)
        send_sems, recv_sems, srcs, lands, make = self.flight[group]
        _, lands = _split_wait(send_sems, recv_sems, srcs, lands, after, make, "gather_wait_" + group)
        full = dict(zip(GATHER_GROUPS[group], lands))
        if group == "mixer":
            full["w_in"] = _merge_w_in(full.pop("w_in_main"), full.pop("w_in_edge"))
        return full

    def send(self, group, grads):
        names = list(grads)
        kinds = ["all" if n == "small" else "win" if n == "w_in" else _kind(n) for n in names]
        shapes = [grads[n].shape if n == "small" else (D_MODEL, WIN_W) if n == "w_in" else self.shards[n].shape
                  for n in names]
        srcs = [grads[n] for n in names]
        lands = [lax.empty((N_DEV,) + tuple(s), g.dtype) for s, g in zip(shapes, srcs)]
        make = _exchange_copies(shapes, kinds)
        send_sems, recv_sems, srcs, lands, tok = _split_start(srcs, lands, make, "grads_start_" + group)
        self.sent[group] = (names, kinds, send_sems, recv_sems, srcs, lands, make)
        return tok[0, 0]

    def received(self, group, after):
        names, kinds, send_sems, recv_sems, srcs, lands, make = self.sent[group]
        srcs, lands = _split_wait(send_sems, recv_sems, srcs, lands, after, make, "grads_wait_" + group)
        return {n: (k, g, r) for n, k, g, r in zip(names, kinds, srcs, lands)}


SMALL_ROWS = ("ffn1_norm", "mix_norm", "ffn2_norm", "ple_norm", "gdn_norm", "q_norm", "k_norm", "a_log", "dt_bias",
              "rel_bias", "conv_w")


def _pack_small(vals):
    rows = []
    for n in SMALL_ROWS:
        v = vals[n]
        if n == "rel_bias":
            v = jnp.pad(v, ((0, 0), (0, 2 * LANES - N_REL)))
        elif n in ("a_log", "dt_bias"):
            v = _pad_lanes(v)
        rows.append(v.reshape(-1, LANES))
    packed = jnp.concatenate(rows, axis=0)
    return jnp.pad(packed, ((0, -packed.shape[0] % SUBLANES), (0, 0)))


def _unpack_small(packed, shapes):
    out, off = {}, 0
    for n in SMALL_ROWS:
        shp = shapes[n]
        if n == "rel_bias":
            out[n] = packed[off:off + 2 * HEADS].reshape(HEADS, 2 * LANES)[:, :N_REL]
            off += 2 * HEADS
        elif n in ("a_log", "dt_bias"):
            out[n] = packed[off:off + 1, :HEADS]
            off += 1
        else:
            r = int(np.prod(shp)) // LANES
            out[n] = packed[off:off + r].reshape(shp)
            off += r
    return out


WEIGHTS = ("ffn1_norm", "ffn1_w_gu", "ffn1_w_down", "mix_norm", "w_in", "conv_w", "a_log", "dt_bias", "gdn_norm",
           "q_norm", "k_norm", "rel_bias", "w_branch_a", "w_branch_b", "w_out", "ffn2_norm", "ffn2_w_gu",
           "ffn2_w_down", "ple_norm", "ple_gate", "ple_proj")


def kernel(x, p, ffn1_norm, ffn1_w_gu, ffn1_w_down, mix_norm, w_in, conv_w, a_log, dt_bias, gdn_norm, q_norm, k_norm, rel_bias, w_branch_a, w_branch_b, w_out, ffn2_norm, ffn2_w_gu, ffn2_w_down, ple_norm, ple_gate, ple_proj, loss_target, m_ffn1_norm, m_ffn1_w_gu, m_ffn1_w_down, m_mix_norm, m_w_in, m_conv_w, m_a_log, m_dt_bias, m_gdn_norm, m_q_norm, m_k_norm, m_rel_bias, m_w_branch_a, m_w_branch_b, m_w_out, m_ffn2_norm, m_ffn2_w_gu, m_ffn2_w_down, m_ple_norm, m_ple_gate, m_ple_proj, v_ffn1_norm, v_ffn1_w_gu, v_ffn1_w_down, v_mix_norm, v_w_in, v_conv_w, v_a_log, v_dt_bias, v_gdn_norm, v_q_norm, v_k_norm, v_rel_bias, v_w_branch_a, v_w_branch_b, v_w_out, v_ffn2_norm, v_ffn2_w_gu, v_ffn2_w_down, v_ple_norm, v_ple_gate, v_ple_proj):
    args = dict(locals())
    def layer0(v):
        return v[0] if v.ndim == 3 else v

    wts = {n: layer0(args[n]) for n in WEIGHTS}
    mom = {n: layer0(args["m_" + n]) for n in WEIGHTS}
    var = {n: layer0(args["v_" + n]) for n in WEIGHTS}
    x2d, p2d, tgt = x[0], p[0, 0], loss_target[0]
    my_index = _index(*_me())

    small = {n: wts[n] for n in SMALL_ROWS if n != "conv_w"}
    conv_shard = wts["conv_w"]
    conv_cols = conv_shard.shape[1]
    conv_packed = jnp.zeros((SUBLANES, N_DEV * conv_cols), F32)
    conv_packed = lax.dynamic_update_slice(conv_packed, jnp.pad(conv_shard, ((0, SUBLANES - CONV_K), (0, 0))),
                                           (0, my_index * conv_cols))
    small["conv_w"] = _all_reduce_small(conv_packed.reshape(-1, LANES), "conv_w_gather").reshape(SUBLANES, -1)[:CONV_K]

    fsdp = _Fsdp(wts, small["conv_w"][0, 0] * 0.0)

    loss, grad_x, gsmall = _local_step(x2d, p2d, tgt, small, fsdp)
    loss = lax.psum(loss, ("x", "y", "c"))

    fsdp.send("small", {"small": _pack_small(gsmall)})

    outs_big, after = {}, grad_x
    for group in list(fsdp.sent):
        for n, (kind, grad, recv) in fsdp.received(group, after).items():
            if n == "small":
                small_sum = _sum_small(recv, grad)
            elif n == "w_in":
                g_in = _sum_w_in_windows(recv, grad)[:, :IN_SHARD]
                outs_big[n] = [g_in] + list(_adamw_small(wts[n], g_in, mom[n], var[n], "adamw_w_in"))
            else:
                outs_big[n] = _adamw_recv(recv, grad, kind, wts[n], mom[n], var[n], "adamw_" + n)
            after = small_sum if n == "small" else outs_big[n][1]

    small_shapes = {n: (small[n].shape if n != "conv_w" else (CONV_K, N_DEV * conv_cols)) for n in SMALL_ROWS}
    gsum = _unpack_small(small_sum, small_shapes)
    gsum["conv_w"] = lax.dynamic_slice(gsum["conv_w"], (0, my_index * conv_cols), (CONV_K, conv_cols))
    rep = [n for n in SMALL_ROWS if n != "conv_w"]
    rep_shapes = {n: small_shapes[n] for n in rep}

    def pack_rep(vals):
        return _pack_small({**{n: vals[n] for n in rep}, "conv_w": jnp.zeros((CONV_K, LANES), F32)})

    def unpack_rep(packed):
        return _unpack_small(packed, {**rep_shapes, "conv_w": (CONV_K, LANES)})

    outs_small = [unpack_rep(o) for o in _adamw_small(pack_rep(wts), pack_rep(gsum), pack_rep(mom), pack_rep(var),
                                                      "adamw_replicated")]
    pad8 = functools.partial(jnp.pad, pad_width=((0, SUBLANES - CONV_K), (0, 0)))
    outs_conv = [o[:CONV_K] for o in _adamw_small(pad8(conv_shard), pad8(gsum["conv_w"]), pad8(mom["conv_w"]),
                                                   pad8(var["conv_w"]), "adamw_conv")]

    def leaf(kind, n):
        if n in BIG:
            return outs_big[n][kind][None]
        if n == "conv_w":
            return (gsum["conv_w"] if kind == 0 else outs_conv[kind - 1])[None]
        return (gsum[n] if kind == 0 else outs_small[kind - 1][n]).reshape(args[n].shape)

    result = [loss, grad_x[None]]
    for kind in range(4):
        result += [leaf(kind, n) for n in WEIGHTS]
    return tuple(result)
```

```python
import functools

import numpy as np
import jax
import jax.numpy as jnp
from jax import lax
from jax.experimental import pallas as pl
from jax.experimental.pallas import tpu as pltpu

F32 = jnp.float32
BF16 = jnp.bfloat16
HIGHEST = lax.Precision.HIGHEST
MESH = pl.DeviceIdType.MESH

D_MODEL = 2048
D_FF = 5632
HEADS = 8
HEAD_DIM = 128
HW = HEADS * HEAD_DIM
CHUNK = 64
LEFT_CHUNKS = 8
MAX_REL = 128
N_REL = (CHUNK - 1) + MAX_REL + 1
CONV_K = 4
EPS = 1e-6
NEG_INF = -1e30
N_DEV = 8
LANES = 128
SUBLANES = 8
VMEM_LIMIT = 56 * 1024 * 1024

MATMUL_WHOLE_K = 2048

ATT_QB = 256
ATT_KW = ATT_QB + LEFT_CHUNKS * CHUNK
ATT_PAD = LEFT_CHUNKS * CHUNK
GDN_CB = 8
GDN_GROUP = 32
GDN_SCAN_UNROLL = 4

ADAM_LR = 0.001
ADAM_B1 = 0.9
ADAM_B2 = 0.999
ADAM_EPS = 1e-08
ADAM_WD = 0.01
ADAM_STEP = 10

IN_QZ = 3 * HW + HW
IN_AB0 = IN_QZ
IN_QKVB0 = IN_AB0 + 2 * HEADS
IN_GG0 = IN_QKVB0 + 3 * HW
IN_COLS = IN_GG0 + 2 * D_MODEL

BIG = ("ffn1_w_gu", "ffn1_w_down", "w_in", "w_branch_a", "w_branch_b", "w_out",
       "ffn2_w_gu", "ffn2_w_down", "ple_gate", "ple_proj")
COL_SHARDED = ("ffn1_w_gu", "w_in", "w_branch_a", "w_branch_b", "ffn2_w_gu", "ple_proj")


def _params(semantics=None, **kw):
    return pltpu.CompilerParams(dimension_semantics=semantics, vmem_limit_bytes=VMEM_LIMIT, **kw)


def _pick(n, cands):
    for c in cands:
        if n % c == 0:
            return c
    return n


SMEM_SPEC = pl.BlockSpec(memory_space=pltpu.SMEM)


def _after(token):
    return ([], []) if token is None else ([SMEM_SPEC], [jnp.reshape(token, (1,)).astype(F32)])


def _matmul(a, b, mode, out_dtype, name, after=None):
    halves = (a.ndim == 3 and mode == "nt") or (b.ndim == 3 and mode == "tn")
    if mode == "nn":
        (m, k), (k2, n) = a.shape, b.shape
    elif mode == "nt":
        (m, k), (n, k2) = (a.shape[-2], a.shape[-1] * (a.ndim - 1)), b.shape
    else:
        (k, m), (k2, n) = a.shape, (b.shape[-2], b.shape[-1] * (b.ndim - 1))
    assert k == k2 and a.ndim + b.ndim == (5 if halves else 4), (a.shape, b.shape, mode)
    tm = _pick(m, (1024, 512, 256, 128))
    if halves and mode == "tn":
        tn = _pick(n // 2, (1408, 1024, 512, 256, 128))
        tk = _pick(k, (2048, 1024, 512, 256, 128))
    elif halves:
        tn = _pick(n, (1024, 512, 256, 128))
        tk = _pick(k // 2, (2816, 2048, 1536, 1024, 512, 256, 128))
    else:
        tn = _pick(n, (1024, 512, 256, 128))
        tk = k if k <= MATMUL_WHOLE_K else _pick(k, (2816, 2048, 1536, 1024, 512, 256, 128))
    nk = k // tk
    per_half = (n // 2) // tn if mode == "tn" else (k // 2) // tk
    if mode == "nn":
        a_spec = pl.BlockSpec((tm, tk), lambda i, j, kk: (i, kk))
        b_spec = pl.BlockSpec((tk, tn), lambda i, j, kk: (kk, j))
        dims = (((1,), (0,)), ((), ()))
    elif mode == "nt":
        a_spec = pl.BlockSpec((tm, tk), lambda i, j, kk: (i, kk))
        b_spec = pl.BlockSpec((tn, tk), lambda i, j, kk: (j, kk))
        dims = (((1,), (1,)), ((), ()))
        if halves:
            a_spec = pl.BlockSpec((None, tm, tk), lambda i, j, kk: (kk // per_half, i, kk % per_half))
    else:
        a_spec = pl.BlockSpec((tk, tm), lambda i, j, kk: (kk, i))
        b_spec = pl.BlockSpec((tk, tn), lambda i, j, kk: (kk, j))
        dims = (((0,), (0,)), ((), ()))
        if halves:
            b_spec = pl.BlockSpec((None, tk, tn), lambda i, j, kk: (j // per_half, kk, j % per_half))

    after_specs, after_args = _after(after)

    def body(a_ref, b_ref, *rest):
        o_ref, acc = rest[len(after_args)], rest[len(after_args) + 1:]
        prod = lax.dot_general(a_ref[...].astype(BF16), b_ref[...].astype(BF16), dims, preferred_element_type=F32)
        if nk == 1:
            o_ref[...] = prod.astype(o_ref.dtype)
            return
        acc_ref, kk = acc[0], pl.program_id(2)

        @pl.when(kk == 0)
        def _():
            acc_ref[...] = prod

        @pl.when((kk > 0) & (kk < nk - 1))
        def _():
            acc_ref[...] += prod

        @pl.when(kk == nk - 1)
        def _():
            o_ref[...] = (acc_ref[...] + prod).astype(o_ref.dtype)

    return pl.pallas_call(
        body, name=name,
        out_shape=jax.ShapeDtypeStruct((m, n), out_dtype),
        grid=(m // tm, n // tn, nk),
        in_specs=[a_spec, b_spec] + after_specs,
        out_specs=pl.BlockSpec((tm, tn), lambda i, j, kk: (i, j)),
        scratch_shapes=[pltpu.VMEM((tm, tn), F32)] if nk > 1 else [],
        compiler_params=_params(("parallel", "parallel", "arbitrary")),
    )(a, b, *after_args)


def _rows(fn, row_ins, consts, row_outs, acc_outs, tile, name):
    t_rows = row_ins[0][0].shape[0]
    tile = min(tile, t_rows)
    assert t_rows % tile == 0 and tile % SUBLANES == 0
    n = t_rows // tile
    per = tile // SUBLANES
    last8 = t_rows // SUBLANES - 1
    in_specs = []
    for arr, kind in row_ins:
        c = arr.shape[1]
        if kind == "t":
            in_specs.append(pl.BlockSpec((tile, c), lambda i: (i, 0)))
        elif kind == "p":
            in_specs.append(pl.BlockSpec((SUBLANES, c), lambda i: (jnp.maximum(i * per - 1, 0), 0)))
        else:
            in_specs.append(pl.BlockSpec((SUBLANES, c), lambda i: (jnp.minimum((i + 1) * per, last8), 0)))
    for arr in consts:
        in_specs.append(pl.BlockSpec(arr.shape, lambda i, nd=arr.ndim: (0,) * nd))
    out_shape = [jax.ShapeDtypeStruct((t_rows, c), dt) for c, dt in row_outs]
    out_specs = [pl.BlockSpec((tile, c), lambda i: (i, 0)) for c, _ in row_outs]
    for shp in acc_outs:
        out_shape.append(jax.ShapeDtypeStruct(shp, F32))
        out_specs.append(pl.BlockSpec(shp, lambda i, nd=len(shp): (0,) * nd))
    n_in = len(row_ins) + len(consts)
    n_row_out = len(row_outs)

    def body(*refs):
        i = pl.program_id(0)
        vals = [r[...].astype(F32) for r in refs[:len(row_ins)]]
        res = fn(i, n, *vals, *refs[len(row_ins):n_in])
        outs = refs[n_in:]
        for r, v in zip(outs[:n_row_out], res[:n_row_out]):
            r[...] = v.astype(r.dtype)
        if acc_outs:
            @pl.when(i == 0)
            def _():
                for r in outs[n_row_out:]:
                    r[...] = jnp.zeros_like(r)

            for r, v in zip(outs[n_row_out:], res[n_row_out:]):
                r[...] += v

    res = pl.pallas_call(
        body, name=name, out_shape=out_shape, grid=(n,), in_specs=in_specs, out_specs=out_specs,
        compiler_params=_params(("arbitrary",) if acc_outs else ("parallel",)),
    )(*[a for a, _ in row_ins], *consts)
    return res


def _rms(x, w):
    return x * lax.rsqrt(jnp.mean(x * x, axis=-1, keepdims=True) + EPS) * w


def _l2n(x):
    return x * lax.rsqrt(jnp.sum(x * x, axis=-1, keepdims=True) + EPS)


def _sigmoid(x):
    return 1.0 / (1.0 + jnp.exp(-x))


def _silu(x):
    return x * _sigmoid(x)


def _softplus(x):
    return jnp.maximum(x, 0.0) + jnp.log(1.0 + jnp.exp(-jnp.abs(x)))


def _heads(fn, *xs):
    nh = xs[0].shape[1] // HEAD_DIM
    return jnp.concatenate(
        [fn(*[x[:, h * HEAD_DIM:(h + 1) * HEAD_DIM] for x in xs]) for h in range(nh)], axis=1)


def _colsum(x):
    return jnp.sum(x, axis=0, keepdims=True)


def _gated_norm(o, z, w):
    return _heads(lambda oh, zh: _rms(oh, w) * _silu(zh), o, z)


def _mix(gg, ta, tb):
    return _sigmoid(gg[:, :D_MODEL]) * ta + _sigmoid(gg[:, D_MODEL:]) * tb


def _gdn_post(y):
    a = _silu(y)
    q = _heads(lambda v: _l2n(v) * (HEAD_DIM ** -0.5), a[:, :HW])
    k = _heads(_l2n, a[:, HW:2 * HW])
    return q, k, a[:, 2 * HW:]


NN = (((1,), (0,)), ((), ()))
NT = (((1,), (1,)), ((), ()))
TN = (((0,), (0,)), ((), ()))


def _dg(a, b, dims):
    return lax.dot_general(a, b, dims, preferred_element_type=F32)


def _split2(x):
    hi = x.astype(BF16)
    return hi, (x - hi.astype(F32)).astype(BF16)


def _split3(x):
    hi = x.astype(BF16)
    r = x - hi.astype(F32)
    mid = r.astype(BF16)
    return hi, mid, (r - mid.astype(F32)).astype(BF16)


def _dg3(a, b, dims):
    ah, al = _split2(a)
    bh, bl = _split2(b)
    return _dg(ah, bh, dims) + (_dg(ah, bl, dims) + _dg(al, bh, dims))


BNN = (((2,), (1,)), ((0,), (0,)))
BNT = (((2,), (2,)), ((0,), (0,)))
BTN = (((1,), (1,)), ((0,), (0,)))


@jax.custom_vjp
def _mm3(a, b):
    return _dg3(a, b, BNN)


_mm3.defvjp(lambda a, b: (_dg3(a, b, BNN), (a, b)),
            lambda res, g: (_dg3(g, res[1], BNT), _dg3(res[0], g, BTN)))


def _xm(x, m, dims):
    mb = m.astype(BF16)
    parts = _split3(x)
    return _dg(parts[0], mb, dims) + (_dg(parts[1], mb, dims) + _dg(parts[2], mb, dims))


def _mx(m, x, dims):
    mb = m.astype(BF16)
    parts = _split3(x)
    return _dg(mb, parts[0], dims) + (_dg(mb, parts[1], dims) + _dg(mb, parts[2], dims))


@jax.custom_vjp
def _times_const(x, m):
    return _xm(x, m, NN)


_times_const.defvjp(lambda x, m: (_xm(x, m, NN), m),
                    lambda m, g: (_xm(g, m, NT), jnp.zeros_like(m)))


@jax.custom_vjp
def _const_times(m, x):
    return _mx(m, x, NN)


_const_times.defvjp(lambda m, x: (_mx(m, x, NN), m),
                    lambda m, g: (jnp.zeros_like(m), _mx(m, g, TN)))


@jax.custom_vjp
def _lane_mean_cols(x, avg):
    return _mx(avg, x, BNT)


_lane_mean_cols.defvjp(lambda x, avg: (_mx(avg, x, BNT), avg),
                       lambda avg, g: (_xm(g, avg, BTN), jnp.zeros_like(avg)))


def _gdn_gates(ab, alog, dtb, e_g, e_b):
    t = ab.shape[0]
    g = -jnp.exp(alog) * _softplus(ab + dtb)
    beta = _sigmoid(ab)
    ri = lax.broadcasted_iota(jnp.int32, (t, t), 0)
    ci = lax.broadcasted_iota(jnp.int32, (t, t), 1)
    shift = CHUNK.bit_length() - 1
    same = jnp.right_shift(ri, shift) == jnp.right_shift(ci, shift)
    tril = jnp.where(same & (ri >= ci), 1.0, 0.0).astype(F32)
    gc = _const_times(tril, g)
    return _times_const(gc, e_g), _times_const(beta, e_b)


def _shift_down(x, halo, s, i):
    if s == 0:
        return x
    halo = jnp.where(i == 0, 0.0, halo)
    xr = pltpu.roll(x, s, 0)
    hr = pltpu.roll(halo, s, 0)
    row = lax.broadcasted_iota(jnp.int32, (SUBLANES, x.shape[1]), 0)
    top = jnp.where(row < s, hr, xr[:SUBLANES])
    return jnp.concatenate([top, xr[SUBLANES:]], axis=0)


def _shift_up(x, halo, s, i, n):
    if s == 0:
        return x
    t = x.shape[0]
    halo = jnp.where(i == n - 1, 0.0, halo)
    xr = pltpu.roll(x, t - s, 0)
    hr = pltpu.roll(halo, SUBLANES - s, 0)
    row = lax.broadcasted_iota(jnp.int32, (SUBLANES, x.shape[1]), 0)
    bot = jnp.where(row >= SUBLANES - s, hr, xr[t - SUBLANES:])
    return jnp.concatenate([xr[:t - SUBLANES], bot], axis=0)


def _conv(pa, prev, cw_ref, i):
    y = pa * cw_ref[CONV_K - 1:CONV_K, :]
    for j in range(CONV_K - 1):
        y = y + _shift_down(pa, prev, CONV_K - 1 - j, i) * cw_ref[j:j + 1, :]
    return y


def _dot_nt(a, b, precision=None):
    return lax.dot_general(a, b, (((1,), (1,)), ((), ())), precision=precision, preferred_element_type=F32)


def _dot_tn(a, b, precision=None):
    return lax.dot_general(a, b, (((0,), (0,)), ((), ())), precision=precision, preferred_element_type=F32)


def _dot(a, b, precision=None):
    return jnp.dot(a, b, precision=precision, preferred_element_type=F32)


def _bf(x):
    return x.astype(BF16)


def _neumann_inverse(lmat):
    nb, c, _ = lmat.shape
    ri = lax.broadcasted_iota(jnp.int32, (nb, c, c), 1)
    ci = lax.broadcasted_iota(jnp.int32, (nb, c, c), 2)
    pw = -lmat
    inv = jnp.where(ri == ci, 1.0, 0.0).astype(F32) + pw
    for _ in range(5):
        pw = _mm3(pw, pw)
        inv = inv + _mm3(inv, pw)
    return inv


@jax.custom_vjp
def _unit_lower_inverse(lmat):
    return _neumann_inverse(lmat)


def _unit_lower_inverse_fwd(lmat):
    inv = _neumann_inverse(lmat)
    return inv, inv


def _unit_lower_inverse_bwd(inv, g):
    return (-_dg3(_dg3(inv, g, BTN), inv, BNT),)


_unit_lower_inverse.defvjp(_unit_lower_inverse_fwd, _unit_lower_inverse_bwd)


def _gdn_chunk(q, k, v, gc, bb):
    nb, c, _ = q.shape
    ri = lax.broadcasted_iota(jnp.int32, (nb, c, c), 1)
    ci = lax.broadcasted_iota(jnp.int32, (nb, c, c), 2)
    incl = ri >= ci
    strict = ri > ci
    g_row = gc[:, :, :c]
    g_col = _lane_mean_cols(gc, jnp.full((nb, c, LANES), 1.0 / LANES, F32))
    decay = jnp.where(incl, jnp.exp(jnp.where(incl, g_row - g_col, 0.0)), 0.0)
    kb = k * bb
    lmat = jnp.where(strict, _dg(_bf(kb), _bf(k), BNT) * decay, 0.0)
    inv = _unit_lower_inverse(lmat)
    egc = jnp.exp(gc)
    u = _mm3(inv, v * bb)
    w = _mm3(inv, kb * egc)
    aqk = _dg(_bf(q), _bf(k), BNT) * decay
    last = lax.broadcasted_iota(jnp.int32, (nb, c, LANES), 1) == c - 1
    tot = jnp.sum(jnp.where(last, gc, 0.0), axis=1, keepdims=True)
    k_tail = k * jnp.exp(tot - gc)
    tail = jnp.broadcast_to(jnp.exp(tot), (nb, SUBLANES, LANES))
    return u, w, aqk, q * egc, k_tail, tail


def _gdn_intra(qn, kn, vv, g_b, beta_b):
    t_rows = qn.shape[0]
    nc = t_rows // CHUNK
    cb = min(GDN_GROUP, nc)
    rows = cb * CHUNK
    col = pl.BlockSpec((rows, HEAD_DIM), lambda h, b: (b, h))

    def body(q_ref, k_ref, v_ref, g_ref, b_ref, u_ref, w_ref, a_ref, qd_ref, kt_ref, tl_ref):
        def group(gi, carry):
            r = pl.ds(pl.multiple_of(gi * (grp * CHUNK), grp * CHUNK), grp * CHUNK)
            ins = [ref[r, :].reshape(grp, CHUNK, HEAD_DIM) for ref in (q_ref, k_ref, v_ref, g_ref, b_ref)]
            u, w, aqk, qd, kt, tl = _gdn_chunk(*ins)
            for ref, val in ((u_ref, u), (w_ref, w), (qd_ref, qd), (kt_ref, kt)):
                ref[r, :] = val.reshape(grp * CHUNK, HEAD_DIM)
            a_ref[0, r, :] = aqk.reshape(grp * CHUNK, CHUNK)
            tl_ref[0, pl.ds(gi * grp, grp)] = tl
            return carry

        grp = min(GDN_GROUP, cb)
        lax.fori_loop(0, cb // grp, group, 0)

    full = jax.ShapeDtypeStruct((t_rows, HW), F32)
    return pl.pallas_call(
        body, name="gdn_intra_fwd",
        out_shape=[full, full, jax.ShapeDtypeStruct((HEADS, t_rows, CHUNK), F32), full, full,
                   jax.ShapeDtypeStruct((HEADS, nc, SUBLANES, LANES), F32)],
        grid=(HEADS, nc // cb),
        in_specs=[col] * 5,
        out_specs=[col, col, pl.BlockSpec((1, rows, CHUNK), lambda h, b: (h, b, 0)), col, col,
                   pl.BlockSpec((1, cb, SUBLANES, LANES), lambda h, b: (h, b, 0, 0))],
        compiler_params=_params(("parallel", "parallel")),
    )(qn, kn, vv, g_b, beta_b)


def _gdn_intra_bwd(qn, kn, vv, g_b, beta_b, du, dw, da, dqd, dkt, dtl):
    t_rows = qn.shape[0]
    nc = t_rows // CHUNK
    cb = min(GDN_GROUP, nc)
    rows = cb * CHUNK
    col = pl.BlockSpec((rows, HEAD_DIM), lambda h, b: (b, h))
    a_spec = pl.BlockSpec((1, rows, CHUNK), lambda h, b: (h, b, 0))
    tl_spec = pl.BlockSpec((1, cb, SUBLANES, LANES), lambda h, b: (h, b, 0, 0))

    def body(q_ref, k_ref, v_ref, g_ref, b_ref, du_ref, dw_ref, da_ref, dqd_ref, dkt_ref, dtl_ref,
             dq_ref, dk_ref, dv_ref, dg_ref, db_ref):
        def group(gi, carry):
            r = pl.ds(pl.multiple_of(gi * (grp * CHUNK), grp * CHUNK), grp * CHUNK)
            wide = (grp, CHUNK, HEAD_DIM)
            ins = [ref[r, :].reshape(wide) for ref in (q_ref, k_ref, v_ref, g_ref, b_ref)]
            cts = (du_ref[r, :].reshape(wide), dw_ref[r, :].reshape(wide),
                   da_ref[0, r, :].reshape(grp, CHUNK, CHUNK), dqd_ref[r, :].reshape(wide),
                   dkt_ref[r, :].reshape(wide), dtl_ref[0, pl.ds(gi * grp, grp)])
            grads = jax.vjp(_gdn_chunk, *ins)[1](cts)
            for ref, val in zip((dq_ref, dk_ref, dv_ref, dg_ref, db_ref), grads):
                ref[r, :] = val.reshape(grp * CHUNK, HEAD_DIM)
            return carry

        grp = min(GDN_GROUP, cb)
        lax.fori_loop(0, cb // grp, group, 0)

    full = jax.ShapeDtypeStruct((t_rows, HW), F32)
    return pl.pallas_call(
        body, name="gdn_intra_bwd",
        out_shape=[full] * 5,
        grid=(HEADS, nc // cb),
        in_specs=[col] * 7 + [a_spec, col, col, tl_spec],
        out_specs=[col] * 5,
        compiler_params=_params(("parallel", "parallel")),
    )(qn, kn, vv, g_b, beta_b, du, dw, da, dqd, dkt, dtl)


def _head_cols(h):
    return slice(h * HEAD_DIM, (h + 1) * HEAD_DIM)


def _gdn_scan(u, w, aqk, qd, kt, tl):
    t_rows = u.shape[0]
    nc = t_rows // CHUNK
    cb = min(GDN_CB, nc)
    rows = cb * CHUNK
    wide = pl.BlockSpec((rows, HW), lambda b: (b, 0))

    def body(u_ref, w_ref, a_ref, qd_ref, kt_ref, tl_ref, o_ref, s_out_ref, s_ref):
        @pl.when(pl.program_id(0) == 0)
        def _():
            s_ref[...] = jnp.zeros_like(s_ref)

        def chunk(ci, carry):
            r = pl.ds(pl.multiple_of(ci * CHUNK, CHUNK), CHUNK)
            for h in range(HEADS):
                hc = _head_cols(h)
                s = s_ref[h]
                s_out_ref[ci, h] = s
                sb = _bf(s)
                vn = u_ref[r, hc] - _dot(_bf(w_ref[r, hc]), sb)
                vnb = _bf(vn)
                o_ref[r, hc] = _dot(_bf(qd_ref[r, hc]), sb) + _dot(_bf(a_ref[h, r, :]), vnb)
                s_ref[h] = s * tl_ref[h, ci, 0:1, :] + _dot_tn(_bf(kt_ref[r, hc]), vnb)
            return carry

        lax.fori_loop(0, cb, chunk, 0, unroll=GDN_SCAN_UNROLL)

    return pl.pallas_call(
        body, name="gdn_scan_fwd",
        out_shape=[jax.ShapeDtypeStruct((t_rows, HW), F32),
                   jax.ShapeDtypeStruct((nc, HEADS, HEAD_DIM, HEAD_DIM), F32)],
        grid=(nc // cb,),
        in_specs=[wide, wide, pl.BlockSpec((HEADS, rows, CHUNK), lambda b: (0, b, 0)), wide, wide,
                  pl.BlockSpec((HEADS, cb, SUBLANES, LANES), lambda b: (0, b, 0, 0))],
        out_specs=[wide, pl.BlockSpec((cb, HEADS, HEAD_DIM, HEAD_DIM), lambda b: (b, 0, 0, 0))],
        scratch_shapes=[pltpu.VMEM((HEADS, HEAD_DIM, HEAD_DIM), F32)],
        compiler_params=_params(("arbitrary",)),
    )(u, w, aqk, qd, kt, tl)


def _gdn_scan_bwd(do, u, w, aqk, qd, kt, tl, states):
    t_rows = u.shape[0]
    nc = t_rows // CHUNK
    cb = min(GDN_CB, nc)
    rows = cb * CHUNK
    nb = nc // cb
    wide = pl.BlockSpec((rows, HW), lambda b: (nb - 1 - b, 0))
    a_spec = pl.BlockSpec((HEADS, rows, CHUNK), lambda b: (0, nb - 1 - b, 0))
    tl_spec = pl.BlockSpec((HEADS, cb, SUBLANES, LANES), lambda b: (0, nb - 1 - b, 0, 0))

    def body(do_ref, u_ref, w_ref, a_ref, qd_ref, kt_ref, tl_ref, s_in_ref,
             du_ref, dw_ref, da_ref, dqd_ref, dkt_ref, dtl_ref, ds_ref):
        @pl.when(pl.program_id(0) == 0)
        def _():
            ds_ref[...] = jnp.zeros_like(ds_ref)

        row0 = lax.broadcasted_iota(jnp.int32, (SUBLANES, LANES), 0) == 0

        def chunk(step, carry):
            ci = cb - 1 - step
            r = pl.ds(pl.multiple_of(ci * CHUNK, CHUNK), CHUNK)
            for h in range(HEADS):
                hc = _head_cols(h)
                s = s_in_ref[ci, h]
                ds_next = ds_ref[h]
                sb, dsb = _bf(s), _bf(ds_next)
                wb, ab, ktb, qdb = _bf(w_ref[r, hc]), _bf(a_ref[h, r, :]), _bf(kt_ref[r, hc]), _bf(qd_ref[r, hc])
                dob = _bf(do_ref[r, hc])
                vn = u_ref[r, hc] - _dot(wb, sb)
                vnb = _bf(vn)
                dvn = _dot_tn(ab, dob) + _dot(ktb, dsb)
                dvnb = _bf(dvn)
                du_ref[r, hc] = dvn
                dw_ref[r, hc] = -_dot_nt(dvnb, sb)
                da_ref[h, r, :] = _dot_nt(dob, vnb)
                dqd_ref[r, hc] = _dot_nt(dob, sb)
                dkt_ref[r, hc] = _dot_nt(vnb, dsb)
                dtl_ref[h, ci] = jnp.where(row0, _colsum(s * ds_next), 0.0)
                ds_ref[h] = _dot_tn(qdb, dob) + ds_next * tl_ref[h, ci, 0:1, :] - _dot_tn(wb, dvnb)
            return carry

        lax.fori_loop(0, cb, chunk, 0, unroll=GDN_SCAN_UNROLL)

    full = jax.ShapeDtypeStruct((t_rows, HW), F32)
    return pl.pallas_call(
        body, name="gdn_scan_bwd",
        out_shape=[full, full, jax.ShapeDtypeStruct((HEADS, t_rows, CHUNK), F32), full, full,
                   jax.ShapeDtypeStruct((HEADS, nc, SUBLANES, LANES), F32)],
        grid=(nb,),
        in_specs=[wide, wide, wide, a_spec, wide, wide, tl_spec,
                  pl.BlockSpec((cb, HEADS, HEAD_DIM, HEAD_DIM), lambda b: (nb - 1 - b, 0, 0, 0))],
        out_specs=[wide, wide, a_spec, wide, wide, tl_spec],
        scratch_shapes=[pltpu.VMEM((HEADS, HEAD_DIM, HEAD_DIM), F32)],
        compiler_params=_params(("arbitrary",)),
    )(do, u, w, aqk, qd, kt, tl, states)


def _att_profile_index():
    j = lax.broadcasted_iota(jnp.int32, (SUBLANES, ATT_KW), 1)
    return jnp.clip(ATT_PAD - j, -(CHUNK - 1), MAX_REL) + (CHUNK - 1)


def _att_far_back():
    qi = lax.broadcasted_iota(jnp.int32, (ATT_QB, ATT_KW), 0)
    kj = lax.broadcasted_iota(jnp.int32, (ATT_QB, ATT_KW), 1)
    return kj < qi


def _rotate_rows(x, forward):
    rows, lanes = x.shape
    row = lax.broadcasted_iota(jnp.int32, x.shape, 0)
    for bit in range(rows.bit_length() - 1):
        amount = (1 << bit) if forward else lanes - (1 << bit)
        x = jnp.where(jnp.bitwise_and(jnp.right_shift(row, bit), 1) == 1, pltpu.roll(x, amount, 1), x)
    return x


def _att_in_band():
    qi = lax.broadcasted_iota(jnp.int32, (ATT_QB, ATT_KW), 0)
    kj = lax.broadcasted_iota(jnp.int32, (ATT_QB, ATT_KW), 1)
    shift = CHUNK.bit_length() - 1
    qc = jnp.right_shift(qi, shift)
    kc = jnp.right_shift(kj, shift) - LEFT_CHUNKS
    return (kc <= qc) & (kc >= qc - LEFT_CHUNKS)


def _att_valid(b):
    kj = lax.broadcasted_iota(jnp.int32, (1, ATT_KW), 1)
    return jnp.where(kj + b * ATT_QB >= ATT_PAD, 0.0, NEG_INF)


def _rms_parts(x, w):
    r = lax.rsqrt(jnp.mean(x * x, axis=-1, keepdims=True) + EPS)
    xn = x * r
    return xn * w, xn, r


def _rms_bwd(dy, xn, r, w):
    dxn = dy * w
    dx = r * (dxn - xn * jnp.mean(dxn * xn, axis=-1, keepdims=True))
    return dx, _colsum(dy * xn)


def _att_probs(qb, kb, bias, before_start):
    s = _dot_nt(qb, kb) * (HEAD_DIM ** -0.5) + bias + before_start
    e = jnp.exp(s - jnp.max(s, axis=-1, keepdims=True))
    return e * (1.0 / jnp.sum(e, axis=-1, keepdims=True))


def _att_specs():
    q_spec = pl.BlockSpec((ATT_QB, HEAD_DIM), lambda h, b: (b, h))
    back = ATT_PAD // ATT_QB
    k_specs = [pl.BlockSpec((ATT_QB, HEAD_DIM), lambda h, b, j=j: (jnp.maximum(b + j - back, 0), HEADS + h))
               for j in range(3)]
    v_specs = [pl.BlockSpec((ATT_QB, HEAD_DIM), lambda h, b, j=j: (jnp.maximum(b + j - back, 0), 2 * HEADS + h))
               for j in range(3)]
    w_spec = pl.BlockSpec((1, HEAD_DIM), lambda h, b: (0, 0))
    smem = pl.BlockSpec(memory_space=pltpu.SMEM)
    return q_spec, k_specs, v_specs, w_spec, smem


BIAS_SPEC = pl.BlockSpec((1, ATT_QB, ATT_KW), lambda h, b: (h, 0, 0))


def _expand_rel_bias(rel):
    def body(rel_ref, bias_ref):
        h = pl.program_id(0)
        idx = _att_profile_index()

        def fill(r, acc):
            return jnp.where(idx == r, rel_ref[h, r], acc)

        profile = lax.fori_loop(0, N_REL, fill, jnp.zeros((SUBLANES, ATT_KW), F32))
        table = _rotate_rows(jnp.concatenate([profile] * (ATT_QB // SUBLANES), axis=0), True)
        table = jnp.where(_att_far_back(), rel_ref[h, N_REL - 1], table)
        bias_ref[0] = jnp.where(_att_in_band(), table, NEG_INF)

    return pl.pallas_call(
        body, name="rel_bias_expand",
        out_shape=jax.ShapeDtypeStruct((HEADS, ATT_QB, ATT_KW), F32), grid=(HEADS,),
        in_specs=[pl.BlockSpec(memory_space=pltpu.SMEM)],
        out_specs=pl.BlockSpec((1, ATT_QB, ATT_KW), lambda h: (h, 0, 0)),
        compiler_params=_params(("parallel",)),
    )(rel)


def _attention(pb, qw, kw, bias):
    t_rows = pb.shape[0]
    q_spec, k_specs, v_specs, w_spec, _ = _att_specs()

    def body(q_ref, k0, k1, k2, v0, v1, v2, qw_ref, kw_ref, bias_ref, o_ref):
        b = pl.program_id(1)
        kwin = jnp.concatenate([k0[...], k1[...], k2[...]], axis=0)
        vwin = jnp.concatenate([v0[...], v1[...], v2[...]], axis=0)
        q = _rms(q_ref[...], qw_ref[...])
        k = _rms(kwin, kw_ref[...])
        p = _att_probs(_bf(q), _bf(k), bias_ref[0], _att_valid(b))
        o_ref[...] = _dot(_bf(p), _bf(vwin)).astype(o_ref.dtype)

    return pl.pallas_call(
        body, name="band_attention_fwd",
        out_shape=jax.ShapeDtypeStruct((t_rows, HW), BF16),
        grid=(HEADS, t_rows // ATT_QB),
        in_specs=[q_spec] + k_specs + v_specs + [w_spec, w_spec, BIAS_SPEC],
        out_specs=pl.BlockSpec((ATT_QB, HEAD_DIM), lambda h, b: (b, h)),
        compiler_params=_params(("parallel", "arbitrary")),
    )(pb, pb, pb, pb, pb, pb, pb, qw, kw, bias)


def _attention_bwd(pb, qw, kw, bias, dyb):
    t_rows = pb.shape[0]
    nb = t_rows // ATT_QB
    q_spec, k_specs, v_specs, w_spec, smem = _att_specs()
    pad_rows = t_rows + ATT_PAD
    acc_spec = pl.BlockSpec((pad_rows, HEAD_DIM), lambda h, b: (0, h))

    def body(q_ref, k0, k1, k2, v0, v1, v2, qw_ref, kw_ref, bias_ref, do_ref,
             dq_ref, dk_ref, dv_ref, dqw_ref, dkw_ref, drel_ref, dbias_ref):
        h, b = pl.program_id(0), pl.program_id(1)

        @pl.when(b == 0)
        def _():
            dbias_ref[...] = jnp.zeros_like(dbias_ref)
            dk_ref[...] = jnp.zeros_like(dk_ref)
            dv_ref[...] = jnp.zeros_like(dv_ref)

        @pl.when((b == 0) & (h == 0))
        def _():
            dqw_ref[...] = jnp.zeros_like(dqw_ref)
            dkw_ref[...] = jnp.zeros_like(dkw_ref)

        kwin = jnp.concatenate([k0[...], k1[...], k2[...]], axis=0)
        vwin = jnp.concatenate([v0[...], v1[...], v2[...]], axis=0)
        scale = HEAD_DIM ** -0.5
        qw_, kw_ = qw_ref[...], kw_ref[...]
        q, qn, rq = _rms_parts(q_ref[...], qw_)
        k, kn, rk = _rms_parts(kwin, kw_)
        qb, kb, dob = _bf(q), _bf(k), _bf(do_ref[...])
        p = _att_probs(qb, kb, bias_ref[0], _att_valid(b))
        dp = _dot_nt(dob, _bf(vwin))
        ds = p * (dp - jnp.sum(p * dp, axis=-1, keepdims=True))
        dbias_ref[...] += ds
        ds = _bf(ds)
        dq, dqw = _rms_bwd(_dot(ds, kb) * scale, qn, rq, qw_)
        dk, dkw = _rms_bwd(_dot_tn(ds, qb) * scale, kn, rk, kw_)
        dq_ref[...] = dq.astype(dq_ref.dtype)
        win = pl.ds(pl.multiple_of(b * ATT_QB, ATT_QB), ATT_KW)
        dk_ref[win, :] += dk
        dv_ref[win, :] += _dot_tn(_bf(p), dob)
        dqw_ref[...] += dqw
        dkw_ref[...] += dkw

        @pl.when(b == nb - 1)
        def _():
            tot, far = dbias_ref[...], _att_far_back()
            far_sum = jnp.sum(jnp.where(far, tot, 0.0))
            per_offset = _colsum(_rotate_rows(jnp.where(far, 0.0, tot), False))
            idx = _att_profile_index()
            first_row = lax.broadcasted_iota(jnp.int32, idx.shape, 0) == 0
            spread = jnp.where(first_row, per_offset, 0.0)

            def reduce(r, carry):
                drel_ref[h, r] = jnp.sum(jnp.where(idx == r, spread, 0.0)) + jnp.where(r == N_REL - 1, far_sum, 0.0)
                return carry

            lax.fori_loop(0, N_REL, reduce, 0)

    return pl.pallas_call(
        body, name="band_attention_bwd",
        out_shape=[jax.ShapeDtypeStruct((t_rows, HW), BF16),
                   jax.ShapeDtypeStruct((pad_rows, HW), F32), jax.ShapeDtypeStruct((pad_rows, HW), F32),
                   jax.ShapeDtypeStruct((1, HEAD_DIM), F32), jax.ShapeDtypeStruct((1, HEAD_DIM), F32),
                   jax.ShapeDtypeStruct((HEADS, N_REL), F32)],
        grid=(HEADS, nb),
        in_specs=[q_spec] + k_specs + v_specs + [w_spec, w_spec, BIAS_SPEC, q_spec],
        out_specs=[q_spec, acc_spec, acc_spec, w_spec, w_spec, smem],
        scratch_shapes=[pltpu.VMEM((ATT_QB, ATT_KW), F32)],
        compiler_params=_params(("arbitrary", "arbitrary")),
    )(pb, pb, pb, pb, pb, pb, pb, qw, kw, bias, dyb)


def _me():
    return lax.axis_index("x"), lax.axis_index("y"), lax.axis_index("c")


def _index(x, y, c):
    return 4 * x + 2 * y + c


HBM_SPEC = pl.BlockSpec(memory_space=pl.ANY)


def _block(ref, kind, d, r, c):
    if kind == "all":
        return ref
    if kind == "rows":
        return ref.at[pl.ds(d * r, r), :]
    if kind == "win":
        return ref.at[:, pl.ds(d * WIN_STEP, c)]
    return ref.at[:, pl.ds(d * c, c)]


def _all_gather(shards, kinds, n_gather):
    n = len(shards)

    def body(*refs):
        x_refs, out_refs = refs[:n], refs[n:2 * n]
        send_sems, recv_sems, local_sems = refs[2 * n:]
        x, y, c = _me()
        me, sibling = (x, y, c), (x, y, 1 - c)
        chips = [(1 - x, y), (x, 1 - y), (1 - x, 1 - y)]

        def copy(i, k, blk, to, src=None):
            r_, c_ = shards[i].shape
            dst = _block(out_refs[i], kinds[i], _index(*blk), r_, c_)
            return pltpu.make_async_remote_copy(
                src_ref=dst if src is None else src, dst_ref=dst,
                send_sem=send_sems.at[i, k], recv_sem=recv_sems.at[i, k], device_id=to, device_id_type=MESH)

        sends, local = [], []
        for i in range(n):
            r_, c_ = shards[i].shape
            mine = pltpu.make_async_copy(x_refs[i], _block(out_refs[i], kinds[i], _index(*me), r_, c_),
                                         local_sems.at[i])
            mine.start()
            local.append(mine)
            if i >= n_gather:
                continue
            first = [copy(i, 0, me, sibling, src=x_refs[i])]
            first += [copy(i, 1 + j, me, (*chip, c), src=x_refs[i]) for j, chip in enumerate(chips)]
            for cp in first:
                cp.start()
            sends += first
        for i in range(n_gather):
            for j, chip in enumerate(chips):
                copy(i, 1 + j, (*chip, c), me).wait_recv()
                passed = copy(i, 4 + j, (*chip, c), sibling)
                passed.start()
                sends.append(passed)
        for i in range(n_gather):
            copy(i, 0, sibling, me).wait_recv()
            for j, chip in enumerate(chips):
                copy(i, 4 + j, (*chip, 1 - c), me).wait_recv()
        for cp in sends:
            cp.wait_send()
        for cp in local:
            cp.wait()

    def full_shape(s, kind):
        r_, c_ = s.shape
        return (N_DEV * r_, c_) if kind == "rows" else (r_, N_DEV * c_)

    return pl.pallas_call(
        body, name="weights_all_gather",
        out_shape=[jax.ShapeDtypeStruct(full_shape(s, k), s.dtype) for s, k in zip(shards, kinds)],
        in_specs=[HBM_SPEC] * n, out_specs=[HBM_SPEC] * n,
        scratch_shapes=[pltpu.SemaphoreType.DMA((n_gather, 7)), pltpu.SemaphoreType.DMA((n_gather, 7)),
                        pltpu.SemaphoreType.DMA((n,))],
        compiler_params=pltpu.CompilerParams(has_side_effects=True),
    )(*shards)


SEM_SPEC = pl.BlockSpec(memory_space=pltpu.SEMAPHORE)
HBM_ONLY = pl.BlockSpec(memory_space=pltpu.HBM)
DATAFLOW = pltpu.SideEffectType.DATAFLOW_SIDE_EFFECTING


def _peers():
    x, y, c = _me()
    return [(x ^ (k >> 2), y ^ ((k >> 1) & 1), c ^ (k & 1)) for k in range(1, N_DEV)]


def _gather_copies(shapes, kinds):
    def make(src_refs, land_refs, send_sems, recv_sems):
        mine = _index(*_me())
        return [pltpu.make_async_remote_copy(
            src_ref=src_refs[i], dst_ref=_block(land_refs[i], kind, mine, r, c),
            send_sem=send_sems.at[7 * i + k], recv_sem=recv_sems.at[7 * i + k], device_id=peer, device_id_type=MESH)
            for i, ((r, c), kind) in enumerate(zip(shapes, kinds)) for k, peer in enumerate(_peers())]

    return make


def _exchange_copies(shapes, kinds):
    def make(src_refs, land_refs, send_sems, recv_sems):
        mine = _index(*_me())
        return [pltpu.make_async_remote_copy(
            src_ref=_block(src_refs[i], kind, _index(*peer), r, c), dst_ref=land_refs[i].at[mine],
            send_sem=send_sems.at[7 * i + k], recv_sem=recv_sems.at[7 * i + k], device_id=peer, device_id_type=MESH)
            for i, ((r, c), kind) in enumerate(zip(shapes, kinds)) for k, peer in enumerate(_peers())]

    return make


def _place_block(shard, kind, name):
    r, c = shard.shape
    tile = _row_tile(r, c)
    nt = r // tile
    full = (N_DEV * r, c) if kind == "rows" else (r, N_DEV * c)

    def body(me_ref, x_ref, out_ref):
        out_ref[...] = x_ref[...]

    if kind == "rows":
        out_spec = pl.BlockSpec((tile, c), lambda i, me: (me[0] * nt + i, 0))
    else:
        out_spec = pl.BlockSpec((tile, c), lambda i, me: (i, me[0]))
    return pl.pallas_call(
        body, name=name, out_shape=jax.ShapeDtypeStruct(full, shard.dtype),
        grid_spec=pltpu.PrefetchScalarGridSpec(
            num_scalar_prefetch=1, grid=(nt,),
            in_specs=[pl.BlockSpec((tile, c), lambda i, me: (i, 0))], out_specs=out_spec),
        compiler_params=_params(("arbitrary",)),
    )(_my_index_operand(), shard)


def _split_start(srcs, lands, make, name):
    n = len(srcs)

    def body(*refs):
        send_sems, recv_sems = refs[2 * n], refs[2 * n + 1]
        for cp in make(refs[:n], refs[n:2 * n], send_sems, recv_sems):
            cp.start()
        refs[-1][...] = jnp.zeros_like(refs[-1])

    arrays = list(srcs) + list(lands)
    out = pl.pallas_call(
        body, name=name,
        out_shape=(pltpu.SemaphoreType.DMA((7 * n,)), pltpu.SemaphoreType.DMA((7 * n,)),
                   *[pltpu.HBM(a.shape, a.dtype) for a in arrays], jax.ShapeDtypeStruct((SUBLANES, LANES), F32)),
        in_specs=[HBM_ONLY] * (2 * n),
        out_specs=(SEM_SPEC, SEM_SPEC, *[HBM_ONLY] * (2 * n), pl.BlockSpec(memory_space=pltpu.VMEM)),
        input_output_aliases={i: 2 + i for i in range(2 * n)},
        compiler_params=pltpu.CompilerParams(has_side_effects=DATAFLOW),
    )(*[pltpu.with_memory_space_constraint(a, pltpu.HBM) for a in arrays])
    return out[0], out[1], list(out[2:2 + n]), list(out[2 + n:2 + 2 * n]), out[-1]


def _split_wait(send_sems, recv_sems, srcs, lands, after, make, name):
    n = len(srcs)

    def body(*refs):
        for cp in make(refs[:n], refs[n:2 * n], refs[2 * n], refs[2 * n + 1]):
            cp.wait_send()
            cp.wait_recv()

    arrays = list(srcs) + list(lands)
    out = pl.pallas_call(
        body, name=name,
        out_shape=tuple(pltpu.HBM(a.shape, a.dtype) for a in arrays),
        in_specs=[HBM_ONLY] * (2 * n) + [SEM_SPEC, SEM_SPEC, pl.BlockSpec(memory_space=pl.ANY)],
        out_specs=tuple([HBM_ONLY] * (2 * n)),
        input_output_aliases={i: i for i in range(2 * n)},
        compiler_params=pltpu.CompilerParams(has_side_effects=DATAFLOW),
    )(*arrays, send_sems, recv_sems, after)
    return list(out[:n]), list(out[n:])


def _all_reduce_small(vals, name):
    rows, width = vals.shape

    def body(x_ref, out_ref, buf_ref, send_sems, recv_sems):
        x, y, c = _me()
        mine = _index(x, y, c)
        buf_ref[mine] = x_ref[...]
        copies = []
        for k in range(1, N_DEV):
            px, py, pc = x ^ (k >> 2), y ^ ((k >> 1) & 1), c ^ (k & 1)
            copies.append(pltpu.make_async_remote_copy(
                src_ref=x_ref, dst_ref=buf_ref.at[mine],
                send_sem=send_sems.at[k - 1], recv_sem=recv_sems.at[k - 1],
                device_id=(px, py, pc), device_id_type=MESH))
        for cp in copies:
            cp.start()
        for cp in copies:
            cp.wait()
        acc = buf_ref[0]
        for j in range(1, N_DEV):
            acc = acc + buf_ref[j]
        out_ref[...] = acc

    vmem = pl.BlockSpec(memory_space=pltpu.VMEM)
    return pl.pallas_call(
        body, name=name,
        out_shape=jax.ShapeDtypeStruct(vals.shape, F32),
        in_specs=[vmem], out_specs=vmem,
        scratch_shapes=[pltpu.VMEM((N_DEV, rows, width), F32),
                        pltpu.SemaphoreType.DMA((7,)), pltpu.SemaphoreType.DMA((7,))],
        compiler_params=pltpu.CompilerParams(has_side_effects=True),
    )(vals)


def _adamw_math(w, g, m, v):
    m = ADAM_B1 * m + (1.0 - ADAM_B1) * g
    v = ADAM_B2 * v + (1.0 - ADAM_B2) * (g * g)
    m_hat = m / (1.0 - ADAM_B1 ** ADAM_STEP)
    v_hat = v / (1.0 - ADAM_B2 ** ADAM_STEP)
    delta = -ADAM_LR * (m_hat / (jnp.sqrt(v_hat) + ADAM_EPS) + ADAM_WD * w)
    return delta, m, v


ROW_TILE_ELEMS = 384 * 1024


def _row_tile(rows, width):
    best = SUBLANES
    for t in range(SUBLANES, rows + 1, SUBLANES):
        if rows % t == 0 and t * width <= ROW_TILE_ELEMS:
            best = t
    return best


def _sum_received(r_ref, own, me):
    g = None
    for j in range(N_DEV):
        term = jnp.where(me == j, own, r_ref[j].astype(F32))
        g = term if g is None else g + term
    return g


def _my_index_operand():
    return _index(*_me()).astype(jnp.int32).reshape(1)


def _sum_small(recv, own):
    def body(me_ref, r_ref, own_ref, out_ref):
        out_ref[...] = _sum_received(r_ref, own_ref[...], me_ref[0])

    whole = lambda shape: pl.BlockSpec(shape, lambda i, me, nd=len(shape): (0,) * nd)
    return pl.pallas_call(
        body, name="small_grads_sum", out_shape=jax.ShapeDtypeStruct(own.shape, F32),
        grid_spec=pltpu.PrefetchScalarGridSpec(
            num_scalar_prefetch=1, grid=(1,), in_specs=[whole(recv.shape), whole(own.shape)],
            out_specs=whole(own.shape)),
        compiler_params=_params(("arbitrary",)),
    )(_my_index_operand(), recv, own)


def _adamw_recv(recv, grad, kind, w, m, v, name):
    _, rows, width = recv.shape
    tile = _row_tile(rows, width)
    nt = rows // tile

    def body(me_ref, r_ref, own_ref, w_ref, m_ref, v_ref, g_out, d_out, m_out, v_out):
        g = _sum_received(r_ref, own_ref[...].astype(F32), me_ref[0])
        d, mn, vn = _adamw_math(w_ref[...], g, m_ref[...], v_ref[...])
        g_out[...] = g
        d_out[...] = d
        m_out[...] = mn
        v_out[...] = vn

    if kind == "rows":
        own_spec = pl.BlockSpec((tile, width), lambda i, me: (me[0] * nt + i, 0))
    else:
        own_spec = pl.BlockSpec((tile, width), lambda i, me: (i, me[0]))
    spec = pl.BlockSpec((tile, width), lambda i, me: (i, 0))
    shape = jax.ShapeDtypeStruct((rows, width), F32)
    return pl.pallas_call(
        body, name=name, out_shape=[shape] * 4,
        grid_spec=pltpu.PrefetchScalarGridSpec(
            num_scalar_prefetch=1, grid=(nt,),
            in_specs=[pl.BlockSpec((N_DEV, tile, width), lambda i, me: (0, i, 0)), own_spec, spec, spec, spec],
            out_specs=[spec] * 4),
        compiler_params=_params(("parallel",)),
    )(_my_index_operand(), recv, grad, w, m, v)


WIN_STEP = 1408
WIN_W = 1536
IN_SHARD = IN_COLS // N_DEV
IN_PADDED = WIN_STEP * (N_DEV - 1) + WIN_W


def _roll_w_in(shard_padded):
    rows = shard_padded.shape[0]
    tile = _row_tile(rows, WIN_W)

    def body(x_ref, main_ref, edge_ref):
        win = pltpu.roll(x_ref[...], 2 * _index(*_me()), 1).astype(BF16)
        main_ref[...] = win[:, :WIN_STEP]
        edge_ref[...] = win[:, WIN_STEP:]

    return pl.pallas_call(
        body, name="w_in_window",
        out_shape=[jax.ShapeDtypeStruct((rows, WIN_STEP), BF16), jax.ShapeDtypeStruct((rows, WIN_W - WIN_STEP), BF16)],
        grid=(rows // tile,),
        in_specs=[pl.BlockSpec((tile, WIN_W), lambda i: (i, 0))],
        out_specs=[pl.BlockSpec((tile, WIN_STEP), lambda i: (i, 0)),
                   pl.BlockSpec((tile, WIN_W - WIN_STEP), lambda i: (i, 0))],
        compiler_params=_params(("parallel",)),
    )(shard_padded)


def _sum_w_in_windows(recv, grad):
    _, rows, width = recv.shape
    tile = _row_tile(rows, width)

    def body(me_ref, r_ref, g_ref, g_out, own_ref, sem):
        me = me_ref[0]
        rows_i = pl.ds(pl.multiple_of(pl.program_id(0) * tile, tile), tile)
        own = pltpu.make_async_copy(g_ref.at[rows_i, pl.ds(pl.multiple_of(me * WIN_STEP, LANES), width)], own_ref, sem)
        own.start()
        own.wait()
        g_out[...] = pltpu.roll(_sum_received(r_ref, own_ref[...].astype(F32), me), width - 2 * me, 1)

    return pl.pallas_call(
        body, name="w_in_grad_sum", out_shape=jax.ShapeDtypeStruct((rows, width), F32),
        grid_spec=pltpu.PrefetchScalarGridSpec(
            num_scalar_prefetch=1, grid=(rows // tile,),
            in_specs=[pl.BlockSpec((N_DEV, tile, width), lambda i, me: (0, i, 0)), HBM_SPEC],
            out_specs=pl.BlockSpec((tile, width), lambda i, me: (i, 0)),
            scratch_shapes=[pltpu.VMEM((tile, width), BF16), pltpu.SemaphoreType.DMA]),
        compiler_params=_params(("arbitrary",)),
    )(_my_index_operand(), recv, grad)


def _adamw_small(w, g, m, v, name):
    def fn(i, n, w_, g_, m_, v_):
        return _adamw_math(w_, g_, m_, v_)

    r, c = w.shape
    return _rows(fn, [(w, "t"), (g, "t"), (m, "t"), (v, "t")], [], [(c, F32)] * 3, [], _row_tile(r, c), name)


def _norm_fwd(x, w, name):
    return _rows(lambda i, n, x_, w_: (_rms(x_, w_[...]),), [(x, "t")], [w], [(D_MODEL, BF16)], [], 512, name)[0]


def _residual_norm_fwd(x, y, scale, w, name):
    def fn(i, n, x_, y_, w_):
        xn = x_ + scale * y_
        return xn, _rms(xn, w_[...])

    return _rows(fn, [(x, "t"), (y, "t")], [w], [(D_MODEL, F32), (D_MODEL, BF16)], [], 512, name)


def _residual_norm_bwd(x, w, dhs, dres, scale, name):
    nh = len(dhs)

    def fn(i, n, x_, dres_, *rest):
        dh = rest[0]
        for extra in rest[1:nh]:
            dh = dh + extra
        _, vjp = jax.vjp(_rms, x_, rest[nh][...])
        dx, dw = vjp(dh)
        dx = dx + dres_
        return dx, scale * dx, dw

    return _rows(fn, [(x, "t"), (dres, "t")] + [(d, "t") for d in dhs], [w],
                 [(D_MODEL, F32), (D_MODEL, BF16)], [(1, D_MODEL)], 256, name)


FFN_UP_TN = 512


def _ffn_up(h, w_gu, name, after=None):
    t, d = h.shape
    f = w_gu.shape[1] // 2
    tm = _pick(t, (1024, 512, 256, 128))
    nj = f // FFN_UP_TN
    after_specs, after_args = _after(after)

    def body(h_ref, wg_ref, wu_ref, *rest):
        g_ref, u_ref, act_ref = rest[len(after_args):]
        hb = h_ref[...]
        g = jnp.dot(hb, wg_ref[...], preferred_element_type=F32)
        u = jnp.dot(hb, wu_ref[...], preferred_element_type=F32)
        g_ref[...] = g.astype(BF16)
        u_ref[...] = u.astype(BF16)
        act_ref[...] = (_silu(g) * u).astype(BF16)

    out = pl.BlockSpec((tm, FFN_UP_TN), lambda i, j: (i, j))
    return pl.pallas_call(
        body, name=name, out_shape=[jax.ShapeDtypeStruct((t, f), BF16)] * 3, grid=(t // tm, nj),
        in_specs=[pl.BlockSpec((tm, d), lambda i, j: (i, 0)),
                  pl.BlockSpec((d, FFN_UP_TN), lambda i, j: (0, j)),
                  pl.BlockSpec((d, FFN_UP_TN), lambda i, j: (0, j + nj))] + after_specs,
        out_specs=[out, out, out],
        compiler_params=_params(("parallel", "parallel")),
    )(h, w_gu, w_gu, *after_args)


def _project_residual_norm(a, b, x, scale, w, name):
    m, k = a.shape
    n = b.shape[1]
    tm = _pick(m, (512, 256, 128))
    tk = k if k <= MATMUL_WHOLE_K else _pick(k, (1408, 1024, 512, 256, 128))
    nk = k // tk

    def body(a_ref, b_ref, x_ref, w_ref, xo_ref, h_ref, *acc):
        prod = jnp.dot(a_ref[...], b_ref[...], preferred_element_type=F32)

        def finish(y):
            xn = x_ref[...] + scale * y
            xo_ref[...] = xn
            h_ref[...] = _rms(xn, w_ref[...]).astype(BF16)

        if nk == 1:
            finish(prod)
            return
        acc_ref, kk = acc[0], pl.program_id(1)

        @pl.when(kk == 0)
        def _():
            acc_ref[...] = prod

        @pl.when((kk > 0) & (kk < nk - 1))
        def _():
            acc_ref[...] += prod

        @pl.when(kk == nk - 1)
        def _():
            finish(acc_ref[...] + prod)

    row = pl.BlockSpec((tm, n), lambda i, kk: (i, 0))
    return pl.pallas_call(
        body, name=name,
        out_shape=[jax.ShapeDtypeStruct((m, n), F32), jax.ShapeDtypeStruct((m, n), BF16)],
        grid=(m // tm, nk),
        in_specs=[pl.BlockSpec((tm, tk), lambda i, kk: (i, kk)), pl.BlockSpec((tk, n), lambda i, kk: (kk, 0)),
                  row, pl.BlockSpec((1, n), lambda i, kk: (0, 0))],
        out_specs=[row, row],
        scratch_shapes=[pltpu.VMEM((tm, n), F32)] if nk > 1 else [],
        compiler_params=_params(("parallel", "arbitrary")),
    )(a, b, x, w)


def _mix_project_residual_norm(gg, ta, tb, b, x, w, name):
    m, n = x.shape
    tm = _pick(m, (256, 128))

    def body(gg_ref, ta_ref, tb_ref, b_ref, x_ref, w_ref, mixed_ref, xo_ref, h_ref):
        mixed = _mix(gg_ref[...].astype(F32), ta_ref[...].astype(F32), tb_ref[...].astype(F32)).astype(BF16)
        mixed_ref[...] = mixed
        xn = x_ref[...] + jnp.dot(mixed, b_ref[...], preferred_element_type=F32)
        xo_ref[...] = xn
        h_ref[...] = _rms(xn, w_ref[...]).astype(BF16)

    row = pl.BlockSpec((tm, n), lambda i: (i, 0))
    return pl.pallas_call(
        body, name=name,
        out_shape=[jax.ShapeDtypeStruct((m, n), BF16), jax.ShapeDtypeStruct((m, n), F32),
                   jax.ShapeDtypeStruct((m, n), BF16)],
        grid=(m // tm,),
        in_specs=[pl.BlockSpec((tm, 2 * n), lambda i: (i, 0)), row, row, pl.BlockSpec(b.shape, lambda i: (0, 0)),
                  row, pl.BlockSpec((1, n), lambda i: (0, 0))],
        out_specs=[row, row, row],
        compiler_params=_params(("parallel",)),
    )(gg, ta, tb, b, x, w)


def _ffn_fwd(h, w_gu, get_w_down, tag, x, w_next, after=None):
    g, u, act = _ffn_up(h, w_gu, tag + "_gu", after)
    x_new, h_next = _project_residual_norm(act, get_w_down(act), x, 0.5, w_next, tag + "_down")
    return (g, u), act, x_new, h_next


def _ffn_dact(dy, w_down, g, u, name):
    t, d = dy.shape
    f = w_down.shape[0]
    tm = _pick(t, (1024, 512, 256, 128))

    def body(dy_ref, w_ref, g_ref, u_ref, out_ref):
        dact = lax.dot_general(dy_ref[...], w_ref[...], NT, preferred_element_type=F32)
        g_, u_ = g_ref[...].astype(F32), u_ref[...].astype(F32)
        sg = _sigmoid(g_)
        out_ref[0] = (dact * u_ * (sg * (1.0 + g_ * (1.0 - sg)))).astype(BF16)
        out_ref[1] = (dact * (g_ * sg)).astype(BF16)

    tile = pl.BlockSpec((tm, FFN_UP_TN), lambda i, j: (i, j))
    return pl.pallas_call(
        body, name=name, out_shape=jax.ShapeDtypeStruct((2, t, f), BF16), grid=(t // tm, f // FFN_UP_TN),
        in_specs=[pl.BlockSpec((tm, d), lambda i, j: (i, 0)), pl.BlockSpec((FFN_UP_TN, d), lambda i, j: (j, 0)),
                  tile, tile],
        out_specs=pl.BlockSpec((2, tm, FFN_UP_TN), lambda i, j: (0, i, j)),
        compiler_params=_params(("parallel", "parallel")),
    )(dy, w_down, g, u)


def _ffn_bwd(h, gu, act, dy, w_gu, w_down, tag, comm, more=None):
    dgu = _ffn_dact(dy, w_down, gu[0], gu[1], tag + "_dact")
    sent = comm.send(tag + "_gu", {tag + "_w_gu": _matmul(h, dgu, "tn", BF16, tag + "_d_w_gu")})
    sent = sent + comm.send(tag + "_down", {tag + "_w_down": _matmul(act, dy, "tn", BF16, tag + "_d_w_down", sent),
                                            **(more or {})})
    dh = _matmul(dgu, w_gu, "nt", BF16, tag + "_dh")
    return dh, sent


def _expanders():
    e_g = np.zeros((LANES, HW), np.float32)
    e_b = np.zeros((LANES, HW), np.float32)
    for h in range(HEADS):
        e_g[h, h * HEAD_DIM:(h + 1) * HEAD_DIM] = 1.0
        e_b[HEADS + h, h * HEAD_DIM:(h + 1) * HEAD_DIM] = 1.0
    return jnp.asarray(e_g), jnp.asarray(e_b)


def _pad_lanes(v):
    return jnp.pad(v, ((0, 0), (0, LANES - v.shape[1])))


class _LocalWeights:
    def __init__(self, big):
        self.big, self.sent = big, {}

    def arrive(self, group, after):
        return self.big

    def send(self, group, grads):
        self.sent.update(grads)
        return jnp.zeros((), F32)


def _local_step(x, p, tgt, small, comm):
    e_g, e_b = _expanders()
    alog, dtb = _pad_lanes(small["a_log"]), _pad_lanes(small["dt_bias"])
    conv_w = jnp.pad(small["conv_w"], ((0, SUBLANES - CONV_K), (0, 0)))
    rel = _expand_rel_bias(small["rel_bias"])

    h1 = _norm_fwd(x, small["ffn1_norm"], "ffn1_norm")
    big = dict(comm.arrive("ffn1", h1))
    started = big.pop("_token", None)

    def ffn1_w_down(act):
        big.update(comm.arrive("ffn1_down", act))
        return big["ffn1_w_down"]

    gu1, act1, x1, h2 = _ffn_fwd(h1, big["ffn1_w_gu"], ffn1_w_down, "ffn1", x, small["mix_norm"], started)

    big = {**big, **comm.arrive("mixer", h2)}
    w_in = big["w_in"]
    w_qz = w_in[:, :IN_QZ]
    w_ab = jnp.pad(w_in[:, IN_AB0:IN_QKVB0], ((0, 0), (0, LANES - 2 * HEADS)))
    w_qkvb = w_in[:, IN_QKVB0:IN_GG0]
    w_gg = w_in[:, IN_GG0:IN_COLS]
    qz = _matmul(h2, w_qz, "nn", F32, "in_qz")
    ab = _matmul(h2, w_ab, "nn", F32, "in_ab")
    pb = _matmul(h2, w_qkvb, "nn", F32, "in_qkvb")
    gg = _matmul(h2, w_gg, "nn", BF16, "in_gates")
    pa, z = qz[:, :3 * HW], qz[:, 3 * HW:]

    def prep(i, n, pa_, prev_, ab_, cw_, alog_, dtb_, eg_, eb_):
        q, k, v = _gdn_post(_conv(pa_, prev_, cw_, i))
        g_b, beta_b = _gdn_gates(ab_, alog_[...], dtb_[...], eg_[...], eb_[...])
        return q, k, v, g_b, beta_b

    qn, kn, vv, g_b, beta_b = _rows(prep, [(pa, "t"), (pa, "p"), (ab, "t")], [conv_w, alog, dtb, e_g, e_b],
                                    [(HW, F32)] * 5, [], 256, "gdn_prep")
    u, w, aqk, qd, kt, tl = _gdn_intra(qn, kn, vv, g_b, beta_b)
    o, states = _gdn_scan(u, w, aqk, qd, kt, tl)
    ya = _rows(lambda i, n, o_, z_, w_: (_gated_norm(o_, z_, w_[...]),), [(o, "t"), (z, "t")], [small["gdn_norm"]],
               [(HW, BF16)], [], 512, "gdn_gated_norm")[0]

    yb = _attention(pb, small["q_norm"], small["k_norm"], rel)

    big = {**big, **comm.arrive("branches", yb)}
    ta = _matmul(ya, big["w_branch_a"], "nn", BF16, "branch_a")
    tb = _matmul(yb, big["w_branch_b"], "nn", BF16, "branch_b")
    mixed, x2, h3 = _mix_project_residual_norm(gg, ta, tb, big["w_out"], x1, small["ffn2_norm"], "w_out")
    big = {**big, **comm.arrive("tail", h3)}
    gu2, act2, x3, h4 = _ffn_fwd(h3, big["ffn2_w_gu"], lambda act: big["ffn2_w_down"], "ffn2", x2,
                                 small["ple_norm"])
    gp = _matmul(h4, big["ple_gate"], "nn", BF16, "ple_gate")
    pp = _matmul(p, big["ple_proj"], "nn", BF16, "ple_proj")

    def head(i, n, x3_, gp_, pp_, tgt_):
        sg = _sigmoid(gp_)
        err = x3_ + sg * pp_ - tgt_
        dx4 = err * (1.0 / D_MODEL)
        sq = _colsum(err * err)
        part = sq[:, :LANES]
        for j in range(1, D_MODEL // LANES):
            part = part + sq[:, j * LANES:(j + 1) * LANES]
        return dx4, dx4 * pp_ * sg * (1.0 - sg), dx4 * sg, (0.5 / D_MODEL) * part

    dx4, dgp, dpp, loss_lanes = _rows(head, [(x3, "t"), (gp, "t"), (pp, "t"), (tgt, "t")], [],
                                      [(D_MODEL, F32), (D_MODEL, BF16), (D_MODEL, BF16)], [(1, LANES)], 256,
                                      "ple_loss_head")
    loss = jnp.sum(loss_lanes)

    gbig, gsmall = {}, {}
    gbig["ple_proj"] = _matmul(p, dpp, "tn", BF16, "d_ple_proj")
    gbig["ple_gate"] = _matmul(h4, dgp, "tn", BF16, "d_ple_gate")
    dh4 = _matmul(dgp, big["ple_gate"], "nt", BF16, "ple_gate_dh")
    dx3, dy2, gsmall["ple_norm"] = _residual_norm_bwd(x3, small["ple_norm"], [dh4], dx4, 0.5, "ple_norm_bwd")

    dh3, sent = _ffn_bwd(h3, gu2, act2, dy2, big["ffn2_w_gu"], big["ffn2_w_down"], "ffn2", comm,
                         {n: gbig[n] for n in ("ple_proj", "ple_gate")})
    dx2, dx2b, gsmall["ffn2_norm"] = _residual_norm_bwd(x2, small["ffn2_norm"] + sent, [dh3], dx3, 1.0,
                                                        "ffn2_norm_bwd")

    gbig["w_out"] = _matmul(mixed, dx2b, "tn", BF16, "d_w_out")
    dmixed = _matmul(dx2b, big["w_out"], "nt", BF16, "w_out_dx")

    def mix_bwd(i, n, gg_, ta_, tb_, dm_):
        _, vjp = jax.vjp(_mix, gg_, ta_, tb_)
        return vjp(dm_)

    dgg, dta, dtb_ = _rows(mix_bwd, [(gg, "t"), (ta, "t"), (tb, "t"), (dmixed, "t")], [],
                           [(2 * D_MODEL, BF16), (D_MODEL, BF16), (D_MODEL, BF16)], [], 256, "mix_bwd")
    gbig["w_branch_a"] = _matmul(ya, dta, "tn", BF16, "d_branch_a")
    gbig["w_branch_b"] = _matmul(yb, dtb_, "tn", BF16, "d_branch_b")
    dya = _matmul(dta, big["w_branch_a"], "nt", BF16, "branch_a_dx")
    dyb = _matmul(dtb_, big["w_branch_b"], "nt", BF16, "branch_b_dx")

    dq_b, dk_b, dv_b, gsmall["q_norm"], gsmall["k_norm"], gsmall["rel_bias"] = _attention_bwd(
        pb, small["q_norm"], small["k_norm"], rel, dyb)
    dpb = jnp.concatenate([dq_b, dk_b[ATT_PAD:].astype(BF16), dv_b[ATT_PAD:].astype(BF16)], axis=1)

    def gated_bwd(i, n, o_, z_, dya_, w_):
        _, vjp = jax.vjp(_gated_norm, o_, z_, w_[...])
        return vjp(dya_)

    do, dz, gsmall["gdn_norm"] = _rows(gated_bwd, [(o, "t"), (z, "t"), (dya, "t")], [small["gdn_norm"]],
                                       [(HW, F32), (HW, BF16)], [(1, HEAD_DIM)], 256, "gdn_gated_norm_bwd")
    du, dw, da, dqd, dkt, dtl = _gdn_scan_bwd(do, u, w, aqk, qd, kt, tl, states)
    dqn, dkn, dvv, dg_b, dbeta_b = _gdn_intra_bwd(qn, kn, vv, g_b, beta_b, du, dw, da, dqd, dkt, dtl)

    def prep_bwd(i, n, pa_, prev_, ab_, dq_, dk_, dv_, dg_, db_, cw_, alog_, dtb_, eg_, eb_):
        _, vjp = jax.vjp(_gdn_post, _conv(pa_, prev_, cw_, i))
        (dy,) = vjp((dq_, dk_, dv_))
        e_g_, e_b_ = eg_[...], eb_[...]
        _, vjp_g = jax.vjp(lambda a, b, c: _gdn_gates(a, b, c, e_g_, e_b_), ab_, alog_[...], dtb_[...])
        dab, dalog, ddtb = vjp_g((dg_, db_))
        return dy, dab, dalog, ddtb

    dy_conv, dab, dalog, ddtb = _rows(
        prep_bwd, [(pa, "t"), (pa, "p"), (ab, "t"), (dqn, "t"), (dkn, "t"), (dvv, "t"), (dg_b, "t"), (dbeta_b, "t")],
        [conv_w, alog, dtb, e_g, e_b], [(3 * HW, F32), (LANES, BF16)], [(1, LANES), (1, LANES)], 256,
        "gdn_prep_bwd")
    gsmall["a_log"] = dalog[:, :HEADS]
    gsmall["dt_bias"] = ddtb[:, :HEADS]

    def conv_bwd(i, n, dy_, nxt_, pa_, prev_, cw_):
        dpa = dy_ * cw_[CONV_K - 1:CONV_K, :]
        row = lax.broadcasted_iota(jnp.int32, (SUBLANES, dy_.shape[1]), 0)
        dcw = jnp.where(row == CONV_K - 1, _colsum(dy_ * pa_), 0.0)
        for j in range(CONV_K - 1):
            s = CONV_K - 1 - j
            dpa = dpa + _shift_up(dy_, nxt_, s, i, n) * cw_[j:j + 1, :]
            dcw = dcw + jnp.where(row == j, _colsum(dy_ * _shift_down(pa_, prev_, s, i)), 0.0)
        return dpa, dcw

    dpa, dcw = _rows(conv_bwd, [(dy_conv, "t"), (dy_conv, "n"), (pa, "t"), (pa, "p")], [conv_w],
                     [(3 * HW, BF16)], [(SUBLANES, 3 * HW)], 256, "gdn_conv_bwd")
    gsmall["conv_w"] = dcw[:CONV_K]

    dqz = jnp.concatenate([dpa, dz], axis=1)
    d_w_qz = _matmul(h2, dqz, "tn", BF16, "d_in_qz")
    d_w_ab = _matmul(h2, dab, "tn", BF16, "d_in_ab")
    d_w_qkvb = _matmul(h2, dpb, "tn", BF16, "d_in_qkvb")
    d_w_gg = _matmul(h2, dgg, "tn", BF16, "d_in_gates")
    gbig["w_in"] = jnp.concatenate([d_w_qz, d_w_ab[:, :2 * HEADS], d_w_qkvb, d_w_gg,
                                    jnp.zeros((D_MODEL, IN_PADDED - IN_COLS), BF16)], axis=1)
    dh2 = [_matmul(dqz, w_qz, "nt", BF16, "in_qz_dh"), _matmul(dab, w_ab, "nt", BF16, "in_ab_dh"),
           _matmul(dpb, w_qkvb, "nt", BF16, "in_qkvb_dh"), _matmul(dgg, w_gg, "nt", BF16, "in_gates_dh")]
    sent = comm.send("mixer", {n: gbig[n] for n in ("w_out", "w_branch_b", "w_branch_a", "w_in")})
    dx1, dy1, gsmall["mix_norm"] = _residual_norm_bwd(x1, small["mix_norm"] + sent, dh2, dx2, 0.5, "mix_norm_bwd")

    dh1, sent = _ffn_bwd(h1, gu1, act1, dy1, big["ffn1_w_gu"], big["ffn1_w_down"], "ffn1", comm)
    grad_x, _, gsmall["ffn1_norm"] = _residual_norm_bwd(x, small["ffn1_norm"] + sent, [dh1], dx1, 1.0,
                                                        "ffn1_norm_bwd")
    return loss, grad_x, gsmall


GATHER_GROUPS = {"ffn1": ("ffn1_w_gu",),
                 "ffn1_down": ("ffn1_w_down",),
                 "mixer": ("w_in_main", "w_in_edge"),
                 "branches": ("w_branch_a", "w_branch_b", "w_out"),
                 "tail": ("ffn2_w_gu", "ffn2_w_down", "ple_gate", "ple_proj")}
SPLIT_GATHERS = ("ffn1_down", "mixer", "branches", "tail")


def _kind(name):
    return "cols" if name in COL_SHARDED or name.startswith("w_in_") else "rows"


def _merge_w_in(main, edges):
    edge_w = WIN_W - WIN_STEP
    w_in = jnp.pad(main, ((0, 0), (0, edge_w)))
    for d in range(N_DEV):
        at = WIN_STEP * (d + 1)
        w_in = w_in + jnp.pad(edges[:, d * edge_w:(d + 1) * edge_w], ((0, 0), (at, IN_PADDED - at - edge_w)))
    return w_in


class _Fsdp:
    def __init__(self, wts, first):
        self.wts, self.first_token = wts, first
        main, edge = _roll_w_in(jnp.pad(wts["w_in"], ((0, 0), (0, WIN_W - IN_SHARD))))
        self.shards = {n: wts[n].astype(BF16) for n in BIG if n not in ("w_in", "ffn1_w_gu")}
        self.shards.update(w_in_main=main, w_in_edge=edge)
        self.lands = {n: _place_block(self.shards[n], _kind(n), "own_" + n)
                      for group in SPLIT_GATHERS for n in GATHER_GROUPS[group]}
        self.flight, self.sent = {}, {}

    def _gather_first(self, after):
        token = self.first_token + after[0, 0].astype(F32) * 0.0
        me = _index(*_me())
        for n, land in self.lands.items():
            r, c = self.shards[n].shape
            at = (me * r, 0) if _kind(n) == "rows" else (0, me * c)
            token = token + lax.dynamic_slice(land, at, (1, 1))[0, 0].astype(F32) * 0.0
        shard = (self.wts["ffn1_w_gu"] + token).astype(BF16)
        self.shards["ffn1_w_gu"] = shard
        first = _all_gather([shard], [_kind("ffn1_w_gu")], 1)[0]
        token = first[0, 0].astype(F32) * 0.0
        for group in SPLIT_GATHERS:
            names = GATHER_GROUPS[group]
            srcs = [self.shards[n] for n in names]
            lands = [self.lands[n] for n in names]
            make = _gather_copies([s.shape for s in srcs], [_kind(n) for n in names])
            srcs[0] = srcs[0] + token.astype(BF16)
            send_sems, recv_sems, srcs, lands, tok = _split_start(srcs, lands, make, "gather_start_" + group)
            token = token + tok[0, 0]
            self.flight[group] = (send_sems, recv_sems, srcs, lands, make)
        return {"ffn1_w_gu": first, "_token": token}

    def arrive(self, group, after):
        if group == "ffn1":
            return self._gather_first(after)
        send_sems, recv_sems, srcs, lands, make = self.flight[group]
        _, lands = _split_wait(send_sems, recv_sems, srcs, lands, after, make, "gather_wait_" + group)
        full = dict(zip(GATHER_GROUPS[group], lands))
        if group == "mixer":
            full["w_in"] = _merge_w_in(full.pop("w_in_main"), full.pop("w_in_edge"))
        return full

    def send(self, group, grads):
        names = list(grads)
        kinds = ["all" if n == "small" else "win" if n == "w_in" else _kind(n) for n in names]
        shapes = [grads[n].shape if n == "small" else (D_MODEL, WIN_W) if n == "w_in" else self.shards[n].shape
                  for n in names]
        srcs = [grads[n] for n in names]
        lands = [lax.empty((N_DEV,) + tuple(s), g.dtype) for s, g in zip(shapes, srcs)]
        make = _exchange_copies(shapes, kinds)
        send_sems, recv_sems, srcs, lands, tok = _split_start(srcs, lands, make, "grads_start_" + group)
        self.sent[group] = (names, kinds, send_sems, recv_sems, srcs, lands, make)
        return tok[0, 0]

    def received(self, group, after):
        names, kinds, send_sems, recv_sems, srcs, lands, make = self.sent[group]
        srcs, lands = _split_wait(send_sems, recv_sems, srcs, lands, after, make, "grads_wait_" + group)
        return {n: (k, g, r) for n, k, g, r in zip(names, kinds, srcs, lands)}


SMALL_ROWS = ("ffn1_norm", "mix_norm", "ffn2_norm", "ple_norm", "gdn_norm", "q_norm", "k_norm", "a_log", "dt_bias",
              "rel_bias", "conv_w")


def _pack_small(vals):
    rows = []
    for n in SMALL_ROWS:
        v = vals[n]
        if n == "rel_bias":
            v = jnp.pad(v, ((0, 0), (0, 2 * LANES - N_REL)))
        elif n in ("a_log", "dt_bias"):
            v = _pad_lanes(v)
        rows.append(v.reshape(-1, LANES))
    packed = jnp.concatenate(rows, axis=0)
    return jnp.pad(packed, ((0, -packed.shape[0] % SUBLANES), (0, 0)))


def _unpack_small(packed, shapes):
    out, off = {}, 0
    for n in SMALL_ROWS:
        shp = shapes[n]
        if n == "rel_bias":
            out[n] = packed[off:off + 2 * HEADS].reshape(HEADS, 2 * LANES)[:, :N_REL]
            off += 2 * HEADS
        elif n in ("a_log", "dt_bias"):
            out[n] = packed[off:off + 1, :HEADS]
            off += 1
        else:
            r = int(np.prod(shp)) // LANES
            out[n] = packed[off:off + r].reshape(shp)
            off += r
    return out


WEIGHTS = ("ffn1_norm", "ffn1_w_gu", "ffn1_w_down", "mix_norm", "w_in", "conv_w", "a_log", "dt_bias", "gdn_norm",
           "q_norm", "k_norm", "rel_bias", "w_branch_a", "w_branch_b", "w_out", "ffn2_norm", "ffn2_w_gu",
           "ffn2_w_down", "ple_norm", "ple_gate", "ple_proj")


def kernel(x, p, ffn1_norm, ffn1_w_gu, ffn1_w_down, mix_norm, w_in, conv_w, a_log, dt_bias, gdn_norm, q_norm, k_norm, rel_bias, w_branch_a, w_branch_b, w_out, ffn2_norm, ffn2_w_gu, ffn2_w_down, ple_norm, ple_gate, ple_proj, loss_target, m_ffn1_norm, m_ffn1_w_gu, m_ffn1_w_down, m_mix_norm, m_w_in, m_conv_w, m_a_log, m_dt_bias, m_gdn_norm, m_q_norm, m_k_norm, m_rel_bias, m_w_branch_a, m_w_branch_b, m_w_out, m_ffn2_norm, m_ffn2_w_gu, m_ffn2_w_down, m_ple_norm, m_ple_gate, m_ple_proj, v_ffn1_norm, v_ffn1_w_gu, v_ffn1_w_down, v_mix_norm, v_w_in, v_conv_w, v_a_log, v_dt_bias, v_gdn_norm, v_q_norm, v_k_norm, v_rel_bias, v_w_branch_a, v_w_branch_b, v_w_out, v_ffn2_norm, v_ffn2_w_gu, v_ffn2_w_down, v_ple_norm, v_ple_gate, v_ple_proj):
    args = dict(locals())
    def layer0(v):
        return v[0] if v.ndim == 3 else v

    wts = {n: layer0(args[n]) for n in WEIGHTS}
    mom = {n: layer0(args["m_" + n]) for n in WEIGHTS}
    var = {n: layer0(args["v_" + n]) for n in WEIGHTS}
    x2d, p2d, tgt = x[0], p[0, 0], loss_target[0]
    my_index = _index(*_me())

    small = {n: wts[n] for n in SMALL_ROWS if n != "conv_w"}
    conv_shard = wts["conv_w"]
    conv_cols = conv_shard.shape[1]
    conv_packed = jnp.zeros((SUBLANES, N_DEV * conv_cols), F32)
    conv_packed = lax.dynamic_update_slice(conv_packed, jnp.pad(conv_shard, ((0, SUBLANES - CONV_K), (0, 0))),
                                           (0, my_index * conv_cols))
    small["conv_w"] = _all_reduce_small(conv_packed.reshape(-1, LANES), "conv_w_gather").reshape(SUBLANES, -1)[:CONV_K]

    fsdp = _Fsdp(wts, small["conv_w"][0, 0] * 0.0)

    loss, grad_x, gsmall = _local_step(x2d, p2d, tgt, small, fsdp)
    loss = lax.psum(loss, ("x", "y", "c"))

    fsdp.send("small", {"small": _pack_small(gsmall)})

    outs_big, after = {}, grad_x
    for group in list(fsdp.sent):
        for n, (kind, grad, recv) in fsdp.received(group, after).items():
            if n == "small":
                small_sum = _sum_small(recv, grad)
            elif n == "w_in":
                g_in = _sum_w_in_windows(recv, grad)[:, :IN_SHARD]
                outs_big[n] = [g_in] + list(_adamw_small(wts[n], g_in, mom[n], var[n], "adamw_w_in"))
            else:
                outs_big[n] = _adamw_recv(recv, grad, kind, wts[n], mom[n], var[n], "adamw_" + n)
            after = small_sum if n == "small" else outs_big[n][1]

    small_shapes = {n: (small[n].shape if n != "conv_w" else (CONV_K, N_DEV * conv_cols)) for n in SMALL_ROWS}
    gsum = _unpack_small(small_sum, small_shapes)
    gsum["conv_w"] = lax.dynamic_slice(gsum["conv_w"], (0, my_index * conv_cols), (CONV_K, conv_cols))
    rep = [n for n in SMALL_ROWS if n != "conv_w"]
    rep_shapes = {n: small_shapes[n] for n in rep}

    def pack_rep(vals):
        return _pack_small({**{n: vals[n] for n in rep}, "conv_w": jnp.zeros((CONV_K, LANES), F32)})

    def unpack_rep(packed):
        return _unpack_small(packed, {**rep_shapes, "conv_w": (CONV_K, LANES)})

    outs_small = [unpack_rep(o) for o in _adamw_small(pack_rep(wts), pack_rep(gsum), pack_rep(mom), pack_rep(var),
                                                      "adamw_replicated")]
    pad8 = functools.partial(jnp.pad, pad_width=((0, SUBLANES - CONV_K), (0, 0)))
    outs_conv = [o[:CONV_K] for o in _adamw_small(pad8(conv_shard), pad8(gsum["conv_w"]), pad8(mom["conv_w"]),
                                                   pad8(var["conv_w"]), "adamw_conv")]

    def leaf(kind, n):
        if n in BIG:
            return outs_big[n][kind][None]
        if n == "conv_w":
            return (gsum["conv_w"] if kind == 0 else outs_conv[kind - 1])[None]
        return (gsum[n] if kind == 0 else outs_small[kind - 1][n]).reshape(args[n].shape)

    result = [loss, grad_x[None]]
    for kind in range(4):
        result += [leaf(kind, n) for n in WEIGHTS]
    return tuple(result)
```
